```python
import jax, jax.numpy as jnp
from jax import lax
import numpy as np

D_MODEL = 2048
BATCH = 32
SEQ = 256
DEPTH = 1
DEC_BATCH = 2
DEC_SEQ = 1024
PAST_LEN = 512

GRID_W = 64
HEAD_DIM = 128
NA_HEADS = 8
NA_WIDTH = NA_HEADS * HEAD_DIM
NA_WIN_ROWS = 8
NA_WIN_COLS = 16
NA_SPAN = 2 * NA_WIN_COLS
NB_Q_HEADS = 8
NB_KV_HEADS = 2
NB_GROUP = NB_Q_HEADS // NB_KV_HEADS
NB_WIDTH = NB_Q_HEADS * HEAD_DIM
NB_KV_WIDTH = NB_KV_HEADS * HEAD_DIM
WINDOW = 128
Q_BLOCK = 128
N_GROUPS = 4
EXPERTS_PER_GROUP = 8
N_EXPERTS = N_GROUPS * EXPERTS_PER_GROUP
TOP_K = 2
D_EXPERT = 1024
MOE_BLOCK = 128

ROPE_BASE = 10000.0
NORM_EPS = 1e-6
NEG_INF = -1e30
SPLIT_POINTS = (NA_WIDTH, 2 * NA_WIDTH, 3 * NA_WIDTH, 3 * NA_WIDTH + NB_WIDTH,
                3 * NA_WIDTH + NB_WIDTH + NB_KV_WIDTH, 3 * NA_WIDTH + NB_WIDTH + 2 * NB_KV_WIDTH,
                3 * NA_WIDTH + NB_WIDTH + 2 * NB_KV_WIDTH + D_MODEL)
IN_WIDTH = 3 * NA_WIDTH + NB_WIDTH + 2 * NB_KV_WIDTH + 2 * D_MODEL

kernel_name = "hybrid_natten_swa_hmoe_diffusion_step"


def rmsnorm(x, w):
    xf = x.astype(jnp.float32)
    y = xf * lax.rsqrt(jnp.mean(xf * xf, axis=-1, keepdims=True) + NORM_EPS)
    return (y * w.astype(jnp.float32)).astype(x.dtype)


def masked_softmax(s, sink=None):
    m = jnp.max(s, axis=-1, keepdims=True)
    if sink is not None:
        m = jnp.maximum(m, sink)
    e = jnp.exp(s - m)
    den = jnp.sum(e, axis=-1, keepdims=True)
    if sink is not None:
        den = den + jnp.exp(sink - m)
    return e / den


def grid_rope(length):
    t = jnp.arange(length)
    row = (t // GRID_W).astype(jnp.float32)
    col = (t % GRID_W).astype(jnp.float32)
    n_freq = HEAD_DIM // 4
    inv = ROPE_BASE ** (-jnp.arange(n_freq, dtype=jnp.float32) / n_freq)
    ar = row[:, None] * inv
    ac = col[:, None] * inv
    ang = jnp.concatenate([ar, ar, ac, ac], axis=-1)
    return jnp.cos(ang), jnp.sin(ang)


def apply_rope(x, cos, sin):
    qd = HEAD_DIM // 4
    a1, a2, b1, b2 = x[..., :qd], x[..., qd:2 * qd], x[..., 2 * qd:3 * qd], x[..., 3 * qd:]
    rot = jnp.concatenate([-a2, a1, -b2, b1], axis=-1)
    shape = (1, x.shape[1]) + (1,) * (x.ndim - 3) + (HEAD_DIM,)
    out = x.astype(jnp.float32) * cos.reshape(shape) + rot.astype(jnp.float32) * sin.reshape(shape)
    return out.astype(x.dtype)


def context_attention(q, k, v, sink=None):
    B, Lq, KV, G, D = q.shape
    nq = Lq // Q_BLOCK
    qb = jnp.moveaxis(q.reshape(B, nq, Q_BLOCK, KV, G, D), 1, 0)
    scale = D ** -0.5
    sk = None if sink is None else sink.astype(jnp.float32)[None, :, :, None, None]

    def one_block(qblk):
        s = jnp.einsum('bqkgd,bskd->bkgqs', qblk, k).astype(jnp.float32) * scale
        p = masked_softmax(s, sk)
        return jnp.einsum('bkgqs,bskd->bqkgd', p.astype(v.dtype), v)

    o = lax.map(one_block, qb)
    return jnp.moveaxis(o, 0, 1).reshape(B, Lq, KV, G, D)


def neighbourhood_attention(q, k, v, ck, cv, rpb):
    B, L, H, D = q.shape
    rows = L // GRID_W
    kr = min(NA_WIN_ROWS, rows)
    ncb = GRID_W // NA_WIN_COLS
    r = jnp.arange(rows)
    ridx = jnp.clip(r - kr // 2, 0, rows - kr)[:, None] + jnp.arange(kr)[None, :]
    cb = jnp.arange(ncb)
    cidx = (jnp.clip(cb * NA_WIN_COLS - NA_WIN_COLS // 2, 0, GRID_W - NA_SPAN)[:, None]
            + jnp.arange(NA_SPAN)[None, :])
    qcol = cb[:, None] * NA_WIN_COLS + jnp.arange(NA_WIN_COLS)[None, :]
    qstart = jnp.clip(qcol - NA_WIN_COLS // 2, 0, GRID_W - NA_WIN_COLS)
    kcol = cidx[:, None, :]
    valid = (kcol >= qstart[:, :, None]) & (kcol < qstart[:, :, None] + NA_WIN_COLS)
    dr = ridx - r[:, None] + (NA_WIN_ROWS - 1)
    dc = jnp.clip(kcol - qcol[:, :, None], -(NA_WIN_COLS - 1), NA_WIN_COLS - 1) + (NA_WIN_COLS - 1)
    bias = rpb[:, dr[:, None, None, :, None], dc[None, :, :, None, :]]
    qg = q.reshape(B, rows, ncb, NA_WIN_COLS, H, D)
    kg = k.reshape(B, rows, GRID_W, H, D)[:, ridx[:, None, :, None], cidx[None, :, None, :]]
    vg = v.reshape(B, rows, GRID_W, H, D)[:, ridx[:, None, :, None], cidx[None, :, None, :]]
    scale = D ** -0.5
    sw = jnp.einsum('brnqhd,brnijhd->bhrnqij', qg, kg).astype(jnp.float32) * scale
    sw = sw + bias.astype(jnp.float32)[None]
    sw = jnp.where(valid[None, None, None, :, :, None, :], sw, NEG_INF)
    nw = kr * NA_SPAN
    sw = sw.reshape(B, H, rows, ncb, NA_WIN_COLS, nw)
    sc = jnp.einsum('brnqhd,bshd->bhrnqs', qg, ck).astype(jnp.float32) * scale
    p = masked_softmax(jnp.concatenate([sw, sc], axis=-1))
    pw = p[..., :nw].reshape(B, H, rows, ncb, NA_WIN_COLS, kr, NA_SPAN).astype(v.dtype)
    pc = p[..., nw:].astype(v.dtype)
    o = (jnp.einsum('bhrnqij,brnijhd->brnqhd', pw, vg)
         + jnp.einsum('bhrnqs,bshd->brnqhd', pc, cv))
    return o.reshape(B, L, H, D)


def window_attention(q, k, v, ck, cv, sink):
    B, L, KV, G, D = q.shape
    nb = L // WINDOW
    pad = ((0, 0), (WINDOW, WINDOW), (0, 0), (0, 0))
    kp = jnp.pad(k, pad)
    vp = jnp.pad(v, pad)
    kb = jnp.concatenate([kp[:, i * WINDOW:i * WINDOW + L].reshape(B, nb, WINDOW, KV, D) for i in range(3)], axis=2)
    vb = jnp.concatenate([vp[:, i * WINDOW:i * WINDOW + L].reshape(B, nb, WINDOW, KV, D) for i in range(3)], axis=2)
    qb = q.reshape(B, nb, WINDOW, KV, G, D)
    qpos = jnp.arange(nb)[:, None] * WINDOW + jnp.arange(WINDOW)[None, :]
    kpos = jnp.arange(nb)[:, None] * WINDOW - WINDOW + jnp.arange(3 * WINDOW)[None, :]
    diff = kpos[:, None, :] - qpos[:, :, None]
    valid = (jnp.abs(diff) <= WINDOW) & (kpos[:, None, :] >= 0) & (kpos[:, None, :] < L)
    scale = D ** -0.5
    sw = jnp.einsum('bnqkgd,bnjkd->bkgnqj', qb, kb).astype(jnp.float32) * scale
    sw = jnp.where(valid[None, None, None], sw, NEG_INF)
    sc = jnp.einsum('bnqkgd,bskd->bkgnqs', qb, ck).astype(jnp.float32) * scale
    p = masked_softmax(jnp.concatenate([sw, sc], axis=-1), sink.astype(jnp.float32)[None, :, :, None, None, None])
    nw = 3 * WINDOW
    o = (jnp.einsum('bkgnqj,bnjkd->bnqkgd', p[..., :nw].astype(v.dtype), vb)
         + jnp.einsum('bkgnqs,bskd->bnqkgd', p[..., nw:].astype(v.dtype), cv))
    return o.reshape(B, L, KV, G, D)


def mixer_projections(h, w_in_l, qn_a, kn_a, qn_b, kn_b):
    B, L, _ = h.shape
    proj = h @ w_in_l
    qa, ka, va, qb, kb, vb, ga, gb = jnp.split(proj, SPLIT_POINTS, axis=-1)
    qa = rmsnorm(qa.reshape(B, L, NA_HEADS, HEAD_DIM), qn_a)
    ka = rmsnorm(ka.reshape(B, L, NA_HEADS, HEAD_DIM), kn_a)
    va = va.reshape(B, L, NA_HEADS, HEAD_DIM)
    qb = rmsnorm(qb.reshape(B, L, NB_KV_HEADS, NB_GROUP, HEAD_DIM), qn_b)
    kb = rmsnorm(kb.reshape(B, L, NB_KV_HEADS, HEAD_DIM), kn_b)
    vb = vb.reshape(B, L, NB_KV_HEADS, HEAD_DIM)
    return qa, ka, va, qb, kb, vb, ga, gb


def merge_branches(oa, ob, ga, gb, w_pa, w_pb, w_out):
    B, L = oa.shape[:2]
    ya = oa.reshape(B, L, NA_WIDTH) @ w_pa
    yb = ob.reshape(B, L, NB_WIDTH) @ w_pb
    return (jax.nn.sigmoid(ga) * ya + jax.nn.sigmoid(gb) * yb) @ w_out


def hierarchical_moe(h, w_rg, b_rg, w_re, b_re, w1, w3, w2):
    B, L, D = h.shape
    x = h.reshape(-1, D)
    T = x.shape[0]
    pg = jax.nn.softmax((x @ w_rg).astype(jnp.float32) + b_rg.astype(jnp.float32), axis=-1)
    gsel = jnp.argmax(pg, axis=-1)
    pg_sel = jnp.take_along_axis(pg, gsel[:, None], axis=-1)
    le = ((x @ w_re).astype(jnp.float32) + b_re.astype(jnp.float32)).reshape(T, N_GROUPS, EXPERTS_PER_GROUP)
    le_sel = jnp.take_along_axis(le, gsel[:, None, None], axis=1)[:, 0]
    top_v, top_i = lax.top_k(le_sel, TOP_K)
    gate = pg_sel * jax.nn.softmax(top_v, axis=-1)
    eid = (gsel[:, None] * EXPERTS_PER_GROUP + top_i).reshape(-1).astype(jnp.int32)
    tok = jnp.repeat(jnp.arange(T, dtype=jnp.int32), TOP_K)
    wflat = gate.reshape(-1)
    P = T * TOP_K
    order = jnp.argsort(eid)
    e_s, tok_s, w_s = eid[order], tok[order], wflat[order]
    counts = jnp.zeros((N_EXPERTS,), jnp.int32).at[eid].add(1)
    start = jnp.cumsum(counts) - counts
    padded = (counts + MOE_BLOCK - 1) // MOE_BLOCK * MOE_BLOCK
    pad_end = jnp.cumsum(padded)
    pad_start = pad_end - padded
    dest = pad_start[e_s] + jnp.arange(P, dtype=jnp.int32) - start[e_s]
    n_blocks = -(-P // MOE_BLOCK) + N_EXPERTS
    tok_buf = jnp.full((n_blocks * MOE_BLOCK,), T, jnp.int32).at[dest].set(tok_s)
    w_buf = jnp.zeros((n_blocks * MOE_BLOCK,), jnp.float32).at[dest].set(w_s)
    blk_e = jnp.minimum(jnp.searchsorted(pad_end, jnp.arange(n_blocks, dtype=jnp.int32) * MOE_BLOCK, side='right'),
                        N_EXPERTS - 1)
    xp = jnp.concatenate([x, jnp.zeros((1, D), x.dtype)], axis=0)
    xb = xp[tok_buf].reshape(n_blocks, MOE_BLOCK, D)

    def expert_block(args):
        xblk, e = args
        return (jax.nn.silu(xblk @ w1[e]) * (xblk @ w3[e])) @ w2[e]

    yb = lax.map(expert_block, (xb, blk_e))
    y = jnp.zeros((T + 1, D), x.dtype).at[tok_buf].add(yb.reshape(-1, D) * w_buf[:, None].astype(x.dtype))
    return y[:T].reshape(B, L, D)


def setup_inputs(seed: int = 0) -> dict:
    key = jax.random.key(seed)
    ks = jax.random.split(key, 32)
    f32 = jnp.float32
    nrm = lambda k, shape, s: jax.random.normal(k, shape, f32) * s
    D = D_MODEL
    return {
        "x_prompt": nrm(ks[0], (BATCH, SEQ, D), 1.0),
        "x_sample": nrm(ks[1], (DEC_BATCH, DEC_SEQ, D), 1.0),
        "cache_a_k": nrm(ks[2], (DEC_BATCH, DEPTH, PAST_LEN, NA_HEADS, HEAD_DIM), 1.0),
        "cache_a_v": nrm(ks[3], (DEC_BATCH, DEPTH, PAST_LEN, NA_HEADS, HEAD_DIM), 1.0),
        "cache_b_k": nrm(ks[4], (DEC_BATCH, DEPTH, PAST_LEN, NB_KV_HEADS, HEAD_DIM), 1.0),
        "cache_b_v": nrm(ks[5], (DEC_BATCH, DEPTH, PAST_LEN, NB_KV_HEADS, HEAD_DIM), 1.0),
        "c": nrm(ks[6], (DEC_BATCH, D), 1.0),
        "c_ctx": nrm(ks[7], (D,), 1.0),
        "norm1_w": 1.0 + nrm(ks[8], (DEPTH, D), 0.02),
        "norm2_w": 1.0 + nrm(ks[9], (DEPTH, D), 0.02),
        "w_ada": nrm(ks[10], (DEPTH, D, 6 * D), 0.5 * D ** -0.5),
        "b_ada": nrm(ks[11], (DEPTH, 6 * D), 0.02),
        "w_in": nrm(ks[12], (DEPTH, D, IN_WIDTH), D ** -0.5),
        "qn_a": 1.0 + nrm(ks[13], (DEPTH, HEAD_DIM), 0.02),
        "kn_a": 1.0 + nrm(ks[14], (DEPTH, HEAD_DIM), 0.02),
        "qn_b": 1.0 + nrm(ks[15], (DEPTH, HEAD_DIM), 0.02),
        "kn_b": 1.0 + nrm(ks[16], (DEPTH, HEAD_DIM), 0.02),
        "rpb_a": nrm(ks[17], (DEPTH, NA_HEADS, 2 * NA_WIN_ROWS - 1, 2 * NA_WIN_COLS - 1), 0.1),
        "sink_b": nrm(ks[18], (DEPTH, NB_Q_HEADS), 0.5),
        "w_pa": nrm(ks[19], (DEPTH, NA_WIDTH, D), NA_WIDTH ** -0.5),
        "w_pb": nrm(ks[20], (DEPTH, NB_WIDTH, D), NB_WIDTH ** -0.5),
        "w_out": nrm(ks[21], (DEPTH, D, D), D ** -0.5),
        "w_rg": nrm(ks[22], (DEPTH, D, N_GROUPS), D ** -0.5),
        "b_rg": nrm(ks[23], (DEPTH, N_GROUPS), 0.01),
        "w_re": nrm(ks[24], (DEPTH, D, N_EXPERTS), D ** -0.5),
        "b_re": nrm(ks[25], (DEPTH, N_EXPERTS), 0.01),
        "w1": nrm(ks[26], (DEPTH, N_EXPERTS, D, D_EXPERT), D ** -0.5),
        "w3": nrm(ks[27], (DEPTH, N_EXPERTS, D, D_EXPERT), D ** -0.5),
        "w2": nrm(ks[28], (DEPTH, N_EXPERTS, D_EXPERT, D), D_EXPERT ** -0.5),
    }


def reference(x_prompt, x_sample, cache_a_k, cache_a_v, cache_b_k, cache_b_v, c, c_ctx,
              norm1_w, norm2_w, w_ada, b_ada, w_in, qn_a, kn_a, qn_b, kn_b, rpb_a, sink_b,
              w_pa, w_pb, w_out, w_rg, b_rg, w_re, b_re, w1, w3, w2):
    xp = x_prompt
    xs = x_sample
    cos, sin = grid_rope(x_sample.shape[1])
    new_a_k, new_a_v, new_b_k, new_b_v = [], [], [], []
    for l in range(DEPTH):
        sink = sink_b[l].reshape(NB_KV_HEADS, NB_GROUP)
        mod = (jax.nn.silu(c_ctx) @ w_ada[l] + b_ada[l])[None, None, :]
        sh1, sc1, g1, sh2, sc2, g2 = jnp.split(mod, 6, axis=-1)
        h = rmsnorm(xp, norm1_w[l]) * (1.0 + sc1) + sh1
        qa, ka, va, qb, kb, vb, ga, gb = mixer_projections(h, w_in[l], qn_a[l], kn_a[l], qn_b[l], kn_b[l])
        oa = context_attention(qa[:, :, :, None, :], ka, va)[:, :, :, 0, :]
        ob = context_attention(qb, kb, vb, sink)
        xp = xp + g1 * merge_branches(oa, ob, ga, gb, w_pa[l], w_pb[l], w_out[l])
        h = rmsnorm(xp, norm2_w[l]) * (1.0 + sc2) + sh2
        xp = xp + g2 * hierarchical_moe(h, w_rg[l], b_rg[l], w_re[l], b_re[l], w1[l], w3[l], w2[l])
        new_a_k.append(ka)
        new_a_v.append(va)
        new_b_k.append(kb)
        new_b_v.append(vb)
        mod = (jax.nn.silu(c) @ w_ada[l] + b_ada[l])[:, None, :]
        sh1, sc1, g1, sh2, sc2, g2 = jnp.split(mod, 6, axis=-1)
        h = rmsnorm(xs, norm1_w[l]) * (1.0 + sc1) + sh1
        qa, ka, va, qb, kb, vb, ga, gb = mixer_projections(h, w_in[l], qn_a[l], kn_a[l], qn_b[l], kn_b[l])
        qb = apply_rope(qb, cos, sin)
        kb = apply_rope(kb, cos, sin)
        oa = neighbourhood_attention(qa, ka, va, cache_a_k[:, l], cache_a_v[:, l], rpb_a[l])
        ob = window_attention(qb, kb, vb, cache_b_k[:, l], cache_b_v[:, l], sink)
        xs = xs + g1 * merge_branches(oa, ob, ga, gb, w_pa[l], w_pb[l], w_out[l])
        h = rmsnorm(xs, norm2_w[l]) * (1.0 + sc2) + sh2
        xs = xs + g2 * hierarchical_moe(h, w_rg[l], b_rg[l], w_re[l], b_re[l], w1[l], w3[l], w2[l])
    state_a_k = jnp.stack(new_a_k, axis=1)
    state_a_v = jnp.stack(new_a_v, axis=1)
    state_b_k = jnp.stack(new_b_k, axis=1)
    state_b_v = jnp.stack(new_b_v, axis=1)
    return (xp, xs, state_a_k, state_a_v, state_b_k, state_b_v)
```

```python
import functools

import jax
import jax.numpy as jnp
from jax import lax
from jax.experimental import pallas as pl
from jax.experimental.pallas import tpu as pltpu

D_MODEL = 2048
HEAD_DIM = 128
NA_HEADS = 8
NA_WIDTH = NA_HEADS * HEAD_DIM
NB_Q_HEADS = 8
NB_KV_HEADS = 2
NB_GROUP = NB_Q_HEADS // NB_KV_HEADS
NB_WIDTH = NB_Q_HEADS * HEAD_DIM
NB_KV_WIDTH = NB_KV_HEADS * HEAD_DIM
GRID_W = 64
NA_WIN_ROWS = 8
NA_WIN_COLS = 16
WINDOW = 128
N_GROUPS = 4
EXPERTS_PER_GROUP = 8
N_EXPERTS = N_GROUPS * EXPERTS_PER_GROUP
D_EXPERT = 1024
IN_WIDTH = 3 * NA_WIDTH + NB_WIDTH + 2 * NB_KV_WIDTH + 2 * D_MODEL
ROPE_BASE = 10000.0
NORM_EPS = 1e-6
NEG_INF = -1e30
ATTN_SCALE = HEAD_DIM ** -0.5

QA_HEAD0 = 0
KA_HEAD0 = NA_HEADS
VA_HEAD0 = 2 * NA_HEADS
QB_HEAD0 = 3 * NA_HEADS
KB_HEAD0 = QB_HEAD0 + NB_Q_HEADS
VB_HEAD0 = KB_HEAD0 + NB_KV_HEADS
GATE_COL0 = (VB_HEAD0 + NB_KV_HEADS) * HEAD_DIM

LOGIT_PAD = 128
MOE_ROW_BLOCK = 256
MOE_ROW_GROUP = 1024
MOE_F_CHUNK = 256
VMEM_LIMIT = 56 * 1024 * 1024

F32 = jnp.float32
BF16 = jnp.bfloat16


def _cparams(sem):
    return pltpu.CompilerParams(dimension_semantics=sem, vmem_limit_bytes=VMEM_LIMIT)


def _rms(x, w):
    return x * lax.rsqrt(jnp.mean(x * x, axis=-1, keepdims=True) + NORM_EPS) * w


def _dot(a, b):
    return jnp.dot(a, b, preferred_element_type=F32)


def _dot_nt(a, b):
    return lax.dot_general(a, b, (((1,), (1,)), ((), ())), preferred_element_type=F32)


def _ada_kernel(c_ref, w_ref, b_ref, o_ref):
    c = c_ref[...]
    s = c * jax.nn.sigmoid(c)
    o_ref[...] = jnp.dot(s, w_ref[...], precision=lax.Precision.HIGHEST,
                         preferred_element_type=F32) + b_ref[...]


def _ada_call(cvec, w_ada, b_ada):
    rows, d = cvec.shape
    n = w_ada.shape[1]
    tn = 1024
    return pl.pallas_call(
        _ada_kernel,
        grid=(n // tn,),
        in_specs=[pl.BlockSpec((rows, d), lambda j: (0, 0)),
                  pl.BlockSpec((d, tn), lambda j: (0, j)),
                  pl.BlockSpec((1, tn), lambda j: (0, j))],
        out_specs=pl.BlockSpec((rows, tn), lambda j: (0, j)),
        out_shape=jax.ShapeDtypeStruct((rows, n), F32),
        compiler_params=_cparams(("arbitrary",)),
        name="ada",
    )(cvec, w_ada, b_ada)


INPROJ_TM = 1024
INPROJ_TN = 512
NORM_ROWS = 128


def _inproj_kernel(x_ref, nw_ref, sc_ref, sh_ref, w_ref, o_ref, h_scr):
    @pl.when(pl.program_id(1) == 0)
    def _():
        nw = nw_ref[...]
        sc = 1.0 + sc_ref[0]
        sh = sh_ref[0]

        def body(r, carry):
            rows = pl.ds(pl.multiple_of(r * NORM_ROWS, NORM_ROWS), NORM_ROWS)
            h_scr[rows, :] = (_rms(x_ref[rows, :], nw) * sc + sh).astype(BF16)
            return carry

        lax.fori_loop(0, INPROJ_TM // NORM_ROWS, body, 0)

    o_ref[...] = _dot(h_scr[...], w_ref[...])


def _inproj_call(x, nw, sc, sh, w_bf, tiles_per_group):
    m, d = x.shape
    n = w_bf.shape[1]
    grp = lambda i, j: (i // tiles_per_group, 0, 0)
    return pl.pallas_call(
        _inproj_kernel,
        grid=(m // INPROJ_TM, n // INPROJ_TN),
        in_specs=[pl.BlockSpec((INPROJ_TM, d), lambda i, j: (i, 0)),
                  pl.BlockSpec((1, d), lambda i, j: (0, 0)),
                  pl.BlockSpec((1, 1, d), grp),
                  pl.BlockSpec((1, 1, d), grp),
                  pl.BlockSpec((d, INPROJ_TN), lambda i, j: (0, j))],
        out_specs=pl.BlockSpec((INPROJ_TM, INPROJ_TN), lambda i, j: (i, j)),
        out_shape=jax.ShapeDtypeStruct((m, n), F32),
        scratch_shapes=[pltpu.VMEM((INPROJ_TM, d), BF16)],
        compiler_params=_cparams(("arbitrary", "arbitrary")),
        name="inproj",
    )(x, nw, sc, sh, w_bf)


def _softmax_pv(scores, values, sink=None):
    m = None
    for s in scores:
        ms = jnp.max(s, axis=-1, keepdims=True)
        m = ms if m is None else jnp.maximum(m, ms)
    if sink is not None:
        m = jnp.maximum(m, sink)
    den = None
    acc = None
    for s, v in zip(scores, values):
        e = jnp.exp(s - m)
        ds = jnp.sum(e, axis=-1, keepdims=True)
        den = ds if den is None else den + ds
        pv = _dot(e.astype(BF16), v)
        acc = pv if acc is None else acc + pv
    if sink is not None:
        den = den + jnp.exp(sink - m)
    return acc / den


def _ctx_attn_kernel(qa_ref, ka_ref, va_ref, qb_ref, kb_ref, vb_ref,
                     qna_ref, kna_ref, qnb_ref, knb_ref, sink_ref,
                     oa_ref, ob_ref, nak_ref, nav_ref, nbk_ref, nbv_ref):
    qna, kna, qnb, knb = qna_ref[...], kna_ref[...], qnb_ref[...], knb_ref[...]
    for h in range(NA_HEADS):
        cols = slice(h * HEAD_DIM, (h + 1) * HEAD_DIM)
        q = (_rms(qa_ref[:, cols], qna) * ATTN_SCALE).astype(BF16)
        k = _rms(ka_ref[:, cols], kna)
        v = va_ref[:, cols]
        nak_ref[:, cols] = k
        nav_ref[:, cols] = v
        s = _dot_nt(q, k.astype(BF16))
        oa_ref[:, cols] = _softmax_pv([s], [v.astype(BF16)]).astype(oa_ref.dtype)
    for kv in range(NB_KV_HEADS):
        kcols = slice(kv * HEAD_DIM, (kv + 1) * HEAD_DIM)
        k = _rms(kb_ref[:, kcols], knb)
        v = vb_ref[:, kcols]
        nbk_ref[:, kcols] = k
        nbv_ref[:, kcols] = v
        kb16 = k.astype(BF16)
        vb16 = v.astype(BF16)
        for g in range(NB_GROUP):
            hq = kv * NB_GROUP + g
            cols = slice(hq * HEAD_DIM, (hq + 1) * HEAD_DIM)
            q = (_rms(qb_ref[:, cols], qnb) * ATTN_SCALE).astype(BF16)
            s = _dot_nt(q, kb16)
            ob_ref[:, cols] = _softmax_pv([s], [vb16], sink=sink_ref[hq]).astype(ob_ref.dtype)


def _ctx_attn_call(proj, seq, qna, kna, qnb, knb, sink):
    m = proj.shape[0]
    nb = m // seq
    wide = lambda blk: pl.BlockSpec((seq, NA_WIDTH), lambda b: (b, blk))
    narrow = lambda blk: pl.BlockSpec((seq, NB_KV_WIDTH), lambda b: (b, blk))
    vec = pl.BlockSpec((1, HEAD_DIM), lambda b: (0, 0))
    return pl.pallas_call(
        _ctx_attn_kernel,
        grid=(nb,),
        in_specs=[wide(QA_HEAD0 // NA_HEADS), wide(KA_HEAD0 // NA_HEADS), wide(VA_HEAD0 // NA_HEADS),
                  wide(QB_HEAD0 // NA_HEADS), narrow(KB_HEAD0 // NB_KV_HEADS), narrow(VB_HEAD0 // NB_KV_HEADS),
                  vec, vec, vec, vec,
                  pl.BlockSpec(memory_space=pltpu.SMEM)],
        out_specs=[pl.BlockSpec((seq, NA_WIDTH), lambda b: (b, 0)),
                   pl.BlockSpec((seq, NB_WIDTH), lambda b: (b, 0)),
                   pl.BlockSpec((seq, NA_WIDTH), lambda b: (b, 0)),
                   pl.BlockSpec((seq, NA_WIDTH), lambda b: (b, 0)),
                   pl.BlockSpec((seq, NB_KV_WIDTH), lambda b: (b, 0)),
                   pl.BlockSpec((seq, NB_KV_WIDTH), lambda b: (b, 0))],
        out_shape=[jax.ShapeDtypeStruct((m, NA_WIDTH), BF16),
                   jax.ShapeDtypeStruct((m, NB_WIDTH), BF16),
                   jax.ShapeDtypeStruct((m, NA_WIDTH), F32),
                   jax.ShapeDtypeStruct((m, NA_WIDTH), F32),
                   jax.ShapeDtypeStruct((m, NB_KV_WIDTH), F32),
                   jax.ShapeDtypeStruct((m, NB_KV_WIDTH), F32)],
        compiler_params=_cparams(("arbitrary",)),
        name="ctx_attn",
    )(proj, proj, proj, proj, proj, proj, qna, kna, qnb, knb, sink)


def _rope(x, cos, sin_a, sin_b):
    quarter = HEAD_DIM // 4
    return (x * cos + pltpu.roll(x, HEAD_DIM - quarter, 1) * sin_a
            + pltpu.roll(x, quarter, 1) * sin_b)


def _lat_attn_a_kernel(q_ref, k_ref, v_ref, ck_ref, cv_ref, bias_ref, qn_ref, kn_ref, o_ref):
    q = (_rms(q_ref[...], qn_ref[...]) * ATTN_SCALE).astype(BF16)
    k = _rms(k_ref[...], kn_ref[...]).astype(BF16)
    s_lat = _dot_nt(q, k) + bias_ref[...]
    s_ctx = _dot_nt(q, ck_ref[...].astype(BF16))
    o = _softmax_pv([s_lat, s_ctx], [v_ref[...].astype(BF16), cv_ref[...].astype(BF16)])
    o_ref[...] = o.astype(o_ref.dtype)


def _lat_attn_b_kernel(q_ref, k_ref, v_ref, ck_ref, cv_ref, cos_ref, sina_ref, sinb_ref,
                       qn_ref, kn_ref, sink_ref, o_ref):
    cos, sin_a, sin_b = cos_ref[...], sina_ref[...], sinb_ref[...]
    q = _rope(_rms(q_ref[...], qn_ref[...]), cos, sin_a, sin_b)
    k = _rope(_rms(k_ref[...], kn_ref[...]), cos, sin_a, sin_b)
    q = (q * ATTN_SCALE).astype(BF16)
    s_lat = _dot_nt(q, k.astype(BF16))
    length = s_lat.shape[0]
    qi = lax.broadcasted_iota(jnp.int32, (length, length), 0)
    kj = lax.broadcasted_iota(jnp.int32, (length, length), 1)
    s_lat = jnp.where(jnp.abs(qi - kj) <= WINDOW, s_lat, NEG_INF)
    s_ctx = _dot_nt(q, ck_ref[...].astype(BF16))
    o = _softmax_pv([s_lat, s_ctx], [v_ref[...].astype(BF16), cv_ref[...].astype(BF16)],
                    sink=sink_ref[pl.program_id(0)])
    o_ref[...] = o.astype(o_ref.dtype)


def _lat_attn_a_call(proj, length, ck, cv, bias, qn, kn):
    m = proj.shape[0]
    past = ck.shape[1]
    head = lambda h0: pl.BlockSpec((length, HEAD_DIM), lambda h, b: (b, h0 + h))
    cache = pl.BlockSpec((None, past, HEAD_DIM), lambda h, b: (b, 0, h))
    vec = pl.BlockSpec((1, HEAD_DIM), lambda h, b: (0, 0))
    return pl.pallas_call(
        _lat_attn_a_kernel,
        grid=(NA_HEADS, m // length),
        in_specs=[head(QA_HEAD0), head(KA_HEAD0), head(VA_HEAD0), cache, cache,
                  pl.BlockSpec((None, length, length), lambda h, b: (h, 0, 0)), vec, vec],
        out_specs=pl.BlockSpec((length, HEAD_DIM), lambda h, b: (b, h)),
        out_shape=jax.ShapeDtypeStruct((m, NA_WIDTH), BF16),
        compiler_params=_cparams(("arbitrary", "arbitrary")),
        name="lat_attn_a",
    )(proj, proj, proj, ck, cv, bias, qn, kn)


def _lat_attn_b_call(proj, length, ck, cv, cos, sin_a, sin_b, qn, kn, sink):
    m = proj.shape[0]
    past = ck.shape[1]
    qspec = pl.BlockSpec((length, HEAD_DIM), lambda h, b: (b, QB_HEAD0 + h))
    kvspec = lambda h0: pl.BlockSpec((length, HEAD_DIM), lambda h, b: (b, h0 + h // NB_GROUP))
    cache = pl.BlockSpec((None, past, HEAD_DIM), lambda h, b: (b, 0, h // NB_GROUP))
    table = pl.BlockSpec((length, HEAD_DIM), lambda h, b: (0, 0))
    vec = pl.BlockSpec((1, HEAD_DIM), lambda h, b: (0, 0))
    return pl.pallas_call(
        _lat_attn_b_kernel,
        grid=(NB_Q_HEADS, m // length),
        in_specs=[qspec, kvspec(KB_HEAD0), kvspec(VB_HEAD0), cache, cache,
                  table, table, table, vec, vec, pl.BlockSpec(memory_space=pltpu.SMEM)],
        out_specs=pl.BlockSpec((length, HEAD_DIM), lambda h, b: (b, h)),
        out_shape=jax.ShapeDtypeStruct((m, NB_WIDTH), BF16),
        compiler_params=_cparams(("arbitrary", "arbitrary")),
        name="lat_attn_b",
    )(proj, proj, proj, ck, cv, cos, sin_a, sin_b, qn, kn, sink)


def _na_bias(rpb, length):
    rows = length // GRID_W
    kr_n = min(NA_WIN_ROWS, rows)
    t = jnp.arange(length)
    r, c = t // GRID_W, t % GRID_W
    r0 = jnp.clip(r - kr_n // 2, 0, rows - kr_n)
    c0 = jnp.clip(c - NA_WIN_COLS // 2, 0, GRID_W - NA_WIN_COLS)
    row_ok = (r[None, :] >= r0[:, None]) & (r[None, :] < r0[:, None] + kr_n)
    col_ok = (c[None, :] >= c0[:, None]) & (c[None, :] < c0[:, None] + NA_WIN_COLS)
    dr = jnp.clip(r[None, :] - r[:, None] + (NA_WIN_ROWS - 1), 0, 2 * NA_WIN_ROWS - 2)
    dc = jnp.clip(c[None, :] - c[:, None], -(NA_WIN_COLS - 1), NA_WIN_COLS - 1) + (NA_WIN_COLS - 1)
    return jnp.where((row_ok & col_ok)[None], rpb[:, dr, dc].astype(F32), NEG_INF)


def _rope_tables(length):
    t = jnp.arange(length)
    row = (t // GRID_W).astype(F32)
    col = (t % GRID_W).astype(F32)
    n_freq = HEAD_DIM // 4
    inv = ROPE_BASE ** (-jnp.arange(n_freq, dtype=F32) / n_freq)
    ar = row[:, None] * inv
    ac = col[:, None] * inv
    ang = jnp.concatenate([ar, ar, ac, ac], axis=-1)
    cos, sin = jnp.cos(ang), jnp.sin(ang)
    lane = jnp.arange(HEAD_DIM)
    takes_left = ((lane // n_freq) % 2 == 0)[None, :]
    return cos, jnp.where(takes_left, -sin, 0.0), jnp.where(takes_left, 0.0, sin)


MERGE_TM = 512
MERGE_TN = 512


def _merge_kernel(oa_ref, ob_ref, ga_ref, gb_ref, wpa_ref, wpb_ref, wout_ref, x_ref,
                  g1_ref, sc2_ref, sh2_ref, n2w_ref, wr_ref,
                  x1_ref, h2_ref, lg_ref, acc_ref):
    j = pl.program_id(1)
    ya = _dot(oa_ref[...], wpa_ref[...])
    yb = _dot(ob_ref[...], wpb_ref[...])
    mix = jax.nn.sigmoid(ga_ref[...]) * ya + jax.nn.sigmoid(gb_ref[...]) * yb
    part = _dot(mix.astype(BF16), wout_ref[...])

    @pl.when(j == 0)
    def _():
        acc_ref[...] = part

    @pl.when(j > 0)
    def _():
        acc_ref[...] += part

    @pl.when(j == pl.num_programs(1) - 1)
    def _():
        x1 = x_ref[...] + g1_ref[0] * acc_ref[...]
        x1_ref[...] = x1
        h2 = _rms(x1, n2w_ref[...]) * (1.0 + sc2_ref[0]) + sh2_ref[0]
        h2_ref[...] = h2
        h_hi = h2.astype(BF16)
        h_lo = (h2 - h_hi.astype(F32)).astype(BF16)
        wr = wr_ref[...]
        w_hi = wr.astype(BF16)
        w_lo = (wr - w_hi.astype(F32)).astype(BF16)
        lg_ref[...] = _dot(h_hi, w_hi) + (_dot(h_lo, w_hi) + _dot(h_hi, w_lo))


def _merge_call(oa, ob, proj, wpa_bf, wpb_bf, wout_bf, x, g1, sc2, sh2, n2w, wr, tiles_per_group):
    m, d = x.shape
    nj = d // MERGE_TN
    gate0 = GATE_COL0 // MERGE_TN
    grp = lambda i, j: (i // tiles_per_group, 0, 0)
    row = lambda i, j: (i, 0)
    return pl.pallas_call(
        _merge_kernel,
        grid=(m // MERGE_TM, nj),
        in_specs=[pl.BlockSpec((MERGE_TM, NA_WIDTH), row),
                  pl.BlockSpec((MERGE_TM, NB_WIDTH), row),
                  pl.BlockSpec((MERGE_TM, MERGE_TN), lambda i, j: (i, gate0 + j)),
                  pl.BlockSpec((MERGE_TM, MERGE_TN), lambda i, j: (i, gate0 + nj + j)),
                  pl.BlockSpec((NA_WIDTH, MERGE_TN), lambda i, j: (0, j)),
                  pl.BlockSpec((NB_WIDTH, MERGE_TN), lambda i, j: (0, j)),
                  pl.BlockSpec((MERGE_TN, d), lambda i, j: (j, 0)),
                  pl.BlockSpec((MERGE_TM, d), row),
                  pl.BlockSpec((1, 1, d), grp), pl.BlockSpec((1, 1, d), grp), pl.BlockSpec((1, 1, d), grp),
                  pl.BlockSpec((1, d), lambda i, j: (0, 0)),
                  pl.BlockSpec((d, LOGIT_PAD), lambda i, j: (0, 0))],
        out_specs=[pl.BlockSpec((MERGE_TM, d), row),
                   pl.BlockSpec((MERGE_TM, d), row),
                   pl.BlockSpec((MERGE_TM, LOGIT_PAD), row)],
        out_shape=[jax.ShapeDtypeStruct((m, d), F32),
                   jax.ShapeDtypeStruct((m, d), F32),
                   jax.ShapeDtypeStruct((m, LOGIT_PAD), F32)],
        scratch_shapes=[pltpu.VMEM((MERGE_TM, d), F32)],
        compiler_params=_cparams(("arbitrary", "arbitrary")),
        name="merge",
    )(oa, ob, proj, proj, wpa_bf, wpb_bf, wout_bf, x, g1, sc2, sh2, n2w, wr)


ROUTE_TM = 512


def _first_index_of_max(vals, idx, n):
    mx = jnp.max(vals, axis=0, keepdims=True)
    first = jnp.min(jnp.where(vals == mx, idx, n), axis=0, keepdims=True)
    return mx, first


def _route_kernel(lg_ref, bias_ref, eid_ref, gw_ref, rank_ref, cnt_ref, base_ref):
    step = pl.program_id(0)

    @pl.when(step == 0)
    def _():
        base_ref[...] = jnp.zeros_like(base_ref)

    lt = lg_ref[...].T + bias_ref[...]
    n_tok = lt.shape[1]
    le = lt[0:N_EXPERTS]
    lgrp = lt[N_EXPERTS:N_EXPERTS + N_GROUPS]
    gi = lax.broadcasted_iota(jnp.int32, (N_GROUPS, n_tok), 0)
    gmax, gsel = _first_index_of_max(lgrp, gi, N_GROUPS)
    pg_sel = 1.0 / jnp.sum(jnp.exp(lgrp - gmax), axis=0, keepdims=True)
    le_sel = jnp.zeros((EXPERTS_PER_GROUP, n_tok), F32)
    for g in range(N_GROUPS):
        le_sel = jnp.where(gsel == g, le[g * EXPERTS_PER_GROUP:(g + 1) * EXPERTS_PER_GROUP], le_sel)
    ei = lax.broadcasted_iota(jnp.int32, (EXPERTS_PER_GROUP, n_tok), 0)
    v0, i0 = _first_index_of_max(le_sel, ei, EXPERTS_PER_GROUP)
    rest = jnp.where(ei == i0, -jnp.inf, le_sel)
    v1, i1 = _first_index_of_max(rest, ei, EXPERTS_PER_GROUP)
    e1 = jnp.exp(v1 - v0)
    w0 = pg_sel / (1.0 + e1)
    w1 = pg_sel * e1 / (1.0 + e1)
    eid0 = gsel * EXPERTS_PER_GROUP + i0
    eid1 = gsel * EXPERTS_PER_GROUP + i1

    xi = lax.broadcasted_iota(jnp.int32, (N_EXPERTS, n_tok), 0)
    si = lax.broadcasted_iota(jnp.int32, (n_tok, n_tok), 0)
    ti = lax.broadcasted_iota(jnp.int32, (n_tok, n_tok), 1)
    before = (si < ti).astype(BF16)
    base = base_ref[...]
    hot0 = (xi == eid0).astype(F32)
    hot1 = (xi == eid1).astype(F32)
    pre0 = _dot(hot0.astype(BF16), before)
    pre1 = _dot(hot1.astype(BF16), before)
    tot0 = jnp.sum(hot0, axis=1, keepdims=True)
    tot1 = jnp.sum(hot1, axis=1, keepdims=True)
    rank0 = jnp.sum(hot0 * (base + pre0), axis=0, keepdims=True)
    rank1 = jnp.sum(hot1 * (base + tot0 + pre1), axis=0, keepdims=True)
    base = base + tot0 + tot1
    base_ref[...] = base

    ri = lax.broadcasted_iota(jnp.int32, (8, n_tok), 0)
    pick = lambda a, b: jnp.where(ri == 0, a, jnp.where(ri == 1, b, jnp.zeros_like(a)))
    eid_ref[...] = pick(eid0, eid1)
    gw_ref[...] = pick(w0, w1)
    rank_ref[...] = pick(rank0, rank1).astype(jnp.int32)
    cnt_ref[...] = jnp.broadcast_to(base, cnt_ref.shape).astype(jnp.int32)


def _route_call(logits, bias_col):
    t = logits.shape[0]
    tok = pl.BlockSpec((8, ROUTE_TM), lambda i: (0, i))
    return pl.pallas_call(
        _route_kernel,
        grid=(t // ROUTE_TM,),
        in_specs=[pl.BlockSpec((ROUTE_TM, LOGIT_PAD), lambda i: (i, 0)),
                  pl.BlockSpec((LOGIT_PAD, 1), lambda i: (0, 0))],
        out_specs=[tok, tok, tok, pl.BlockSpec((N_EXPERTS, 128), lambda i: (0, 0))],
        out_shape=[jax.ShapeDtypeStruct((8, t), jnp.int32),
                   jax.ShapeDtypeStruct((8, t), F32),
                   jax.ShapeDtypeStruct((8, t), jnp.int32),
                   jax.ShapeDtypeStruct((N_EXPERTS, 128), jnp.int32)],
        scratch_shapes=[pltpu.VMEM((N_EXPERTS, 1), F32)],
        compiler_params=_cparams(("arbitrary",)),
        name="route",
    )(logits, bias_col)


def _moe_layout(n_pairs):
    padded_rows = -(-(n_pairs + N_EXPERTS * (MOE_ROW_BLOCK - 1)) // MOE_ROW_BLOCK) * MOE_ROW_BLOCK
    n_items = -(-padded_rows // MOE_ROW_GROUP) + N_EXPERTS
    return padded_rows, n_items


def _routing_tables(eid, rank, counts, n_items):
    padded = (counts + MOE_ROW_BLOCK - 1) // MOE_ROW_BLOCK * MOE_ROW_BLOCK
    pad_end = jnp.cumsum(padded)
    pad_start = pad_end - padded
    dest = (pad_start[eid] + rank).astype(jnp.int32)
    tail = jnp.where(padded > counts, pad_end - MOE_ROW_BLOCK, -1).astype(jnp.int32)
    per_expert = (padded + MOE_ROW_GROUP - 1) // MOE_ROW_GROUP
    item_end = jnp.cumsum(per_expert)
    item_start = item_end - per_expert
    total = item_end[-1]
    ii = jnp.arange(n_items, dtype=jnp.int32)
    e_of = jnp.minimum(jnp.searchsorted(item_end, ii, side="right"), N_EXPERTS - 1).astype(jnp.int32)
    valid = ii < total
    e_last = e_of[jnp.maximum(total - 1, 0)]
    local = ii - item_start[e_of]
    row0 = pad_start[e_of] + local * MOE_ROW_GROUP
    nblk = jnp.clip((padded[e_of] - local * MOE_ROW_GROUP) // MOE_ROW_BLOCK, 0, MOE_ROW_GROUP // MOE_ROW_BLOCK)
    item_e = jnp.where(valid, e_of, e_last).astype(jnp.int32)
    item_row0 = jnp.where(valid, row0, 0).astype(jnp.int32)
    item_nblk = jnp.where(valid, nblk, 0).astype(jnp.int32)
    return dest, tail, item_e, item_row0, item_nblk


DISPATCH_TOKENS = 1024


def _row_copy(src, s, dst, d, sem):
    return pltpu.make_async_copy(src.at[pl.ds(s, 1), :], dst.at[pl.ds(d, 1), :], sem)


def _dispatch_kernel(dest_ref, tail_ref, h_ctx, h_lat, xs, zero_buf, sem, *, n_ctx, n_tok):
    step = pl.program_id(0)
    tail_copy = lambda e: pltpu.make_async_copy(
        zero_buf, xs.at[pl.ds(pl.multiple_of(tail_ref[e], MOE_ROW_BLOCK), MOE_ROW_BLOCK), :], sem.at[1])

    @pl.when(step == 0)
    def _():
        zero_buf[...] = jnp.zeros_like(zero_buf)
        for e in range(N_EXPERTS):
            @pl.when(tail_ref[e] >= 0)
            def _():
                tail_copy(e).start()
        for e in range(N_EXPERTS):
            @pl.when(tail_ref[e] >= 0)
            def _():
                tail_copy(e).wait()

    tok0 = step * DISPATCH_TOKENS

    def scatter(src, src_tok0):
        def issue(t, carry):
            tg = tok0 + t
            _row_copy(src, tg - src_tok0, xs, dest_ref[tg], sem.at[0]).start()
            _row_copy(src, tg - src_tok0, xs, dest_ref[n_tok + tg], sem.at[0]).start()
            return carry

        def drain(t, carry):
            _row_copy(src, 0, xs, 0, sem.at[0]).wait()
            _row_copy(src, 0, xs, 0, sem.at[0]).wait()
            return carry

        lax.fori_loop(0, DISPATCH_TOKENS, issue, 0)
        lax.fori_loop(0, DISPATCH_TOKENS, drain, 0)

    @pl.when(tok0 < n_ctx)
    def _():
        scatter(h_ctx, 0)

    @pl.when(tok0 >= n_ctx)
    def _():
        scatter(h_lat, n_ctx)


def _dispatch_call(dest_flat, tail, h_ctx, h_lat, padded_rows):
    n_ctx, d = h_ctx.shape
    n_tok = n_ctx + h_lat.shape[0]
    return pl.pallas_call(
        functools.partial(_dispatch_kernel, n_ctx=n_ctx, n_tok=n_tok),
        grid_spec=pltpu.PrefetchScalarGridSpec(
            num_scalar_prefetch=2,
            grid=(n_tok // DISPATCH_TOKENS,),
            in_specs=[pl.BlockSpec(memory_space=pl.ANY), pl.BlockSpec(memory_space=pl.ANY)],
            out_specs=pl.BlockSpec(memory_space=pl.ANY),
            scratch_shapes=[pltpu.VMEM((MOE_ROW_BLOCK, d), F32), pltpu.SemaphoreType.DMA((2,))]),
        out_shape=jax.ShapeDtypeStruct((padded_rows, d), F32),
        compiler_params=_cparams(("arbitrary",)),
        name="dispatch",
    )(dest_flat, tail, h_ctx, h_lat)


def _moe_kernel(item_e, item_row0, item_nblk, xs, w1_ref, w3_ref, w2_ref, ys,
                x_f32, x_bf, acc, w1_bf, w3_bf, w2_bf, sem):
    i = pl.program_id(0)
    c = pl.program_id(1)
    nblk = item_nblk[i]
    row0 = pl.multiple_of(item_row0[i], MOE_ROW_BLOCK)
    max_blk = MOE_ROW_GROUP // MOE_ROW_BLOCK
    blk = lambda b: pl.ds(b * MOE_ROW_BLOCK, MOE_ROW_BLOCK)
    load = lambda b: pltpu.make_async_copy(
        xs.at[pl.ds(row0 + b * MOE_ROW_BLOCK, MOE_ROW_BLOCK), :], x_f32.at[blk(b), :], sem.at[0])
    store = lambda b: pltpu.make_async_copy(
        acc.at[blk(b), :], ys.at[pl.ds(row0 + b * MOE_ROW_BLOCK, MOE_ROW_BLOCK), :], sem.at[1])

    @pl.when(nblk > 0)
    def _():
        @pl.when(c == 0)
        def _():
            for b in range(max_blk):
                @pl.when(b < nblk)
                def _():
                    load(b).start()
            for b in range(max_blk):
                @pl.when(b < nblk)
                def _():
                    load(b).wait()
            for b in range(max_blk):
                @pl.when(b < nblk)
                def _():
                    x_bf[blk(b), :] = x_f32[blk(b), :].astype(BF16)
                    acc[blk(b), :] = jnp.zeros((MOE_ROW_BLOCK, acc.shape[1]), F32)

        w1_bf[...] = w1_ref[0].astype(BF16)
        w3_bf[...] = w3_ref[0].astype(BF16)
        w2_bf[...] = w2_ref[0].astype(BF16)

        def body(b, carry):
            rows = pl.ds(pl.multiple_of(b * MOE_ROW_BLOCK, MOE_ROW_BLOCK), MOE_ROW_BLOCK)
            x = x_bf[rows, :]
            h1 = _dot(x, w1_bf[...])
            h3 = _dot(x, w3_bf[...])
            a = (h1 * jax.nn.sigmoid(h1) * h3).astype(BF16)
            acc[rows, :] += _dot(a, w2_bf[...])
            return carry

        lax.fori_loop(0, nblk, body, 0)

        @pl.when(c == pl.num_programs(1) - 1)
        def _():
            for b in range(max_blk):
                @pl.when(b < nblk)
                def _():
                    store(b).start()
            for b in range(max_blk):
                @pl.when(b < nblk)
                def _():
                    store(b).wait()


def _moe_call(item_e, item_row0, item_nblk, xs, w1, w3, w2):
    padded_rows, d = xs.shape
    n_items = item_e.shape[0]
    f = w1.shape[2]
    nc = f // MOE_F_CHUNK
    return pl.pallas_call(
        _moe_kernel,
        grid_spec=pltpu.PrefetchScalarGridSpec(
            num_scalar_prefetch=3,
            grid=(n_items, nc),
            in_specs=[pl.BlockSpec(memory_space=pl.ANY),
                      pl.BlockSpec((1, d, MOE_F_CHUNK), lambda i, c, ie, ir, ib: (ie[i], 0, c)),
                      pl.BlockSpec((1, d, MOE_F_CHUNK), lambda i, c, ie, ir, ib: (ie[i], 0, c)),
                      pl.BlockSpec((1, MOE_F_CHUNK, d), lambda i, c, ie, ir, ib: (ie[i], c, 0))],
            out_specs=pl.BlockSpec(memory_space=pl.ANY),
            scratch_shapes=[pltpu.VMEM((MOE_ROW_GROUP, d), F32),
                            pltpu.VMEM((MOE_ROW_GROUP, d), BF16),
                            pltpu.VMEM((MOE_ROW_GROUP, d), F32),
                            pltpu.VMEM((d, MOE_F_CHUNK), BF16),
                            pltpu.VMEM((d, MOE_F_CHUNK), BF16),
                            pltpu.VMEM((MOE_F_CHUNK, d), BF16),
                            pltpu.SemaphoreType.DMA((2,))]),
        out_shape=jax.ShapeDtypeStruct((padded_rows, d), F32),
        compiler_params=_cparams(("arbitrary", "arbitrary")),
        name="moe",
    )(item_e, item_row0, item_nblk, xs, w1, w3, w2)


COMBINE_TM = 256


def _combine_kernel(dest_ref, x1_ref, gw_ref, g2_ref, ys, o_ref, y0, y1, sem, *, tok_base, n_tok):
    tok0 = tok_base + pl.program_id(0) * COMBINE_TM

    def issue(t, carry):
        _row_copy(ys, dest_ref[tok0 + t], y0, t, sem).start()
        _row_copy(ys, dest_ref[n_tok + tok0 + t], y1, t, sem).start()
        return carry

    def drain(t, carry):
        _row_copy(ys, 0, y0, 0, sem).wait()
        _row_copy(ys, 0, y1, 0, sem).wait()
        return carry

    lax.fori_loop(0, COMBINE_TM, issue, 0)
    lax.fori_loop(0, COMBINE_TM, drain, 0)
    gw = gw_ref[...]
    moe = gw[:, 0:1] * y0[...] + gw[:, 1:2] * y1[...]
    o_ref[...] = x1_ref[...] + g2_ref[0] * moe


def _combine_call(dest_flat, x1, gw, g2, ys, tok_base, n_tok, tiles_per_group):
    m, d = x1.shape
    return pl.pallas_call(
        functools.partial(_combine_kernel, tok_base=tok_base, n_tok=n_tok),
        grid_spec=pltpu.PrefetchScalarGridSpec(
            num_scalar_prefetch=1,
            grid=(m // COMBINE_TM,),
            in_specs=[pl.BlockSpec((COMBINE_TM, d), lambda i, dr: (i, 0)),
                      pl.BlockSpec((COMBINE_TM, 2), lambda i, dr: (i, 0)),
                      pl.BlockSpec((1, 1, d), lambda i, dr: (i // tiles_per_group, 0, 0)),
                      pl.BlockSpec(memory_space=pl.ANY)],
            out_specs=pl.BlockSpec((COMBINE_TM, d), lambda i, dr: (i, 0)),
            scratch_shapes=[pltpu.VMEM((COMBINE_TM, d), F32), pltpu.VMEM((COMBINE_TM, d), F32),
                            pltpu.SemaphoreType.DMA]),
        out_shape=jax.ShapeDtypeStruct((m, d), F32),
        compiler_params=_cparams(("arbitrary",)),
        name="combine",
    )(dest_flat, x1, gw, g2, ys)


def kernel(x_prompt, x_sample, cache_a_k, cache_a_v, cache_b_k, cache_b_v, c, c_ctx, norm1_w, norm2_w, w_ada, b_ada, w_in, qn_a, kn_a, qn_b, kn_b, rpb_a, sink_b, w_pa, w_pb, w_out, w_rg, b_rg, w_re, b_re, w1, w3, w2):
    batch, seq, d = x_prompt.shape
    dec_batch, dec_seq, _ = x_sample.shape
    depth = norm1_w.shape[0]
    assert depth == 1, "one trunk layer"
    past = cache_a_k.shape[2]
    n_ctx, n_lat = batch * seq, dec_batch * dec_seq
    n_tok = n_ctx + n_lat

    xc = x_prompt.reshape(n_ctx, d)
    xl = x_sample.reshape(n_lat, d)

    n_rows = 8
    cvec = jnp.zeros((n_rows, d), F32).at[0].set(c_ctx).at[1:1 + dec_batch].set(c)
    mod = _ada_call(cvec, w_ada[0], b_ada[0][None, :])
    sh1, sc1, g1, sh2, sc2, g2 = [mod[:, i * d:(i + 1) * d][:, None, :] for i in range(6)]
    ctx_rows, lat_rows = slice(0, 1), slice(1, 1 + dec_batch)

    nw1, nw2 = norm1_w[0][None, :], norm2_w[0][None, :]
    qna, kna, qnb, knb = qn_a[0][None, :], kn_a[0][None, :], qn_b[0][None, :], kn_b[0][None, :]
    sink = sink_b[0]
    w_in_bf = w_in[0].astype(BF16)
    wpa_bf, wpb_bf, wout_bf = w_pa[0].astype(BF16), w_pb[0].astype(BF16), w_out[0].astype(BF16)
    wr = jnp.zeros((d, LOGIT_PAD), F32).at[:, :N_EXPERTS].set(w_re[0]).at[:, N_EXPERTS:N_EXPERTS + N_GROUPS].set(w_rg[0])
    br = jnp.zeros((LOGIT_PAD, 1), F32).at[:N_EXPERTS, 0].set(b_re[0]).at[N_EXPERTS:N_EXPERTS + N_GROUPS, 0].set(b_rg[0])

    proj_c = _inproj_call(xc, nw1, sc1[ctx_rows], sh1[ctx_rows], w_in_bf, n_ctx // INPROJ_TM)
    proj_l = _inproj_call(xl, nw1, sc1[lat_rows], sh1[lat_rows], w_in_bf, dec_seq // INPROJ_TM)

    oa_c, ob_c, new_a_k, new_a_v, new_b_k, new_b_v = _ctx_attn_call(proj_c, seq, qna, kna, qnb, knb, sink)

    bias = _na_bias(rpb_a[0], dec_seq)
    cos, sin_a, sin_b = _rope_tables(dec_seq)
    cak = cache_a_k[:, 0].reshape(dec_batch, past, NA_WIDTH)
    cav = cache_a_v[:, 0].reshape(dec_batch, past, NA_WIDTH)
    cbk = cache_b_k[:, 0].reshape(dec_batch, past, NB_KV_WIDTH)
    cbv = cache_b_v[:, 0].reshape(dec_batch, past, NB_KV_WIDTH)
    oa_l = _lat_attn_a_call(proj_l, dec_seq, cak, cav, bias, qna, kna)
    ob_l = _lat_attn_b_call(proj_l, dec_seq, cbk, cbv, cos, sin_a, sin_b, qnb, knb, sink)

    x1_c, h2_c, lg_c = _merge_call(oa_c, ob_c, proj_c, wpa_bf, wpb_bf, wout_bf, xc,
                                   g1[ctx_rows], sc2[ctx_rows], sh2[ctx_rows], nw2, wr, n_ctx // MERGE_TM)
    x1_l, h2_l, lg_l = _merge_call(oa_l, ob_l, proj_l, wpa_bf, wpb_bf, wout_bf, xl,
                                   g1[lat_rows], sc2[lat_rows], sh2[lat_rows], nw2, wr, dec_seq // MERGE_TM)

    eid, gw, rank, cnt = _route_call(jnp.concatenate([lg_c, lg_l], axis=0), br)
    padded_rows, n_items = _moe_layout(2 * n_tok)
    dest, tail, item_e, item_row0, item_nblk = _routing_tables(eid[:2], rank[:2], cnt[:, 0], n_items)
    dest_flat = dest.reshape(-1)
    xs = _dispatch_call(dest_flat, tail, h2_c, h2_l, padded_rows)
    ys = _moe_call(item_e, item_row0, item_nblk, xs, w1[0], w3[0], w2[0])
    gw_t = gw[:2].T
    y_c = _combine_call(dest_flat, x1_c, gw_t[:n_ctx], g2[ctx_rows], ys, 0, n_tok, n_ctx // COMBINE_TM)
    y_l = _combine_call(dest_flat, x1_l, gw_t[n_ctx:], g2[lat_rows], ys, n_ctx, n_tok, dec_seq // COMBINE_TM)

    state = lambda a, heads: a.reshape(batch, 1, seq, heads, HEAD_DIM)
    return (y_c.reshape(batch, seq, d), y_l.reshape(dec_batch, dec_seq, d),
            state(new_a_k, NA_HEADS), state(new_a_v, NA_HEADS),
            state(new_b_k, NB_KV_HEADS), state(new_b_v, NB_KV_HEADS))
```

```python
import functools

import jax
import jax.numpy as jnp
import numpy as np
from jax import lax
from jax.experimental import pallas as pl
from jax.experimental.pallas import tpu as pltpu

D_MODEL = 2048
HEAD_DIM = 128
NA_HEADS = 8
NA_WIDTH = NA_HEADS * HEAD_DIM
NB_Q_HEADS = 8
NB_KV_HEADS = 2
NB_GROUP = NB_Q_HEADS // NB_KV_HEADS
NB_WIDTH = NB_Q_HEADS * HEAD_DIM
NB_KV_WIDTH = NB_KV_HEADS * HEAD_DIM
GRID_W = 64
NA_WIN_ROWS = 8
NA_WIN_COLS = 16
WINDOW = 128
N_GROUPS = 4
EXPERTS_PER_GROUP = 8
N_EXPERTS = N_GROUPS * EXPERTS_PER_GROUP
D_EXPERT = 1024
IN_WIDTH = 3 * NA_WIDTH + NB_WIDTH + 2 * NB_KV_WIDTH + 2 * D_MODEL
ROPE_BASE = 10000.0
NORM_EPS = 1e-6
NEG_INF = -1e30
ATTN_SCALE = HEAD_DIM ** -0.5

QA_HEAD0 = 0
KA_HEAD0 = NA_HEADS
VA_HEAD0 = 2 * NA_HEADS
QB_HEAD0 = 3 * NA_HEADS
KB_HEAD0 = QB_HEAD0 + NB_Q_HEADS
VB_HEAD0 = KB_HEAD0 + NB_KV_HEADS
GATE_COL0 = (VB_HEAD0 + NB_KV_HEADS) * HEAD_DIM

LOGIT_PAD = 128
MOE_ROW_BLOCK = 256
MOE_ROW_GROUP = 1024
MOE_F_CHUNK = 512
VMEM_LIMIT = 56 * 1024 * 1024

F32 = jnp.float32
BF16 = jnp.bfloat16


def _cparams(sem):
    return pltpu.CompilerParams(dimension_semantics=sem, vmem_limit_bytes=VMEM_LIMIT)


def _rms(x, w):
    return x * lax.rsqrt(jnp.mean(x * x, axis=-1, keepdims=True) + NORM_EPS) * w


def _dot(a, b):
    return jnp.dot(a, b, preferred_element_type=F32)


def _dot_nt(a, b):
    return lax.dot_general(a, b, (((1,), (1,)), ((), ())), preferred_element_type=F32)


def _ada_kernel(c_ref, w_ref, b_ref, o_ref):
    c = c_ref[...]
    s = c * jax.nn.sigmoid(c)
    o_ref[...] = jnp.dot(s, w_ref[...], precision=lax.Precision.HIGHEST,
                         preferred_element_type=F32) + b_ref[...]


def _ada_call(cvec, w_ada, b_ada):
    rows, d = cvec.shape
    n = w_ada.shape[1]
    tn = 1024
    return pl.pallas_call(
        _ada_kernel,
        grid=(n // tn,),
        in_specs=[pl.BlockSpec((rows, d), lambda j: (0, 0)),
                  pl.BlockSpec((d, tn), lambda j: (0, j)),
                  pl.BlockSpec((1, tn), lambda j: (0, j))],
        out_specs=pl.BlockSpec((rows, tn), lambda j: (0, j)),
        out_shape=jax.ShapeDtypeStruct((rows, n), F32),
        compiler_params=_cparams(("arbitrary",)),
        name="ada",
    )(cvec, w_ada, b_ada)


INPROJ_TM = 1024
INPROJ_TN = 512
NORM_ROWS = 128


def _inproj_kernel(x_ref, nw_ref, sc_ref, sh_ref, w_ref, o_ref, h_scr):
    @pl.when(pl.program_id(1) == 0)
    def _():
        nw = nw_ref[...]
        sc = 1.0 + sc_ref[0]
        sh = sh_ref[0]

        def body(r, carry):
            rows = pl.ds(pl.multiple_of(r * NORM_ROWS, NORM_ROWS), NORM_ROWS)
            h_scr[rows, :] = (_rms(x_ref[rows, :], nw) * sc + sh).astype(BF16)
            return carry

        lax.fori_loop(0, INPROJ_TM // NORM_ROWS, body, 0)

    o_ref[...] = _dot(h_scr[...], w_ref[...])


def _inproj_call(x, nw, sc, sh, w_bf, tiles_per_group):
    m, d = x.shape
    n = w_bf.shape[1]
    grp = lambda i, j: (i // tiles_per_group, 0, 0)
    return pl.pallas_call(
        _inproj_kernel,
        grid=(m // INPROJ_TM, n // INPROJ_TN),
        in_specs=[pl.BlockSpec((INPROJ_TM, d), lambda i, j: (i, 0)),
                  pl.BlockSpec((1, d), lambda i, j: (0, 0)),
                  pl.BlockSpec((1, 1, d), grp),
                  pl.BlockSpec((1, 1, d), grp),
                  pl.BlockSpec((d, INPROJ_TN), lambda i, j: (0, j))],
        out_specs=pl.BlockSpec((INPROJ_TM, INPROJ_TN), lambda i, j: (i, j)),
        out_shape=jax.ShapeDtypeStruct((m, n), F32),
        scratch_shapes=[pltpu.VMEM((INPROJ_TM, d), BF16)],
        compiler_params=_cparams(("arbitrary", "arbitrary")),
        name="inproj",
    )(x, nw, sc, sh, w_bf)


def _softmax_pv(scores, values, sink=None):
    m = None
    for s in scores:
        ms = jnp.max(s, axis=-1, keepdims=True)
        m = ms if m is None else jnp.maximum(m, ms)
    if sink is not None:
        m = jnp.maximum(m, sink)
    den = None
    acc = None
    for s, v in zip(scores, values):
        e = jnp.exp(s - m)
        ds = jnp.sum(e, axis=-1, keepdims=True)
        den = ds if den is None else den + ds
        pv = _dot(e.astype(BF16), v)
        acc = pv if acc is None else acc + pv
    if sink is not None:
        den = den + jnp.exp(sink - m)
    return acc / den


def _ctx_attn_kernel(qa_ref, ka_ref, va_ref, qb_ref, kb_ref, vb_ref,
                     qna_ref, kna_ref, qnb_ref, knb_ref, sink_ref,
                     oa_ref, ob_ref, nak_ref, nav_ref, nbk_ref, nbv_ref):
    qna, kna, qnb, knb = qna_ref[...], kna_ref[...], qnb_ref[...], knb_ref[...]
    for h in range(NA_HEADS):
        cols = slice(h * HEAD_DIM, (h + 1) * HEAD_DIM)
        q = (_rms(qa_ref[:, cols], qna) * ATTN_SCALE).astype(BF16)
        k = _rms(ka_ref[:, cols], kna)
        v = va_ref[:, cols]
        nak_ref[:, cols] = k
        nav_ref[:, cols] = v
        s = _dot_nt(q, k.astype(BF16))
        oa_ref[:, cols] = _softmax_pv([s], [v.astype(BF16)]).astype(oa_ref.dtype)
    for kv in range(NB_KV_HEADS):
        kcols = slice(kv * HEAD_DIM, (kv + 1) * HEAD_DIM)
        k = _rms(kb_ref[:, kcols], knb)
        v = vb_ref[:, kcols]
        nbk_ref[:, kcols] = k
        nbv_ref[:, kcols] = v
        kb16 = k.astype(BF16)
        vb16 = v.astype(BF16)
        for g in range(NB_GROUP):
            hq = kv * NB_GROUP + g
            cols = slice(hq * HEAD_DIM, (hq + 1) * HEAD_DIM)
            q = (_rms(qb_ref[:, cols], qnb) * ATTN_SCALE).astype(BF16)
            s = _dot_nt(q, kb16)
            ob_ref[:, cols] = _softmax_pv([s], [vb16], sink=sink_ref[hq]).astype(ob_ref.dtype)


def _ctx_attn_call(proj, seq, qna, kna, qnb, knb, sink):
    m = proj.shape[0]
    nb = m // seq
    wide = lambda blk: pl.BlockSpec((seq, NA_WIDTH), lambda b: (b, blk))
    narrow = lambda blk: pl.BlockSpec((seq, NB_KV_WIDTH), lambda b: (b, blk))
    vec = pl.BlockSpec((1, HEAD_DIM), lambda b: (0, 0))
    return pl.pallas_call(
        _ctx_attn_kernel,
        grid=(nb,),
        in_specs=[wide(QA_HEAD0 // NA_HEADS), wide(KA_HEAD0 // NA_HEADS), wide(VA_HEAD0 // NA_HEADS),
                  wide(QB_HEAD0 // NA_HEADS), narrow(KB_HEAD0 // NB_KV_HEADS), narrow(VB_HEAD0 // NB_KV_HEADS),
                  vec, vec, vec, vec,
                  pl.BlockSpec(memory_space=pltpu.SMEM)],
        out_specs=[pl.BlockSpec((seq, NA_WIDTH), lambda b: (b, 0)),
                   pl.BlockSpec((seq, NB_WIDTH), lambda b: (b, 0)),
                   pl.BlockSpec((seq, NA_WIDTH), lambda b: (b, 0)),
                   pl.BlockSpec((seq, NA_WIDTH), lambda b: (b, 0)),
                   pl.BlockSpec((seq, NB_KV_WIDTH), lambda b: (b, 0)),
                   pl.BlockSpec((seq, NB_KV_WIDTH), lambda b: (b, 0))],
        out_shape=[jax.ShapeDtypeStruct((m, NA_WIDTH), BF16),
                   jax.ShapeDtypeStruct((m, NB_WIDTH), BF16),
                   jax.ShapeDtypeStruct((m, NA_WIDTH), F32),
                   jax.ShapeDtypeStruct((m, NA_WIDTH), F32),
                   jax.ShapeDtypeStruct((m, NB_KV_WIDTH), F32),
                   jax.ShapeDtypeStruct((m, NB_KV_WIDTH), F32)],
        compiler_params=_cparams(("arbitrary",)),
        name="ctx_attn",
    )(proj, proj, proj, proj, proj, proj, qna, kna, qnb, knb, sink)


def _rope(x, cos, sin_a, sin_b):
    quarter = HEAD_DIM // 4
    return (x * cos + pltpu.roll(x, HEAD_DIM - quarter, 1) * sin_a
            + pltpu.roll(x, quarter, 1) * sin_b)


def _lat_attn_a_kernel(q_ref, k_ref, v_ref, ck_ref, cv_ref, bias_ref, qn_ref, kn_ref, o_ref):
    q = (_rms(q_ref[...], qn_ref[...]) * ATTN_SCALE).astype(BF16)
    k = _rms(k_ref[...], kn_ref[...]).astype(BF16)
    s_lat = _dot_nt(q, k) + bias_ref[...]
    s_ctx = _dot_nt(q, ck_ref[...].astype(BF16))
    o = _softmax_pv([s_lat, s_ctx], [v_ref[...].astype(BF16), cv_ref[...].astype(BF16)])
    o_ref[...] = o.astype(o_ref.dtype)


def _lat_attn_b_kernel(q_ref, k_ref, v_ref, ck_ref, cv_ref, cos_ref, sina_ref, sinb_ref,
                       qn_ref, kn_ref, sink_ref, o_ref):
    cos, sin_a, sin_b = cos_ref[...], sina_ref[...], sinb_ref[...]
    q = _rope(_rms(q_ref[...], qn_ref[...]), cos, sin_a, sin_b)
    k = _rope(_rms(k_ref[...], kn_ref[...]), cos, sin_a, sin_b)
    q = (q * ATTN_SCALE).astype(BF16)
    s_lat = _dot_nt(q, k.astype(BF16))
    length = s_lat.shape[0]
    qi = lax.broadcasted_iota(jnp.int32, (length, length), 0)
    kj = lax.broadcasted_iota(jnp.int32, (length, length), 1)
    s_lat = jnp.where(jnp.abs(qi - kj) <= WINDOW, s_lat, NEG_INF)
    s_ctx = _dot_nt(q, ck_ref[...].astype(BF16))
    o = _softmax_pv([s_lat, s_ctx], [v_ref[...].astype(BF16), cv_ref[...].astype(BF16)],
                    sink=sink_ref[pl.program_id(0)])
    o_ref[...] = o.astype(o_ref.dtype)


def _lat_attn_a_call(proj, length, ck, cv, bias, qn, kn):
    m = proj.shape[0]
    past = ck.shape[1]
    head = lambda h0: pl.BlockSpec((length, HEAD_DIM), lambda h, b: (b, h0 + h))
    cache = pl.BlockSpec((None, past, HEAD_DIM), lambda h, b: (b, 0, h))
    vec = pl.BlockSpec((1, HEAD_DIM), lambda h, b: (0, 0))
    return pl.pallas_call(
        _lat_attn_a_kernel,
        grid=(NA_HEADS, m // length),
        in_specs=[head(QA_HEAD0), head(KA_HEAD0), head(VA_HEAD0), cache, cache,
                  pl.BlockSpec((None, length, length), lambda h, b: (h, 0, 0)), vec, vec],
        out_specs=pl.BlockSpec((length, HEAD_DIM), lambda h, b: (b, h)),
        out_shape=jax.ShapeDtypeStruct((m, NA_WIDTH), BF16),
        compiler_params=_cparams(("arbitrary", "arbitrary")),
        name="lat_attn_a",
    )(proj, proj, proj, ck, cv, bias, qn, kn)


def _lat_attn_b_call(proj, length, ck, cv, cos, sin_a, sin_b, qn, kn, sink):
    m = proj.shape[0]
    past = ck.shape[1]
    qspec = pl.BlockSpec((length, HEAD_DIM), lambda h, b: (b, QB_HEAD0 + h))
    kvspec = lambda h0: pl.BlockSpec((length, HEAD_DIM), lambda h, b: (b, h0 + h // NB_GROUP))
    cache = pl.BlockSpec((None, past, HEAD_DIM), lambda h, b: (b, 0, h // NB_GROUP))
    table = pl.BlockSpec((length, HEAD_DIM), lambda h, b: (0, 0))
    vec = pl.BlockSpec((1, HEAD_DIM), lambda h, b: (0, 0))
    return pl.pallas_call(
        _lat_attn_b_kernel,
        grid=(NB_Q_HEADS, m // length),
        in_specs=[qspec, kvspec(KB_HEAD0), kvspec(VB_HEAD0), cache, cache,
                  table, table, table, vec, vec, pl.BlockSpec(memory_space=pltpu.SMEM)],
        out_specs=pl.BlockSpec((length, HEAD_DIM), lambda h, b: (b, h)),
        out_shape=jax.ShapeDtypeStruct((m, NB_WIDTH), BF16),
        compiler_params=_cparams(("arbitrary", "arbitrary")),
        name="lat_attn_b",
    )(proj, proj, proj, ck, cv, cos, sin_a, sin_b, qn, kn, sink)


def _na_bias(rpb, length):
    rows = length // GRID_W
    kr_n = min(NA_WIN_ROWS, rows)
    r = np.arange(rows)
    c = np.arange(GRID_W)
    r0 = np.clip(r - kr_n // 2, 0, rows - kr_n)
    c0 = np.clip(c - NA_WIN_COLS // 2, 0, GRID_W - NA_WIN_COLS)
    row_ok = (r[None, :] >= r0[:, None]) & (r[None, :] < r0[:, None] + kr_n)
    col_ok = (c[None, :] >= c0[:, None]) & (c[None, :] < c0[:, None] + NA_WIN_COLS)
    dr = r[None, :] - r[:, None] + (NA_WIN_ROWS - 1)
    dc = np.clip(c[None, :] - c[:, None], -(NA_WIN_COLS - 1), NA_WIN_COLS - 1) + (NA_WIN_COLS - 1)
    row_sel = ((dr[None] == np.arange(2 * NA_WIN_ROWS - 1)[:, None, None]) & row_ok[None]).astype(np.float32)
    col_sel = ((dc[None] == np.arange(2 * NA_WIN_COLS - 1)[:, None, None]) & col_ok[None]).astype(np.float32)
    hi = lax.Precision.HIGHEST
    per_col = jnp.einsum("hde,eqk->hdqk", rpb.astype(F32), col_sel, precision=hi)
    table = jnp.einsum("drs,hdqk->hrqsk", row_sel, per_col, precision=hi)
    valid = row_ok[:, None, :, None] & col_ok[None, :, None, :]
    return jnp.where(valid[None], table, NEG_INF).reshape(rpb.shape[0], length, length)


def _rope_tables(length):
    t = jnp.arange(length)
    row = (t // GRID_W).astype(F32)
    col = (t % GRID_W).astype(F32)
    n_freq = HEAD_DIM // 4
    inv = ROPE_BASE ** (-jnp.arange(n_freq, dtype=F32) / n_freq)
    ar = row[:, None] * inv
    ac = col[:, None] * inv
    ang = jnp.concatenate([ar, ar, ac, ac], axis=-1)
    cos, sin = jnp.cos(ang), jnp.sin(ang)
    lane = jnp.arange(HEAD_DIM)
    takes_left = ((lane // n_freq) % 2 == 0)[None, :]
    return cos, jnp.where(takes_left, -sin, 0.0), jnp.where(takes_left, 0.0, sin)


MERGE_TM = 512
MERGE_TN = 512


def _merge_kernel(oa_ref, ob_ref, ga_ref, gb_ref, wpa_ref, wpb_ref, wout_ref, x_ref,
                  g1_ref, sc2_ref, sh2_ref, n2w_ref, wr_ref,
                  x1_ref, h2_ref, lg_ref, acc_ref):
    j = pl.program_id(1)
    ya = _dot(oa_ref[...], wpa_ref[...])
    yb = _dot(ob_ref[...], wpb_ref[...])
    mix = jax.nn.sigmoid(ga_ref[...]) * ya + jax.nn.sigmoid(gb_ref[...]) * yb
    part = _dot(mix.astype(BF16), wout_ref[...])

    @pl.when(j == 0)
    def _():
        acc_ref[...] = part

    @pl.when(j > 0)
    def _():
        acc_ref[...] += part

    @pl.when(j == pl.num_programs(1) - 1)
    def _():
        x1 = x_ref[...] + g1_ref[0] * acc_ref[...]
        x1_ref[...] = x1
        h2 = _rms(x1, n2w_ref[...]) * (1.0 + sc2_ref[0]) + sh2_ref[0]
        h2_ref[...] = h2
        h_hi = h2.astype(BF16)
        h_lo = (h2 - h_hi.astype(F32)).astype(BF16)
        wr = wr_ref[...]
        w_hi = wr.astype(BF16)
        w_lo = (wr - w_hi.astype(F32)).astype(BF16)
        lg_ref[...] = _dot(h_hi, w_hi) + (_dot(h_lo, w_hi) + _dot(h_hi, w_lo))


def _merge_call(oa, ob, proj, wpa_bf, wpb_bf, wout_bf, x, g1, sc2, sh2, n2w, wr, tiles_per_group):
    m, d = x.shape
    nj = d // MERGE_TN
    gate0 = GATE_COL0 // MERGE_TN
    grp = lambda i, j: (i // tiles_per_group, 0, 0)
    row = lambda i, j: (i, 0)
    return pl.pallas_call(
        _merge_kernel,
        grid=(m // MERGE_TM, nj),
        in_specs=[pl.BlockSpec((MERGE_TM, NA_WIDTH), row),
                  pl.BlockSpec((MERGE_TM, NB_WIDTH), row),
                  pl.BlockSpec((MERGE_TM, MERGE_TN), lambda i, j: (i, gate0 + j)),
                  pl.BlockSpec((MERGE_TM, MERGE_TN), lambda i, j: (i, gate0 + nj + j)),
                  pl.BlockSpec((NA_WIDTH, MERGE_TN), lambda i, j: (0, j)),
                  pl.BlockSpec((NB_WIDTH, MERGE_TN), lambda i, j: (0, j)),
                  pl.BlockSpec((MERGE_TN, d), lambda i, j: (j, 0)),
                  pl.BlockSpec((MERGE_TM, d), row),
                  pl.BlockSpec((1, 1, d), grp), pl.BlockSpec((1, 1, d), grp), pl.BlockSpec((1, 1, d), grp),
                  pl.BlockSpec((1, d), lambda i, j: (0, 0)),
                  pl.BlockSpec((d, LOGIT_PAD), lambda i, j: (0, 0))],
        out_specs=[pl.BlockSpec((MERGE_TM, d), row),
                   pl.BlockSpec((MERGE_TM, d), row),
                   pl.BlockSpec((MERGE_TM, LOGIT_PAD), row)],
        out_shape=[jax.ShapeDtypeStruct((m, d), F32),
                   jax.ShapeDtypeStruct((m, d), F32),
                   jax.ShapeDtypeStruct((m, LOGIT_PAD), F32)],
        scratch_shapes=[pltpu.VMEM((MERGE_TM, d), F32)],
        compiler_params=_cparams(("arbitrary", "arbitrary")),
        name="merge",
    )(oa, ob, proj, proj, wpa_bf, wpb_bf, wout_bf, x, g1, sc2, sh2, n2w, wr)


ROUTE_TM = 512


def _first_index_of_max(vals, idx, n):
    mx = jnp.max(vals, axis=0, keepdims=True)
    first = jnp.min(jnp.where(vals == mx, idx, n), axis=0, keepdims=True)
    return mx, first


def _route_kernel(lg_ref, bias_ref, eid_ref, gw_ref, rank_ref, cnt_ref, base_ref):
    step = pl.program_id(0)

    @pl.when(step == 0)
    def _():
        base_ref[...] = jnp.zeros_like(base_ref)

    lt = lg_ref[...].T + bias_ref[...]
    n_tok = lt.shape[1]
    le = lt[0:N_EXPERTS]
    lgrp = lt[N_EXPERTS:N_EXPERTS + N_GROUPS]
    gi = lax.broadcasted_iota(jnp.int32, (N_GROUPS, n_tok), 0)
    gmax, gsel = _first_index_of_max(lgrp, gi, N_GROUPS)
    pg_sel = 1.0 / jnp.sum(jnp.exp(lgrp - gmax), axis=0, keepdims=True)
    le_sel = jnp.zeros((EXPERTS_PER_GROUP, n_tok), F32)
    for g in range(N_GROUPS):
        le_sel = jnp.where(gsel == g, le[g * EXPERTS_PER_GROUP:(g + 1) * EXPERTS_PER_GROUP], le_sel)
    ei = lax.broadcasted_iota(jnp.int32, (EXPERTS_PER_GROUP, n_tok), 0)
    v0, i0 = _first_index_of_max(le_sel, ei, EXPERTS_PER_GROUP)
    rest = jnp.where(ei == i0, -jnp.inf, le_sel)
    v1, i1 = _first_index_of_max(rest, ei, EXPERTS_PER_GROUP)
    e1 = jnp.exp(v1 - v0)
    w0 = pg_sel / (1.0 + e1)
    w1 = pg_sel * e1 / (1.0 + e1)
    eid0 = gsel * EXPERTS_PER_GROUP + i0
    eid1 = gsel * EXPERTS_PER_GROUP + i1

    xi = lax.broadcasted_iota(jnp.int32, (N_EXPERTS, n_tok), 0)
    si = lax.broadcasted_iota(jnp.int32, (n_tok, n_tok), 0)
    ti = lax.broadcasted_iota(jnp.int32, (n_tok, n_tok), 1)
    before = (si < ti).astype(BF16)
    base = base_ref[...]
    hot0 = (xi == eid0).astype(F32)
    hot1 = (xi == eid1).astype(F32)
    pre0 = _dot(hot0.astype(BF16), before)
    pre1 = _dot(hot1.astype(BF16), before)
    tot0 = jnp.sum(hot0, axis=1, keepdims=True)
    tot1 = jnp.sum(hot1, axis=1, keepdims=True)
    rank0 = jnp.sum(hot0 * (base + pre0), axis=0, keepdims=True)
    rank1 = jnp.sum(hot1 * (base + tot0 + pre1), axis=0, keepdims=True)
    base = base + tot0 + tot1
    base_ref[...] = base

    ri = lax.broadcasted_iota(jnp.int32, (8, n_tok), 0)
    pick = lambda a, b: jnp.where(ri == 0, a, jnp.where(ri == 1, b, jnp.zeros_like(a)))
    eid_ref[...] = pick(eid0, eid1)
    gw_ref[...] = pick(w0, w1)
    rank_ref[...] = pick(rank0, rank1).astype(jnp.int32)
    cnt_ref[...] = jnp.broadcast_to(base, cnt_ref.shape).astype(jnp.int32)


def _route_call(logits, bias_col):
    t = logits.shape[0]
    tok = pl.BlockSpec((8, ROUTE_TM), lambda i: (0, i))
    return pl.pallas_call(
        _route_kernel,
        grid=(t // ROUTE_TM,),
        in_specs=[pl.BlockSpec((ROUTE_TM, LOGIT_PAD), lambda i: (i, 0)),
                  pl.BlockSpec((LOGIT_PAD, 1), lambda i: (0, 0))],
        out_specs=[tok, tok, tok, pl.BlockSpec((N_EXPERTS, 128), lambda i: (0, 0))],
        out_shape=[jax.ShapeDtypeStruct((8, t), jnp.int32),
                   jax.ShapeDtypeStruct((8, t), F32),
                   jax.ShapeDtypeStruct((8, t), jnp.int32),
                   jax.ShapeDtypeStruct((N_EXPERTS, 128), jnp.int32)],
        scratch_shapes=[pltpu.VMEM((N_EXPERTS, 1), F32)],
        compiler_params=_cparams(("arbitrary",)),
        name="route",
    )(logits, bias_col)


def _moe_layout(n_pairs):
    padded_rows = -(-(n_pairs + N_EXPERTS * (MOE_ROW_BLOCK - 1)) // MOE_ROW_BLOCK) * MOE_ROW_BLOCK
    n_items = -(-padded_rows // MOE_ROW_GROUP) + N_EXPERTS
    return padded_rows, n_items


def _routing_tables(eid, rank, counts, n_items):
    padded = (counts + MOE_ROW_BLOCK - 1) // MOE_ROW_BLOCK * MOE_ROW_BLOCK
    pad_end = jnp.cumsum(padded)
    pad_start = pad_end - padded
    hot = eid[..., None] == jnp.arange(N_EXPERTS, dtype=jnp.int32)
    dest = (jnp.sum(jnp.where(hot, pad_start, 0), axis=-1) + rank).astype(jnp.int32)
    tail = jnp.where(padded > counts, pad_end - MOE_ROW_BLOCK, -1).astype(jnp.int32)
    per_expert = (padded + MOE_ROW_GROUP - 1) // MOE_ROW_GROUP
    item_end = jnp.cumsum(per_expert)
    item_start = item_end - per_expert
    total = item_end[-1]
    ii = jnp.arange(n_items, dtype=jnp.int32)
    e_of = jnp.minimum(jnp.searchsorted(item_end, ii, side="right"), N_EXPERTS - 1).astype(jnp.int32)
    valid = ii < total
    e_last = e_of[jnp.maximum(total - 1, 0)]
    local = ii - item_start[e_of]
    row0 = pad_start[e_of] + local * MOE_ROW_GROUP
    nblk = jnp.clip((padded[e_of] - local * MOE_ROW_GROUP) // MOE_ROW_BLOCK, 0, MOE_ROW_GROUP // MOE_ROW_BLOCK)
    item_e = jnp.where(valid, e_of, e_last).astype(jnp.int32)
    item_row0 = jnp.where(valid, row0, 0).astype(jnp.int32)
    item_nblk = jnp.where(valid, nblk, 0).astype(jnp.int32)
    return dest, tail, item_e, item_row0, item_nblk


DISPATCH_TOKENS = 256


def _row_copy(src, s, dst, d, sem):
    return pltpu.make_async_copy(src.at[pl.ds(s, 1), :], dst.at[pl.ds(d, 1), :], sem)


def _dispatch_kernel(dest_ref, tail_ref, h_ctx, h_lat, xs, zero_buf, sem, *, n_ctx, n_tok):
    step = pl.program_id(0)
    tail_copy = lambda e: pltpu.make_async_copy(
        zero_buf, xs.at[pl.ds(pl.multiple_of(tail_ref[e], MOE_ROW_BLOCK), MOE_ROW_BLOCK), :], sem.at[1])

    @pl.when(step == 0)
    def _():
        zero_buf[...] = jnp.zeros_like(zero_buf)
        for e in range(N_EXPERTS):
            @pl.when(tail_ref[e] >= 0)
            def _():
                tail_copy(e).start()
        for e in range(N_EXPERTS):
            @pl.when(tail_ref[e] >= 0)
            def _():
                tail_copy(e).wait()

    tok0 = step * DISPATCH_TOKENS

    def scatter(src):
        def issue(t, carry):
            _row_copy(src, t, xs, dest_ref[tok0 + t], sem.at[0]).start()
            _row_copy(src, t, xs, dest_ref[n_tok + tok0 + t], sem.at[0]).start()
            return carry

        def drain(t, carry):
            _row_copy(src, 0, xs, 0, sem.at[0]).wait()
            _row_copy(src, 0, xs, 0, sem.at[0]).wait()
            return carry

        lax.fori_loop(0, DISPATCH_TOKENS, issue, 0)
        lax.fori_loop(0, DISPATCH_TOKENS, drain, 0)

    @pl.when(tok0 < n_ctx)
    def _():
        scatter(h_ctx)

    @pl.when(tok0 >= n_ctx)
    def _():
        scatter(h_lat)


def _dispatch_call(dest_flat, tail, h_ctx, h_lat, padded_rows):
    n_ctx, d = h_ctx.shape
    n_tok = n_ctx + h_lat.shape[0]
    ctx_tiles = n_ctx // DISPATCH_TOKENS
    return pl.pallas_call(
        functools.partial(_dispatch_kernel, n_ctx=n_ctx, n_tok=n_tok),
        grid_spec=pltpu.PrefetchScalarGridSpec(
            num_scalar_prefetch=2,
            grid=(n_tok // DISPATCH_TOKENS,),
            in_specs=[pl.BlockSpec((DISPATCH_TOKENS, d), lambda i, dr, tr: (jnp.minimum(i, ctx_tiles - 1), 0)),
                      pl.BlockSpec((DISPATCH_TOKENS, d), lambda i, dr, tr: (jnp.maximum(i - ctx_tiles, 0), 0))],
            out_specs=pl.BlockSpec(memory_space=pl.ANY),
            scratch_shapes=[pltpu.VMEM((MOE_ROW_BLOCK, d), F32), pltpu.SemaphoreType.DMA((2,))]),
        out_shape=jax.ShapeDtypeStruct((padded_rows, d), F32),
        compiler_params=_cparams(("arbitrary",)),
        name="dispatch",
    )(dest_flat, tail, h_ctx, h_lat)


def _moe_kernel(item_e, item_row0, item_nblk, xs, w1_ref, w3_ref, w2_ref, ys,
                x_bf, acc, w1_bf, w3_bf, w2_bf, sem):
    i = pl.program_id(0)
    c = pl.program_id(1)
    nblk = item_nblk[i]
    row0 = pl.multiple_of(item_row0[i], MOE_ROW_BLOCK)
    max_blk = MOE_ROW_GROUP // MOE_ROW_BLOCK
    blk = lambda b: pl.ds(b * MOE_ROW_BLOCK, MOE_ROW_BLOCK)
    load = lambda b: pltpu.make_async_copy(
        xs.at[pl.ds(row0 + b * MOE_ROW_BLOCK, MOE_ROW_BLOCK), :], acc.at[blk(b), :], sem.at[0])
    store = lambda b: pltpu.make_async_copy(
        acc.at[blk(b), :], ys.at[pl.ds(row0 + b * MOE_ROW_BLOCK, MOE_ROW_BLOCK), :], sem.at[1])

    @pl.when(nblk > 0)
    def _():
        @pl.when(c == 0)
        def _():
            for b in range(max_blk):
                @pl.when(b < nblk)
                def _():
                    load(b).start()
            for b in range(max_blk):
                @pl.when(b < nblk)
                def _():
                    load(b).wait()
            for b in range(max_blk):
                @pl.when(b < nblk)
                def _():
                    x_bf[blk(b), :] = acc[blk(b), :].astype(BF16)
                    acc[blk(b), :] = jnp.zeros((MOE_ROW_BLOCK, acc.shape[1]), F32)

        w1_bf[...] = w1_ref[0].astype(BF16)
        w3_bf[...] = w3_ref[0].astype(BF16)
        w2_bf[...] = w2_ref[0].astype(BF16)

        def body(b, carry):
            rows = pl.ds(pl.multiple_of(b * MOE_ROW_BLOCK, MOE_ROW_BLOCK), MOE_ROW_BLOCK)
            x = x_bf[rows, :]
            h1 = _dot(x, w1_bf[...])
            h3 = _dot(x, w3_bf[...])
            a = (h1 * jax.nn.sigmoid(h1) * h3).astype(BF16)
            acc[rows, :] += _dot(a, w2_bf[...])
            return carry

        lax.fori_loop(0, nblk, body, 0)

        @pl.when(c == pl.num_programs(1) - 1)
        def _():
            for b in range(max_blk):
                @pl.when(b < nblk)
                def _():
                    store(b).start()
            for b in range(max_blk):
                @pl.when(b < nblk)
                def _():
                    store(b).wait()


def _moe_call(item_e, item_row0, item_nblk, xs, w1, w3, w2):
    padded_rows, d = xs.shape
    n_items = item_e.shape[0]
    f = w1.shape[2]
    nc = f // MOE_F_CHUNK
    return pl.pallas_call(
        _moe_kernel,
        grid_spec=pltpu.PrefetchScalarGridSpec(
            num_scalar_prefetch=3,
            grid=(n_items, nc),
            in_specs=[pl.BlockSpec(memory_space=pl.ANY),
                      pl.BlockSpec((1, d, MOE_F_CHUNK), lambda i, c, ie, ir, ib: (ie[i], 0, c)),
                      pl.BlockSpec((1, d, MOE_F_CHUNK), lambda i, c, ie, ir, ib: (ie[i], 0, c)),
                      pl.BlockSpec((1, MOE_F_CHUNK, d), lambda i, c, ie, ir, ib: (ie[i], c, 0))],
            out_specs=pl.BlockSpec(memory_space=pl.ANY),
            scratch_shapes=[pltpu.VMEM((MOE_ROW_GROUP, d), BF16),
                            pltpu.VMEM((MOE_ROW_GROUP, d), F32),
                            pltpu.VMEM((d, MOE_F_CHUNK), BF16),
                            pltpu.VMEM((d, MOE_F_CHUNK), BF16),
                            pltpu.VMEM((MOE_F_CHUNK, d), BF16),
                            pltpu.SemaphoreType.DMA((2,))]),
        out_shape=jax.ShapeDtypeStruct((padded_rows, d), F32),
        compiler_params=_cparams(("arbitrary", "arbitrary")),
        name="moe",
    )(item_e, item_row0, item_nblk, xs, w1, w3, w2)


COMBINE_TM = 256


def _combine_kernel(dest_ref, x1_ref, gw_ref, g2_ref, ys, o_ref, y0, y1, sem, *, tok_base, n_tok):
    tok0 = tok_base + pl.program_id(0) * COMBINE_TM

    def issue(t, carry):
        _row_copy(ys, dest_ref[tok0 + t], y0, t, sem).start()
        _row_copy(ys, dest_ref[n_tok + tok0 + t], y1, t, sem).start()
        return carry

    def drain(t, carry):
        _row_copy(ys, 0, y0, 0, sem).wait()
        _row_copy(ys, 0, y1, 0, sem).wait()
        return carry

    lax.fori_loop(0, COMBINE_TM, issue, 0)
    lax.fori_loop(0, COMBINE_TM, drain, 0)
    gw = gw_ref[...]
    moe = gw[:, 0:1] * y0[...] + gw[:, 1:2] * y1[...]
    o_ref[...] = x1_ref[...] + g2_ref[0] * moe


def _combine_call(dest_flat, x1, gw, g2, ys, tok_base, n_tok, tiles_per_group):
    m, d = x1.shape
    return pl.pallas_call(
        functools.partial(_combine_kernel, tok_base=tok_base, n_tok=n_tok),
        grid_spec=pltpu.PrefetchScalarGridSpec(
            num_scalar_prefetch=1,
            grid=(m // COMBINE_TM,),
            in_specs=[pl.BlockSpec((COMBINE_TM, d), lambda i, dr: (i, 0)),
                      pl.BlockSpec((COMBINE_TM, 2), lambda i, dr: (i, 0)),
                      pl.BlockSpec((1, 1, d), lambda i, dr: (i // tiles_per_group, 0, 0)),
                      pl.BlockSpec(memory_space=pl.ANY)],
            out_specs=pl.BlockSpec((COMBINE_TM, d), lambda i, dr: (i, 0)),
            scratch_shapes=[pltpu.VMEM((COMBINE_TM, d), F32), pltpu.VMEM((COMBINE_TM, d), F32),
                            pltpu.SemaphoreType.DMA]),
        out_shape=jax.ShapeDtypeStruct((m, d), F32),
        compiler_params=_cparams(("arbitrary",)),
        name="combine",
    )(dest_flat, x1, gw, g2, ys)


def kernel(x_prompt, x_sample, cache_a_k, cache_a_v, cache_b_k, cache_b_v, c, c_ctx, norm1_w, norm2_w, w_ada, b_ada, w_in, qn_a, kn_a, qn_b, kn_b, rpb_a, sink_b, w_pa, w_pb, w_out, w_rg, b_rg, w_re, b_re, w1, w3, w2):
    batch, seq, d = x_prompt.shape
    dec_batch, dec_seq, _ = x_sample.shape
    depth = norm1_w.shape[0]
    assert depth == 1, "one trunk layer"
    past = cache_a_k.shape[2]
    n_ctx, n_lat = batch * seq, dec_batch * dec_seq
    n_tok = n_ctx + n_lat

    xc = x_prompt.reshape(n_ctx, d)
    xl = x_sample.reshape(n_lat, d)

    n_rows = 8
    cvec = jnp.zeros((n_rows, d), F32).at[0].set(c_ctx).at[1:1 + dec_batch].set(c)
    mod = _ada_call(cvec, w_ada[0], b_ada[0][None, :])
    sh1, sc1, g1, sh2, sc2, g2 = [mod[:, i * d:(i + 1) * d][:, None, :] for i in range(6)]
    ctx_rows, lat_rows = slice(0, 1), slice(1, 1 + dec_batch)

    nw1, nw2 = norm1_w[0][None, :], norm2_w[0][None, :]
    qna, kna, qnb, knb = qn_a[0][None, :], kn_a[0][None, :], qn_b[0][None, :], kn_b[0][None, :]
    sink = sink_b[0]
    w_in_bf = w_in[0].astype(BF16)
    wpa_bf, wpb_bf, wout_bf = w_pa[0].astype(BF16), w_pb[0].astype(BF16), w_out[0].astype(BF16)
    wr = jnp.zeros((d, LOGIT_PAD), F32).at[:, :N_EXPERTS].set(w_re[0]).at[:, N_EXPERTS:N_EXPERTS + N_GROUPS].set(w_rg[0])
    br = jnp.zeros((LOGIT_PAD, 1), F32).at[:N_EXPERTS, 0].set(b_re[0]).at[N_EXPERTS:N_EXPERTS + N_GROUPS, 0].set(b_rg[0])

    proj_c = _inproj_call(xc, nw1, sc1[ctx_rows], sh1[ctx_rows], w_in_bf, n_ctx // INPROJ_TM)
    proj_l = _inproj_call(xl, nw1, sc1[lat_rows], sh1[lat_rows], w_in_bf, dec_seq // INPROJ_TM)

    oa_c, ob_c, new_a_k, new_a_v, new_b_k, new_b_v = _ctx_attn_call(proj_c, seq, qna, kna, qnb, knb, sink)

    bias = _na_bias(rpb_a[0], dec_seq)
    cos, sin_a, sin_b = _rope_tables(dec_seq)
    cak = cache_a_k[:, 0].reshape(dec_batch, past, NA_WIDTH)
    cav = cache_a_v[:, 0].reshape(dec_batch, past, NA_WIDTH)
    cbk = cache_b_k[:, 0].reshape(dec_batch, past, NB_KV_WIDTH)
    cbv = cache_b_v[:, 0].reshape(dec_batch, past, NB_KV_WIDTH)
    oa_l = _lat_attn_a_call(proj_l, dec_seq, cak, cav, bias, qna, kna)
    ob_l = _lat_attn_b_call(proj_l, dec_seq, cbk, cbv, cos, sin_a, sin_b, qnb, knb, sink)

    x1_c, h2_c, lg_c = _merge_call(oa_c, ob_c, proj_c, wpa_bf, wpb_bf, wout_bf, xc,
                                   g1[ctx_rows], sc2[ctx_rows], sh2[ctx_rows], nw2, wr, n_ctx // MERGE_TM)
    x1_l, h2_l, lg_l = _merge_call(oa_l, ob_l, proj_l, wpa_bf, wpb_bf, wout_bf, xl,
                                   g1[lat_rows], sc2[lat_rows], sh2[lat_rows], nw2, wr, dec_seq // MERGE_TM)

    eid, gw, rank, cnt = _route_call(jnp.concatenate([lg_c, lg_l], axis=0), br)
    padded_rows, n_items = _moe_layout(2 * n_tok)
    dest, tail, item_e, item_row0, item_nblk = _routing_tables(eid[:2], rank[:2], cnt[:, 0], n_items)
    dest_flat = dest.reshape(-1)
    xs = _dispatch_call(dest_flat, tail, h2_c, h2_l, padded_rows)
    ys = _moe_call(item_e, item_row0, item_nblk, xs, w1[0], w3[0], w2[0])
    gw_t = gw[:2].T
    y_c = _combine_call(dest_flat, x1_c, gw_t[:n_ctx], g2[ctx_rows], ys, 0, n_tok, n_ctx // COMBINE_TM)
    y_l = _combine_call(dest_flat, x1_l, gw_t[n_ctx:], g2[lat_rows], ys, n_ctx, n_tok, dec_seq // COMBINE_TM)

    state = lambda a, heads: a.reshape(batch, 1, seq, heads, HEAD_DIM)
    return (y_c.reshape(batch, seq, d), y_l.reshape(dec_batch, dec_seq, d),
            state(new_a_k, NA_HEADS), state(new_a_v, NA_HEADS),
            state(new_b_k, NB_KV_HEADS), state(new_b_v, NB_KV_HEADS))
```

```python
import functools

import jax
import jax.numpy as jnp
import numpy as np
from jax import lax
from jax.experimental import pallas as pl
from jax.experimental.pallas import tpu as pltpu

D_MODEL = 2048
HEAD_DIM = 128
NA_HEADS = 8
NA_WIDTH = NA_HEADS * HEAD_DIM
NB_Q_HEADS = 8
NB_KV_HEADS = 2
NB_GROUP = NB_Q_HEADS // NB_KV_HEADS
NB_WIDTH = NB_Q_HEADS * HEAD_DIM
NB_KV_WIDTH = NB_KV_HEADS * HEAD_DIM
GRID_W = 64
NA_WIN_ROWS = 8
NA_WIN_COLS = 16
WINDOW = 128
N_GROUPS = 4
EXPERTS_PER_GROUP = 8
N_EXPERTS = N_GROUPS * EXPERTS_PER_GROUP
D_EXPERT = 1024
IN_WIDTH = 3 * NA_WIDTH + NB_WIDTH + 2 * NB_KV_WIDTH + 2 * D_MODEL
ROPE_BASE = 10000.0
NORM_EPS = 1e-6
NEG_INF = -1e30
ATTN_SCALE = HEAD_DIM ** -0.5

QA_HEAD0 = 0
KA_HEAD0 = NA_HEADS
VA_HEAD0 = 2 * NA_HEADS
QB_HEAD0 = 3 * NA_HEADS
KB_HEAD0 = QB_HEAD0 + NB_Q_HEADS
VB_HEAD0 = KB_HEAD0 + NB_KV_HEADS
GATE_COL0 = (VB_HEAD0 + NB_KV_HEADS) * HEAD_DIM

LOGIT_PAD = 128
MOE_ROW_BLOCK = 256
MOE_ROW_GROUP = 1024
MOE_F_CHUNK = 512
VMEM_LIMIT = 56 * 1024 * 1024

F32 = jnp.float32
BF16 = jnp.bfloat16


def _cparams(sem):
    return pltpu.CompilerParams(dimension_semantics=sem, vmem_limit_bytes=VMEM_LIMIT)


def _rms(x, w):
    return x * lax.rsqrt(jnp.mean(x * x, axis=-1, keepdims=True) + NORM_EPS) * w


def _dot(a, b):
    return jnp.dot(a, b, preferred_element_type=F32)


def _dot_nt(a, b):
    return lax.dot_general(a, b, (((1,), (1,)), ((), ())), preferred_element_type=F32)


def _dot_split(a, b):
    a_hi = a.astype(BF16)
    a_lo = (a - a_hi.astype(F32)).astype(BF16)
    b_hi = b.astype(BF16)
    b_lo = (b - b_hi.astype(F32)).astype(BF16)
    return _dot(a_hi, b_hi) + (_dot(a_lo, b_hi) + _dot(a_hi, b_lo))


def _ada_kernel(c_ref, w_ref, b_ref, o_ref):
    c = c_ref[...]
    s = c * jax.nn.sigmoid(c)
    o_ref[...] = _dot_split(s, w_ref[...]) + b_ref[...]


def _ada_call(cvec, w_ada, b_ada):
    rows, d = cvec.shape
    n = w_ada.shape[1]
    tn = 1024
    return pl.pallas_call(
        _ada_kernel,
        grid=(n // tn,),
        in_specs=[pl.BlockSpec((rows, d), lambda j: (0, 0)),
                  pl.BlockSpec((d, tn), lambda j: (0, j)),
                  pl.BlockSpec((1, tn), lambda j: (0, j))],
        out_specs=pl.BlockSpec((rows, tn), lambda j: (0, j)),
        out_shape=jax.ShapeDtypeStruct((rows, n), F32),
        compiler_params=_cparams(("arbitrary",)),
        name="ada",
    )(cvec, w_ada, b_ada)


INPROJ_TM = 1024
INPROJ_TN = 512
NORM_ROWS = 128


def _inproj_kernel(x_ref, nw_ref, sc_ref, sh_ref, w_ref, qkv_ref, gate_ref, h_scr, *, qkv_tiles):
    j = pl.program_id(1)

    @pl.when(j == 0)
    def _():
        nw = nw_ref[...]
        sc = 1.0 + sc_ref[0]
        sh = sh_ref[0]

        def body(r, carry):
            rows = pl.ds(pl.multiple_of(r * NORM_ROWS, NORM_ROWS), NORM_ROWS)
            h_scr[rows, :] = (_rms(x_ref[rows, :], nw) * sc + sh).astype(BF16)
            return carry

        lax.fori_loop(0, INPROJ_TM // NORM_ROWS, body, 0)

    res = _dot(h_scr[...], w_ref[...])

    @pl.when(j < qkv_tiles)
    def _():
        qkv_ref[...] = res

    @pl.when(j >= qkv_tiles)
    def _():
        gate_ref[...] = res.astype(gate_ref.dtype)


def _inproj_call(x, nw, sc, sh, w_bf, tiles_per_group):
    m, d = x.shape
    n = w_bf.shape[1]
    qkv_tiles = GATE_COL0 // INPROJ_TN
    grp = lambda i, j: (i // tiles_per_group, 0, 0)
    return pl.pallas_call(
        functools.partial(_inproj_kernel, qkv_tiles=qkv_tiles),
        grid=(m // INPROJ_TM, n // INPROJ_TN),
        in_specs=[pl.BlockSpec((INPROJ_TM, d), lambda i, j: (i, 0)),
                  pl.BlockSpec((1, d), lambda i, j: (0, 0)),
                  pl.BlockSpec((1, 1, d), grp),
                  pl.BlockSpec((1, 1, d), grp),
                  pl.BlockSpec((d, INPROJ_TN), lambda i, j: (0, j))],
        out_specs=[pl.BlockSpec((INPROJ_TM, INPROJ_TN), lambda i, j: (i, jnp.minimum(j, qkv_tiles - 1))),
                   pl.BlockSpec((INPROJ_TM, INPROJ_TN), lambda i, j: (i, jnp.maximum(j - qkv_tiles, 0)))],
        out_shape=[jax.ShapeDtypeStruct((m, GATE_COL0), F32),
                   jax.ShapeDtypeStruct((m, n - GATE_COL0), BF16)],
        scratch_shapes=[pltpu.VMEM((INPROJ_TM, d), BF16)],
        compiler_params=_cparams(("arbitrary", "arbitrary")),
        name="inproj",
    )(x, nw, sc, sh, w_bf)


def _softmax_pv(scores, values, sink=None):
    m = None
    for s in scores:
        ms = jnp.max(s, axis=-1, keepdims=True)
        m = ms if m is None else jnp.maximum(m, ms)
    if sink is not None:
        m = jnp.maximum(m, sink)
    den = None
    acc = None
    for s, v in zip(scores, values):
        e = jnp.exp(s - m)
        ds = jnp.sum(e, axis=-1, keepdims=True)
        den = ds if den is None else den + ds
        pv = _dot(e.astype(BF16), v)
        acc = pv if acc is None else acc + pv
    if sink is not None:
        den = den + jnp.exp(sink - m)
    return acc / den


def _ctx_attn_kernel(qa_ref, ka_ref, va_ref, qb_ref, kb_ref, vb_ref,
                     qna_ref, kna_ref, qnb_ref, knb_ref, sink_ref,
                     oa_ref, ob_ref, nak_ref, nav_ref, nbk_ref, nbv_ref):
    qna, kna, qnb, knb = qna_ref[...], kna_ref[...], qnb_ref[...], knb_ref[...]
    for h in range(NA_HEADS):
        cols = slice(h * HEAD_DIM, (h + 1) * HEAD_DIM)
        q = (_rms(qa_ref[:, cols], qna) * ATTN_SCALE).astype(BF16)
        k = _rms(ka_ref[:, cols], kna)
        v = va_ref[:, cols]
        nak_ref[:, cols] = k
        nav_ref[:, cols] = v
        s = _dot_nt(q, k.astype(BF16))
        oa_ref[:, cols] = _softmax_pv([s], [v.astype(BF16)]).astype(oa_ref.dtype)
    for kv in range(NB_KV_HEADS):
        kcols = slice(kv * HEAD_DIM, (kv + 1) * HEAD_DIM)
        k = _rms(kb_ref[:, kcols], knb)
        v = vb_ref[:, kcols]
        nbk_ref[:, kcols] = k
        nbv_ref[:, kcols] = v
        kb16 = k.astype(BF16)
        vb16 = v.astype(BF16)
        for g in range(NB_GROUP):
            hq = kv * NB_GROUP + g
            cols = slice(hq * HEAD_DIM, (hq + 1) * HEAD_DIM)
            q = (_rms(qb_ref[:, cols], qnb) * ATTN_SCALE).astype(BF16)
            s = _dot_nt(q, kb16)
            ob_ref[:, cols] = _softmax_pv([s], [vb16], sink=sink_ref[hq]).astype(ob_ref.dtype)


def _ctx_attn_call(proj, seq, qna, kna, qnb, knb, sink):
    m = proj.shape[0]
    nb = m // seq
    wide = lambda blk: pl.BlockSpec((seq, NA_WIDTH), lambda b: (b, blk))
    narrow = lambda blk: pl.BlockSpec((seq, NB_KV_WIDTH), lambda b: (b, blk))
    vec = pl.BlockSpec((1, HEAD_DIM), lambda b: (0, 0))
    return pl.pallas_call(
        _ctx_attn_kernel,
        grid=(nb,),
        in_specs=[wide(QA_HEAD0 // NA_HEADS), wide(KA_HEAD0 // NA_HEADS), wide(VA_HEAD0 // NA_HEADS),
                  wide(QB_HEAD0 // NA_HEADS), narrow(KB_HEAD0 // NB_KV_HEADS), narrow(VB_HEAD0 // NB_KV_HEADS),
                  vec, vec, vec, vec,
                  pl.BlockSpec(memory_space=pltpu.SMEM)],
        out_specs=[pl.BlockSpec((seq, NA_WIDTH), lambda b: (b, 0)),
                   pl.BlockSpec((seq, NB_WIDTH), lambda b: (b, 0)),
                   pl.BlockSpec((seq, NA_WIDTH), lambda b: (b, 0)),
                   pl.BlockSpec((seq, NA_WIDTH), lambda b: (b, 0)),
                   pl.BlockSpec((seq, NB_KV_WIDTH), lambda b: (b, 0)),
                   pl.BlockSpec((seq, NB_KV_WIDTH), lambda b: (b, 0))],
        out_shape=[jax.ShapeDtypeStruct((m, NA_WIDTH), BF16),
                   jax.ShapeDtypeStruct((m, NB_WIDTH), BF16),
                   jax.ShapeDtypeStruct((m, NA_WIDTH), F32),
                   jax.ShapeDtypeStruct((m, NA_WIDTH), F32),
                   jax.ShapeDtypeStruct((m, NB_KV_WIDTH), F32),
                   jax.ShapeDtypeStruct((m, NB_KV_WIDTH), F32)],
        compiler_params=_cparams(("arbitrary",)),
        name="ctx_attn",
    )(proj, proj, proj, proj, proj, proj, qna, kna, qnb, knb, sink)


def _rope(x, cos, sin_a, sin_b):
    quarter = HEAD_DIM // 4
    return (x * cos + pltpu.roll(x, HEAD_DIM - quarter, 1) * sin_a
            + pltpu.roll(x, quarter, 1) * sin_b)


def _lat_attn_a_kernel(q_ref, k_ref, v_ref, ck_ref, cv_ref, bias_ref, qn_ref, kn_ref, o_ref):
    q = (_rms(q_ref[...], qn_ref[...]) * ATTN_SCALE).astype(BF16)
    k = _rms(k_ref[...], kn_ref[...]).astype(BF16)
    s_lat = _dot_nt(q, k) + bias_ref[...]
    s_ctx = _dot_nt(q, ck_ref[...].astype(BF16))
    o = _softmax_pv([s_lat, s_ctx], [v_ref[...].astype(BF16), cv_ref[...].astype(BF16)])
    o_ref[...] = o.astype(o_ref.dtype)


def _lat_attn_b_kernel(q_ref, k_ref, v_ref, ck_ref, cv_ref, cos_ref, sina_ref, sinb_ref,
                       qn_ref, kn_ref, sink_ref, o_ref):
    cos, sin_a, sin_b = cos_ref[...], sina_ref[...], sinb_ref[...]
    q = _rope(_rms(q_ref[...], qn_ref[...]), cos, sin_a, sin_b)
    k = _rope(_rms(k_ref[...], kn_ref[...]), cos, sin_a, sin_b)
    q = (q * ATTN_SCALE).astype(BF16)
    s_lat = _dot_nt(q, k.astype(BF16))
    length = s_lat.shape[0]
    qi = lax.broadcasted_iota(jnp.int32, (length, length), 0)
    kj = lax.broadcasted_iota(jnp.int32, (length, length), 1)
    s_lat = jnp.where(jnp.abs(qi - kj) <= WINDOW, s_lat, NEG_INF)
    s_ctx = _dot_nt(q, ck_ref[...].astype(BF16))
    o = _softmax_pv([s_lat, s_ctx], [v_ref[...].astype(BF16), cv_ref[...].astype(BF16)],
                    sink=sink_ref[pl.program_id(0)])
    o_ref[...] = o.astype(o_ref.dtype)


def _lat_attn_a_call(proj, length, ck, cv, bias, qn, kn):
    m = proj.shape[0]
    past = ck.shape[1]
    head = lambda h0: pl.BlockSpec((length, HEAD_DIM), lambda h, b: (b, h0 + h))
    cache = pl.BlockSpec((None, past, HEAD_DIM), lambda h, b: (b, 0, h))
    vec = pl.BlockSpec((1, HEAD_DIM), lambda h, b: (0, 0))
    return pl.pallas_call(
        _lat_attn_a_kernel,
        grid=(NA_HEADS, m // length),
        in_specs=[head(QA_HEAD0), head(KA_HEAD0), head(VA_HEAD0), cache, cache,
                  pl.BlockSpec((None, length, length), lambda h, b: (h, 0, 0)), vec, vec],
        out_specs=pl.BlockSpec((length, HEAD_DIM), lambda h, b: (b, h)),
        out_shape=jax.ShapeDtypeStruct((m, NA_WIDTH), BF16),
        compiler_params=_cparams(("arbitrary", "arbitrary")),
        name="lat_attn_a",
    )(proj, proj, proj, ck, cv, bias, qn, kn)


def _lat_attn_b_call(proj, length, ck, cv, cos, sin_a, sin_b, qn, kn, sink):
    m = proj.shape[0]
    past = ck.shape[1]
    qspec = pl.BlockSpec((length, HEAD_DIM), lambda h, b: (b, QB_HEAD0 + h))
    kvspec = lambda h0: pl.BlockSpec((length, HEAD_DIM), lambda h, b: (b, h0 + h // NB_GROUP))
    cache = pl.BlockSpec((None, past, HEAD_DIM), lambda h, b: (b, 0, h // NB_GROUP))
    table = pl.BlockSpec((length, HEAD_DIM), lambda h, b: (0, 0))
    vec = pl.BlockSpec((1, HEAD_DIM), lambda h, b: (0, 0))
    return pl.pallas_call(
        _lat_attn_b_kernel,
        grid=(NB_Q_HEADS, m // length),
        in_specs=[qspec, kvspec(KB_HEAD0), kvspec(VB_HEAD0), cache, cache,
                  table, table, table, vec, vec, pl.BlockSpec(memory_space=pltpu.SMEM)],
        out_specs=pl.BlockSpec((length, HEAD_DIM), lambda h, b: (b, h)),
        out_shape=jax.ShapeDtypeStruct((m, NB_WIDTH), BF16),
        compiler_params=_cparams(("arbitrary", "arbitrary")),
        name="lat_attn_b",
    )(proj, proj, proj, ck, cv, cos, sin_a, sin_b, qn, kn, sink)


def _na_bias(rpb, length):
    rows = length // GRID_W
    kr_n = min(NA_WIN_ROWS, rows)
    r = np.arange(rows)
    c = np.arange(GRID_W)
    r0 = np.clip(r - kr_n // 2, 0, rows - kr_n)
    c0 = np.clip(c - NA_WIN_COLS // 2, 0, GRID_W - NA_WIN_COLS)
    row_ok = (r[None, :] >= r0[:, None]) & (r[None, :] < r0[:, None] + kr_n)
    col_ok = (c[None, :] >= c0[:, None]) & (c[None, :] < c0[:, None] + NA_WIN_COLS)
    dr = r[None, :] - r[:, None] + (NA_WIN_ROWS - 1)
    dc = np.clip(c[None, :] - c[:, None], -(NA_WIN_COLS - 1), NA_WIN_COLS - 1) + (NA_WIN_COLS - 1)
    row_sel = ((dr[None] == np.arange(2 * NA_WIN_ROWS - 1)[:, None, None]) & row_ok[None]).astype(np.float32)
    col_sel = ((dc[None] == np.arange(2 * NA_WIN_COLS - 1)[:, None, None]) & col_ok[None]).astype(np.float32)
    hi = lax.Precision.HIGHEST
    per_col = jnp.einsum("hde,eqk->hdqk", rpb.astype(F32), col_sel, precision=hi)
    table = jnp.einsum("drs,hdqk->hrqsk", row_sel, per_col, precision=hi)
    valid = row_ok[:, None, :, None] & col_ok[None, :, None, :]
    return jnp.where(valid[None], table, NEG_INF).reshape(rpb.shape[0], length, length)


def _rope_tables(length):
    t = jnp.arange(length)
    row = (t // GRID_W).astype(F32)
    col = (t % GRID_W).astype(F32)
    n_freq = HEAD_DIM // 4
    inv = ROPE_BASE ** (-jnp.arange(n_freq, dtype=F32) / n_freq)
    ar = row[:, None] * inv
    ac = col[:, None] * inv
    ang = jnp.concatenate([ar, ar, ac, ac], axis=-1)
    cos, sin = jnp.cos(ang), jnp.sin(ang)
    lane = jnp.arange(HEAD_DIM)
    takes_left = ((lane // n_freq) % 2 == 0)[None, :]
    return cos, jnp.where(takes_left, -sin, 0.0), jnp.where(takes_left, 0.0, sin)


MERGE_TM = 512


def _resident(shape):
    zeros = (0,) * len(shape)
    return pl.BlockSpec(shape, lambda i: zeros, pipeline_mode=pl.Buffered(1))


def _mix_kernel(oa_ref, ob_ref, ga_ref, gb_ref, wpa_ref, wpb_ref, mix_ref):
    ya = _dot(oa_ref[...], wpa_ref[...])
    yb = _dot(ob_ref[...], wpb_ref[...])
    mix = (jax.nn.sigmoid(ga_ref[...].astype(F32)) * ya
           + jax.nn.sigmoid(gb_ref[...].astype(F32)) * yb)
    mix_ref[...] = mix.astype(mix_ref.dtype)


def _mix_call(oa, ob, gates, wpa_bf, wpb_bf):
    m = oa.shape[0]
    d = wpa_bf.shape[1]
    row = lambda i: (i, 0)
    return pl.pallas_call(
        _mix_kernel,
        grid=(m // MERGE_TM,),
        in_specs=[pl.BlockSpec((MERGE_TM, NA_WIDTH), row),
                  pl.BlockSpec((MERGE_TM, NB_WIDTH), row),
                  pl.BlockSpec((MERGE_TM, d), lambda i: (i, 0)),
                  pl.BlockSpec((MERGE_TM, d), lambda i: (i, 1)),
                  _resident((NA_WIDTH, d)), _resident((NB_WIDTH, d))],
        out_specs=pl.BlockSpec((MERGE_TM, d), row),
        out_shape=jax.ShapeDtypeStruct((m, d), BF16),
        compiler_params=_cparams(("arbitrary",)),
        name="mix",
    )(oa, ob, gates, gates, wpa_bf, wpb_bf)


def _outproj_kernel(mix_ref, wout_ref, x_ref, g1_ref, sc2_ref, sh2_ref, n2w_ref, wr_ref,
                    x1_ref, h2_ref, lg_ref):
    x1 = x_ref[...] + g1_ref[0] * _dot(mix_ref[...], wout_ref[...])
    x1_ref[...] = x1
    h2 = _rms(x1, n2w_ref[...]) * (1.0 + sc2_ref[0]) + sh2_ref[0]
    h2_ref[...] = h2
    lg_ref[...] = _dot_split(h2, wr_ref[...])


def _outproj_call(mix, wout_bf, x, g1, sc2, sh2, n2w, wr, tiles_per_group):
    m, d = x.shape
    grp = lambda i: (i // tiles_per_group, 0, 0)
    row = lambda i: (i, 0)
    return pl.pallas_call(
        _outproj_kernel,
        grid=(m // MERGE_TM,),
        in_specs=[pl.BlockSpec((MERGE_TM, d), row),
                  _resident((d, d)),
                  pl.BlockSpec((MERGE_TM, d), row),
                  pl.BlockSpec((1, 1, d), grp), pl.BlockSpec((1, 1, d), grp), pl.BlockSpec((1, 1, d), grp),
                  pl.BlockSpec((1, d), lambda i: (0, 0)),
                  _resident((d, LOGIT_PAD))],
        out_specs=[pl.BlockSpec((MERGE_TM, d), row),
                   pl.BlockSpec((MERGE_TM, d), row),
                   pl.BlockSpec((MERGE_TM, LOGIT_PAD), row)],
        out_shape=[jax.ShapeDtypeStruct((m, d), F32),
                   jax.ShapeDtypeStruct((m, d), F32),
                   jax.ShapeDtypeStruct((m, LOGIT_PAD), F32)],
        compiler_params=_cparams(("arbitrary",)),
        name="outproj",
    )(mix, wout_bf, x, g1, sc2, sh2, n2w, wr)


ROUTE_TM = 512


def _first_index_of_max(vals, idx, n):
    mx = jnp.max(vals, axis=0, keepdims=True)
    first = jnp.min(jnp.where(vals == mx, idx, n), axis=0, keepdims=True)
    return mx, first


def _route_kernel(lg_ref, bias_ref, eid_ref, gw_ref, rank_ref, cnt_ref, base_ref):
    step = pl.program_id(0)

    @pl.when(step == 0)
    def _():
        base_ref[...] = jnp.zeros_like(base_ref)

    lt = lg_ref[...].T + bias_ref[...]
    n_tok = lt.shape[1]
    le = lt[0:N_EXPERTS]
    lgrp = lt[N_EXPERTS:N_EXPERTS + N_GROUPS]
    gi = lax.broadcasted_iota(jnp.int32, (N_GROUPS, n_tok), 0)
    gmax, gsel = _first_index_of_max(lgrp, gi, N_GROUPS)
    pg_sel = 1.0 / jnp.sum(jnp.exp(lgrp - gmax), axis=0, keepdims=True)
    le_sel = jnp.zeros((EXPERTS_PER_GROUP, n_tok), F32)
    for g in range(N_GROUPS):
        le_sel = jnp.where(gsel == g, le[g * EXPERTS_PER_GROUP:(g + 1) * EXPERTS_PER_GROUP], le_sel)
    ei = lax.broadcasted_iota(jnp.int32, (EXPERTS_PER_GROUP, n_tok), 0)
    v0, i0 = _first_index_of_max(le_sel, ei, EXPERTS_PER_GROUP)
    rest = jnp.where(ei == i0, -jnp.inf, le_sel)
    v1, i1 = _first_index_of_max(rest, ei, EXPERTS_PER_GROUP)
    e1 = jnp.exp(v1 - v0)
    w0 = pg_sel / (1.0 + e1)
    w1 = pg_sel * e1 / (1.0 + e1)
    eid0 = gsel * EXPERTS_PER_GROUP + i0
    eid1 = gsel * EXPERTS_PER_GROUP + i1

    xi = lax.broadcasted_iota(jnp.int32, (N_EXPERTS, n_tok), 0)
    si = lax.broadcasted_iota(jnp.int32, (n_tok, n_tok), 0)
    ti = lax.broadcasted_iota(jnp.int32, (n_tok, n_tok), 1)
    before = (si < ti).astype(BF16)
    base = base_ref[...]
    hot0 = (xi == eid0).astype(F32)
    hot1 = (xi == eid1).astype(F32)
    pre0 = _dot(hot0.astype(BF16), before)
    pre1 = _dot(hot1.astype(BF16), before)
    tot0 = jnp.sum(hot0, axis=1, keepdims=True)
    tot1 = jnp.sum(hot1, axis=1, keepdims=True)
    rank0 = jnp.sum(hot0 * (base + pre0), axis=0, keepdims=True)
    rank1 = jnp.sum(hot1 * (base + tot0 + pre1), axis=0, keepdims=True)
    base = base + tot0 + tot1
    base_ref[...] = base

    ri = lax.broadcasted_iota(jnp.int32, (8, n_tok), 0)
    pick = lambda a, b: jnp.where(ri == 0, a, jnp.where(ri == 1, b, jnp.zeros_like(a)))
    eid_ref[...] = pick(eid0, eid1)
    gw_ref[...] = pick(w0, w1)
    rank_ref[...] = pick(rank0, rank1).astype(jnp.int32)
    cnt_ref[...] = jnp.broadcast_to(base, cnt_ref.shape).astype(jnp.int32)


def _route_call(logits, bias_col):
    t = logits.shape[0]
    tok = pl.BlockSpec((8, ROUTE_TM), lambda i: (0, i))
    return pl.pallas_call(
        _route_kernel,
        grid=(t // ROUTE_TM,),
        in_specs=[pl.BlockSpec((ROUTE_TM, LOGIT_PAD), lambda i: (i, 0)),
                  pl.BlockSpec((LOGIT_PAD, 1), lambda i: (0, 0))],
        out_specs=[tok, tok, tok, pl.BlockSpec((N_EXPERTS, 128), lambda i: (0, 0))],
        out_shape=[jax.ShapeDtypeStruct((8, t), jnp.int32),
                   jax.ShapeDtypeStruct((8, t), F32),
                   jax.ShapeDtypeStruct((8, t), jnp.int32),
                   jax.ShapeDtypeStruct((N_EXPERTS, 128), jnp.int32)],
        scratch_shapes=[pltpu.VMEM((N_EXPERTS, 1), F32)],
        compiler_params=_cparams(("arbitrary",)),
        name="route",
    )(logits, bias_col)


def _moe_layout(n_pairs):
    padded_rows = -(-(n_pairs + N_EXPERTS * (MOE_ROW_BLOCK - 1)) // MOE_ROW_BLOCK) * MOE_ROW_BLOCK
    n_items = -(-padded_rows // MOE_ROW_GROUP) + N_EXPERTS
    return padded_rows, n_items


def _routing_tables(eid, rank, counts, n_items):
    padded = (counts + MOE_ROW_BLOCK - 1) // MOE_ROW_BLOCK * MOE_ROW_BLOCK
    pad_end = jnp.cumsum(padded)
    pad_start = pad_end - padded
    hot = eid[..., None] == jnp.arange(N_EXPERTS, dtype=jnp.int32)
    dest = (jnp.sum(jnp.where(hot, pad_start, 0), axis=-1) + rank).astype(jnp.int32)
    tail = jnp.where(padded > counts, pad_end - MOE_ROW_BLOCK, -1).astype(jnp.int32)
    per_expert = (padded + MOE_ROW_GROUP - 1) // MOE_ROW_GROUP
    item_end = jnp.cumsum(per_expert)
    item_start = item_end - per_expert
    total = item_end[-1]
    ii = jnp.arange(n_items, dtype=jnp.int32)
    e_of = jnp.minimum(jnp.searchsorted(item_end, ii, side="right"), N_EXPERTS - 1).astype(jnp.int32)
    valid = ii < total
    e_last = e_of[jnp.maximum(total - 1, 0)]
    local = ii - item_start[e_of]
    row0 = pad_start[e_of] + local * MOE_ROW_GROUP
    nblk = jnp.clip((padded[e_of] - local * MOE_ROW_GROUP) // MOE_ROW_BLOCK, 0, MOE_ROW_GROUP // MOE_ROW_BLOCK)
    item_e = jnp.where(valid, e_of, e_last).astype(jnp.int32)
    item_row0 = jnp.where(valid, row0, 0).astype(jnp.int32)
    item_nblk = jnp.where(valid, nblk, 0).astype(jnp.int32)
    return dest, tail, item_e, item_row0, item_nblk


DISPATCH_TOKENS = 256
ROW_DMA_UNROLL = 8


def _row_copy(src, s, dst, d, sem):
    return pltpu.make_async_copy(src.at[pl.ds(s, 1), :], dst.at[pl.ds(d, 1), :], sem)


def _dispatch_kernel(dest_ref, tail_ref, h_ctx, h_lat, xs, zero_buf, sem, *, n_ctx, n_tok):
    step = pl.program_id(0)
    tail_copy = lambda e: pltpu.make_async_copy(
        zero_buf, xs.at[pl.ds(pl.multiple_of(tail_ref[e], MOE_ROW_BLOCK), MOE_ROW_BLOCK), :], sem.at[1])

    @pl.when(step == 0)
    def _():
        zero_buf[...] = jnp.zeros_like(zero_buf)
        for e in range(N_EXPERTS):
            @pl.when(tail_ref[e] >= 0)
            def _():
                tail_copy(e).start()
        for e in range(N_EXPERTS):
            @pl.when(tail_ref[e] >= 0)
            def _():
                tail_copy(e).wait()

    tok0 = step * DISPATCH_TOKENS

    def scatter(src):
        def issue(t, carry):
            _row_copy(src, t, xs, dest_ref[tok0 + t], sem.at[0]).start()
            _row_copy(src, t, xs, dest_ref[n_tok + tok0 + t], sem.at[0]).start()
            return carry

        def drain(t, carry):
            _row_copy(src, 0, xs, 0, sem.at[0]).wait()
            _row_copy(src, 0, xs, 0, sem.at[0]).wait()
            return carry

        lax.fori_loop(0, DISPATCH_TOKENS, issue, 0, unroll=ROW_DMA_UNROLL)
        lax.fori_loop(0, DISPATCH_TOKENS, drain, 0, unroll=ROW_DMA_UNROLL)

    @pl.when(tok0 < n_ctx)
    def _():
        scatter(h_ctx)

    @pl.when(tok0 >= n_ctx)
    def _():
        scatter(h_lat)


def _dispatch_call(dest_flat, tail, h_ctx, h_lat, padded_rows):
    n_ctx, d = h_ctx.shape
    n_tok = n_ctx + h_lat.shape[0]
    ctx_tiles = n_ctx // DISPATCH_TOKENS
    return pl.pallas_call(
        functools.partial(_dispatch_kernel, n_ctx=n_ctx, n_tok=n_tok),
        grid_spec=pltpu.PrefetchScalarGridSpec(
            num_scalar_prefetch=2,
            grid=(n_tok // DISPATCH_TOKENS,),
            in_specs=[pl.BlockSpec((DISPATCH_TOKENS, d), lambda i, dr, tr: (jnp.minimum(i, ctx_tiles - 1), 0)),
                      pl.BlockSpec((DISPATCH_TOKENS, d), lambda i, dr, tr: (jnp.maximum(i - ctx_tiles, 0), 0))],
            out_specs=pl.BlockSpec(memory_space=pl.ANY),
            scratch_shapes=[pltpu.VMEM((MOE_ROW_BLOCK, d), F32), pltpu.SemaphoreType.DMA((2,))]),
        out_shape=jax.ShapeDtypeStruct((padded_rows, d), F32),
        compiler_params=_cparams(("arbitrary",)),
        name="dispatch",
    )(dest_flat, tail, h_ctx, h_lat)


def _moe_kernel(item_e, item_row0, item_nblk, xs, w1_ref, w3_ref, w2_ref, ys,
                x_in, x_bf, acc, w1_bf, w3_bf, w2_bf, sem):
    i = pl.program_id(0)
    c = pl.program_id(1)
    n_items = pl.num_programs(0)
    last_c = pl.num_programs(1) - 1
    nblk = item_nblk[i]
    max_blk = MOE_ROW_GROUP // MOE_ROW_BLOCK
    blk = lambda b: pl.ds(b * MOE_ROW_BLOCK, MOE_ROW_BLOCK)

    def rows_of(item, b):
        return pl.ds(pl.multiple_of(item_row0[item], MOE_ROW_BLOCK) + b * MOE_ROW_BLOCK, MOE_ROW_BLOCK)

    load = lambda item, b: pltpu.make_async_copy(xs.at[rows_of(item, b), :], x_in.at[blk(b), :], sem.at[0])
    store = lambda item, b: pltpu.make_async_copy(acc.at[blk(b), :], ys.at[rows_of(item, b), :], sem.at[1])

    def for_blocks(item, fn):
        n = item_nblk[item]
        for b in range(max_blk):
            @pl.when(b < n)
            def _():
                fn(item, b)

    @pl.when(c == 0)
    def _():
        @pl.when(i == 0)
        def _():
            for_blocks(0, lambda it, b: load(it, b).start())

        for_blocks(i, lambda it, b: load(it, b).wait())

        def cast(it, b):
            x_bf[blk(b), :] = x_in[blk(b), :].astype(BF16)

        for_blocks(i, cast)

        @pl.when(i + 1 < n_items)
        def _():
            for_blocks(i + 1, lambda it, b: load(it, b).start())

        @pl.when(i > 0)
        def _():
            for_blocks(i - 1, lambda it, b: store(it, b).wait())

        def clear(it, b):
            acc[blk(b), :] = jnp.zeros((MOE_ROW_BLOCK, acc.shape[1]), F32)

        for_blocks(i, clear)

    @pl.when(nblk > 0)
    def _():
        w1_bf[...] = w1_ref[0].astype(BF16)
        w3_bf[...] = w3_ref[0].astype(BF16)
        w2_bf[...] = w2_ref[0].astype(BF16)

        def body(b, carry):
            rows = pl.ds(pl.multiple_of(b * MOE_ROW_BLOCK, MOE_ROW_BLOCK), MOE_ROW_BLOCK)
            x = x_bf[rows, :]
            h1 = _dot(x, w1_bf[...])
            h3 = _dot(x, w3_bf[...])
            a = (h1 * jax.nn.sigmoid(h1) * h3).astype(BF16)
            acc[rows, :] += _dot(a, w2_bf[...])
            return carry

        lax.fori_loop(0, nblk, body, 0)

    @pl.when(c == last_c)
    def _():
        for_blocks(i, lambda it, b: store(it, b).start())

        @pl.when(i == n_items - 1)
        def _():
            for_blocks(i, lambda it, b: store(it, b).wait())


def _moe_call(item_e, item_row0, item_nblk, xs, w1, w3, w2):
    padded_rows, d = xs.shape
    n_items = item_e.shape[0]
    f = w1.shape[2]
    nc = f // MOE_F_CHUNK
    chunk = lambda i, c, ib: jnp.where(ib[i] > 0, c, nc - 1)
    return pl.pallas_call(
        _moe_kernel,
        grid_spec=pltpu.PrefetchScalarGridSpec(
            num_scalar_prefetch=3,
            grid=(n_items, nc),
            in_specs=[pl.BlockSpec(memory_space=pl.ANY),
                      pl.BlockSpec((1, d, MOE_F_CHUNK), lambda i, c, ie, ir, ib: (ie[i], 0, chunk(i, c, ib))),
                      pl.BlockSpec((1, d, MOE_F_CHUNK), lambda i, c, ie, ir, ib: (ie[i], 0, chunk(i, c, ib))),
                      pl.BlockSpec((1, MOE_F_CHUNK, d), lambda i, c, ie, ir, ib: (ie[i], chunk(i, c, ib), 0))],
            out_specs=pl.BlockSpec(memory_space=pl.ANY),
            scratch_shapes=[pltpu.VMEM((MOE_ROW_GROUP, d), F32),
                            pltpu.VMEM((MOE_ROW_GROUP, d), BF16),
                            pltpu.VMEM((MOE_ROW_GROUP, d), F32),
                            pltpu.VMEM((d, MOE_F_CHUNK), BF16),
                            pltpu.VMEM((d, MOE_F_CHUNK), BF16),
                            pltpu.VMEM((MOE_F_CHUNK, d), BF16),
                            pltpu.SemaphoreType.DMA((2,))]),
        out_shape=jax.ShapeDtypeStruct((padded_rows, d), F32),
        compiler_params=_cparams(("arbitrary", "arbitrary")),
        name="moe",
    )(item_e, item_row0, item_nblk, xs, w1, w3, w2)


COMBINE_TM = 256


def _combine_kernel(dest_ref, x1_ref, gw_ref, g2_ref, ys, o_ref, y0, y1, sem, *, tok_base, n_tok):
    tok0 = tok_base + pl.program_id(0) * COMBINE_TM

    def issue(t, carry):
        _row_copy(ys, dest_ref[tok0 + t], y0, t, sem).start()
        _row_copy(ys, dest_ref[n_tok + tok0 + t], y1, t, sem).start()
        return carry

    def drain(t, carry):
        _row_copy(ys, 0, y0, 0, sem).wait()
        _row_copy(ys, 0, y1, 0, sem).wait()
        return carry

    lax.fori_loop(0, COMBINE_TM, issue, 0, unroll=ROW_DMA_UNROLL)
    lax.fori_loop(0, COMBINE_TM, drain, 0, unroll=ROW_DMA_UNROLL)
    gw = gw_ref[...]
    moe = gw[:, 0:1] * y0[...] + gw[:, 1:2] * y1[...]
    o_ref[...] = x1_ref[...] + g2_ref[0] * moe


def _combine_call(dest_flat, x1, gw, g2, ys, tok_base, n_tok, tiles_per_group):
    m, d = x1.shape
    return pl.pallas_call(
        functools.partial(_combine_kernel, tok_base=tok_base, n_tok=n_tok),
        grid_spec=pltpu.PrefetchScalarGridSpec(
            num_scalar_prefetch=1,
            grid=(m // COMBINE_TM,),
            in_specs=[pl.BlockSpec((COMBINE_TM, d), lambda i, dr: (i, 0)),
                      pl.BlockSpec((COMBINE_TM, 2), lambda i, dr: (i, 0)),
                      pl.BlockSpec((1, 1, d), lambda i, dr: (i // tiles_per_group, 0, 0)),
                      pl.BlockSpec(memory_space=pl.ANY)],
            out_specs=pl.BlockSpec((COMBINE_TM, d), lambda i, dr: (i, 0)),
            scratch_shapes=[pltpu.VMEM((COMBINE_TM, d), F32), pltpu.VMEM((COMBINE_TM, d), F32),
                            pltpu.SemaphoreType.DMA]),
        out_shape=jax.ShapeDtypeStruct((m, d), F32),
        compiler_params=_cparams(("arbitrary",)),
        name="combine",
    )(dest_flat, x1, gw, g2, ys)


def kernel(x_prompt, x_sample, cache_a_k, cache_a_v, cache_b_k, cache_b_v, c, c_ctx, norm1_w, norm2_w, w_ada, b_ada, w_in, qn_a, kn_a, qn_b, kn_b, rpb_a, sink_b, w_pa, w_pb, w_out, w_rg, b_rg, w_re, b_re, w1, w3, w2):
    batch, seq, d = x_prompt.shape
    dec_batch, dec_seq, _ = x_sample.shape
    depth = norm1_w.shape[0]
    assert depth == 1, "one trunk layer"
    past = cache_a_k.shape[2]
    n_ctx, n_lat = batch * seq, dec_batch * dec_seq
    n_tok = n_ctx + n_lat

    xc = x_prompt.reshape(n_ctx, d)
    xl = x_sample.reshape(n_lat, d)

    n_rows = 8
    cvec = jnp.zeros((n_rows, d), F32).at[0].set(c_ctx).at[1:1 + dec_batch].set(c)
    mod = _ada_call(cvec, w_ada[0], b_ada[0][None, :])
    sh1, sc1, g1, sh2, sc2, g2 = [mod[:, i * d:(i + 1) * d][:, None, :] for i in range(6)]
    ctx_rows, lat_rows = slice(0, 1), slice(1, 1 + dec_batch)

    nw1, nw2 = norm1_w[0][None, :], norm2_w[0][None, :]
    qna, kna, qnb, knb = qn_a[0][None, :], kn_a[0][None, :], qn_b[0][None, :], kn_b[0][None, :]
    sink = sink_b[0]
    w_in_bf = w_in[0].astype(BF16)
    wpa_bf, wpb_bf, wout_bf = w_pa[0].astype(BF16), w_pb[0].astype(BF16), w_out[0].astype(BF16)
    wr = jnp.zeros((d, LOGIT_PAD), F32).at[:, :N_EXPERTS].set(w_re[0]).at[:, N_EXPERTS:N_EXPERTS + N_GROUPS].set(w_rg[0])
    br = jnp.zeros((LOGIT_PAD, 1), F32).at[:N_EXPERTS, 0].set(b_re[0]).at[N_EXPERTS:N_EXPERTS + N_GROUPS, 0].set(b_rg[0])

    proj_c, gates_c = _inproj_call(xc, nw1, sc1[ctx_rows], sh1[ctx_rows], w_in_bf, n_ctx // INPROJ_TM)
    proj_l, gates_l = _inproj_call(xl, nw1, sc1[lat_rows], sh1[lat_rows], w_in_bf, dec_seq // INPROJ_TM)

    oa_c, ob_c, new_a_k, new_a_v, new_b_k, new_b_v = _ctx_attn_call(proj_c, seq, qna, kna, qnb, knb, sink)

    bias = _na_bias(rpb_a[0], dec_seq)
    cos, sin_a, sin_b = _rope_tables(dec_seq)
    cak = cache_a_k[:, 0].reshape(dec_batch, past, NA_WIDTH)
    cav = cache_a_v[:, 0].reshape(dec_batch, past, NA_WIDTH)
    cbk = cache_b_k[:, 0].reshape(dec_batch, past, NB_KV_WIDTH)
    cbv = cache_b_v[:, 0].reshape(dec_batch, past, NB_KV_WIDTH)
    oa_l = _lat_attn_a_call(proj_l, dec_seq, cak, cav, bias, qna, kna)
    ob_l = _lat_attn_b_call(proj_l, dec_seq, cbk, cbv, cos, sin_a, sin_b, qnb, knb, sink)

    mix_c = _mix_call(oa_c, ob_c, gates_c, wpa_bf, wpb_bf)
    mix_l = _mix_call(oa_l, ob_l, gates_l, wpa_bf, wpb_bf)
    x1_c, h2_c, lg_c = _outproj_call(mix_c, wout_bf, xc, g1[ctx_rows], sc2[ctx_rows], sh2[ctx_rows],
                                     nw2, wr, n_ctx // MERGE_TM)
    x1_l, h2_l, lg_l = _outproj_call(mix_l, wout_bf, xl, g1[lat_rows], sc2[lat_rows], sh2[lat_rows],
                                     nw2, wr, dec_seq // MERGE_TM)

    eid, gw, rank, cnt = _route_call(jnp.concatenate([lg_c, lg_l], axis=0), br)
    padded_rows, n_items = _moe_layout(2 * n_tok)
    dest, tail, item_e, item_row0, item_nblk = _routing_tables(eid[:2], rank[:2], cnt[:, 0], n_items)
    dest_flat = dest.reshape(-1)
    xs = _dispatch_call(dest_flat, tail, h2_c, h2_l, padded_rows)
    ys = _moe_call(item_e, item_row0, item_nblk, xs, w1[0], w3[0], w2[0])
    gw_t = gw[:2].T
    y_c = _combine_call(dest_flat, x1_c, gw_t[:n_ctx], g2[ctx_rows], ys, 0, n_tok, n_ctx // COMBINE_TM)
    y_l = _combine_call(dest_flat, x1_l, gw_t[n_ctx:], g2[lat_rows], ys, n_ctx, n_tok, dec_seq // COMBINE_TM)

    state = lambda a, heads: a.reshape(batch, 1, seq, heads, HEAD_DIM)
    return (y_c.reshape(batch, seq, d), y_l.reshape(dec_batch, dec_seq, d),
            state(new_a_k, NA_HEADS), state(new_a_v, NA_HEADS),
            state(new_b_k, NB_KV_HEADS), state(new_b_v, NB_KV_HEADS))
```

```python
import functools

import jax
import jax.numpy as jnp
import numpy as np
from jax import lax
from jax.experimental import pallas as pl
from jax.experimental.pallas import tpu as pltpu

D_MODEL = 2048
HEAD_DIM = 128
NA_HEADS = 8
NA_WIDTH = NA_HEADS * HEAD_DIM
NB_Q_HEADS = 8
NB_KV_HEADS = 2
NB_GROUP = NB_Q_HEADS // NB_KV_HEADS
NB_WIDTH = NB_Q_HEADS * HEAD_DIM
NB_KV_WIDTH = NB_KV_HEADS * HEAD_DIM
GRID_W = 64
NA_WIN_ROWS = 8
NA_WIN_COLS = 16
WINDOW = 128
N_GROUPS = 4
EXPERTS_PER_GROUP = 8
N_EXPERTS = N_GROUPS * EXPERTS_PER_GROUP
D_EXPERT = 1024
IN_WIDTH = 3 * NA_WIDTH + NB_WIDTH + 2 * NB_KV_WIDTH + 2 * D_MODEL
ROPE_BASE = 10000.0
NORM_EPS = 1e-6
NEG_INF = -1e30
ATTN_SCALE = HEAD_DIM ** -0.5

QA_HEAD0 = 0
KA_HEAD0 = NA_HEADS
VA_HEAD0 = 2 * NA_HEADS
QB_HEAD0 = 3 * NA_HEADS
KB_HEAD0 = QB_HEAD0 + NB_Q_HEADS
VB_HEAD0 = KB_HEAD0 + NB_KV_HEADS
GATE_COL0 = (VB_HEAD0 + NB_KV_HEADS) * HEAD_DIM

LOGIT_PAD = 128
MOE_ROW_BLOCK = 256
MOE_ROW_GROUP = 1024
MOE_F_CHUNK = 512
VMEM_LIMIT = 56 * 1024 * 1024

F32 = jnp.float32
BF16 = jnp.bfloat16


def _cparams(sem):
    return pltpu.CompilerParams(dimension_semantics=sem, vmem_limit_bytes=VMEM_LIMIT)


def _rms(x, w):
    return x * lax.rsqrt(jnp.mean(x * x, axis=-1, keepdims=True) + NORM_EPS) * w


def _dot(a, b):
    return jnp.dot(a, b, preferred_element_type=F32)


def _dot_nt(a, b):
    return lax.dot_general(a, b, (((1,), (1,)), ((), ())), preferred_element_type=F32)


def _dot_split(a, b):
    a_hi = a.astype(BF16)
    a_lo = (a - a_hi.astype(F32)).astype(BF16)
    b_hi = b.astype(BF16)
    b_lo = (b - b_hi.astype(F32)).astype(BF16)
    return _dot(a_hi, b_hi) + (_dot(a_lo, b_hi) + _dot(a_hi, b_lo))


ADA_ROWS = 8
ADA_TN = 1024


def _ada_kernel(c_ref, w_ref, b_ref, o_ref):
    n_rows, d, lanes = c_ref.shape
    tn = w_ref.shape[1]

    def body(kb, acc):
        ks = pl.ds(pl.multiple_of(kb * 8, 8), 8)
        w = w_ref[ks, :]
        out = []
        for r in range(n_rows):
            c = c_ref[r, ks, :]
            s = c * jax.nn.sigmoid(c)
            out.append(acc[r] + w * jnp.concatenate([s] * (tn // lanes), axis=1))
        return tuple(out)

    acc = lax.fori_loop(0, d // 8, body, tuple(jnp.zeros((8, tn), F32) for _ in range(n_rows)), unroll=8)
    ri = lax.broadcasted_iota(jnp.int32, (ADA_ROWS, tn), 0)
    res = jnp.zeros((ADA_ROWS, tn), F32)
    for r in range(n_rows):
        row = jnp.sum(acc[r], axis=0, keepdims=True) + b_ref[...]
        res = jnp.where(ri == r, row, res)
    o_ref[...] = res


def _ada_call(cond, w_ada, b_ada):
    n_rows, d = cond.shape
    n = w_ada.shape[1]
    lanes = 128
    cond_lanes = jnp.broadcast_to(cond[:, :, None], (n_rows, d, lanes))
    return pl.pallas_call(
        _ada_kernel,
        grid=(n // ADA_TN,),
        in_specs=[pl.BlockSpec((n_rows, d, lanes), lambda j: (0, 0, 0)),
                  pl.BlockSpec((d, ADA_TN), lambda j: (0, j)),
                  pl.BlockSpec((1, ADA_TN), lambda j: (0, j))],
        out_specs=pl.BlockSpec((ADA_ROWS, ADA_TN), lambda j: (0, j)),
        out_shape=jax.ShapeDtypeStruct((ADA_ROWS, n), F32),
        compiler_params=_cparams(("arbitrary",)),
        name="ada",
    )(cond_lanes, w_ada, b_ada)


INPROJ_TM = 1024
INPROJ_TN = 512
NORM_ROWS = 128


def _inproj_kernel(x_ref, nw_ref, sc_ref, sh_ref, w_ref, qkv_ref, gate_ref, h_scr, *, qkv_tiles):
    j = pl.program_id(1)

    @pl.when(j == 0)
    def _():
        nw = nw_ref[...]
        sc = 1.0 + sc_ref[0]
        sh = sh_ref[0]

        def body(r, carry):
            rows = pl.ds(pl.multiple_of(r * NORM_ROWS, NORM_ROWS), NORM_ROWS)
            h_scr[rows, :] = (_rms(x_ref[rows, :], nw) * sc + sh).astype(BF16)
            return carry

        lax.fori_loop(0, INPROJ_TM // NORM_ROWS, body, 0)

    res = _dot(h_scr[...], w_ref[...])

    @pl.when(j < qkv_tiles)
    def _():
        qkv_ref[...] = res

    @pl.when(j >= qkv_tiles)
    def _():
        gate_ref[...] = res.astype(gate_ref.dtype)


def _inproj_call(x, nw, sc, sh, w_bf, tiles_per_group):
    m, d = x.shape
    n = w_bf.shape[1]
    qkv_tiles = GATE_COL0 // INPROJ_TN
    grp = lambda i, j: (i // tiles_per_group, 0, 0)
    return pl.pallas_call(
        functools.partial(_inproj_kernel, qkv_tiles=qkv_tiles),
        grid=(m // INPROJ_TM, n // INPROJ_TN),
        in_specs=[pl.BlockSpec((INPROJ_TM, d), lambda i, j: (i, 0)),
                  pl.BlockSpec((1, d), lambda i, j: (0, 0)),
                  pl.BlockSpec((1, 1, d), grp),
                  pl.BlockSpec((1, 1, d), grp),
                  pl.BlockSpec((d, INPROJ_TN), lambda i, j: (0, j))],
        out_specs=[pl.BlockSpec((INPROJ_TM, INPROJ_TN), lambda i, j: (i, jnp.minimum(j, qkv_tiles - 1))),
                   pl.BlockSpec((INPROJ_TM, INPROJ_TN), lambda i, j: (i, jnp.maximum(j - qkv_tiles, 0)))],
        out_shape=[jax.ShapeDtypeStruct((m, GATE_COL0), F32),
                   jax.ShapeDtypeStruct((m, n - GATE_COL0), BF16)],
        scratch_shapes=[pltpu.VMEM((INPROJ_TM, d), BF16)],
        compiler_params=_cparams(("arbitrary", "arbitrary")),
        name="inproj",
    )(x, nw, sc, sh, w_bf)


def _softmax_pv(scores, values, sink=None):
    m = None
    for s in scores:
        ms = jnp.max(s, axis=-1, keepdims=True)
        m = ms if m is None else jnp.maximum(m, ms)
    if sink is not None:
        m = jnp.maximum(m, sink)
    den = None
    acc = None
    for s, v in zip(scores, values):
        e = jnp.exp(s - m)
        ds = jnp.sum(e, axis=-1, keepdims=True)
        den = ds if den is None else den + ds
        pv = _dot(e.astype(BF16), v)
        acc = pv if acc is None else acc + pv
    if sink is not None:
        den = den + jnp.exp(sink - m)
    return acc / den


def _ctx_attn_kernel(qa_ref, ka_ref, va_ref, qb_ref, kb_ref, vb_ref,
                     qna_ref, kna_ref, qnb_ref, knb_ref, sink_ref,
                     oa_ref, ob_ref, nak_ref, nav_ref, nbk_ref, nbv_ref):
    qna, kna, qnb, knb = qna_ref[...], kna_ref[...], qnb_ref[...], knb_ref[...]
    for h in range(NA_HEADS):
        cols = slice(h * HEAD_DIM, (h + 1) * HEAD_DIM)
        q = (_rms(qa_ref[:, cols], qna) * ATTN_SCALE).astype(BF16)
        k = _rms(ka_ref[:, cols], kna)
        v = va_ref[:, cols]
        nak_ref[:, cols] = k
        nav_ref[:, cols] = v
        s = _dot_nt(q, k.astype(BF16))
        oa_ref[:, cols] = _softmax_pv([s], [v.astype(BF16)]).astype(oa_ref.dtype)
    for kv in range(NB_KV_HEADS):
        kcols = slice(kv * HEAD_DIM, (kv + 1) * HEAD_DIM)
        k = _rms(kb_ref[:, kcols], knb)
        v = vb_ref[:, kcols]
        nbk_ref[:, kcols] = k
        nbv_ref[:, kcols] = v
        kb16 = k.astype(BF16)
        vb16 = v.astype(BF16)
        for g in range(NB_GROUP):
            hq = kv * NB_GROUP + g
            cols = slice(hq * HEAD_DIM, (hq + 1) * HEAD_DIM)
            q = (_rms(qb_ref[:, cols], qnb) * ATTN_SCALE).astype(BF16)
            s = _dot_nt(q, kb16)
            ob_ref[:, cols] = _softmax_pv([s], [vb16], sink=sink_ref[hq]).astype(ob_ref.dtype)


def _ctx_attn_call(proj, seq, qna, kna, qnb, knb, sink):
    m = proj.shape[0]
    nb = m // seq
    wide = lambda blk: pl.BlockSpec((seq, NA_WIDTH), lambda b: (b, blk))
    narrow = lambda blk: pl.BlockSpec((seq, NB_KV_WIDTH), lambda b: (b, blk))
    vec = pl.BlockSpec((1, HEAD_DIM), lambda b: (0, 0))
    return pl.pallas_call(
        _ctx_attn_kernel,
        grid=(nb,),
        in_specs=[wide(QA_HEAD0 // NA_HEADS), wide(KA_HEAD0 // NA_HEADS), wide(VA_HEAD0 // NA_HEADS),
                  wide(QB_HEAD0 // NA_HEADS), narrow(KB_HEAD0 // NB_KV_HEADS), narrow(VB_HEAD0 // NB_KV_HEADS),
                  vec, vec, vec, vec,
                  pl.BlockSpec(memory_space=pltpu.SMEM)],
        out_specs=[pl.BlockSpec((seq, NA_WIDTH), lambda b: (b, 0)),
                   pl.BlockSpec((seq, NB_WIDTH), lambda b: (b, 0)),
                   pl.BlockSpec((seq, NA_WIDTH), lambda b: (b, 0)),
                   pl.BlockSpec((seq, NA_WIDTH), lambda b: (b, 0)),
                   pl.BlockSpec((seq, NB_KV_WIDTH), lambda b: (b, 0)),
                   pl.BlockSpec((seq, NB_KV_WIDTH), lambda b: (b, 0))],
        out_shape=[jax.ShapeDtypeStruct((m, NA_WIDTH), BF16),
                   jax.ShapeDtypeStruct((m, NB_WIDTH), BF16),
                   jax.ShapeDtypeStruct((m, NA_WIDTH), F32),
                   jax.ShapeDtypeStruct((m, NA_WIDTH), F32),
                   jax.ShapeDtypeStruct((m, NB_KV_WIDTH), F32),
                   jax.ShapeDtypeStruct((m, NB_KV_WIDTH), F32)],
        compiler_params=_cparams(("arbitrary",)),
        name="ctx_attn",
    )(proj, proj, proj, proj, proj, proj, qna, kna, qnb, knb, sink)


def _rope(x, cos, sin_a, sin_b):
    quarter = HEAD_DIM // 4
    return (x * cos + pltpu.roll(x, HEAD_DIM - quarter, 1) * sin_a
            + pltpu.roll(x, quarter, 1) * sin_b)


def _head_rows(cache_ref, head, n_heads):
    past = cache_ref.shape[0] // n_heads
    return cache_ref[pl.ds(head, past, stride=n_heads), :]


def _lat_attn_a_kernel(q_ref, k_ref, v_ref, ck_ref, cv_ref, cb_ref, qn_ref, kn_ref, o_ref, bias_scr,
                       *, pair0):
    head = pl.program_id(0)

    @pl.when(pl.program_id(1) == 0)
    def _():
        bias_scr[...] = jnp.full(bias_scr.shape, NEG_INF, F32)
        for qr, p0 in enumerate(pair0):
            for j in range(cb_ref.shape[1]):
                bias_scr[qr * GRID_W:(qr + 1) * GRID_W, (p0 + j) * 2 * GRID_W:(p0 + j + 1) * 2 * GRID_W] = cb_ref[qr, j]

    q = (_rms(q_ref[...], qn_ref[...]) * ATTN_SCALE).astype(BF16)
    k = _rms(k_ref[...], kn_ref[...]).astype(BF16)
    s_lat = _dot_nt(q, k) + bias_scr[...]
    ck = _head_rows(ck_ref, head, NA_HEADS).astype(BF16)
    cv = _head_rows(cv_ref, head, NA_HEADS).astype(BF16)
    s_ctx = _dot_nt(q, ck)
    o = _softmax_pv([s_lat, s_ctx], [v_ref[...].astype(BF16), cv])
    o_ref[...] = o.astype(o_ref.dtype)


def _lat_attn_b_kernel(q_ref, k_ref, v_ref, ck_ref, cv_ref, cos_ref, sina_ref, sinb_ref,
                       qn_ref, kn_ref, sink_ref, o_ref):
    cos, sin_a, sin_b = cos_ref[...], sina_ref[...], sinb_ref[...]
    q = _rope(_rms(q_ref[...], qn_ref[...]), cos, sin_a, sin_b)
    k = _rope(_rms(k_ref[...], kn_ref[...]), cos, sin_a, sin_b)
    q = (q * ATTN_SCALE).astype(BF16)
    s_lat = _dot_nt(q, k.astype(BF16))
    length = s_lat.shape[0]
    qi = lax.broadcasted_iota(jnp.int32, (length, length), 0)
    kj = lax.broadcasted_iota(jnp.int32, (length, length), 1)
    s_lat = jnp.where(jnp.abs(qi - kj) <= WINDOW, s_lat, NEG_INF)
    kv = pl.program_id(0) // NB_GROUP
    ck = _head_rows(ck_ref, kv, NB_KV_HEADS).astype(BF16)
    cv = _head_rows(cv_ref, kv, NB_KV_HEADS).astype(BF16)
    s_ctx = _dot_nt(q, ck)
    o = _softmax_pv([s_lat, s_ctx], [v_ref[...].astype(BF16), cv], sink=sink_ref[pl.program_id(0)])
    o_ref[...] = o.astype(o_ref.dtype)


def _lat_attn_a_call(proj, length, past, ck, cv, cb, pair0, qn, kn):
    m = proj.shape[0]
    head = lambda h0: pl.BlockSpec((length, HEAD_DIM), lambda h, b: (b, h0 + h))
    cache = pl.BlockSpec((past * NA_HEADS, HEAD_DIM), lambda h, b: (b, 0))
    vec = pl.BlockSpec((1, HEAD_DIM), lambda h, b: (0, 0))
    return pl.pallas_call(
        functools.partial(_lat_attn_a_kernel, pair0=pair0),
        grid=(NA_HEADS, m // length),
        in_specs=[head(QA_HEAD0), head(KA_HEAD0), head(VA_HEAD0), cache, cache,
                  pl.BlockSpec((None,) + cb.shape[1:], lambda h, b: (h, 0, 0, 0, 0)), vec, vec],
        out_specs=pl.BlockSpec((length, HEAD_DIM), lambda h, b: (b, h)),
        out_shape=jax.ShapeDtypeStruct((m, NA_WIDTH), BF16),
        scratch_shapes=[pltpu.VMEM((length, length), F32)],
        compiler_params=_cparams(("arbitrary", "arbitrary")),
        name="lat_attn_a",
    )(proj, proj, proj, ck, cv, cb, qn, kn)


def _lat_attn_b_call(proj, length, past, ck, cv, cos, sin_a, sin_b, qn, kn, sink):
    m = proj.shape[0]
    qspec = pl.BlockSpec((length, HEAD_DIM), lambda h, b: (b, QB_HEAD0 + h))
    kvspec = lambda h0: pl.BlockSpec((length, HEAD_DIM), lambda h, b: (b, h0 + h // NB_GROUP))
    cache = pl.BlockSpec((past * NB_KV_HEADS, HEAD_DIM), lambda h, b: (b, 0))
    table = pl.BlockSpec((length, HEAD_DIM), lambda h, b: (0, 0))
    vec = pl.BlockSpec((1, HEAD_DIM), lambda h, b: (0, 0))
    return pl.pallas_call(
        _lat_attn_b_kernel,
        grid=(NB_Q_HEADS, m // length),
        in_specs=[qspec, kvspec(KB_HEAD0), kvspec(VB_HEAD0), cache, cache,
                  table, table, table, vec, vec, pl.BlockSpec(memory_space=pltpu.SMEM)],
        out_specs=pl.BlockSpec((length, HEAD_DIM), lambda h, b: (b, h)),
        out_shape=jax.ShapeDtypeStruct((m, NB_WIDTH), BF16),
        compiler_params=_cparams(("arbitrary", "arbitrary")),
        name="lat_attn_b",
    )(proj, proj, proj, ck, cv, cos, sin_a, sin_b, qn, kn, sink)


def _na_bias_blocks(rpb, length):
    rows = length // GRID_W
    kr_n = min(NA_WIN_ROWS, rows)
    n_pairs = min(kr_n // 2 + 1, rows // 2)
    r = np.arange(rows)
    c = np.arange(GRID_W)
    r0 = np.clip(r - kr_n // 2, 0, rows - kr_n)
    c0 = np.clip(c - NA_WIN_COLS // 2, 0, GRID_W - NA_WIN_COLS)
    pair0 = np.minimum(r0 // 2, rows // 2 - n_pairs)
    kr = 2 * (pair0[:, None, None] + np.arange(n_pairs)[None, :, None]) + np.arange(2)[None, None, :]
    row_ok = (kr >= r0[:, None, None]) & (kr < r0[:, None, None] + kr_n)
    col_ok = (c[None, :] >= c0[:, None]) & (c[None, :] < c0[:, None] + NA_WIN_COLS)
    dr = kr - r[:, None, None] + (NA_WIN_ROWS - 1)
    dc = np.clip(c[None, :] - c[:, None], -(NA_WIN_COLS - 1), NA_WIN_COLS - 1) + (NA_WIN_COLS - 1)
    row_sel = ((dr[None] == np.arange(2 * NA_WIN_ROWS - 1)[:, None, None, None]) & row_ok[None]).astype(np.float32)
    col_sel = ((dc[None] == np.arange(2 * NA_WIN_COLS - 1)[:, None, None]) & col_ok[None]).astype(np.float32)
    hi = lax.Precision.HIGHEST
    per_col = jnp.einsum("hde,eck->hdck", rpb.astype(F32), col_sel, precision=hi)
    table = jnp.einsum("dqjl,hdck->hqjclk", row_sel, per_col, precision=hi)
    valid = row_ok[:, :, None, :, None] & col_ok[None, None, :, None, :]
    blocks = jnp.where(valid[None], table, NEG_INF)
    return blocks.reshape(rpb.shape[0], rows, n_pairs, GRID_W, 2 * GRID_W), tuple(int(p) for p in pair0)


def _rope_tables(length):
    t = jnp.arange(length)
    row = (t // GRID_W).astype(F32)
    col = (t % GRID_W).astype(F32)
    n_freq = HEAD_DIM // 4
    inv = ROPE_BASE ** (-jnp.arange(n_freq, dtype=F32) / n_freq)
    ar = row[:, None] * inv
    ac = col[:, None] * inv
    ang = jnp.concatenate([ar, ar, ac, ac], axis=-1)
    cos, sin = jnp.cos(ang), jnp.sin(ang)
    lane = jnp.arange(HEAD_DIM)
    takes_left = ((lane // n_freq) % 2 == 0)[None, :]
    return cos, jnp.where(takes_left, -sin, 0.0), jnp.where(takes_left, 0.0, sin)


MERGE_TM = 512


def _resident(shape):
    zeros = (0,) * len(shape)
    return pl.BlockSpec(shape, lambda i: zeros, pipeline_mode=pl.Buffered(1))


def _mix_kernel(oa_ref, ob_ref, ga_ref, gb_ref, wpa_ref, wpb_ref, mix_ref):
    ya = _dot(oa_ref[...], wpa_ref[...])
    yb = _dot(ob_ref[...], wpb_ref[...])
    mix = (jax.nn.sigmoid(ga_ref[...].astype(F32)) * ya
           + jax.nn.sigmoid(gb_ref[...].astype(F32)) * yb)
    mix_ref[...] = mix.astype(mix_ref.dtype)


def _mix_call(oa, ob, gates, wpa_bf, wpb_bf):
    m = oa.shape[0]
    d = wpa_bf.shape[1]
    row = lambda i: (i, 0)
    return pl.pallas_call(
        _mix_kernel,
        grid=(m // MERGE_TM,),
        in_specs=[pl.BlockSpec((MERGE_TM, NA_WIDTH), row),
                  pl.BlockSpec((MERGE_TM, NB_WIDTH), row),
                  pl.BlockSpec((MERGE_TM, d), lambda i: (i, 0)),
                  pl.BlockSpec((MERGE_TM, d), lambda i: (i, 1)),
                  _resident((NA_WIDTH, d)), _resident((NB_WIDTH, d))],
        out_specs=pl.BlockSpec((MERGE_TM, d), row),
        out_shape=jax.ShapeDtypeStruct((m, d), BF16),
        compiler_params=_cparams(("arbitrary",)),
        name="mix",
    )(oa, ob, gates, gates, wpa_bf, wpb_bf)


def _outproj_kernel(mix_ref, wout_ref, x_ref, g1_ref, sc2_ref, sh2_ref, n2w_ref, wr_ref,
                    x1_ref, h2_ref, lg_ref):
    x1 = x_ref[...] + g1_ref[0] * _dot(mix_ref[...], wout_ref[...])
    x1_ref[...] = x1
    h2 = _rms(x1, n2w_ref[...]) * (1.0 + sc2_ref[0]) + sh2_ref[0]
    h2_ref[...] = h2
    lg_ref[...] = _dot_split(h2, wr_ref[...])


def _outproj_call(mix, wout_bf, x, g1, sc2, sh2, n2w, wr, tiles_per_group):
    m, d = x.shape
    grp = lambda i: (i // tiles_per_group, 0, 0)
    row = lambda i: (i, 0)
    return pl.pallas_call(
        _outproj_kernel,
        grid=(m // MERGE_TM,),
        in_specs=[pl.BlockSpec((MERGE_TM, d), row),
                  _resident((d, d)),
                  pl.BlockSpec((MERGE_TM, d), row),
                  pl.BlockSpec((1, 1, d), grp), pl.BlockSpec((1, 1, d), grp), pl.BlockSpec((1, 1, d), grp),
                  pl.BlockSpec((1, d), lambda i: (0, 0)),
                  _resident((d, LOGIT_PAD))],
        out_specs=[pl.BlockSpec((MERGE_TM, d), row),
                   pl.BlockSpec((MERGE_TM, d), row),
                   pl.BlockSpec((MERGE_TM, LOGIT_PAD), row)],
        out_shape=[jax.ShapeDtypeStruct((m, d), F32),
                   jax.ShapeDtypeStruct((m, d), F32),
                   jax.ShapeDtypeStruct((m, LOGIT_PAD), F32)],
        compiler_params=_cparams(("arbitrary",)),
        name="outproj",
    )(mix, wout_bf, x, g1, sc2, sh2, n2w, wr)


ROUTE_TM = 512


def _first_index_of_max(vals, idx, n):
    mx = jnp.max(vals, axis=0, keepdims=True)
    first = jnp.min(jnp.where(vals == mx, idx, n), axis=0, keepdims=True)
    return mx, first


def _route_kernel(lg_ref, bias_ref, eid_ref, gw_ref, rank_ref, cnt_ref, base_ref):
    step = pl.program_id(0)

    @pl.when(step == 0)
    def _():
        base_ref[...] = jnp.zeros_like(base_ref)

    lt = lg_ref[...].T + bias_ref[...]
    n_tok = lt.shape[1]
    le = lt[0:N_EXPERTS]
    lgrp = lt[N_EXPERTS:N_EXPERTS + N_GROUPS]
    gi = lax.broadcasted_iota(jnp.int32, (N_GROUPS, n_tok), 0)
    gmax, gsel = _first_index_of_max(lgrp, gi, N_GROUPS)
    pg_sel = 1.0 / jnp.sum(jnp.exp(lgrp - gmax), axis=0, keepdims=True)
    le_sel = jnp.zeros((EXPERTS_PER_GROUP, n_tok), F32)
    for g in range(N_GROUPS):
        le_sel = jnp.where(gsel == g, le[g * EXPERTS_PER_GROUP:(g + 1) * EXPERTS_PER_GROUP], le_sel)
    ei = lax.broadcasted_iota(jnp.int32, (EXPERTS_PER_GROUP, n_tok), 0)
    v0, i0 = _first_index_of_max(le_sel, ei, EXPERTS_PER_GROUP)
    rest = jnp.where(ei == i0, -jnp.inf, le_sel)
    v1, i1 = _first_index_of_max(rest, ei, EXPERTS_PER_GROUP)
    e1 = jnp.exp(v1 - v0)
    w0 = pg_sel / (1.0 + e1)
    w1 = pg_sel * e1 / (1.0 + e1)
    eid0 = gsel * EXPERTS_PER_GROUP + i0
    eid1 = gsel * EXPERTS_PER_GROUP + i1

    xi = lax.broadcasted_iota(jnp.int32, (N_EXPERTS, n_tok), 0)
    si = lax.broadcasted_iota(jnp.int32, (n_tok, n_tok), 0)
    ti = lax.broadcasted_iota(jnp.int32, (n_tok, n_tok), 1)
    before = (si < ti).astype(BF16)
    base = base_ref[...]
    hot0 = (xi == eid0).astype(F32)
    hot1 = (xi == eid1).astype(F32)
    pre0 = _dot(hot0.astype(BF16), before)
    pre1 = _dot(hot1.astype(BF16), before)
    tot0 = jnp.sum(hot0, axis=1, keepdims=True)
    tot1 = jnp.sum(hot1, axis=1, keepdims=True)
    rank0 = jnp.sum(hot0 * (base + pre0), axis=0, keepdims=True)
    rank1 = jnp.sum(hot1 * (base + tot0 + pre1), axis=0, keepdims=True)
    base = base + tot0 + tot1
    base_ref[...] = base

    ri = lax.broadcasted_iota(jnp.int32, (8, n_tok), 0)
    pick = lambda a, b: jnp.where(ri == 0, a, jnp.where(ri == 1, b, jnp.zeros_like(a)))
    eid_ref[...] = pick(eid0, eid1)
    gw_ref[...] = pick(w0, w1)
    rank_ref[...] = pick(rank0, rank1).astype(jnp.int32)
    cnt_ref[...] = jnp.broadcast_to(base, cnt_ref.shape).astype(jnp.int32)


def _route_call(logits, bias_col):
    t = logits.shape[0]
    tok = pl.BlockSpec((8, ROUTE_TM), lambda i: (0, i))
    return pl.pallas_call(
        _route_kernel,
        grid=(t // ROUTE_TM,),
        in_specs=[pl.BlockSpec((ROUTE_TM, LOGIT_PAD), lambda i: (i, 0)),
                  pl.BlockSpec((LOGIT_PAD, 1), lambda i: (0, 0))],
        out_specs=[tok, tok, tok, pl.BlockSpec((N_EXPERTS, 128), lambda i: (0, 0))],
        out_shape=[jax.ShapeDtypeStruct((8, t), jnp.int32),
                   jax.ShapeDtypeStruct((8, t), F32),
                   jax.ShapeDtypeStruct((8, t), jnp.int32),
                   jax.ShapeDtypeStruct((N_EXPERTS, 128), jnp.int32)],
        scratch_shapes=[pltpu.VMEM((N_EXPERTS, 1), F32)],
        compiler_params=_cparams(("arbitrary",)),
        name="route",
    )(logits, bias_col)


def _moe_layout(n_pairs):
    padded_rows = -(-(n_pairs + N_EXPERTS * (MOE_ROW_BLOCK - 1)) // MOE_ROW_BLOCK) * MOE_ROW_BLOCK
    n_items = -(-padded_rows // MOE_ROW_GROUP) + N_EXPERTS
    return padded_rows, n_items


def _routing_tables(eid, rank, counts, n_items):
    padded = (counts + MOE_ROW_BLOCK - 1) // MOE_ROW_BLOCK * MOE_ROW_BLOCK
    pad_end = jnp.cumsum(padded)
    pad_start = pad_end - padded
    hot = eid[..., None] == jnp.arange(N_EXPERTS, dtype=jnp.int32)
    dest = (jnp.sum(jnp.where(hot, pad_start, 0), axis=-1) + rank).astype(jnp.int32)
    tail = jnp.where(padded > counts, pad_end - MOE_ROW_BLOCK, -1).astype(jnp.int32)
    per_expert = (padded + MOE_ROW_GROUP - 1) // MOE_ROW_GROUP
    item_end = jnp.cumsum(per_expert)
    item_start = item_end - per_expert
    total = item_end[-1]
    ii = jnp.arange(n_items, dtype=jnp.int32)
    e_of = jnp.minimum(jnp.searchsorted(item_end, ii, side="right"), N_EXPERTS - 1).astype(jnp.int32)
    valid = ii < total
    e_last = e_of[jnp.maximum(total - 1, 0)]
    local = ii - item_start[e_of]
    row0 = pad_start[e_of] + local * MOE_ROW_GROUP
    nblk = jnp.clip((padded[e_of] - local * MOE_ROW_GROUP) // MOE_ROW_BLOCK, 0, MOE_ROW_GROUP // MOE_ROW_BLOCK)
    item_e = jnp.where(valid, e_of, e_last).astype(jnp.int32)
    item_row0 = jnp.where(valid, row0, 0).astype(jnp.int32)
    item_nblk = jnp.where(valid, nblk, 0).astype(jnp.int32)
    return dest, tail, item_e, item_row0, item_nblk


DISPATCH_TOKENS = 256
ROW_DMA_GROUP = 8


def _row_copy(src, s, dst, d, sem):
    return pltpu.make_async_copy(src.at[pl.ds(s, 1), :], dst.at[pl.ds(d, 1), :], sem)


def _dispatch_kernel(dest_ref, tail_ref, h_ctx, h_lat, xs, zero_buf, sem, *, n_ctx, n_tok):
    step = pl.program_id(0)
    tail_copy = lambda e: pltpu.make_async_copy(
        zero_buf, xs.at[pl.ds(pl.multiple_of(tail_ref[e], MOE_ROW_BLOCK), MOE_ROW_BLOCK), :], sem.at[1])

    @pl.when(step == 0)
    def _():
        zero_buf[...] = jnp.zeros_like(zero_buf)
        for e in range(N_EXPERTS):
            @pl.when(tail_ref[e] >= 0)
            def _():
                tail_copy(e).start()
        for e in range(N_EXPERTS):
            @pl.when(tail_ref[e] >= 0)
            def _():
                tail_copy(e).wait()

    tok0 = step * DISPATCH_TOKENS

    def scatter(src):
        def issue(g, carry):
            base = pl.multiple_of(g * ROW_DMA_GROUP, ROW_DMA_GROUP)
            for j in range(ROW_DMA_GROUP):
                _row_copy(src, base + j, xs, dest_ref[tok0 + base + j], sem.at[0]).start()
                _row_copy(src, base + j, xs, dest_ref[n_tok + tok0 + base + j], sem.at[0]).start()
            return carry

        def drain(g, carry):
            for j in range(2 * ROW_DMA_GROUP):
                _row_copy(src, 0, xs, 0, sem.at[0]).wait()
            return carry

        lax.fori_loop(0, DISPATCH_TOKENS // ROW_DMA_GROUP, issue, 0)
        lax.fori_loop(0, DISPATCH_TOKENS // ROW_DMA_GROUP, drain, 0)

    @pl.when(tok0 < n_ctx)
    def _():
        scatter(h_ctx)

    @pl.when(tok0 >= n_ctx)
    def _():
        scatter(h_lat)


def _dispatch_call(dest_flat, tail, h_ctx, h_lat, padded_rows):
    n_ctx, d = h_ctx.shape
    n_tok = n_ctx + h_lat.shape[0]
    ctx_tiles = n_ctx // DISPATCH_TOKENS
    return pl.pallas_call(
        functools.partial(_dispatch_kernel, n_ctx=n_ctx, n_tok=n_tok),
        grid_spec=pltpu.PrefetchScalarGridSpec(
            num_scalar_prefetch=2,
            grid=(n_tok // DISPATCH_TOKENS,),
            in_specs=[pl.BlockSpec((DISPATCH_TOKENS, d), lambda i, dr, tr: (jnp.minimum(i, ctx_tiles - 1), 0)),
                      pl.BlockSpec((DISPATCH_TOKENS, d), lambda i, dr, tr: (jnp.maximum(i - ctx_tiles, 0), 0))],
            out_specs=pl.BlockSpec(memory_space=pl.ANY),
            scratch_shapes=[pltpu.VMEM((MOE_ROW_BLOCK, d), F32), pltpu.SemaphoreType.DMA((2,))]),
        out_shape=jax.ShapeDtypeStruct((padded_rows, d), F32),
        compiler_params=_cparams(("arbitrary",)),
        name="dispatch",
    )(dest_flat, tail, h_ctx, h_lat)


def _moe_kernel(item_e, item_row0, item_nblk, xs, w1_ref, w3_ref, w2_ref, ys,
                x_in, x_bf, acc, w1_bf, w3_bf, w2_bf, sem):
    i = pl.program_id(0)
    c = pl.program_id(1)
    n_items = pl.num_programs(0)
    last_c = pl.num_programs(1) - 1
    nblk = item_nblk[i]
    max_blk = MOE_ROW_GROUP // MOE_ROW_BLOCK
    blk = lambda b: pl.ds(b * MOE_ROW_BLOCK, MOE_ROW_BLOCK)

    def rows_of(item, b):
        return pl.ds(pl.multiple_of(item_row0[item], MOE_ROW_BLOCK) + b * MOE_ROW_BLOCK, MOE_ROW_BLOCK)

    load = lambda item, b: pltpu.make_async_copy(xs.at[rows_of(item, b), :], x_in.at[blk(b), :], sem.at[0])
    store = lambda item, b: pltpu.make_async_copy(acc.at[blk(b), :], ys.at[rows_of(item, b), :], sem.at[1])

    def for_blocks(item, fn):
        n = item_nblk[item]
        for b in range(max_blk):
            @pl.when(b < n)
            def _():
                fn(item, b)

    @pl.when(c == 0)
    def _():
        @pl.when(i == 0)
        def _():
            for_blocks(0, lambda it, b: load(it, b).start())

        for_blocks(i, lambda it, b: load(it, b).wait())

        def cast(it, b):
            x_bf[blk(b), :] = x_in[blk(b), :].astype(BF16)

        for_blocks(i, cast)

        @pl.when(i + 1 < n_items)
        def _():
            for_blocks(i + 1, lambda it, b: load(it, b).start())

        @pl.when(i > 0)
        def _():
            for_blocks(i - 1, lambda it, b: store(it, b).wait())

        def clear(it, b):
            acc[blk(b), :] = jnp.zeros((MOE_ROW_BLOCK, acc.shape[1]), F32)

        for_blocks(i, clear)

    @pl.when(nblk > 0)
    def _():
        w1_bf[...] = w1_ref[0].astype(BF16)
        w3_bf[...] = w3_ref[0].astype(BF16)
        w2_bf[...] = w2_ref[0].astype(BF16)

        def body(b, carry):
            rows = pl.ds(pl.multiple_of(b * MOE_ROW_BLOCK, MOE_ROW_BLOCK), MOE_ROW_BLOCK)
            x = x_bf[rows, :]
            h1 = _dot(x, w1_bf[...])
            h3 = _dot(x, w3_bf[...])
            a = (h1 * jax.nn.sigmoid(h1) * h3).astype(BF16)
            acc[rows, :] += _dot(a, w2_bf[...])
            return carry

        lax.fori_loop(0, nblk, body, 0)

    @pl.when(c == last_c)
    def _():
        for_blocks(i, lambda it, b: store(it, b).start())

        @pl.when(i == n_items - 1)
        def _():
            for_blocks(i, lambda it, b: store(it, b).wait())


def _moe_call(item_e, item_row0, item_nblk, xs, w1, w3, w2):
    padded_rows, d = xs.shape
    n_items = item_e.shape[0]
    f = w1.shape[2]
    nc = f // MOE_F_CHUNK
    chunk = lambda i, c, ib: jnp.where(ib[i] > 0, c, nc - 1)
    return pl.pallas_call(
        _moe_kernel,
        grid_spec=pltpu.PrefetchScalarGridSpec(
            num_scalar_prefetch=3,
            grid=(n_items, nc),
            in_specs=[pl.BlockSpec(memory_space=pl.ANY),
                      pl.BlockSpec((1, d, MOE_F_CHUNK), lambda i, c, ie, ir, ib: (ie[i], 0, chunk(i, c, ib))),
                      pl.BlockSpec((1, d, MOE_F_CHUNK), lambda i, c, ie, ir, ib: (ie[i], 0, chunk(i, c, ib))),
                      pl.BlockSpec((1, MOE_F_CHUNK, d), lambda i, c, ie, ir, ib: (ie[i], chunk(i, c, ib), 0))],
            out_specs=pl.BlockSpec(memory_space=pl.ANY),
            scratch_shapes=[pltpu.VMEM((MOE_ROW_GROUP, d), F32),
                            pltpu.VMEM((MOE_ROW_GROUP, d), BF16),
                            pltpu.VMEM((MOE_ROW_GROUP, d), F32),
                            pltpu.VMEM((d, MOE_F_CHUNK), BF16),
                            pltpu.VMEM((d, MOE_F_CHUNK), BF16),
                            pltpu.VMEM((MOE_F_CHUNK, d), BF16),
                            pltpu.SemaphoreType.DMA((2,))]),
        out_shape=jax.ShapeDtypeStruct((padded_rows, d), F32),
        compiler_params=_cparams(("arbitrary", "arbitrary")),
        name="moe",
    )(item_e, item_row0, item_nblk, xs, w1, w3, w2)


COMBINE_TM = 256


def _combine_kernel(dest_ref, x1_ref, gw_ref, g2_ref, ys, o_ref, y0, y1, sem, *, tok_base, n_tok):
    tok0 = tok_base + pl.program_id(0) * COMBINE_TM

    def issue(g, carry):
        base = pl.multiple_of(g * ROW_DMA_GROUP, ROW_DMA_GROUP)
        for j in range(ROW_DMA_GROUP):
            _row_copy(ys, dest_ref[tok0 + base + j], y0, base + j, sem).start()
            _row_copy(ys, dest_ref[n_tok + tok0 + base + j], y1, base + j, sem).start()
        return carry

    def drain(g, carry):
        for j in range(ROW_DMA_GROUP):
            _row_copy(ys, 0, y0, 0, sem).wait()
            _row_copy(ys, 0, y1, 0, sem).wait()
        return carry

    lax.fori_loop(0, COMBINE_TM // ROW_DMA_GROUP, issue, 0)
    lax.fori_loop(0, COMBINE_TM // ROW_DMA_GROUP, drain, 0)
    gw = gw_ref[...]
    moe = gw[:, 0:1] * y0[...] + gw[:, 1:2] * y1[...]
    o_ref[...] = x1_ref[...] + g2_ref[0] * moe


def _combine_call(dest_flat, x1, gw, g2, ys, tok_base, n_tok, tiles_per_group):
    m, d = x1.shape
    return pl.pallas_call(
        functools.partial(_combine_kernel, tok_base=tok_base, n_tok=n_tok),
        grid_spec=pltpu.PrefetchScalarGridSpec(
            num_scalar_prefetch=1,
            grid=(m // COMBINE_TM,),
            in_specs=[pl.BlockSpec((COMBINE_TM, d), lambda i, dr: (i, 0)),
                      pl.BlockSpec((COMBINE_TM, 2), lambda i, dr: (i, 0)),
                      pl.BlockSpec((1, 1, d), lambda i, dr: (i // tiles_per_group, 0, 0)),
                      pl.BlockSpec(memory_space=pl.ANY)],
            out_specs=pl.BlockSpec((COMBINE_TM, d), lambda i, dr: (i, 0)),
            scratch_shapes=[pltpu.VMEM((COMBINE_TM, d), F32), pltpu.VMEM((COMBINE_TM, d), F32),
                            pltpu.SemaphoreType.DMA]),
        out_shape=jax.ShapeDtypeStruct((m, d), F32),
        compiler_params=_cparams(("arbitrary",)),
        name="combine",
    )(dest_flat, x1, gw, g2, ys)


def kernel(x_prompt, x_sample, cache_a_k, cache_a_v, cache_b_k, cache_b_v, c, c_ctx, norm1_w, norm2_w, w_ada, b_ada, w_in, qn_a, kn_a, qn_b, kn_b, rpb_a, sink_b, w_pa, w_pb, w_out, w_rg, b_rg, w_re, b_re, w1, w3, w2):
    batch, seq, d = x_prompt.shape
    dec_batch, dec_seq, _ = x_sample.shape
    depth = norm1_w.shape[0]
    assert depth == 1, "one trunk layer"
    past = cache_a_k.shape[2]
    n_ctx, n_lat = batch * seq, dec_batch * dec_seq
    n_tok = n_ctx + n_lat

    xc = x_prompt.reshape(n_ctx, d)
    xl = x_sample.reshape(n_lat, d)

    cond = jnp.concatenate([c_ctx[None, :], c], axis=0)
    mod = _ada_call(cond, w_ada[0], b_ada[0][None, :])
    sh1, sc1, g1, sh2, sc2, g2 = [mod[:, i * d:(i + 1) * d][:, None, :] for i in range(6)]
    ctx_rows, lat_rows = slice(0, 1), slice(1, 1 + dec_batch)

    nw1, nw2 = norm1_w[0][None, :], norm2_w[0][None, :]
    qna, kna, qnb, knb = qn_a[0][None, :], kn_a[0][None, :], qn_b[0][None, :], kn_b[0][None, :]
    sink = sink_b[0]
    w_in_bf = w_in[0].astype(BF16)
    wpa_bf, wpb_bf, wout_bf = w_pa[0].astype(BF16), w_pb[0].astype(BF16), w_out[0].astype(BF16)
    wr = jnp.zeros((d, LOGIT_PAD), F32).at[:, :N_EXPERTS].set(w_re[0]).at[:, N_EXPERTS:N_EXPERTS + N_GROUPS].set(w_rg[0])
    br = jnp.zeros((LOGIT_PAD, 1), F32).at[:N_EXPERTS, 0].set(b_re[0]).at[N_EXPERTS:N_EXPERTS + N_GROUPS, 0].set(b_rg[0])

    proj_c, gates_c = _inproj_call(xc, nw1, sc1[ctx_rows], sh1[ctx_rows], w_in_bf, n_ctx // INPROJ_TM)
    proj_l, gates_l = _inproj_call(xl, nw1, sc1[lat_rows], sh1[lat_rows], w_in_bf, dec_seq // INPROJ_TM)

    oa_c, ob_c, new_a_k, new_a_v, new_b_k, new_b_v = _ctx_attn_call(proj_c, seq, qna, kna, qnb, knb, sink)

    bias_blocks, pair0 = _na_bias_blocks(rpb_a[0], dec_seq)
    cos, sin_a, sin_b = _rope_tables(dec_seq)
    rows_of = lambda cache: cache.reshape(-1, HEAD_DIM)
    oa_l = _lat_attn_a_call(proj_l, dec_seq, past, rows_of(cache_a_k), rows_of(cache_a_v),
                            bias_blocks, pair0, qna, kna)
    ob_l = _lat_attn_b_call(proj_l, dec_seq, past, rows_of(cache_b_k), rows_of(cache_b_v),
                            cos, sin_a, sin_b, qnb, knb, sink)

    mix_c = _mix_call(oa_c, ob_c, gates_c, wpa_bf, wpb_bf)
    mix_l = _mix_call(oa_l, ob_l, gates_l, wpa_bf, wpb_bf)
    x1_c, h2_c, lg_c = _outproj_call(mix_c, wout_bf, xc, g1[ctx_rows], sc2[ctx_rows], sh2[ctx_rows],
                                     nw2, wr, n_ctx // MERGE_TM)
    x1_l, h2_l, lg_l = _outproj_call(mix_l, wout_bf, xl, g1[lat_rows], sc2[lat_rows], sh2[lat_rows],
                                     nw2, wr, dec_seq // MERGE_TM)

    eid, gw, rank, cnt = _route_call(jnp.concatenate([lg_c, lg_l], axis=0), br)
    padded_rows, n_items = _moe_layout(2 * n_tok)
    dest, tail, item_e, item_row0, item_nblk = _routing_tables(eid[:2], rank[:2], cnt[:, 0], n_items)
    dest_flat = dest.reshape(-1)
    xs = _dispatch_call(dest_flat, tail, h2_c, h2_l, padded_rows)
    ys = _moe_call(item_e, item_row0, item_nblk, xs, w1[0], w3[0], w2[0])
    gw_t = gw[:2].T
    y_c = _combine_call(dest_flat, x1_c, gw_t[:n_ctx], g2[ctx_rows], ys, 0, n_tok, n_ctx // COMBINE_TM)
    y_l = _combine_call(dest_flat, x1_l, gw_t[n_ctx:], g2[lat_rows], ys, n_ctx, n_tok, dec_seq // COMBINE_TM)

    state = lambda a, heads: a.reshape(batch, 1, seq, heads, HEAD_DIM)
    return (y_c.reshape(batch, seq, d), y_l.reshape(dec_batch, dec_seq, d),
            state(new_a_k, NA_HEADS), state(new_a_v, NA_HEADS),
            state(new_b_k, NB_KV_HEADS), state(new_b_v, NB_KV_HEADS))
```

```python
import functools

import jax
import jax.numpy as jnp
import numpy as np
from jax import lax
from jax.experimental import pallas as pl
from jax.experimental.pallas import tpu as pltpu

D_MODEL = 2048
HEAD_DIM = 128
NA_HEADS = 8
NA_WIDTH = NA_HEADS * HEAD_DIM
NB_Q_HEADS = 8
NB_KV_HEADS = 2
NB_GROUP = NB_Q_HEADS // NB_KV_HEADS
NB_WIDTH = NB_Q_HEADS * HEAD_DIM
NB_KV_WIDTH = NB_KV_HEADS * HEAD_DIM
GRID_W = 64
NA_WIN_ROWS = 8
NA_WIN_COLS = 16
WINDOW = 128
N_GROUPS = 4
EXPERTS_PER_GROUP = 8
N_EXPERTS = N_GROUPS * EXPERTS_PER_GROUP
D_EXPERT = 1024
IN_WIDTH = 3 * NA_WIDTH + NB_WIDTH + 2 * NB_KV_WIDTH + 2 * D_MODEL
ROPE_BASE = 10000.0
NORM_EPS = 1e-6
NEG_INF = -1e30
ATTN_SCALE = HEAD_DIM ** -0.5

QA_HEAD0 = 0
KA_HEAD0 = NA_HEADS
VA_HEAD0 = 2 * NA_HEADS
QB_HEAD0 = 3 * NA_HEADS
KB_HEAD0 = QB_HEAD0 + NB_Q_HEADS
VB_HEAD0 = KB_HEAD0 + NB_KV_HEADS
GATE_COL0 = (VB_HEAD0 + NB_KV_HEADS) * HEAD_DIM

LOGIT_PAD = 128
MOE_ROW_BLOCK = 256
MOE_ROW_GROUP = 1024
MOE_F_CHUNK = 512
VMEM_LIMIT = 56 * 1024 * 1024

F32 = jnp.float32
BF16 = jnp.bfloat16


def _cparams(sem):
    return pltpu.CompilerParams(dimension_semantics=sem, vmem_limit_bytes=VMEM_LIMIT)


def _rms(x, w):
    return x * lax.rsqrt(jnp.mean(x * x, axis=-1, keepdims=True) + NORM_EPS) * w


def _dot(a, b):
    return jnp.dot(a, b, preferred_element_type=F32)


def _dot_nt(a, b):
    return lax.dot_general(a, b, (((1,), (1,)), ((), ())), preferred_element_type=F32)


def _pack_halves(x):
    n = x.shape[1] // 2
    lo = lax.bitcast_convert_type(x[:, :n].astype(BF16).astype(F32), jnp.uint32)
    hi = lax.bitcast_convert_type(x[:, n:].astype(BF16).astype(F32), jnp.uint32)
    return hi | (lo >> 16)


def _unpack_halves(w):
    lo = lax.bitcast_convert_type(w << 16, F32)
    hi = lax.bitcast_convert_type(w & jnp.uint32(0xFFFF0000), F32)
    return jnp.concatenate([lo, hi], axis=1)


def _dot_split(a, b):
    a_hi = a.astype(BF16)
    a_lo = (a - a_hi.astype(F32)).astype(BF16)
    b_hi = b.astype(BF16)
    b_lo = (b - b_hi.astype(F32)).astype(BF16)
    return _dot(a_hi, b_hi) + (_dot(a_lo, b_hi) + _dot(a_hi, b_lo))


ADA_ROWS = 8
ADA_TN = 1024


def _ada_kernel(c_ref, w_ref, b_ref, o_ref):
    n_rows, d, lanes = c_ref.shape
    tn = w_ref.shape[1]

    def body(kb, acc):
        ks = pl.ds(pl.multiple_of(kb * 8, 8), 8)
        w = w_ref[ks, :]
        out = []
        for r in range(n_rows):
            c = c_ref[r, ks, :]
            s = c * jax.nn.sigmoid(c)
            out.append(acc[r] + w * jnp.concatenate([s] * (tn // lanes), axis=1))
        return tuple(out)

    acc = lax.fori_loop(0, d // 8, body, tuple(jnp.zeros((8, tn), F32) for _ in range(n_rows)), unroll=8)
    ri = lax.broadcasted_iota(jnp.int32, (ADA_ROWS, tn), 0)
    res = jnp.zeros((ADA_ROWS, tn), F32)
    for r in range(n_rows):
        row = jnp.sum(acc[r], axis=0, keepdims=True) + b_ref[...]
        res = jnp.where(ri == r, row, res)
    o_ref[...] = res


def _ada_call(cond, w_ada, b_ada):
    n_rows, d = cond.shape
    n = w_ada.shape[1]
    lanes = 128
    cond_lanes = jnp.broadcast_to(cond[:, :, None], (n_rows, d, lanes))
    return pl.pallas_call(
        _ada_kernel,
        grid=(n // ADA_TN,),
        in_specs=[pl.BlockSpec((n_rows, d, lanes), lambda j: (0, 0, 0)),
                  pl.BlockSpec((d, ADA_TN), lambda j: (0, j)),
                  pl.BlockSpec((1, ADA_TN), lambda j: (0, j))],
        out_specs=pl.BlockSpec((ADA_ROWS, ADA_TN), lambda j: (0, j)),
        out_shape=jax.ShapeDtypeStruct((ADA_ROWS, n), F32),
        compiler_params=_cparams(("arbitrary",)),
        name="ada",
    )(cond_lanes, w_ada, b_ada)


INPROJ_TM = 1024
INPROJ_TN = 512
NORM_ROWS = 128


def _inproj_kernel(x_ref, nw_ref, sc_ref, sh_ref, w_ref, qkv_ref, gate_ref, h_scr, *, qkv_tiles):
    j = pl.program_id(1)

    @pl.when(j == 0)
    def _():
        nw = nw_ref[...]
        sc = 1.0 + sc_ref[0]
        sh = sh_ref[0]

        def body(r, carry):
            rows = pl.ds(pl.multiple_of(r * NORM_ROWS, NORM_ROWS), NORM_ROWS)
            h_scr[rows, :] = (_rms(x_ref[rows, :], nw) * sc + sh).astype(BF16)
            return carry

        lax.fori_loop(0, INPROJ_TM // NORM_ROWS, body, 0)

    res = _dot(h_scr[...], w_ref[...])

    @pl.when(j < qkv_tiles)
    def _():
        qkv_ref[...] = res

    @pl.when(j >= qkv_tiles)
    def _():
        gate_ref[...] = res.astype(gate_ref.dtype)


def _inproj_call(x, nw, sc, sh, w_bf, tiles_per_group):
    m, d = x.shape
    n = w_bf.shape[1]
    qkv_tiles = GATE_COL0 // INPROJ_TN
    grp = lambda i, j: (i // tiles_per_group, 0, 0)
    return pl.pallas_call(
        functools.partial(_inproj_kernel, qkv_tiles=qkv_tiles),
        grid=(m // INPROJ_TM, n // INPROJ_TN),
        in_specs=[pl.BlockSpec((INPROJ_TM, d), lambda i, j: (i, 0)),
                  pl.BlockSpec((1, d), lambda i, j: (0, 0)),
                  pl.BlockSpec((1, 1, d), grp),
                  pl.BlockSpec((1, 1, d), grp),
                  pl.BlockSpec((d, INPROJ_TN), lambda i, j: (0, j))],
        out_specs=[pl.BlockSpec((INPROJ_TM, INPROJ_TN), lambda i, j: (i, jnp.minimum(j, qkv_tiles - 1))),
                   pl.BlockSpec((INPROJ_TM, INPROJ_TN), lambda i, j: (i, jnp.maximum(j - qkv_tiles, 0)))],
        out_shape=[jax.ShapeDtypeStruct((m, GATE_COL0), F32),
                   jax.ShapeDtypeStruct((m, n - GATE_COL0), BF16)],
        scratch_shapes=[pltpu.VMEM((INPROJ_TM, d), BF16)],
        compiler_params=_cparams(("arbitrary", "arbitrary")),
        name="inproj",
    )(x, nw, sc, sh, w_bf)


def _softmax_pv(scores, values, sink=None):
    m = None
    for s in scores:
        ms = jnp.max(s, axis=-1, keepdims=True)
        m = ms if m is None else jnp.maximum(m, ms)
    if sink is not None:
        m = jnp.maximum(m, sink)
    den = None
    acc = None
    for s, v in zip(scores, values):
        e = jnp.exp(s - m)
        ds = jnp.sum(e, axis=-1, keepdims=True)
        den = ds if den is None else den + ds
        pv = _dot(e.astype(BF16), v)
        acc = pv if acc is None else acc + pv
    if sink is not None:
        den = den + jnp.exp(sink - m)
    return acc / den


def _ctx_attn_kernel(qa_ref, ka_ref, va_ref, qb_ref, kb_ref, vb_ref,
                     qna_ref, kna_ref, qnb_ref, knb_ref, sink_ref,
                     oa_ref, ob_ref, nak_ref, nav_ref, nbk_ref, nbv_ref):
    qna, kna, qnb, knb = qna_ref[...], kna_ref[...], qnb_ref[...], knb_ref[...]
    seq = qa_ref.shape[0]
    for h in range(NA_HEADS):
        cols = slice(h * HEAD_DIM, (h + 1) * HEAD_DIM)
        q = (_rms(qa_ref[:, cols], qna) * ATTN_SCALE).astype(BF16)
        k = _rms(ka_ref[:, cols], kna)
        v = va_ref[:, cols]
        nak_ref[pl.ds(h, seq, stride=NA_HEADS), :] = k
        nav_ref[pl.ds(h, seq, stride=NA_HEADS), :] = v
        s = _dot_nt(q, k.astype(BF16))
        oa_ref[:, cols] = _softmax_pv([s], [v.astype(BF16)]).astype(oa_ref.dtype)
    for kv in range(NB_KV_HEADS):
        kcols = slice(kv * HEAD_DIM, (kv + 1) * HEAD_DIM)
        k = _rms(kb_ref[:, kcols], knb)
        v = vb_ref[:, kcols]
        nbk_ref[pl.ds(kv, seq, stride=NB_KV_HEADS), :] = k
        nbv_ref[pl.ds(kv, seq, stride=NB_KV_HEADS), :] = v
        kb16 = k.astype(BF16)
        vb16 = v.astype(BF16)
        for g in range(NB_GROUP):
            hq = kv * NB_GROUP + g
            cols = slice(hq * HEAD_DIM, (hq + 1) * HEAD_DIM)
            q = (_rms(qb_ref[:, cols], qnb) * ATTN_SCALE).astype(BF16)
            s = _dot_nt(q, kb16)
            ob_ref[:, cols] = _softmax_pv([s], [vb16], sink=sink_ref[hq]).astype(ob_ref.dtype)


def _ctx_attn_call(proj, seq, qna, kna, qnb, knb, sink):
    m = proj.shape[0]
    nb = m // seq
    wide = lambda blk: pl.BlockSpec((seq, NA_WIDTH), lambda b: (b, blk))
    narrow = lambda blk: pl.BlockSpec((seq, NB_KV_WIDTH), lambda b: (b, blk))
    vec = pl.BlockSpec((1, HEAD_DIM), lambda b: (0, 0))
    return pl.pallas_call(
        _ctx_attn_kernel,
        grid=(nb,),
        in_specs=[wide(QA_HEAD0 // NA_HEADS), wide(KA_HEAD0 // NA_HEADS), wide(VA_HEAD0 // NA_HEADS),
                  wide(QB_HEAD0 // NA_HEADS), narrow(KB_HEAD0 // NB_KV_HEADS), narrow(VB_HEAD0 // NB_KV_HEADS),
                  vec, vec, vec, vec,
                  pl.BlockSpec(memory_space=pltpu.SMEM)],
        out_specs=[pl.BlockSpec((seq, NA_WIDTH), lambda b: (b, 0)),
                   pl.BlockSpec((seq, NB_WIDTH), lambda b: (b, 0)),
                   pl.BlockSpec((seq * NA_HEADS, HEAD_DIM), lambda b: (b, 0)),
                   pl.BlockSpec((seq * NA_HEADS, HEAD_DIM), lambda b: (b, 0)),
                   pl.BlockSpec((seq * NB_KV_HEADS, HEAD_DIM), lambda b: (b, 0)),
                   pl.BlockSpec((seq * NB_KV_HEADS, HEAD_DIM), lambda b: (b, 0))],
        out_shape=[jax.ShapeDtypeStruct((m, NA_WIDTH), BF16),
                   jax.ShapeDtypeStruct((m, NB_WIDTH), BF16),
                   jax.ShapeDtypeStruct((m * NA_HEADS, HEAD_DIM), F32),
                   jax.ShapeDtypeStruct((m * NA_HEADS, HEAD_DIM), F32),
                   jax.ShapeDtypeStruct((m * NB_KV_HEADS, HEAD_DIM), F32),
                   jax.ShapeDtypeStruct((m * NB_KV_HEADS, HEAD_DIM), F32)],
        compiler_params=_cparams(("arbitrary",)),
        name="ctx_attn",
    )(proj, proj, proj, proj, proj, proj, qna, kna, qnb, knb, sink)


def _rope(x, cos, sin_a, sin_b):
    quarter = HEAD_DIM // 4
    return (x * cos + pltpu.roll(x, HEAD_DIM - quarter, 1) * sin_a
            + pltpu.roll(x, quarter, 1) * sin_b)


def _head_rows(cache_ref, head, n_heads):
    past = cache_ref.shape[0] // n_heads
    return cache_ref[pl.ds(head, past, stride=n_heads), :]


def _lat_attn_a_kernel(q_ref, k_ref, v_ref, ck_ref, cv_ref, cb_ref, qn_ref, kn_ref, o_ref, bias_scr,
                       *, pair0):
    head = pl.program_id(0)

    @pl.when(pl.program_id(1) == 0)
    def _():
        bias_scr[...] = jnp.full(bias_scr.shape, NEG_INF, F32)
        for qr, p0 in enumerate(pair0):
            for j in range(cb_ref.shape[1]):
                bias_scr[qr * GRID_W:(qr + 1) * GRID_W, (p0 + j) * 2 * GRID_W:(p0 + j + 1) * 2 * GRID_W] = cb_ref[qr, j]

    q = (_rms(q_ref[...], qn_ref[...]) * ATTN_SCALE).astype(BF16)
    k = _rms(k_ref[...], kn_ref[...]).astype(BF16)
    s_lat = _dot_nt(q, k) + bias_scr[...]
    ck = _head_rows(ck_ref, head, NA_HEADS).astype(BF16)
    cv = _head_rows(cv_ref, head, NA_HEADS).astype(BF16)
    s_ctx = _dot_nt(q, ck)
    o = _softmax_pv([s_lat, s_ctx], [v_ref[...].astype(BF16), cv])
    o_ref[...] = o.astype(o_ref.dtype)


def _lat_attn_b_kernel(q_ref, k_ref, v_ref, ck_ref, cv_ref, cos_ref, sina_ref, sinb_ref,
                       qn_ref, kn_ref, sink_ref, o_ref):
    cos, sin_a, sin_b = cos_ref[...], sina_ref[...], sinb_ref[...]
    q = _rope(_rms(q_ref[...], qn_ref[...]), cos, sin_a, sin_b)
    k = _rope(_rms(k_ref[...], kn_ref[...]), cos, sin_a, sin_b)
    q = (q * ATTN_SCALE).astype(BF16)
    s_lat = _dot_nt(q, k.astype(BF16))
    length = s_lat.shape[0]
    qi = lax.broadcasted_iota(jnp.int32, (length, length), 0)
    kj = lax.broadcasted_iota(jnp.int32, (length, length), 1)
    s_lat = jnp.where(jnp.abs(qi - kj) <= WINDOW, s_lat, NEG_INF)
    kv = pl.program_id(0) // NB_GROUP
    ck = _head_rows(ck_ref, kv, NB_KV_HEADS).astype(BF16)
    cv = _head_rows(cv_ref, kv, NB_KV_HEADS).astype(BF16)
    s_ctx = _dot_nt(q, ck)
    o = _softmax_pv([s_lat, s_ctx], [v_ref[...].astype(BF16), cv], sink=sink_ref[pl.program_id(0)])
    o_ref[...] = o.astype(o_ref.dtype)


def _lat_attn_a_call(proj, length, past, ck, cv, cb, pair0, qn, kn):
    m = proj.shape[0]
    head = lambda h0: pl.BlockSpec((length, HEAD_DIM), lambda h, b: (b, h0 + h))
    cache = pl.BlockSpec((past * NA_HEADS, HEAD_DIM), lambda h, b: (b, 0))
    vec = pl.BlockSpec((1, HEAD_DIM), lambda h, b: (0, 0))
    return pl.pallas_call(
        functools.partial(_lat_attn_a_kernel, pair0=pair0),
        grid=(NA_HEADS, m // length),
        in_specs=[head(QA_HEAD0), head(KA_HEAD0), head(VA_HEAD0), cache, cache,
                  pl.BlockSpec((None,) + cb.shape[1:], lambda h, b: (h, 0, 0, 0, 0)), vec, vec],
        out_specs=pl.BlockSpec((length, HEAD_DIM), lambda h, b: (b, h)),
        out_shape=jax.ShapeDtypeStruct((m, NA_WIDTH), BF16),
        scratch_shapes=[pltpu.VMEM((length, length), F32)],
        compiler_params=_cparams(("arbitrary", "arbitrary")),
        name="lat_attn_a",
    )(proj, proj, proj, ck, cv, cb, qn, kn)


def _lat_attn_b_call(proj, length, past, ck, cv, cos, sin_a, sin_b, qn, kn, sink):
    m = proj.shape[0]
    qspec = pl.BlockSpec((length, HEAD_DIM), lambda h, b: (b, QB_HEAD0 + h))
    kvspec = lambda h0: pl.BlockSpec((length, HEAD_DIM), lambda h, b: (b, h0 + h // NB_GROUP))
    cache = pl.BlockSpec((past * NB_KV_HEADS, HEAD_DIM), lambda h, b: (b, 0))
    table = pl.BlockSpec((length, HEAD_DIM), lambda h, b: (0, 0))
    vec = pl.BlockSpec((1, HEAD_DIM), lambda h, b: (0, 0))
    return pl.pallas_call(
        _lat_attn_b_kernel,
        grid=(NB_Q_HEADS, m // length),
        in_specs=[qspec, kvspec(KB_HEAD0), kvspec(VB_HEAD0), cache, cache,
                  table, table, table, vec, vec, pl.BlockSpec(memory_space=pltpu.SMEM)],
        out_specs=pl.BlockSpec((length, HEAD_DIM), lambda h, b: (b, h)),
        out_shape=jax.ShapeDtypeStruct((m, NB_WIDTH), BF16),
        compiler_params=_cparams(("arbitrary", "arbitrary")),
        name="lat_attn_b",
    )(proj, proj, proj, ck, cv, cos, sin_a, sin_b, qn, kn, sink)


def _na_bias_blocks(rpb, length):
    rows = length // GRID_W
    kr_n = min(NA_WIN_ROWS, rows)
    n_pairs = min(kr_n // 2 + 1, rows // 2)
    r = np.arange(rows)
    c = np.arange(GRID_W)
    r0 = np.clip(r - kr_n // 2, 0, rows - kr_n)
    c0 = np.clip(c - NA_WIN_COLS // 2, 0, GRID_W - NA_WIN_COLS)
    pair0 = np.minimum(r0 // 2, rows // 2 - n_pairs)
    kr = 2 * (pair0[:, None, None] + np.arange(n_pairs)[None, :, None]) + np.arange(2)[None, None, :]
    row_ok = (kr >= r0[:, None, None]) & (kr < r0[:, None, None] + kr_n)
    col_ok = (c[None, :] >= c0[:, None]) & (c[None, :] < c0[:, None] + NA_WIN_COLS)
    dr = kr - r[:, None, None] + (NA_WIN_ROWS - 1)
    dc = np.clip(c[None, :] - c[:, None], -(NA_WIN_COLS - 1), NA_WIN_COLS - 1) + (NA_WIN_COLS - 1)
    row_sel = ((dr[None] == np.arange(2 * NA_WIN_ROWS - 1)[:, None, None, None]) & row_ok[None]).astype(np.float32)
    col_hit = (dc[None] == np.arange(2 * NA_WIN_COLS - 1)[:, None, None]) & col_ok[None]
    col_sel = np.zeros((2,) + col_hit.shape[:2] + (2 * GRID_W,), np.float32)
    for half in range(2):
        col_sel[half, :, :, half * GRID_W:(half + 1) * GRID_W] = col_hit
    hi = lax.Precision.HIGHEST
    per_col = jnp.einsum("hde,lecn->hldcn", rpb.astype(F32), col_sel, precision=hi)
    table = jnp.einsum("dqjl,hldcn->hqjcn", row_sel, per_col, precision=hi)
    valid = (row_ok[:, :, None, :, None] & col_ok[None, None, :, None, :]).reshape(rows, n_pairs, GRID_W, 2 * GRID_W)
    return jnp.where(valid[None], table, NEG_INF), tuple(int(p) for p in pair0)


def _rope_tables(length):
    t = jnp.arange(length)
    row = (t // GRID_W).astype(F32)
    col = (t % GRID_W).astype(F32)
    n_freq = HEAD_DIM // 4
    inv = ROPE_BASE ** (-jnp.arange(n_freq, dtype=F32) / n_freq)
    ar = row[:, None] * inv
    ac = col[:, None] * inv
    ang = jnp.concatenate([ar, ar, ac, ac], axis=-1)
    cos, sin = jnp.cos(ang), jnp.sin(ang)
    lane = jnp.arange(HEAD_DIM)
    takes_left = ((lane // n_freq) % 2 == 0)[None, :]
    return cos, jnp.where(takes_left, -sin, 0.0), jnp.where(takes_left, 0.0, sin)


MERGE_TM = 512


def _resident(shape):
    zeros = (0,) * len(shape)
    return pl.BlockSpec(shape, lambda i: zeros, pipeline_mode=pl.Buffered(1))


def _mix_kernel(oa_ref, ob_ref, ga_ref, gb_ref, wpa_ref, wpb_ref, mix_ref):
    ya = _dot(oa_ref[...], wpa_ref[...])
    yb = _dot(ob_ref[...], wpb_ref[...])
    mix = (jax.nn.sigmoid(ga_ref[...].astype(F32)) * ya
           + jax.nn.sigmoid(gb_ref[...].astype(F32)) * yb)
    mix_ref[...] = mix.astype(mix_ref.dtype)


def _mix_call(oa, ob, gates, wpa_bf, wpb_bf):
    m = oa.shape[0]
    d = wpa_bf.shape[1]
    row = lambda i: (i, 0)
    return pl.pallas_call(
        _mix_kernel,
        grid=(m // MERGE_TM,),
        in_specs=[pl.BlockSpec((MERGE_TM, NA_WIDTH), row),
                  pl.BlockSpec((MERGE_TM, NB_WIDTH), row),
                  pl.BlockSpec((MERGE_TM, d), lambda i: (i, 0)),
                  pl.BlockSpec((MERGE_TM, d), lambda i: (i, 1)),
                  _resident((NA_WIDTH, d)), _resident((NB_WIDTH, d))],
        out_specs=pl.BlockSpec((MERGE_TM, d), row),
        out_shape=jax.ShapeDtypeStruct((m, d), BF16),
        compiler_params=_cparams(("arbitrary",)),
        name="mix",
    )(oa, ob, gates, gates, wpa_bf, wpb_bf)


def _outproj_kernel(mix_ref, wout_ref, x_ref, g1_ref, sc2_ref, sh2_ref, n2w_ref, wr_ref,
                    x1_ref, h2_ref, lg_ref):
    x1 = x_ref[...] + g1_ref[0] * _dot(mix_ref[...], wout_ref[...])
    x1_ref[...] = x1
    h2 = _rms(x1, n2w_ref[...]) * (1.0 + sc2_ref[0]) + sh2_ref[0]
    h2_ref[...] = _pack_halves(h2)
    lg_ref[...] = _dot_split(h2, wr_ref[...])


def _outproj_call(mix, wout_bf, x, g1, sc2, sh2, n2w, wr, tiles_per_group):
    m, d = x.shape
    grp = lambda i: (i // tiles_per_group, 0, 0)
    row = lambda i: (i, 0)
    return pl.pallas_call(
        _outproj_kernel,
        grid=(m // MERGE_TM,),
        in_specs=[pl.BlockSpec((MERGE_TM, d), row),
                  _resident((d, d)),
                  pl.BlockSpec((MERGE_TM, d), row),
                  pl.BlockSpec((1, 1, d), grp), pl.BlockSpec((1, 1, d), grp), pl.BlockSpec((1, 1, d), grp),
                  pl.BlockSpec((1, d), lambda i: (0, 0)),
                  _resident((d, LOGIT_PAD))],
        out_specs=[pl.BlockSpec((MERGE_TM, d), row),
                   pl.BlockSpec((MERGE_TM, d // 2), row),
                   pl.BlockSpec((MERGE_TM, LOGIT_PAD), row)],
        out_shape=[jax.ShapeDtypeStruct((m, d), F32),
                   jax.ShapeDtypeStruct((m, d // 2), jnp.uint32),
                   jax.ShapeDtypeStruct((m, LOGIT_PAD), F32)],
        compiler_params=_cparams(("arbitrary",)),
        name="outproj",
    )(mix, wout_bf, x, g1, sc2, sh2, n2w, wr)


ROUTE_TM = 512


def _first_index_of_max(vals, idx, n):
    mx = jnp.max(vals, axis=0, keepdims=True)
    first = jnp.min(jnp.where(vals == mx, idx, n), axis=0, keepdims=True)
    return mx, first


def _route_kernel(lg_ref, bias_ref, eid_ref, gw_ref, rank_ref, cnt_ref, base_ref):
    step = pl.program_id(0)

    @pl.when(step == 0)
    def _():
        base_ref[...] = jnp.zeros_like(base_ref)

    lt = lg_ref[...].T + bias_ref[...]
    n_tok = lt.shape[1]
    le = lt[0:N_EXPERTS]
    lgrp = lt[N_EXPERTS:N_EXPERTS + N_GROUPS]
    gi = lax.broadcasted_iota(jnp.int32, (N_GROUPS, n_tok), 0)
    gmax, gsel = _first_index_of_max(lgrp, gi, N_GROUPS)
    pg_sel = 1.0 / jnp.sum(jnp.exp(lgrp - gmax), axis=0, keepdims=True)
    le_sel = jnp.zeros((EXPERTS_PER_GROUP, n_tok), F32)
    for g in range(N_GROUPS):
        le_sel = jnp.where(gsel == g, le[g * EXPERTS_PER_GROUP:(g + 1) * EXPERTS_PER_GROUP], le_sel)
    ei = lax.broadcasted_iota(jnp.int32, (EXPERTS_PER_GROUP, n_tok), 0)
    v0, i0 = _first_index_of_max(le_sel, ei, EXPERTS_PER_GROUP)
    rest = jnp.where(ei == i0, -jnp.inf, le_sel)
    v1, i1 = _first_index_of_max(rest, ei, EXPERTS_PER_GROUP)
    e1 = jnp.exp(v1 - v0)
    w0 = pg_sel / (1.0 + e1)
    w1 = pg_sel * e1 / (1.0 + e1)
    eid0 = gsel * EXPERTS_PER_GROUP + i0
    eid1 = gsel * EXPERTS_PER_GROUP + i1

    xi = lax.broadcasted_iota(jnp.int32, (N_EXPERTS, n_tok), 0)
    si = lax.broadcasted_iota(jnp.int32, (n_tok, n_tok), 0)
    ti = lax.broadcasted_iota(jnp.int32, (n_tok, n_tok), 1)
    before = (si < ti).astype(BF16)
    base = base_ref[...]
    hot0 = (xi == eid0).astype(F32)
    hot1 = (xi == eid1).astype(F32)
    pre0 = _dot(hot0.astype(BF16), before)
    pre1 = _dot(hot1.astype(BF16), before)
    tot0 = jnp.sum(hot0, axis=1, keepdims=True)
    tot1 = jnp.sum(hot1, axis=1, keepdims=True)
    rank0 = jnp.sum(hot0 * (base + pre0), axis=0, keepdims=True)
    rank1 = jnp.sum(hot1 * (base + tot0 + pre1), axis=0, keepdims=True)
    base = base + tot0 + tot1
    base_ref[...] = base

    ri = lax.broadcasted_iota(jnp.int32, (8, n_tok), 0)
    pick = lambda a, b: jnp.where(ri == 0, a, jnp.where(ri == 1, b, jnp.zeros_like(a)))
    eid_ref[...] = pick(eid0, eid1)
    gw_ref[...] = pick(w0, w1)
    rank_ref[...] = pick(rank0, rank1).astype(jnp.int32)
    cnt_ref[...] = jnp.broadcast_to(base, cnt_ref.shape).astype(jnp.int32)


def _route_call(logits, bias_col):
    t = logits.shape[0]
    tok = pl.BlockSpec((8, ROUTE_TM), lambda i: (0, i))
    return pl.pallas_call(
        _route_kernel,
        grid=(t // ROUTE_TM,),
        in_specs=[pl.BlockSpec((ROUTE_TM, LOGIT_PAD), lambda i: (i, 0)),
                  pl.BlockSpec((LOGIT_PAD, 1), lambda i: (0, 0))],
        out_specs=[tok, tok, tok, pl.BlockSpec((N_EXPERTS, 128), lambda i: (0, 0))],
        out_shape=[jax.ShapeDtypeStruct((8, t), jnp.int32),
                   jax.ShapeDtypeStruct((8, t), F32),
                   jax.ShapeDtypeStruct((8, t), jnp.int32),
                   jax.ShapeDtypeStruct((N_EXPERTS, 128), jnp.int32)],
        scratch_shapes=[pltpu.VMEM((N_EXPERTS, 1), F32)],
        compiler_params=_cparams(("arbitrary",)),
        name="route",
    )(logits, bias_col)


def _moe_layout(n_pairs):
    padded_rows = -(-(n_pairs + N_EXPERTS * (MOE_ROW_BLOCK - 1)) // MOE_ROW_BLOCK) * MOE_ROW_BLOCK
    n_items = -(-padded_rows // MOE_ROW_GROUP) + N_EXPERTS
    return padded_rows, n_items


def _routing_tables(eid, rank, counts, n_items):
    padded = (counts + MOE_ROW_BLOCK - 1) // MOE_ROW_BLOCK * MOE_ROW_BLOCK
    pad_end = jnp.cumsum(padded)
    pad_start = pad_end - padded
    hot = eid[..., None] == jnp.arange(N_EXPERTS, dtype=jnp.int32)
    dest = (jnp.sum(jnp.where(hot, pad_start, 0), axis=-1) + rank).astype(jnp.int32)
    tail = jnp.where(padded > counts, pad_end - MOE_ROW_BLOCK, -1).astype(jnp.int32)
    per_expert = (padded + MOE_ROW_GROUP - 1) // MOE_ROW_GROUP
    item_end = jnp.cumsum(per_expert)
    item_start = item_end - per_expert
    total = item_end[-1]
    ii = jnp.arange(n_items, dtype=jnp.int32)
    e_of = jnp.minimum(jnp.searchsorted(item_end, ii, side="right"), N_EXPERTS - 1).astype(jnp.int32)
    valid = ii < total
    e_last = e_of[jnp.maximum(total - 1, 0)]
    local = ii - item_start[e_of]
    row0 = pad_start[e_of] + local * MOE_ROW_GROUP
    nblk = jnp.clip((padded[e_of] - local * MOE_ROW_GROUP) // MOE_ROW_BLOCK, 0, MOE_ROW_GROUP // MOE_ROW_BLOCK)
    item_e = jnp.where(valid, e_of, e_last).astype(jnp.int32)
    item_row0 = jnp.where(valid, row0, 0).astype(jnp.int32)
    item_nblk = jnp.where(valid, nblk, 0).astype(jnp.int32)
    return dest, tail, item_e, item_row0, item_nblk


DISPATCH_TOKENS = 256
ROW_DMA_GROUP = 8


def _row_copy(src, s, dst, d, sem):
    return pltpu.make_async_copy(src.at[pl.ds(s, 1), :], dst.at[pl.ds(d, 1), :], sem)


def _dispatch_kernel(dest_ref, tail_ref, h_ctx, h_lat, xs, zero_buf, sem, *, n_ctx, n_tok):
    step = pl.program_id(0)
    tail_copy = lambda e: pltpu.make_async_copy(
        zero_buf, xs.at[pl.ds(pl.multiple_of(tail_ref[e], MOE_ROW_BLOCK), MOE_ROW_BLOCK), :], sem.at[1])

    @pl.when(step == 0)
    def _():
        zero_buf[...] = jnp.zeros_like(zero_buf)
        for e in range(N_EXPERTS):
            @pl.when(tail_ref[e] >= 0)
            def _():
                tail_copy(e).start()
        for e in range(N_EXPERTS):
            @pl.when(tail_ref[e] >= 0)
            def _():
                tail_copy(e).wait()

    tok0 = step * DISPATCH_TOKENS

    def scatter(src):
        def issue(g, carry):
            base = pl.multiple_of(g * ROW_DMA_GROUP, ROW_DMA_GROUP)
            for j in range(ROW_DMA_GROUP):
                _row_copy(src, base + j, xs, dest_ref[tok0 + base + j], sem.at[0]).start()
                _row_copy(src, base + j, xs, dest_ref[n_tok + tok0 + base + j], sem.at[0]).start()
            return carry

        def drain(g, carry):
            for j in range(2 * ROW_DMA_GROUP):
                _row_copy(src, 0, xs, 0, sem.at[0]).wait()
            return carry

        lax.fori_loop(0, DISPATCH_TOKENS // ROW_DMA_GROUP, issue, 0)
        lax.fori_loop(0, DISPATCH_TOKENS // ROW_DMA_GROUP, drain, 0)

    @pl.when(tok0 < n_ctx)
    def _():
        scatter(h_ctx)

    @pl.when(tok0 >= n_ctx)
    def _():
        scatter(h_lat)


def _dispatch_call(dest_flat, tail, h_ctx, h_lat, padded_rows):
    n_ctx, d = h_ctx.shape
    n_tok = n_ctx + h_lat.shape[0]
    ctx_tiles = n_ctx // DISPATCH_TOKENS
    return pl.pallas_call(
        functools.partial(_dispatch_kernel, n_ctx=n_ctx, n_tok=n_tok),
        grid_spec=pltpu.PrefetchScalarGridSpec(
            num_scalar_prefetch=2,
            grid=(n_tok // DISPATCH_TOKENS,),
            in_specs=[pl.BlockSpec((DISPATCH_TOKENS, d), lambda i, dr, tr: (jnp.minimum(i, ctx_tiles - 1), 0)),
                      pl.BlockSpec((DISPATCH_TOKENS, d), lambda i, dr, tr: (jnp.maximum(i - ctx_tiles, 0), 0))],
            out_specs=pl.BlockSpec(memory_space=pl.ANY),
            scratch_shapes=[pltpu.VMEM((MOE_ROW_BLOCK, d), h_ctx.dtype), pltpu.SemaphoreType.DMA((2,))]),
        out_shape=jax.ShapeDtypeStruct((padded_rows, d), h_ctx.dtype),
        compiler_params=_cparams(("arbitrary",)),
        name="dispatch",
    )(dest_flat, tail, h_ctx, h_lat)


def _moe_kernel(item_e, item_row0, item_nblk, xs, w1_ref, w3_ref, w2_ref, ys,
                x_in, x_bf, acc, y_out, w1_bf, w3_bf, w2_bf, sem):
    i = pl.program_id(0)
    c = pl.program_id(1)
    n_items = pl.num_programs(0)
    last_c = pl.num_programs(1) - 1
    nblk = item_nblk[i]
    max_blk = MOE_ROW_GROUP // MOE_ROW_BLOCK
    blk = lambda b: pl.ds(b * MOE_ROW_BLOCK, MOE_ROW_BLOCK)

    def rows_of(item, b):
        return pl.ds(pl.multiple_of(item_row0[item], MOE_ROW_BLOCK) + b * MOE_ROW_BLOCK, MOE_ROW_BLOCK)

    load = lambda item, b: pltpu.make_async_copy(xs.at[rows_of(item, b), :], x_in.at[blk(b), :], sem.at[0])
    store = lambda item, b: pltpu.make_async_copy(y_out.at[blk(b), :], ys.at[rows_of(item, b), :], sem.at[1])

    def for_blocks(item, fn):
        n = item_nblk[item]
        for b in range(max_blk):
            @pl.when(b < n)
            def _():
                fn(item, b)

    @pl.when(c == 0)
    def _():
        @pl.when(i == 0)
        def _():
            for_blocks(0, lambda it, b: load(it, b).start())

        for_blocks(i, lambda it, b: load(it, b).wait())

        def cast(it, b):
            x_bf[blk(b), :] = _unpack_halves(x_in[blk(b), :]).astype(BF16)

        for_blocks(i, cast)

        @pl.when(i + 1 < n_items)
        def _():
            for_blocks(i + 1, lambda it, b: load(it, b).start())

        def clear(it, b):
            acc[blk(b), :] = jnp.zeros((MOE_ROW_BLOCK, acc.shape[1]), F32)

        for_blocks(i, clear)

    @pl.when(nblk > 0)
    def _():
        w1_bf[...] = w1_ref[0].astype(BF16)
        w3_bf[...] = w3_ref[0].astype(BF16)
        w2_bf[...] = w2_ref[0].astype(BF16)

        def body(b, carry):
            rows = pl.ds(pl.multiple_of(b * MOE_ROW_BLOCK, MOE_ROW_BLOCK), MOE_ROW_BLOCK)
            x = x_bf[rows, :]
            h1 = _dot(x, w1_bf[...])
            h3 = _dot(x, w3_bf[...])
            a = (h1 * jax.nn.sigmoid(h1) * h3).astype(BF16)
            acc[rows, :] += _dot(a, w2_bf[...])
            return carry

        lax.fori_loop(0, nblk, body, 0)

    @pl.when(c == last_c)
    def _():
        @pl.when(i > 0)
        def _():
            for_blocks(i - 1, lambda it, b: store(it, b).wait())

        def pack(it, b):
            y_out[blk(b), :] = _pack_halves(acc[blk(b), :])

        for_blocks(i, pack)
        for_blocks(i, lambda it, b: store(it, b).start())

        @pl.when(i == n_items - 1)
        def _():
            for_blocks(i, lambda it, b: store(it, b).wait())


def _moe_call(item_e, item_row0, item_nblk, xs, w1, w3, w2):
    padded_rows, d_packed = xs.shape
    d = w1.shape[1]
    n_items = item_e.shape[0]
    f = w1.shape[2]
    nc = f // MOE_F_CHUNK
    chunk = lambda i, c, ib: jnp.where(ib[i] > 0, c, nc - 1)
    return pl.pallas_call(
        _moe_kernel,
        grid_spec=pltpu.PrefetchScalarGridSpec(
            num_scalar_prefetch=3,
            grid=(n_items, nc),
            in_specs=[pl.BlockSpec(memory_space=pl.ANY),
                      pl.BlockSpec((1, d, MOE_F_CHUNK), lambda i, c, ie, ir, ib: (ie[i], 0, chunk(i, c, ib))),
                      pl.BlockSpec((1, d, MOE_F_CHUNK), lambda i, c, ie, ir, ib: (ie[i], 0, chunk(i, c, ib))),
                      pl.BlockSpec((1, MOE_F_CHUNK, d), lambda i, c, ie, ir, ib: (ie[i], chunk(i, c, ib), 0))],
            out_specs=pl.BlockSpec(memory_space=pl.ANY),
            scratch_shapes=[pltpu.VMEM((MOE_ROW_GROUP, d_packed), xs.dtype),
                            pltpu.VMEM((MOE_ROW_GROUP, d), BF16),
                            pltpu.VMEM((MOE_ROW_GROUP, d), F32),
                            pltpu.VMEM((MOE_ROW_GROUP, d_packed), xs.dtype),
                            pltpu.VMEM((d, MOE_F_CHUNK), BF16),
                            pltpu.VMEM((d, MOE_F_CHUNK), BF16),
                            pltpu.VMEM((MOE_F_CHUNK, d), BF16),
                            pltpu.SemaphoreType.DMA((2,))]),
        out_shape=jax.ShapeDtypeStruct((padded_rows, d_packed), xs.dtype),
        compiler_params=_cparams(("arbitrary", "arbitrary")),
        name="moe",
    )(item_e, item_row0, item_nblk, xs, w1, w3, w2)


COMBINE_TM = 256


def _combine_kernel(dest_ref, x1_ref, gw_ref, g2_ref, ys, o_ref, y0, y1, sem, *, tok_base, n_tok):
    tok0 = tok_base + pl.program_id(0) * COMBINE_TM

    def issue(g, carry):
        base = pl.multiple_of(g * ROW_DMA_GROUP, ROW_DMA_GROUP)
        for j in range(ROW_DMA_GROUP):
            _row_copy(ys, dest_ref[tok0 + base + j], y0, base + j, sem).start()
            _row_copy(ys, dest_ref[n_tok + tok0 + base + j], y1, base + j, sem).start()
        return carry

    def drain(g, carry):
        for j in range(ROW_DMA_GROUP):
            _row_copy(ys, 0, y0, 0, sem).wait()
            _row_copy(ys, 0, y1, 0, sem).wait()
        return carry

    lax.fori_loop(0, COMBINE_TM // ROW_DMA_GROUP, issue, 0)
    lax.fori_loop(0, COMBINE_TM // ROW_DMA_GROUP, drain, 0)
    gw = gw_ref[...]
    moe = gw[:, 0:1] * _unpack_halves(y0[...]) + gw[:, 1:2] * _unpack_halves(y1[...])
    o_ref[...] = x1_ref[...] + g2_ref[0] * moe


def _combine_call(dest_flat, x1, gw, g2, ys, tok_base, n_tok, tiles_per_group):
    m, d = x1.shape
    return pl.pallas_call(
        functools.partial(_combine_kernel, tok_base=tok_base, n_tok=n_tok),
        grid_spec=pltpu.PrefetchScalarGridSpec(
            num_scalar_prefetch=1,
            grid=(m // COMBINE_TM,),
            in_specs=[pl.BlockSpec((COMBINE_TM, d), lambda i, dr: (i, 0)),
                      pl.BlockSpec((COMBINE_TM, 2), lambda i, dr: (i, 0)),
                      pl.BlockSpec((1, 1, d), lambda i, dr: (i // tiles_per_group, 0, 0)),
                      pl.BlockSpec(memory_space=pl.ANY)],
            out_specs=pl.BlockSpec((COMBINE_TM, d), lambda i, dr: (i, 0)),
            scratch_shapes=[pltpu.VMEM((COMBINE_TM,) + ys.shape[1:], ys.dtype),
                            pltpu.VMEM((COMBINE_TM,) + ys.shape[1:], ys.dtype),
                            pltpu.SemaphoreType.DMA]),
        out_shape=jax.ShapeDtypeStruct((m, d), F32),
        compiler_params=_cparams(("arbitrary",)),
        name="combine",
    )(dest_flat, x1, gw, g2, ys)


def kernel(x_prompt, x_sample, cache_a_k, cache_a_v, cache_b_k, cache_b_v, c, c_ctx, norm1_w, norm2_w, w_ada, b_ada, w_in, qn_a, kn_a, qn_b, kn_b, rpb_a, sink_b, w_pa, w_pb, w_out, w_rg, b_rg, w_re, b_re, w1, w3, w2):
    batch, seq, d = x_prompt.shape
    dec_batch, dec_seq, _ = x_sample.shape
    depth = norm1_w.shape[0]
    assert depth == 1, "one trunk layer"
    past = cache_a_k.shape[2]
    n_ctx, n_lat = batch * seq, dec_batch * dec_seq
    n_tok = n_ctx + n_lat

    xc = x_prompt.reshape(n_ctx, d)
    xl = x_sample.reshape(n_lat, d)

    cond = jnp.concatenate([c_ctx[None, :], c], axis=0)
    mod = _ada_call(cond, w_ada[0], b_ada[0][None, :])
    sh1, sc1, g1, sh2, sc2, g2 = [mod[:, i * d:(i + 1) * d][:, None, :] for i in range(6)]
    ctx_rows, lat_rows = slice(0, 1), slice(1, 1 + dec_batch)

    nw1, nw2 = norm1_w[0][None, :], norm2_w[0][None, :]
    qna, kna, qnb, knb = qn_a[0][None, :], kn_a[0][None, :], qn_b[0][None, :], kn_b[0][None, :]
    sink = sink_b[0]
    w_in_bf = w_in[0].astype(BF16)
    wpa_bf, wpb_bf, wout_bf = w_pa[0].astype(BF16), w_pb[0].astype(BF16), w_out[0].astype(BF16)
    wr = jnp.zeros((d, LOGIT_PAD), F32).at[:, :N_EXPERTS].set(w_re[0]).at[:, N_EXPERTS:N_EXPERTS + N_GROUPS].set(w_rg[0])
    br = jnp.zeros((LOGIT_PAD, 1), F32).at[:N_EXPERTS, 0].set(b_re[0]).at[N_EXPERTS:N_EXPERTS + N_GROUPS, 0].set(b_rg[0])

    proj_c, gates_c = _inproj_call(xc, nw1, sc1[ctx_rows], sh1[ctx_rows], w_in_bf, n_ctx // INPROJ_TM)
    proj_l, gates_l = _inproj_call(xl, nw1, sc1[lat_rows], sh1[lat_rows], w_in_bf, dec_seq // INPROJ_TM)

    oa_c, ob_c, new_a_k, new_a_v, new_b_k, new_b_v = _ctx_attn_call(proj_c, seq, qna, kna, qnb, knb, sink)

    bias_blocks, pair0 = _na_bias_blocks(rpb_a[0], dec_seq)
    cos, sin_a, sin_b = _rope_tables(dec_seq)
    rows_of = lambda cache: cache.reshape(-1, HEAD_DIM)
    oa_l = _lat_attn_a_call(proj_l, dec_seq, past, rows_of(cache_a_k), rows_of(cache_a_v),
                            bias_blocks, pair0, qna, kna)
    ob_l = _lat_attn_b_call(proj_l, dec_seq, past, rows_of(cache_b_k), rows_of(cache_b_v),
                            cos, sin_a, sin_b, qnb, knb, sink)

    mix_c = _mix_call(oa_c, ob_c, gates_c, wpa_bf, wpb_bf)
    mix_l = _mix_call(oa_l, ob_l, gates_l, wpa_bf, wpb_bf)
    x1_c, h2_c, lg_c = _outproj_call(mix_c, wout_bf, xc, g1[ctx_rows], sc2[ctx_rows], sh2[ctx_rows],
                                     nw2, wr, n_ctx // MERGE_TM)
    x1_l, h2_l, lg_l = _outproj_call(mix_l, wout_bf, xl, g1[lat_rows], sc2[lat_rows], sh2[lat_rows],
                                     nw2, wr, dec_seq // MERGE_TM)

    eid, gw, rank, cnt = _route_call(jnp.concatenate([lg_c, lg_l], axis=0), br)
    padded_rows, n_items = _moe_layout(2 * n_tok)
    dest, tail, item_e, item_row0, item_nblk = _routing_tables(eid[:2], rank[:2], cnt[:, 0], n_items)
    dest_flat = dest.reshape(-1)
    xs = _dispatch_call(dest_flat, tail, h2_c, h2_l, padded_rows)
    ys = _moe_call(item_e, item_row0, item_nblk, xs, w1[0], w3[0], w2[0])
    gw_t = gw[:2].T
    y_c = _combine_call(dest_flat, x1_c, gw_t[:n_ctx], g2[ctx_rows], ys, 0, n_tok, n_ctx // COMBINE_TM)
    y_l = _combine_call(dest_flat, x1_l, gw_t[n_ctx:], g2[lat_rows], ys, n_ctx, n_tok, dec_seq // COMBINE_TM)

    state = lambda a, heads: a.reshape(batch, 1, seq, heads, HEAD_DIM)
    return (y_c.reshape(batch, seq, d), y_l.reshape(dec_batch, dec_seq, d),
            state(new_a_k, NA_HEADS), state(new_a_v, NA_HEADS),
            state(new_b_k, NB_KV_HEADS), state(new_b_v, NB_KV_HEADS))
```

```python
import functools

import jax
import jax.numpy as jnp
import numpy as np
from jax import lax
from jax.experimental import pallas as pl
from jax.experimental.pallas import tpu as pltpu

D_MODEL = 2048
HEAD_DIM = 128
NA_HEADS = 8
NA_WIDTH = NA_HEADS * HEAD_DIM
NB_Q_HEADS = 8
NB_KV_HEADS = 2
NB_GROUP = NB_Q_HEADS // NB_KV_HEADS
NB_WIDTH = NB_Q_HEADS * HEAD_DIM
NB_KV_WIDTH = NB_KV_HEADS * HEAD_DIM
GRID_W = 64
NA_WIN_ROWS = 8
NA_WIN_COLS = 16
WINDOW = 128
N_GROUPS = 4
EXPERTS_PER_GROUP = 8
N_EXPERTS = N_GROUPS * EXPERTS_PER_GROUP
D_EXPERT = 1024
IN_WIDTH = 3 * NA_WIDTH + NB_WIDTH + 2 * NB_KV_WIDTH + 2 * D_MODEL
ROPE_BASE = 10000.0
NORM_EPS = 1e-6
NEG_INF = -1e30
ATTN_SCALE = HEAD_DIM ** -0.5

QA_HEAD0 = 0
KA_HEAD0 = NA_HEADS
VA_HEAD0 = 2 * NA_HEADS
QB_HEAD0 = 3 * NA_HEADS
KB_HEAD0 = QB_HEAD0 + NB_Q_HEADS
VB_HEAD0 = KB_HEAD0 + NB_KV_HEADS
GATE_COL0 = (VB_HEAD0 + NB_KV_HEADS) * HEAD_DIM

LOGIT_PAD = 128
MOE_ROW_BLOCK = 256
MOE_ROW_GROUP = 1024
MOE_F_CHUNK = 512
VMEM_LIMIT = 56 * 1024 * 1024

F32 = jnp.float32
BF16 = jnp.bfloat16


def _cparams(sem):
    return pltpu.CompilerParams(dimension_semantics=sem, vmem_limit_bytes=VMEM_LIMIT)


def _rms(x, w):
    return x * lax.rsqrt(jnp.mean(x * x, axis=-1, keepdims=True) + NORM_EPS) * w


def _dot(a, b):
    return jnp.dot(a, b, preferred_element_type=F32)


def _dot_nt(a, b):
    return lax.dot_general(a, b, (((1,), (1,)), ((), ())), preferred_element_type=F32)


def _pack_halves(x):
    n = x.shape[1] // 2
    lo = lax.bitcast_convert_type(x[:, :n].astype(BF16).astype(F32), jnp.uint32)
    hi = lax.bitcast_convert_type(x[:, n:].astype(BF16).astype(F32), jnp.uint32)
    return hi | (lo >> 16)


def _unpack_halves(w):
    lo = lax.bitcast_convert_type(w << 16, F32)
    hi = lax.bitcast_convert_type(w & jnp.uint32(0xFFFF0000), F32)
    return jnp.concatenate([lo, hi], axis=1)


def _dot_split(a, b):
    a_hi = a.astype(BF16)
    a_lo = (a - a_hi.astype(F32)).astype(BF16)
    b_hi = b.astype(BF16)
    b_lo = (b - b_hi.astype(F32)).astype(BF16)
    return _dot(a_hi, b_hi) + (_dot(a_lo, b_hi) + _dot(a_hi, b_lo))


ADA_ROWS = 8
ADA_TN = 1024


def _ada_kernel(c_ref, w_ref, b_ref, o_ref):
    n_rows, d, lanes = c_ref.shape
    tn = w_ref.shape[1]

    def body(kb, acc):
        ks = pl.ds(pl.multiple_of(kb * 8, 8), 8)
        w = w_ref[ks, :]
        out = []
        for r in range(n_rows):
            c = c_ref[r, ks, :]
            s = c * jax.nn.sigmoid(c)
            out.append(acc[r] + w * jnp.concatenate([s] * (tn // lanes), axis=1))
        return tuple(out)

    acc = lax.fori_loop(0, d // 8, body, tuple(jnp.zeros((8, tn), F32) for _ in range(n_rows)), unroll=8)
    ri = lax.broadcasted_iota(jnp.int32, (ADA_ROWS, tn), 0)
    res = jnp.zeros((ADA_ROWS, tn), F32)
    for r in range(n_rows):
        row = jnp.sum(acc[r], axis=0, keepdims=True) + b_ref[...]
        res = jnp.where(ri == r, row, res)
    o_ref[...] = res


def _ada_call(cond, w_ada, b_ada):
    n_rows, d = cond.shape
    n = w_ada.shape[1]
    lanes = 128
    cond_lanes = jnp.broadcast_to(cond[:, :, None], (n_rows, d, lanes))
    return pl.pallas_call(
        _ada_kernel,
        grid=(n // ADA_TN,),
        in_specs=[pl.BlockSpec((n_rows, d, lanes), lambda j: (0, 0, 0)),
                  pl.BlockSpec((d, ADA_TN), lambda j: (0, j)),
                  pl.BlockSpec((1, ADA_TN), lambda j: (0, j))],
        out_specs=pl.BlockSpec((ADA_ROWS, ADA_TN), lambda j: (0, j)),
        out_shape=jax.ShapeDtypeStruct((ADA_ROWS, n), F32),
        compiler_params=_cparams(("arbitrary",)),
        name="ada",
    )(cond_lanes, w_ada, b_ada)


INPROJ_TM = 1024
INPROJ_TN = 512
NORM_ROWS = 128


def _inproj_kernel(x_ref, nw_ref, sc_ref, sh_ref, w_ref, qkv_ref, gate_ref, h_scr, *, qkv_tiles):
    j = pl.program_id(1)

    @pl.when(j == 0)
    def _():
        nw = nw_ref[...]
        sc = 1.0 + sc_ref[0]
        sh = sh_ref[0]

        def body(r, carry):
            rows = pl.ds(pl.multiple_of(r * NORM_ROWS, NORM_ROWS), NORM_ROWS)
            h_scr[rows, :] = (_rms(x_ref[rows, :], nw) * sc + sh).astype(BF16)
            return carry

        lax.fori_loop(0, INPROJ_TM // NORM_ROWS, body, 0)

    res = _dot(h_scr[...], w_ref[...])

    @pl.when(j < qkv_tiles)
    def _():
        qkv_ref[...] = res

    @pl.when(j >= qkv_tiles)
    def _():
        gate_ref[...] = res.astype(gate_ref.dtype)


def _inproj_call(x, nw, sc, sh, w_bf, tiles_per_group):
    m, d = x.shape
    n = w_bf.shape[1]
    qkv_tiles = GATE_COL0 // INPROJ_TN
    grp = lambda i, j: (i // tiles_per_group, 0, 0)
    return pl.pallas_call(
        functools.partial(_inproj_kernel, qkv_tiles=qkv_tiles),
        grid=(m // INPROJ_TM, n // INPROJ_TN),
        in_specs=[pl.BlockSpec((INPROJ_TM, d), lambda i, j: (i, 0)),
                  pl.BlockSpec((1, d), lambda i, j: (0, 0)),
                  pl.BlockSpec((1, 1, d), grp),
                  pl.BlockSpec((1, 1, d), grp),
                  pl.BlockSpec((d, INPROJ_TN), lambda i, j: (0, j))],
        out_specs=[pl.BlockSpec((INPROJ_TM, INPROJ_TN), lambda i, j: (i, jnp.minimum(j, qkv_tiles - 1))),
                   pl.BlockSpec((INPROJ_TM, INPROJ_TN), lambda i, j: (i, jnp.maximum(j - qkv_tiles, 0)))],
        out_shape=[jax.ShapeDtypeStruct((m, GATE_COL0), F32),
                   jax.ShapeDtypeStruct((m, n - GATE_COL0), BF16)],
        scratch_shapes=[pltpu.VMEM((INPROJ_TM, d), BF16)],
        compiler_params=_cparams(("arbitrary", "arbitrary")),
        name="inproj",
    )(x, nw, sc, sh, w_bf)


def _with_ones(v):
    return jnp.concatenate([v.astype(BF16), jnp.ones(v.shape, BF16)], axis=1)


def _softmax_pv(scores, values_with_ones, sink=None):
    m = None
    for s in scores:
        ms = jnp.max(s, axis=-1, keepdims=True)
        m = ms if m is None else jnp.maximum(m, ms)
    if sink is not None:
        m = jnp.maximum(m, sink)
    acc = None
    for s, v1 in zip(scores, values_with_ones):
        pv = _dot(jnp.exp(s - m).astype(BF16), v1)
        acc = pv if acc is None else acc + pv
    d = acc.shape[1] // 2
    den = acc[:, d:]
    if sink is not None:
        den = den + jnp.exp(sink - m)
    return acc[:, :d] / den


def _ctx_attn_kernel(qa_ref, ka_ref, va_ref, qb_ref, kb_ref, vb_ref,
                     qna_ref, kna_ref, qnb_ref, knb_ref, sink_ref,
                     oa_ref, ob_ref, nak_ref, nav_ref, nbk_ref, nbv_ref):
    qna, kna, qnb, knb = qna_ref[...], kna_ref[...], qnb_ref[...], knb_ref[...]
    seq = qa_ref.shape[0]
    for h in range(NA_HEADS):
        cols = slice(h * HEAD_DIM, (h + 1) * HEAD_DIM)
        q = (_rms(qa_ref[:, cols], qna) * ATTN_SCALE).astype(BF16)
        k = _rms(ka_ref[:, cols], kna)
        v = va_ref[:, cols]
        nak_ref[pl.ds(h, seq, stride=NA_HEADS), :] = k
        nav_ref[pl.ds(h, seq, stride=NA_HEADS), :] = v
        s = _dot_nt(q, k.astype(BF16))
        oa_ref[:, cols] = _softmax_pv([s], [_with_ones(v)]).astype(oa_ref.dtype)
    for kv in range(NB_KV_HEADS):
        kcols = slice(kv * HEAD_DIM, (kv + 1) * HEAD_DIM)
        k = _rms(kb_ref[:, kcols], knb)
        v = vb_ref[:, kcols]
        nbk_ref[pl.ds(kv, seq, stride=NB_KV_HEADS), :] = k
        nbv_ref[pl.ds(kv, seq, stride=NB_KV_HEADS), :] = v
        kb16 = k.astype(BF16)
        vb1 = _with_ones(v)
        for g in range(NB_GROUP):
            hq = kv * NB_GROUP + g
            cols = slice(hq * HEAD_DIM, (hq + 1) * HEAD_DIM)
            q = (_rms(qb_ref[:, cols], qnb) * ATTN_SCALE).astype(BF16)
            s = _dot_nt(q, kb16)
            ob_ref[:, cols] = _softmax_pv([s], [vb1], sink=sink_ref[hq]).astype(ob_ref.dtype)


def _ctx_attn_call(proj, seq, qna, kna, qnb, knb, sink):
    m = proj.shape[0]
    nb = m // seq
    wide = lambda blk: pl.BlockSpec((seq, NA_WIDTH), lambda b: (b, blk))
    narrow = lambda blk: pl.BlockSpec((seq, NB_KV_WIDTH), lambda b: (b, blk))
    vec = pl.BlockSpec((1, HEAD_DIM), lambda b: (0, 0))
    return pl.pallas_call(
        _ctx_attn_kernel,
        grid=(nb,),
        in_specs=[wide(QA_HEAD0 // NA_HEADS), wide(KA_HEAD0 // NA_HEADS), wide(VA_HEAD0 // NA_HEADS),
                  wide(QB_HEAD0 // NA_HEADS), narrow(KB_HEAD0 // NB_KV_HEADS), narrow(VB_HEAD0 // NB_KV_HEADS),
                  vec, vec, vec, vec,
                  pl.BlockSpec(memory_space=pltpu.SMEM)],
        out_specs=[pl.BlockSpec((seq, NA_WIDTH), lambda b: (b, 0)),
                   pl.BlockSpec((seq, NB_WIDTH), lambda b: (b, 0)),
                   pl.BlockSpec((seq * NA_HEADS, HEAD_DIM), lambda b: (b, 0)),
                   pl.BlockSpec((seq * NA_HEADS, HEAD_DIM), lambda b: (b, 0)),
                   pl.BlockSpec((seq * NB_KV_HEADS, HEAD_DIM), lambda b: (b, 0)),
                   pl.BlockSpec((seq * NB_KV_HEADS, HEAD_DIM), lambda b: (b, 0))],
        out_shape=[jax.ShapeDtypeStruct((m, NA_WIDTH), BF16),
                   jax.ShapeDtypeStruct((m, NB_WIDTH), BF16),
                   jax.ShapeDtypeStruct((m * NA_HEADS, HEAD_DIM), F32),
                   jax.ShapeDtypeStruct((m * NA_HEADS, HEAD_DIM), F32),
                   jax.ShapeDtypeStruct((m * NB_KV_HEADS, HEAD_DIM), F32),
                   jax.ShapeDtypeStruct((m * NB_KV_HEADS, HEAD_DIM), F32)],
        compiler_params=_cparams(("arbitrary",)),
        name="ctx_attn",
    )(proj, proj, proj, proj, proj, proj, qna, kna, qnb, knb, sink)


def _rope(x, cos, sin_a, sin_b):
    quarter = HEAD_DIM // 4
    return (x * cos + pltpu.roll(x, HEAD_DIM - quarter, 1) * sin_a
            + pltpu.roll(x, quarter, 1) * sin_b)


def _head_rows(cache_ref, head, n_heads):
    past = cache_ref.shape[0] // n_heads
    return cache_ref[pl.ds(head, past, stride=n_heads), :]


def _lat_attn_a_kernel(q_ref, k_ref, v_ref, ck_ref, cv_ref, cb_ref, qn_ref, kn_ref, o_ref, bias_scr,
                       *, pair0):
    head = pl.program_id(0)

    @pl.when(pl.program_id(1) == 0)
    def _():
        bias_scr[...] = jnp.full(bias_scr.shape, NEG_INF, F32)
        for qr, p0 in enumerate(pair0):
            for j in range(cb_ref.shape[1]):
                bias_scr[qr * GRID_W:(qr + 1) * GRID_W, (p0 + j) * 2 * GRID_W:(p0 + j + 1) * 2 * GRID_W] = cb_ref[qr, j]

    q = (_rms(q_ref[...], qn_ref[...]) * ATTN_SCALE).astype(BF16)
    k = _rms(k_ref[...], kn_ref[...]).astype(BF16)
    s_lat = _dot_nt(q, k) + bias_scr[...]
    ck = _head_rows(ck_ref, head, NA_HEADS).astype(BF16)
    cv = _head_rows(cv_ref, head, NA_HEADS)
    s_ctx = _dot_nt(q, ck)
    o = _softmax_pv([s_lat, s_ctx], [_with_ones(v_ref[...]), _with_ones(cv)])
    o_ref[...] = o.astype(o_ref.dtype)


def _lat_attn_b_kernel(q_ref, k_ref, v_ref, ck_ref, cv_ref, cos_ref, sina_ref, sinb_ref,
                       qn_ref, kn_ref, sink_ref, o_ref):
    cos, sin_a, sin_b = cos_ref[...], sina_ref[...], sinb_ref[...]
    q = _rope(_rms(q_ref[...], qn_ref[...]), cos, sin_a, sin_b)
    k = _rope(_rms(k_ref[...], kn_ref[...]), cos, sin_a, sin_b)
    q = (q * ATTN_SCALE).astype(BF16)
    k = k.astype(BF16)
    v1 = _with_ones(v_ref[...])
    length = q.shape[0]
    kv = pl.program_id(0) // NB_GROUP
    ck = _head_rows(ck_ref, kv, NB_KV_HEADS).astype(BF16)
    cv1 = _with_ones(_head_rows(cv_ref, kv, NB_KV_HEADS))
    sink = sink_ref[pl.program_id(0)]
    for qb in range(length // WINDOW):
        rows = slice(qb * WINDOW, (qb + 1) * WINDOW)
        lo, hi = max(0, (qb - 1) * WINDOW), min(length, (qb + 2) * WINDOW)
        qs = q[rows]
        s_win = _dot_nt(qs, k[lo:hi])
        qi = qb * WINDOW + lax.broadcasted_iota(jnp.int32, s_win.shape, 0)
        kj = lo + lax.broadcasted_iota(jnp.int32, s_win.shape, 1)
        s_win = jnp.where(jnp.abs(qi - kj) <= WINDOW, s_win, NEG_INF)
        s_ctx = _dot_nt(qs, ck)
        o_ref[rows, :] = _softmax_pv([s_win, s_ctx], [v1[lo:hi], cv1], sink=sink).astype(o_ref.dtype)


def _lat_attn_a_call(proj, length, past, ck, cv, cb, pair0, qn, kn):
    m = proj.shape[0]
    head = lambda h0: pl.BlockSpec((length, HEAD_DIM), lambda h, b: (b, h0 + h))
    cache = pl.BlockSpec((past * NA_HEADS, HEAD_DIM), lambda h, b: (b, 0))
    vec = pl.BlockSpec((1, HEAD_DIM), lambda h, b: (0, 0))
    return pl.pallas_call(
        functools.partial(_lat_attn_a_kernel, pair0=pair0),
        grid=(NA_HEADS, m // length),
        in_specs=[head(QA_HEAD0), head(KA_HEAD0), head(VA_HEAD0), cache, cache,
                  pl.BlockSpec((None,) + cb.shape[1:], lambda h, b: (h, 0, 0, 0, 0)), vec, vec],
        out_specs=pl.BlockSpec((length, HEAD_DIM), lambda h, b: (b, h)),
        out_shape=jax.ShapeDtypeStruct((m, NA_WIDTH), BF16),
        scratch_shapes=[pltpu.VMEM((length, length), F32)],
        compiler_params=_cparams(("arbitrary", "arbitrary")),
        name="lat_attn_a",
    )(proj, proj, proj, ck, cv, cb, qn, kn)


def _lat_attn_b_call(proj, length, past, ck, cv, cos, sin_a, sin_b, qn, kn, sink):
    m = proj.shape[0]
    qspec = pl.BlockSpec((length, HEAD_DIM), lambda h, b: (b, QB_HEAD0 + h))
    kvspec = lambda h0: pl.BlockSpec((length, HEAD_DIM), lambda h, b: (b, h0 + h // NB_GROUP))
    cache = pl.BlockSpec((past * NB_KV_HEADS, HEAD_DIM), lambda h, b: (b, 0))
    table = pl.BlockSpec((length, HEAD_DIM), lambda h, b: (0, 0))
    vec = pl.BlockSpec((1, HEAD_DIM), lambda h, b: (0, 0))
    return pl.pallas_call(
        _lat_attn_b_kernel,
        grid=(NB_Q_HEADS, m // length),
        in_specs=[qspec, kvspec(KB_HEAD0), kvspec(VB_HEAD0), cache, cache,
                  table, table, table, vec, vec, pl.BlockSpec(memory_space=pltpu.SMEM)],
        out_specs=pl.BlockSpec((length, HEAD_DIM), lambda h, b: (b, h)),
        out_shape=jax.ShapeDtypeStruct((m, NB_WIDTH), BF16),
        compiler_params=_cparams(("arbitrary", "arbitrary")),
        name="lat_attn_b",
    )(proj, proj, proj, ck, cv, cos, sin_a, sin_b, qn, kn, sink)


def _na_bias_blocks(rpb, length):
    rows = length // GRID_W
    kr_n = min(NA_WIN_ROWS, rows)
    n_pairs = min(kr_n // 2 + 1, rows // 2)
    r = np.arange(rows)
    c = np.arange(GRID_W)
    r0 = np.clip(r - kr_n // 2, 0, rows - kr_n)
    c0 = np.clip(c - NA_WIN_COLS // 2, 0, GRID_W - NA_WIN_COLS)
    pair0 = np.minimum(r0 // 2, rows // 2 - n_pairs)
    kr = 2 * (pair0[:, None, None] + np.arange(n_pairs)[None, :, None]) + np.arange(2)[None, None, :]
    row_ok = (kr >= r0[:, None, None]) & (kr < r0[:, None, None] + kr_n)
    col_ok = (c[None, :] >= c0[:, None]) & (c[None, :] < c0[:, None] + NA_WIN_COLS)
    dr = kr - r[:, None, None] + (NA_WIN_ROWS - 1)
    dc = np.clip(c[None, :] - c[:, None], -(NA_WIN_COLS - 1), NA_WIN_COLS - 1) + (NA_WIN_COLS - 1)
    row_sel = ((dr[None] == np.arange(2 * NA_WIN_ROWS - 1)[:, None, None, None]) & row_ok[None]).astype(np.float32)
    col_hit = (dc[None] == np.arange(2 * NA_WIN_COLS - 1)[:, None, None]) & col_ok[None]
    col_sel = np.zeros((2,) + col_hit.shape[:2] + (2 * GRID_W,), np.float32)
    for half in range(2):
        col_sel[half, :, :, half * GRID_W:(half + 1) * GRID_W] = col_hit
    hi = lax.Precision.HIGHEST
    per_col = jnp.einsum("hde,lecn->hldcn", rpb.astype(F32), col_sel, precision=hi)
    table = jnp.einsum("dqjl,hldcn->hqjcn", row_sel, per_col, precision=hi)
    valid = (row_ok[:, :, None, :, None] & col_ok[None, None, :, None, :]).reshape(rows, n_pairs, GRID_W, 2 * GRID_W)
    return jnp.where(valid[None], table, NEG_INF), tuple(int(p) for p in pair0)


def _rope_tables(length):
    t = jnp.arange(length)
    row = (t // GRID_W).astype(F32)
    col = (t % GRID_W).astype(F32)
    n_freq = HEAD_DIM // 4
    inv = ROPE_BASE ** (-jnp.arange(n_freq, dtype=F32) / n_freq)
    ar = row[:, None] * inv
    ac = col[:, None] * inv
    ang = jnp.concatenate([ar, ar, ac, ac], axis=-1)
    cos, sin = jnp.cos(ang), jnp.sin(ang)
    lane = jnp.arange(HEAD_DIM)
    takes_left = ((lane // n_freq) % 2 == 0)[None, :]
    return cos, jnp.where(takes_left, -sin, 0.0), jnp.where(takes_left, 0.0, sin)


MERGE_TM = 512


def _resident(shape):
    zeros = (0,) * len(shape)
    return pl.BlockSpec(shape, lambda i: zeros, pipeline_mode=pl.Buffered(1))


def _mix_kernel(oa_ref, ob_ref, ga_ref, gb_ref, wpa_ref, wpb_ref, mix_ref):
    ya = _dot(oa_ref[...], wpa_ref[...])
    yb = _dot(ob_ref[...], wpb_ref[...])
    mix = (jax.nn.sigmoid(ga_ref[...].astype(F32)) * ya
           + jax.nn.sigmoid(gb_ref[...].astype(F32)) * yb)
    mix_ref[...] = mix.astype(mix_ref.dtype)


def _mix_call(oa, ob, gates, wpa_bf, wpb_bf):
    m = oa.shape[0]
    d = wpa_bf.shape[1]
    row = lambda i: (i, 0)
    return pl.pallas_call(
        _mix_kernel,
        grid=(m // MERGE_TM,),
        in_specs=[pl.BlockSpec((MERGE_TM, NA_WIDTH), row),
                  pl.BlockSpec((MERGE_TM, NB_WIDTH), row),
                  pl.BlockSpec((MERGE_TM, d), lambda i: (i, 0)),
                  pl.BlockSpec((MERGE_TM, d), lambda i: (i, 1)),
                  _resident((NA_WIDTH, d)), _resident((NB_WIDTH, d))],
        out_specs=pl.BlockSpec((MERGE_TM, d), row),
        out_shape=jax.ShapeDtypeStruct((m, d), BF16),
        compiler_params=_cparams(("arbitrary",)),
        name="mix",
    )(oa, ob, gates, gates, wpa_bf, wpb_bf)


def _outproj_kernel(mix_ref, wout_ref, x_ref, g1_ref, sc2_ref, sh2_ref, n2w_ref, wr_ref,
                    x1_ref, h2_ref, lg_ref, acc_ref):
    @pl.when(pl.program_id(0) == 0)
    def _():
        acc_ref[...] = jnp.zeros_like(acc_ref)

    prev = acc_ref[...]
    acc_ref[...] = _dot(mix_ref[...], wout_ref[...])
    x1 = x_ref[...] + g1_ref[0] * prev
    x1_ref[...] = x1
    h2 = _rms(x1, n2w_ref[...]) * (1.0 + sc2_ref[0]) + sh2_ref[0]
    h2_ref[...] = _pack_halves(h2)
    lg_ref[...] = _dot_split(h2, wr_ref[...])


def _outproj_call(mix, wout_bf, x, g1, sc2, sh2, n2w, wr, tiles_per_group):
    m, d = x.shape
    n_tiles = m // MERGE_TM
    ahead = lambda i: (jnp.minimum(i, n_tiles - 1), 0)
    done = lambda i: (jnp.maximum(i - 1, 0), 0)
    grp = lambda i: (jnp.maximum(i - 1, 0) // tiles_per_group, 0, 0)
    return pl.pallas_call(
        _outproj_kernel,
        grid=(n_tiles + 1,),
        in_specs=[pl.BlockSpec((MERGE_TM, d), ahead),
                  _resident((d, d)),
                  pl.BlockSpec((MERGE_TM, d), done),
                  pl.BlockSpec((1, 1, d), grp), pl.BlockSpec((1, 1, d), grp), pl.BlockSpec((1, 1, d), grp),
                  pl.BlockSpec((1, d), lambda i: (0, 0)),
                  _resident((d, LOGIT_PAD))],
        out_specs=[pl.BlockSpec((MERGE_TM, d), done),
                   pl.BlockSpec((MERGE_TM, d // 2), done),
                   pl.BlockSpec((MERGE_TM, LOGIT_PAD), done)],
        out_shape=[jax.ShapeDtypeStruct((m, d), F32),
                   jax.ShapeDtypeStruct((m, d // 2), jnp.uint32),
                   jax.ShapeDtypeStruct((m, LOGIT_PAD), F32)],
        scratch_shapes=[pltpu.VMEM((MERGE_TM, d), F32)],
        compiler_params=_cparams(("arbitrary",)),
        name="outproj",
    )(mix, wout_bf, x, g1, sc2, sh2, n2w, wr)


ROUTE_TM = 512


def _first_index_of_max(vals, idx, n):
    mx = jnp.max(vals, axis=0, keepdims=True)
    first = jnp.min(jnp.where(vals == mx, idx, n), axis=0, keepdims=True)
    return mx, first


def _route_kernel(lg_ref, bias_ref, eid_ref, gw_ref, rank_ref, cnt_ref, base_ref):
    step = pl.program_id(0)

    @pl.when(step == 0)
    def _():
        base_ref[...] = jnp.zeros_like(base_ref)

    lt = lg_ref[...].T + bias_ref[...]
    n_tok = lt.shape[1]
    le = lt[0:N_EXPERTS]
    lgrp = lt[N_EXPERTS:N_EXPERTS + N_GROUPS]
    gi = lax.broadcasted_iota(jnp.int32, (N_GROUPS, n_tok), 0)
    gmax, gsel = _first_index_of_max(lgrp, gi, N_GROUPS)
    pg_sel = 1.0 / jnp.sum(jnp.exp(lgrp - gmax), axis=0, keepdims=True)
    le_sel = jnp.zeros((EXPERTS_PER_GROUP, n_tok), F32)
    for g in range(N_GROUPS):
        le_sel = jnp.where(gsel == g, le[g * EXPERTS_PER_GROUP:(g + 1) * EXPERTS_PER_GROUP], le_sel)
    ei = lax.broadcasted_iota(jnp.int32, (EXPERTS_PER_GROUP, n_tok), 0)
    v0, i0 = _first_index_of_max(le_sel, ei, EXPERTS_PER_GROUP)
    rest = jnp.where(ei == i0, -jnp.inf, le_sel)
    v1, i1 = _first_index_of_max(rest, ei, EXPERTS_PER_GROUP)
    e1 = jnp.exp(v1 - v0)
    w0 = pg_sel / (1.0 + e1)
    w1 = pg_sel * e1 / (1.0 + e1)
    eid0 = gsel * EXPERTS_PER_GROUP + i0
    eid1 = gsel * EXPERTS_PER_GROUP + i1

    xi = lax.broadcasted_iota(jnp.int32, (N_EXPERTS, n_tok), 0)
    si = lax.broadcasted_iota(jnp.int32, (n_tok, n_tok), 0)
    ti = lax.broadcasted_iota(jnp.int32, (n_tok, n_tok), 1)
    before = (si < ti).astype(BF16)
    base = base_ref[...]
    hot0 = (xi == eid0).astype(F32)
    hot1 = (xi == eid1).astype(F32)
    pre0 = _dot(hot0.astype(BF16), before)
    pre1 = _dot(hot1.astype(BF16), before)
    tot0 = jnp.sum(hot0, axis=1, keepdims=True)
    tot1 = jnp.sum(hot1, axis=1, keepdims=True)
    rank0 = jnp.sum(hot0 * (base + pre0), axis=0, keepdims=True)
    rank1 = jnp.sum(hot1 * (base + tot0 + pre1), axis=0, keepdims=True)
    base = base + tot0 + tot1
    base_ref[...] = base

    ri = lax.broadcasted_iota(jnp.int32, (8, n_tok), 0)
    pick = lambda a, b: jnp.where(ri == 0, a, jnp.where(ri == 1, b, jnp.zeros_like(a)))
    eid_ref[...] = pick(eid0, eid1)
    gw_ref[...] = pick(w0, w1)
    rank_ref[...] = pick(rank0, rank1).astype(jnp.int32)
    cnt_ref[...] = jnp.broadcast_to(base, cnt_ref.shape).astype(jnp.int32)


def _route_call(logits, bias_col):
    t = logits.shape[0]
    tok = pl.BlockSpec((8, ROUTE_TM), lambda i: (0, i))
    return pl.pallas_call(
        _route_kernel,
        grid=(t // ROUTE_TM,),
        in_specs=[pl.BlockSpec((ROUTE_TM, LOGIT_PAD), lambda i: (i, 0)),
                  pl.BlockSpec((LOGIT_PAD, 1), lambda i: (0, 0))],
        out_specs=[tok, tok, tok, pl.BlockSpec((N_EXPERTS, 128), lambda i: (0, 0))],
        out_shape=[jax.ShapeDtypeStruct((8, t), jnp.int32),
                   jax.ShapeDtypeStruct((8, t), F32),
                   jax.ShapeDtypeStruct((8, t), jnp.int32),
                   jax.ShapeDtypeStruct((N_EXPERTS, 128), jnp.int32)],
        scratch_shapes=[pltpu.VMEM((N_EXPERTS, 1), F32)],
        compiler_params=_cparams(("arbitrary",)),
        name="route",
    )(logits, bias_col)


def _moe_layout(n_pairs):
    padded_rows = -(-(n_pairs + N_EXPERTS * (MOE_ROW_BLOCK - 1)) // MOE_ROW_BLOCK) * MOE_ROW_BLOCK
    n_items = -(-padded_rows // MOE_ROW_GROUP) + N_EXPERTS
    return padded_rows, n_items


def _routing_tables(eid, rank, counts, n_items):
    padded = (counts + MOE_ROW_BLOCK - 1) // MOE_ROW_BLOCK * MOE_ROW_BLOCK
    pad_end = jnp.cumsum(padded)
    pad_start = pad_end - padded
    hot = eid[..., None] == jnp.arange(N_EXPERTS, dtype=jnp.int32)
    dest = (jnp.sum(jnp.where(hot, pad_start, 0), axis=-1) + rank).astype(jnp.int32)
    tail = jnp.where(padded > counts, pad_end - MOE_ROW_BLOCK, -1).astype(jnp.int32)
    per_expert = (padded + MOE_ROW_GROUP - 1) // MOE_ROW_GROUP
    item_end = jnp.cumsum(per_expert)
    item_start = item_end - per_expert
    total = item_end[-1]
    ii = jnp.arange(n_items, dtype=jnp.int32)
    e_of = jnp.minimum(jnp.searchsorted(item_end, ii, side="right"), N_EXPERTS - 1).astype(jnp.int32)
    valid = ii < total
    e_last = e_of[jnp.maximum(total - 1, 0)]
    local = ii - item_start[e_of]
    row0 = pad_start[e_of] + local * MOE_ROW_GROUP
    nblk = jnp.clip((padded[e_of] - local * MOE_ROW_GROUP) // MOE_ROW_BLOCK, 0, MOE_ROW_GROUP // MOE_ROW_BLOCK)
    item_e = jnp.where(valid, e_of, e_last).astype(jnp.int32)
    item_row0 = jnp.where(valid, row0, 0).astype(jnp.int32)
    item_nblk = jnp.where(valid, nblk, 0).astype(jnp.int32)
    return dest, tail, item_e, item_row0, item_nblk


DISPATCH_TOKENS = 256
ROW_DMA_GROUP = 8


def _row_copy(src, s, dst, d, sem):
    return pltpu.make_async_copy(src.at[pl.ds(s, 1), :], dst.at[pl.ds(d, 1), :], sem)


def _dispatch_kernel(dest_ref, tail_ref, h_ctx, h_lat, xs, zero_buf, sem, *, n_ctx, n_tok):
    step = pl.program_id(0)
    tail_copy = lambda e: pltpu.make_async_copy(
        zero_buf, xs.at[pl.ds(pl.multiple_of(tail_ref[e], MOE_ROW_BLOCK), MOE_ROW_BLOCK), :], sem.at[1])

    @pl.when(step == 0)
    def _():
        zero_buf[...] = jnp.zeros_like(zero_buf)
        for e in range(N_EXPERTS):
            @pl.when(tail_ref[e] >= 0)
            def _():
                tail_copy(e).start()
        for e in range(N_EXPERTS):
            @pl.when(tail_ref[e] >= 0)
            def _():
                tail_copy(e).wait()

    tok0 = step * DISPATCH_TOKENS

    def scatter(src):
        def issue(g, carry):
            base = pl.multiple_of(g * ROW_DMA_GROUP, ROW_DMA_GROUP)
            for j in range(ROW_DMA_GROUP):
                _row_copy(src, base + j, xs, dest_ref[tok0 + base + j], sem.at[0]).start()
                _row_copy(src, base + j, xs, dest_ref[n_tok + tok0 + base + j], sem.at[0]).start()
            return carry

        def drain(g, carry):
            for j in range(2 * ROW_DMA_GROUP):
                _row_copy(src, 0, xs, 0, sem.at[0]).wait()
            return carry

        lax.fori_loop(0, DISPATCH_TOKENS // ROW_DMA_GROUP, issue, 0)
        lax.fori_loop(0, DISPATCH_TOKENS // ROW_DMA_GROUP, drain, 0)

    @pl.when(tok0 < n_ctx)
    def _():
        scatter(h_ctx)

    @pl.when(tok0 >= n_ctx)
    def _():
        scatter(h_lat)


def _dispatch_call(dest_flat, tail, h_ctx, h_lat, padded_rows):
    n_ctx, d = h_ctx.shape
    n_tok = n_ctx + h_lat.shape[0]
    ctx_tiles = n_ctx // DISPATCH_TOKENS
    return pl.pallas_call(
        functools.partial(_dispatch_kernel, n_ctx=n_ctx, n_tok=n_tok),
        grid_spec=pltpu.PrefetchScalarGridSpec(
            num_scalar_prefetch=2,
            grid=(n_tok // DISPATCH_TOKENS,),
            in_specs=[pl.BlockSpec((DISPATCH_TOKENS, d), lambda i, dr, tr: (jnp.minimum(i, ctx_tiles - 1), 0)),
                      pl.BlockSpec((DISPATCH_TOKENS, d), lambda i, dr, tr: (jnp.maximum(i - ctx_tiles, 0), 0))],
            out_specs=pl.BlockSpec(memory_space=pl.ANY),
            scratch_shapes=[pltpu.VMEM((MOE_ROW_BLOCK, d), h_ctx.dtype), pltpu.SemaphoreType.DMA((2,))]),
        out_shape=jax.ShapeDtypeStruct((padded_rows, d), h_ctx.dtype),
        compiler_params=_cparams(("arbitrary",)),
        name="dispatch",
    )(dest_flat, tail, h_ctx, h_lat)


def _moe_kernel(item_e, item_row0, item_nblk, xs, w1_ref, w3_ref, w2_ref, ys,
                x_in, x_bf, acc, y_out, w1_bf, w3_bf, w2_bf, sem):
    i = pl.program_id(0)
    c = pl.program_id(1)
    n_items = pl.num_programs(0)
    last_c = pl.num_programs(1) - 1
    nblk = item_nblk[i]
    max_blk = MOE_ROW_GROUP // MOE_ROW_BLOCK
    blk = lambda b: pl.ds(b * MOE_ROW_BLOCK, MOE_ROW_BLOCK)

    def rows_of(item, b):
        return pl.ds(pl.multiple_of(item_row0[item], MOE_ROW_BLOCK) + b * MOE_ROW_BLOCK, MOE_ROW_BLOCK)

    load = lambda item, b: pltpu.make_async_copy(xs.at[rows_of(item, b), :], x_in.at[blk(b), :], sem.at[0])
    store = lambda item, b: pltpu.make_async_copy(y_out.at[blk(b), :], ys.at[rows_of(item, b), :], sem.at[1])

    def for_blocks(item, fn):
        n = item_nblk[item]
        for b in range(max_blk):
            @pl.when(b < n)
            def _():
                fn(item, b)

    @pl.when(c == 0)
    def _():
        @pl.when(i == 0)
        def _():
            for_blocks(0, lambda it, b: load(it, b).start())

        for_blocks(i, lambda it, b: load(it, b).wait())

        def cast(it, b):
            x_bf[blk(b), :] = _unpack_halves(x_in[blk(b), :]).astype(BF16)

        for_blocks(i, cast)

        @pl.when(i + 1 < n_items)
        def _():
            for_blocks(i + 1, lambda it, b: load(it, b).start())

        def clear(it, b):
            acc[blk(b), :] = jnp.zeros((MOE_ROW_BLOCK, acc.shape[1]), F32)

        for_blocks(i, clear)

    @pl.when(nblk > 0)
    def _():
        w1_bf[...] = w1_ref[0].astype(BF16)
        w3_bf[...] = w3_ref[0].astype(BF16)
        w2_bf[...] = w2_ref[0].astype(BF16)

        def body(b, carry):
            rows = pl.ds(pl.multiple_of(b * MOE_ROW_BLOCK, MOE_ROW_BLOCK), MOE_ROW_BLOCK)
            x = x_bf[rows, :]
            h1 = _dot(x, w1_bf[...])
            h3 = _dot(x, w3_bf[...])
            a = (h1 * jax.nn.sigmoid(h1) * h3).astype(BF16)
            acc[rows, :] += _dot(a, w2_bf[...])
            return carry

        lax.fori_loop(0, nblk, body, 0)

    @pl.when(c == last_c)
    def _():
        @pl.when(i > 0)
        def _():
            for_blocks(i - 1, lambda it, b: store(it, b).wait())

        def pack(it, b):
            y_out[blk(b), :] = _pack_halves(acc[blk(b), :])

        for_blocks(i, pack)
        for_blocks(i, lambda it, b: store(it, b).start())

        @pl.when(i == n_items - 1)
        def _():
            for_blocks(i, lambda it, b: store(it, b).wait())


def _moe_call(item_e, item_row0, item_nblk, xs, w1, w3, w2):
    padded_rows, d_packed = xs.shape
    d = w1.shape[1]
    n_items = item_e.shape[0]
    f = w1.shape[2]
    nc = f // MOE_F_CHUNK
    chunk = lambda i, c, ib: jnp.where(ib[i] > 0, c, nc - 1)
    return pl.pallas_call(
        _moe_kernel,
        grid_spec=pltpu.PrefetchScalarGridSpec(
            num_scalar_prefetch=3,
            grid=(n_items, nc),
            in_specs=[pl.BlockSpec(memory_space=pl.ANY),
                      pl.BlockSpec((1, d, MOE_F_CHUNK), lambda i, c, ie, ir, ib: (ie[i], 0, chunk(i, c, ib))),
                      pl.BlockSpec((1, d, MOE_F_CHUNK), lambda i, c, ie, ir, ib: (ie[i], 0, chunk(i, c, ib))),
                      pl.BlockSpec((1, MOE_F_CHUNK, d), lambda i, c, ie, ir, ib: (ie[i], chunk(i, c, ib), 0))],
            out_specs=pl.BlockSpec(memory_space=pl.ANY),
            scratch_shapes=[pltpu.VMEM((MOE_ROW_GROUP, d_packed), xs.dtype),
                            pltpu.VMEM((MOE_ROW_GROUP, d), BF16),
                            pltpu.VMEM((MOE_ROW_GROUP, d), F32),
                            pltpu.VMEM((MOE_ROW_GROUP, d_packed), xs.dtype),
                            pltpu.VMEM((d, MOE_F_CHUNK), BF16),
                            pltpu.VMEM((d, MOE_F_CHUNK), BF16),
                            pltpu.VMEM((MOE_F_CHUNK, d), BF16),
                            pltpu.SemaphoreType.DMA((2,))]),
        out_shape=jax.ShapeDtypeStruct((padded_rows, d_packed), xs.dtype),
        compiler_params=_cparams(("arbitrary", "arbitrary")),
        name="moe",
    )(item_e, item_row0, item_nblk, xs, w1, w3, w2)


COMBINE_TM = 256


def _combine_kernel(dest_ref, x1_ref, gw_ref, g2_ref, ys, o_ref, y0, y1, sem, *, tok_base, n_tok):
    tok0 = tok_base + pl.program_id(0) * COMBINE_TM

    def issue(g, carry):
        base = pl.multiple_of(g * ROW_DMA_GROUP, ROW_DMA_GROUP)
        for j in range(ROW_DMA_GROUP):
            _row_copy(ys, dest_ref[tok0 + base + j], y0, base + j, sem).start()
            _row_copy(ys, dest_ref[n_tok + tok0 + base + j], y1, base + j, sem).start()
        return carry

    def drain(g, carry):
        for j in range(ROW_DMA_GROUP):
            _row_copy(ys, 0, y0, 0, sem).wait()
            _row_copy(ys, 0, y1, 0, sem).wait()
        return carry

    lax.fori_loop(0, COMBINE_TM // ROW_DMA_GROUP, issue, 0)
    lax.fori_loop(0, COMBINE_TM // ROW_DMA_GROUP, drain, 0)
    gw = gw_ref[...]
    moe = gw[:, 0:1] * _unpack_halves(y0[...]) + gw[:, 1:2] * _unpack_halves(y1[...])
    o_ref[...] = x1_ref[...] + g2_ref[0] * moe


def _combine_call(dest_flat, x1, gw, g2, ys, tok_base, n_tok, tiles_per_group):
    m, d = x1.shape
    return pl.pallas_call(
        functools.partial(_combine_kernel, tok_base=tok_base, n_tok=n_tok),
        grid_spec=pltpu.PrefetchScalarGridSpec(
            num_scalar_prefetch=1,
            grid=(m // COMBINE_TM,),
            in_specs=[pl.BlockSpec((COMBINE_TM, d), lambda i, dr: (i, 0)),
                      pl.BlockSpec((COMBINE_TM, 2), lambda i, dr: (i, 0)),
                      pl.BlockSpec((1, 1, d), lambda i, dr: (i // tiles_per_group, 0, 0)),
                      pl.BlockSpec(memory_space=pl.ANY)],
            out_specs=pl.BlockSpec((COMBINE_TM, d), lambda i, dr: (i, 0)),
            scratch_shapes=[pltpu.VMEM((COMBINE_TM,) + ys.shape[1:], ys.dtype),
                            pltpu.VMEM((COMBINE_TM,) + ys.shape[1:], ys.dtype),
                            pltpu.SemaphoreType.DMA]),
        out_shape=jax.ShapeDtypeStruct((m, d), F32),
        compiler_params=_cparams(("arbitrary",)),
        name="combine",
    )(dest_flat, x1, gw, g2, ys)


def kernel(x_prompt, x_sample, cache_a_k, cache_a_v, cache_b_k, cache_b_v, c, c_ctx, norm1_w, norm2_w, w_ada, b_ada, w_in, qn_a, kn_a, qn_b, kn_b, rpb_a, sink_b, w_pa, w_pb, w_out, w_rg, b_rg, w_re, b_re, w1, w3, w2):
    batch, seq, d = x_prompt.shape
    dec_batch, dec_seq, _ = x_sample.shape
    depth = norm1_w.shape[0]
    assert depth == 1, "one trunk layer"
    past = cache_a_k.shape[2]
    n_ctx, n_lat = batch * seq, dec_batch * dec_seq
    n_tok = n_ctx + n_lat

    xc = x_prompt.reshape(n_ctx, d)
    xl = x_sample.reshape(n_lat, d)

    cond = jnp.concatenate([c_ctx[None, :], c], axis=0)
    mod = _ada_call(cond, w_ada[0], b_ada[0][None, :])
    sh1, sc1, g1, sh2, sc2, g2 = [mod[:, i * d:(i + 1) * d][:, None, :] for i in range(6)]
    ctx_rows, lat_rows = slice(0, 1), slice(1, 1 + dec_batch)

    nw1, nw2 = norm1_w[0][None, :], norm2_w[0][None, :]
    qna, kna, qnb, knb = qn_a[0][None, :], kn_a[0][None, :], qn_b[0][None, :], kn_b[0][None, :]
    sink = sink_b[0]
    w_in_bf = w_in[0].astype(BF16)
    wpa_bf, wpb_bf, wout_bf = w_pa[0].astype(BF16), w_pb[0].astype(BF16), w_out[0].astype(BF16)
    wr = jnp.zeros((d, LOGIT_PAD), F32).at[:, :N_EXPERTS].set(w_re[0]).at[:, N_EXPERTS:N_EXPERTS + N_GROUPS].set(w_rg[0])
    br = jnp.zeros((LOGIT_PAD, 1), F32).at[:N_EXPERTS, 0].set(b_re[0]).at[N_EXPERTS:N_EXPERTS + N_GROUPS, 0].set(b_rg[0])

    proj_c, gates_c = _inproj_call(xc, nw1, sc1[ctx_rows], sh1[ctx_rows], w_in_bf, n_ctx // INPROJ_TM)
    proj_l, gates_l = _inproj_call(xl, nw1, sc1[lat_rows], sh1[lat_rows], w_in_bf, dec_seq // INPROJ_TM)

    oa_c, ob_c, new_a_k, new_a_v, new_b_k, new_b_v = _ctx_attn_call(proj_c, seq, qna, kna, qnb, knb, sink)

    bias_blocks, pair0 = _na_bias_blocks(rpb_a[0], dec_seq)
    cos, sin_a, sin_b = _rope_tables(dec_seq)
    rows_of = lambda cache: cache.reshape(-1, HEAD_DIM)
    oa_l = _lat_attn_a_call(proj_l, dec_seq, past, rows_of(cache_a_k), rows_of(cache_a_v),
                            bias_blocks, pair0, qna, kna)
    ob_l = _lat_attn_b_call(proj_l, dec_seq, past, rows_of(cache_b_k), rows_of(cache_b_v),
                            cos, sin_a, sin_b, qnb, knb, sink)

    mix_c = _mix_call(oa_c, ob_c, gates_c, wpa_bf, wpb_bf)
    mix_l = _mix_call(oa_l, ob_l, gates_l, wpa_bf, wpb_bf)
    x1_c, h2_c, lg_c = _outproj_call(mix_c, wout_bf, xc, g1[ctx_rows], sc2[ctx_rows], sh2[ctx_rows],
                                     nw2, wr, n_ctx // MERGE_TM)
    x1_l, h2_l, lg_l = _outproj_call(mix_l, wout_bf, xl, g1[lat_rows], sc2[lat_rows], sh2[lat_rows],
                                     nw2, wr, dec_seq // MERGE_TM)

    eid, gw, rank, cnt = _route_call(jnp.concatenate([lg_c, lg_l], axis=0), br)
    padded_rows, n_items = _moe_layout(2 * n_tok)
    dest, tail, item_e, item_row0, item_nblk = _routing_tables(eid[:2], rank[:2], cnt[:, 0], n_items)
    dest_flat = dest.reshape(-1)
    xs = _dispatch_call(dest_flat, tail, h2_c, h2_l, padded_rows)
    ys = _moe_call(item_e, item_row0, item_nblk, xs, w1[0], w3[0], w2[0])
    gw_t = gw[:2].T
    y_c = _combine_call(dest_flat, x1_c, gw_t[:n_ctx], g2[ctx_rows], ys, 0, n_tok, n_ctx // COMBINE_TM)
    y_l = _combine_call(dest_flat, x1_l, gw_t[n_ctx:], g2[lat_rows], ys, n_ctx, n_tok, dec_seq // COMBINE_TM)

    state = lambda a, heads: a.reshape(batch, 1, seq, heads, HEAD_DIM)
    return (y_c.reshape(batch, seq, d), y_l.reshape(dec_batch, dec_seq, d),
            state(new_a_k, NA_HEADS), state(new_a_v, NA_HEADS),
            state(new_b_k, NB_KV_HEADS), state(new_b_v, NB_KV_HEADS))
```

```python
import functools

import jax
import jax.numpy as jnp
import numpy as np
from jax import lax
from jax.experimental import pallas as pl
from jax.experimental.pallas import tpu as pltpu

D_MODEL = 2048
HEAD_DIM = 128
NA_HEADS = 8
NA_WIDTH = NA_HEADS * HEAD_DIM
NB_Q_HEADS = 8
NB_KV_HEADS = 2
NB_GROUP = NB_Q_HEADS // NB_KV_HEADS
NB_WIDTH = NB_Q_HEADS * HEAD_DIM
NB_KV_WIDTH = NB_KV_HEADS * HEAD_DIM
GRID_W = 64
NA_WIN_ROWS = 8
NA_WIN_COLS = 16
WINDOW = 128
N_GROUPS = 4
EXPERTS_PER_GROUP = 8
N_EXPERTS = N_GROUPS * EXPERTS_PER_GROUP
D_EXPERT = 1024
IN_WIDTH = 3 * NA_WIDTH + NB_WIDTH + 2 * NB_KV_WIDTH + 2 * D_MODEL
ROPE_BASE = 10000.0
NORM_EPS = 1e-6
NEG_INF = -1e30
ATTN_SCALE = HEAD_DIM ** -0.5

QA_HEAD0 = 0
KA_HEAD0 = NA_HEADS
VA_HEAD0 = 2 * NA_HEADS
QB_HEAD0 = 3 * NA_HEADS
KB_HEAD0 = QB_HEAD0 + NB_Q_HEADS
VB_HEAD0 = KB_HEAD0 + NB_KV_HEADS
GATE_COL0 = (VB_HEAD0 + NB_KV_HEADS) * HEAD_DIM

LOGIT_PAD = 128
MOE_ROW_BLOCK = 256
MOE_ROW_GROUP = 1024
MOE_F_CHUNK = 512
VMEM_LIMIT = 56 * 1024 * 1024

F32 = jnp.float32
BF16 = jnp.bfloat16


def _cparams(sem):
    return pltpu.CompilerParams(dimension_semantics=sem, vmem_limit_bytes=VMEM_LIMIT)


def _rms(x, w):
    x = x.astype(F32)
    return x * lax.rsqrt(jnp.mean(x * x, axis=-1, keepdims=True) + NORM_EPS) * w


def _dot(a, b):
    return jnp.dot(a, b, preferred_element_type=F32)


def _dot_nt(a, b):
    return lax.dot_general(a, b, (((1,), (1,)), ((), ())), preferred_element_type=F32)


def _pack_halves(x):
    n = x.shape[1] // 2
    lo = lax.bitcast_convert_type(x[:, :n].astype(BF16).astype(F32), jnp.uint32)
    hi = lax.bitcast_convert_type(x[:, n:].astype(BF16).astype(F32), jnp.uint32)
    return hi | (lo >> 16)


def _unpack_halves(w):
    lo = lax.bitcast_convert_type(w << 16, F32)
    hi = lax.bitcast_convert_type(w & jnp.uint32(0xFFFF0000), F32)
    return jnp.concatenate([lo, hi], axis=1)


def _dot_split(a, b):
    a_hi = a.astype(BF16)
    a_lo = (a - a_hi.astype(F32)).astype(BF16)
    b_hi = b.astype(BF16)
    b_lo = (b - b_hi.astype(F32)).astype(BF16)
    return _dot(a_hi, b_hi) + (_dot(a_lo, b_hi) + _dot(a_hi, b_lo))


ADA_ROWS = 8
ADA_TN = 1024


def _ada_kernel(c_ref, w_ref, b_ref, o_ref):
    n_rows, d, lanes = c_ref.shape
    tn = w_ref.shape[1]

    def body(kb, acc):
        ks = pl.ds(pl.multiple_of(kb * 8, 8), 8)
        w = w_ref[ks, :]
        out = []
        for r in range(n_rows):
            c = c_ref[r, ks, :]
            s = c * jax.nn.sigmoid(c)
            out.append(acc[r] + w * jnp.concatenate([s] * (tn // lanes), axis=1))
        return tuple(out)

    acc = lax.fori_loop(0, d // 8, body, tuple(jnp.zeros((8, tn), F32) for _ in range(n_rows)), unroll=8)
    ri = lax.broadcasted_iota(jnp.int32, (ADA_ROWS, tn), 0)
    res = jnp.zeros((ADA_ROWS, tn), F32)
    for r in range(n_rows):
        row = jnp.sum(acc[r], axis=0, keepdims=True) + b_ref[...]
        res = jnp.where(ri == r, row, res)
    o_ref[...] = res


def _ada_call(cond, w_ada, b_ada):
    n_rows, d = cond.shape
    n = w_ada.shape[1]
    lanes = 128
    cond_lanes = jnp.broadcast_to(cond[:, :, None], (n_rows, d, lanes))
    return pl.pallas_call(
        _ada_kernel,
        grid=(n // ADA_TN,),
        in_specs=[pl.BlockSpec((n_rows, d, lanes), lambda j: (0, 0, 0)),
                  pl.BlockSpec((d, ADA_TN), lambda j: (0, j)),
                  pl.BlockSpec((1, ADA_TN), lambda j: (0, j))],
        out_specs=pl.BlockSpec((ADA_ROWS, ADA_TN), lambda j: (0, j)),
        out_shape=jax.ShapeDtypeStruct((ADA_ROWS, n), F32),
        compiler_params=_cparams(("arbitrary",)),
        name="ada",
    )(cond_lanes, w_ada, b_ada)


INPROJ_TM = 1024
INPROJ_TN = 512
NORM_ROWS = 128


def _inproj_kernel(x_ref, nw_ref, sc_ref, sh_ref, w_ref, qkv_ref, gate_ref, h_scr, *, qkv_tiles):
    j = pl.program_id(1)

    @pl.when(j == 0)
    def _():
        nw = nw_ref[...]
        sc = 1.0 + sc_ref[0]
        sh = sh_ref[0]

        def body(r, carry):
            rows = pl.ds(pl.multiple_of(r * NORM_ROWS, NORM_ROWS), NORM_ROWS)
            h_scr[rows, :] = (_rms(x_ref[rows, :], nw) * sc + sh).astype(BF16)
            return carry

        lax.fori_loop(0, INPROJ_TM // NORM_ROWS, body, 0)

    res = _dot(h_scr[...], w_ref[...])

    @pl.when(j < qkv_tiles)
    def _():
        qkv_ref[...] = res.astype(qkv_ref.dtype)

    @pl.when(j >= qkv_tiles)
    def _():
        gate_ref[...] = res.astype(gate_ref.dtype)


def _inproj_call(x, nw, sc, sh, w_bf, tiles_per_group):
    m, d = x.shape
    n = w_bf.shape[1]
    qkv_tiles = GATE_COL0 // INPROJ_TN
    grp = lambda i, j: (i // tiles_per_group, 0, 0)
    return pl.pallas_call(
        functools.partial(_inproj_kernel, qkv_tiles=qkv_tiles),
        grid=(m // INPROJ_TM, n // INPROJ_TN),
        in_specs=[pl.BlockSpec((INPROJ_TM, d), lambda i, j: (i, 0)),
                  pl.BlockSpec((1, d), lambda i, j: (0, 0)),
                  pl.BlockSpec((1, 1, d), grp),
                  pl.BlockSpec((1, 1, d), grp),
                  pl.BlockSpec((d, INPROJ_TN), lambda i, j: (0, j))],
        out_specs=[pl.BlockSpec((INPROJ_TM, INPROJ_TN), lambda i, j: (i, jnp.minimum(j, qkv_tiles - 1))),
                   pl.BlockSpec((INPROJ_TM, INPROJ_TN), lambda i, j: (i, jnp.maximum(j - qkv_tiles, 0)))],
        out_shape=[jax.ShapeDtypeStruct((m, GATE_COL0), BF16),
                   jax.ShapeDtypeStruct((m, n - GATE_COL0), BF16)],
        scratch_shapes=[pltpu.VMEM((INPROJ_TM, d), BF16)],
        compiler_params=_cparams(("arbitrary", "arbitrary")),
        name="inproj",
    )(x, nw, sc, sh, w_bf)


def _with_ones(v):
    return jnp.concatenate([v.astype(BF16), jnp.ones(v.shape, BF16)], axis=1)


def _softmax_pv(scores, values_with_ones, sink=None):
    m = None
    for s in scores:
        ms = jnp.max(s, axis=-1, keepdims=True)
        m = ms if m is None else jnp.maximum(m, ms)
    if sink is not None:
        m = jnp.maximum(m, sink)
    acc = None
    for s, v1 in zip(scores, values_with_ones):
        pv = _dot(jnp.exp(s - m).astype(BF16), v1)
        acc = pv if acc is None else acc + pv
    d = acc.shape[1] // 2
    den = acc[:, d:]
    if sink is not None:
        den = den + jnp.exp(sink - m)
    return acc[:, :d] / den


def _ctx_attn_kernel(qa_ref, ka_ref, va_ref, qb_ref, kb_ref, vb_ref,
                     qna_ref, kna_ref, qnb_ref, knb_ref, sink_ref,
                     oa_ref, ob_ref, nak_ref, nav_ref, nbk_ref, nbv_ref):
    qna, kna, qnb, knb = qna_ref[...], kna_ref[...], qnb_ref[...], knb_ref[...]
    seq = qa_ref.shape[0]
    for h in range(NA_HEADS):
        cols = slice(h * HEAD_DIM, (h + 1) * HEAD_DIM)
        q = (_rms(qa_ref[:, cols], qna) * ATTN_SCALE).astype(BF16)
        k = _rms(ka_ref[:, cols], kna)
        v = va_ref[:, cols]
        nak_ref[pl.ds(h, seq, stride=NA_HEADS), :] = k
        nav_ref[pl.ds(h, seq, stride=NA_HEADS), :] = v.astype(F32)
        s = _dot_nt(q, k.astype(BF16))
        oa_ref[:, cols] = _softmax_pv([s], [_with_ones(v)]).astype(oa_ref.dtype)
    for kv in range(NB_KV_HEADS):
        kcols = slice(kv * HEAD_DIM, (kv + 1) * HEAD_DIM)
        k = _rms(kb_ref[:, kcols], knb)
        v = vb_ref[:, kcols]
        nbk_ref[pl.ds(kv, seq, stride=NB_KV_HEADS), :] = k
        nbv_ref[pl.ds(kv, seq, stride=NB_KV_HEADS), :] = v.astype(F32)
        kb16 = k.astype(BF16)
        vb1 = _with_ones(v)
        for g in range(NB_GROUP):
            hq = kv * NB_GROUP + g
            cols = slice(hq * HEAD_DIM, (hq + 1) * HEAD_DIM)
            q = (_rms(qb_ref[:, cols], qnb) * ATTN_SCALE).astype(BF16)
            s = _dot_nt(q, kb16)
            ob_ref[:, cols] = _softmax_pv([s], [vb1], sink=sink_ref[hq]).astype(ob_ref.dtype)


def _ctx_attn_call(proj, seq, qna, kna, qnb, knb, sink):
    m = proj.shape[0]
    nb = m // seq
    wide = lambda blk: pl.BlockSpec((seq, NA_WIDTH), lambda b: (b, blk))
    narrow = lambda blk: pl.BlockSpec((seq, NB_KV_WIDTH), lambda b: (b, blk))
    vec = pl.BlockSpec((1, HEAD_DIM), lambda b: (0, 0))
    return pl.pallas_call(
        _ctx_attn_kernel,
        grid=(nb,),
        in_specs=[wide(QA_HEAD0 // NA_HEADS), wide(KA_HEAD0 // NA_HEADS), wide(VA_HEAD0 // NA_HEADS),
                  wide(QB_HEAD0 // NA_HEADS), narrow(KB_HEAD0 // NB_KV_HEADS), narrow(VB_HEAD0 // NB_KV_HEADS),
                  vec, vec, vec, vec,
                  pl.BlockSpec(memory_space=pltpu.SMEM)],
        out_specs=[pl.BlockSpec((seq, NA_WIDTH), lambda b: (b, 0)),
                   pl.BlockSpec((seq, NB_WIDTH), lambda b: (b, 0)),
                   pl.BlockSpec((seq * NA_HEADS, HEAD_DIM), lambda b: (b, 0)),
                   pl.BlockSpec((seq * NA_HEADS, HEAD_DIM), lambda b: (b, 0)),
                   pl.BlockSpec((seq * NB_KV_HEADS, HEAD_DIM), lambda b: (b, 0)),
                   pl.BlockSpec((seq * NB_KV_HEADS, HEAD_DIM), lambda b: (b, 0))],
        out_shape=[jax.ShapeDtypeStruct((m, NA_WIDTH), BF16),
                   jax.ShapeDtypeStruct((m, NB_WIDTH), BF16),
                   jax.ShapeDtypeStruct((m * NA_HEADS, HEAD_DIM), F32),
                   jax.ShapeDtypeStruct((m * NA_HEADS, HEAD_DIM), F32),
                   jax.ShapeDtypeStruct((m * NB_KV_HEADS, HEAD_DIM), F32),
                   jax.ShapeDtypeStruct((m * NB_KV_HEADS, HEAD_DIM), F32)],
        compiler_params=_cparams(("arbitrary",)),
        name="ctx_attn",
    )(proj, proj, proj, proj, proj, proj, qna, kna, qnb, knb, sink)


def _rope(x, cos, sin_a, sin_b):
    quarter = HEAD_DIM // 4
    return (x * cos + pltpu.roll(x, HEAD_DIM - quarter, 1) * sin_a
            + pltpu.roll(x, quarter, 1) * sin_b)


def _head_rows(cache_ref, head, n_heads):
    past = cache_ref.shape[0] // n_heads
    return cache_ref[pl.ds(head, past, stride=n_heads), :]


def _lat_attn_a_kernel(q_ref, k_ref, v_ref, ck_ref, cv_ref, cb_ref, qn_ref, kn_ref, o_ref, bias_scr,
                       *, pair0):
    head = pl.program_id(0)

    @pl.when(pl.program_id(1) == 0)
    def _():
        bias_scr[...] = jnp.full(bias_scr.shape, NEG_INF, F32)
        for qr, p0 in enumerate(pair0):
            for j in range(cb_ref.shape[1]):
                bias_scr[qr * GRID_W:(qr + 1) * GRID_W, (p0 + j) * 2 * GRID_W:(p0 + j + 1) * 2 * GRID_W] = cb_ref[qr, j]

    q = (_rms(q_ref[...], qn_ref[...]) * ATTN_SCALE).astype(BF16)
    k = _rms(k_ref[...], kn_ref[...]).astype(BF16)
    s_lat = _dot_nt(q, k) + bias_scr[...]
    ck = _head_rows(ck_ref, head, NA_HEADS).astype(BF16)
    cv = _head_rows(cv_ref, head, NA_HEADS)
    s_ctx = _dot_nt(q, ck)
    o = _softmax_pv([s_lat, s_ctx], [_with_ones(v_ref[...]), _with_ones(cv)])
    o_ref[...] = o.astype(o_ref.dtype)


def _lat_attn_b_kernel(q_ref, k_ref, v_ref, ck_ref, cv_ref, cos_ref, sina_ref, sinb_ref,
                       qn_ref, kn_ref, sink_ref, o_ref):
    cos, sin_a, sin_b = cos_ref[...], sina_ref[...], sinb_ref[...]
    q = _rope(_rms(q_ref[...], qn_ref[...]), cos, sin_a, sin_b)
    k = _rope(_rms(k_ref[...], kn_ref[...]), cos, sin_a, sin_b)
    q = (q * ATTN_SCALE).astype(BF16)
    k = k.astype(BF16)
    v1 = _with_ones(v_ref[...])
    length = q.shape[0]
    kv = pl.program_id(0) // NB_GROUP
    ck = _head_rows(ck_ref, kv, NB_KV_HEADS).astype(BF16)
    cv1 = _with_ones(_head_rows(cv_ref, kv, NB_KV_HEADS))
    sink = sink_ref[pl.program_id(0)]
    for qb in range(length // WINDOW):
        rows = slice(qb * WINDOW, (qb + 1) * WINDOW)
        lo, hi = max(0, (qb - 1) * WINDOW), min(length, (qb + 2) * WINDOW)
        qs = q[rows]
        s_win = _dot_nt(qs, k[lo:hi])
        qi = qb * WINDOW + lax.broadcasted_iota(jnp.int32, s_win.shape, 0)
        kj = lo + lax.broadcasted_iota(jnp.int32, s_win.shape, 1)
        s_win = jnp.where(jnp.abs(qi - kj) <= WINDOW, s_win, NEG_INF)
        s_ctx = _dot_nt(qs, ck)
        o_ref[rows, :] = _softmax_pv([s_win, s_ctx], [v1[lo:hi], cv1], sink=sink).astype(o_ref.dtype)


def _lat_attn_a_call(proj, length, past, ck, cv, cb, pair0, qn, kn):
    m = proj.shape[0]
    head = lambda h0: pl.BlockSpec((length, HEAD_DIM), lambda h, b: (b, h0 + h))
    cache = pl.BlockSpec((past * NA_HEADS, HEAD_DIM), lambda h, b: (b, 0))
    vec = pl.BlockSpec((1, HEAD_DIM), lambda h, b: (0, 0))
    return pl.pallas_call(
        functools.partial(_lat_attn_a_kernel, pair0=pair0),
        grid=(NA_HEADS, m // length),
        in_specs=[head(QA_HEAD0), head(KA_HEAD0), head(VA_HEAD0), cache, cache,
                  pl.BlockSpec((None,) + cb.shape[1:], lambda h, b: (h, 0, 0, 0, 0)), vec, vec],
        out_specs=pl.BlockSpec((length, HEAD_DIM), lambda h, b: (b, h)),
        out_shape=jax.ShapeDtypeStruct((m, NA_WIDTH), BF16),
        scratch_shapes=[pltpu.VMEM((length, length), F32)],
        compiler_params=_cparams(("arbitrary", "arbitrary")),
        name="lat_attn_a",
    )(proj, proj, proj, ck, cv, cb, qn, kn)


def _lat_attn_b_call(proj, length, past, ck, cv, cos, sin_a, sin_b, qn, kn, sink):
    m = proj.shape[0]
    qspec = pl.BlockSpec((length, HEAD_DIM), lambda h, b: (b, QB_HEAD0 + h))
    kvspec = lambda h0: pl.BlockSpec((length, HEAD_DIM), lambda h, b: (b, h0 + h // NB_GROUP))
    cache = pl.BlockSpec((past * NB_KV_HEADS, HEAD_DIM), lambda h, b: (b, 0))
    table = pl.BlockSpec((length, HEAD_DIM), lambda h, b: (0, 0))
    vec = pl.BlockSpec((1, HEAD_DIM), lambda h, b: (0, 0))
    return pl.pallas_call(
        _lat_attn_b_kernel,
        grid=(NB_Q_HEADS, m // length),
        in_specs=[qspec, kvspec(KB_HEAD0), kvspec(VB_HEAD0), cache, cache,
                  table, table, table, vec, vec, pl.BlockSpec(memory_space=pltpu.SMEM)],
        out_specs=pl.BlockSpec((length, HEAD_DIM), lambda h, b: (b, h)),
        out_shape=jax.ShapeDtypeStruct((m, NB_WIDTH), BF16),
        compiler_params=_cparams(("arbitrary", "arbitrary")),
        name="lat_attn_b",
    )(proj, proj, proj, ck, cv, cos, sin_a, sin_b, qn, kn, sink)


def _na_bias_blocks(rpb, length):
    rows = length // GRID_W
    kr_n = min(NA_WIN_ROWS, rows)
    n_pairs = min(kr_n // 2 + 1, rows // 2)
    r = np.arange(rows)
    c = np.arange(GRID_W)
    r0 = np.clip(r - kr_n // 2, 0, rows - kr_n)
    c0 = np.clip(c - NA_WIN_COLS // 2, 0, GRID_W - NA_WIN_COLS)
    pair0 = np.minimum(r0 // 2, rows // 2 - n_pairs)
    kr = 2 * (pair0[:, None, None] + np.arange(n_pairs)[None, :, None]) + np.arange(2)[None, None, :]
    row_ok = (kr >= r0[:, None, None]) & (kr < r0[:, None, None] + kr_n)
    col_ok = (c[None, :] >= c0[:, None]) & (c[None, :] < c0[:, None] + NA_WIN_COLS)
    dr = kr - r[:, None, None] + (NA_WIN_ROWS - 1)
    dc = np.clip(c[None, :] - c[:, None], -(NA_WIN_COLS - 1), NA_WIN_COLS - 1) + (NA_WIN_COLS - 1)
    row_sel = ((dr[None] == np.arange(2 * NA_WIN_ROWS - 1)[:, None, None, None]) & row_ok[None]).astype(np.float32)
    col_hit = (dc[None] == np.arange(2 * NA_WIN_COLS - 1)[:, None, None]) & col_ok[None]
    col_sel = np.zeros((2,) + col_hit.shape[:2] + (2 * GRID_W,), np.float32)
    for half in range(2):
        col_sel[half, :, :, half * GRID_W:(half + 1) * GRID_W] = col_hit
    hi = lax.Precision.HIGHEST
    per_col = jnp.einsum("hde,lecn->hldcn", rpb.astype(F32), col_sel, precision=hi)
    table = jnp.einsum("dqjl,hldcn->hqjcn", row_sel, per_col, precision=hi)
    valid = (row_ok[:, :, None, :, None] & col_ok[None, None, :, None, :]).reshape(rows, n_pairs, GRID_W, 2 * GRID_W)
    return jnp.where(valid[None], table, NEG_INF), tuple(int(p) for p in pair0)


def _rope_tables(length):
    t = jnp.arange(length)
    row = (t // GRID_W).astype(F32)
    col = (t % GRID_W).astype(F32)
    n_freq = HEAD_DIM // 4
    inv = ROPE_BASE ** (-jnp.arange(n_freq, dtype=F32) / n_freq)
    ar = row[:, None] * inv
    ac = col[:, None] * inv
    ang = jnp.concatenate([ar, ar, ac, ac], axis=-1)
    cos, sin = jnp.cos(ang), jnp.sin(ang)
    lane = jnp.arange(HEAD_DIM)
    takes_left = ((lane // n_freq) % 2 == 0)[None, :]
    return cos, jnp.where(takes_left, -sin, 0.0), jnp.where(takes_left, 0.0, sin)


MERGE_TM = 512


def _resident(shape):
    zeros = (0,) * len(shape)
    return pl.BlockSpec(shape, lambda i: zeros, pipeline_mode=pl.Buffered(1))


def _mix_kernel(oa_ref, ob_ref, ga_ref, gb_ref, wpa_ref, wpb_ref, mix_ref):
    ya = _dot(oa_ref[...], wpa_ref[...])
    yb = _dot(ob_ref[...], wpb_ref[...])
    mix = (jax.nn.sigmoid(ga_ref[...].astype(F32)) * ya
           + jax.nn.sigmoid(gb_ref[...].astype(F32)) * yb)
    mix_ref[...] = mix.astype(mix_ref.dtype)


def _mix_call(oa, ob, gates, wpa_bf, wpb_bf):
    m = oa.shape[0]
    d = wpa_bf.shape[1]
    row = lambda i: (i, 0)
    return pl.pallas_call(
        _mix_kernel,
        grid=(m // MERGE_TM,),
        in_specs=[pl.BlockSpec((MERGE_TM, NA_WIDTH), row),
                  pl.BlockSpec((MERGE_TM, NB_WIDTH), row),
                  pl.BlockSpec((MERGE_TM, d), lambda i: (i, 0)),
                  pl.BlockSpec((MERGE_TM, d), lambda i: (i, 1)),
                  _resident((NA_WIDTH, d)), _resident((NB_WIDTH, d))],
        out_specs=pl.BlockSpec((MERGE_TM, d), row),
        out_shape=jax.ShapeDtypeStruct((m, d), BF16),
        compiler_params=_cparams(("arbitrary",)),
        name="mix",
    )(oa, ob, gates, gates, wpa_bf, wpb_bf)


def _outproj_kernel(mix_ref, wout_ref, x_ref, g1_ref, sc2_ref, sh2_ref, n2w_ref, wr_ref,
                    x1_ref, h2_ref, lg_ref, acc_ref):
    @pl.when(pl.program_id(0) == 0)
    def _():
        acc_ref[...] = jnp.zeros_like(acc_ref)

    prev = acc_ref[...]
    acc_ref[...] = _dot(mix_ref[...], wout_ref[...])
    x1 = x_ref[...] + g1_ref[0] * prev
    x1_ref[...] = x1
    h2 = _rms(x1, n2w_ref[...]) * (1.0 + sc2_ref[0]) + sh2_ref[0]
    h2_ref[...] = _pack_halves(h2)
    lg_ref[...] = _dot_split(h2, wr_ref[...])


def _outproj_call(mix, wout_bf, x, g1, sc2, sh2, n2w, wr, tiles_per_group):
    m, d = x.shape
    n_tiles = m // MERGE_TM
    ahead = lambda i: (jnp.minimum(i, n_tiles - 1), 0)
    done = lambda i: (jnp.maximum(i - 1, 0), 0)
    grp = lambda i: (jnp.maximum(i - 1, 0) // tiles_per_group, 0, 0)
    return pl.pallas_call(
        _outproj_kernel,
        grid=(n_tiles + 1,),
        in_specs=[pl.BlockSpec((MERGE_TM, d), ahead),
                  _resident((d, d)),
                  pl.BlockSpec((MERGE_TM, d), done),
                  pl.BlockSpec((1, 1, d), grp), pl.BlockSpec((1, 1, d), grp), pl.BlockSpec((1, 1, d), grp),
                  pl.BlockSpec((1, d), lambda i: (0, 0)),
                  _resident((d, LOGIT_PAD))],
        out_specs=[pl.BlockSpec((MERGE_TM, d), done),
                   pl.BlockSpec((MERGE_TM, d // 2), done),
                   pl.BlockSpec((MERGE_TM, LOGIT_PAD), done)],
        out_shape=[jax.ShapeDtypeStruct((m, d), F32),
                   jax.ShapeDtypeStruct((m, d // 2), jnp.uint32),
                   jax.ShapeDtypeStruct((m, LOGIT_PAD), F32)],
        scratch_shapes=[pltpu.VMEM((MERGE_TM, d), F32)],
        compiler_params=_cparams(("arbitrary",)),
        name="outproj",
    )(mix, wout_bf, x, g1, sc2, sh2, n2w, wr)


ROUTE_TM = 512


def _first_index_of_max(vals, idx, n):
    mx = jnp.max(vals, axis=0, keepdims=True)
    first = jnp.min(jnp.where(vals == mx, idx, n), axis=0, keepdims=True)
    return mx, first


def _route_kernel(lg_ref, bias_ref, eid_ref, gw_ref, rank_ref, cnt_ref, base_ref):
    step = pl.program_id(0)

    @pl.when(step == 0)
    def _():
        base_ref[...] = jnp.zeros_like(base_ref)

    lt = lg_ref[...].T + bias_ref[...]
    n_tok = lt.shape[1]
    le = lt[0:N_EXPERTS]
    lgrp = lt[N_EXPERTS:N_EXPERTS + N_GROUPS]
    gi = lax.broadcasted_iota(jnp.int32, (N_GROUPS, n_tok), 0)
    gmax, gsel = _first_index_of_max(lgrp, gi, N_GROUPS)
    pg_sel = 1.0 / jnp.sum(jnp.exp(lgrp - gmax), axis=0, keepdims=True)
    le_sel = jnp.zeros((EXPERTS_PER_GROUP, n_tok), F32)
    for g in range(N_GROUPS):
        le_sel = jnp.where(gsel == g, le[g * EXPERTS_PER_GROUP:(g + 1) * EXPERTS_PER_GROUP], le_sel)
    ei = lax.broadcasted_iota(jnp.int32, (EXPERTS_PER_GROUP, n_tok), 0)
    v0, i0 = _first_index_of_max(le_sel, ei, EXPERTS_PER_GROUP)
    rest = jnp.where(ei == i0, -jnp.inf, le_sel)
    v1, i1 = _first_index_of_max(rest, ei, EXPERTS_PER_GROUP)
    e1 = jnp.exp(v1 - v0)
    w0 = pg_sel / (1.0 + e1)
    w1 = pg_sel * e1 / (1.0 + e1)
    eid0 = gsel * EXPERTS_PER_GROUP + i0
    eid1 = gsel * EXPERTS_PER_GROUP + i1

    xi = lax.broadcasted_iota(jnp.int32, (N_EXPERTS, n_tok), 0)
    si = lax.broadcasted_iota(jnp.int32, (n_tok, n_tok), 0)
    ti = lax.broadcasted_iota(jnp.int32, (n_tok, n_tok), 1)
    before = (si < ti).astype(BF16)
    base = base_ref[...]
    hot0 = (xi == eid0).astype(F32)
    hot1 = (xi == eid1).astype(F32)
    pre0 = _dot(hot0.astype(BF16), before)
    pre1 = _dot(hot1.astype(BF16), before)
    tot0 = jnp.sum(hot0, axis=1, keepdims=True)
    tot1 = jnp.sum(hot1, axis=1, keepdims=True)
    rank0 = jnp.sum(hot0 * (base + pre0), axis=0, keepdims=True)
    rank1 = jnp.sum(hot1 * (base + tot0 + pre1), axis=0, keepdims=True)
    base = base + tot0 + tot1
    base_ref[...] = base

    ri = lax.broadcasted_iota(jnp.int32, (8, n_tok), 0)
    pick = lambda a, b: jnp.where(ri == 0, a, jnp.where(ri == 1, b, jnp.zeros_like(a)))
    eid_ref[...] = pick(eid0, eid1)
    gw_ref[...] = pick(w0, w1)
    rank_ref[...] = pick(rank0, rank1).astype(jnp.int32)
    cnt_ref[...] = jnp.broadcast_to(base, cnt_ref.shape).astype(jnp.int32)


def _route_call(logits, bias_col):
    t = logits.shape[0]
    tok = pl.BlockSpec((8, ROUTE_TM), lambda i: (0, i))
    return pl.pallas_call(
        _route_kernel,
        grid=(t // ROUTE_TM,),
        in_specs=[pl.BlockSpec((ROUTE_TM, LOGIT_PAD), lambda i: (i, 0)),
                  pl.BlockSpec((LOGIT_PAD, 1), lambda i: (0, 0))],
        out_specs=[tok, tok, tok, pl.BlockSpec((N_EXPERTS, 128), lambda i: (0, 0))],
        out_shape=[jax.ShapeDtypeStruct((8, t), jnp.int32),
                   jax.ShapeDtypeStruct((8, t), F32),
                   jax.ShapeDtypeStruct((8, t), jnp.int32),
                   jax.ShapeDtypeStruct((N_EXPERTS, 128), jnp.int32)],
        scratch_shapes=[pltpu.VMEM((N_EXPERTS, 1), F32)],
        compiler_params=_cparams(("arbitrary",)),
        name="route",
    )(logits, bias_col)


def _moe_layout(n_pairs):
    padded_rows = -(-(n_pairs + N_EXPERTS * (MOE_ROW_BLOCK - 1)) // MOE_ROW_BLOCK) * MOE_ROW_BLOCK
    n_items = -(-padded_rows // MOE_ROW_GROUP) + N_EXPERTS
    return padded_rows, n_items


def _routing_tables(eid, rank, counts, n_items):
    padded = (counts + MOE_ROW_BLOCK - 1) // MOE_ROW_BLOCK * MOE_ROW_BLOCK
    pad_end = jnp.cumsum(padded)
    pad_start = pad_end - padded
    hot = eid[..., None] == jnp.arange(N_EXPERTS, dtype=jnp.int32)
    dest = (jnp.sum(jnp.where(hot, pad_start, 0), axis=-1) + rank).astype(jnp.int32)
    tail = jnp.where(padded > counts, pad_end - MOE_ROW_BLOCK, -1).astype(jnp.int32)
    per_expert = (padded + MOE_ROW_GROUP - 1) // MOE_ROW_GROUP
    item_end = jnp.cumsum(per_expert)
    item_start = item_end - per_expert
    total = item_end[-1]
    ii = jnp.arange(n_items, dtype=jnp.int32)
    e_of = jnp.minimum(jnp.searchsorted(item_end, ii, side="right"), N_EXPERTS - 1).astype(jnp.int32)
    valid = ii < total
    e_last = e_of[jnp.maximum(total - 1, 0)]
    local = ii - item_start[e_of]
    row0 = pad_start[e_of] + local * MOE_ROW_GROUP
    nblk = jnp.clip((padded[e_of] - local * MOE_ROW_GROUP) // MOE_ROW_BLOCK, 0, MOE_ROW_GROUP // MOE_ROW_BLOCK)
    item_e = jnp.where(valid, e_of, e_last).astype(jnp.int32)
    item_row0 = jnp.where(valid, row0, 0).astype(jnp.int32)
    item_nblk = jnp.where(valid, nblk, 0).astype(jnp.int32)
    return dest, tail, item_e, item_row0, item_nblk


DISPATCH_TOKENS = 256
ROW_DMA_GROUP = 8


def _row_copy(src, s, dst, d, sem):
    return pltpu.make_async_copy(src.at[pl.ds(s, 1), :], dst.at[pl.ds(d, 1), :], sem)


def _dispatch_kernel(dest_ref, tail_ref, h_ctx, h_lat, xs, zero_buf, sem, *, n_ctx, n_tok):
    step = pl.program_id(0)
    tail_copy = lambda e: pltpu.make_async_copy(
        zero_buf, xs.at[pl.ds(pl.multiple_of(tail_ref[e], MOE_ROW_BLOCK), MOE_ROW_BLOCK), :], sem.at[1])

    @pl.when(step == 0)
    def _():
        zero_buf[...] = jnp.zeros_like(zero_buf)
        for e in range(N_EXPERTS):
            @pl.when(tail_ref[e] >= 0)
            def _():
                tail_copy(e).start()
        for e in range(N_EXPERTS):
            @pl.when(tail_ref[e] >= 0)
            def _():
                tail_copy(e).wait()

    tok0 = step * DISPATCH_TOKENS

    def scatter(src):
        def issue(g, carry):
            base = pl.multiple_of(g * ROW_DMA_GROUP, ROW_DMA_GROUP)
            for j in range(ROW_DMA_GROUP):
                _row_copy(src, base + j, xs, dest_ref[tok0 + base + j], sem.at[0]).start(priority=0)
                _row_copy(src, base + j, xs, dest_ref[n_tok + tok0 + base + j], sem.at[0]).start(priority=1)
            return carry

        def drain(g, carry):
            for j in range(2 * ROW_DMA_GROUP):
                _row_copy(src, 0, xs, 0, sem.at[0]).wait()
            return carry

        lax.fori_loop(0, DISPATCH_TOKENS // ROW_DMA_GROUP, issue, 0)
        lax.fori_loop(0, DISPATCH_TOKENS // ROW_DMA_GROUP, drain, 0)

    @pl.when(tok0 < n_ctx)
    def _():
        scatter(h_ctx)

    @pl.when(tok0 >= n_ctx)
    def _():
        scatter(h_lat)


def _dispatch_call(dest_flat, tail, h_ctx, h_lat, padded_rows):
    n_ctx, d = h_ctx.shape
    n_tok = n_ctx + h_lat.shape[0]
    ctx_tiles = n_ctx // DISPATCH_TOKENS
    return pl.pallas_call(
        functools.partial(_dispatch_kernel, n_ctx=n_ctx, n_tok=n_tok),
        grid_spec=pltpu.PrefetchScalarGridSpec(
            num_scalar_prefetch=2,
            grid=(n_tok // DISPATCH_TOKENS,),
            in_specs=[pl.BlockSpec((DISPATCH_TOKENS, d), lambda i, dr, tr: (jnp.minimum(i, ctx_tiles - 1), 0)),
                      pl.BlockSpec((DISPATCH_TOKENS, d), lambda i, dr, tr: (jnp.maximum(i - ctx_tiles, 0), 0))],
            out_specs=pl.BlockSpec(memory_space=pl.ANY),
            scratch_shapes=[pltpu.VMEM((MOE_ROW_BLOCK, d), h_ctx.dtype), pltpu.SemaphoreType.DMA((2,))]),
        out_shape=jax.ShapeDtypeStruct((padded_rows, d), h_ctx.dtype),
        compiler_params=_cparams(("arbitrary",)),
        name="dispatch",
    )(dest_flat, tail, h_ctx, h_lat)


def _moe_kernel(item_e, item_row0, item_nblk, xs, w1_ref, w3_ref, w2_ref, ys,
                x_in, x_bf, acc, y_out, w1_bf, w3_bf, w2_bf, sem):
    i = pl.program_id(0)
    c = pl.program_id(1)
    n_items = pl.num_programs(0)
    last_c = pl.num_programs(1) - 1
    nblk = item_nblk[i]
    max_blk = MOE_ROW_GROUP // MOE_ROW_BLOCK
    blk = lambda b: pl.ds(b * MOE_ROW_BLOCK, MOE_ROW_BLOCK)

    def rows_of(item, b):
        return pl.ds(pl.multiple_of(item_row0[item], MOE_ROW_BLOCK) + b * MOE_ROW_BLOCK, MOE_ROW_BLOCK)

    load = lambda item, b: pltpu.make_async_copy(xs.at[rows_of(item, b), :], x_in.at[blk(b), :], sem.at[0])
    store = lambda item, b: pltpu.make_async_copy(y_out.at[blk(b), :], ys.at[rows_of(item, b), :], sem.at[1])

    def for_blocks(item, fn):
        n = item_nblk[item]
        for b in range(max_blk):
            @pl.when(b < n)
            def _():
                fn(item, b)

    @pl.when(c == 0)
    def _():
        @pl.when(i == 0)
        def _():
            for_blocks(0, lambda it, b: load(it, b).start())

        for_blocks(i, lambda it, b: load(it, b).wait())

        def cast(it, b):
            x_bf[blk(b), :] = _unpack_halves(x_in[blk(b), :]).astype(BF16)

        for_blocks(i, cast)

        @pl.when(i + 1 < n_items)
        def _():
            for_blocks(i + 1, lambda it, b: load(it, b).start())

        def clear(it, b):
            acc[blk(b), :] = jnp.zeros((MOE_ROW_BLOCK, acc.shape[1]), F32)

        for_blocks(i, clear)

    @pl.when(nblk > 0)
    def _():
        w1_bf[...] = w1_ref[0].astype(BF16)
        w3_bf[...] = w3_ref[0].astype(BF16)
        w2_bf[...] = w2_ref[0].astype(BF16)

        def body(b, carry):
            rows = pl.ds(pl.multiple_of(b * MOE_ROW_BLOCK, MOE_ROW_BLOCK), MOE_ROW_BLOCK)
            x = x_bf[rows, :]
            h1 = _dot(x, w1_bf[...])
            h3 = _dot(x, w3_bf[...])
            a = (h1 * jax.nn.sigmoid(h1) * h3).astype(BF16)
            acc[rows, :] += _dot(a, w2_bf[...])
            return carry

        lax.fori_loop(0, nblk, body, 0)

    @pl.when(c == last_c)
    def _():
        @pl.when(i > 0)
        def _():
            for_blocks(i - 1, lambda it, b: store(it, b).wait())

        def pack(it, b):
            y_out[blk(b), :] = _pack_halves(acc[blk(b), :])

        for_blocks(i, pack)
        for_blocks(i, lambda it, b: store(it, b).start())

        @pl.when(i == n_items - 1)
        def _():
            for_blocks(i, lambda it, b: store(it, b).wait())


def _moe_call(item_e, item_row0, item_nblk, xs, w1, w3, w2):
    padded_rows, d_packed = xs.shape
    d = w1.shape[1]
    n_items = item_e.shape[0]
    f = w1.shape[2]
    nc = f // MOE_F_CHUNK
    chunk = lambda i, c, ib: jnp.where(ib[i] > 0, c, nc - 1)
    return pl.pallas_call(
        _moe_kernel,
        grid_spec=pltpu.PrefetchScalarGridSpec(
            num_scalar_prefetch=3,
            grid=(n_items, nc),
            in_specs=[pl.BlockSpec(memory_space=pl.ANY),
                      pl.BlockSpec((1, d, MOE_F_CHUNK), lambda i, c, ie, ir, ib: (ie[i], 0, chunk(i, c, ib))),
                      pl.BlockSpec((1, d, MOE_F_CHUNK), lambda i, c, ie, ir, ib: (ie[i], 0, chunk(i, c, ib))),
                      pl.BlockSpec((1, MOE_F_CHUNK, d), lambda i, c, ie, ir, ib: (ie[i], chunk(i, c, ib), 0))],
            out_specs=pl.BlockSpec(memory_space=pl.ANY),
            scratch_shapes=[pltpu.VMEM((MOE_ROW_GROUP, d_packed), xs.dtype),
                            pltpu.VMEM((MOE_ROW_GROUP, d), BF16),
                            pltpu.VMEM((MOE_ROW_GROUP, d), F32),
                            pltpu.VMEM((MOE_ROW_GROUP, d_packed), xs.dtype),
                            pltpu.VMEM((d, MOE_F_CHUNK), BF16),
                            pltpu.VMEM((d, MOE_F_CHUNK), BF16),
                            pltpu.VMEM((MOE_F_CHUNK, d), BF16),
                            pltpu.SemaphoreType.DMA((2,))]),
        out_shape=jax.ShapeDtypeStruct((padded_rows, d_packed), xs.dtype),
        compiler_params=_cparams(("arbitrary", "arbitrary")),
        name="moe",
    )(item_e, item_row0, item_nblk, xs, w1, w3, w2)


COMBINE_TM = 256


def _combine_kernel(dest_ref, x1_ref, gw_ref, g2_ref, ys, o_ref, y0, y1, sem, *, tok_base, n_tok):
    tok0 = tok_base + pl.program_id(0) * COMBINE_TM

    def issue(g, carry):
        base = pl.multiple_of(g * ROW_DMA_GROUP, ROW_DMA_GROUP)
        for j in range(ROW_DMA_GROUP):
            _row_copy(ys, dest_ref[tok0 + base + j], y0, base + j, sem).start(priority=0)
            _row_copy(ys, dest_ref[n_tok + tok0 + base + j], y1, base + j, sem).start(priority=1)
        return carry

    def drain(g, carry):
        for j in range(ROW_DMA_GROUP):
            _row_copy(ys, 0, y0, 0, sem).wait()
            _row_copy(ys, 0, y1, 0, sem).wait()
        return carry

    lax.fori_loop(0, COMBINE_TM // ROW_DMA_GROUP, issue, 0)
    lax.fori_loop(0, COMBINE_TM // ROW_DMA_GROUP, drain, 0)
    gw = gw_ref[...]
    moe = gw[:, 0:1] * _unpack_halves(y0[...]) + gw[:, 1:2] * _unpack_halves(y1[...])
    o_ref[...] = x1_ref[...] + g2_ref[0] * moe


def _combine_call(dest_flat, x1, gw, g2, ys, tok_base, n_tok, tiles_per_group):
    m, d = x1.shape
    return pl.pallas_call(
        functools.partial(_combine_kernel, tok_base=tok_base, n_tok=n_tok),
        grid_spec=pltpu.PrefetchScalarGridSpec(
            num_scalar_prefetch=1,
            grid=(m // COMBINE_TM,),
            in_specs=[pl.BlockSpec((COMBINE_TM, d), lambda i, dr: (i, 0)),
                      pl.BlockSpec((COMBINE_TM, 2), lambda i, dr: (i, 0)),
                      pl.BlockSpec((1, 1, d), lambda i, dr: (i // tiles_per_group, 0, 0)),
                      pl.BlockSpec(memory_space=pl.ANY)],
            out_specs=pl.BlockSpec((COMBINE_TM, d), lambda i, dr: (i, 0)),
            scratch_shapes=[pltpu.VMEM((COMBINE_TM,) + ys.shape[1:], ys.dtype),
                            pltpu.VMEM((COMBINE_TM,) + ys.shape[1:], ys.dtype),
                            pltpu.SemaphoreType.DMA]),
        out_shape=jax.ShapeDtypeStruct((m, d), F32),
        compiler_params=_cparams(("arbitrary",)),
        name="combine",
    )(dest_flat, x1, gw, g2, ys)


def kernel(x_prompt, x_sample, cache_a_k, cache_a_v, cache_b_k, cache_b_v, c, c_ctx, norm1_w, norm2_w, w_ada, b_ada, w_in, qn_a, kn_a, qn_b, kn_b, rpb_a, sink_b, w_pa, w_pb, w_out, w_rg, b_rg, w_re, b_re, w1, w3, w2):
    batch, seq, d = x_prompt.shape
    dec_batch, dec_seq, _ = x_sample.shape
    depth = norm1_w.shape[0]
    assert depth == 1, "one trunk layer"
    past = cache_a_k.shape[2]
    n_ctx, n_lat = batch * seq, dec_batch * dec_seq
    n_tok = n_ctx + n_lat

    xc = x_prompt.reshape(n_ctx, d)
    xl = x_sample.reshape(n_lat, d)

    cond = jnp.concatenate([c_ctx[None, :], c], axis=0)
    mod = _ada_call(cond, w_ada[0], b_ada[0][None, :])
    sh1, sc1, g1, sh2, sc2, g2 = [mod[:, i * d:(i + 1) * d][:, None, :] for i in range(6)]
    ctx_rows, lat_rows = slice(0, 1), slice(1, 1 + dec_batch)

    nw1, nw2 = norm1_w[0][None, :], norm2_w[0][None, :]
    qna, kna, qnb, knb = qn_a[0][None, :], kn_a[0][None, :], qn_b[0][None, :], kn_b[0][None, :]
    sink = sink_b[0]
    w_in_bf = w_in[0].astype(BF16)
    wpa_bf, wpb_bf, wout_bf = w_pa[0].astype(BF16), w_pb[0].astype(BF16), w_out[0].astype(BF16)
    wr = jnp.zeros((d, LOGIT_PAD), F32).at[:, :N_EXPERTS].set(w_re[0]).at[:, N_EXPERTS:N_EXPERTS + N_GROUPS].set(w_rg[0])
    br = jnp.zeros((LOGIT_PAD, 1), F32).at[:N_EXPERTS, 0].set(b_re[0]).at[N_EXPERTS:N_EXPERTS + N_GROUPS, 0].set(b_rg[0])

    proj_c, gates_c = _inproj_call(xc, nw1, sc1[ctx_rows], sh1[ctx_rows], w_in_bf, n_ctx // INPROJ_TM)
    proj_l, gates_l = _inproj_call(xl, nw1, sc1[lat_rows], sh1[lat_rows], w_in_bf, dec_seq // INPROJ_TM)

    oa_c, ob_c, new_a_k, new_a_v, new_b_k, new_b_v = _ctx_attn_call(proj_c, seq, qna, kna, qnb, knb, sink)

    bias_blocks, pair0 = _na_bias_blocks(rpb_a[0], dec_seq)
    cos, sin_a, sin_b = _rope_tables(dec_seq)
    rows_of = lambda cache: cache.reshape(-1, HEAD_DIM)
    oa_l = _lat_attn_a_call(proj_l, dec_seq, past, rows_of(cache_a_k), rows_of(cache_a_v),
                            bias_blocks, pair0, qna, kna)
    ob_l = _lat_attn_b_call(proj_l, dec_seq, past, rows_of(cache_b_k), rows_of(cache_b_v),
                            cos, sin_a, sin_b, qnb, knb, sink)

    mix_c = _mix_call(oa_c, ob_c, gates_c, wpa_bf, wpb_bf)
    mix_l = _mix_call(oa_l, ob_l, gates_l, wpa_bf, wpb_bf)
    x1_c, h2_c, lg_c = _outproj_call(mix_c, wout_bf, xc, g1[ctx_rows], sc2[ctx_rows], sh2[ctx_rows],
                                     nw2, wr, n_ctx // MERGE_TM)
    x1_l, h2_l, lg_l = _outproj_call(mix_l, wout_bf, xl, g1[lat_rows], sc2[lat_rows], sh2[lat_rows],
                                     nw2, wr, dec_seq // MERGE_TM)

    eid, gw, rank, cnt = _route_call(jnp.concatenate([lg_c, lg_l], axis=0), br)
    padded_rows, n_items = _moe_layout(2 * n_tok)
    dest, tail, item_e, item_row0, item_nblk = _routing_tables(eid[:2], rank[:2], cnt[:, 0], n_items)
    dest_flat = dest.reshape(-1)
    xs = _dispatch_call(dest_flat, tail, h2_c, h2_l, padded_rows)
    ys = _moe_call(item_e, item_row0, item_nblk, xs, w1[0], w3[0], w2[0])
    gw_t = gw[:2].T
    y_c = _combine_call(dest_flat, x1_c, gw_t[:n_ctx], g2[ctx_rows], ys, 0, n_tok, n_ctx // COMBINE_TM)
    y_l = _combine_call(dest_flat, x1_l, gw_t[n_ctx:], g2[lat_rows], ys, n_ctx, n_tok, dec_seq // COMBINE_TM)

    state = lambda a, heads: a.reshape(batch, 1, seq, heads, HEAD_DIM)
    return (y_c.reshape(batch, seq, d), y_l.reshape(dec_batch, dec_seq, d),
            state(new_a_k, NA_HEADS), state(new_a_v, NA_HEADS),
            state(new_b_k, NB_KV_HEADS), state(new_b_v, NB_KV_HEADS))
```

```python
import functools

import jax
import jax.numpy as jnp
import numpy as np
from jax import lax
from jax.experimental import pallas as pl
from jax.experimental.pallas import tpu as pltpu

D_MODEL = 2048
HEAD_DIM = 128
NA_HEADS = 8
NA_WIDTH = NA_HEADS * HEAD_DIM
NB_Q_HEADS = 8
NB_KV_HEADS = 2
NB_GROUP = NB_Q_HEADS // NB_KV_HEADS
NB_WIDTH = NB_Q_HEADS * HEAD_DIM
NB_KV_WIDTH = NB_KV_HEADS * HEAD_DIM
GRID_W = 64
NA_WIN_ROWS = 8
NA_WIN_COLS = 16
WINDOW = 128
N_GROUPS = 4
EXPERTS_PER_GROUP = 8
N_EXPERTS = N_GROUPS * EXPERTS_PER_GROUP
D_EXPERT = 1024
IN_WIDTH = 3 * NA_WIDTH + NB_WIDTH + 2 * NB_KV_WIDTH + 2 * D_MODEL
ROPE_BASE = 10000.0
NORM_EPS = 1e-6
NEG_INF = -1e30
ATTN_SCALE = HEAD_DIM ** -0.5

QA_HEAD0 = 0
KA_HEAD0 = NA_HEADS
VA_HEAD0 = 2 * NA_HEADS
QB_HEAD0 = 3 * NA_HEADS
KB_HEAD0 = QB_HEAD0 + NB_Q_HEADS
VB_HEAD0 = KB_HEAD0 + NB_KV_HEADS
GATE_COL0 = (VB_HEAD0 + NB_KV_HEADS) * HEAD_DIM

LOGIT_PAD = 128
MOE_ROW_BLOCK = 256
MOE_ROW_GROUP = 1024
MOE_F_CHUNK = 512
VMEM_LIMIT = 56 * 1024 * 1024

F32 = jnp.float32
BF16 = jnp.bfloat16


def _cparams(sem):
    return pltpu.CompilerParams(dimension_semantics=sem, vmem_limit_bytes=VMEM_LIMIT)


def _rms(x, w):
    x = x.astype(F32)
    return x * lax.rsqrt(jnp.mean(x * x, axis=-1, keepdims=True) + NORM_EPS) * w


def _dot(a, b):
    return jnp.dot(a, b, preferred_element_type=F32)


def _dot_nt(a, b):
    return lax.dot_general(a, b, (((1,), (1,)), ((), ())), preferred_element_type=F32)


def _pack_halves(x):
    n = x.shape[1] // 2
    lo = lax.bitcast_convert_type(x[:, :n].astype(BF16).astype(F32), jnp.uint32)
    hi = lax.bitcast_convert_type(x[:, n:].astype(BF16).astype(F32), jnp.uint32)
    return hi | (lo >> 16)


def _unpack_halves(w):
    lo = lax.bitcast_convert_type(w << 16, F32)
    hi = lax.bitcast_convert_type(w & jnp.uint32(0xFFFF0000), F32)
    return jnp.concatenate([lo, hi], axis=1)


def _dot_split(a, b):
    a_hi = a.astype(BF16)
    a_lo = (a - a_hi.astype(F32)).astype(BF16)
    b_hi = b.astype(BF16)
    b_lo = (b - b_hi.astype(F32)).astype(BF16)
    return _dot(a_hi, b_hi) + (_dot(a_lo, b_hi) + _dot(a_hi, b_lo))


ADA_ROWS = 8
ADA_TN = 1024


def _ada_kernel(c_ref, w_ref, b_ref, o_ref):
    n_rows, d, lanes = c_ref.shape
    tn = w_ref.shape[1]

    def body(kb, acc):
        ks = pl.ds(pl.multiple_of(kb * 8, 8), 8)
        w = w_ref[ks, :]
        out = []
        for r in range(n_rows):
            c = c_ref[r, ks, :]
            s = c * jax.nn.sigmoid(c)
            out.append(acc[r] + w * jnp.concatenate([s] * (tn // lanes), axis=1))
        return tuple(out)

    acc = lax.fori_loop(0, d // 8, body, tuple(jnp.zeros((8, tn), F32) for _ in range(n_rows)), unroll=8)
    ri = lax.broadcasted_iota(jnp.int32, (ADA_ROWS, tn), 0)
    res = jnp.zeros((ADA_ROWS, tn), F32)
    for r in range(n_rows):
        row = jnp.sum(acc[r], axis=0, keepdims=True) + b_ref[...]
        res = jnp.where(ri == r, row, res)
    o_ref[...] = res


def _ada_call(cond, w_ada, b_ada):
    n_rows, d = cond.shape
    n = w_ada.shape[1]
    lanes = 128
    cond_lanes = jnp.broadcast_to(cond[:, :, None], (n_rows, d, lanes))
    return pl.pallas_call(
        _ada_kernel,
        grid=(n // ADA_TN,),
        in_specs=[pl.BlockSpec((n_rows, d, lanes), lambda j: (0, 0, 0)),
                  pl.BlockSpec((d, ADA_TN), lambda j: (0, j)),
                  pl.BlockSpec((1, ADA_TN), lambda j: (0, j))],
        out_specs=pl.BlockSpec((ADA_ROWS, ADA_TN), lambda j: (0, j)),
        out_shape=jax.ShapeDtypeStruct((ADA_ROWS, n), F32),
        compiler_params=_cparams(("arbitrary",)),
        name="ada",
    )(cond_lanes, w_ada, b_ada)


INPROJ_TM = 1024
INPROJ_TN = 512
NORM_ROWS = 128


def _inproj_kernel(x_ref, nw_ref, sc_ref, sh_ref, w_ref, qkv_ref, gate_ref, h_scr, *, qkv_tiles):
    j = pl.program_id(1)

    @pl.when(j == 0)
    def _():
        nw = nw_ref[...]
        sc = 1.0 + sc_ref[0]
        sh = sh_ref[0]

        def body(r, carry):
            rows = pl.ds(pl.multiple_of(r * NORM_ROWS, NORM_ROWS), NORM_ROWS)
            h_scr[rows, :] = (_rms(x_ref[rows, :], nw) * sc + sh).astype(BF16)
            return carry

        lax.fori_loop(0, INPROJ_TM // NORM_ROWS, body, 0)

    res = _dot(h_scr[...], w_ref[...])

    @pl.when(j < qkv_tiles)
    def _():
        qkv_ref[...] = res.astype(qkv_ref.dtype)

    @pl.when(j >= qkv_tiles)
    def _():
        gate_ref[...] = res.astype(gate_ref.dtype)


def _inproj_call(x, nw, sc, sh, w_bf, tiles_per_group):
    m, d = x.shape
    n = w_bf.shape[1]
    qkv_tiles = GATE_COL0 // INPROJ_TN
    grp = lambda i, j: (i // tiles_per_group, 0, 0)
    return pl.pallas_call(
        functools.partial(_inproj_kernel, qkv_tiles=qkv_tiles),
        grid=(m // INPROJ_TM, n // INPROJ_TN),
        in_specs=[pl.BlockSpec((INPROJ_TM, d), lambda i, j: (i, 0)),
                  pl.BlockSpec((1, d), lambda i, j: (0, 0)),
                  pl.BlockSpec((1, 1, d), grp),
                  pl.BlockSpec((1, 1, d), grp),
                  pl.BlockSpec((d, INPROJ_TN), lambda i, j: (0, j))],
        out_specs=[pl.BlockSpec((INPROJ_TM, INPROJ_TN), lambda i, j: (i, jnp.minimum(j, qkv_tiles - 1))),
                   pl.BlockSpec((INPROJ_TM, INPROJ_TN), lambda i, j: (i, jnp.maximum(j - qkv_tiles, 0)))],
        out_shape=[jax.ShapeDtypeStruct((m, GATE_COL0), BF16),
                   jax.ShapeDtypeStruct((m, n - GATE_COL0), BF16)],
        scratch_shapes=[pltpu.VMEM((INPROJ_TM, d), BF16)],
        compiler_params=_cparams(("arbitrary", "arbitrary")),
        name="inproj",
    )(x, nw, sc, sh, w_bf)


def _with_ones(v):
    return jnp.concatenate([v.astype(BF16), jnp.ones(v.shape, BF16)], axis=1)


def _softmax_pv(scores, values_with_ones, sink=None):
    m = None
    for s in scores:
        ms = jnp.max(s, axis=-1, keepdims=True)
        m = ms if m is None else jnp.maximum(m, ms)
    if sink is not None:
        m = jnp.maximum(m, sink)
    acc = None
    for s, v1 in zip(scores, values_with_ones):
        pv = _dot(jnp.exp(s - m).astype(BF16), v1)
        acc = pv if acc is None else acc + pv
    d = acc.shape[1] // 2
    den = acc[:, d:]
    if sink is not None:
        den = den + jnp.exp(sink - m)
    return acc[:, :d] / den


def _ctx_attn_kernel(qa_ref, ka_ref, va_ref, qb_ref, kb_ref, vb_ref,
                     qna_ref, kna_ref, qnb_ref, knb_ref, sink_ref,
                     oa_ref, ob_ref, nak_ref, nav_ref, nbk_ref, nbv_ref):
    qna, kna, qnb, knb = qna_ref[...], kna_ref[...], qnb_ref[...], knb_ref[...]
    seq = qa_ref.shape[0]
    for h in range(NA_HEADS):
        cols = slice(h * HEAD_DIM, (h + 1) * HEAD_DIM)
        q = (_rms(qa_ref[:, cols], qna) * ATTN_SCALE).astype(BF16)
        k = _rms(ka_ref[:, cols], kna)
        v = va_ref[:, cols]
        nak_ref[pl.ds(h, seq, stride=NA_HEADS), :] = k
        nav_ref[pl.ds(h, seq, stride=NA_HEADS), :] = v.astype(F32)
        s = _dot_nt(q, k.astype(BF16))
        oa_ref[:, cols] = _softmax_pv([s], [_with_ones(v)]).astype(oa_ref.dtype)
    for kv in range(NB_KV_HEADS):
        kcols = slice(kv * HEAD_DIM, (kv + 1) * HEAD_DIM)
        k = _rms(kb_ref[:, kcols], knb)
        v = vb_ref[:, kcols]
        nbk_ref[pl.ds(kv, seq, stride=NB_KV_HEADS), :] = k
        nbv_ref[pl.ds(kv, seq, stride=NB_KV_HEADS), :] = v.astype(F32)
        kb16 = k.astype(BF16)
        vb1 = _with_ones(v)
        for g in range(NB_GROUP):
            hq = kv * NB_GROUP + g
            cols = slice(hq * HEAD_DIM, (hq + 1) * HEAD_DIM)
            q = (_rms(qb_ref[:, cols], qnb) * ATTN_SCALE).astype(BF16)
            s = _dot_nt(q, kb16)
            ob_ref[:, cols] = _softmax_pv([s], [vb1], sink=sink_ref[hq]).astype(ob_ref.dtype)


def _ctx_attn_call(proj, seq, qna, kna, qnb, knb, sink):
    m = proj.shape[0]
    nb = m // seq
    wide = lambda blk: pl.BlockSpec((seq, NA_WIDTH), lambda b: (b, blk))
    narrow = lambda blk: pl.BlockSpec((seq, NB_KV_WIDTH), lambda b: (b, blk))
    vec = pl.BlockSpec((1, HEAD_DIM), lambda b: (0, 0))
    return pl.pallas_call(
        _ctx_attn_kernel,
        grid=(nb,),
        in_specs=[wide(QA_HEAD0 // NA_HEADS), wide(KA_HEAD0 // NA_HEADS), wide(VA_HEAD0 // NA_HEADS),
                  wide(QB_HEAD0 // NA_HEADS), narrow(KB_HEAD0 // NB_KV_HEADS), narrow(VB_HEAD0 // NB_KV_HEADS),
                  vec, vec, vec, vec,
                  pl.BlockSpec(memory_space=pltpu.SMEM)],
        out_specs=[pl.BlockSpec((seq, NA_WIDTH), lambda b: (b, 0)),
                   pl.BlockSpec((seq, NB_WIDTH), lambda b: (b, 0)),
                   pl.BlockSpec((seq * NA_HEADS, HEAD_DIM), lambda b: (b, 0)),
                   pl.BlockSpec((seq * NA_HEADS, HEAD_DIM), lambda b: (b, 0)),
                   pl.BlockSpec((seq * NB_KV_HEADS, HEAD_DIM), lambda b: (b, 0)),
                   pl.BlockSpec((seq * NB_KV_HEADS, HEAD_DIM), lambda b: (b, 0))],
        out_shape=[jax.ShapeDtypeStruct((m, NA_WIDTH), BF16),
                   jax.ShapeDtypeStruct((m, NB_WIDTH), BF16),
                   jax.ShapeDtypeStruct((m * NA_HEADS, HEAD_DIM), F32),
                   jax.ShapeDtypeStruct((m * NA_HEADS, HEAD_DIM), F32),
                   jax.ShapeDtypeStruct((m * NB_KV_HEADS, HEAD_DIM), F32),
                   jax.ShapeDtypeStruct((m * NB_KV_HEADS, HEAD_DIM), F32)],
        compiler_params=_cparams(("arbitrary",)),
        name="ctx_attn",
    )(proj, proj, proj, proj, proj, proj, qna, kna, qnb, knb, sink)


def _rope(x, cos, sin_a, sin_b):
    quarter = HEAD_DIM // 4
    return (x * cos + pltpu.roll(x, HEAD_DIM - quarter, 1) * sin_a
            + pltpu.roll(x, quarter, 1) * sin_b)


def _head_rows(cache_ref, head, n_heads):
    past = cache_ref.shape[0] // n_heads
    return cache_ref[pl.ds(head, past, stride=n_heads), :]


def _lat_attn_a_kernel(q_ref, k_ref, v_ref, ck_ref, cv_ref, cb_ref, qn_ref, kn_ref, o_ref, bias_scr,
                       *, pair0, index):
    head = pl.program_id(0)

    @pl.when(pl.program_id(1) == 0)
    def _():
        bias_scr[...] = jnp.full(bias_scr.shape, NEG_INF, F32)
        for qr, p0 in enumerate(pair0):
            for j, u in enumerate(index[qr]):
                bias_scr[qr * GRID_W:(qr + 1) * GRID_W, (p0 + j) * 2 * GRID_W:(p0 + j + 1) * 2 * GRID_W] = cb_ref[u]

    q = (_rms(q_ref[...], qn_ref[...]) * ATTN_SCALE).astype(BF16)
    k = _rms(k_ref[...], kn_ref[...]).astype(BF16)
    s_lat = _dot_nt(q, k) + bias_scr[...]
    ck = _head_rows(ck_ref, head, NA_HEADS).astype(BF16)
    cv = _head_rows(cv_ref, head, NA_HEADS)
    s_ctx = _dot_nt(q, ck)
    o = _softmax_pv([s_lat, s_ctx], [_with_ones(v_ref[...]), _with_ones(cv)])
    o_ref[...] = o.astype(o_ref.dtype)


def _lat_attn_b_kernel(q_ref, k_ref, v_ref, ck_ref, cv_ref, cos_ref, sina_ref, sinb_ref,
                       qn_ref, kn_ref, sink_ref, o_ref):
    cos, sin_a, sin_b = cos_ref[...], sina_ref[...], sinb_ref[...]
    q = _rope(_rms(q_ref[...], qn_ref[...]), cos, sin_a, sin_b)
    k = _rope(_rms(k_ref[...], kn_ref[...]), cos, sin_a, sin_b)
    q = (q * ATTN_SCALE).astype(BF16)
    k = k.astype(BF16)
    v1 = _with_ones(v_ref[...])
    length = q.shape[0]
    kv = pl.program_id(0) // NB_GROUP
    ck = _head_rows(ck_ref, kv, NB_KV_HEADS).astype(BF16)
    cv1 = _with_ones(_head_rows(cv_ref, kv, NB_KV_HEADS))
    sink = sink_ref[pl.program_id(0)]
    for qb in range(length // WINDOW):
        rows = slice(qb * WINDOW, (qb + 1) * WINDOW)
        lo, hi = max(0, (qb - 1) * WINDOW), min(length, (qb + 2) * WINDOW)
        qs = q[rows]
        s_win = _dot_nt(qs, k[lo:hi])
        qi = qb * WINDOW + lax.broadcasted_iota(jnp.int32, s_win.shape, 0)
        kj = lo + lax.broadcasted_iota(jnp.int32, s_win.shape, 1)
        s_win = jnp.where(jnp.abs(qi - kj) <= WINDOW, s_win, NEG_INF)
        s_ctx = _dot_nt(qs, ck)
        o_ref[rows, :] = _softmax_pv([s_win, s_ctx], [v1[lo:hi], cv1], sink=sink).astype(o_ref.dtype)


def _lat_attn_a_call(proj, length, past, ck, cv, cb, pair0, index, qn, kn):
    m = proj.shape[0]
    head = lambda h0: pl.BlockSpec((length, HEAD_DIM), lambda h, b: (b, h0 + h))
    cache = pl.BlockSpec((past * NA_HEADS, HEAD_DIM), lambda h, b: (b, 0))
    vec = pl.BlockSpec((1, HEAD_DIM), lambda h, b: (0, 0))
    return pl.pallas_call(
        functools.partial(_lat_attn_a_kernel, pair0=pair0, index=index),
        grid=(NA_HEADS, m // length),
        in_specs=[head(QA_HEAD0), head(KA_HEAD0), head(VA_HEAD0), cache, cache,
                  pl.BlockSpec((None,) + cb.shape[1:], lambda h, b: (h, 0, 0, 0)), vec, vec],
        out_specs=pl.BlockSpec((length, HEAD_DIM), lambda h, b: (b, h)),
        out_shape=jax.ShapeDtypeStruct((m, NA_WIDTH), BF16),
        scratch_shapes=[pltpu.VMEM((length, length), F32)],
        compiler_params=_cparams(("arbitrary", "arbitrary")),
        name="lat_attn_a",
    )(proj, proj, proj, ck, cv, cb, qn, kn)


def _lat_attn_b_call(proj, length, past, ck, cv, cos, sin_a, sin_b, qn, kn, sink):
    m = proj.shape[0]
    qspec = pl.BlockSpec((length, HEAD_DIM), lambda h, b: (b, QB_HEAD0 + h))
    kvspec = lambda h0: pl.BlockSpec((length, HEAD_DIM), lambda h, b: (b, h0 + h // NB_GROUP))
    cache = pl.BlockSpec((past * NB_KV_HEADS, HEAD_DIM), lambda h, b: (b, 0))
    table = pl.BlockSpec((length, HEAD_DIM), lambda h, b: (0, 0))
    vec = pl.BlockSpec((1, HEAD_DIM), lambda h, b: (0, 0))
    return pl.pallas_call(
        _lat_attn_b_kernel,
        grid=(NB_Q_HEADS, m // length),
        in_specs=[qspec, kvspec(KB_HEAD0), kvspec(VB_HEAD0), cache, cache,
                  table, table, table, vec, vec, pl.BlockSpec(memory_space=pltpu.SMEM)],
        out_specs=pl.BlockSpec((length, HEAD_DIM), lambda h, b: (b, h)),
        out_shape=jax.ShapeDtypeStruct((m, NB_WIDTH), BF16),
        compiler_params=_cparams(("arbitrary", "arbitrary")),
        name="lat_attn_b",
    )(proj, proj, proj, ck, cv, cos, sin_a, sin_b, qn, kn, sink)


def _na_bias_blocks(rpb, length):
    rows = length // GRID_W
    kr_n = min(NA_WIN_ROWS, rows)
    n_pairs = min(kr_n // 2 + 1, rows // 2)
    r = np.arange(rows)
    c = np.arange(GRID_W)
    r0 = np.clip(r - kr_n // 2, 0, rows - kr_n)
    c0 = np.clip(c - NA_WIN_COLS // 2, 0, GRID_W - NA_WIN_COLS)
    pair0 = np.minimum(r0 // 2, rows // 2 - n_pairs)
    kr = 2 * (pair0[:, None, None] + np.arange(n_pairs)[None, :, None]) + np.arange(2)[None, None, :]
    row_ok = (kr >= r0[:, None, None]) & (kr < r0[:, None, None] + kr_n)
    col_ok = (c[None, :] >= c0[:, None]) & (c[None, :] < c0[:, None] + NA_WIN_COLS)
    dr = kr - r[:, None, None] + (NA_WIN_ROWS - 1)
    dc = np.clip(c[None, :] - c[:, None], -(NA_WIN_COLS - 1), NA_WIN_COLS - 1) + (NA_WIN_COLS - 1)
    offs = np.where(row_ok, dr, -1).reshape(-1, 2)
    uniq, inverse = np.unique(offs, axis=0, return_inverse=True)
    index = inverse.reshape(rows, n_pairs)
    row_sel = (uniq[:, :, None] == np.arange(2 * NA_WIN_ROWS - 1)[None, None, :]).astype(np.float32)
    col_hit = (dc[None] == np.arange(2 * NA_WIN_COLS - 1)[:, None, None]) & col_ok[None]
    col_sel = np.zeros((2,) + col_hit.shape[:2] + (2 * GRID_W,), np.float32)
    for half in range(2):
        col_sel[half, :, :, half * GRID_W:(half + 1) * GRID_W] = col_hit
    hi = lax.Precision.HIGHEST
    per_col = jnp.einsum("hde,lecn->hldcn", rpb.astype(F32), col_sel, precision=hi)
    table = jnp.einsum("uld,hldcn->hucn", row_sel, per_col, precision=hi)
    valid = ((uniq >= 0)[:, None, :, None] & col_ok[None, :, None, :]).reshape(len(uniq), GRID_W, 2 * GRID_W)
    return (jnp.where(valid[None], table, NEG_INF), tuple(int(p) for p in pair0),
            tuple(tuple(int(u) for u in row) for row in index))


def _rope_tables(length):
    t = jnp.arange(length)
    row = (t // GRID_W).astype(F32)
    col = (t % GRID_W).astype(F32)
    n_freq = HEAD_DIM // 4
    inv = ROPE_BASE ** (-jnp.arange(n_freq, dtype=F32) / n_freq)
    ar = row[:, None] * inv
    ac = col[:, None] * inv
    ang = jnp.concatenate([ar, ar, ac, ac], axis=-1)
    cos, sin = jnp.cos(ang), jnp.sin(ang)
    lane = jnp.arange(HEAD_DIM)
    takes_left = ((lane // n_freq) % 2 == 0)[None, :]
    return cos, jnp.where(takes_left, -sin, 0.0), jnp.where(takes_left, 0.0, sin)


MERGE_TM = 512


def _resident(shape):
    zeros = (0,) * len(shape)
    return pl.BlockSpec(shape, lambda i: zeros, pipeline_mode=pl.Buffered(1))


def _mix_kernel(oa_ref, ob_ref, ga_ref, gb_ref, wpa_ref, wpb_ref, mix_ref):
    ya = _dot(oa_ref[...], wpa_ref[...])
    yb = _dot(ob_ref[...], wpb_ref[...])
    mix = (jax.nn.sigmoid(ga_ref[...].astype(F32)) * ya
           + jax.nn.sigmoid(gb_ref[...].astype(F32)) * yb)
    mix_ref[...] = mix.astype(mix_ref.dtype)


def _mix_call(oa, ob, gates, wpa_bf, wpb_bf):
    m = oa.shape[0]
    d = wpa_bf.shape[1]
    row = lambda i: (i, 0)
    return pl.pallas_call(
        _mix_kernel,
        grid=(m // MERGE_TM,),
        in_specs=[pl.BlockSpec((MERGE_TM, NA_WIDTH), row),
                  pl.BlockSpec((MERGE_TM, NB_WIDTH), row),
                  pl.BlockSpec((MERGE_TM, d), lambda i: (i, 0)),
                  pl.BlockSpec((MERGE_TM, d), lambda i: (i, 1)),
                  _resident((NA_WIDTH, d)), _resident((NB_WIDTH, d))],
        out_specs=pl.BlockSpec((MERGE_TM, d), row),
        out_shape=jax.ShapeDtypeStruct((m, d), BF16),
        compiler_params=_cparams(("arbitrary",)),
        name="mix",
    )(oa, ob, gates, gates, wpa_bf, wpb_bf)


def _outproj_kernel(mix_ref, wout_ref, x_ref, g1_ref, sc2_ref, sh2_ref, n2w_ref, wr_ref,
                    x1_ref, h2_ref, lg_ref, acc_ref):
    i = pl.program_id(0)
    last = pl.num_programs(0) - 1

    def finish(prod):
        x1 = x_ref[...] + g1_ref[0] * prod
        x1_ref[...] = x1
        h2 = _rms(x1, n2w_ref[...]) * (1.0 + sc2_ref[0]) + sh2_ref[0]
        h2_ref[...] = _pack_halves(h2)
        lg_ref[...] = _dot_split(h2, wr_ref[...])

    @pl.when(i == 0)
    def _():
        acc_ref[...] = _dot(mix_ref[...], wout_ref[...])

    @pl.when((i > 0) & (i < last))
    def _():
        prod = acc_ref[...]
        acc_ref[...] = _dot(mix_ref[...], wout_ref[...])
        finish(prod)

    @pl.when(i == last)
    def _():
        finish(acc_ref[...])


def _outproj_call(mix, wout_bf, x, g1, sc2, sh2, n2w, wr, tiles_per_group):
    m, d = x.shape
    n_tiles = m // MERGE_TM
    ahead = lambda i: (jnp.minimum(i, n_tiles - 1), 0)
    done = lambda i: (jnp.maximum(i - 1, 0), 0)
    grp = lambda i: (jnp.maximum(i - 1, 0) // tiles_per_group, 0, 0)
    return pl.pallas_call(
        _outproj_kernel,
        grid=(n_tiles + 1,),
        in_specs=[pl.BlockSpec((MERGE_TM, d), ahead),
                  _resident((d, d)),
                  pl.BlockSpec((MERGE_TM, d), done),
                  pl.BlockSpec((1, 1, d), grp), pl.BlockSpec((1, 1, d), grp), pl.BlockSpec((1, 1, d), grp),
                  pl.BlockSpec((1, d), lambda i: (0, 0)),
                  _resident((d, LOGIT_PAD))],
        out_specs=[pl.BlockSpec((MERGE_TM, d), done),
                   pl.BlockSpec((MERGE_TM, d // 2), done),
                   pl.BlockSpec((MERGE_TM, LOGIT_PAD), done)],
        out_shape=[jax.ShapeDtypeStruct((m, d), F32),
                   jax.ShapeDtypeStruct((m, d // 2), jnp.uint32),
                   jax.ShapeDtypeStruct((m, LOGIT_PAD), F32)],
        scratch_shapes=[pltpu.VMEM((MERGE_TM, d), F32)],
        compiler_params=_cparams(("arbitrary",)),
        name="outproj",
    )(mix, wout_bf, x, g1, sc2, sh2, n2w, wr)


ROUTE_TM = 512


def _first_index_of_max(vals, idx, n):
    mx = jnp.max(vals, axis=0, keepdims=True)
    first = jnp.min(jnp.where(vals == mx, idx, n), axis=0, keepdims=True)
    return mx, first


def _route_kernel(lg_ref, bias_ref, eid_ref, gw_ref, rank_ref, cnt_ref, base_ref):
    step = pl.program_id(0)

    @pl.when(step == 0)
    def _():
        base_ref[...] = jnp.zeros_like(base_ref)

    lt = lg_ref[...].T + bias_ref[...]
    n_tok = lt.shape[1]
    le = lt[0:N_EXPERTS]
    lgrp = lt[N_EXPERTS:N_EXPERTS + N_GROUPS]
    gi = lax.broadcasted_iota(jnp.int32, (N_GROUPS, n_tok), 0)
    gmax, gsel = _first_index_of_max(lgrp, gi, N_GROUPS)
    pg_sel = 1.0 / jnp.sum(jnp.exp(lgrp - gmax), axis=0, keepdims=True)
    le_sel = jnp.zeros((EXPERTS_PER_GROUP, n_tok), F32)
    for g in range(N_GROUPS):
        le_sel = jnp.where(gsel == g, le[g * EXPERTS_PER_GROUP:(g + 1) * EXPERTS_PER_GROUP], le_sel)
    ei = lax.broadcasted_iota(jnp.int32, (EXPERTS_PER_GROUP, n_tok), 0)
    v0, i0 = _first_index_of_max(le_sel, ei, EXPERTS_PER_GROUP)
    rest = jnp.where(ei == i0, -jnp.inf, le_sel)
    v1, i1 = _first_index_of_max(rest, ei, EXPERTS_PER_GROUP)
    e1 = jnp.exp(v1 - v0)
    w0 = pg_sel / (1.0 + e1)
    w1 = pg_sel * e1 / (1.0 + e1)
    eid0 = gsel * EXPERTS_PER_GROUP + i0
    eid1 = gsel * EXPERTS_PER_GROUP + i1

    xi = lax.broadcasted_iota(jnp.int32, (N_EXPERTS, n_tok), 0)
    si = lax.broadcasted_iota(jnp.int32, (n_tok, n_tok), 0)
    ti = lax.broadcasted_iota(jnp.int32, (n_tok, n_tok), 1)
    before = (si < ti).astype(BF16)
    base = base_ref[...]
    hot0 = (xi == eid0).astype(F32)
    hot1 = (xi == eid1).astype(F32)
    pre0 = _dot(hot0.astype(BF16), before)
    pre1 = _dot(hot1.astype(BF16), before)
    tot0 = jnp.sum(hot0, axis=1, keepdims=True)
    tot1 = jnp.sum(hot1, axis=1, keepdims=True)
    rank0 = jnp.sum(hot0 * (base + pre0), axis=0, keepdims=True)
    rank1 = jnp.sum(hot1 * (base + tot0 + pre1), axis=0, keepdims=True)
    base = base + tot0 + tot1
    base_ref[...] = base

    ri = lax.broadcasted_iota(jnp.int32, (8, n_tok), 0)
    pick = lambda a, b: jnp.where(ri == 0, a, jnp.where(ri == 1, b, jnp.zeros_like(a)))
    eid_ref[...] = pick(eid0, eid1)
    gw_ref[...] = pick(w0, w1)
    rank_ref[...] = pick(rank0, rank1).astype(jnp.int32)
    cnt_ref[...] = jnp.broadcast_to(base, cnt_ref.shape).astype(jnp.int32)


def _route_call(logits, bias_col):
    t = logits.shape[0]
    tok = pl.BlockSpec((8, ROUTE_TM), lambda i: (0, i))
    return pl.pallas_call(
        _route_kernel,
        grid=(t // ROUTE_TM,),
        in_specs=[pl.BlockSpec((ROUTE_TM, LOGIT_PAD), lambda i: (i, 0)),
                  pl.BlockSpec((LOGIT_PAD, 1), lambda i: (0, 0))],
        out_specs=[tok, tok, tok, pl.BlockSpec((N_EXPERTS, 128), lambda i: (0, 0))],
        out_shape=[jax.ShapeDtypeStruct((8, t), jnp.int32),
                   jax.ShapeDtypeStruct((8, t), F32),
                   jax.ShapeDtypeStruct((8, t), jnp.int32),
                   jax.ShapeDtypeStruct((N_EXPERTS, 128), jnp.int32)],
        scratch_shapes=[pltpu.VMEM((N_EXPERTS, 1), F32)],
        compiler_params=_cparams(("arbitrary",)),
        name="route",
    )(logits, bias_col)


def _moe_layout(n_pairs):
    padded_rows = -(-(n_pairs + N_EXPERTS * (MOE_ROW_BLOCK - 1)) // MOE_ROW_BLOCK) * MOE_ROW_BLOCK
    n_items = -(-padded_rows // MOE_ROW_GROUP) + N_EXPERTS
    return padded_rows, n_items


def _routing_tables(eid, rank, counts, n_items):
    padded = (counts + MOE_ROW_BLOCK - 1) // MOE_ROW_BLOCK * MOE_ROW_BLOCK
    pad_end = jnp.cumsum(padded)
    pad_start = pad_end - padded
    hot = eid[..., None] == jnp.arange(N_EXPERTS, dtype=jnp.int32)
    dest = (jnp.sum(jnp.where(hot, pad_start, 0), axis=-1) + rank).astype(jnp.int32)
    tail = jnp.where(padded > counts, pad_end - MOE_ROW_BLOCK, -1).astype(jnp.int32)
    per_expert = (padded + MOE_ROW_GROUP - 1) // MOE_ROW_GROUP
    item_end = jnp.cumsum(per_expert)
    item_start = item_end - per_expert
    total = item_end[-1]
    ii = jnp.arange(n_items, dtype=jnp.int32)
    e_of = jnp.minimum(jnp.searchsorted(item_end, ii, side="right"), N_EXPERTS - 1).astype(jnp.int32)
    valid = ii < total
    e_last = e_of[jnp.maximum(total - 1, 0)]
    local = ii - item_start[e_of]
    row0 = pad_start[e_of] + local * MOE_ROW_GROUP
    nblk = jnp.clip((padded[e_of] - local * MOE_ROW_GROUP) // MOE_ROW_BLOCK, 0, MOE_ROW_GROUP // MOE_ROW_BLOCK)
    item_e = jnp.where(valid, e_of, e_last).astype(jnp.int32)
    item_row0 = jnp.where(valid, row0, 0).astype(jnp.int32)
    item_nblk = jnp.where(valid, nblk, 0).astype(jnp.int32)
    return dest, tail, item_e, item_row0, item_nblk


DISPATCH_TOKENS = 256
ROW_DMA_GROUP = 8


def _row_copy(src, s, dst, d, sem):
    return pltpu.make_async_copy(src.at[pl.ds(s, 1), :], dst.at[pl.ds(d, 1), :], sem)


def _dispatch_kernel(dest_ref, tail_ref, h_ctx, h_lat, xs, zero_buf, sem, *, n_ctx, n_tok):
    step = pl.program_id(0)
    tail_copy = lambda e: pltpu.make_async_copy(
        zero_buf, xs.at[pl.ds(pl.multiple_of(tail_ref[e], MOE_ROW_BLOCK), MOE_ROW_BLOCK), :], sem.at[1])

    @pl.when(step == 0)
    def _():
        zero_buf[...] = jnp.zeros_like(zero_buf)
        for e in range(N_EXPERTS):
            @pl.when(tail_ref[e] >= 0)
            def _():
                tail_copy(e).start()
        for e in range(N_EXPERTS):
            @pl.when(tail_ref[e] >= 0)
            def _():
                tail_copy(e).wait()

    tok0 = step * DISPATCH_TOKENS

    def scatter(src):
        def issue(g, carry):
            base = pl.multiple_of(g * ROW_DMA_GROUP, ROW_DMA_GROUP)
            for j in range(ROW_DMA_GROUP):
                _row_copy(src, base + j, xs, dest_ref[tok0 + base + j], sem.at[0]).start(priority=0)
                _row_copy(src, base + j, xs, dest_ref[n_tok + tok0 + base + j], sem.at[0]).start(priority=1)
            return carry

        def drain(g, carry):
            for j in range(2 * ROW_DMA_GROUP):
                _row_copy(src, 0, xs, 0, sem.at[0]).wait()
            return carry

        lax.fori_loop(0, DISPATCH_TOKENS // ROW_DMA_GROUP, issue, 0)
        lax.fori_loop(0, DISPATCH_TOKENS // ROW_DMA_GROUP, drain, 0)

    @pl.when(tok0 < n_ctx)
    def _():
        scatter(h_ctx)

    @pl.when(tok0 >= n_ctx)
    def _():
        scatter(h_lat)


def _dispatch_call(dest_flat, tail, h_ctx, h_lat, padded_rows):
    n_ctx, d = h_ctx.shape
    n_tok = n_ctx + h_lat.shape[0]
    ctx_tiles = n_ctx // DISPATCH_TOKENS
    return pl.pallas_call(
        functools.partial(_dispatch_kernel, n_ctx=n_ctx, n_tok=n_tok),
        grid_spec=pltpu.PrefetchScalarGridSpec(
            num_scalar_prefetch=2,
            grid=(n_tok // DISPATCH_TOKENS,),
            in_specs=[pl.BlockSpec((DISPATCH_TOKENS, d), lambda i, dr, tr: (jnp.minimum(i, ctx_tiles - 1), 0)),
                      pl.BlockSpec((DISPATCH_TOKENS, d), lambda i, dr, tr: (jnp.maximum(i - ctx_tiles, 0), 0))],
            out_specs=pl.BlockSpec(memory_space=pl.ANY),
            scratch_shapes=[pltpu.VMEM((MOE_ROW_BLOCK, d), h_ctx.dtype), pltpu.SemaphoreType.DMA((2,))]),
        out_shape=jax.ShapeDtypeStruct((padded_rows, d), h_ctx.dtype),
        compiler_params=_cparams(("arbitrary",)),
        name="dispatch",
    )(dest_flat, tail, h_ctx, h_lat)


def _moe_kernel(item_e, item_row0, item_nblk, xs, w1_ref, w3_ref, w2_ref, ys,
                x_in, x_bf, acc, y_out, w1_bf, w3_bf, w2_bf, sem):
    i = pl.program_id(0)
    c = pl.program_id(1)
    n_items = pl.num_programs(0)
    last_c = pl.num_programs(1) - 1
    nblk = item_nblk[i]
    max_blk = MOE_ROW_GROUP // MOE_ROW_BLOCK
    blk = lambda b: pl.ds(b * MOE_ROW_BLOCK, MOE_ROW_BLOCK)

    def rows_of(item, b):
        return pl.ds(pl.multiple_of(item_row0[item], MOE_ROW_BLOCK) + b * MOE_ROW_BLOCK, MOE_ROW_BLOCK)

    load = lambda item, b: pltpu.make_async_copy(xs.at[rows_of(item, b), :], x_in.at[blk(b), :], sem.at[0])
    store = lambda item, b: pltpu.make_async_copy(y_out.at[blk(b), :], ys.at[rows_of(item, b), :], sem.at[1])

    def for_blocks(item, fn):
        n = item_nblk[item]
        for b in range(max_blk):
            @pl.when(b < n)
            def _():
                fn(item, b)

    @pl.when(c == 0)
    def _():
        @pl.when(i == 0)
        def _():
            for_blocks(0, lambda it, b: load(it, b).start())

        for_blocks(i, lambda it, b: load(it, b).wait())

        def cast(it, b):
            x_bf[blk(b), :] = _unpack_halves(x_in[blk(b), :]).astype(BF16)

        for_blocks(i, cast)

        @pl.when(i + 1 < n_items)
        def _():
            for_blocks(i + 1, lambda it, b: load(it, b).start())

        def clear(it, b):
            acc[blk(b), :] = jnp.zeros((MOE_ROW_BLOCK, acc.shape[1]), F32)

        for_blocks(i, clear)

    @pl.when(nblk > 0)
    def _():
        def block(rows, w1, w3, w2):
            x = x_bf[rows, :]
            h1 = _dot(x, w1)
            h3 = _dot(x, w3)
            a = (h1 * jax.nn.sigmoid(h1) * h3).astype(BF16)
            acc[rows, :] += _dot(a, w2)

        w1 = w1_ref[0].astype(BF16)
        w3 = w3_ref[0].astype(BF16)
        w2 = w2_ref[0].astype(BF16)
        w1_bf[...] = w1
        w3_bf[...] = w3
        w2_bf[...] = w2
        block(blk(0), w1, w3, w2)

        def body(b, carry):
            rows = pl.ds(pl.multiple_of(b * MOE_ROW_BLOCK, MOE_ROW_BLOCK), MOE_ROW_BLOCK)
            block(rows, w1_bf[...], w3_bf[...], w2_bf[...])
            return carry

        lax.fori_loop(1, nblk, body, 0)

    @pl.when(c == last_c)
    def _():
        @pl.when(i > 0)
        def _():
            for_blocks(i - 1, lambda it, b: store(it, b).wait())

        def pack(it, b):
            y_out[blk(b), :] = _pack_halves(acc[blk(b), :])

        for_blocks(i, pack)
        for_blocks(i, lambda it, b: store(it, b).start())

        @pl.when(i == n_items - 1)
        def _():
            for_blocks(i, lambda it, b: store(it, b).wait())


def _moe_call(item_e, item_row0, item_nblk, xs, w1, w3, w2):
    padded_rows, d_packed = xs.shape
    d = w1.shape[1]
    n_items = item_e.shape[0]
    f = w1.shape[2]
    nc = f // MOE_F_CHUNK
    chunk = lambda i, c, ib: jnp.where(ib[i] > 0, c, nc - 1)
    return pl.pallas_call(
        _moe_kernel,
        grid_spec=pltpu.PrefetchScalarGridSpec(
            num_scalar_prefetch=3,
            grid=(n_items, nc),
            in_specs=[pl.BlockSpec(memory_space=pl.ANY),
                      pl.BlockSpec((1, d, MOE_F_CHUNK), lambda i, c, ie, ir, ib: (ie[i], 0, chunk(i, c, ib))),
                      pl.BlockSpec((1, d, MOE_F_CHUNK), lambda i, c, ie, ir, ib: (ie[i], 0, chunk(i, c, ib))),
                      pl.BlockSpec((1, MOE_F_CHUNK, d), lambda i, c, ie, ir, ib: (ie[i], chunk(i, c, ib), 0))],
            out_specs=pl.BlockSpec(memory_space=pl.ANY),
            scratch_shapes=[pltpu.VMEM((MOE_ROW_GROUP, d_packed), xs.dtype),
                            pltpu.VMEM((MOE_ROW_GROUP, d), BF16),
                            pltpu.VMEM((MOE_ROW_GROUP, d), F32),
                            pltpu.VMEM((MOE_ROW_GROUP, d_packed), xs.dtype),
                            pltpu.VMEM((d, MOE_F_CHUNK), BF16),
                            pltpu.VMEM((d, MOE_F_CHUNK), BF16),
                            pltpu.VMEM((MOE_F_CHUNK, d), BF16),
                            pltpu.SemaphoreType.DMA((2,))]),
        out_shape=jax.ShapeDtypeStruct((padded_rows, d_packed), xs.dtype),
        compiler_params=_cparams(("arbitrary", "arbitrary")),
        name="moe",
    )(item_e, item_row0, item_nblk, xs, w1, w3, w2)


COMBINE_TM = 256


def _combine_kernel(dest_ref, x1_ref, gw_ref, g2_ref, ys, o_ref, y0, y1, sem, *, tok_base, n_tok):
    tok0 = tok_base + pl.program_id(0) * COMBINE_TM

    def issue(g, carry):
        base = pl.multiple_of(g * ROW_DMA_GROUP, ROW_DMA_GROUP)
        for j in range(ROW_DMA_GROUP):
            _row_copy(ys, dest_ref[tok0 + base + j], y0, base + j, sem).start(priority=0)
            _row_copy(ys, dest_ref[n_tok + tok0 + base + j], y1, base + j, sem).start(priority=1)
        return carry

    def drain(g, carry):
        for j in range(ROW_DMA_GROUP):
            _row_copy(ys, 0, y0, 0, sem).wait()
            _row_copy(ys, 0, y1, 0, sem).wait()
        return carry

    lax.fori_loop(0, COMBINE_TM // ROW_DMA_GROUP, issue, 0)
    lax.fori_loop(0, COMBINE_TM // ROW_DMA_GROUP, drain, 0)
    gw = gw_ref[...]
    moe = gw[:, 0:1] * _unpack_halves(y0[...]) + gw[:, 1:2] * _unpack_halves(y1[...])
    o_ref[...] = x1_ref[...] + g2_ref[0] * moe


def _combine_call(dest_flat, x1, gw, g2, ys, tok_base, n_tok, tiles_per_group):
    m, d = x1.shape
    return pl.pallas_call(
        functools.partial(_combine_kernel, tok_base=tok_base, n_tok=n_tok),
        grid_spec=pltpu.PrefetchScalarGridSpec(
            num_scalar_prefetch=1,
            grid=(m // COMBINE_TM,),
            in_specs=[pl.BlockSpec((COMBINE_TM, d), lambda i, dr: (i, 0)),
                      pl.BlockSpec((COMBINE_TM, 2), lambda i, dr: (i, 0)),
                      pl.BlockSpec((1, 1, d), lambda i, dr: (i // tiles_per_group, 0, 0)),
                      pl.BlockSpec(memory_space=pl.ANY)],
            out_specs=pl.BlockSpec((COMBINE_TM, d), lambda i, dr: (i, 0)),
            scratch_shapes=[pltpu.VMEM((COMBINE_TM,) + ys.shape[1:], ys.dtype),
                            pltpu.VMEM((COMBINE_TM,) + ys.shape[1:], ys.dtype),
                            pltpu.SemaphoreType.DMA]),
        out_shape=jax.ShapeDtypeStruct((m, d), F32),
        compiler_params=_cparams(("arbitrary",)),
        name="combine",
    )(dest_flat, x1, gw, g2, ys)


def kernel(x_prompt, x_sample, cache_a_k, cache_a_v, cache_b_k, cache_b_v, c, c_ctx, norm1_w, norm2_w, w_ada, b_ada, w_in, qn_a, kn_a, qn_b, kn_b, rpb_a, sink_b, w_pa, w_pb, w_out, w_rg, b_rg, w_re, b_re, w1, w3, w2):
    batch, seq, d = x_prompt.shape
    dec_batch, dec_seq, _ = x_sample.shape
    depth = norm1_w.shape[0]
    assert depth == 1, "one trunk layer"
    past = cache_a_k.shape[2]
    n_ctx, n_lat = batch * seq, dec_batch * dec_seq
    n_tok = n_ctx + n_lat

    xc = x_prompt.reshape(n_ctx, d)
    xl = x_sample.reshape(n_lat, d)

    cond = jnp.concatenate([c_ctx[None, :], c], axis=0)
    mod = _ada_call(cond, w_ada[0], b_ada[0][None, :])
    sh1, sc1, g1, sh2, sc2, g2 = [mod[:, i * d:(i + 1) * d][:, None, :] for i in range(6)]
    ctx_rows, lat_rows = slice(0, 1), slice(1, 1 + dec_batch)

    nw1, nw2 = norm1_w[0][None, :], norm2_w[0][None, :]
    qna, kna, qnb, knb = qn_a[0][None, :], kn_a[0][None, :], qn_b[0][None, :], kn_b[0][None, :]
    sink = sink_b[0]
    w_in_bf = w_in[0].astype(BF16)
    wpa_bf, wpb_bf, wout_bf = w_pa[0].astype(BF16), w_pb[0].astype(BF16), w_out[0].astype(BF16)
    wr = jnp.zeros((d, LOGIT_PAD), F32).at[:, :N_EXPERTS].set(w_re[0]).at[:, N_EXPERTS:N_EXPERTS + N_GROUPS].set(w_rg[0])
    br = jnp.zeros((LOGIT_PAD, 1), F32).at[:N_EXPERTS, 0].set(b_re[0]).at[N_EXPERTS:N_EXPERTS + N_GROUPS, 0].set(b_rg[0])

    proj_c, gates_c = _inproj_call(xc, nw1, sc1[ctx_rows], sh1[ctx_rows], w_in_bf, n_ctx // INPROJ_TM)
    proj_l, gates_l = _inproj_call(xl, nw1, sc1[lat_rows], sh1[lat_rows], w_in_bf, dec_seq // INPROJ_TM)

    oa_c, ob_c, new_a_k, new_a_v, new_b_k, new_b_v = _ctx_attn_call(proj_c, seq, qna, kna, qnb, knb, sink)

    bias_blocks, pair0, bias_index = _na_bias_blocks(rpb_a[0], dec_seq)
    cos, sin_a, sin_b = _rope_tables(dec_seq)
    rows_of = lambda cache: cache.reshape(-1, HEAD_DIM)
    oa_l = _lat_attn_a_call(proj_l, dec_seq, past, rows_of(cache_a_k), rows_of(cache_a_v),
                            bias_blocks, pair0, bias_index, qna, kna)
    ob_l = _lat_attn_b_call(proj_l, dec_seq, past, rows_of(cache_b_k), rows_of(cache_b_v),
                            cos, sin_a, sin_b, qnb, knb, sink)

    mix_c = _mix_call(oa_c, ob_c, gates_c, wpa_bf, wpb_bf)
    mix_l = _mix_call(oa_l, ob_l, gates_l, wpa_bf, wpb_bf)
    x1_c, h2_c, lg_c = _outproj_call(mix_c, wout_bf, xc, g1[ctx_rows], sc2[ctx_rows], sh2[ctx_rows],
                                     nw2, wr, n_ctx // MERGE_TM)
    x1_l, h2_l, lg_l = _outproj_call(mix_l, wout_bf, xl, g1[lat_rows], sc2[lat_rows], sh2[lat_rows],
                                     nw2, wr, dec_seq // MERGE_TM)

    eid, gw, rank, cnt = _route_call(jnp.concatenate([lg_c, lg_l], axis=0), br)
    padded_rows, n_items = _moe_layout(2 * n_tok)
    dest, tail, item_e, item_row0, item_nblk = _routing_tables(eid[:2], rank[:2], cnt[:, 0], n_items)
    dest_flat = dest.reshape(-1)
    xs = _dispatch_call(dest_flat, tail, h2_c, h2_l, padded_rows)
    ys = _moe_call(item_e, item_row0, item_nblk, xs, w1[0], w3[0], w2[0])
    gw_t = gw[:2].T
    y_c = _combine_call(dest_flat, x1_c, gw_t[:n_ctx], g2[ctx_rows], ys, 0, n_tok, n_ctx // COMBINE_TM)
    y_l = _combine_call(dest_flat, x1_l, gw_t[n_ctx:], g2[lat_rows], ys, n_ctx, n_tok, dec_seq // COMBINE_TM)

    state = lambda a, heads: a.reshape(batch, 1, seq, heads, HEAD_DIM)
    return (y_c.reshape(batch, seq, d), y_l.reshape(dec_batch, dec_seq, d),
            state(new_a_k, NA_HEADS), state(new_a_v, NA_HEADS),
            state(new_b_k, NB_KV_HEADS), state(new_b_v, NB_KV_HEADS))
```

```python
import functools

import jax
import jax.numpy as jnp
import numpy as np
from jax import lax
from jax.experimental import pallas as pl
from jax.experimental.pallas import tpu as pltpu

D_MODEL = 2048
HEAD_DIM = 128
NA_HEADS = 8
NA_WIDTH = NA_HEADS * HEAD_DIM
NB_Q_HEADS = 8
NB_KV_HEADS = 2
NB_GROUP = NB_Q_HEADS // NB_KV_HEADS
NB_WIDTH = NB_Q_HEADS * HEAD_DIM
NB_KV_WIDTH = NB_KV_HEADS * HEAD_DIM
GRID_W = 64
NA_WIN_ROWS = 8
NA_WIN_COLS = 16
WINDOW = 128
N_GROUPS = 4
EXPERTS_PER_GROUP = 8
N_EXPERTS = N_GROUPS * EXPERTS_PER_GROUP
D_EXPERT = 1024
IN_WIDTH = 3 * NA_WIDTH + NB_WIDTH + 2 * NB_KV_WIDTH + 2 * D_MODEL
ROPE_BASE = 10000.0
NORM_EPS = 1e-6
NEG_INF = -1e30
ATTN_SCALE = HEAD_DIM ** -0.5

QA_HEAD0 = 0
KA_HEAD0 = NA_HEADS
VA_HEAD0 = 2 * NA_HEADS
QB_HEAD0 = 3 * NA_HEADS
KB_HEAD0 = QB_HEAD0 + NB_Q_HEADS
VB_HEAD0 = KB_HEAD0 + NB_KV_HEADS
GATE_COL0 = (VB_HEAD0 + NB_KV_HEADS) * HEAD_DIM

LOGIT_PAD = 128
MOE_ROW_BLOCK = 256
MOE_ROW_GROUP = 1024
MOE_F_CHUNK = 512
VMEM_LIMIT = 56 * 1024 * 1024

F32 = jnp.float32
BF16 = jnp.bfloat16


def _cparams(sem, vmem_limit=VMEM_LIMIT):
    return pltpu.CompilerParams(dimension_semantics=sem, vmem_limit_bytes=vmem_limit)


def _rms(x, w):
    x = x.astype(F32)
    return x * lax.rsqrt(jnp.mean(x * x, axis=-1, keepdims=True) + NORM_EPS) * w


def _dot(a, b):
    return jnp.dot(a, b, preferred_element_type=F32)


def _dot_nt(a, b):
    return lax.dot_general(a, b, (((1,), (1,)), ((), ())), preferred_element_type=F32)


def _pack_halves(x):
    n = x.shape[1] // 2
    lo = lax.bitcast_convert_type(x[:, :n].astype(BF16).astype(F32), jnp.uint32)
    hi = lax.bitcast_convert_type(x[:, n:].astype(BF16).astype(F32), jnp.uint32)
    return hi | (lo >> 16)


def _unpack_halves(w):
    lo = lax.bitcast_convert_type(w << 16, F32)
    hi = lax.bitcast_convert_type(w & jnp.uint32(0xFFFF0000), F32)
    return jnp.concatenate([lo, hi], axis=1)


def _dot_split(a, b):
    a_hi = a.astype(BF16)
    a_lo = (a - a_hi.astype(F32)).astype(BF16)
    b_hi = b.astype(BF16)
    b_lo = (b - b_hi.astype(F32)).astype(BF16)
    return _dot(a_hi, b_hi) + (_dot(a_lo, b_hi) + _dot(a_hi, b_lo))


ADA_ROWS = 8
ADA_TN = 1024


def _ada_kernel(c_ref, w_ref, b_ref, o_ref):
    n_rows, d, lanes = c_ref.shape
    tn = w_ref.shape[1]

    def body(kb, acc):
        ks = pl.ds(pl.multiple_of(kb * 8, 8), 8)
        w = w_ref[ks, :]
        out = []
        for r in range(n_rows):
            c = c_ref[r, ks, :]
            s = c * jax.nn.sigmoid(c)
            out.append(acc[r] + w * jnp.concatenate([s] * (tn // lanes), axis=1))
        return tuple(out)

    acc = lax.fori_loop(0, d // 8, body, tuple(jnp.zeros((8, tn), F32) for _ in range(n_rows)), unroll=8)
    ri = lax.broadcasted_iota(jnp.int32, (ADA_ROWS, tn), 0)
    res = jnp.zeros((ADA_ROWS, tn), F32)
    for r in range(n_rows):
        row = jnp.sum(acc[r], axis=0, keepdims=True) + b_ref[...]
        res = jnp.where(ri == r, row, res)
    o_ref[...] = res


def _ada_call(cond, w_ada, b_ada):
    n_rows, d = cond.shape
    n = w_ada.shape[1]
    lanes = 128
    cond_lanes = jnp.broadcast_to(cond[:, :, None], (n_rows, d, lanes))
    return pl.pallas_call(
        _ada_kernel,
        grid=(n // ADA_TN,),
        in_specs=[pl.BlockSpec((n_rows, d, lanes), lambda j: (0, 0, 0)),
                  pl.BlockSpec((d, ADA_TN), lambda j: (0, j)),
                  pl.BlockSpec((1, ADA_TN), lambda j: (0, j))],
        out_specs=pl.BlockSpec((ADA_ROWS, ADA_TN), lambda j: (0, j)),
        out_shape=jax.ShapeDtypeStruct((ADA_ROWS, n), F32),
        compiler_params=_cparams(("arbitrary",)),
        name="ada",
    )(cond_lanes, w_ada, b_ada)


INPROJ_TM = 2048
INPROJ_TN = 512
NORM_ROWS = 128


def _inproj_kernel(x_hbm, nw_ref, sc_ref, sh_ref, w_ref, qkv_ref, gate_ref, x_buf, h_scr, sem, *, qkv_tiles):
    i = pl.program_id(0)
    j = pl.program_id(1)
    tm = x_buf.shape[0]
    fetch = lambda tile: pltpu.make_async_copy(
        x_hbm.at[pl.ds(pl.multiple_of(tile * tm, tm), tm), :], x_buf, sem)

    @pl.when(j == 0)
    def _():
        @pl.when(i == 0)
        def _():
            fetch(0).start()

        fetch(i).wait()
        nw = nw_ref[...]
        sc = 1.0 + sc_ref[0]
        sh = sh_ref[0]

        def body(r, carry):
            rows = pl.ds(pl.multiple_of(r * NORM_ROWS, NORM_ROWS), NORM_ROWS)
            h_scr[rows, :] = (_rms(x_buf[rows, :], nw) * sc + sh).astype(BF16)
            return carry

        lax.fori_loop(0, tm // NORM_ROWS, body, 0)

    @pl.when((j == 1) & (i + 1 < pl.num_programs(0)))
    def _():
        fetch(i + 1).start()

    res = _dot(h_scr[...], w_ref[...].astype(BF16))

    @pl.when(j < qkv_tiles)
    def _():
        qkv_ref[...] = res.astype(qkv_ref.dtype)

    @pl.when(j >= qkv_tiles)
    def _():
        gate_ref[...] = res.astype(gate_ref.dtype)


def _inproj_call(x, nw, sc, sh, w_in, rows_per_group):
    m, d = x.shape
    n = w_in.shape[1]
    assert n // INPROJ_TN >= 2, "the next token tile is requested in the second column step"
    tm = min(INPROJ_TM, rows_per_group)
    tiles_per_group = rows_per_group // tm
    qkv_tiles = GATE_COL0 // INPROJ_TN
    grp = lambda i, j: (i // tiles_per_group, 0, 0)
    return pl.pallas_call(
        functools.partial(_inproj_kernel, qkv_tiles=qkv_tiles),
        grid=(m // tm, n // INPROJ_TN),
        in_specs=[pl.BlockSpec(memory_space=pl.ANY),
                  pl.BlockSpec((1, d), lambda i, j: (0, 0)),
                  pl.BlockSpec((1, 1, d), grp),
                  pl.BlockSpec((1, 1, d), grp),
                  pl.BlockSpec((d, INPROJ_TN), lambda i, j: (0, j))],
        out_specs=[pl.BlockSpec((tm, INPROJ_TN), lambda i, j: (i, jnp.minimum(j, qkv_tiles - 1))),
                   pl.BlockSpec((tm, INPROJ_TN), lambda i, j: (i, jnp.maximum(j - qkv_tiles, 0)))],
        out_shape=[jax.ShapeDtypeStruct((m, GATE_COL0), BF16),
                   jax.ShapeDtypeStruct((m, n - GATE_COL0), BF16)],
        scratch_shapes=[pltpu.VMEM((tm, d), F32), pltpu.VMEM((tm, d), BF16), pltpu.SemaphoreType.DMA],
        compiler_params=_cparams(("arbitrary", "arbitrary")),
        name="inproj",
    )(x, nw, sc, sh, w_in)


def _with_ones(v):
    return jnp.concatenate([v.astype(BF16), jnp.ones(v.shape, BF16)], axis=1)


def _softmax_pv(scores, values_with_ones, sink=None):
    m = None
    for s in scores:
        ms = jnp.max(s, axis=-1, keepdims=True)
        m = ms if m is None else jnp.maximum(m, ms)
    if sink is not None:
        m = jnp.maximum(m, sink)
    acc = None
    for s, v1 in zip(scores, values_with_ones):
        pv = _dot(jnp.exp(s - m).astype(BF16), v1)
        acc = pv if acc is None else acc + pv
    d = acc.shape[1] // 2
    den = acc[:, d:]
    if sink is not None:
        den = den + jnp.exp(sink - m)
    return acc[:, :d] / den


def _ctx_attn_kernel(qa_ref, ka_ref, va_ref, qb_ref, kb_ref, vb_ref,
                     qna_ref, kna_ref, qnb_ref, knb_ref, sink_ref,
                     oa_ref, ob_ref, nak_ref, nav_ref, nbk_ref, nbv_ref):
    qna, kna, qnb, knb = qna_ref[...], kna_ref[...], qnb_ref[...], knb_ref[...]
    seq = qa_ref.shape[0]
    for h in range(NA_HEADS):
        cols = slice(h * HEAD_DIM, (h + 1) * HEAD_DIM)
        q = (_rms(qa_ref[:, cols], qna) * ATTN_SCALE).astype(BF16)
        k = _rms(ka_ref[:, cols], kna)
        v = va_ref[:, cols]
        nak_ref[pl.ds(h, seq, stride=NA_HEADS), :] = k
        nav_ref[pl.ds(h, seq, stride=NA_HEADS), :] = v.astype(F32)
        s = _dot_nt(q, k.astype(BF16))
        oa_ref[:, cols] = _softmax_pv([s], [_with_ones(v)]).astype(oa_ref.dtype)
    for kv in range(NB_KV_HEADS):
        kcols = slice(kv * HEAD_DIM, (kv + 1) * HEAD_DIM)
        k = _rms(kb_ref[:, kcols], knb)
        v = vb_ref[:, kcols]
        nbk_ref[pl.ds(kv, seq, stride=NB_KV_HEADS), :] = k
        nbv_ref[pl.ds(kv, seq, stride=NB_KV_HEADS), :] = v.astype(F32)
        kb16 = k.astype(BF16)
        vb1 = _with_ones(v)
        for g in range(NB_GROUP):
            hq = kv * NB_GROUP + g
            cols = slice(hq * HEAD_DIM, (hq + 1) * HEAD_DIM)
            q = (_rms(qb_ref[:, cols], qnb) * ATTN_SCALE).astype(BF16)
            s = _dot_nt(q, kb16)
            ob_ref[:, cols] = _softmax_pv([s], [vb1], sink=sink_ref[hq]).astype(ob_ref.dtype)


def _ctx_attn_call(proj, seq, qna, kna, qnb, knb, sink):
    m = proj.shape[0]
    nb = m // seq
    wide = lambda blk: pl.BlockSpec((seq, NA_WIDTH), lambda b: (b, blk))
    narrow = lambda blk: pl.BlockSpec((seq, NB_KV_WIDTH), lambda b: (b, blk))
    vec = pl.BlockSpec((1, HEAD_DIM), lambda b: (0, 0))
    return pl.pallas_call(
        _ctx_attn_kernel,
        grid=(nb,),
        in_specs=[wide(QA_HEAD0 // NA_HEADS), wide(KA_HEAD0 // NA_HEADS), wide(VA_HEAD0 // NA_HEADS),
                  wide(QB_HEAD0 // NA_HEADS), narrow(KB_HEAD0 // NB_KV_HEADS), narrow(VB_HEAD0 // NB_KV_HEADS),
                  vec, vec, vec, vec,
                  pl.BlockSpec(memory_space=pltpu.SMEM)],
        out_specs=[pl.BlockSpec((seq, NA_WIDTH), lambda b: (b, 0)),
                   pl.BlockSpec((seq, NB_WIDTH), lambda b: (b, 0)),
                   pl.BlockSpec((seq * NA_HEADS, HEAD_DIM), lambda b: (b, 0)),
                   pl.BlockSpec((seq * NA_HEADS, HEAD_DIM), lambda b: (b, 0)),
                   pl.BlockSpec((seq * NB_KV_HEADS, HEAD_DIM), lambda b: (b, 0)),
                   pl.BlockSpec((seq * NB_KV_HEADS, HEAD_DIM), lambda b: (b, 0))],
        out_shape=[jax.ShapeDtypeStruct((m, NA_WIDTH), BF16),
                   jax.ShapeDtypeStruct((m, NB_WIDTH), BF16),
                   jax.ShapeDtypeStruct((m * NA_HEADS, HEAD_DIM), F32),
                   jax.ShapeDtypeStruct((m * NA_HEADS, HEAD_DIM), F32),
                   jax.ShapeDtypeStruct((m * NB_KV_HEADS, HEAD_DIM), F32),
                   jax.ShapeDtypeStruct((m * NB_KV_HEADS, HEAD_DIM), F32)],
        compiler_params=_cparams(("arbitrary",)),
        name="ctx_attn",
    )(proj, proj, proj, proj, proj, proj, qna, kna, qnb, knb, sink)


def _rope(x, cos, sin_a, sin_b):
    quarter = HEAD_DIM // 4
    return (x * cos + pltpu.roll(x, HEAD_DIM - quarter, 1) * sin_a
            + pltpu.roll(x, quarter, 1) * sin_b)


def _head_rows(cache_ref, head, n_heads):
    past = cache_ref.shape[0] // n_heads
    return cache_ref[pl.ds(head, past, stride=n_heads), :]


def _lat_attn_a_kernel(q_ref, k_ref, v_ref, ck_ref, cv_ref, cb_ref, qn_ref, kn_ref, o_ref, bias_scr,
                       *, pair0, index):
    head = pl.program_id(0)

    @pl.when(pl.program_id(1) == 0)
    def _():
        bias_scr[...] = jnp.full(bias_scr.shape, NEG_INF, F32)
        for qr, p0 in enumerate(pair0):
            for j, u in enumerate(index[qr]):
                bias_scr[qr * GRID_W:(qr + 1) * GRID_W, (p0 + j) * 2 * GRID_W:(p0 + j + 1) * 2 * GRID_W] = cb_ref[u]

    q = (_rms(q_ref[...], qn_ref[...]) * ATTN_SCALE).astype(BF16)
    k = _rms(k_ref[...], kn_ref[...]).astype(BF16)
    s_lat = _dot_nt(q, k) + bias_scr[...]
    ck = _head_rows(ck_ref, head, NA_HEADS).astype(BF16)
    cv = _head_rows(cv_ref, head, NA_HEADS)
    s_ctx = _dot_nt(q, ck)
    o = _softmax_pv([s_lat, s_ctx], [_with_ones(v_ref[...]), _with_ones(cv)])
    o_ref[...] = o.astype(o_ref.dtype)


def _lat_attn_b_kernel(q_ref, k_ref, v_ref, ck_ref, cv_ref, cos_ref, sina_ref, sinb_ref,
                       qn_ref, kn_ref, sink_ref, o_ref):
    cos, sin_a, sin_b = cos_ref[...], sina_ref[...], sinb_ref[...]
    q = _rope(_rms(q_ref[...], qn_ref[...]), cos, sin_a, sin_b)
    k = _rope(_rms(k_ref[...], kn_ref[...]), cos, sin_a, sin_b)
    q = (q * ATTN_SCALE).astype(BF16)
    k = k.astype(BF16)
    v1 = _with_ones(v_ref[...])
    length = q.shape[0]
    kv = pl.program_id(0) // NB_GROUP
    ck = _head_rows(ck_ref, kv, NB_KV_HEADS).astype(BF16)
    cv1 = _with_ones(_head_rows(cv_ref, kv, NB_KV_HEADS))
    sink = sink_ref[pl.program_id(0)]
    for qb in range(length // WINDOW):
        rows = slice(qb * WINDOW, (qb + 1) * WINDOW)
        lo, hi = max(0, (qb - 1) * WINDOW), min(length, (qb + 2) * WINDOW)
        qs = q[rows]
        s_win = _dot_nt(qs, k[lo:hi])
        qi = qb * WINDOW + lax.broadcasted_iota(jnp.int32, s_win.shape, 0)
        kj = lo + lax.broadcasted_iota(jnp.int32, s_win.shape, 1)
        s_win = jnp.where(jnp.abs(qi - kj) <= WINDOW, s_win, NEG_INF)
        s_ctx = _dot_nt(qs, ck)
        o_ref[rows, :] = _softmax_pv([s_win, s_ctx], [v1[lo:hi], cv1], sink=sink).astype(o_ref.dtype)


def _lat_attn_a_call(proj, length, past, ck, cv, cb, pair0, index, qn, kn):
    m = proj.shape[0]
    head = lambda h0: pl.BlockSpec((length, HEAD_DIM), lambda h, b: (b, h0 + h))
    cache = pl.BlockSpec((past * NA_HEADS, HEAD_DIM), lambda h, b: (b, 0))
    vec = pl.BlockSpec((1, HEAD_DIM), lambda h, b: (0, 0))
    return pl.pallas_call(
        functools.partial(_lat_attn_a_kernel, pair0=pair0, index=index),
        grid=(NA_HEADS, m // length),
        in_specs=[head(QA_HEAD0), head(KA_HEAD0), head(VA_HEAD0), cache, cache,
                  pl.BlockSpec((None,) + cb.shape[1:], lambda h, b: (h, 0, 0, 0)), vec, vec],
        out_specs=pl.BlockSpec((length, HEAD_DIM), lambda h, b: (b, h)),
        out_shape=jax.ShapeDtypeStruct((m, NA_WIDTH), BF16),
        scratch_shapes=[pltpu.VMEM((length, length), F32)],
        compiler_params=_cparams(("arbitrary", "arbitrary")),
        name="lat_attn_a",
    )(proj, proj, proj, ck, cv, cb, qn, kn)


def _lat_attn_b_call(proj, length, past, ck, cv, cos, sin_a, sin_b, qn, kn, sink):
    m = proj.shape[0]
    qspec = pl.BlockSpec((length, HEAD_DIM), lambda h, b: (b, QB_HEAD0 + h))
    kvspec = lambda h0: pl.BlockSpec((length, HEAD_DIM), lambda h, b: (b, h0 + h // NB_GROUP))
    cache = pl.BlockSpec((past * NB_KV_HEADS, HEAD_DIM), lambda h, b: (b, 0))
    table = pl.BlockSpec((length, HEAD_DIM), lambda h, b: (0, 0))
    vec = pl.BlockSpec((1, HEAD_DIM), lambda h, b: (0, 0))
    return pl.pallas_call(
        _lat_attn_b_kernel,
        grid=(NB_Q_HEADS, m // length),
        in_specs=[qspec, kvspec(KB_HEAD0), kvspec(VB_HEAD0), cache, cache,
                  table, table, table, vec, vec, pl.BlockSpec(memory_space=pltpu.SMEM)],
        out_specs=pl.BlockSpec((length, HEAD_DIM), lambda h, b: (b, h)),
        out_shape=jax.ShapeDtypeStruct((m, NB_WIDTH), BF16),
        compiler_params=_cparams(("arbitrary", "arbitrary")),
        name="lat_attn_b",
    )(proj, proj, proj, ck, cv, cos, sin_a, sin_b, qn, kn, sink)


def _na_bias_blocks(rpb, length):
    rows = length // GRID_W
    kr_n = min(NA_WIN_ROWS, rows)
    n_pairs = min(kr_n // 2 + 1, rows // 2)
    r = np.arange(rows)
    c = np.arange(GRID_W)
    r0 = np.clip(r - kr_n // 2, 0, rows - kr_n)
    c0 = np.clip(c - NA_WIN_COLS // 2, 0, GRID_W - NA_WIN_COLS)
    pair0 = np.minimum(r0 // 2, rows // 2 - n_pairs)
    kr = 2 * (pair0[:, None, None] + np.arange(n_pairs)[None, :, None]) + np.arange(2)[None, None, :]
    row_ok = (kr >= r0[:, None, None]) & (kr < r0[:, None, None] + kr_n)
    col_ok = (c[None, :] >= c0[:, None]) & (c[None, :] < c0[:, None] + NA_WIN_COLS)
    dr = kr - r[:, None, None] + (NA_WIN_ROWS - 1)
    dc = np.clip(c[None, :] - c[:, None], -(NA_WIN_COLS - 1), NA_WIN_COLS - 1) + (NA_WIN_COLS - 1)
    offs = np.where(row_ok, dr, -1).reshape(-1, 2)
    uniq, inverse = np.unique(offs, axis=0, return_inverse=True)
    index = inverse.reshape(rows, n_pairs)
    row_sel = (uniq[:, :, None] == np.arange(2 * NA_WIN_ROWS - 1)[None, None, :]).astype(np.float32)
    col_hit = (dc[None] == np.arange(2 * NA_WIN_COLS - 1)[:, None, None]) & col_ok[None]
    col_sel = np.zeros((2,) + col_hit.shape[:2] + (2 * GRID_W,), np.float32)
    for half in range(2):
        col_sel[half, :, :, half * GRID_W:(half + 1) * GRID_W] = col_hit
    hi = lax.Precision.HIGHEST
    per_col = jnp.einsum("hde,lecn->hldcn", rpb.astype(F32), col_sel, precision=hi)
    table = jnp.einsum("uld,hldcn->hucn", row_sel, per_col, precision=hi)
    valid = ((uniq >= 0)[:, None, :, None] & col_ok[None, :, None, :]).reshape(len(uniq), GRID_W, 2 * GRID_W)
    return (jnp.where(valid[None], table, NEG_INF), tuple(int(p) for p in pair0),
            tuple(tuple(int(u) for u in row) for row in index))


def _rope_tables(length):
    t = jnp.arange(length)
    row = (t // GRID_W).astype(F32)
    col = (t % GRID_W).astype(F32)
    n_freq = HEAD_DIM // 4
    inv = ROPE_BASE ** (-jnp.arange(n_freq, dtype=F32) / n_freq)
    ar = row[:, None] * inv
    ac = col[:, None] * inv
    ang = jnp.concatenate([ar, ar, ac, ac], axis=-1)
    cos, sin = jnp.cos(ang), jnp.sin(ang)
    lane = jnp.arange(HEAD_DIM)
    takes_left = ((lane // n_freq) % 2 == 0)[None, :]
    return cos, jnp.where(takes_left, -sin, 0.0), jnp.where(takes_left, 0.0, sin)


MERGE_TM = 512


def _resident(shape):
    zeros = (0,) * len(shape)
    return pl.BlockSpec(shape, lambda i: zeros, pipeline_mode=pl.Buffered(1))


def _mix_kernel(oa_ref, ob_ref, ga_ref, gb_ref, wpa_ref, wpb_ref, mix_ref):
    ya = _dot(oa_ref[...], wpa_ref[...])
    yb = _dot(ob_ref[...], wpb_ref[...])
    mix = (jax.nn.sigmoid(ga_ref[...].astype(F32)) * ya
           + jax.nn.sigmoid(gb_ref[...].astype(F32)) * yb)
    mix_ref[...] = mix.astype(mix_ref.dtype)


def _mix_call(oa, ob, gates, wpa_bf, wpb_bf):
    m = oa.shape[0]
    d = wpa_bf.shape[1]
    row = lambda i: (i, 0)
    return pl.pallas_call(
        _mix_kernel,
        grid=(m // MERGE_TM,),
        in_specs=[pl.BlockSpec((MERGE_TM, NA_WIDTH), row),
                  pl.BlockSpec((MERGE_TM, NB_WIDTH), row),
                  pl.BlockSpec((MERGE_TM, d), lambda i: (i, 0)),
                  pl.BlockSpec((MERGE_TM, d), lambda i: (i, 1)),
                  _resident((NA_WIDTH, d)), _resident((NB_WIDTH, d))],
        out_specs=pl.BlockSpec((MERGE_TM, d), row),
        out_shape=jax.ShapeDtypeStruct((m, d), BF16),
        compiler_params=_cparams(("arbitrary",)),
        name="mix",
    )(oa, ob, gates, gates, wpa_bf, wpb_bf)


def _outproj_kernel(mix_ref, wout_ref, x_ref, g1_ref, sc2_ref, sh2_ref, n2w_ref, wr_ref,
                    x1_ref, h2_ref, lg_ref, acc_ref):
    i = pl.program_id(0)
    last = pl.num_programs(0) - 1

    def finish(prod):
        x1 = x_ref[...] + g1_ref[0] * prod
        x1_ref[...] = x1
        h2 = _rms(x1, n2w_ref[...]) * (1.0 + sc2_ref[0]) + sh2_ref[0]
        h2_ref[...] = _pack_halves(h2)
        lg_ref[...] = _dot_split(h2, wr_ref[...])

    @pl.when(i == 0)
    def _():
        acc_ref[...] = _dot(mix_ref[...], wout_ref[...])

    @pl.when((i > 0) & (i < last))
    def _():
        prod = acc_ref[...]
        acc_ref[...] = _dot(mix_ref[...], wout_ref[...])
        finish(prod)

    @pl.when(i == last)
    def _():
        finish(acc_ref[...])


def _outproj_call(mix, wout_bf, x, g1, sc2, sh2, n2w, wr, tiles_per_group):
    m, d = x.shape
    n_tiles = m // MERGE_TM
    ahead = lambda i: (jnp.minimum(i, n_tiles - 1), 0)
    done = lambda i: (jnp.maximum(i - 1, 0), 0)
    grp = lambda i: (jnp.maximum(i - 1, 0) // tiles_per_group, 0, 0)
    return pl.pallas_call(
        _outproj_kernel,
        grid=(n_tiles + 1,),
        in_specs=[pl.BlockSpec((MERGE_TM, d), ahead),
                  _resident((d, d)),
                  pl.BlockSpec((MERGE_TM, d), done),
                  pl.BlockSpec((1, 1, d), grp), pl.BlockSpec((1, 1, d), grp), pl.BlockSpec((1, 1, d), grp),
                  pl.BlockSpec((1, d), lambda i: (0, 0)),
                  _resident((d, LOGIT_PAD))],
        out_specs=[pl.BlockSpec((MERGE_TM, d), done),
                   pl.BlockSpec((MERGE_TM, d // 2), done),
                   pl.BlockSpec((MERGE_TM, LOGIT_PAD), done)],
        out_shape=[jax.ShapeDtypeStruct((m, d), F32),
                   jax.ShapeDtypeStruct((m, d // 2), jnp.uint32),
                   jax.ShapeDtypeStruct((m, LOGIT_PAD), F32)],
        scratch_shapes=[pltpu.VMEM((MERGE_TM, d), F32)],
        compiler_params=_cparams(("arbitrary",)),
        name="outproj",
    )(mix, wout_bf, x, g1, sc2, sh2, n2w, wr)


ROUTE_TM = 512


def _first_index_of_max(vals, idx, n):
    mx = jnp.max(vals, axis=0, keepdims=True)
    first = jnp.min(jnp.where(vals == mx, idx, n), axis=0, keepdims=True)
    return mx, first


def _route_kernel(lg_ref, bias_ref, eid_ref, gw_ref, rank_ref, cnt_ref, base_ref):
    step = pl.program_id(0)

    @pl.when(step == 0)
    def _():
        base_ref[...] = jnp.zeros_like(base_ref)

    lt = lg_ref[...].T + bias_ref[...]
    n_tok = lt.shape[1]
    le = lt[0:N_EXPERTS]
    lgrp = lt[N_EXPERTS:N_EXPERTS + N_GROUPS]
    gi = lax.broadcasted_iota(jnp.int32, (N_GROUPS, n_tok), 0)
    gmax, gsel = _first_index_of_max(lgrp, gi, N_GROUPS)
    pg_sel = 1.0 / jnp.sum(jnp.exp(lgrp - gmax), axis=0, keepdims=True)
    le_sel = jnp.zeros((EXPERTS_PER_GROUP, n_tok), F32)
    for g in range(N_GROUPS):
        le_sel = jnp.where(gsel == g, le[g * EXPERTS_PER_GROUP:(g + 1) * EXPERTS_PER_GROUP], le_sel)
    ei = lax.broadcasted_iota(jnp.int32, (EXPERTS_PER_GROUP, n_tok), 0)
    v0, i0 = _first_index_of_max(le_sel, ei, EXPERTS_PER_GROUP)
    rest = jnp.where(ei == i0, -jnp.inf, le_sel)
    v1, i1 = _first_index_of_max(rest, ei, EXPERTS_PER_GROUP)
    e1 = jnp.exp(v1 - v0)
    w0 = pg_sel / (1.0 + e1)
    w1 = pg_sel * e1 / (1.0 + e1)
    eid0 = gsel * EXPERTS_PER_GROUP + i0
    eid1 = gsel * EXPERTS_PER_GROUP + i1

    xi = lax.broadcasted_iota(jnp.int32, (N_EXPERTS, n_tok), 0)
    si = lax.broadcasted_iota(jnp.int32, (n_tok, n_tok), 0)
    ti = lax.broadcasted_iota(jnp.int32, (n_tok, n_tok), 1)
    before = (si < ti).astype(BF16)
    base = base_ref[...]
    hot0 = (xi == eid0).astype(F32)
    hot1 = (xi == eid1).astype(F32)
    pre0 = _dot(hot0.astype(BF16), before)
    pre1 = _dot(hot1.astype(BF16), before)
    tot0 = jnp.sum(hot0, axis=1, keepdims=True)
    tot1 = jnp.sum(hot1, axis=1, keepdims=True)
    rank0 = jnp.sum(hot0 * (base + pre0), axis=0, keepdims=True)
    rank1 = jnp.sum(hot1 * (base + tot0 + pre1), axis=0, keepdims=True)
    base = base + tot0 + tot1
    base_ref[...] = base

    ri = lax.broadcasted_iota(jnp.int32, (8, n_tok), 0)
    pick = lambda a, b: jnp.where(ri == 0, a, jnp.where(ri == 1, b, jnp.zeros_like(a)))
    eid_ref[...] = pick(eid0, eid1)
    gw_ref[...] = pick(w0, w1)
    rank_ref[...] = pick(rank0, rank1).astype(jnp.int32)
    cnt_ref[...] = jnp.broadcast_to(base, cnt_ref.shape).astype(jnp.int32)


def _route_call(logits, bias_col):
    t = logits.shape[0]
    tok = pl.BlockSpec((8, ROUTE_TM), lambda i: (0, i))
    return pl.pallas_call(
        _route_kernel,
        grid=(t // ROUTE_TM,),
        in_specs=[pl.BlockSpec((ROUTE_TM, LOGIT_PAD), lambda i: (i, 0)),
                  pl.BlockSpec((LOGIT_PAD, 1), lambda i: (0, 0))],
        out_specs=[tok, tok, tok, pl.BlockSpec((N_EXPERTS, 128), lambda i: (0, 0))],
        out_shape=[jax.ShapeDtypeStruct((8, t), jnp.int32),
                   jax.ShapeDtypeStruct((8, t), F32),
                   jax.ShapeDtypeStruct((8, t), jnp.int32),
                   jax.ShapeDtypeStruct((N_EXPERTS, 128), jnp.int32)],
        scratch_shapes=[pltpu.VMEM((N_EXPERTS, 1), F32)],
        compiler_params=_cparams(("arbitrary",)),
        name="route",
    )(logits, bias_col)


def _moe_layout(n_pairs):
    padded_rows = -(-(n_pairs + N_EXPERTS * (MOE_ROW_BLOCK - 1)) // MOE_ROW_BLOCK) * MOE_ROW_BLOCK
    n_items = -(-padded_rows // MOE_ROW_GROUP) + N_EXPERTS
    return padded_rows, n_items


def _routing_tables(eid, rank, counts, n_items):
    padded = (counts + MOE_ROW_BLOCK - 1) // MOE_ROW_BLOCK * MOE_ROW_BLOCK
    pad_end = jnp.cumsum(padded)
    pad_start = pad_end - padded
    hot = eid[..., None] == jnp.arange(N_EXPERTS, dtype=jnp.int32)
    dest = (jnp.sum(jnp.where(hot, pad_start, 0), axis=-1) + rank).astype(jnp.int32)
    tail = jnp.where(padded > counts, pad_end - MOE_ROW_BLOCK, -1).astype(jnp.int32)
    per_expert = (padded + MOE_ROW_GROUP - 1) // MOE_ROW_GROUP
    item_end = jnp.cumsum(per_expert)
    item_start = item_end - per_expert
    total = item_end[-1]
    ii = jnp.arange(n_items, dtype=jnp.int32)
    e_of = jnp.minimum(jnp.searchsorted(item_end, ii, side="right"), N_EXPERTS - 1).astype(jnp.int32)
    valid = ii < total
    e_last = e_of[jnp.maximum(total - 1, 0)]
    local = ii - item_start[e_of]
    row0 = pad_start[e_of] + local * MOE_ROW_GROUP
    nblk = jnp.clip((padded[e_of] - local * MOE_ROW_GROUP) // MOE_ROW_BLOCK, 0, MOE_ROW_GROUP // MOE_ROW_BLOCK)
    item_e = jnp.where(valid, e_of, e_last).astype(jnp.int32)
    item_row0 = jnp.where(valid, row0, 0).astype(jnp.int32)
    item_nblk = jnp.where(valid, nblk, 0).astype(jnp.int32)
    return dest, tail, item_e, item_row0, item_nblk


DISPATCH_TOKENS = 256
ROW_DMA_GROUP = 8


def _row_copy(src, s, dst, d, sem):
    return pltpu.make_async_copy(src.at[pl.ds(s, 1), :], dst.at[pl.ds(d, 1), :], sem)


def _dispatch_kernel(dest_ref, tail_ref, h_ctx, h_lat, xs, zero_buf, sem, *, n_ctx, n_tok):
    step = pl.program_id(0)
    tail_copy = lambda e: pltpu.make_async_copy(
        zero_buf, xs.at[pl.ds(pl.multiple_of(tail_ref[e], MOE_ROW_BLOCK), MOE_ROW_BLOCK), :], sem.at[1])

    @pl.when(step == 0)
    def _():
        zero_buf[...] = jnp.zeros_like(zero_buf)
        for e in range(N_EXPERTS):
            @pl.when(tail_ref[e] >= 0)
            def _():
                tail_copy(e).start()
        for e in range(N_EXPERTS):
            @pl.when(tail_ref[e] >= 0)
            def _():
                tail_copy(e).wait()

    tok0 = step * DISPATCH_TOKENS

    def scatter(src):
        def issue(g, carry):
            base = pl.multiple_of(g * ROW_DMA_GROUP, ROW_DMA_GROUP)
            for j in range(ROW_DMA_GROUP):
                _row_copy(src, base + j, xs, dest_ref[tok0 + base + j], sem.at[0]).start(priority=0)
                _row_copy(src, base + j, xs, dest_ref[n_tok + tok0 + base + j], sem.at[0]).start(priority=1)
            return carry

        def drain(g, carry):
            for j in range(2 * ROW_DMA_GROUP):
                _row_copy(src, 0, xs, 0, sem.at[0]).wait()
            return carry

        lax.fori_loop(0, DISPATCH_TOKENS // ROW_DMA_GROUP, issue, 0)
        lax.fori_loop(0, DISPATCH_TOKENS // ROW_DMA_GROUP, drain, 0)

    @pl.when(tok0 < n_ctx)
    def _():
        scatter(h_ctx)

    @pl.when(tok0 >= n_ctx)
    def _():
        scatter(h_lat)


def _dispatch_call(dest_flat, tail, h_ctx, h_lat, padded_rows):
    n_ctx, d = h_ctx.shape
    n_tok = n_ctx + h_lat.shape[0]
    ctx_tiles = n_ctx // DISPATCH_TOKENS
    return pl.pallas_call(
        functools.partial(_dispatch_kernel, n_ctx=n_ctx, n_tok=n_tok),
        grid_spec=pltpu.PrefetchScalarGridSpec(
            num_scalar_prefetch=2,
            grid=(n_tok // DISPATCH_TOKENS,),
            in_specs=[pl.BlockSpec((DISPATCH_TOKENS, d), lambda i, dr, tr: (jnp.minimum(i, ctx_tiles - 1), 0)),
                      pl.BlockSpec((DISPATCH_TOKENS, d), lambda i, dr, tr: (jnp.maximum(i - ctx_tiles, 0), 0))],
            out_specs=pl.BlockSpec(memory_space=pl.ANY),
            scratch_shapes=[pltpu.VMEM((MOE_ROW_BLOCK, d), h_ctx.dtype), pltpu.SemaphoreType.DMA((2,))]),
        out_shape=jax.ShapeDtypeStruct((padded_rows, d), h_ctx.dtype),
        compiler_params=_cparams(("arbitrary",)),
        name="dispatch",
    )(dest_flat, tail, h_ctx, h_lat)


def _moe_kernel(item_e, item_row0, item_nblk, xs, w1_ref, w3_ref, w2_ref, ys,
                x_in, x_bf, acc, y_out, w1_bf, w3_bf, w2_bf, sem):
    i = pl.program_id(0)
    c = pl.program_id(1)
    n_items = pl.num_programs(0)
    last_c = pl.num_programs(1) - 1
    nblk = item_nblk[i]
    max_blk = MOE_ROW_GROUP // MOE_ROW_BLOCK
    blk = lambda b: pl.ds(b * MOE_ROW_BLOCK, MOE_ROW_BLOCK)

    def rows_of(item, b):
        return pl.ds(pl.multiple_of(item_row0[item], MOE_ROW_BLOCK) + b * MOE_ROW_BLOCK, MOE_ROW_BLOCK)

    load = lambda item, b: pltpu.make_async_copy(xs.at[rows_of(item, b), :], x_in.at[blk(b), :], sem.at[0])
    store = lambda item, b: pltpu.make_async_copy(y_out.at[blk(b), :], ys.at[rows_of(item, b), :], sem.at[1])

    def for_blocks(item, fn):
        n = item_nblk[item]
        for b in range(max_blk):
            @pl.when(b < n)
            def _():
                fn(item, b)

    @pl.when(c == 0)
    def _():
        @pl.when(i == 0)
        def _():
            for_blocks(0, lambda it, b: load(it, b).start())

        for_blocks(i, lambda it, b: load(it, b).wait())

        def cast(it, b):
            x_bf[blk(b), :] = _unpack_halves(x_in[blk(b), :]).astype(BF16)

        for_blocks(i, cast)

        @pl.when(i + 1 < n_items)
        def _():
            for_blocks(i + 1, lambda it, b: load(it, b).start())

        def clear(it, b):
            acc[blk(b), :] = jnp.zeros((MOE_ROW_BLOCK, acc.shape[1]), F32)

        for_blocks(i, clear)

    @pl.when(nblk > 0)
    def _():
        def block(rows, w1, w3, w2):
            x = x_bf[rows, :]
            h1 = _dot(x, w1)
            h3 = _dot(x, w3)
            a = (h1 * jax.nn.sigmoid(h1) * h3).astype(BF16)
            acc[rows, :] += _dot(a, w2)

        w1 = w1_ref[0].astype(BF16)
        w3 = w3_ref[0].astype(BF16)
        w2 = w2_ref[0].astype(BF16)
        w1_bf[...] = w1
        w3_bf[...] = w3
        w2_bf[...] = w2
        block(blk(0), w1, w3, w2)

        def body(b, carry):
            rows = pl.ds(pl.multiple_of(b * MOE_ROW_BLOCK, MOE_ROW_BLOCK), MOE_ROW_BLOCK)
            block(rows, w1_bf[...], w3_bf[...], w2_bf[...])
            return carry

        lax.fori_loop(1, nblk, body, 0)

    @pl.when(c == last_c)
    def _():
        @pl.when(i > 0)
        def _():
            for_blocks(i - 1, lambda it, b: store(it, b).wait())

        def pack(it, b):
            y_out[blk(b), :] = _pack_halves(acc[blk(b), :])

        for_blocks(i, pack)
        for_blocks(i, lambda it, b: store(it, b).start())

        @pl.when(i == n_items - 1)
        def _():
            for_blocks(i, lambda it, b: store(it, b).wait())


def _moe_call(item_e, item_row0, item_nblk, xs, w1, w3, w2):
    padded_rows, d_packed = xs.shape
    d = w1.shape[1]
    n_items = item_e.shape[0]
    f = w1.shape[2]
    nc = f // MOE_F_CHUNK
    chunk = lambda i, c, ib: jnp.where(ib[i] > 0, c, nc - 1)
    return pl.pallas_call(
        _moe_kernel,
        grid_spec=pltpu.PrefetchScalarGridSpec(
            num_scalar_prefetch=3,
            grid=(n_items, nc),
            in_specs=[pl.BlockSpec(memory_space=pl.ANY),
                      pl.BlockSpec((1, d, MOE_F_CHUNK), lambda i, c, ie, ir, ib: (ie[i], 0, chunk(i, c, ib))),
                      pl.BlockSpec((1, d, MOE_F_CHUNK), lambda i, c, ie, ir, ib: (ie[i], 0, chunk(i, c, ib))),
                      pl.BlockSpec((1, MOE_F_CHUNK, d), lambda i, c, ie, ir, ib: (ie[i], chunk(i, c, ib), 0))],
            out_specs=pl.BlockSpec(memory_space=pl.ANY),
            scratch_shapes=[pltpu.VMEM((MOE_ROW_GROUP, d_packed), xs.dtype),
                            pltpu.VMEM((MOE_ROW_GROUP, d), BF16),
                            pltpu.VMEM((MOE_ROW_GROUP, d), F32),
                            pltpu.VMEM((MOE_ROW_GROUP, d_packed), xs.dtype),
                            pltpu.VMEM((d, MOE_F_CHUNK), BF16),
                            pltpu.VMEM((d, MOE_F_CHUNK), BF16),
                            pltpu.VMEM((MOE_F_CHUNK, d), BF16),
                            pltpu.SemaphoreType.DMA((2,))]),
        out_shape=jax.ShapeDtypeStruct((padded_rows, d_packed), xs.dtype),
        compiler_params=_cparams(("arbitrary", "arbitrary")),
        name="moe",
    )(item_e, item_row0, item_nblk, xs, w1, w3, w2)


COMBINE_TM = 256


def _combine_kernel(dest_ref, x1_ref, gw_ref, g2_ref, ys, o_ref, y0, y1, sem, *, tok_base, n_tok):
    tok0 = tok_base + pl.program_id(0) * COMBINE_TM

    def issue(g, carry):
        base = pl.multiple_of(g * ROW_DMA_GROUP, ROW_DMA_GROUP)
        for j in range(ROW_DMA_GROUP):
            _row_copy(ys, dest_ref[tok0 + base + j], y0, base + j, sem).start(priority=0)
            _row_copy(ys, dest_ref[n_tok + tok0 + base + j], y1, base + j, sem).start(priority=1)
        return carry

    def drain(g, carry):
        for j in range(ROW_DMA_GROUP):
            _row_copy(ys, 0, y0, 0, sem).wait()
            _row_copy(ys, 0, y1, 0, sem).wait()
        return carry

    lax.fori_loop(0, COMBINE_TM // ROW_DMA_GROUP, issue, 0)
    lax.fori_loop(0, COMBINE_TM // ROW_DMA_GROUP, drain, 0)
    gw = gw_ref[...]
    moe = gw[:, 0:1] * _unpack_halves(y0[...]) + gw[:, 1:2] * _unpack_halves(y1[...])
    o_ref[...] = x1_ref[...] + g2_ref[0] * moe


def _combine_call(dest_flat, x1, gw, g2, ys, tok_base, n_tok, tiles_per_group):
    m, d = x1.shape
    return pl.pallas_call(
        functools.partial(_combine_kernel, tok_base=tok_base, n_tok=n_tok),
        grid_spec=pltpu.PrefetchScalarGridSpec(
            num_scalar_prefetch=1,
            grid=(m // COMBINE_TM,),
            in_specs=[pl.BlockSpec((COMBINE_TM, d), lambda i, dr: (i, 0)),
                      pl.BlockSpec((COMBINE_TM, 2), lambda i, dr: (i, 0)),
                      pl.BlockSpec((1, 1, d), lambda i, dr: (i // tiles_per_group, 0, 0)),
                      pl.BlockSpec(memory_space=pl.ANY)],
            out_specs=pl.BlockSpec((COMBINE_TM, d), lambda i, dr: (i, 0)),
            scratch_shapes=[pltpu.VMEM((COMBINE_TM,) + ys.shape[1:], ys.dtype),
                            pltpu.VMEM((COMBINE_TM,) + ys.shape[1:], ys.dtype),
                            pltpu.SemaphoreType.DMA]),
        out_shape=jax.ShapeDtypeStruct((m, d), F32),
        compiler_params=_cparams(("arbitrary",)),
        name="combine",
    )(dest_flat, x1, gw, g2, ys)


def kernel(x_prompt, x_sample, cache_a_k, cache_a_v, cache_b_k, cache_b_v, c, c_ctx, norm1_w, norm2_w, w_ada, b_ada, w_in, qn_a, kn_a, qn_b, kn_b, rpb_a, sink_b, w_pa, w_pb, w_out, w_rg, b_rg, w_re, b_re, w1, w3, w2):
    batch, seq, d = x_prompt.shape
    dec_batch, dec_seq, _ = x_sample.shape
    depth = norm1_w.shape[0]
    assert depth == 1, "one trunk layer"
    past = cache_a_k.shape[2]
    n_ctx, n_lat = batch * seq, dec_batch * dec_seq
    n_tok = n_ctx + n_lat

    xc = x_prompt.reshape(n_ctx, d)
    xl = x_sample.reshape(n_lat, d)

    cond = jnp.concatenate([c_ctx[None, :], c], axis=0)
    mod = _ada_call(cond, w_ada[0], b_ada[0][None, :])
    sh1, sc1, g1, sh2, sc2, g2 = [mod[:, i * d:(i + 1) * d][:, None, :] for i in range(6)]
    ctx_rows, lat_rows = slice(0, 1), slice(1, 1 + dec_batch)

    nw1, nw2 = norm1_w[0][None, :], norm2_w[0][None, :]
    qna, kna, qnb, knb = qn_a[0][None, :], kn_a[0][None, :], qn_b[0][None, :], kn_b[0][None, :]
    sink = sink_b[0]
    wpa_bf, wpb_bf, wout_bf = w_pa[0].astype(BF16), w_pb[0].astype(BF16), w_out[0].astype(BF16)
    wr = jnp.zeros((d, LOGIT_PAD), F32).at[:, :N_EXPERTS].set(w_re[0]).at[:, N_EXPERTS:N_EXPERTS + N_GROUPS].set(w_rg[0])
    br = jnp.zeros((LOGIT_PAD, 1), F32).at[:N_EXPERTS, 0].set(b_re[0]).at[N_EXPERTS:N_EXPERTS + N_GROUPS, 0].set(b_rg[0])

    proj_c, gates_c = _inproj_call(xc, nw1, sc1[ctx_rows], sh1[ctx_rows], w_in[0], n_ctx)
    proj_l, gates_l = _inproj_call(xl, nw1, sc1[lat_rows], sh1[lat_rows], w_in[0], dec_seq)

    oa_c, ob_c, new_a_k, new_a_v, new_b_k, new_b_v = _ctx_attn_call(proj_c, seq, qna, kna, qnb, knb, sink)

    bias_blocks, pair0, bias_index = _na_bias_blocks(rpb_a[0], dec_seq)
    cos, sin_a, sin_b = _rope_tables(dec_seq)
    rows_of = lambda cache: cache.reshape(-1, HEAD_DIM)
    oa_l = _lat_attn_a_call(proj_l, dec_seq, past, rows_of(cache_a_k), rows_of(cache_a_v),
                            bias_blocks, pair0, bias_index, qna, kna)
    ob_l = _lat_attn_b_call(proj_l, dec_seq, past, rows_of(cache_b_k), rows_of(cache_b_v),
                            cos, sin_a, sin_b, qnb, knb, sink)

    mix_c = _mix_call(oa_c, ob_c, gates_c, wpa_bf, wpb_bf)
    mix_l = _mix_call(oa_l, ob_l, gates_l, wpa_bf, wpb_bf)
    x1_c, h2_c, lg_c = _outproj_call(mix_c, wout_bf, xc, g1[ctx_rows], sc2[ctx_rows], sh2[ctx_rows],
                                     nw2, wr, n_ctx // MERGE_TM)
    x1_l, h2_l, lg_l = _outproj_call(mix_l, wout_bf, xl, g1[lat_rows], sc2[lat_rows], sh2[lat_rows],
                                     nw2, wr, dec_seq // MERGE_TM)

    eid, gw, rank, cnt = _route_call(jnp.concatenate([lg_c, lg_l], axis=0), br)
    padded_rows, n_items = _moe_layout(2 * n_tok)
    dest, tail, item_e, item_row0, item_nblk = _routing_tables(eid[:2], rank[:2], cnt[:, 0], n_items)
    dest_flat = dest.reshape(-1)
    xs = _dispatch_call(dest_flat, tail, h2_c, h2_l, padded_rows)
    ys = _moe_call(item_e, item_row0, item_nblk, xs, w1[0], w3[0], w2[0])
    gw_t = gw[:2].T
    y_c = _combine_call(dest_flat, x1_c, gw_t[:n_ctx], g2[ctx_rows], ys, 0, n_tok, n_ctx // COMBINE_TM)
    y_l = _combine_call(dest_flat, x1_l, gw_t[n_ctx:], g2[lat_rows], ys, n_ctx, n_tok, dec_seq // COMBINE_TM)

    state = lambda a, heads: a.reshape(batch, 1, seq, heads, HEAD_DIM)
    return (y_c.reshape(batch, seq, d), y_l.reshape(dec_batch, dec_seq, d),
            state(new_a_k, NA_HEADS), state(new_a_v, NA_HEADS),
            state(new_b_k, NB_KV_HEADS), state(new_b_v, NB_KV_HEADS))
```

```python
import functools

import jax
import jax.numpy as jnp
import numpy as np
from jax import lax
from jax.experimental import pallas as pl
from jax.experimental.pallas import tpu as pltpu

D_MODEL = 2048
HEAD_DIM = 128
NA_HEADS = 8
NA_WIDTH = NA_HEADS * HEAD_DIM
NB_Q_HEADS = 8
NB_KV_HEADS = 2
NB_GROUP = NB_Q_HEADS // NB_KV_HEADS
NB_WIDTH = NB_Q_HEADS * HEAD_DIM
NB_KV_WIDTH = NB_KV_HEADS * HEAD_DIM
GRID_W = 64
NA_WIN_ROWS = 8
NA_WIN_COLS = 16
WINDOW = 128
N_GROUPS = 4
EXPERTS_PER_GROUP = 8
N_EXPERTS = N_GROUPS * EXPERTS_PER_GROUP
D_EXPERT = 1024
IN_WIDTH = 3 * NA_WIDTH + NB_WIDTH + 2 * NB_KV_WIDTH + 2 * D_MODEL
ROPE_BASE = 10000.0
NORM_EPS = 1e-6
NEG_INF = -1e30
ATTN_SCALE = HEAD_DIM ** -0.5

QA_HEAD0 = 0
KA_HEAD0 = NA_HEADS
VA_HEAD0 = 2 * NA_HEADS
QB_HEAD0 = 3 * NA_HEADS
KB_HEAD0 = QB_HEAD0 + NB_Q_HEADS
VB_HEAD0 = KB_HEAD0 + NB_KV_HEADS
GATE_COL0 = (VB_HEAD0 + NB_KV_HEADS) * HEAD_DIM

LOGIT_PAD = 128
MOE_ROW_BLOCK = 256
MOE_ROW_GROUP = 1024
MOE_F_CHUNK = 512
VMEM_LIMIT = 56 * 1024 * 1024

F32 = jnp.float32
BF16 = jnp.bfloat16


def _cparams(sem, vmem_limit=VMEM_LIMIT):
    return pltpu.CompilerParams(dimension_semantics=sem, vmem_limit_bytes=vmem_limit)


def _rms(x, w):
    x = x.astype(F32)
    return x * lax.rsqrt(jnp.mean(x * x, axis=-1, keepdims=True) + NORM_EPS) * w


def _dot(a, b):
    return jnp.dot(a, b, preferred_element_type=F32)


def _dot_nt(a, b):
    return lax.dot_general(a, b, (((1,), (1,)), ((), ())), preferred_element_type=F32)


def _pack_halves(x):
    n = x.shape[1] // 2
    lo = lax.bitcast_convert_type(x[:, :n].astype(BF16).astype(F32), jnp.uint32)
    hi = lax.bitcast_convert_type(x[:, n:].astype(BF16).astype(F32), jnp.uint32)
    return hi | (lo >> 16)


def _unpack_halves(w):
    lo = lax.bitcast_convert_type(w << 16, F32)
    hi = lax.bitcast_convert_type(w & jnp.uint32(0xFFFF0000), F32)
    return jnp.concatenate([lo, hi], axis=1)


def _dot_split(a, b):
    a_hi = a.astype(BF16)
    a_lo = (a - a_hi.astype(F32)).astype(BF16)
    b_hi = b.astype(BF16)
    b_lo = (b - b_hi.astype(F32)).astype(BF16)
    return _dot(a_hi, b_hi) + (_dot(a_lo, b_hi) + _dot(a_hi, b_lo))


ADA_ROWS = 8
ADA_TN = 1024


def _ada_kernel(c_ref, w_ref, b_ref, o_ref):
    n_rows, d, lanes = c_ref.shape
    tn = w_ref.shape[1]

    def body(kb, acc):
        ks = pl.ds(pl.multiple_of(kb * 8, 8), 8)
        w = w_ref[ks, :]
        out = []
        for r in range(n_rows):
            c = c_ref[r, ks, :]
            s = c * jax.nn.sigmoid(c)
            out.append(acc[r] + w * jnp.concatenate([s] * (tn // lanes), axis=1))
        return tuple(out)

    acc = lax.fori_loop(0, d // 8, body, tuple(jnp.zeros((8, tn), F32) for _ in range(n_rows)), unroll=8)
    ri = lax.broadcasted_iota(jnp.int32, (ADA_ROWS, tn), 0)
    res = jnp.zeros((ADA_ROWS, tn), F32)
    for r in range(n_rows):
        row = jnp.sum(acc[r], axis=0, keepdims=True) + b_ref[...]
        res = jnp.where(ri == r, row, res)
    o_ref[...] = res


def _ada_call(cond, w_ada, b_ada):
    n_rows, d = cond.shape
    n = w_ada.shape[1]
    lanes = 128
    cond_lanes = jnp.broadcast_to(cond[:, :, None], (n_rows, d, lanes))
    return pl.pallas_call(
        _ada_kernel,
        grid=(n // ADA_TN,),
        in_specs=[pl.BlockSpec((n_rows, d, lanes), lambda j: (0, 0, 0)),
                  pl.BlockSpec((d, ADA_TN), lambda j: (0, j)),
                  pl.BlockSpec((1, ADA_TN), lambda j: (0, j))],
        out_specs=pl.BlockSpec((ADA_ROWS, ADA_TN), lambda j: (0, j)),
        out_shape=jax.ShapeDtypeStruct((ADA_ROWS, n), F32),
        compiler_params=_cparams(("arbitrary",)),
        name="ada",
    )(cond_lanes, w_ada, b_ada)


INPROJ_TM = 2048
INPROJ_TN = 512
NORM_ROWS = 128


def _inproj_kernel(x_hbm, nw_ref, sc_ref, sh_ref, w_ref, qkv_ref, gate_ref, x_buf, h_scr, sem, *, qkv_tiles):
    i = pl.program_id(0)
    j = pl.program_id(1)
    tm = x_buf.shape[0]
    fetch = lambda tile: pltpu.make_async_copy(
        x_hbm.at[pl.ds(pl.multiple_of(tile * tm, tm), tm), :], x_buf, sem)

    @pl.when(j == 0)
    def _():
        @pl.when(i == 0)
        def _():
            fetch(0).start()

        fetch(i).wait()
        nw = nw_ref[...]
        sc = 1.0 + sc_ref[0]
        sh = sh_ref[0]

        def body(r, carry):
            rows = pl.ds(pl.multiple_of(r * NORM_ROWS, NORM_ROWS), NORM_ROWS)
            h_scr[rows, :] = (_rms(x_buf[rows, :], nw) * sc + sh).astype(BF16)
            return carry

        lax.fori_loop(0, tm // NORM_ROWS, body, 0)

    @pl.when((j == 1) & (i + 1 < pl.num_programs(0)))
    def _():
        fetch(i + 1).start()

    res = _dot(h_scr[...], w_ref[...].astype(BF16))

    @pl.when(j < qkv_tiles)
    def _():
        qkv_ref[...] = res.astype(qkv_ref.dtype)

    @pl.when(j >= qkv_tiles)
    def _():
        gate_ref[...] = res.astype(gate_ref.dtype)


def _inproj_call(x, nw, sc, sh, w_in, rows_per_group):
    m, d = x.shape
    n = w_in.shape[1]
    assert n // INPROJ_TN >= 2, "the next token tile is requested in the second column step"
    tm = min(INPROJ_TM, rows_per_group)
    tiles_per_group = rows_per_group // tm
    qkv_tiles = GATE_COL0 // INPROJ_TN
    grp = lambda i, j: (i // tiles_per_group, 0, 0)
    return pl.pallas_call(
        functools.partial(_inproj_kernel, qkv_tiles=qkv_tiles),
        grid=(m // tm, n // INPROJ_TN),
        in_specs=[pl.BlockSpec(memory_space=pl.ANY),
                  pl.BlockSpec((1, d), lambda i, j: (0, 0)),
                  pl.BlockSpec((1, 1, d), grp),
                  pl.BlockSpec((1, 1, d), grp),
                  pl.BlockSpec((d, INPROJ_TN), lambda i, j: (0, j))],
        out_specs=[pl.BlockSpec((tm, INPROJ_TN), lambda i, j: (i, jnp.minimum(j, qkv_tiles - 1))),
                   pl.BlockSpec((tm, INPROJ_TN), lambda i, j: (i, jnp.maximum(j - qkv_tiles, 0)))],
        out_shape=[jax.ShapeDtypeStruct((m, GATE_COL0), BF16),
                   jax.ShapeDtypeStruct((m, n - GATE_COL0), BF16)],
        scratch_shapes=[pltpu.VMEM((tm, d), F32), pltpu.VMEM((tm, d), BF16), pltpu.SemaphoreType.DMA],
        compiler_params=_cparams(("arbitrary", "arbitrary")),
        name="inproj",
    )(x, nw, sc, sh, w_in)


def _with_ones(v):
    return jnp.concatenate([v.astype(BF16), jnp.ones(v.shape, BF16)], axis=1)


def _softmax_pv(scores, values_with_ones, sink=None):
    m = None
    for s in scores:
        ms = jnp.max(s, axis=-1, keepdims=True)
        m = ms if m is None else jnp.maximum(m, ms)
    if sink is not None:
        m = jnp.maximum(m, sink)
    acc = None
    for s, v1 in zip(scores, values_with_ones):
        pv = _dot(jnp.exp(s - m).astype(BF16), v1)
        acc = pv if acc is None else acc + pv
    d = acc.shape[1] // 2
    den = acc[:, d:]
    if sink is not None:
        den = den + jnp.exp(sink - m)
    return acc[:, :d] / den


def _ctx_attn_kernel(qa_ref, ka_ref, va_ref, qb_ref, kb_ref, vb_ref,
                     qna_ref, kna_ref, qnb_ref, knb_ref, sink_ref,
                     oa_ref, ob_ref, nak_ref, nav_ref, nbk_ref, nbv_ref):
    qna, kna, qnb, knb = qna_ref[...], kna_ref[...], qnb_ref[...], knb_ref[...]
    seq = qa_ref.shape[0]
    for h in range(NA_HEADS):
        cols = slice(h * HEAD_DIM, (h + 1) * HEAD_DIM)
        q = (_rms(qa_ref[:, cols], qna) * ATTN_SCALE).astype(BF16)
        k = _rms(ka_ref[:, cols], kna)
        v = va_ref[:, cols]
        nak_ref[pl.ds(h, seq, stride=NA_HEADS), :] = k
        nav_ref[pl.ds(h, seq, stride=NA_HEADS), :] = v.astype(F32)
        s = _dot_nt(q, k.astype(BF16))
        oa_ref[:, cols] = _softmax_pv([s], [_with_ones(v)]).astype(oa_ref.dtype)
    for kv in range(NB_KV_HEADS):
        kcols = slice(kv * HEAD_DIM, (kv + 1) * HEAD_DIM)
        k = _rms(kb_ref[:, kcols], knb)
        v = vb_ref[:, kcols]
        nbk_ref[pl.ds(kv, seq, stride=NB_KV_HEADS), :] = k
        nbv_ref[pl.ds(kv, seq, stride=NB_KV_HEADS), :] = v.astype(F32)
        kb16 = k.astype(BF16)
        vb1 = _with_ones(v)
        for g in range(NB_GROUP):
            hq = kv * NB_GROUP + g
            cols = slice(hq * HEAD_DIM, (hq + 1) * HEAD_DIM)
            q = (_rms(qb_ref[:, cols], qnb) * ATTN_SCALE).astype(BF16)
            s = _dot_nt(q, kb16)
            ob_ref[:, cols] = _softmax_pv([s], [vb1], sink=sink_ref[hq]).astype(ob_ref.dtype)


def _ctx_attn_call(proj, seq, qna, kna, qnb, knb, sink):
    m = proj.shape[0]
    nb = m // seq
    wide = lambda blk: pl.BlockSpec((seq, NA_WIDTH), lambda b: (b, blk))
    narrow = lambda blk: pl.BlockSpec((seq, NB_KV_WIDTH), lambda b: (b, blk))
    vec = pl.BlockSpec((1, HEAD_DIM), lambda b: (0, 0))
    return pl.pallas_call(
        _ctx_attn_kernel,
        grid=(nb,),
        in_specs=[wide(QA_HEAD0 // NA_HEADS), wide(KA_HEAD0 // NA_HEADS), wide(VA_HEAD0 // NA_HEADS),
                  wide(QB_HEAD0 // NA_HEADS), narrow(KB_HEAD0 // NB_KV_HEADS), narrow(VB_HEAD0 // NB_KV_HEADS),
                  vec, vec, vec, vec,
                  pl.BlockSpec(memory_space=pltpu.SMEM)],
        out_specs=[pl.BlockSpec((seq, NA_WIDTH), lambda b: (b, 0)),
                   pl.BlockSpec((seq, NB_WIDTH), lambda b: (b, 0)),
                   pl.BlockSpec((seq * NA_HEADS, HEAD_DIM), lambda b: (b, 0)),
                   pl.BlockSpec((seq * NA_HEADS, HEAD_DIM), lambda b: (b, 0)),
                   pl.BlockSpec((seq * NB_KV_HEADS, HEAD_DIM), lambda b: (b, 0)),
                   pl.BlockSpec((seq * NB_KV_HEADS, HEAD_DIM), lambda b: (b, 0))],
        out_shape=[jax.ShapeDtypeStruct((m, NA_WIDTH), BF16),
                   jax.ShapeDtypeStruct((m, NB_WIDTH), BF16),
                   jax.ShapeDtypeStruct((m * NA_HEADS, HEAD_DIM), F32),
                   jax.ShapeDtypeStruct((m * NA_HEADS, HEAD_DIM), F32),
                   jax.ShapeDtypeStruct((m * NB_KV_HEADS, HEAD_DIM), F32),
                   jax.ShapeDtypeStruct((m * NB_KV_HEADS, HEAD_DIM), F32)],
        compiler_params=_cparams(("arbitrary",)),
        name="ctx_attn",
    )(proj, proj, proj, proj, proj, proj, qna, kna, qnb, knb, sink)


def _rope(x, cos, sin_a, sin_b):
    quarter = HEAD_DIM // 4
    return (x * cos + pltpu.roll(x, HEAD_DIM - quarter, 1) * sin_a
            + pltpu.roll(x, quarter, 1) * sin_b)


def _head_rows(cache_ref, head, n_heads):
    past = cache_ref.shape[0] // n_heads
    return cache_ref[pl.ds(head, past, stride=n_heads), :]


def _lat_attn_a_kernel(q_ref, k_ref, v_ref, ck_ref, cv_ref, cb_ref, qn_ref, kn_ref, o_ref, *, pair0, index):
    head = pl.program_id(0)
    q = (_rms(q_ref[...], qn_ref[...]) * ATTN_SCALE).astype(BF16)
    k = _rms(k_ref[...], kn_ref[...]).astype(BF16)
    v1 = _with_ones(v_ref[...])
    ck = _head_rows(ck_ref, head, NA_HEADS).astype(BF16)
    cv1 = _with_ones(_head_rows(cv_ref, head, NA_HEADS))
    pair = 2 * GRID_W
    rows_per_block = pair // GRID_W
    n_pairs = len(index[0])
    for blk in range(len(pair0) // rows_per_block):
        grid_rows = range(blk * rows_per_block, (blk + 1) * rows_per_block)
        p0 = pair0[grid_rows[0]]
        assert all(pair0[r] == p0 for r in grid_rows), "query rows of a block share their key pairs"
        rows = slice(blk * pair, (blk + 1) * pair)
        keys = slice(p0 * pair, (p0 + n_pairs) * pair)
        bias = jnp.concatenate(
            [jnp.concatenate([cb_ref[u] for u in index[r]], axis=1) for r in grid_rows], axis=0)
        qs = q[rows]
        s_win = _dot_nt(qs, k[keys]) + bias
        s_ctx = _dot_nt(qs, ck)
        o_ref[rows, :] = _softmax_pv([s_win, s_ctx], [v1[keys], cv1]).astype(o_ref.dtype)


def _lat_attn_b_kernel(q_ref, k_ref, v_ref, ck_ref, cv_ref, cos_ref, sina_ref, sinb_ref,
                       qn_ref, kn_ref, sink_ref, o_ref):
    cos, sin_a, sin_b = cos_ref[...], sina_ref[...], sinb_ref[...]
    q = _rope(_rms(q_ref[...], qn_ref[...]), cos, sin_a, sin_b)
    k = _rope(_rms(k_ref[...], kn_ref[...]), cos, sin_a, sin_b)
    q = (q * ATTN_SCALE).astype(BF16)
    k = k.astype(BF16)
    v1 = _with_ones(v_ref[...])
    length = q.shape[0]
    kv = pl.program_id(0) // NB_GROUP
    ck = _head_rows(ck_ref, kv, NB_KV_HEADS).astype(BF16)
    cv1 = _with_ones(_head_rows(cv_ref, kv, NB_KV_HEADS))
    sink = sink_ref[pl.program_id(0)]
    for qb in range(length // WINDOW):
        rows = slice(qb * WINDOW, (qb + 1) * WINDOW)
        lo, hi = max(0, (qb - 1) * WINDOW), min(length, (qb + 2) * WINDOW)
        qs = q[rows]
        s_win = _dot_nt(qs, k[lo:hi])
        qi = qb * WINDOW + lax.broadcasted_iota(jnp.int32, s_win.shape, 0)
        kj = lo + lax.broadcasted_iota(jnp.int32, s_win.shape, 1)
        s_win = jnp.where(jnp.abs(qi - kj) <= WINDOW, s_win, NEG_INF)
        s_ctx = _dot_nt(qs, ck)
        o_ref[rows, :] = _softmax_pv([s_win, s_ctx], [v1[lo:hi], cv1], sink=sink).astype(o_ref.dtype)


def _lat_attn_a_call(proj, length, past, ck, cv, cb, pair0, index, qn, kn):
    m = proj.shape[0]
    head = lambda h0: pl.BlockSpec((length, HEAD_DIM), lambda h, b: (b, h0 + h))
    cache = pl.BlockSpec((past * NA_HEADS, HEAD_DIM), lambda h, b: (b, 0))
    vec = pl.BlockSpec((1, HEAD_DIM), lambda h, b: (0, 0))
    return pl.pallas_call(
        functools.partial(_lat_attn_a_kernel, pair0=pair0, index=index),
        grid=(NA_HEADS, m // length),
        in_specs=[head(QA_HEAD0), head(KA_HEAD0), head(VA_HEAD0), cache, cache,
                  pl.BlockSpec((None,) + cb.shape[1:], lambda h, b: (h, 0, 0, 0)), vec, vec],
        out_specs=pl.BlockSpec((length, HEAD_DIM), lambda h, b: (b, h)),
        out_shape=jax.ShapeDtypeStruct((m, NA_WIDTH), BF16),
        compiler_params=_cparams(("arbitrary", "arbitrary")),
        name="lat_attn_a",
    )(proj, proj, proj, ck, cv, cb, qn, kn)


def _lat_attn_b_call(proj, length, past, ck, cv, cos, sin_a, sin_b, qn, kn, sink):
    m = proj.shape[0]
    qspec = pl.BlockSpec((length, HEAD_DIM), lambda h, b: (b, QB_HEAD0 + h))
    kvspec = lambda h0: pl.BlockSpec((length, HEAD_DIM), lambda h, b: (b, h0 + h // NB_GROUP))
    cache = pl.BlockSpec((past * NB_KV_HEADS, HEAD_DIM), lambda h, b: (b, 0))
    table = pl.BlockSpec((length, HEAD_DIM), lambda h, b: (0, 0))
    vec = pl.BlockSpec((1, HEAD_DIM), lambda h, b: (0, 0))
    return pl.pallas_call(
        _lat_attn_b_kernel,
        grid=(NB_Q_HEADS, m // length),
        in_specs=[qspec, kvspec(KB_HEAD0), kvspec(VB_HEAD0), cache, cache,
                  table, table, table, vec, vec, pl.BlockSpec(memory_space=pltpu.SMEM)],
        out_specs=pl.BlockSpec((length, HEAD_DIM), lambda h, b: (b, h)),
        out_shape=jax.ShapeDtypeStruct((m, NB_WIDTH), BF16),
        compiler_params=_cparams(("arbitrary", "arbitrary")),
        name="lat_attn_b",
    )(proj, proj, proj, ck, cv, cos, sin_a, sin_b, qn, kn, sink)


def _na_bias_blocks(rpb, length):
    rows = length // GRID_W
    kr_n = min(NA_WIN_ROWS, rows)
    n_pairs = min(kr_n // 2 + 1, rows // 2)
    r = np.arange(rows)
    c = np.arange(GRID_W)
    r0 = np.clip(r - kr_n // 2, 0, rows - kr_n)
    c0 = np.clip(c - NA_WIN_COLS // 2, 0, GRID_W - NA_WIN_COLS)
    pair0 = np.minimum(r0 // 2, rows // 2 - n_pairs)
    kr = 2 * (pair0[:, None, None] + np.arange(n_pairs)[None, :, None]) + np.arange(2)[None, None, :]
    row_ok = (kr >= r0[:, None, None]) & (kr < r0[:, None, None] + kr_n)
    col_ok = (c[None, :] >= c0[:, None]) & (c[None, :] < c0[:, None] + NA_WIN_COLS)
    dr = kr - r[:, None, None] + (NA_WIN_ROWS - 1)
    dc = np.clip(c[None, :] - c[:, None], -(NA_WIN_COLS - 1), NA_WIN_COLS - 1) + (NA_WIN_COLS - 1)
    offs = np.where(row_ok, dr, -1).reshape(-1, 2)
    uniq, inverse = np.unique(offs, axis=0, return_inverse=True)
    index = inverse.reshape(rows, n_pairs)
    row_sel = (uniq[:, :, None] == np.arange(2 * NA_WIN_ROWS - 1)[None, None, :]).astype(np.float32)
    col_hit = (dc[None] == np.arange(2 * NA_WIN_COLS - 1)[:, None, None]) & col_ok[None]
    col_sel = np.zeros((2,) + col_hit.shape[:2] + (2 * GRID_W,), np.float32)
    for half in range(2):
        col_sel[half, :, :, half * GRID_W:(half + 1) * GRID_W] = col_hit
    hi = lax.Precision.HIGHEST
    per_col = jnp.einsum("hde,lecn->hldcn", rpb.astype(F32), col_sel, precision=hi)
    table = jnp.einsum("uld,hldcn->hucn", row_sel, per_col, precision=hi)
    valid = ((uniq >= 0)[:, None, :, None] & col_ok[None, :, None, :]).reshape(len(uniq), GRID_W, 2 * GRID_W)
    return (jnp.where(valid[None], table, NEG_INF), tuple(int(p) for p in pair0),
            tuple(tuple(int(u) for u in row) for row in index))


def _rope_tables(length):
    t = jnp.arange(length)
    row = (t // GRID_W).astype(F32)
    col = (t % GRID_W).astype(F32)
    n_freq = HEAD_DIM // 4
    inv = ROPE_BASE ** (-jnp.arange(n_freq, dtype=F32) / n_freq)
    ar = row[:, None] * inv
    ac = col[:, None] * inv
    ang = jnp.concatenate([ar, ar, ac, ac], axis=-1)
    cos, sin = jnp.cos(ang), jnp.sin(ang)
    lane = jnp.arange(HEAD_DIM)
    takes_left = ((lane // n_freq) % 2 == 0)[None, :]
    return cos, jnp.where(takes_left, -sin, 0.0), jnp.where(takes_left, 0.0, sin)


MERGE_TM = 512


def _resident(shape):
    zeros = (0,) * len(shape)
    return pl.BlockSpec(shape, lambda i: zeros, pipeline_mode=pl.Buffered(1))


def _mix_kernel(oa_ref, ob_ref, ga_ref, gb_ref, wpa_ref, wpb_ref, mix_ref):
    ya = _dot(oa_ref[...], wpa_ref[...])
    yb = _dot(ob_ref[...], wpb_ref[...])
    mix = (jax.nn.sigmoid(ga_ref[...].astype(F32)) * ya
           + jax.nn.sigmoid(gb_ref[...].astype(F32)) * yb)
    mix_ref[...] = mix.astype(mix_ref.dtype)


def _mix_call(oa, ob, gates, wpa_bf, wpb_bf):
    m = oa.shape[0]
    d = wpa_bf.shape[1]
    row = lambda i: (i, 0)
    return pl.pallas_call(
        _mix_kernel,
        grid=(m // MERGE_TM,),
        in_specs=[pl.BlockSpec((MERGE_TM, NA_WIDTH), row),
                  pl.BlockSpec((MERGE_TM, NB_WIDTH), row),
                  pl.BlockSpec((MERGE_TM, d), lambda i: (i, 0)),
                  pl.BlockSpec((MERGE_TM, d), lambda i: (i, 1)),
                  _resident((NA_WIDTH, d)), _resident((NB_WIDTH, d))],
        out_specs=pl.BlockSpec((MERGE_TM, d), row),
        out_shape=jax.ShapeDtypeStruct((m, d), BF16),
        compiler_params=_cparams(("arbitrary",)),
        name="mix",
    )(oa, ob, gates, gates, wpa_bf, wpb_bf)


def _outproj_kernel(mix_ref, wout_ref, x_ref, g1_ref, sc2_ref, sh2_ref, n2w_ref, wr_ref,
                    x1_ref, h2_ref, lg_ref, acc_ref):
    i = pl.program_id(0)
    last = pl.num_programs(0) - 1

    def finish(prod):
        x1 = x_ref[...] + g1_ref[0] * prod
        x1_ref[...] = x1
        h2 = _rms(x1, n2w_ref[...]) * (1.0 + sc2_ref[0]) + sh2_ref[0]
        h2_ref[...] = _pack_halves(h2)
        lg_ref[...] = _dot_split(h2, wr_ref[...])

    @pl.when(i == 0)
    def _():
        acc_ref[...] = _dot(mix_ref[...], wout_ref[...])

    @pl.when((i > 0) & (i < last))
    def _():
        prod = acc_ref[...]
        acc_ref[...] = _dot(mix_ref[...], wout_ref[...])
        finish(prod)

    @pl.when(i == last)
    def _():
        finish(acc_ref[...])


def _outproj_call(mix, wout_bf, x, g1, sc2, sh2, n2w, wr, tiles_per_group):
    m, d = x.shape
    n_tiles = m // MERGE_TM
    ahead = lambda i: (jnp.minimum(i, n_tiles - 1), 0)
    done = lambda i: (jnp.maximum(i - 1, 0), 0)
    grp = lambda i: (jnp.maximum(i - 1, 0) // tiles_per_group, 0, 0)
    return pl.pallas_call(
        _outproj_kernel,
        grid=(n_tiles + 1,),
        in_specs=[pl.BlockSpec((MERGE_TM, d), ahead),
                  _resident((d, d)),
                  pl.BlockSpec((MERGE_TM, d), done),
                  pl.BlockSpec((1, 1, d), grp), pl.BlockSpec((1, 1, d), grp), pl.BlockSpec((1, 1, d), grp),
                  pl.BlockSpec((1, d), lambda i: (0, 0)),
                  _resident((d, LOGIT_PAD))],
        out_specs=[pl.BlockSpec((MERGE_TM, d), done),
                   pl.BlockSpec((MERGE_TM, d // 2), done),
                   pl.BlockSpec((MERGE_TM, LOGIT_PAD), done)],
        out_shape=[jax.ShapeDtypeStruct((m, d), F32),
                   jax.ShapeDtypeStruct((m, d // 2), jnp.uint32),
                   jax.ShapeDtypeStruct((m, LOGIT_PAD), F32)],
        scratch_shapes=[pltpu.VMEM((MERGE_TM, d), F32)],
        compiler_params=_cparams(("arbitrary",)),
        name="outproj",
    )(mix, wout_bf, x, g1, sc2, sh2, n2w, wr)


ROUTE_TM = 512


def _first_index_of_max(vals, idx, n):
    mx = jnp.max(vals, axis=0, keepdims=True)
    first = jnp.min(jnp.where(vals == mx, idx, n), axis=0, keepdims=True)
    return mx, first


def _route_kernel(lg_ref, bias_ref, eid_ref, gw_ref, rank_ref, cnt_ref, base_ref):
    step = pl.program_id(0)

    @pl.when(step == 0)
    def _():
        base_ref[...] = jnp.zeros_like(base_ref)

    lt = lg_ref[...].T + bias_ref[...]
    n_tok = lt.shape[1]
    le = lt[0:N_EXPERTS]
    lgrp = lt[N_EXPERTS:N_EXPERTS + N_GROUPS]
    gi = lax.broadcasted_iota(jnp.int32, (N_GROUPS, n_tok), 0)
    gmax, gsel = _first_index_of_max(lgrp, gi, N_GROUPS)
    pg_sel = 1.0 / jnp.sum(jnp.exp(lgrp - gmax), axis=0, keepdims=True)
    le_sel = jnp.zeros((EXPERTS_PER_GROUP, n_tok), F32)
    for g in range(N_GROUPS):
        le_sel = jnp.where(gsel == g, le[g * EXPERTS_PER_GROUP:(g + 1) * EXPERTS_PER_GROUP], le_sel)
    ei = lax.broadcasted_iota(jnp.int32, (EXPERTS_PER_GROUP, n_tok), 0)
    v0, i0 = _first_index_of_max(le_sel, ei, EXPERTS_PER_GROUP)
    rest = jnp.where(ei == i0, -jnp.inf, le_sel)
    v1, i1 = _first_index_of_max(rest, ei, EXPERTS_PER_GROUP)
    e1 = jnp.exp(v1 - v0)
    w0 = pg_sel / (1.0 + e1)
    w1 = pg_sel * e1 / (1.0 + e1)
    eid0 = gsel * EXPERTS_PER_GROUP + i0
    eid1 = gsel * EXPERTS_PER_GROUP + i1

    xi = lax.broadcasted_iota(jnp.int32, (N_EXPERTS, n_tok), 0)
    si = lax.broadcasted_iota(jnp.int32, (n_tok, n_tok), 0)
    ti = lax.broadcasted_iota(jnp.int32, (n_tok, n_tok), 1)
    before = (si < ti).astype(BF16)
    base = base_ref[...]
    hot0 = (xi == eid0).astype(F32)
    hot1 = (xi == eid1).astype(F32)
    pre0 = _dot(hot0.astype(BF16), before)
    pre1 = _dot(hot1.astype(BF16), before)
    tot0 = jnp.sum(hot0, axis=1, keepdims=True)
    tot1 = jnp.sum(hot1, axis=1, keepdims=True)
    rank0 = jnp.sum(hot0 * (base + pre0), axis=0, keepdims=True)
    rank1 = jnp.sum(hot1 * (base + tot0 + pre1), axis=0, keepdims=True)
    base = base + tot0 + tot1
    base_ref[...] = base

    ri = lax.broadcasted_iota(jnp.int32, (8, n_tok), 0)
    pick = lambda a, b: jnp.where(ri == 0, a, jnp.where(ri == 1, b, jnp.zeros_like(a)))
    eid_ref[...] = pick(eid0, eid1)
    gw_ref[...] = pick(w0, w1)
    rank_ref[...] = pick(rank0, rank1).astype(jnp.int32)
    cnt_ref[...] = jnp.broadcast_to(base, cnt_ref.shape).astype(jnp.int32)


def _route_call(logits, bias_col):
    t = logits.shape[0]
    tok = pl.BlockSpec((8, ROUTE_TM), lambda i: (0, i))
    return pl.pallas_call(
        _route_kernel,
        grid=(t // ROUTE_TM,),
        in_specs=[pl.BlockSpec((ROUTE_TM, LOGIT_PAD), lambda i: (i, 0)),
                  pl.BlockSpec((LOGIT_PAD, 1), lambda i: (0, 0))],
        out_specs=[tok, tok, tok, pl.BlockSpec((N_EXPERTS, 128), lambda i: (0, 0))],
        out_shape=[jax.ShapeDtypeStruct((8, t), jnp.int32),
                   jax.ShapeDtypeStruct((8, t), F32),
                   jax.ShapeDtypeStruct((8, t), jnp.int32),
                   jax.ShapeDtypeStruct((N_EXPERTS, 128), jnp.int32)],
        scratch_shapes=[pltpu.VMEM((N_EXPERTS, 1), F32)],
        compiler_params=_cparams(("arbitrary",)),
        name="route",
    )(logits, bias_col)


def _moe_layout(n_pairs):
    padded_rows = -(-(n_pairs + N_EXPERTS * (MOE_ROW_BLOCK - 1)) // MOE_ROW_BLOCK) * MOE_ROW_BLOCK
    n_items = (padded_rows + N_EXPERTS * (MOE_ROW_GROUP - MOE_ROW_BLOCK)) // MOE_ROW_GROUP
    return padded_rows, n_items


def _routing_tables(eid, rank, counts, n_items):
    padded = (counts + MOE_ROW_BLOCK - 1) // MOE_ROW_BLOCK * MOE_ROW_BLOCK
    pad_end = jnp.cumsum(padded)
    pad_start = pad_end - padded
    hot = eid[..., None] == jnp.arange(N_EXPERTS, dtype=jnp.int32)
    dest = (jnp.sum(jnp.where(hot, pad_start, 0), axis=-1) + rank).astype(jnp.int32)
    tail = jnp.where(padded > counts, pad_end - MOE_ROW_BLOCK, -1).astype(jnp.int32)
    per_expert = (padded + MOE_ROW_GROUP - 1) // MOE_ROW_GROUP
    item_end = jnp.cumsum(per_expert)
    item_start = item_end - per_expert
    total = item_end[-1]
    ii = jnp.arange(n_items, dtype=jnp.int32)
    e_of = jnp.minimum(jnp.searchsorted(item_end, ii, side="right"), N_EXPERTS - 1).astype(jnp.int32)
    valid = ii < total
    e_last = e_of[jnp.maximum(total - 1, 0)]
    local = ii - item_start[e_of]
    row0 = pad_start[e_of] + local * MOE_ROW_GROUP
    nblk = jnp.clip((padded[e_of] - local * MOE_ROW_GROUP) // MOE_ROW_BLOCK, 0, MOE_ROW_GROUP // MOE_ROW_BLOCK)
    item_e = jnp.where(valid, e_of, e_last).astype(jnp.int32)
    item_row0 = jnp.where(valid, row0, 0).astype(jnp.int32)
    item_nblk = jnp.where(valid, nblk, 0).astype(jnp.int32)
    return dest, tail, item_e, item_row0, item_nblk


DISPATCH_TOKENS = 256
ROW_DMA_GROUP = 8


def _row_copy(src, s, dst, d, sem):
    return pltpu.make_async_copy(src.at[pl.ds(s, 1), :], dst.at[pl.ds(d, 1), :], sem)


def _dispatch_kernel(dest_ref, tail_ref, h_ctx, h_lat, xs, zero_buf, sem, *, n_ctx, n_tok):
    step = pl.program_id(0)
    tail_copy = lambda e: pltpu.make_async_copy(
        zero_buf, xs.at[pl.ds(pl.multiple_of(tail_ref[e], MOE_ROW_BLOCK), MOE_ROW_BLOCK), :], sem.at[1])

    @pl.when(step == 0)
    def _():
        zero_buf[...] = jnp.zeros_like(zero_buf)
        for e in range(N_EXPERTS):
            @pl.when(tail_ref[e] >= 0)
            def _():
                tail_copy(e).start()
        for e in range(N_EXPERTS):
            @pl.when(tail_ref[e] >= 0)
            def _():
                tail_copy(e).wait()

    tok0 = step * DISPATCH_TOKENS

    def scatter(src):
        def issue(g, carry):
            base = pl.multiple_of(g * ROW_DMA_GROUP, ROW_DMA_GROUP)
            for j in range(ROW_DMA_GROUP):
                _row_copy(src, base + j, xs, dest_ref[tok0 + base + j], sem.at[0]).start(priority=0)
                _row_copy(src, base + j, xs, dest_ref[n_tok + tok0 + base + j], sem.at[0]).start(priority=1)
            return carry

        def drain(g, carry):
            for j in range(2 * ROW_DMA_GROUP):
                _row_copy(src, 0, xs, 0, sem.at[0]).wait()
            return carry

        lax.fori_loop(0, DISPATCH_TOKENS // ROW_DMA_GROUP, issue, 0)
        lax.fori_loop(0, DISPATCH_TOKENS // ROW_DMA_GROUP, drain, 0)

    @pl.when(tok0 < n_ctx)
    def _():
        scatter(h_ctx)

    @pl.when(tok0 >= n_ctx)
    def _():
        scatter(h_lat)


def _dispatch_call(dest_flat, tail, h_ctx, h_lat, padded_rows):
    n_ctx, d = h_ctx.shape
    n_tok = n_ctx + h_lat.shape[0]
    ctx_tiles = n_ctx // DISPATCH_TOKENS
    return pl.pallas_call(
        functools.partial(_dispatch_kernel, n_ctx=n_ctx, n_tok=n_tok),
        grid_spec=pltpu.PrefetchScalarGridSpec(
            num_scalar_prefetch=2,
            grid=(n_tok // DISPATCH_TOKENS,),
            in_specs=[pl.BlockSpec((DISPATCH_TOKENS, d), lambda i, dr, tr: (jnp.minimum(i, ctx_tiles - 1), 0)),
                      pl.BlockSpec((DISPATCH_TOKENS, d), lambda i, dr, tr: (jnp.maximum(i - ctx_tiles, 0), 0))],
            out_specs=pl.BlockSpec(memory_space=pl.ANY),
            scratch_shapes=[pltpu.VMEM((MOE_ROW_BLOCK, d), h_ctx.dtype), pltpu.SemaphoreType.DMA((2,))]),
        out_shape=jax.ShapeDtypeStruct((padded_rows, d), h_ctx.dtype),
        compiler_params=_cparams(("arbitrary",)),
        name="dispatch",
    )(dest_flat, tail, h_ctx, h_lat)


def _moe_kernel(item_e, item_row0, item_nblk, xs, w1_ref, w3_ref, w2_ref, ys,
                x_in, x_bf, acc, y_out, w1_bf, w3_bf, w2_bf, sem):
    i = pl.program_id(0)
    c = pl.program_id(1)
    n_items = pl.num_programs(0)
    last_c = pl.num_programs(1) - 1
    nblk = item_nblk[i]
    max_blk = MOE_ROW_GROUP // MOE_ROW_BLOCK
    blk = lambda b: pl.ds(b * MOE_ROW_BLOCK, MOE_ROW_BLOCK)

    def rows_of(item, b):
        return pl.ds(pl.multiple_of(item_row0[item], MOE_ROW_BLOCK) + b * MOE_ROW_BLOCK, MOE_ROW_BLOCK)

    load = lambda item, b: pltpu.make_async_copy(xs.at[rows_of(item, b), :], x_in.at[blk(b), :], sem.at[0])
    store = lambda item, b: pltpu.make_async_copy(y_out.at[blk(b), :], ys.at[rows_of(item, b), :], sem.at[1])

    def for_blocks(item, fn):
        n = item_nblk[item]
        for b in range(max_blk):
            @pl.when(b < n)
            def _():
                fn(item, b)

    @pl.when(c == 0)
    def _():
        @pl.when(i == 0)
        def _():
            for_blocks(0, lambda it, b: load(it, b).start())

        for_blocks(i, lambda it, b: load(it, b).wait())

        def cast(it, b):
            x_bf[blk(b), :] = _unpack_halves(x_in[blk(b), :]).astype(BF16)

        for_blocks(i, cast)

        @pl.when(i + 1 < n_items)
        def _():
            for_blocks(i + 1, lambda it, b: load(it, b).start())

        def clear(it, b):
            acc[blk(b), :] = jnp.zeros((MOE_ROW_BLOCK, acc.shape[1]), F32)

        for_blocks(i, clear)

    @pl.when(nblk > 0)
    def _():
        def block(rows, w1, w3, w2):
            x = x_bf[rows, :]
            h1 = _dot(x, w1)
            h3 = _dot(x, w3)
            a = (h1 * jax.nn.sigmoid(h1) * h3).astype(BF16)
            acc[rows, :] += _dot(a, w2)

        w1 = w1_ref[0].astype(BF16)
        w3 = w3_ref[0].astype(BF16)
        w2 = w2_ref[0].astype(BF16)
        w1_bf[...] = w1
        w3_bf[...] = w3
        w2_bf[...] = w2
        block(blk(0), w1, w3, w2)

        def body(b, carry):
            rows = pl.ds(pl.multiple_of(b * MOE_ROW_BLOCK, MOE_ROW_BLOCK), MOE_ROW_BLOCK)
            block(rows, w1_bf[...], w3_bf[...], w2_bf[...])
            return carry

        lax.fori_loop(1, nblk, body, 0)

    @pl.when(c == last_c)
    def _():
        @pl.when(i > 0)
        def _():
            for_blocks(i - 1, lambda it, b: store(it, b).wait())

        def pack(it, b):
            y_out[blk(b), :] = _pack_halves(acc[blk(b), :])

        for_blocks(i, pack)
        for_blocks(i, lambda it, b: store(it, b).start())

        @pl.when(i == n_items - 1)
        def _():
            for_blocks(i, lambda it, b: store(it, b).wait())


def _moe_call(item_e, item_row0, item_nblk, xs, w1, w3, w2):
    padded_rows, d_packed = xs.shape
    d = w1.shape[1]
    n_items = item_e.shape[0]
    f = w1.shape[2]
    nc = f // MOE_F_CHUNK
    chunk = lambda i, c, ib: jnp.where(ib[i] > 0, c, nc - 1)
    return pl.pallas_call(
        _moe_kernel,
        grid_spec=pltpu.PrefetchScalarGridSpec(
            num_scalar_prefetch=3,
            grid=(n_items, nc),
            in_specs=[pl.BlockSpec(memory_space=pl.ANY),
                      pl.BlockSpec((1, d, MOE_F_CHUNK), lambda i, c, ie, ir, ib: (ie[i], 0, chunk(i, c, ib))),
                      pl.BlockSpec((1, d, MOE_F_CHUNK), lambda i, c, ie, ir, ib: (ie[i], 0, chunk(i, c, ib))),
                      pl.BlockSpec((1, MOE_F_CHUNK, d), lambda i, c, ie, ir, ib: (ie[i], chunk(i, c, ib), 0))],
            out_specs=pl.BlockSpec(memory_space=pl.ANY),
            scratch_shapes=[pltpu.VMEM((MOE_ROW_GROUP, d_packed), xs.dtype),
                            pltpu.VMEM((MOE_ROW_GROUP, d), BF16),
                            pltpu.VMEM((MOE_ROW_GROUP, d), F32),
                            pltpu.VMEM((MOE_ROW_GROUP, d_packed), xs.dtype),
                            pltpu.VMEM((d, MOE_F_CHUNK), BF16),
                            pltpu.VMEM((d, MOE_F_CHUNK), BF16),
                            pltpu.VMEM((MOE_F_CHUNK, d), BF16),
                            pltpu.SemaphoreType.DMA((2,))]),
        out_shape=jax.ShapeDtypeStruct((padded_rows, d_packed), xs.dtype),
        compiler_params=_cparams(("arbitrary", "arbitrary")),
        name="moe",
    )(item_e, item_row0, item_nblk, xs, w1, w3, w2)


COMBINE_TM = 256


def _combine_kernel(dest_ref, x1_ref, gw_ref, g2_ref, ys, o_ref, y0, y1, sem, *, tok_base, n_tok):
    tok0 = tok_base + pl.program_id(0) * COMBINE_TM

    def issue(g, carry):
        base = pl.multiple_of(g * ROW_DMA_GROUP, ROW_DMA_GROUP)
        for j in range(ROW_DMA_GROUP):
            _row_copy(ys, dest_ref[tok0 + base + j], y0, base + j, sem).start(priority=0)
            _row_copy(ys, dest_ref[n_tok + tok0 + base + j], y1, base + j, sem).start(priority=1)
        return carry

    def drain(g, carry):
        for j in range(ROW_DMA_GROUP):
            _row_copy(ys, 0, y0, 0, sem).wait()
            _row_copy(ys, 0, y1, 0, sem).wait()
        return carry

    lax.fori_loop(0, COMBINE_TM // ROW_DMA_GROUP, issue, 0)
    lax.fori_loop(0, COMBINE_TM // ROW_DMA_GROUP, drain, 0)
    gw = gw_ref[...]
    moe = gw[:, 0:1] * _unpack_halves(y0[...]) + gw[:, 1:2] * _unpack_halves(y1[...])
    o_ref[...] = x1_ref[...] + g2_ref[0] * moe


def _combine_call(dest_flat, x1, gw, g2, ys, tok_base, n_tok, tiles_per_group):
    m, d = x1.shape
    return pl.pallas_call(
        functools.partial(_combine_kernel, tok_base=tok_base, n_tok=n_tok),
        grid_spec=pltpu.PrefetchScalarGridSpec(
            num_scalar_prefetch=1,
            grid=(m // COMBINE_TM,),
            in_specs=[pl.BlockSpec((COMBINE_TM, d), lambda i, dr: (i, 0)),
                      pl.BlockSpec((COMBINE_TM, 2), lambda i, dr: (i, 0)),
                      pl.BlockSpec((1, 1, d), lambda i, dr: (i // tiles_per_group, 0, 0)),
                      pl.BlockSpec(memory_space=pl.ANY)],
            out_specs=pl.BlockSpec((COMBINE_TM, d), lambda i, dr: (i, 0)),
            scratch_shapes=[pltpu.VMEM((COMBINE_TM,) + ys.shape[1:], ys.dtype),
                            pltpu.VMEM((COMBINE_TM,) + ys.shape[1:], ys.dtype),
                            pltpu.SemaphoreType.DMA]),
        out_shape=jax.ShapeDtypeStruct((m, d), F32),
        compiler_params=_cparams(("arbitrary",)),
        name="combine",
    )(dest_flat, x1, gw, g2, ys)


def kernel(x_prompt, x_sample, cache_a_k, cache_a_v, cache_b_k, cache_b_v, c, c_ctx, norm1_w, norm2_w, w_ada, b_ada, w_in, qn_a, kn_a, qn_b, kn_b, rpb_a, sink_b, w_pa, w_pb, w_out, w_rg, b_rg, w_re, b_re, w1, w3, w2):
    batch, seq, d = x_prompt.shape
    dec_batch, dec_seq, _ = x_sample.shape
    depth = norm1_w.shape[0]
    assert depth == 1, "one trunk layer"
    past = cache_a_k.shape[2]
    n_ctx, n_lat = batch * seq, dec_batch * dec_seq
    n_tok = n_ctx + n_lat

    xc = x_prompt.reshape(n_ctx, d)
    xl = x_sample.reshape(n_lat, d)

    cond = jnp.concatenate([c_ctx[None, :], c], axis=0)
    mod = _ada_call(cond, w_ada[0], b_ada[0][None, :])
    sh1, sc1, g1, sh2, sc2, g2 = [mod[:, i * d:(i + 1) * d][:, None, :] for i in range(6)]
    ctx_rows, lat_rows = slice(0, 1), slice(1, 1 + dec_batch)

    nw1, nw2 = norm1_w[0][None, :], norm2_w[0][None, :]
    qna, kna, qnb, knb = qn_a[0][None, :], kn_a[0][None, :], qn_b[0][None, :], kn_b[0][None, :]
    sink = sink_b[0]
    wpa_bf, wpb_bf, wout_bf = w_pa[0].astype(BF16), w_pb[0].astype(BF16), w_out[0].astype(BF16)
    wr = jnp.zeros((d, LOGIT_PAD), F32).at[:, :N_EXPERTS].set(w_re[0]).at[:, N_EXPERTS:N_EXPERTS + N_GROUPS].set(w_rg[0])
    br = jnp.zeros((LOGIT_PAD, 1), F32).at[:N_EXPERTS, 0].set(b_re[0]).at[N_EXPERTS:N_EXPERTS + N_GROUPS, 0].set(b_rg[0])

    proj_c, gates_c = _inproj_call(xc, nw1, sc1[ctx_rows], sh1[ctx_rows], w_in[0], n_ctx)
    proj_l, gates_l = _inproj_call(xl, nw1, sc1[lat_rows], sh1[lat_rows], w_in[0], dec_seq)

    oa_c, ob_c, new_a_k, new_a_v, new_b_k, new_b_v = _ctx_attn_call(proj_c, seq, qna, kna, qnb, knb, sink)

    bias_blocks, pair0, bias_index = _na_bias_blocks(rpb_a[0], dec_seq)
    cos, sin_a, sin_b = _rope_tables(dec_seq)
    rows_of = lambda cache: cache.reshape(-1, HEAD_DIM)
    oa_l = _lat_attn_a_call(proj_l, dec_seq, past, rows_of(cache_a_k), rows_of(cache_a_v),
                            bias_blocks, pair0, bias_index, qna, kna)
    ob_l = _lat_attn_b_call(proj_l, dec_seq, past, rows_of(cache_b_k), rows_of(cache_b_v),
                            cos, sin_a, sin_b, qnb, knb, sink)

    mix_c = _mix_call(oa_c, ob_c, gates_c, wpa_bf, wpb_bf)
    mix_l = _mix_call(oa_l, ob_l, gates_l, wpa_bf, wpb_bf)
    x1_c, h2_c, lg_c = _outproj_call(mix_c, wout_bf, xc, g1[ctx_rows], sc2[ctx_rows], sh2[ctx_rows],
                                     nw2, wr, n_ctx // MERGE_TM)
    x1_l, h2_l, lg_l = _outproj_call(mix_l, wout_bf, xl, g1[lat_rows], sc2[lat_rows], sh2[lat_rows],
                                     nw2, wr, dec_seq // MERGE_TM)

    eid, gw, rank, cnt = _route_call(jnp.concatenate([lg_c, lg_l], axis=0), br)
    padded_rows, n_items = _moe_layout(2 * n_tok)
    dest, tail, item_e, item_row0, item_nblk = _routing_tables(eid[:2], rank[:2], cnt[:, 0], n_items)
    dest_flat = dest.reshape(-1)
    xs = _dispatch_call(dest_flat, tail, h2_c, h2_l, padded_rows)
    ys = _moe_call(item_e, item_row0, item_nblk, xs, w1[0], w3[0], w2[0])
    gw_t = gw[:2].T
    y_c = _combine_call(dest_flat, x1_c, gw_t[:n_ctx], g2[ctx_rows], ys, 0, n_tok, n_ctx // COMBINE_TM)
    y_l = _combine_call(dest_flat, x1_l, gw_t[n_ctx:], g2[lat_rows], ys, n_ctx, n_tok, dec_seq // COMBINE_TM)

    state = lambda a, heads: a.reshape(batch, 1, seq, heads, HEAD_DIM)
    return (y_c.reshape(batch, seq, d), y_l.reshape(dec_batch, dec_seq, d),
            state(new_a_k, NA_HEADS), state(new_a_v, NA_HEADS),
            state(new_b_k, NB_KV_HEADS), state(new_b_v, NB_KV_HEADS))
```

```python
import functools

import jax
import jax.numpy as jnp
import numpy as np
from jax import lax
from jax.experimental import pallas as pl
from jax.experimental.pallas import tpu as pltpu

D_MODEL = 2048
HEAD_DIM = 128
NA_HEADS = 8
NA_WIDTH = NA_HEADS * HEAD_DIM
NB_Q_HEADS = 8
NB_KV_HEADS = 2
NB_GROUP = NB_Q_HEADS // NB_KV_HEADS
NB_WIDTH = NB_Q_HEADS * HEAD_DIM
NB_KV_WIDTH = NB_KV_HEADS * HEAD_DIM
GRID_W = 64
NA_WIN_ROWS = 8
NA_WIN_COLS = 16
WINDOW = 128
N_GROUPS = 4
EXPERTS_PER_GROUP = 8
N_EXPERTS = N_GROUPS * EXPERTS_PER_GROUP
D_EXPERT = 1024
IN_WIDTH = 3 * NA_WIDTH + NB_WIDTH + 2 * NB_KV_WIDTH + 2 * D_MODEL
ROPE_BASE = 10000.0
NORM_EPS = 1e-6
NEG_INF = -1e30
ATTN_SCALE = HEAD_DIM ** -0.5

QA_HEAD0 = 0
KA_HEAD0 = NA_HEADS
VA_HEAD0 = 2 * NA_HEADS
QB_HEAD0 = 3 * NA_HEADS
KB_HEAD0 = QB_HEAD0 + NB_Q_HEADS
VB_HEAD0 = KB_HEAD0 + NB_KV_HEADS
GATE_COL0 = (VB_HEAD0 + NB_KV_HEADS) * HEAD_DIM

LOGIT_PAD = 128
MOE_ROW_BLOCK = 256
MOE_ROW_GROUP = 1024
MOE_F_CHUNK = 512
VMEM_LIMIT = 56 * 1024 * 1024

F32 = jnp.float32
BF16 = jnp.bfloat16


def _cparams(sem, vmem_limit=VMEM_LIMIT):
    return pltpu.CompilerParams(dimension_semantics=sem, vmem_limit_bytes=vmem_limit)


def _rms(x, w):
    x = x.astype(F32)
    return x * lax.rsqrt(jnp.mean(x * x, axis=-1, keepdims=True) + NORM_EPS) * w


def _dot(a, b):
    return jnp.dot(a, b, preferred_element_type=F32)


def _dot_nt(a, b):
    return lax.dot_general(a, b, (((1,), (1,)), ((), ())), preferred_element_type=F32)


def _pack_halves(x):
    n = x.shape[1] // 2
    lo = lax.bitcast_convert_type(x[:, :n].astype(BF16).astype(F32), jnp.uint32)
    hi = lax.bitcast_convert_type(x[:, n:].astype(BF16).astype(F32), jnp.uint32)
    return hi | (lo >> 16)


def _unpack_halves(w):
    lo = lax.bitcast_convert_type(w << 16, F32)
    hi = lax.bitcast_convert_type(w & jnp.uint32(0xFFFF0000), F32)
    return jnp.concatenate([lo, hi], axis=1)


def _dot_split(a, b):
    a_hi = a.astype(BF16)
    a_lo = (a - a_hi.astype(F32)).astype(BF16)
    b_hi = b.astype(BF16)
    b_lo = (b - b_hi.astype(F32)).astype(BF16)
    return _dot(a_hi, b_hi) + (_dot(a_lo, b_hi) + _dot(a_hi, b_lo))


ADA_ROWS = 8
ADA_TN = 1024


def _ada_kernel(c_ref, w_ref, b_ref, o_ref):
    n_rows, d, lanes = c_ref.shape
    tn = w_ref.shape[1]

    def body(kb, acc):
        ks = pl.ds(pl.multiple_of(kb * 8, 8), 8)
        w = w_ref[ks, :]
        out = []
        for r in range(n_rows):
            c = c_ref[r, ks, :]
            s = c * jax.nn.sigmoid(c)
            out.append(acc[r] + w * jnp.concatenate([s] * (tn // lanes), axis=1))
        return tuple(out)

    acc = lax.fori_loop(0, d // 8, body, tuple(jnp.zeros((8, tn), F32) for _ in range(n_rows)), unroll=8)
    ri = lax.broadcasted_iota(jnp.int32, (ADA_ROWS, tn), 0)
    res = jnp.zeros((ADA_ROWS, tn), F32)
    for r in range(n_rows):
        row = jnp.sum(acc[r], axis=0, keepdims=True) + b_ref[...]
        res = jnp.where(ri == r, row, res)
    o_ref[...] = res


def _ada_call(cond, w_ada, b_ada):
    n_rows, d = cond.shape
    n = w_ada.shape[1]
    lanes = 128
    cond_lanes = jnp.broadcast_to(cond[:, :, None], (n_rows, d, lanes))
    return pl.pallas_call(
        _ada_kernel,
        grid=(n // ADA_TN,),
        in_specs=[pl.BlockSpec((n_rows, d, lanes), lambda j: (0, 0, 0)),
                  pl.BlockSpec((d, ADA_TN), lambda j: (0, j)),
                  pl.BlockSpec((1, ADA_TN), lambda j: (0, j))],
        out_specs=pl.BlockSpec((ADA_ROWS, ADA_TN), lambda j: (0, j)),
        out_shape=jax.ShapeDtypeStruct((ADA_ROWS, n), F32),
        compiler_params=_cparams(("arbitrary",)),
        name="ada",
    )(cond_lanes, w_ada, b_ada)


INPROJ_TM = 2048
INPROJ_TN = 512
NORM_ROWS = 128


def _inproj_kernel(x_hbm, nw_ref, sc_ref, sh_ref, w_ref, qkv_ref, gate_ref, x_buf, h_scr, sem, *, qkv_tiles):
    i = pl.program_id(0)
    j = pl.program_id(1)
    tm = x_buf.shape[0]
    fetch = lambda tile: pltpu.make_async_copy(
        x_hbm.at[pl.ds(pl.multiple_of(tile * tm, tm), tm), :], x_buf, sem)

    @pl.when(j == 0)
    def _():
        @pl.when(i == 0)
        def _():
            fetch(0).start()

        fetch(i).wait()
        nw = nw_ref[...]
        sc = 1.0 + sc_ref[0]
        sh = sh_ref[0]

        def body(r, carry):
            rows = pl.ds(pl.multiple_of(r * NORM_ROWS, NORM_ROWS), NORM_ROWS)
            h_scr[rows, :] = (_rms(x_buf[rows, :], nw) * sc + sh).astype(BF16)
            return carry

        lax.fori_loop(0, tm // NORM_ROWS, body, 0)

    @pl.when((j == 1) & (i + 1 < pl.num_programs(0)))
    def _():
        fetch(i + 1).start()

    res = _dot(h_scr[...], w_ref[...].astype(BF16))

    @pl.when(j < qkv_tiles)
    def _():
        qkv_ref[...] = res.astype(qkv_ref.dtype)

    @pl.when(j >= qkv_tiles)
    def _():
        gate_ref[...] = res.astype(gate_ref.dtype)


def _inproj_call(x, nw, sc, sh, w_in, rows_per_group):
    m, d = x.shape
    n = w_in.shape[1]
    assert n // INPROJ_TN >= 2, "the next token tile is requested in the second column step"
    tm = min(INPROJ_TM, rows_per_group)
    tiles_per_group = rows_per_group // tm
    qkv_tiles = GATE_COL0 // INPROJ_TN
    grp = lambda i, j: (i // tiles_per_group, 0, 0)
    return pl.pallas_call(
        functools.partial(_inproj_kernel, qkv_tiles=qkv_tiles),
        grid=(m // tm, n // INPROJ_TN),
        in_specs=[pl.BlockSpec(memory_space=pl.ANY),
                  pl.BlockSpec((1, d), lambda i, j: (0, 0)),
                  pl.BlockSpec((1, 1, d), grp),
                  pl.BlockSpec((1, 1, d), grp),
                  pl.BlockSpec((d, INPROJ_TN), lambda i, j: (0, j))],
        out_specs=[pl.BlockSpec((tm, INPROJ_TN), lambda i, j: (i, jnp.minimum(j, qkv_tiles - 1))),
                   pl.BlockSpec((tm, INPROJ_TN), lambda i, j: (i, jnp.maximum(j - qkv_tiles, 0)))],
        out_shape=[jax.ShapeDtypeStruct((m, GATE_COL0), BF16),
                   jax.ShapeDtypeStruct((m, n - GATE_COL0), BF16)],
        scratch_shapes=[pltpu.VMEM((tm, d), F32), pltpu.VMEM((tm, d), BF16), pltpu.SemaphoreType.DMA],
        compiler_params=_cparams(("arbitrary", "arbitrary")),
        name="inproj",
    )(x, nw, sc, sh, w_in)


def _with_ones(v):
    return jnp.concatenate([v.astype(BF16), jnp.ones(v.shape, BF16)], axis=1)


def _softmax_pv(scores, values_with_ones, sink=None):
    m = None
    for s in scores:
        ms = jnp.max(s, axis=-1, keepdims=True)
        m = ms if m is None else jnp.maximum(m, ms)
    if sink is not None:
        m = jnp.maximum(m, sink)
    acc = None
    for s, v1 in zip(scores, values_with_ones):
        pv = _dot(jnp.exp(s - m).astype(BF16), v1)
        acc = pv if acc is None else acc + pv
    d = acc.shape[1] // 2
    den = acc[:, d:]
    if sink is not None:
        den = den + jnp.exp(sink - m)
    return acc[:, :d] / den


def _ctx_attn_kernel(qa_ref, ka_ref, va_ref, qb_ref, kb_ref, vb_ref,
                     qna_ref, kna_ref, qnb_ref, knb_ref, sink_ref,
                     oa_ref, ob_ref, nak_ref, nav_ref, nbk_ref, nbv_ref):
    qna, kna, qnb, knb = qna_ref[...], kna_ref[...], qnb_ref[...], knb_ref[...]
    seq = qa_ref.shape[0]
    for h in range(NA_HEADS):
        cols = slice(h * HEAD_DIM, (h + 1) * HEAD_DIM)
        q = (_rms(qa_ref[:, cols], qna) * ATTN_SCALE).astype(BF16)
        k = _rms(ka_ref[:, cols], kna)
        v = va_ref[:, cols]
        nak_ref[pl.ds(h, seq, stride=NA_HEADS), :] = k
        nav_ref[pl.ds(h, seq, stride=NA_HEADS), :] = v.astype(F32)
        s = _dot_nt(q, k.astype(BF16))
        oa_ref[:, cols] = _softmax_pv([s], [_with_ones(v)]).astype(oa_ref.dtype)
    for kv in range(NB_KV_HEADS):
        kcols = slice(kv * HEAD_DIM, (kv + 1) * HEAD_DIM)
        k = _rms(kb_ref[:, kcols], knb)
        v = vb_ref[:, kcols]
        nbk_ref[pl.ds(kv, seq, stride=NB_KV_HEADS), :] = k
        nbv_ref[pl.ds(kv, seq, stride=NB_KV_HEADS), :] = v.astype(F32)
        kb16 = k.astype(BF16)
        vb1 = _with_ones(v)
        for g in range(NB_GROUP):
            hq = kv * NB_GROUP + g
            cols = slice(hq * HEAD_DIM, (hq + 1) * HEAD_DIM)
            q = (_rms(qb_ref[:, cols], qnb) * ATTN_SCALE).astype(BF16)
            s = _dot_nt(q, kb16)
            ob_ref[:, cols] = _softmax_pv([s], [vb1], sink=sink_ref[hq]).astype(ob_ref.dtype)


def _ctx_attn_call(proj, seq, qna, kna, qnb, knb, sink):
    m = proj.shape[0]
    nb = m // seq
    wide = lambda blk: pl.BlockSpec((seq, NA_WIDTH), lambda b: (b, blk))
    narrow = lambda blk: pl.BlockSpec((seq, NB_KV_WIDTH), lambda b: (b, blk))
    vec = pl.BlockSpec((1, HEAD_DIM), lambda b: (0, 0))
    return pl.pallas_call(
        _ctx_attn_kernel,
        grid=(nb,),
        in_specs=[wide(QA_HEAD0 // NA_HEADS), wide(KA_HEAD0 // NA_HEADS), wide(VA_HEAD0 // NA_HEADS),
                  wide(QB_HEAD0 // NA_HEADS), narrow(KB_HEAD0 // NB_KV_HEADS), narrow(VB_HEAD0 // NB_KV_HEADS),
                  vec, vec, vec, vec,
                  pl.BlockSpec(memory_space=pltpu.SMEM)],
        out_specs=[pl.BlockSpec((seq, NA_WIDTH), lambda b: (b, 0)),
                   pl.BlockSpec((seq, NB_WIDTH), lambda b: (b, 0)),
                   pl.BlockSpec((seq * NA_HEADS, HEAD_DIM), lambda b: (b, 0)),
                   pl.BlockSpec((seq * NA_HEADS, HEAD_DIM), lambda b: (b, 0)),
                   pl.BlockSpec((seq * NB_KV_HEADS, HEAD_DIM), lambda b: (b, 0)),
                   pl.BlockSpec((seq * NB_KV_HEADS, HEAD_DIM), lambda b: (b, 0))],
        out_shape=[jax.ShapeDtypeStruct((m, NA_WIDTH), BF16),
                   jax.ShapeDtypeStruct((m, NB_WIDTH), BF16),
                   jax.ShapeDtypeStruct((m * NA_HEADS, HEAD_DIM), F32),
                   jax.ShapeDtypeStruct((m * NA_HEADS, HEAD_DIM), F32),
                   jax.ShapeDtypeStruct((m * NB_KV_HEADS, HEAD_DIM), F32),
                   jax.ShapeDtypeStruct((m * NB_KV_HEADS, HEAD_DIM), F32)],
        compiler_params=_cparams(("arbitrary",)),
        name="ctx_attn",
    )(proj, proj, proj, proj, proj, proj, qna, kna, qnb, knb, sink)


def _rope(x, cos, sin_a, sin_b):
    quarter = HEAD_DIM // 4
    return (x * cos + pltpu.roll(x, HEAD_DIM - quarter, 1) * sin_a
            + pltpu.roll(x, quarter, 1) * sin_b)


def _head_rows(cache_ref, head, n_heads):
    past = cache_ref.shape[0] // n_heads
    return cache_ref[pl.ds(head, past, stride=n_heads), :]


def _lat_attn_a_kernel(q_ref, k_ref, v_ref, ck_ref, cv_ref, cb_ref, qn_ref, kn_ref, o_ref, *, pair0, index):
    head = pl.program_id(0)
    q = (_rms(q_ref[...], qn_ref[...]) * ATTN_SCALE).astype(BF16)
    k = _rms(k_ref[...], kn_ref[...]).astype(BF16)
    v1 = _with_ones(v_ref[...])
    ck = _head_rows(ck_ref, head, NA_HEADS).astype(BF16)
    cv1 = _with_ones(_head_rows(cv_ref, head, NA_HEADS))
    pair = 2 * GRID_W
    rows_per_block = pair // GRID_W
    n_pairs = len(index[0])
    for blk in range(len(pair0) // rows_per_block):
        grid_rows = range(blk * rows_per_block, (blk + 1) * rows_per_block)
        p0 = pair0[grid_rows[0]]
        assert all(pair0[r] == p0 for r in grid_rows), "query rows of a block share their key pairs"
        rows = slice(blk * pair, (blk + 1) * pair)
        keys = slice(p0 * pair, (p0 + n_pairs) * pair)
        bias = jnp.concatenate(
            [jnp.concatenate([cb_ref[u] for u in index[r]], axis=1) for r in grid_rows], axis=0)
        qs = q[rows]
        s_win = _dot_nt(qs, k[keys]) + bias
        s_ctx = _dot_nt(qs, ck)
        o_ref[rows, :] = _softmax_pv([s_win, s_ctx], [v1[keys], cv1]).astype(o_ref.dtype)


def _lat_attn_b_kernel(q_ref, k_ref, v_ref, ck_ref, cv_ref, cos_ref, sina_ref, sinb_ref,
                       qn_ref, kn_ref, sink_ref, o_ref):
    cos, sin_a, sin_b = cos_ref[...], sina_ref[...], sinb_ref[...]
    q = _rope(_rms(q_ref[...], qn_ref[...]), cos, sin_a, sin_b)
    k = _rope(_rms(k_ref[...], kn_ref[...]), cos, sin_a, sin_b)
    q = (q * ATTN_SCALE).astype(BF16)
    k = k.astype(BF16)
    v1 = _with_ones(v_ref[...])
    length = q.shape[0]
    kv = pl.program_id(0) // NB_GROUP
    ck = _head_rows(ck_ref, kv, NB_KV_HEADS).astype(BF16)
    cv1 = _with_ones(_head_rows(cv_ref, kv, NB_KV_HEADS))
    sink = sink_ref[pl.program_id(0)]
    for qb in range(length // WINDOW):
        rows = slice(qb * WINDOW, (qb + 1) * WINDOW)
        lo, hi = max(0, (qb - 1) * WINDOW), min(length, (qb + 2) * WINDOW)
        qs = q[rows]
        s_win = _dot_nt(qs, k[lo:hi])
        qi = qb * WINDOW + lax.broadcasted_iota(jnp.int32, s_win.shape, 0)
        kj = lo + lax.broadcasted_iota(jnp.int32, s_win.shape, 1)
        s_win = jnp.where(jnp.abs(qi - kj) <= WINDOW, s_win, NEG_INF)
        s_ctx = _dot_nt(qs, ck)
        o_ref[rows, :] = _softmax_pv([s_win, s_ctx], [v1[lo:hi], cv1], sink=sink).astype(o_ref.dtype)


def _lat_attn_a_call(proj, length, past, ck, cv, cb, pair0, index, qn, kn):
    m = proj.shape[0]
    head = lambda h0: pl.BlockSpec((length, HEAD_DIM), lambda h, b: (b, h0 + h))
    cache = pl.BlockSpec((past * NA_HEADS, HEAD_DIM), lambda h, b: (b, 0))
    vec = pl.BlockSpec((1, HEAD_DIM), lambda h, b: (0, 0))
    return pl.pallas_call(
        functools.partial(_lat_attn_a_kernel, pair0=pair0, index=index),
        grid=(NA_HEADS, m // length),
        in_specs=[head(QA_HEAD0), head(KA_HEAD0), head(VA_HEAD0), cache, cache,
                  pl.BlockSpec((None,) + cb.shape[1:], lambda h, b: (h, 0, 0, 0)), vec, vec],
        out_specs=pl.BlockSpec((length, HEAD_DIM), lambda h, b: (b, h)),
        out_shape=jax.ShapeDtypeStruct((m, NA_WIDTH), BF16),
        compiler_params=_cparams(("arbitrary", "arbitrary")),
        name="lat_attn_a",
    )(proj, proj, proj, ck, cv, cb, qn, kn)


def _lat_attn_b_call(proj, length, past, ck, cv, cos, sin_a, sin_b, qn, kn, sink):
    m = proj.shape[0]
    qspec = pl.BlockSpec((length, HEAD_DIM), lambda h, b: (b, QB_HEAD0 + h))
    kvspec = lambda h0: pl.BlockSpec((length, HEAD_DIM), lambda h, b: (b, h0 + h // NB_GROUP))
    cache = pl.BlockSpec((past * NB_KV_HEADS, HEAD_DIM), lambda h, b: (b, 0))
    table = pl.BlockSpec((length, HEAD_DIM), lambda h, b: (0, 0))
    vec = pl.BlockSpec((1, HEAD_DIM), lambda h, b: (0, 0))
    return pl.pallas_call(
        _lat_attn_b_kernel,
        grid=(NB_Q_HEADS, m // length),
        in_specs=[qspec, kvspec(KB_HEAD0), kvspec(VB_HEAD0), cache, cache,
                  table, table, table, vec, vec, pl.BlockSpec(memory_space=pltpu.SMEM)],
        out_specs=pl.BlockSpec((length, HEAD_DIM), lambda h, b: (b, h)),
        out_shape=jax.ShapeDtypeStruct((m, NB_WIDTH), BF16),
        compiler_params=_cparams(("arbitrary", "arbitrary")),
        name="lat_attn_b",
    )(proj, proj, proj, ck, cv, cos, sin_a, sin_b, qn, kn, sink)


def _na_bias_blocks(rpb, length):
    rows = length // GRID_W
    kr_n = min(NA_WIN_ROWS, rows)
    n_pairs = min(kr_n // 2 + 1, rows // 2)
    r = np.arange(rows)
    c = np.arange(GRID_W)
    r0 = np.clip(r - kr_n // 2, 0, rows - kr_n)
    c0 = np.clip(c - NA_WIN_COLS // 2, 0, GRID_W - NA_WIN_COLS)
    pair0 = np.minimum(r0 // 2, rows // 2 - n_pairs)
    kr = 2 * (pair0[:, None, None] + np.arange(n_pairs)[None, :, None]) + np.arange(2)[None, None, :]
    row_ok = (kr >= r0[:, None, None]) & (kr < r0[:, None, None] + kr_n)
    col_ok = (c[None, :] >= c0[:, None]) & (c[None, :] < c0[:, None] + NA_WIN_COLS)
    dr = kr - r[:, None, None] + (NA_WIN_ROWS - 1)
    dc = np.clip(c[None, :] - c[:, None], -(NA_WIN_COLS - 1), NA_WIN_COLS - 1) + (NA_WIN_COLS - 1)
    offs = np.where(row_ok, dr, -1).reshape(-1, 2)
    uniq, inverse = np.unique(offs, axis=0, return_inverse=True)
    index = inverse.reshape(rows, n_pairs)
    row_sel = (uniq[:, :, None] == np.arange(2 * NA_WIN_ROWS - 1)[None, None, :]).astype(np.float32)
    col_hit = (dc[None] == np.arange(2 * NA_WIN_COLS - 1)[:, None, None]) & col_ok[None]
    col_sel = np.zeros((2,) + col_hit.shape[:2] + (2 * GRID_W,), np.float32)
    for half in range(2):
        col_sel[half, :, :, half * GRID_W:(half + 1) * GRID_W] = col_hit
    hi = lax.Precision.HIGHEST
    per_col = jnp.einsum("hde,lecn->hldcn", rpb.astype(F32), col_sel, precision=hi)
    table = jnp.einsum("uld,hldcn->hucn", row_sel, per_col, precision=hi)
    valid = ((uniq >= 0)[:, None, :, None] & col_ok[None, :, None, :]).reshape(len(uniq), GRID_W, 2 * GRID_W)
    return (jnp.where(valid[None], table, NEG_INF), tuple(int(p) for p in pair0),
            tuple(tuple(int(u) for u in row) for row in index))


def _rope_tables(length):
    t = jnp.arange(length)
    row = (t // GRID_W).astype(F32)
    col = (t % GRID_W).astype(F32)
    n_freq = HEAD_DIM // 4
    inv = ROPE_BASE ** (-jnp.arange(n_freq, dtype=F32) / n_freq)
    ar = row[:, None] * inv
    ac = col[:, None] * inv
    ang = jnp.concatenate([ar, ar, ac, ac], axis=-1)
    cos, sin = jnp.cos(ang), jnp.sin(ang)
    lane = jnp.arange(HEAD_DIM)
    takes_left = ((lane // n_freq) % 2 == 0)[None, :]
    return cos, jnp.where(takes_left, -sin, 0.0), jnp.where(takes_left, 0.0, sin)


MERGE_TM = 512


def _resident(shape):
    zeros = (0,) * len(shape)
    return pl.BlockSpec(shape, lambda i: zeros, pipeline_mode=pl.Buffered(1))


def _mix_kernel(oa_ref, ob_ref, ga_ref, gb_ref, wpa_ref, wpb_ref, mix_ref):
    ya = _dot(oa_ref[...], wpa_ref[...].astype(BF16))
    yb = _dot(ob_ref[...], wpb_ref[...].astype(BF16))
    mix = (jax.nn.sigmoid(ga_ref[...].astype(F32)) * ya
           + jax.nn.sigmoid(gb_ref[...].astype(F32)) * yb)
    mix_ref[...] = mix.astype(mix_ref.dtype)


def _mix_call(oa, ob, gates, w_pa, w_pb):
    m = oa.shape[0]
    d = w_pa.shape[1]
    row = lambda i: (i, 0)
    return pl.pallas_call(
        _mix_kernel,
        grid=(m // MERGE_TM,),
        in_specs=[pl.BlockSpec((MERGE_TM, NA_WIDTH), row),
                  pl.BlockSpec((MERGE_TM, NB_WIDTH), row),
                  pl.BlockSpec((MERGE_TM, d), lambda i: (i, 0)),
                  pl.BlockSpec((MERGE_TM, d), lambda i: (i, 1)),
                  _resident((NA_WIDTH, d)), _resident((NB_WIDTH, d))],
        out_specs=pl.BlockSpec((MERGE_TM, d), row),
        out_shape=jax.ShapeDtypeStruct((m, d), BF16),
        compiler_params=_cparams(("arbitrary",)),
        name="mix",
    )(oa, ob, gates, gates, w_pa, w_pb)


def _outproj_kernel(mix_ref, wout_ref, x_ref, g1_ref, sc2_ref, sh2_ref, n2w_ref, wr_ref,
                    x1_ref, h2_ref, lg_ref, acc_ref):
    i = pl.program_id(0)
    last = pl.num_programs(0) - 1

    def finish(prod):
        x1 = x_ref[...] + g1_ref[0] * prod
        x1_ref[...] = x1
        h2 = _rms(x1, n2w_ref[...]) * (1.0 + sc2_ref[0]) + sh2_ref[0]
        h2_ref[...] = _pack_halves(h2)
        lg_ref[...] = _dot_split(h2, wr_ref[...])

    @pl.when(i == 0)
    def _():
        acc_ref[...] = _dot(mix_ref[...], wout_ref[...])

    @pl.when((i > 0) & (i < last))
    def _():
        prod = acc_ref[...]
        acc_ref[...] = _dot(mix_ref[...], wout_ref[...])
        finish(prod)

    @pl.when(i == last)
    def _():
        finish(acc_ref[...])


def _outproj_call(mix, wout_bf, x, g1, sc2, sh2, n2w, wr, tiles_per_group):
    m, d = x.shape
    n_tiles = m // MERGE_TM
    ahead = lambda i: (jnp.minimum(i, n_tiles - 1), 0)
    done = lambda i: (jnp.maximum(i - 1, 0), 0)
    grp = lambda i: (jnp.maximum(i - 1, 0) // tiles_per_group, 0, 0)
    return pl.pallas_call(
        _outproj_kernel,
        grid=(n_tiles + 1,),
        in_specs=[pl.BlockSpec((MERGE_TM, d), ahead),
                  _resident((d, d)),
                  pl.BlockSpec((MERGE_TM, d), done),
                  pl.BlockSpec((1, 1, d), grp), pl.BlockSpec((1, 1, d), grp), pl.BlockSpec((1, 1, d), grp),
                  pl.BlockSpec((1, d), lambda i: (0, 0)),
                  _resident((d, LOGIT_PAD))],
        out_specs=[pl.BlockSpec((MERGE_TM, d), done),
                   pl.BlockSpec((MERGE_TM, d // 2), done),
                   pl.BlockSpec((MERGE_TM, LOGIT_PAD), done)],
        out_shape=[jax.ShapeDtypeStruct((m, d), F32),
                   jax.ShapeDtypeStruct((m, d // 2), jnp.uint32),
                   jax.ShapeDtypeStruct((m, LOGIT_PAD), F32)],
        scratch_shapes=[pltpu.VMEM((MERGE_TM, d), F32)],
        compiler_params=_cparams(("arbitrary",)),
        name="outproj",
    )(mix, wout_bf, x, g1, sc2, sh2, n2w, wr)


ROUTE_TM = 512


def _first_index_of_max(vals, idx, n):
    mx = jnp.max(vals, axis=0, keepdims=True)
    first = jnp.min(jnp.where(vals == mx, idx, n), axis=0, keepdims=True)
    return mx, first


def _route_kernel(lg_ref, bias_ref, eid_ref, gw_ref, rank_ref, cnt_ref, base_ref):
    step = pl.program_id(0)

    @pl.when(step == 0)
    def _():
        base_ref[...] = jnp.zeros_like(base_ref)

    lt = lg_ref[...].T + bias_ref[...]
    n_tok = lt.shape[1]
    le = lt[0:N_EXPERTS]
    lgrp = lt[N_EXPERTS:N_EXPERTS + N_GROUPS]
    gi = lax.broadcasted_iota(jnp.int32, (N_GROUPS, n_tok), 0)
    gmax, gsel = _first_index_of_max(lgrp, gi, N_GROUPS)
    pg_sel = 1.0 / jnp.sum(jnp.exp(lgrp - gmax), axis=0, keepdims=True)
    le_sel = jnp.zeros((EXPERTS_PER_GROUP, n_tok), F32)
    for g in range(N_GROUPS):
        le_sel = jnp.where(gsel == g, le[g * EXPERTS_PER_GROUP:(g + 1) * EXPERTS_PER_GROUP], le_sel)
    ei = lax.broadcasted_iota(jnp.int32, (EXPERTS_PER_GROUP, n_tok), 0)
    v0, i0 = _first_index_of_max(le_sel, ei, EXPERTS_PER_GROUP)
    rest = jnp.where(ei == i0, -jnp.inf, le_sel)
    v1, i1 = _first_index_of_max(rest, ei, EXPERTS_PER_GROUP)
    e1 = jnp.exp(v1 - v0)
    w0 = pg_sel / (1.0 + e1)
    w1 = pg_sel * e1 / (1.0 + e1)
    eid0 = gsel * EXPERTS_PER_GROUP + i0
    eid1 = gsel * EXPERTS_PER_GROUP + i1

    xi = lax.broadcasted_iota(jnp.int32, (N_EXPERTS, n_tok), 0)
    si = lax.broadcasted_iota(jnp.int32, (n_tok, n_tok), 0)
    ti = lax.broadcasted_iota(jnp.int32, (n_tok, n_tok), 1)
    before = (si < ti).astype(BF16)
    base = base_ref[...]
    hot0 = (xi == eid0).astype(F32)
    hot1 = (xi == eid1).astype(F32)
    pre0 = _dot(hot0.astype(BF16), before)
    pre1 = _dot(hot1.astype(BF16), before)
    tot0 = jnp.sum(hot0, axis=1, keepdims=True)
    tot1 = jnp.sum(hot1, axis=1, keepdims=True)
    rank0 = jnp.sum(hot0 * (base + pre0), axis=0, keepdims=True)
    rank1 = jnp.sum(hot1 * (base + tot0 + pre1), axis=0, keepdims=True)
    base = base + tot0 + tot1
    base_ref[...] = base

    ri = lax.broadcasted_iota(jnp.int32, (8, n_tok), 0)
    pick = lambda a, b: jnp.where(ri == 0, a, jnp.where(ri == 1, b, jnp.zeros_like(a)))
    eid_ref[...] = pick(eid0, eid1)
    gw_ref[...] = pick(w0, w1)
    rank_ref[...] = pick(rank0, rank1).astype(jnp.int32)
    cnt_ref[...] = jnp.broadcast_to(base, cnt_ref.shape).astype(jnp.int32)


def _route_call(logits, bias_col):
    t = logits.shape[0]
    tok = pl.BlockSpec((8, ROUTE_TM), lambda i: (0, i))
    return pl.pallas_call(
        _route_kernel,
        grid=(t // ROUTE_TM,),
        in_specs=[pl.BlockSpec((ROUTE_TM, LOGIT_PAD), lambda i: (i, 0)),
                  pl.BlockSpec((LOGIT_PAD, 1), lambda i: (0, 0))],
        out_specs=[tok, tok, tok, pl.BlockSpec((N_EXPERTS, 128), lambda i: (0, 0))],
        out_shape=[jax.ShapeDtypeStruct((8, t), jnp.int32),
                   jax.ShapeDtypeStruct((8, t), F32),
                   jax.ShapeDtypeStruct((8, t), jnp.int32),
                   jax.ShapeDtypeStruct((N_EXPERTS, 128), jnp.int32)],
        scratch_shapes=[pltpu.VMEM((N_EXPERTS, 1), F32)],
        compiler_params=_cparams(("arbitrary",)),
        name="route",
    )(logits, bias_col)


def _moe_layout(n_pairs):
    padded_rows = -(-(n_pairs + N_EXPERTS * (MOE_ROW_BLOCK - 1)) // MOE_ROW_BLOCK) * MOE_ROW_BLOCK
    n_items = (padded_rows + N_EXPERTS * (MOE_ROW_GROUP - MOE_ROW_BLOCK)) // MOE_ROW_GROUP
    return padded_rows, n_items


def _routing_tables(eid, rank, counts, n_items):
    padded = (counts + MOE_ROW_BLOCK - 1) // MOE_ROW_BLOCK * MOE_ROW_BLOCK
    pad_end = jnp.cumsum(padded)
    pad_start = pad_end - padded
    hot = eid[..., None] == jnp.arange(N_EXPERTS, dtype=jnp.int32)
    dest = (jnp.sum(jnp.where(hot, pad_start, 0), axis=-1) + rank).astype(jnp.int32)
    tail = jnp.where(padded > counts, pad_end - MOE_ROW_BLOCK, -1).astype(jnp.int32)
    per_expert = (padded + MOE_ROW_GROUP - 1) // MOE_ROW_GROUP
    item_end = jnp.cumsum(per_expert)
    item_start = item_end - per_expert
    total = item_end[-1]
    ii = jnp.arange(n_items, dtype=jnp.int32)
    e_of = jnp.minimum(jnp.searchsorted(item_end, ii, side="right"), N_EXPERTS - 1).astype(jnp.int32)
    valid = ii < total
    e_last = e_of[jnp.maximum(total - 1, 0)]
    local = ii - item_start[e_of]
    row0 = pad_start[e_of] + local * MOE_ROW_GROUP
    nblk = jnp.clip((padded[e_of] - local * MOE_ROW_GROUP) // MOE_ROW_BLOCK, 0, MOE_ROW_GROUP // MOE_ROW_BLOCK)
    item_e = jnp.where(valid, e_of, e_last).astype(jnp.int32)
    item_row0 = jnp.where(valid, row0, 0).astype(jnp.int32)
    item_nblk = jnp.where(valid, nblk, 0).astype(jnp.int32)
    return dest, tail, item_e, item_row0, item_nblk


DISPATCH_TOKENS = 256
ROW_DMA_GROUP = 8


def _row_copy(src, s, dst, d, sem):
    return pltpu.make_async_copy(src.at[pl.ds(s, 1), :], dst.at[pl.ds(d, 1), :], sem)


def _dispatch_kernel(dest_ref, tail_ref, h_ctx, h_lat, xs, zero_buf, sem, *, n_ctx, n_tok):
    step = pl.program_id(0)
    tail_copy = lambda e: pltpu.make_async_copy(
        zero_buf, xs.at[pl.ds(pl.multiple_of(tail_ref[e], MOE_ROW_BLOCK), MOE_ROW_BLOCK), :], sem.at[1])

    @pl.when(step == 0)
    def _():
        zero_buf[...] = jnp.zeros_like(zero_buf)
        for e in range(N_EXPERTS):
            @pl.when(tail_ref[e] >= 0)
            def _():
                tail_copy(e).start()
        for e in range(N_EXPERTS):
            @pl.when(tail_ref[e] >= 0)
            def _():
                tail_copy(e).wait()

    tok0 = step * DISPATCH_TOKENS

    def scatter(src):
        def issue(g, carry):
            base = pl.multiple_of(g * ROW_DMA_GROUP, ROW_DMA_GROUP)
            for j in range(ROW_DMA_GROUP):
                _row_copy(src, base + j, xs, dest_ref[tok0 + base + j], sem.at[0]).start(priority=0)
                _row_copy(src, base + j, xs, dest_ref[n_tok + tok0 + base + j], sem.at[0]).start(priority=1)
            return carry

        lax.fori_loop(0, DISPATCH_TOKENS // ROW_DMA_GROUP, issue, 0)
        for _ in range(2):
            pltpu.make_async_copy(src, xs.at[pl.ds(0, DISPATCH_TOKENS), :], sem.at[0]).wait()

    @pl.when(tok0 < n_ctx)
    def _():
        scatter(h_ctx)

    @pl.when(tok0 >= n_ctx)
    def _():
        scatter(h_lat)


def _dispatch_call(dest_flat, tail, h_ctx, h_lat, padded_rows):
    n_ctx, d = h_ctx.shape
    n_tok = n_ctx + h_lat.shape[0]
    ctx_tiles = n_ctx // DISPATCH_TOKENS
    return pl.pallas_call(
        functools.partial(_dispatch_kernel, n_ctx=n_ctx, n_tok=n_tok),
        grid_spec=pltpu.PrefetchScalarGridSpec(
            num_scalar_prefetch=2,
            grid=(n_tok // DISPATCH_TOKENS,),
            in_specs=[pl.BlockSpec((DISPATCH_TOKENS, d), lambda i, dr, tr: (jnp.minimum(i, ctx_tiles - 1), 0)),
                      pl.BlockSpec((DISPATCH_TOKENS, d), lambda i, dr, tr: (jnp.maximum(i - ctx_tiles, 0), 0))],
            out_specs=pl.BlockSpec(memory_space=pl.ANY),
            scratch_shapes=[pltpu.VMEM((MOE_ROW_BLOCK, d), h_ctx.dtype), pltpu.SemaphoreType.DMA((2,))]),
        out_shape=jax.ShapeDtypeStruct((padded_rows, d), h_ctx.dtype),
        compiler_params=_cparams(("arbitrary",)),
        name="dispatch",
    )(dest_flat, tail, h_ctx, h_lat)


def _moe_kernel(item_e, item_row0, item_nblk, xs, w1_ref, w3_ref, w2_ref, ys,
                x_in, x_bf, acc, y_out, w1_bf, w3_bf, w2_bf, sem):
    i = pl.program_id(0)
    c = pl.program_id(1)
    n_items = pl.num_programs(0)
    last_c = pl.num_programs(1) - 1
    nblk = item_nblk[i]
    max_blk = MOE_ROW_GROUP // MOE_ROW_BLOCK
    blk = lambda b: pl.ds(b * MOE_ROW_BLOCK, MOE_ROW_BLOCK)

    def rows_of(item, b):
        return pl.ds(pl.multiple_of(item_row0[item], MOE_ROW_BLOCK) + b * MOE_ROW_BLOCK, MOE_ROW_BLOCK)

    load = lambda item, b: pltpu.make_async_copy(xs.at[rows_of(item, b), :], x_in.at[blk(b), :], sem.at[0])
    store = lambda item, b: pltpu.make_async_copy(y_out.at[blk(b), :], ys.at[rows_of(item, b), :], sem.at[1])

    def for_blocks(item, fn):
        n = item_nblk[item]
        for b in range(max_blk):
            @pl.when(b < n)
            def _():
                fn(item, b)

    @pl.when(c == 0)
    def _():
        @pl.when(i == 0)
        def _():
            for_blocks(0, lambda it, b: load(it, b).start())

        for_blocks(i, lambda it, b: load(it, b).wait())

        def cast(it, b):
            x_bf[blk(b), :] = _unpack_halves(x_in[blk(b), :]).astype(BF16)

        for_blocks(i, cast)

        @pl.when(i + 1 < n_items)
        def _():
            for_blocks(i + 1, lambda it, b: load(it, b).start())

        def clear(it, b):
            acc[blk(b), :] = jnp.zeros((MOE_ROW_BLOCK, acc.shape[1]), F32)

        for_blocks(i, clear)

    @pl.when(nblk > 0)
    def _():
        def block(rows, w1, w3, w2):
            x = x_bf[rows, :]
            h1 = _dot(x, w1)
            h3 = _dot(x, w3)
            a = (h1 * jax.nn.sigmoid(h1) * h3).astype(BF16)
            acc[rows, :] += _dot(a, w2)

        w1 = w1_ref[0].astype(BF16)
        w3 = w3_ref[0].astype(BF16)
        w2 = w2_ref[0].astype(BF16)
        w1_bf[...] = w1
        w3_bf[...] = w3
        w2_bf[...] = w2
        block(blk(0), w1, w3, w2)

        def body(b, carry):
            rows = pl.ds(pl.multiple_of(b * MOE_ROW_BLOCK, MOE_ROW_BLOCK), MOE_ROW_BLOCK)
            block(rows, w1_bf[...], w3_bf[...], w2_bf[...])
            return carry

        lax.fori_loop(1, nblk, body, 0)

    @pl.when(c == last_c)
    def _():
        @pl.when(i > 0)
        def _():
            for_blocks(i - 1, lambda it, b: store(it, b).wait())

        def pack(it, b):
            y_out[blk(b), :] = _pack_halves(acc[blk(b), :])

        for_blocks(i, pack)
        for_blocks(i, lambda it, b: store(it, b).start())

        @pl.when(i == n_items - 1)
        def _():
            for_blocks(i, lambda it, b: store(it, b).wait())


def _moe_call(item_e, item_row0, item_nblk, xs, w1, w3, w2):
    padded_rows, d_packed = xs.shape
    d = w1.shape[1]
    n_items = item_e.shape[0]
    f = w1.shape[2]
    nc = f // MOE_F_CHUNK
    chunk = lambda i, c, ib: jnp.where(ib[i] > 0, c, nc - 1)
    return pl.pallas_call(
        _moe_kernel,
        grid_spec=pltpu.PrefetchScalarGridSpec(
            num_scalar_prefetch=3,
            grid=(n_items, nc),
            in_specs=[pl.BlockSpec(memory_space=pl.ANY),
                      pl.BlockSpec((1, d, MOE_F_CHUNK), lambda i, c, ie, ir, ib: (ie[i], 0, chunk(i, c, ib))),
                      pl.BlockSpec((1, d, MOE_F_CHUNK), lambda i, c, ie, ir, ib: (ie[i], 0, chunk(i, c, ib))),
                      pl.BlockSpec((1, MOE_F_CHUNK, d), lambda i, c, ie, ir, ib: (ie[i], chunk(i, c, ib), 0))],
            out_specs=pl.BlockSpec(memory_space=pl.ANY),
            scratch_shapes=[pltpu.VMEM((MOE_ROW_GROUP, d_packed), xs.dtype),
                            pltpu.VMEM((MOE_ROW_GROUP, d), BF16),
                            pltpu.VMEM((MOE_ROW_GROUP, d), F32),
                            pltpu.VMEM((MOE_ROW_GROUP, d_packed), xs.dtype),
                            pltpu.VMEM((d, MOE_F_CHUNK), BF16),
                            pltpu.VMEM((d, MOE_F_CHUNK), BF16),
                            pltpu.VMEM((MOE_F_CHUNK, d), BF16),
                            pltpu.SemaphoreType.DMA((2,))]),
        out_shape=jax.ShapeDtypeStruct((padded_rows, d_packed), xs.dtype),
        compiler_params=_cparams(("arbitrary", "arbitrary")),
        name="moe",
    )(item_e, item_row0, item_nblk, xs, w1, w3, w2)


COMBINE_TM = 256


def _combine_kernel(dest_ref, x1_ref, gw_ref, g2_ref, ys, o_ref, y0, y1, sem, *, tok_base, n_tok):
    tok0 = tok_base + pl.program_id(0) * COMBINE_TM

    def issue(g, carry):
        base = pl.multiple_of(g * ROW_DMA_GROUP, ROW_DMA_GROUP)
        for j in range(ROW_DMA_GROUP):
            _row_copy(ys, dest_ref[tok0 + base + j], y0, base + j, sem).start(priority=0)
            _row_copy(ys, dest_ref[n_tok + tok0 + base + j], y1, base + j, sem).start(priority=1)
        return carry

    lax.fori_loop(0, COMBINE_TM // ROW_DMA_GROUP, issue, 0)
    for buf in (y0, y1):
        pltpu.make_async_copy(ys.at[pl.ds(0, COMBINE_TM), :], buf, sem).wait()
    gw = gw_ref[...]
    moe = gw[:, 0:1] * _unpack_halves(y0[...]) + gw[:, 1:2] * _unpack_halves(y1[...])
    o_ref[...] = x1_ref[...] + g2_ref[0] * moe


def _combine_call(dest_flat, x1, gw, g2, ys, tok_base, n_tok, tiles_per_group):
    m, d = x1.shape
    return pl.pallas_call(
        functools.partial(_combine_kernel, tok_base=tok_base, n_tok=n_tok),
        grid_spec=pltpu.PrefetchScalarGridSpec(
            num_scalar_prefetch=1,
            grid=(m // COMBINE_TM,),
            in_specs=[pl.BlockSpec((COMBINE_TM, d), lambda i, dr: (i, 0)),
                      pl.BlockSpec((COMBINE_TM, 2), lambda i, dr: (i, 0)),
                      pl.BlockSpec((1, 1, d), lambda i, dr: (i // tiles_per_group, 0, 0)),
                      pl.BlockSpec(memory_space=pl.ANY)],
            out_specs=pl.BlockSpec((COMBINE_TM, d), lambda i, dr: (i, 0)),
            scratch_shapes=[pltpu.VMEM((COMBINE_TM,) + ys.shape[1:], ys.dtype),
                            pltpu.VMEM((COMBINE_TM,) + ys.shape[1:], ys.dtype),
                            pltpu.SemaphoreType.DMA]),
        out_shape=jax.ShapeDtypeStruct((m, d), F32),
        compiler_params=_cparams(("arbitrary",)),
        name="combine",
    )(dest_flat, x1, gw, g2, ys)


def kernel(x_prompt, x_sample, cache_a_k, cache_a_v, cache_b_k, cache_b_v, c, c_ctx, norm1_w, norm2_w, w_ada, b_ada, w_in, qn_a, kn_a, qn_b, kn_b, rpb_a, sink_b, w_pa, w_pb, w_out, w_rg, b_rg, w_re, b_re, w1, w3, w2):
    batch, seq, d = x_prompt.shape
    dec_batch, dec_seq, _ = x_sample.shape
    depth = norm1_w.shape[0]
    assert depth == 1, "one trunk layer"
    past = cache_a_k.shape[2]
    n_ctx, n_lat = batch * seq, dec_batch * dec_seq
    n_tok = n_ctx + n_lat

    xc = x_prompt.reshape(n_ctx, d)
    xl = x_sample.reshape(n_lat, d)

    cond = jnp.concatenate([c_ctx[None, :], c], axis=0)
    mod = _ada_call(cond, w_ada[0], b_ada[0][None, :])
    sh1, sc1, g1, sh2, sc2, g2 = [mod[:, i * d:(i + 1) * d][:, None, :] for i in range(6)]
    ctx_rows, lat_rows = slice(0, 1), slice(1, 1 + dec_batch)

    nw1, nw2 = norm1_w[0][None, :], norm2_w[0][None, :]
    qna, kna, qnb, knb = qn_a[0][None, :], kn_a[0][None, :], qn_b[0][None, :], kn_b[0][None, :]
    sink = sink_b[0]
    wout_bf = w_out[0].astype(BF16)
    wr = jnp.zeros((d, LOGIT_PAD), F32).at[:, :N_EXPERTS].set(w_re[0]).at[:, N_EXPERTS:N_EXPERTS + N_GROUPS].set(w_rg[0])
    br = jnp.zeros((LOGIT_PAD, 1), F32).at[:N_EXPERTS, 0].set(b_re[0]).at[N_EXPERTS:N_EXPERTS + N_GROUPS, 0].set(b_rg[0])

    proj_c, gates_c = _inproj_call(xc, nw1, sc1[ctx_rows], sh1[ctx_rows], w_in[0], n_ctx)
    proj_l, gates_l = _inproj_call(xl, nw1, sc1[lat_rows], sh1[lat_rows], w_in[0], dec_seq)

    oa_c, ob_c, new_a_k, new_a_v, new_b_k, new_b_v = _ctx_attn_call(proj_c, seq, qna, kna, qnb, knb, sink)

    bias_blocks, pair0, bias_index = _na_bias_blocks(rpb_a[0], dec_seq)
    cos, sin_a, sin_b = _rope_tables(dec_seq)
    rows_of = lambda cache: cache.reshape(-1, HEAD_DIM)
    oa_l = _lat_attn_a_call(proj_l, dec_seq, past, rows_of(cache_a_k), rows_of(cache_a_v),
                            bias_blocks, pair0, bias_index, qna, kna)
    ob_l = _lat_attn_b_call(proj_l, dec_seq, past, rows_of(cache_b_k), rows_of(cache_b_v),
                            cos, sin_a, sin_b, qnb, knb, sink)

    mix_c = _mix_call(oa_c, ob_c, gates_c, w_pa[0], w_pb[0])
    mix_l = _mix_call(oa_l, ob_l, gates_l, w_pa[0], w_pb[0])
    x1_c, h2_c, lg_c = _outproj_call(mix_c, wout_bf, xc, g1[ctx_rows], sc2[ctx_rows], sh2[ctx_rows],
                                     nw2, wr, n_ctx // MERGE_TM)
    x1_l, h2_l, lg_l = _outproj_call(mix_l, wout_bf, xl, g1[lat_rows], sc2[lat_rows], sh2[lat_rows],
                                     nw2, wr, dec_seq // MERGE_TM)

    eid, gw, rank, cnt = _route_call(jnp.concatenate([lg_c, lg_l], axis=0), br)
    padded_rows, n_items = _moe_layout(2 * n_tok)
    dest, tail, item_e, item_row0, item_nblk = _routing_tables(eid[:2], rank[:2], cnt[:, 0], n_items)
    dest_flat = dest.reshape(-1)
    xs = _dispatch_call(dest_flat, tail, h2_c, h2_l, padded_rows)
    ys = _moe_call(item_e, item_row0, item_nblk, xs, w1[0], w3[0], w2[0])
    gw_t = gw[:2].T
    y_c = _combine_call(dest_flat, x1_c, gw_t[:n_ctx], g2[ctx_rows], ys, 0, n_tok, n_ctx // COMBINE_TM)
    y_l = _combine_call(dest_flat, x1_l, gw_t[n_ctx:], g2[lat_rows], ys, n_ctx, n_tok, dec_seq // COMBINE_TM)

    state = lambda a, heads: a.reshape(batch, 1, seq, heads, HEAD_DIM)
    return (y_c.reshape(batch, seq, d), y_l.reshape(dec_batch, dec_seq, d),
            state(new_a_k, NA_HEADS), state(new_a_v, NA_HEADS),
            state(new_b_k, NB_KV_HEADS), state(new_b_v, NB_KV_HEADS))
```

```python
import functools

import jax
import jax.numpy as jnp
import numpy as np
from jax import lax
from jax.experimental import pallas as pl
from jax.experimental.pallas import tpu as pltpu

D_MODEL = 2048
HEAD_DIM = 128
NA_HEADS = 8
NA_WIDTH = NA_HEADS * HEAD_DIM
NB_Q_HEADS = 8
NB_KV_HEADS = 2
NB_GROUP = NB_Q_HEADS // NB_KV_HEADS
NB_WIDTH = NB_Q_HEADS * HEAD_DIM
NB_KV_WIDTH = NB_KV_HEADS * HEAD_DIM
GRID_W = 64
NA_WIN_ROWS = 8
NA_WIN_COLS = 16
WINDOW = 128
N_GROUPS = 4
EXPERTS_PER_GROUP = 8
N_EXPERTS = N_GROUPS * EXPERTS_PER_GROUP
D_EXPERT = 1024
IN_WIDTH = 3 * NA_WIDTH + NB_WIDTH + 2 * NB_KV_WIDTH + 2 * D_MODEL
ROPE_BASE = 10000.0
NORM_EPS = 1e-6
NEG_INF = -1e30
ATTN_SCALE = HEAD_DIM ** -0.5

QA_HEAD0 = 0
KA_HEAD0 = NA_HEADS
VA_HEAD0 = 2 * NA_HEADS
QB_HEAD0 = 3 * NA_HEADS
KB_HEAD0 = QB_HEAD0 + NB_Q_HEADS
VB_HEAD0 = KB_HEAD0 + NB_KV_HEADS
GATE_COL0 = (VB_HEAD0 + NB_KV_HEADS) * HEAD_DIM

LOGIT_PAD = 128
MOE_ROW_BLOCK = 256
MOE_ROW_GROUP = 1024
MOE_F_CHUNK = 512
VMEM_LIMIT = 56 * 1024 * 1024

F32 = jnp.float32
BF16 = jnp.bfloat16


def _cparams(sem, vmem_limit=VMEM_LIMIT):
    return pltpu.CompilerParams(dimension_semantics=sem, vmem_limit_bytes=vmem_limit)


def _rms(x, w):
    x = x.astype(F32)
    return x * lax.rsqrt(jnp.mean(x * x, axis=-1, keepdims=True) + NORM_EPS) * w


def _dot(a, b):
    return jnp.dot(a, b, preferred_element_type=F32)


def _dot_nt(a, b):
    return lax.dot_general(a, b, (((1,), (1,)), ((), ())), preferred_element_type=F32)


def _pack_halves(x):
    n = x.shape[1] // 2
    lo = lax.bitcast_convert_type(x[:, :n].astype(BF16).astype(F32), jnp.uint32)
    hi = lax.bitcast_convert_type(x[:, n:].astype(BF16).astype(F32), jnp.uint32)
    return hi | (lo >> 16)


def _unpack_halves(w):
    lo = lax.bitcast_convert_type(w << 16, F32)
    hi = lax.bitcast_convert_type(w & jnp.uint32(0xFFFF0000), F32)
    return jnp.concatenate([lo, hi], axis=1)


def _dot_split(a, b):
    a_hi = a.astype(BF16)
    a_lo = (a - a_hi.astype(F32)).astype(BF16)
    b_hi = b.astype(BF16)
    b_lo = (b - b_hi.astype(F32)).astype(BF16)
    return _dot(a_hi, b_hi) + (_dot(a_lo, b_hi) + _dot(a_hi, b_lo))


ADA_ROWS = 8
ADA_TN = 1024


def _ada_kernel(c_ref, w_ref, b_ref, o_ref):
    n_rows, d, lanes = c_ref.shape
    tn = w_ref.shape[1]

    def body(kb, acc):
        ks = pl.ds(pl.multiple_of(kb * 8, 8), 8)
        w = w_ref[ks, :]
        out = []
        for r in range(n_rows):
            c = c_ref[r, ks, :]
            s = c * jax.nn.sigmoid(c)
            out.append(acc[r] + w * jnp.concatenate([s] * (tn // lanes), axis=1))
        return tuple(out)

    acc = lax.fori_loop(0, d // 8, body, tuple(jnp.zeros((8, tn), F32) for _ in range(n_rows)), unroll=8)
    ri = lax.broadcasted_iota(jnp.int32, (ADA_ROWS, tn), 0)
    res = jnp.zeros((ADA_ROWS, tn), F32)
    for r in range(n_rows):
        row = jnp.sum(acc[r], axis=0, keepdims=True) + b_ref[...]
        res = jnp.where(ri == r, row, res)
    o_ref[...] = res


def _ada_call(cond, w_ada, b_ada):
    n_rows, d = cond.shape
    n = w_ada.shape[1]
    lanes = 128
    cond_lanes = jnp.broadcast_to(cond[:, :, None], (n_rows, d, lanes))
    return pl.pallas_call(
        _ada_kernel,
        grid=(n // ADA_TN,),
        in_specs=[pl.BlockSpec((n_rows, d, lanes), lambda j: (0, 0, 0)),
                  pl.BlockSpec((d, ADA_TN), lambda j: (0, j)),
                  pl.BlockSpec((1, ADA_TN), lambda j: (0, j))],
        out_specs=pl.BlockSpec((ADA_ROWS, ADA_TN), lambda j: (0, j)),
        out_shape=jax.ShapeDtypeStruct((ADA_ROWS, n), F32),
        compiler_params=_cparams(("arbitrary",)),
        name="ada",
    )(cond_lanes, w_ada, b_ada)


INPROJ_TM = 2048
INPROJ_TN = 512
NORM_ROWS = 128


def _inproj_kernel(x_hbm, nw_ref, sc_ref, sh_ref, w_ref, qkv_ref, gate_ref, x_buf, h_scr, sem, *, qkv_tiles):
    i = pl.program_id(0)
    j = pl.program_id(1)
    tm = x_buf.shape[0]
    fetch = lambda tile: pltpu.make_async_copy(
        x_hbm.at[pl.ds(pl.multiple_of(tile * tm, tm), tm), :], x_buf, sem)

    @pl.when(j == 0)
    def _():
        @pl.when(i == 0)
        def _():
            fetch(0).start()

        fetch(i).wait()
        nw = nw_ref[...]
        sc = 1.0 + sc_ref[0]
        sh = sh_ref[0]

        def body(r, carry):
            rows = pl.ds(pl.multiple_of(r * NORM_ROWS, NORM_ROWS), NORM_ROWS)
            h_scr[rows, :] = (_rms(x_buf[rows, :], nw) * sc + sh).astype(BF16)
            return carry

        lax.fori_loop(0, tm // NORM_ROWS, body, 0)

    @pl.when((j == 1) & (i + 1 < pl.num_programs(0)))
    def _():
        fetch(i + 1).start()

    res = _dot(h_scr[...], w_ref[...].astype(BF16))

    @pl.when(j < qkv_tiles)
    def _():
        qkv_ref[...] = res.astype(qkv_ref.dtype)

    @pl.when(j >= qkv_tiles)
    def _():
        gate_ref[...] = res.astype(gate_ref.dtype)


def _inproj_call(x, nw, sc, sh, w_in, rows_per_group):
    m, d = x.shape
    n = w_in.shape[1]
    assert n // INPROJ_TN >= 2, "the next token tile is requested in the second column step"
    tm = min(INPROJ_TM, rows_per_group)
    tiles_per_group = rows_per_group // tm
    qkv_tiles = GATE_COL0 // INPROJ_TN
    grp = lambda i, j: (i // tiles_per_group, 0, 0)
    return pl.pallas_call(
        functools.partial(_inproj_kernel, qkv_tiles=qkv_tiles),
        grid=(m // tm, n // INPROJ_TN),
        in_specs=[pl.BlockSpec(memory_space=pl.ANY),
                  pl.BlockSpec((1, d), lambda i, j: (0, 0)),
                  pl.BlockSpec((1, 1, d), grp),
                  pl.BlockSpec((1, 1, d), grp),
                  pl.BlockSpec((d, INPROJ_TN), lambda i, j: (0, j))],
        out_specs=[pl.BlockSpec((tm, INPROJ_TN), lambda i, j: (i, jnp.minimum(j, qkv_tiles - 1))),
                   pl.BlockSpec((tm, INPROJ_TN), lambda i, j: (i, jnp.maximum(j - qkv_tiles, 0)))],
        out_shape=[jax.ShapeDtypeStruct((m, GATE_COL0), BF16),
                   jax.ShapeDtypeStruct((m, n - GATE_COL0), BF16)],
        scratch_shapes=[pltpu.VMEM((tm, d), F32), pltpu.VMEM((tm, d), BF16), pltpu.SemaphoreType.DMA],
        compiler_params=_cparams(("arbitrary", "arbitrary")),
        name="inproj",
    )(x, nw, sc, sh, w_in)


def _with_ones(v):
    return jnp.concatenate([v.astype(BF16), jnp.ones(v.shape, BF16)], axis=1)


def _softmax_pv(scores, values_with_ones, sink=None):
    m = None
    for s in scores:
        ms = jnp.max(s, axis=-1, keepdims=True)
        m = ms if m is None else jnp.maximum(m, ms)
    if sink is not None:
        m = jnp.maximum(m, sink)
    acc = None
    for s, v1 in zip(scores, values_with_ones):
        pv = _dot(jnp.exp(s - m).astype(BF16), v1)
        acc = pv if acc is None else acc + pv
    d = acc.shape[1] // 2
    den = acc[:, d:]
    if sink is not None:
        den = den + jnp.exp(sink - m)
    return acc[:, :d] / den


def _ctx_attn_kernel(qa_ref, ka_ref, va_ref, qb_ref, kb_ref, vb_ref,
                     qna_ref, kna_ref, qnb_ref, knb_ref, sink_ref,
                     oa_ref, ob_ref, nak_ref, nav_ref, nbk_ref, nbv_ref):
    qna, kna, qnb, knb = qna_ref[...], kna_ref[...], qnb_ref[...], knb_ref[...]
    seq = qa_ref.shape[0]
    for h in range(NA_HEADS):
        cols = slice(h * HEAD_DIM, (h + 1) * HEAD_DIM)
        q = (_rms(qa_ref[:, cols], qna) * ATTN_SCALE).astype(BF16)
        k = _rms(ka_ref[:, cols], kna)
        v = va_ref[:, cols]
        nak_ref[pl.ds(h, seq, stride=NA_HEADS), :] = k
        nav_ref[pl.ds(h, seq, stride=NA_HEADS), :] = v.astype(F32)
        s = _dot_nt(q, k.astype(BF16))
        oa_ref[:, cols] = _softmax_pv([s], [_with_ones(v)]).astype(oa_ref.dtype)
    for kv in range(NB_KV_HEADS):
        kcols = slice(kv * HEAD_DIM, (kv + 1) * HEAD_DIM)
        k = _rms(kb_ref[:, kcols], knb)
        v = vb_ref[:, kcols]
        nbk_ref[pl.ds(kv, seq, stride=NB_KV_HEADS), :] = k
        nbv_ref[pl.ds(kv, seq, stride=NB_KV_HEADS), :] = v.astype(F32)
        kb16 = k.astype(BF16)
        vb1 = _with_ones(v)
        for g in range(NB_GROUP):
            hq = kv * NB_GROUP + g
            cols = slice(hq * HEAD_DIM, (hq + 1) * HEAD_DIM)
            q = (_rms(qb_ref[:, cols], qnb) * ATTN_SCALE).astype(BF16)
            s = _dot_nt(q, kb16)
            ob_ref[:, cols] = _softmax_pv([s], [vb1], sink=sink_ref[hq]).astype(ob_ref.dtype)


def _ctx_attn_call(proj, seq, qna, kna, qnb, knb, sink):
    m = proj.shape[0]
    nb = m // seq
    wide = lambda blk: pl.BlockSpec((seq, NA_WIDTH), lambda b: (b, blk))
    narrow = lambda blk: pl.BlockSpec((seq, NB_KV_WIDTH), lambda b: (b, blk))
    vec = pl.BlockSpec((1, HEAD_DIM), lambda b: (0, 0))
    return pl.pallas_call(
        _ctx_attn_kernel,
        grid=(nb,),
        in_specs=[wide(QA_HEAD0 // NA_HEADS), wide(KA_HEAD0 // NA_HEADS), wide(VA_HEAD0 // NA_HEADS),
                  wide(QB_HEAD0 // NA_HEADS), narrow(KB_HEAD0 // NB_KV_HEADS), narrow(VB_HEAD0 // NB_KV_HEADS),
                  vec, vec, vec, vec,
                  pl.BlockSpec(memory_space=pltpu.SMEM)],
        out_specs=[pl.BlockSpec((seq, NA_WIDTH), lambda b: (b, 0)),
                   pl.BlockSpec((seq, NB_WIDTH), lambda b: (b, 0)),
                   pl.BlockSpec((seq * NA_HEADS, HEAD_DIM), lambda b: (b, 0)),
                   pl.BlockSpec((seq * NA_HEADS, HEAD_DIM), lambda b: (b, 0)),
                   pl.BlockSpec((seq * NB_KV_HEADS, HEAD_DIM), lambda b: (b, 0)),
                   pl.BlockSpec((seq * NB_KV_HEADS, HEAD_DIM), lambda b: (b, 0))],
        out_shape=[jax.ShapeDtypeStruct((m, NA_WIDTH), BF16),
                   jax.ShapeDtypeStruct((m, NB_WIDTH), BF16),
                   jax.ShapeDtypeStruct((m * NA_HEADS, HEAD_DIM), F32),
                   jax.ShapeDtypeStruct((m * NA_HEADS, HEAD_DIM), F32),
                   jax.ShapeDtypeStruct((m * NB_KV_HEADS, HEAD_DIM), F32),
                   jax.ShapeDtypeStruct((m * NB_KV_HEADS, HEAD_DIM), F32)],
        compiler_params=_cparams(("arbitrary",)),
        name="ctx_attn",
    )(proj, proj, proj, proj, proj, proj, qna, kna, qnb, knb, sink)


def _rope(x, cos, sin_a, sin_b):
    quarter = HEAD_DIM // 4
    return (x * cos + pltpu.roll(x, HEAD_DIM - quarter, 1) * sin_a
            + pltpu.roll(x, quarter, 1) * sin_b)


def _head_rows(cache_ref, head, n_heads):
    past = cache_ref.shape[0] // n_heads
    return cache_ref[pl.ds(head, past, stride=n_heads), :]


def _lat_attn_a_kernel(q_ref, k_ref, v_ref, ck_ref, cv_ref, cb_ref, qn_ref, kn_ref, o_ref, *, pair0, index):
    head = pl.program_id(0)
    q = (_rms(q_ref[...], qn_ref[...]) * ATTN_SCALE).astype(BF16)
    k = _rms(k_ref[...], kn_ref[...]).astype(BF16)
    v1 = _with_ones(v_ref[...])
    ck = _head_rows(ck_ref, head, NA_HEADS).astype(BF16)
    cv1 = _with_ones(_head_rows(cv_ref, head, NA_HEADS))
    pair = 2 * GRID_W
    rows_per_block = pair // GRID_W
    n_pairs = len(index[0])
    for blk in range(len(pair0) // rows_per_block):
        grid_rows = range(blk * rows_per_block, (blk + 1) * rows_per_block)
        p0 = pair0[grid_rows[0]]
        assert all(pair0[r] == p0 for r in grid_rows), "query rows of a block share their key pairs"
        rows = slice(blk * pair, (blk + 1) * pair)
        keys = slice(p0 * pair, (p0 + n_pairs) * pair)
        bias = jnp.concatenate(
            [jnp.concatenate([cb_ref[u] for u in index[r]], axis=1) for r in grid_rows], axis=0)
        qs = q[rows]
        s_win = _dot_nt(qs, k[keys]) + bias
        s_ctx = _dot_nt(qs, ck)
        o_ref[rows, :] = _softmax_pv([s_win, s_ctx], [v1[keys], cv1]).astype(o_ref.dtype)


def _lat_attn_b_kernel(q_ref, k_ref, v_ref, ck_ref, cv_ref, cos_ref, sina_ref, sinb_ref,
                       qn_ref, kn_ref, sink_ref, o_ref):
    cos, sin_a, sin_b = cos_ref[...], sina_ref[...], sinb_ref[...]
    q = _rope(_rms(q_ref[...], qn_ref[...]), cos, sin_a, sin_b)
    k = _rope(_rms(k_ref[...], kn_ref[...]), cos, sin_a, sin_b)
    q = (q * ATTN_SCALE).astype(BF16)
    k = k.astype(BF16)
    v1 = _with_ones(v_ref[...])
    length = q.shape[0]
    kv = pl.program_id(0) // NB_GROUP
    ck = _head_rows(ck_ref, kv, NB_KV_HEADS).astype(BF16)
    cv1 = _with_ones(_head_rows(cv_ref, kv, NB_KV_HEADS))
    sink = sink_ref[pl.program_id(0)]
    for qb in range(length // WINDOW):
        rows = slice(qb * WINDOW, (qb + 1) * WINDOW)
        lo, hi = max(0, (qb - 1) * WINDOW), min(length, (qb + 2) * WINDOW)
        qs = q[rows]
        s_win = _dot_nt(qs, k[lo:hi])
        qi = qb * WINDOW + lax.broadcasted_iota(jnp.int32, s_win.shape, 0)
        kj = lo + lax.broadcasted_iota(jnp.int32, s_win.shape, 1)
        s_win = jnp.where(jnp.abs(qi - kj) <= WINDOW, s_win, NEG_INF)
        s_ctx = _dot_nt(qs, ck)
        o_ref[rows, :] = _softmax_pv([s_win, s_ctx], [v1[lo:hi], cv1], sink=sink).astype(o_ref.dtype)


def _lat_attn_a_call(proj, length, past, ck, cv, cb, pair0, index, qn, kn):
    m = proj.shape[0]
    head = lambda h0: pl.BlockSpec((length, HEAD_DIM), lambda h, b: (b, h0 + h))
    cache = pl.BlockSpec((past * NA_HEADS, HEAD_DIM), lambda h, b: (b, 0))
    vec = pl.BlockSpec((1, HEAD_DIM), lambda h, b: (0, 0))
    return pl.pallas_call(
        functools.partial(_lat_attn_a_kernel, pair0=pair0, index=index),
        grid=(NA_HEADS, m // length),
        in_specs=[head(QA_HEAD0), head(KA_HEAD0), head(VA_HEAD0), cache, cache,
                  pl.BlockSpec((None,) + cb.shape[1:], lambda h, b: (h, 0, 0, 0)), vec, vec],
        out_specs=pl.BlockSpec((length, HEAD_DIM), lambda h, b: (b, h)),
        out_shape=jax.ShapeDtypeStruct((m, NA_WIDTH), BF16),
        compiler_params=_cparams(("arbitrary", "arbitrary")),
        name="lat_attn_a",
    )(proj, proj, proj, ck, cv, cb, qn, kn)


def _lat_attn_b_call(proj, length, past, ck, cv, cos, sin_a, sin_b, qn, kn, sink):
    m = proj.shape[0]
    qspec = pl.BlockSpec((length, HEAD_DIM), lambda h, b: (b, QB_HEAD0 + h))
    kvspec = lambda h0: pl.BlockSpec((length, HEAD_DIM), lambda h, b: (b, h0 + h // NB_GROUP))
    cache = pl.BlockSpec((past * NB_KV_HEADS, HEAD_DIM), lambda h, b: (b, 0))
    table = pl.BlockSpec((length, HEAD_DIM), lambda h, b: (0, 0))
    vec = pl.BlockSpec((1, HEAD_DIM), lambda h, b: (0, 0))
    return pl.pallas_call(
        _lat_attn_b_kernel,
        grid=(NB_Q_HEADS, m // length),
        in_specs=[qspec, kvspec(KB_HEAD0), kvspec(VB_HEAD0), cache, cache,
                  table, table, table, vec, vec, pl.BlockSpec(memory_space=pltpu.SMEM)],
        out_specs=pl.BlockSpec((length, HEAD_DIM), lambda h, b: (b, h)),
        out_shape=jax.ShapeDtypeStruct((m, NB_WIDTH), BF16),
        compiler_params=_cparams(("arbitrary", "arbitrary")),
        name="lat_attn_b",
    )(proj, proj, proj, ck, cv, cos, sin_a, sin_b, qn, kn, sink)


def _na_bias_blocks(rpb, length):
    rows = length // GRID_W
    kr_n = min(NA_WIN_ROWS, rows)
    n_pairs = min(kr_n // 2 + 1, rows // 2)
    r = np.arange(rows)
    c = np.arange(GRID_W)
    r0 = np.clip(r - kr_n // 2, 0, rows - kr_n)
    c0 = np.clip(c - NA_WIN_COLS // 2, 0, GRID_W - NA_WIN_COLS)
    pair0 = np.minimum(r0 // 2, rows // 2 - n_pairs)
    kr = 2 * (pair0[:, None, None] + np.arange(n_pairs)[None, :, None]) + np.arange(2)[None, None, :]
    row_ok = (kr >= r0[:, None, None]) & (kr < r0[:, None, None] + kr_n)
    col_ok = (c[None, :] >= c0[:, None]) & (c[None, :] < c0[:, None] + NA_WIN_COLS)
    dr = kr - r[:, None, None] + (NA_WIN_ROWS - 1)
    dc = np.clip(c[None, :] - c[:, None], -(NA_WIN_COLS - 1), NA_WIN_COLS - 1) + (NA_WIN_COLS - 1)
    offs = np.where(row_ok, dr, -1).reshape(-1, 2)
    uniq, inverse = np.unique(offs, axis=0, return_inverse=True)
    index = inverse.reshape(rows, n_pairs)
    row_sel = (uniq[:, :, None] == np.arange(2 * NA_WIN_ROWS - 1)[None, None, :]).astype(np.float32)
    col_hit = (dc[None] == np.arange(2 * NA_WIN_COLS - 1)[:, None, None]) & col_ok[None]
    col_sel = np.zeros((2,) + col_hit.shape[:2] + (2 * GRID_W,), np.float32)
    for half in range(2):
        col_sel[half, :, :, half * GRID_W:(half + 1) * GRID_W] = col_hit
    hi = lax.Precision.HIGHEST
    per_col = jnp.einsum("hde,lecn->hldcn", rpb.astype(F32), col_sel, precision=hi)
    table = jnp.einsum("uld,hldcn->hucn", row_sel, per_col, precision=hi)
    valid = ((uniq >= 0)[:, None, :, None] & col_ok[None, :, None, :]).reshape(len(uniq), GRID_W, 2 * GRID_W)
    return (jnp.where(valid[None], table, NEG_INF), tuple(int(p) for p in pair0),
            tuple(tuple(int(u) for u in row) for row in index))


def _rope_tables(length):
    t = jnp.arange(length)
    row = (t // GRID_W).astype(F32)
    col = (t % GRID_W).astype(F32)
    n_freq = HEAD_DIM // 4
    inv = ROPE_BASE ** (-jnp.arange(n_freq, dtype=F32) / n_freq)
    ar = row[:, None] * inv
    ac = col[:, None] * inv
    ang = jnp.concatenate([ar, ar, ac, ac], axis=-1)
    cos, sin = jnp.cos(ang), jnp.sin(ang)
    lane = jnp.arange(HEAD_DIM)
    takes_left = ((lane // n_freq) % 2 == 0)[None, :]
    return cos, jnp.where(takes_left, -sin, 0.0), jnp.where(takes_left, 0.0, sin)


MERGE_TM = 512


def _resident(shape):
    zeros = (0,) * len(shape)
    return pl.BlockSpec(shape, lambda i: zeros, pipeline_mode=pl.Buffered(1))


def _mix_kernel(oa_ref, ob_ref, ga_ref, gb_ref, wpa_ref, wpb_ref, mix_ref):
    ya = _dot(oa_ref[...], wpa_ref[...].astype(BF16))
    yb = _dot(ob_ref[...], wpb_ref[...].astype(BF16))
    mix = (jax.nn.sigmoid(ga_ref[...].astype(F32)) * ya
           + jax.nn.sigmoid(gb_ref[...].astype(F32)) * yb)
    mix_ref[...] = mix.astype(mix_ref.dtype)


def _mix_call(oa, ob, gates, w_pa, w_pb):
    m = oa.shape[0]
    d = w_pa.shape[1]
    row = lambda i: (i, 0)
    return pl.pallas_call(
        _mix_kernel,
        grid=(m // MERGE_TM,),
        in_specs=[pl.BlockSpec((MERGE_TM, NA_WIDTH), row),
                  pl.BlockSpec((MERGE_TM, NB_WIDTH), row),
                  pl.BlockSpec((MERGE_TM, d), lambda i: (i, 0)),
                  pl.BlockSpec((MERGE_TM, d), lambda i: (i, 1)),
                  _resident((NA_WIDTH, d)), _resident((NB_WIDTH, d))],
        out_specs=pl.BlockSpec((MERGE_TM, d), row),
        out_shape=jax.ShapeDtypeStruct((m, d), BF16),
        compiler_params=_cparams(("arbitrary",)),
        name="mix",
    )(oa, ob, gates, gates, w_pa, w_pb)


def _outproj_kernel(mix_ref, wout_ref, x_ref, g1_ref, sc2_ref, sh2_ref, n2w_ref, wr_ref,
                    x1_ref, h2_ref, lg_ref, acc_ref):
    i = pl.program_id(0)
    last = pl.num_programs(0) - 1

    def finish(prod):
        x1 = x_ref[...] + g1_ref[0] * prod
        x1_ref[...] = x1
        h2 = _rms(x1, n2w_ref[...]) * (1.0 + sc2_ref[0]) + sh2_ref[0]
        h2_ref[...] = _pack_halves(h2)
        lg_ref[...] = _dot_split(h2, wr_ref[...])

    @pl.when(i == 0)
    def _():
        acc_ref[...] = _dot(mix_ref[...], wout_ref[...])

    @pl.when((i > 0) & (i < last))
    def _():
        prod = acc_ref[...]
        acc_ref[...] = _dot(mix_ref[...], wout_ref[...])
        finish(prod)

    @pl.when(i == last)
    def _():
        finish(acc_ref[...])


def _outproj_call(mix, wout_bf, x, g1, sc2, sh2, n2w, wr, tiles_per_group):
    m, d = x.shape
    n_tiles = m // MERGE_TM
    ahead = lambda i: (jnp.minimum(i, n_tiles - 1), 0)
    done = lambda i: (jnp.maximum(i - 1, 0), 0)
    grp = lambda i: (jnp.maximum(i - 1, 0) // tiles_per_group, 0, 0)
    return pl.pallas_call(
        _outproj_kernel,
        grid=(n_tiles + 1,),
        in_specs=[pl.BlockSpec((MERGE_TM, d), ahead),
                  _resident((d, d)),
                  pl.BlockSpec((MERGE_TM, d), done),
                  pl.BlockSpec((1, 1, d), grp), pl.BlockSpec((1, 1, d), grp), pl.BlockSpec((1, 1, d), grp),
                  pl.BlockSpec((1, d), lambda i: (0, 0)),
                  _resident((d, LOGIT_PAD))],
        out_specs=[pl.BlockSpec((MERGE_TM, d), done),
                   pl.BlockSpec((MERGE_TM, d // 2), done),
                   pl.BlockSpec((MERGE_TM, LOGIT_PAD), done)],
        out_shape=[jax.ShapeDtypeStruct((m, d), F32),
                   jax.ShapeDtypeStruct((m, d // 2), jnp.uint32),
                   jax.ShapeDtypeStruct((m, LOGIT_PAD), F32)],
        scratch_shapes=[pltpu.VMEM((MERGE_TM, d), F32)],
        compiler_params=_cparams(("arbitrary",)),
        name="outproj",
    )(mix, wout_bf, x, g1, sc2, sh2, n2w, wr)


ROUTE_TM = 512


def _first_index_of_max(vals, idx, n):
    mx = jnp.max(vals, axis=0, keepdims=True)
    first = jnp.min(jnp.where(vals == mx, idx, n), axis=0, keepdims=True)
    return mx, first


def _route_kernel(lg_ref, bias_ref, eid_ref, gw_ref, rank_ref, cnt_ref, base_ref):
    step = pl.program_id(0)

    @pl.when(step == 0)
    def _():
        base_ref[...] = jnp.zeros_like(base_ref)

    lt = lg_ref[...].T + bias_ref[...]
    n_tok = lt.shape[1]
    le = lt[0:N_EXPERTS]
    lgrp = lt[N_EXPERTS:N_EXPERTS + N_GROUPS]
    gi = lax.broadcasted_iota(jnp.int32, (N_GROUPS, n_tok), 0)
    gmax, gsel = _first_index_of_max(lgrp, gi, N_GROUPS)
    pg_sel = 1.0 / jnp.sum(jnp.exp(lgrp - gmax), axis=0, keepdims=True)
    le_sel = jnp.zeros((EXPERTS_PER_GROUP, n_tok), F32)
    for g in range(N_GROUPS):
        le_sel = jnp.where(gsel == g, le[g * EXPERTS_PER_GROUP:(g + 1) * EXPERTS_PER_GROUP], le_sel)
    ei = lax.broadcasted_iota(jnp.int32, (EXPERTS_PER_GROUP, n_tok), 0)
    v0, i0 = _first_index_of_max(le_sel, ei, EXPERTS_PER_GROUP)
    rest = jnp.where(ei == i0, -jnp.inf, le_sel)
    v1, i1 = _first_index_of_max(rest, ei, EXPERTS_PER_GROUP)
    e1 = jnp.exp(v1 - v0)
    w0 = pg_sel / (1.0 + e1)
    w1 = pg_sel * e1 / (1.0 + e1)
    eid0 = gsel * EXPERTS_PER_GROUP + i0
    eid1 = gsel * EXPERTS_PER_GROUP + i1

    xi = lax.broadcasted_iota(jnp.int32, (N_EXPERTS, n_tok), 0)
    si = lax.broadcasted_iota(jnp.int32, (n_tok, n_tok), 0)
    ti = lax.broadcasted_iota(jnp.int32, (n_tok, n_tok), 1)
    before = (si < ti).astype(BF16)
    base = base_ref[...]
    hot0 = (xi == eid0).astype(F32)
    hot1 = (xi == eid1).astype(F32)
    pre0 = _dot(hot0.astype(BF16), before)
    pre1 = _dot(hot1.astype(BF16), before)
    tot0 = jnp.sum(hot0, axis=1, keepdims=True)
    tot1 = jnp.sum(hot1, axis=1, keepdims=True)
    rank0 = jnp.sum(hot0 * (base + pre0), axis=0, keepdims=True)
    rank1 = jnp.sum(hot1 * (base + tot0 + pre1), axis=0, keepdims=True)
    base = base + tot0 + tot1
    base_ref[...] = base

    ri = lax.broadcasted_iota(jnp.int32, (8, n_tok), 0)
    pick = lambda a, b: jnp.where(ri == 0, a, jnp.where(ri == 1, b, jnp.zeros_like(a)))
    eid_ref[...] = pick(eid0, eid1)
    gw_ref[...] = pick(w0, w1)
    rank_ref[...] = pick(rank0, rank1).astype(jnp.int32)
    cnt_ref[...] = jnp.broadcast_to(base, cnt_ref.shape).astype(jnp.int32)


def _route_call(logits, bias_col):
    t = logits.shape[0]
    tok = pl.BlockSpec((8, ROUTE_TM), lambda i: (0, i))
    return pl.pallas_call(
        _route_kernel,
        grid=(t // ROUTE_TM,),
        in_specs=[pl.BlockSpec((ROUTE_TM, LOGIT_PAD), lambda i: (i, 0)),
                  pl.BlockSpec((LOGIT_PAD, 1), lambda i: (0, 0))],
        out_specs=[tok, tok, tok, pl.BlockSpec((N_EXPERTS, 128), lambda i: (0, 0))],
        out_shape=[jax.ShapeDtypeStruct((8, t), jnp.int32),
                   jax.ShapeDtypeStruct((8, t), F32),
                   jax.ShapeDtypeStruct((8, t), jnp.int32),
                   jax.ShapeDtypeStruct((N_EXPERTS, 128), jnp.int32)],
        scratch_shapes=[pltpu.VMEM((N_EXPERTS, 1), F32)],
        compiler_params=_cparams(("arbitrary",)),
        name="route",
    )(logits, bias_col)


def _moe_layout(n_pairs):
    padded_rows = -(-(n_pairs + N_EXPERTS * (MOE_ROW_BLOCK - 1)) // MOE_ROW_BLOCK) * MOE_ROW_BLOCK
    n_items = (padded_rows + N_EXPERTS * (MOE_ROW_GROUP - MOE_ROW_BLOCK)) // MOE_ROW_GROUP
    return padded_rows, n_items


def _routing_tables(eid, rank, counts, n_items):
    padded = (counts + MOE_ROW_BLOCK - 1) // MOE_ROW_BLOCK * MOE_ROW_BLOCK
    pad_end = jnp.cumsum(padded)
    pad_start = pad_end - padded
    hot = eid[..., None] == jnp.arange(N_EXPERTS, dtype=jnp.int32)
    dest = (jnp.sum(jnp.where(hot, pad_start, 0), axis=-1) + rank).astype(jnp.int32)
    tail = jnp.where(padded > counts, pad_end - MOE_ROW_BLOCK, -1).astype(jnp.int32)
    per_expert = (padded + MOE_ROW_GROUP - 1) // MOE_ROW_GROUP
    item_end = jnp.cumsum(per_expert)
    item_start = item_end - per_expert
    total = item_end[-1]
    ii = jnp.arange(n_items, dtype=jnp.int32)
    e_of = jnp.minimum(jnp.searchsorted(item_end, ii, side="right"), N_EXPERTS - 1).astype(jnp.int32)
    valid = ii < total
    e_last = e_of[jnp.maximum(total - 1, 0)]
    local = ii - item_start[e_of]
    row0 = pad_start[e_of] + local * MOE_ROW_GROUP
    nblk = jnp.clip((padded[e_of] - local * MOE_ROW_GROUP) // MOE_ROW_BLOCK, 0, MOE_ROW_GROUP // MOE_ROW_BLOCK)
    item_e = jnp.where(valid, e_of, e_last).astype(jnp.int32)
    item_row0 = jnp.where(valid, row0, 0).astype(jnp.int32)
    item_nblk = jnp.where(valid, nblk, 0).astype(jnp.int32)
    return dest, tail, item_e, item_row0, item_nblk


DISPATCH_TOKENS = 256
ROW_DMA_GROUP = 8


def _row_copy(src, s, dst, d, sem):
    return pltpu.make_async_copy(src.at[pl.ds(s, 1), :], dst.at[pl.ds(d, 1), :], sem)


def _dispatch_kernel(dest_ref, tail_ref, h_ctx, h_lat, xs, zero_buf, sem, *, n_ctx, n_tok):
    step = pl.program_id(0)
    tail_copy = lambda e: pltpu.make_async_copy(
        zero_buf, xs.at[pl.ds(pl.multiple_of(tail_ref[e], MOE_ROW_BLOCK), MOE_ROW_BLOCK), :], sem.at[1])

    @pl.when(step == 0)
    def _():
        zero_buf[...] = jnp.zeros_like(zero_buf)
        for e in range(N_EXPERTS):
            @pl.when(tail_ref[e] >= 0)
            def _():
                tail_copy(e).start()
        for e in range(N_EXPERTS):
            @pl.when(tail_ref[e] >= 0)
            def _():
                tail_copy(e).wait()

    tok0 = step * DISPATCH_TOKENS

    def scatter(src):
        def issue(g, carry):
            base = pl.multiple_of(g * ROW_DMA_GROUP, ROW_DMA_GROUP)
            for j in range(ROW_DMA_GROUP):
                _row_copy(src, base + j, xs, dest_ref[tok0 + base + j], sem.at[0]).start(priority=0)
                _row_copy(src, base + j, xs, dest_ref[n_tok + tok0 + base + j], sem.at[0]).start(priority=1)
            return carry

        lax.fori_loop(0, DISPATCH_TOKENS // ROW_DMA_GROUP, issue, 0)
        for _ in range(2):
            pltpu.make_async_copy(src, xs.at[pl.ds(0, DISPATCH_TOKENS), :], sem.at[0]).wait()

    @pl.when(tok0 < n_ctx)
    def _():
        scatter(h_ctx)

    @pl.when(tok0 >= n_ctx)
    def _():
        scatter(h_lat)


def _dispatch_call(dest_flat, tail, h_ctx, h_lat, padded_rows):
    n_ctx, d = h_ctx.shape
    n_tok = n_ctx + h_lat.shape[0]
    ctx_tiles = n_ctx // DISPATCH_TOKENS
    return pl.pallas_call(
        functools.partial(_dispatch_kernel, n_ctx=n_ctx, n_tok=n_tok),
        grid_spec=pltpu.PrefetchScalarGridSpec(
            num_scalar_prefetch=2,
            grid=(n_tok // DISPATCH_TOKENS,),
            in_specs=[pl.BlockSpec((DISPATCH_TOKENS, d), lambda i, dr, tr: (jnp.minimum(i, ctx_tiles - 1), 0)),
                      pl.BlockSpec((DISPATCH_TOKENS, d), lambda i, dr, tr: (jnp.maximum(i - ctx_tiles, 0), 0))],
            out_specs=pl.BlockSpec(memory_space=pl.ANY),
            scratch_shapes=[pltpu.VMEM((MOE_ROW_BLOCK, d), h_ctx.dtype), pltpu.SemaphoreType.DMA((2,))]),
        out_shape=jax.ShapeDtypeStruct((padded_rows, d), h_ctx.dtype),
        compiler_params=_cparams(("arbitrary",)),
        name="dispatch",
    )(dest_flat, tail, h_ctx, h_lat)


def _moe_kernel(item_e, item_row0, item_nblk, xs, w1_ref, w3_ref, w2_ref, ys,
                x_in, x_bf, acc, y_out, w1_bf, w3_bf, w2_bf, sem, *, n_chunks):
    i = pl.program_id(0)
    c = pl.program_id(1)
    n_items = pl.num_programs(0)
    last_c = n_chunks - 1
    nblk = item_nblk[i]
    max_blk = MOE_ROW_GROUP // MOE_ROW_BLOCK
    blk = lambda b: pl.ds(b * MOE_ROW_BLOCK, MOE_ROW_BLOCK)

    def rows_of(item, b):
        return pl.ds(pl.multiple_of(item_row0[item], MOE_ROW_BLOCK) + b * MOE_ROW_BLOCK, MOE_ROW_BLOCK)

    load = lambda item, b: pltpu.make_async_copy(xs.at[rows_of(item, b), :], x_in.at[blk(b), :], sem.at[0])
    store = lambda item, b: pltpu.make_async_copy(y_out.at[blk(b), :], ys.at[rows_of(item, b), :], sem.at[1])

    def for_blocks(item, fn):
        n = item_nblk[item]
        for b in range(max_blk):
            @pl.when(b < n)
            def _():
                fn(item, b)

    @pl.when(c == 0)
    def _():
        @pl.when(i == 0)
        def _():
            for_blocks(0, lambda it, b: load(it, b).start())

        for_blocks(i, lambda it, b: load(it, b).wait())

    @pl.when((c == last_c) & (i > 0))
    def _():
        for_blocks(i - 1, lambda it, b: store(it, b).wait())

    def run_blocks(first, last):
        def block(rows, w1, w3, w2):
            if first:
                x = _unpack_halves(x_in[rows, :]).astype(BF16)
                x_bf[rows, :] = x
            else:
                x = x_bf[rows, :]
            h1 = _dot(x, w1)
            h3 = _dot(x, w3)
            a = (h1 * jax.nn.sigmoid(h1) * h3).astype(BF16)
            y = _dot(a, w2)
            if not first:
                y = acc[rows, :] + y
            if last:
                y_out[rows, :] = _pack_halves(y)
            else:
                acc[rows, :] = y

        w1 = w1_ref[0].astype(BF16)
        w3 = w3_ref[0].astype(BF16)
        w2 = w2_ref[0].astype(BF16)
        w1_bf[...] = w1
        w3_bf[...] = w3
        w2_bf[...] = w2
        block(blk(0), w1, w3, w2)

        def body(b, carry):
            rows = pl.ds(pl.multiple_of(b * MOE_ROW_BLOCK, MOE_ROW_BLOCK), MOE_ROW_BLOCK)
            block(rows, w1_bf[...], w3_bf[...], w2_bf[...])
            return carry

        lax.fori_loop(1, nblk, body, 0)

    for first, last in sorted({(cc == 0, cc == last_c) for cc in range(n_chunks)}):
        chunk_is = (c == 0) if first else ((c == last_c) if last else ((c > 0) & (c < last_c)))

        @pl.when((nblk > 0) & chunk_is)
        def _():
            run_blocks(first, last)

    @pl.when((c == 0) & (i + 1 < n_items))
    def _():
        for_blocks(i + 1, lambda it, b: load(it, b).start())

    @pl.when(c == last_c)
    def _():
        for_blocks(i, lambda it, b: store(it, b).start())

        @pl.when(i == n_items - 1)
        def _():
            for_blocks(i, lambda it, b: store(it, b).wait())


def _moe_call(item_e, item_row0, item_nblk, xs, w1, w3, w2):
    padded_rows, d_packed = xs.shape
    d = w1.shape[1]
    n_items = item_e.shape[0]
    f = w1.shape[2]
    nc = f // MOE_F_CHUNK
    chunk = lambda i, c, ib: jnp.where(ib[i] > 0, c, nc - 1)
    return pl.pallas_call(
        functools.partial(_moe_kernel, n_chunks=nc),
        grid_spec=pltpu.PrefetchScalarGridSpec(
            num_scalar_prefetch=3,
            grid=(n_items, nc),
            in_specs=[pl.BlockSpec(memory_space=pl.ANY),
                      pl.BlockSpec((1, d, MOE_F_CHUNK), lambda i, c, ie, ir, ib: (ie[i], 0, chunk(i, c, ib))),
                      pl.BlockSpec((1, d, MOE_F_CHUNK), lambda i, c, ie, ir, ib: (ie[i], 0, chunk(i, c, ib))),
                      pl.BlockSpec((1, MOE_F_CHUNK, d), lambda i, c, ie, ir, ib: (ie[i], chunk(i, c, ib), 0))],
            out_specs=pl.BlockSpec(memory_space=pl.ANY),
            scratch_shapes=[pltpu.VMEM((MOE_ROW_GROUP, d_packed), xs.dtype),
                            pltpu.VMEM((MOE_ROW_GROUP, d), BF16),
                            pltpu.VMEM((MOE_ROW_GROUP, d), F32),
                            pltpu.VMEM((MOE_ROW_GROUP, d_packed), xs.dtype),
                            pltpu.VMEM((d, MOE_F_CHUNK), BF16),
                            pltpu.VMEM((d, MOE_F_CHUNK), BF16),
                            pltpu.VMEM((MOE_F_CHUNK, d), BF16),
                            pltpu.SemaphoreType.DMA((2,))]),
        out_shape=jax.ShapeDtypeStruct((padded_rows, d_packed), xs.dtype),
        compiler_params=_cparams(("arbitrary", "arbitrary")),
        name="moe",
    )(item_e, item_row0, item_nblk, xs, w1, w3, w2)


COMBINE_TM = 256


def _combine_kernel(dest_ref, x1_ref, gw_ref, g2_ref, ys, o_ref, y0, y1, sem, *, tok_base, n_tok):
    tok0 = tok_base + pl.program_id(0) * COMBINE_TM

    def issue(g, carry):
        base = pl.multiple_of(g * ROW_DMA_GROUP, ROW_DMA_GROUP)
        for j in range(ROW_DMA_GROUP):
            _row_copy(ys, dest_ref[tok0 + base + j], y0, base + j, sem).start(priority=0)
            _row_copy(ys, dest_ref[n_tok + tok0 + base + j], y1, base + j, sem).start(priority=1)
        return carry

    lax.fori_loop(0, COMBINE_TM // ROW_DMA_GROUP, issue, 0)
    for buf in (y0, y1):
        pltpu.make_async_copy(ys.at[pl.ds(0, COMBINE_TM), :], buf, sem).wait()
    gw = gw_ref[...]
    moe = gw[:, 0:1] * _unpack_halves(y0[...]) + gw[:, 1:2] * _unpack_halves(y1[...])
    o_ref[...] = x1_ref[...] + g2_ref[0] * moe


def _combine_call(dest_flat, x1, gw, g2, ys, tok_base, n_tok, tiles_per_group):
    m, d = x1.shape
    return pl.pallas_call(
        functools.partial(_combine_kernel, tok_base=tok_base, n_tok=n_tok),
        grid_spec=pltpu.PrefetchScalarGridSpec(
            num_scalar_prefetch=1,
            grid=(m // COMBINE_TM,),
            in_specs=[pl.BlockSpec((COMBINE_TM, d), lambda i, dr: (i, 0)),
                      pl.BlockSpec((COMBINE_TM, 2), lambda i, dr: (i, 0)),
                      pl.BlockSpec((1, 1, d), lambda i, dr: (i // tiles_per_group, 0, 0)),
                      pl.BlockSpec(memory_space=pl.ANY)],
            out_specs=pl.BlockSpec((COMBINE_TM, d), lambda i, dr: (i, 0)),
            scratch_shapes=[pltpu.VMEM((COMBINE_TM,) + ys.shape[1:], ys.dtype),
                            pltpu.VMEM((COMBINE_TM,) + ys.shape[1:], ys.dtype),
                            pltpu.SemaphoreType.DMA]),
        out_shape=jax.ShapeDtypeStruct((m, d), F32),
        compiler_params=_cparams(("arbitrary",)),
        name="combine",
    )(dest_flat, x1, gw, g2, ys)


def kernel(x_prompt, x_sample, cache_a_k, cache_a_v, cache_b_k, cache_b_v, c, c_ctx, norm1_w, norm2_w, w_ada, b_ada, w_in, qn_a, kn_a, qn_b, kn_b, rpb_a, sink_b, w_pa, w_pb, w_out, w_rg, b_rg, w_re, b_re, w1, w3, w2):
    batch, seq, d = x_prompt.shape
    dec_batch, dec_seq, _ = x_sample.shape
    depth = norm1_w.shape[0]
    assert depth == 1, "one trunk layer"
    past = cache_a_k.shape[2]
    n_ctx, n_lat = batch * seq, dec_batch * dec_seq
    n_tok = n_ctx + n_lat

    xc = x_prompt.reshape(n_ctx, d)
    xl = x_sample.reshape(n_lat, d)

    cond = jnp.concatenate([c_ctx[None, :], c], axis=0)
    mod = _ada_call(cond, w_ada[0], b_ada[0][None, :])
    sh1, sc1, g1, sh2, sc2, g2 = [mod[:, i * d:(i + 1) * d][:, None, :] for i in range(6)]
    ctx_rows, lat_rows = slice(0, 1), slice(1, 1 + dec_batch)

    nw1, nw2 = norm1_w[0][None, :], norm2_w[0][None, :]
    qna, kna, qnb, knb = qn_a[0][None, :], kn_a[0][None, :], qn_b[0][None, :], kn_b[0][None, :]
    sink = sink_b[0]
    wout_bf = w_out[0].astype(BF16)
    wr = jnp.zeros((d, LOGIT_PAD), F32).at[:, :N_EXPERTS].set(w_re[0]).at[:, N_EXPERTS:N_EXPERTS + N_GROUPS].set(w_rg[0])
    br = jnp.zeros((LOGIT_PAD, 1), F32).at[:N_EXPERTS, 0].set(b_re[0]).at[N_EXPERTS:N_EXPERTS + N_GROUPS, 0].set(b_rg[0])

    proj_c, gates_c = _inproj_call(xc, nw1, sc1[ctx_rows], sh1[ctx_rows], w_in[0], n_ctx)
    proj_l, gates_l = _inproj_call(xl, nw1, sc1[lat_rows], sh1[lat_rows], w_in[0], dec_seq)

    oa_c, ob_c, new_a_k, new_a_v, new_b_k, new_b_v = _ctx_attn_call(proj_c, seq, qna, kna, qnb, knb, sink)

    bias_blocks, pair0, bias_index = _na_bias_blocks(rpb_a[0], dec_seq)
    cos, sin_a, sin_b = _rope_tables(dec_seq)
    rows_of = lambda cache: cache.reshape(-1, HEAD_DIM)
    oa_l = _lat_attn_a_call(proj_l, dec_seq, past, rows_of(cache_a_k), rows_of(cache_a_v),
                            bias_blocks, pair0, bias_index, qna, kna)
    ob_l = _lat_attn_b_call(proj_l, dec_seq, past, rows_of(cache_b_k), rows_of(cache_b_v),
                            cos, sin_a, sin_b, qnb, knb, sink)

    mix_c = _mix_call(oa_c, ob_c, gates_c, w_pa[0], w_pb[0])
    mix_l = _mix_call(oa_l, ob_l, gates_l, w_pa[0], w_pb[0])
    x1_c, h2_c, lg_c = _outproj_call(mix_c, wout_bf, xc, g1[ctx_rows], sc2[ctx_rows], sh2[ctx_rows],
                                     nw2, wr, n_ctx // MERGE_TM)
    x1_l, h2_l, lg_l = _outproj_call(mix_l, wout_bf, xl, g1[lat_rows], sc2[lat_rows], sh2[lat_rows],
                                     nw2, wr, dec_seq // MERGE_TM)

    eid, gw, rank, cnt = _route_call(jnp.concatenate([lg_c, lg_l], axis=0), br)
    padded_rows, n_items = _moe_layout(2 * n_tok)
    dest, tail, item_e, item_row0, item_nblk = _routing_tables(eid[:2], rank[:2], cnt[:, 0], n_items)
    dest_flat = dest.reshape(-1)
    xs = _dispatch_call(dest_flat, tail, h2_c, h2_l, padded_rows)
    ys = _moe_call(item_e, item_row0, item_nblk, xs, w1[0], w3[0], w2[0])
    gw_t = gw[:2].T
    y_c = _combine_call(dest_flat, x1_c, gw_t[:n_ctx], g2[ctx_rows], ys, 0, n_tok, n_ctx // COMBINE_TM)
    y_l = _combine_call(dest_flat, x1_l, gw_t[n_ctx:], g2[lat_rows], ys, n_ctx, n_tok, dec_seq // COMBINE_TM)

    state = lambda a, heads: a.reshape(batch, 1, seq, heads, HEAD_DIM)
    return (y_c.reshape(batch, seq, d), y_l.reshape(dec_batch, dec_seq, d),
            state(new_a_k, NA_HEADS), state(new_a_v, NA_HEADS),
            state(new_b_k, NB_KV_HEADS), state(new_b_v, NB_KV_HEADS))
```

```python
import functools

import jax
import jax.numpy as jnp
import numpy as np
from jax import lax
from jax.experimental import pallas as pl
from jax.experimental.pallas import tpu as pltpu

D_MODEL = 2048
HEAD_DIM = 128
NA_HEADS = 8
NA_WIDTH = NA_HEADS * HEAD_DIM
NB_Q_HEADS = 8
NB_KV_HEADS = 2
NB_GROUP = NB_Q_HEADS // NB_KV_HEADS
NB_WIDTH = NB_Q_HEADS * HEAD_DIM
NB_KV_WIDTH = NB_KV_HEADS * HEAD_DIM
GRID_W = 64
NA_WIN_ROWS = 8
NA_WIN_COLS = 16
WINDOW = 128
N_GROUPS = 4
EXPERTS_PER_GROUP = 8
N_EXPERTS = N_GROUPS * EXPERTS_PER_GROUP
D_EXPERT = 1024
IN_WIDTH = 3 * NA_WIDTH + NB_WIDTH + 2 * NB_KV_WIDTH + 2 * D_MODEL
ROPE_BASE = 10000.0
NORM_EPS = 1e-6
NEG_INF = -1e30
ATTN_SCALE = HEAD_DIM ** -0.5

QA_HEAD0 = 0
KA_HEAD0 = NA_HEADS
VA_HEAD0 = 2 * NA_HEADS
QB_HEAD0 = 3 * NA_HEADS
KB_HEAD0 = QB_HEAD0 + NB_Q_HEADS
VB_HEAD0 = KB_HEAD0 + NB_KV_HEADS
GATE_COL0 = (VB_HEAD0 + NB_KV_HEADS) * HEAD_DIM

LOGIT_PAD = 128
MOE_ROW_BLOCK = 256
MOE_ROW_GROUP = 1024
MOE_F_CHUNK = 512
VMEM_LIMIT = 56 * 1024 * 1024

F32 = jnp.float32
BF16 = jnp.bfloat16


def _cparams(sem, vmem_limit=VMEM_LIMIT):
    return pltpu.CompilerParams(dimension_semantics=sem, vmem_limit_bytes=vmem_limit)


def _rms(x, w):
    x = x.astype(F32)
    return x * lax.rsqrt(jnp.mean(x * x, axis=-1, keepdims=True) + NORM_EPS) * w


def _dot(a, b):
    return jnp.dot(a, b, preferred_element_type=F32)


def _dot_nt(a, b):
    return lax.dot_general(a, b, (((1,), (1,)), ((), ())), preferred_element_type=F32)


def _pack_halves(x):
    n = x.shape[1] // 2
    lo = lax.bitcast_convert_type(x[:, :n].astype(BF16).astype(F32), jnp.uint32)
    hi = lax.bitcast_convert_type(x[:, n:].astype(BF16).astype(F32), jnp.uint32)
    return hi | (lo >> 16)


def _unpack_halves(w):
    lo = lax.bitcast_convert_type(w << 16, F32)
    hi = lax.bitcast_convert_type(w & jnp.uint32(0xFFFF0000), F32)
    return jnp.concatenate([lo, hi], axis=1)


ROW_TILE = (8, 128)


def _rows_to_tiles(w):
    return w.reshape((w.shape[0],) + ROW_TILE)


def _tiles_to_rows(t):
    return t.reshape(t.shape[0], ROW_TILE[0] * ROW_TILE[1])


def _dot_split(a, b):
    a_hi = a.astype(BF16)
    a_lo = (a - a_hi.astype(F32)).astype(BF16)
    b_hi = b.astype(BF16)
    b_lo = (b - b_hi.astype(F32)).astype(BF16)
    return _dot(a_hi, b_hi) + (_dot(a_lo, b_hi) + _dot(a_hi, b_lo))


ADA_ROWS = 8
ADA_TN = 1024


def _ada_kernel(c_ref, w_ref, b_ref, o_ref):
    n_rows, d, lanes = c_ref.shape
    tn = w_ref.shape[1]

    def body(kb, acc):
        ks = pl.ds(pl.multiple_of(kb * 8, 8), 8)
        w = w_ref[ks, :]
        out = []
        for r in range(n_rows):
            c = c_ref[r, ks, :]
            s = c * jax.nn.sigmoid(c)
            out.append(acc[r] + w * jnp.concatenate([s] * (tn // lanes), axis=1))
        return tuple(out)

    acc = lax.fori_loop(0, d // 8, body, tuple(jnp.zeros((8, tn), F32) for _ in range(n_rows)), unroll=8)
    ri = lax.broadcasted_iota(jnp.int32, (ADA_ROWS, tn), 0)
    res = jnp.zeros((ADA_ROWS, tn), F32)
    for r in range(n_rows):
        row = jnp.sum(acc[r], axis=0, keepdims=True) + b_ref[...]
        res = jnp.where(ri == r, row, res)
    o_ref[...] = res


def _ada_call(cond, w_ada, b_ada):
    n_rows, d = cond.shape
    n = w_ada.shape[1]
    lanes = 128
    cond_lanes = jnp.broadcast_to(cond[:, :, None], (n_rows, d, lanes))
    return pl.pallas_call(
        _ada_kernel,
        grid=(n // ADA_TN,),
        in_specs=[pl.BlockSpec((n_rows, d, lanes), lambda j: (0, 0, 0)),
                  pl.BlockSpec((d, ADA_TN), lambda j: (0, j)),
                  pl.BlockSpec((1, ADA_TN), lambda j: (0, j))],
        out_specs=pl.BlockSpec((ADA_ROWS, ADA_TN), lambda j: (0, j)),
        out_shape=jax.ShapeDtypeStruct((ADA_ROWS, n), F32),
        compiler_params=_cparams(("arbitrary",)),
        name="ada",
    )(cond_lanes, w_ada, b_ada)


INPROJ_TM = 2048
INPROJ_TN = 512
NORM_ROWS = 128


def _inproj_kernel(x_hbm, nw_ref, sc_ref, sh_ref, w_ref, qkv_ref, gate_ref, x_buf, h_scr, sem, *, qkv_tiles):
    i = pl.program_id(0)
    j = pl.program_id(1)
    tm = x_buf.shape[0]
    fetch = lambda tile: pltpu.make_async_copy(
        x_hbm.at[pl.ds(pl.multiple_of(tile * tm, tm), tm), :], x_buf, sem)

    @pl.when(j == 0)
    def _():
        @pl.when(i == 0)
        def _():
            fetch(0).start()

        fetch(i).wait()
        nw = nw_ref[...]
        sc = 1.0 + sc_ref[0]
        sh = sh_ref[0]

        def body(r, carry):
            rows = pl.ds(pl.multiple_of(r * NORM_ROWS, NORM_ROWS), NORM_ROWS)
            h_scr[rows, :] = (_rms(x_buf[rows, :], nw) * sc + sh).astype(BF16)
            return carry

        lax.fori_loop(0, tm // NORM_ROWS, body, 0)

    @pl.when((j == 1) & (i + 1 < pl.num_programs(0)))
    def _():
        fetch(i + 1).start()

    res = _dot(h_scr[...], w_ref[...].astype(BF16))

    @pl.when(j < qkv_tiles)
    def _():
        qkv_ref[...] = res.astype(qkv_ref.dtype)

    @pl.when(j >= qkv_tiles)
    def _():
        gate_ref[...] = res.astype(gate_ref.dtype)


def _inproj_call(x, nw, sc, sh, w_in, rows_per_group):
    m, d = x.shape
    n = w_in.shape[1]
    assert n // INPROJ_TN >= 2, "the next token tile is requested in the second column step"
    tm = min(INPROJ_TM, rows_per_group)
    tiles_per_group = rows_per_group // tm
    qkv_tiles = GATE_COL0 // INPROJ_TN
    grp = lambda i, j: (i // tiles_per_group, 0, 0)
    return pl.pallas_call(
        functools.partial(_inproj_kernel, qkv_tiles=qkv_tiles),
        grid=(m // tm, n // INPROJ_TN),
        in_specs=[pl.BlockSpec(memory_space=pl.ANY),
                  pl.BlockSpec((1, d), lambda i, j: (0, 0)),
                  pl.BlockSpec((1, 1, d), grp),
                  pl.BlockSpec((1, 1, d), grp),
                  pl.BlockSpec((d, INPROJ_TN), lambda i, j: (0, j))],
        out_specs=[pl.BlockSpec((tm, INPROJ_TN), lambda i, j: (i, jnp.minimum(j, qkv_tiles - 1))),
                   pl.BlockSpec((tm, INPROJ_TN), lambda i, j: (i, jnp.maximum(j - qkv_tiles, 0)))],
        out_shape=[jax.ShapeDtypeStruct((m, GATE_COL0), BF16),
                   jax.ShapeDtypeStruct((m, n - GATE_COL0), BF16)],
        scratch_shapes=[pltpu.VMEM((tm, d), F32), pltpu.VMEM((tm, d), BF16), pltpu.SemaphoreType.DMA],
        compiler_params=_cparams(("arbitrary", "arbitrary")),
        name="inproj",
    )(x, nw, sc, sh, w_in)


def _with_ones(v):
    return jnp.concatenate([v.astype(BF16), jnp.ones(v.shape, BF16)], axis=1)


def _softmax_pv(scores, values_with_ones, sink=None):
    m = None
    for s in scores:
        ms = jnp.max(s, axis=-1, keepdims=True)
        m = ms if m is None else jnp.maximum(m, ms)
    if sink is not None:
        m = jnp.maximum(m, sink)
    acc = None
    for s, v1 in zip(scores, values_with_ones):
        pv = _dot(jnp.exp(s - m).astype(BF16), v1)
        acc = pv if acc is None else acc + pv
    d = acc.shape[1] // 2
    den = acc[:, d:]
    if sink is not None:
        den = den + jnp.exp(sink - m)
    return acc[:, :d] / den


def _ctx_attn_kernel(qa_ref, ka_ref, va_ref, qb_ref, kb_ref, vb_ref,
                     qna_ref, kna_ref, qnb_ref, knb_ref, sink_ref,
                     oa_ref, ob_ref, nak_ref, nav_ref, nbk_ref, nbv_ref):
    qna, kna, qnb, knb = qna_ref[...], kna_ref[...], qnb_ref[...], knb_ref[...]
    seq = qa_ref.shape[0]
    for h in range(NA_HEADS):
        cols = slice(h * HEAD_DIM, (h + 1) * HEAD_DIM)
        q = (_rms(qa_ref[:, cols], qna) * ATTN_SCALE).astype(BF16)
        k = _rms(ka_ref[:, cols], kna)
        v = va_ref[:, cols]
        nak_ref[pl.ds(h, seq, stride=NA_HEADS), :] = k
        nav_ref[pl.ds(h, seq, stride=NA_HEADS), :] = v.astype(F32)
        s = _dot_nt(q, k.astype(BF16))
        oa_ref[:, cols] = _softmax_pv([s], [_with_ones(v)]).astype(oa_ref.dtype)
    for kv in range(NB_KV_HEADS):
        kcols = slice(kv * HEAD_DIM, (kv + 1) * HEAD_DIM)
        k = _rms(kb_ref[:, kcols], knb)
        v = vb_ref[:, kcols]
        nbk_ref[pl.ds(kv, seq, stride=NB_KV_HEADS), :] = k
        nbv_ref[pl.ds(kv, seq, stride=NB_KV_HEADS), :] = v.astype(F32)
        kb16 = k.astype(BF16)
        vb1 = _with_ones(v)
        for g in range(NB_GROUP):
            hq = kv * NB_GROUP + g
            cols = slice(hq * HEAD_DIM, (hq + 1) * HEAD_DIM)
            q = (_rms(qb_ref[:, cols], qnb) * ATTN_SCALE).astype(BF16)
            s = _dot_nt(q, kb16)
            ob_ref[:, cols] = _softmax_pv([s], [vb1], sink=sink_ref[hq]).astype(ob_ref.dtype)


def _ctx_attn_call(proj, seq, qna, kna, qnb, knb, sink):
    m = proj.shape[0]
    nb = m // seq
    wide = lambda blk: pl.BlockSpec((seq, NA_WIDTH), lambda b: (b, blk))
    narrow = lambda blk: pl.BlockSpec((seq, NB_KV_WIDTH), lambda b: (b, blk))
    vec = pl.BlockSpec((1, HEAD_DIM), lambda b: (0, 0))
    return pl.pallas_call(
        _ctx_attn_kernel,
        grid=(nb,),
        in_specs=[wide(QA_HEAD0 // NA_HEADS), wide(KA_HEAD0 // NA_HEADS), wide(VA_HEAD0 // NA_HEADS),
                  wide(QB_HEAD0 // NA_HEADS), narrow(KB_HEAD0 // NB_KV_HEADS), narrow(VB_HEAD0 // NB_KV_HEADS),
                  vec, vec, vec, vec,
                  pl.BlockSpec(memory_space=pltpu.SMEM)],
        out_specs=[pl.BlockSpec((seq, NA_WIDTH), lambda b: (b, 0)),
                   pl.BlockSpec((seq, NB_WIDTH), lambda b: (b, 0)),
                   pl.BlockSpec((seq * NA_HEADS, HEAD_DIM), lambda b: (b, 0)),
                   pl.BlockSpec((seq * NA_HEADS, HEAD_DIM), lambda b: (b, 0)),
                   pl.BlockSpec((seq * NB_KV_HEADS, HEAD_DIM), lambda b: (b, 0)),
                   pl.BlockSpec((seq * NB_KV_HEADS, HEAD_DIM), lambda b: (b, 0))],
        out_shape=[jax.ShapeDtypeStruct((m, NA_WIDTH), BF16),
                   jax.ShapeDtypeStruct((m, NB_WIDTH), BF16),
                   jax.ShapeDtypeStruct((m * NA_HEADS, HEAD_DIM), F32),
                   jax.ShapeDtypeStruct((m * NA_HEADS, HEAD_DIM), F32),
                   jax.ShapeDtypeStruct((m * NB_KV_HEADS, HEAD_DIM), F32),
                   jax.ShapeDtypeStruct((m * NB_KV_HEADS, HEAD_DIM), F32)],
        compiler_params=_cparams(("arbitrary",)),
        name="ctx_attn",
    )(proj, proj, proj, proj, proj, proj, qna, kna, qnb, knb, sink)


def _rope(x, cos, sin_a, sin_b):
    quarter = HEAD_DIM // 4
    return (x * cos + pltpu.roll(x, HEAD_DIM - quarter, 1) * sin_a
            + pltpu.roll(x, quarter, 1) * sin_b)


def _head_rows(cache_ref, head, n_heads):
    past = cache_ref.shape[0] // n_heads
    return cache_ref[pl.ds(head, past, stride=n_heads), :]


def _lat_attn_a_kernel(q_ref, k_ref, v_ref, ck_ref, cv_ref, cb_ref, qn_ref, kn_ref, o_ref, *, pair0, index):
    head = pl.program_id(0)
    q = (_rms(q_ref[...], qn_ref[...]) * ATTN_SCALE).astype(BF16)
    k = _rms(k_ref[...], kn_ref[...]).astype(BF16)
    v1 = _with_ones(v_ref[...])
    ck = _head_rows(ck_ref, head, NA_HEADS).astype(BF16)
    cv1 = _with_ones(_head_rows(cv_ref, head, NA_HEADS))
    pair = 2 * GRID_W
    rows_per_block = pair // GRID_W
    n_pairs = len(index[0])
    for blk in range(len(pair0) // rows_per_block):
        grid_rows = range(blk * rows_per_block, (blk + 1) * rows_per_block)
        p0 = pair0[grid_rows[0]]
        assert all(pair0[r] == p0 for r in grid_rows), "query rows of a block share their key pairs"
        rows = slice(blk * pair, (blk + 1) * pair)
        keys = slice(p0 * pair, (p0 + n_pairs) * pair)
        bias = jnp.concatenate(
            [jnp.concatenate([cb_ref[u] for u in index[r]], axis=1) for r in grid_rows], axis=0)
        qs = q[rows]
        s_win = _dot_nt(qs, k[keys]) + bias
        s_ctx = _dot_nt(qs, ck)
        o_ref[rows, :] = _softmax_pv([s_win, s_ctx], [v1[keys], cv1]).astype(o_ref.dtype)


def _lat_attn_b_kernel(q_ref, k_ref, v_ref, ck_ref, cv_ref, cos_ref, sina_ref, sinb_ref,
                       qn_ref, kn_ref, sink_ref, o_ref):
    cos, sin_a, sin_b = cos_ref[...], sina_ref[...], sinb_ref[...]
    q = _rope(_rms(q_ref[...], qn_ref[...]), cos, sin_a, sin_b)
    k = _rope(_rms(k_ref[...], kn_ref[...]), cos, sin_a, sin_b)
    q = (q * ATTN_SCALE).astype(BF16)
    k = k.astype(BF16)
    v1 = _with_ones(v_ref[...])
    length = q.shape[0]
    kv = pl.program_id(0) // NB_GROUP
    ck = _head_rows(ck_ref, kv, NB_KV_HEADS).astype(BF16)
    cv1 = _with_ones(_head_rows(cv_ref, kv, NB_KV_HEADS))
    sink = sink_ref[pl.program_id(0)]
    for qb in range(length // WINDOW):
        rows = slice(qb * WINDOW, (qb + 1) * WINDOW)
        lo, hi = max(0, (qb - 1) * WINDOW), min(length, (qb + 2) * WINDOW)
        qs = q[rows]
        s_win = _dot_nt(qs, k[lo:hi])
        qi = qb * WINDOW + lax.broadcasted_iota(jnp.int32, s_win.shape, 0)
        kj = lo + lax.broadcasted_iota(jnp.int32, s_win.shape, 1)
        s_win = jnp.where(jnp.abs(qi - kj) <= WINDOW, s_win, NEG_INF)
        s_ctx = _dot_nt(qs, ck)
        o_ref[rows, :] = _softmax_pv([s_win, s_ctx], [v1[lo:hi], cv1], sink=sink).astype(o_ref.dtype)


def _lat_attn_a_call(proj, length, past, ck, cv, cb, pair0, index, qn, kn):
    m = proj.shape[0]
    head = lambda h0: pl.BlockSpec((length, HEAD_DIM), lambda h, b: (b, h0 + h))
    cache = pl.BlockSpec((past * NA_HEADS, HEAD_DIM), lambda h, b: (b, 0))
    vec = pl.BlockSpec((1, HEAD_DIM), lambda h, b: (0, 0))
    return pl.pallas_call(
        functools.partial(_lat_attn_a_kernel, pair0=pair0, index=index),
        grid=(NA_HEADS, m // length),
        in_specs=[head(QA_HEAD0), head(KA_HEAD0), head(VA_HEAD0), cache, cache,
                  pl.BlockSpec((None,) + cb.shape[1:], lambda h, b: (h, 0, 0, 0)), vec, vec],
        out_specs=pl.BlockSpec((length, HEAD_DIM), lambda h, b: (b, h)),
        out_shape=jax.ShapeDtypeStruct((m, NA_WIDTH), BF16),
        compiler_params=_cparams(("arbitrary", "arbitrary")),
        name="lat_attn_a",
    )(proj, proj, proj, ck, cv, cb, qn, kn)


def _lat_attn_b_call(proj, length, past, ck, cv, cos, sin_a, sin_b, qn, kn, sink):
    m = proj.shape[0]
    qspec = pl.BlockSpec((length, HEAD_DIM), lambda h, b: (b, QB_HEAD0 + h))
    kvspec = lambda h0: pl.BlockSpec((length, HEAD_DIM), lambda h, b: (b, h0 + h // NB_GROUP))
    cache = pl.BlockSpec((past * NB_KV_HEADS, HEAD_DIM), lambda h, b: (b, 0))
    table = pl.BlockSpec((length, HEAD_DIM), lambda h, b: (0, 0))
    vec = pl.BlockSpec((1, HEAD_DIM), lambda h, b: (0, 0))
    return pl.pallas_call(
        _lat_attn_b_kernel,
        grid=(NB_Q_HEADS, m // length),
        in_specs=[qspec, kvspec(KB_HEAD0), kvspec(VB_HEAD0), cache, cache,
                  table, table, table, vec, vec, pl.BlockSpec(memory_space=pltpu.SMEM)],
        out_specs=pl.BlockSpec((length, HEAD_DIM), lambda h, b: (b, h)),
        out_shape=jax.ShapeDtypeStruct((m, NB_WIDTH), BF16),
        compiler_params=_cparams(("arbitrary", "arbitrary")),
        name="lat_attn_b",
    )(proj, proj, proj, ck, cv, cos, sin_a, sin_b, qn, kn, sink)


def _na_bias_blocks(rpb, length):
    rows = length // GRID_W
    kr_n = min(NA_WIN_ROWS, rows)
    n_pairs = min(kr_n // 2 + 1, rows // 2)
    r = np.arange(rows)
    c = np.arange(GRID_W)
    r0 = np.clip(r - kr_n // 2, 0, rows - kr_n)
    c0 = np.clip(c - NA_WIN_COLS // 2, 0, GRID_W - NA_WIN_COLS)
    pair0 = np.minimum(r0 // 2, rows // 2 - n_pairs)
    kr = 2 * (pair0[:, None, None] + np.arange(n_pairs)[None, :, None]) + np.arange(2)[None, None, :]
    row_ok = (kr >= r0[:, None, None]) & (kr < r0[:, None, None] + kr_n)
    col_ok = (c[None, :] >= c0[:, None]) & (c[None, :] < c0[:, None] + NA_WIN_COLS)
    dr = kr - r[:, None, None] + (NA_WIN_ROWS - 1)
    dc = np.clip(c[None, :] - c[:, None], -(NA_WIN_COLS - 1), NA_WIN_COLS - 1) + (NA_WIN_COLS - 1)
    offs = np.where(row_ok, dr, -1).reshape(-1, 2)
    uniq, inverse = np.unique(offs, axis=0, return_inverse=True)
    index = inverse.reshape(rows, n_pairs)
    row_sel = (uniq[:, :, None] == np.arange(2 * NA_WIN_ROWS - 1)[None, None, :]).astype(np.float32)
    col_hit = (dc[None] == np.arange(2 * NA_WIN_COLS - 1)[:, None, None]) & col_ok[None]
    col_sel = np.zeros((2,) + col_hit.shape[:2] + (2 * GRID_W,), np.float32)
    for half in range(2):
        col_sel[half, :, :, half * GRID_W:(half + 1) * GRID_W] = col_hit
    hi = lax.Precision.HIGHEST
    per_col = jnp.einsum("hde,lecn->hldcn", rpb.astype(F32), col_sel, precision=hi)
    table = jnp.einsum("uld,hldcn->hucn", row_sel, per_col, precision=hi)
    valid = ((uniq >= 0)[:, None, :, None] & col_ok[None, :, None, :]).reshape(len(uniq), GRID_W, 2 * GRID_W)
    return (jnp.where(valid[None], table, NEG_INF), tuple(int(p) for p in pair0),
            tuple(tuple(int(u) for u in row) for row in index))


def _rope_tables(length):
    t = jnp.arange(length)
    row = (t // GRID_W).astype(F32)
    col = (t % GRID_W).astype(F32)
    n_freq = HEAD_DIM // 4
    inv = ROPE_BASE ** (-jnp.arange(n_freq, dtype=F32) / n_freq)
    ar = row[:, None] * inv
    ac = col[:, None] * inv
    ang = jnp.concatenate([ar, ar, ac, ac], axis=-1)
    cos, sin = jnp.cos(ang), jnp.sin(ang)
    lane = jnp.arange(HEAD_DIM)
    takes_left = ((lane // n_freq) % 2 == 0)[None, :]
    return cos, jnp.where(takes_left, -sin, 0.0), jnp.where(takes_left, 0.0, sin)


MERGE_TM = 512


def _resident(shape):
    zeros = (0,) * len(shape)
    return pl.BlockSpec(shape, lambda i: zeros, pipeline_mode=pl.Buffered(1))


def _mix_kernel(oa_ref, ob_ref, ga_ref, gb_ref, wpa_ref, wpb_ref, mix_ref):
    ya = _dot(oa_ref[...], wpa_ref[...].astype(BF16))
    yb = _dot(ob_ref[...], wpb_ref[...].astype(BF16))
    mix = (jax.nn.sigmoid(ga_ref[...].astype(F32)) * ya
           + jax.nn.sigmoid(gb_ref[...].astype(F32)) * yb)
    mix_ref[...] = mix.astype(mix_ref.dtype)


def _mix_call(oa, ob, gates, w_pa, w_pb):
    m = oa.shape[0]
    d = w_pa.shape[1]
    row = lambda i: (i, 0)
    return pl.pallas_call(
        _mix_kernel,
        grid=(m // MERGE_TM,),
        in_specs=[pl.BlockSpec((MERGE_TM, NA_WIDTH), row),
                  pl.BlockSpec((MERGE_TM, NB_WIDTH), row),
                  pl.BlockSpec((MERGE_TM, d), lambda i: (i, 0)),
                  pl.BlockSpec((MERGE_TM, d), lambda i: (i, 1)),
                  _resident((NA_WIDTH, d)), _resident((NB_WIDTH, d))],
        out_specs=pl.BlockSpec((MERGE_TM, d), row),
        out_shape=jax.ShapeDtypeStruct((m, d), BF16),
        compiler_params=_cparams(("arbitrary",)),
        name="mix",
    )(oa, ob, gates, gates, w_pa, w_pb)


def _outproj_kernel(mix_ref, wout_ref, x_ref, g1_ref, sc2_ref, sh2_ref, n2w_ref, wr_ref,
                    x1_ref, h2_ref, lg_ref, acc_ref):
    i = pl.program_id(0)
    last = pl.num_programs(0) - 1

    def finish(prod):
        x1 = x_ref[...] + g1_ref[0] * prod
        x1_ref[...] = x1
        h2 = _rms(x1, n2w_ref[...]) * (1.0 + sc2_ref[0]) + sh2_ref[0]
        h2_ref[...] = _rows_to_tiles(_pack_halves(h2))
        lg_ref[...] = _dot_split(h2, wr_ref[...])

    @pl.when(i == 0)
    def _():
        acc_ref[...] = _dot(mix_ref[...], wout_ref[...])

    @pl.when((i > 0) & (i < last))
    def _():
        prod = acc_ref[...]
        acc_ref[...] = _dot(mix_ref[...], wout_ref[...])
        finish(prod)

    @pl.when(i == last)
    def _():
        finish(acc_ref[...])


def _outproj_call(mix, wout_bf, x, g1, sc2, sh2, n2w, wr, tiles_per_group):
    m, d = x.shape
    n_tiles = m // MERGE_TM
    ahead = lambda i: (jnp.minimum(i, n_tiles - 1), 0)
    done = lambda i: (jnp.maximum(i - 1, 0), 0)
    grp = lambda i: (jnp.maximum(i - 1, 0) // tiles_per_group, 0, 0)
    return pl.pallas_call(
        _outproj_kernel,
        grid=(n_tiles + 1,),
        in_specs=[pl.BlockSpec((MERGE_TM, d), ahead),
                  _resident((d, d)),
                  pl.BlockSpec((MERGE_TM, d), done),
                  pl.BlockSpec((1, 1, d), grp), pl.BlockSpec((1, 1, d), grp), pl.BlockSpec((1, 1, d), grp),
                  pl.BlockSpec((1, d), lambda i: (0, 0)),
                  _resident((d, LOGIT_PAD))],
        out_specs=[pl.BlockSpec((MERGE_TM, d), done),
                   pl.BlockSpec((MERGE_TM,) + ROW_TILE, lambda i: (jnp.maximum(i - 1, 0), 0, 0)),
                   pl.BlockSpec((MERGE_TM, LOGIT_PAD), done)],
        out_shape=[jax.ShapeDtypeStruct((m, d), F32),
                   jax.ShapeDtypeStruct((m,) + ROW_TILE, jnp.uint32),
                   jax.ShapeDtypeStruct((m, LOGIT_PAD), F32)],
        scratch_shapes=[pltpu.VMEM((MERGE_TM, d), F32)],
        compiler_params=_cparams(("arbitrary",)),
        name="outproj",
    )(mix, wout_bf, x, g1, sc2, sh2, n2w, wr)


ROUTE_TM = 512


def _first_index_of_max(vals, idx, n):
    mx = jnp.max(vals, axis=0, keepdims=True)
    first = jnp.min(jnp.where(vals == mx, idx, n), axis=0, keepdims=True)
    return mx, first


def _route_kernel(lg_ref, bias_ref, eid_ref, gw_ref, rank_ref, cnt_ref, base_ref):
    step = pl.program_id(0)

    @pl.when(step == 0)
    def _():
        base_ref[...] = jnp.zeros_like(base_ref)

    lt = lg_ref[...].T + bias_ref[...]
    n_tok = lt.shape[1]
    le = lt[0:N_EXPERTS]
    lgrp = lt[N_EXPERTS:N_EXPERTS + N_GROUPS]
    gi = lax.broadcasted_iota(jnp.int32, (N_GROUPS, n_tok), 0)
    gmax, gsel = _first_index_of_max(lgrp, gi, N_GROUPS)
    pg_sel = 1.0 / jnp.sum(jnp.exp(lgrp - gmax), axis=0, keepdims=True)
    le_sel = jnp.zeros((EXPERTS_PER_GROUP, n_tok), F32)
    for g in range(N_GROUPS):
        le_sel = jnp.where(gsel == g, le[g * EXPERTS_PER_GROUP:(g + 1) * EXPERTS_PER_GROUP], le_sel)
    ei = lax.broadcasted_iota(jnp.int32, (EXPERTS_PER_GROUP, n_tok), 0)
    v0, i0 = _first_index_of_max(le_sel, ei, EXPERTS_PER_GROUP)
    rest = jnp.where(ei == i0, -jnp.inf, le_sel)
    v1, i1 = _first_index_of_max(rest, ei, EXPERTS_PER_GROUP)
    e1 = jnp.exp(v1 - v0)
    w0 = pg_sel / (1.0 + e1)
    w1 = pg_sel * e1 / (1.0 + e1)
    eid0 = gsel * EXPERTS_PER_GROUP + i0
    eid1 = gsel * EXPERTS_PER_GROUP + i1

    xi = lax.broadcasted_iota(jnp.int32, (N_EXPERTS, n_tok), 0)
    si = lax.broadcasted_iota(jnp.int32, (n_tok, n_tok), 0)
    ti = lax.broadcasted_iota(jnp.int32, (n_tok, n_tok), 1)
    before = (si < ti).astype(BF16)
    base = base_ref[...]
    hot0 = (xi == eid0).astype(F32)
    hot1 = (xi == eid1).astype(F32)
    pre0 = _dot(hot0.astype(BF16), before)
    pre1 = _dot(hot1.astype(BF16), before)
    tot0 = jnp.sum(hot0, axis=1, keepdims=True)
    tot1 = jnp.sum(hot1, axis=1, keepdims=True)
    rank0 = jnp.sum(hot0 * (base + pre0), axis=0, keepdims=True)
    rank1 = jnp.sum(hot1 * (base + tot0 + pre1), axis=0, keepdims=True)
    base = base + tot0 + tot1
    base_ref[...] = base

    ri = lax.broadcasted_iota(jnp.int32, (8, n_tok), 0)
    pick = lambda a, b: jnp.where(ri == 0, a, jnp.where(ri == 1, b, jnp.zeros_like(a)))
    eid_ref[...] = pick(eid0, eid1)
    gw_ref[...] = pick(w0, w1)
    rank_ref[...] = pick(rank0, rank1).astype(jnp.int32)
    cnt_ref[...] = jnp.broadcast_to(base, cnt_ref.shape).astype(jnp.int32)


def _route_call(logits, bias_col):
    t = logits.shape[0]
    tok = pl.BlockSpec((8, ROUTE_TM), lambda i: (0, i))
    return pl.pallas_call(
        _route_kernel,
        grid=(t // ROUTE_TM,),
        in_specs=[pl.BlockSpec((ROUTE_TM, LOGIT_PAD), lambda i: (i, 0)),
                  pl.BlockSpec((LOGIT_PAD, 1), lambda i: (0, 0))],
        out_specs=[tok, tok, tok, pl.BlockSpec((N_EXPERTS, 128), lambda i: (0, 0))],
        out_shape=[jax.ShapeDtypeStruct((8, t), jnp.int32),
                   jax.ShapeDtypeStruct((8, t), F32),
                   jax.ShapeDtypeStruct((8, t), jnp.int32),
                   jax.ShapeDtypeStruct((N_EXPERTS, 128), jnp.int32)],
        scratch_shapes=[pltpu.VMEM((N_EXPERTS, 1), F32)],
        compiler_params=_cparams(("arbitrary",)),
        name="route",
    )(logits, bias_col)


def _moe_layout(n_pairs):
    padded_rows = -(-(n_pairs + N_EXPERTS * (MOE_ROW_BLOCK - 1)) // MOE_ROW_BLOCK) * MOE_ROW_BLOCK
    n_items = (padded_rows + N_EXPERTS * (MOE_ROW_GROUP - MOE_ROW_BLOCK)) // MOE_ROW_GROUP
    return padded_rows, n_items


def _routing_tables(eid, rank, counts, n_items):
    padded = (counts + MOE_ROW_BLOCK - 1) // MOE_ROW_BLOCK * MOE_ROW_BLOCK
    pad_end = jnp.cumsum(padded)
    pad_start = pad_end - padded
    hot = eid[..., None] == jnp.arange(N_EXPERTS, dtype=jnp.int32)
    dest = (jnp.sum(jnp.where(hot, pad_start, 0), axis=-1) + rank).astype(jnp.int32)
    tail = jnp.where(padded > counts, pad_end - MOE_ROW_BLOCK, -1).astype(jnp.int32)
    per_expert = (padded + MOE_ROW_GROUP - 1) // MOE_ROW_GROUP
    item_end = jnp.cumsum(per_expert)
    item_start = item_end - per_expert
    total = item_end[-1]
    ii = jnp.arange(n_items, dtype=jnp.int32)
    e_of = jnp.minimum(jnp.searchsorted(item_end, ii, side="right"), N_EXPERTS - 1).astype(jnp.int32)
    valid = ii < total
    e_last = e_of[jnp.maximum(total - 1, 0)]
    local = ii - item_start[e_of]
    row0 = pad_start[e_of] + local * MOE_ROW_GROUP
    nblk = jnp.clip((padded[e_of] - local * MOE_ROW_GROUP) // MOE_ROW_BLOCK, 0, MOE_ROW_GROUP // MOE_ROW_BLOCK)
    item_e = jnp.where(valid, e_of, e_last).astype(jnp.int32)
    item_row0 = jnp.where(valid, row0, 0).astype(jnp.int32)
    item_nblk = jnp.where(valid, nblk, 0).astype(jnp.int32)
    return dest, tail, item_e, item_row0, item_nblk


DISPATCH_TOKENS = 256
ROW_DMA_GROUP = 8


def _row_copy(src, s, dst, d, sem):
    return pltpu.make_async_copy(src.at[pl.ds(s, 1)], dst.at[pl.ds(d, 1)], sem)


def _dispatch_kernel(dest_ref, tail_ref, h_ctx, h_lat, xs, zero_buf, sem, *, n_ctx, n_tok):
    step = pl.program_id(0)
    tail_copy = lambda e: pltpu.make_async_copy(
        zero_buf, xs.at[pl.ds(pl.multiple_of(tail_ref[e], MOE_ROW_BLOCK), MOE_ROW_BLOCK)], sem.at[1])

    @pl.when(step == 0)
    def _():
        zero_buf[...] = jnp.zeros_like(zero_buf)
        for e in range(N_EXPERTS):
            @pl.when(tail_ref[e] >= 0)
            def _():
                tail_copy(e).start()
        for e in range(N_EXPERTS):
            @pl.when(tail_ref[e] >= 0)
            def _():
                tail_copy(e).wait()

    tok0 = step * DISPATCH_TOKENS

    def scatter(src):
        def issue(g, carry):
            base = pl.multiple_of(g * ROW_DMA_GROUP, ROW_DMA_GROUP)
            for j in range(ROW_DMA_GROUP):
                _row_copy(src, base + j, xs, dest_ref[tok0 + base + j], sem.at[0]).start(priority=0)
                _row_copy(src, base + j, xs, dest_ref[n_tok + tok0 + base + j], sem.at[0]).start(priority=1)
            return carry

        lax.fori_loop(0, DISPATCH_TOKENS // ROW_DMA_GROUP, issue, 0)
        for _ in range(2):
            pltpu.make_async_copy(src, xs.at[pl.ds(0, DISPATCH_TOKENS)], sem.at[0]).wait()

    @pl.when(tok0 < n_ctx)
    def _():
        scatter(h_ctx)

    @pl.when(tok0 >= n_ctx)
    def _():
        scatter(h_lat)


def _dispatch_call(dest_flat, tail, h_ctx, h_lat, padded_rows):
    n_ctx = h_ctx.shape[0]
    row = h_ctx.shape[1:]
    n_tok = n_ctx + h_lat.shape[0]
    ctx_tiles = n_ctx // DISPATCH_TOKENS
    return pl.pallas_call(
        functools.partial(_dispatch_kernel, n_ctx=n_ctx, n_tok=n_tok),
        grid_spec=pltpu.PrefetchScalarGridSpec(
            num_scalar_prefetch=2,
            grid=(n_tok // DISPATCH_TOKENS,),
            in_specs=[pl.BlockSpec((DISPATCH_TOKENS,) + row, lambda i, dr, tr: (jnp.minimum(i, ctx_tiles - 1), 0, 0)),
                      pl.BlockSpec((DISPATCH_TOKENS,) + row, lambda i, dr, tr: (jnp.maximum(i - ctx_tiles, 0), 0, 0))],
            out_specs=pl.BlockSpec(memory_space=pl.ANY),
            scratch_shapes=[pltpu.VMEM((MOE_ROW_BLOCK,) + row, h_ctx.dtype), pltpu.SemaphoreType.DMA((2,))]),
        out_shape=jax.ShapeDtypeStruct((padded_rows,) + row, h_ctx.dtype),
        compiler_params=_cparams(("arbitrary",)),
        name="dispatch",
    )(dest_flat, tail, h_ctx, h_lat)


def _moe_kernel(item_e, item_row0, item_nblk, xs, w1_ref, w3_ref, w2_ref, ys,
                x_in, x_bf, acc, y_out, w1_bf, w3_bf, w2_bf, sem, *, n_chunks):
    i = pl.program_id(0)
    c = pl.program_id(1)
    n_items = pl.num_programs(0)
    last_c = n_chunks - 1
    nblk = item_nblk[i]
    max_blk = MOE_ROW_GROUP // MOE_ROW_BLOCK
    blk = lambda b: pl.ds(b * MOE_ROW_BLOCK, MOE_ROW_BLOCK)

    def rows_of(item, b):
        return pl.ds(pl.multiple_of(item_row0[item], MOE_ROW_BLOCK) + b * MOE_ROW_BLOCK, MOE_ROW_BLOCK)

    load = lambda item, b: pltpu.make_async_copy(xs.at[rows_of(item, b)], x_in.at[blk(b)], sem.at[0])
    store = lambda item, b: pltpu.make_async_copy(y_out.at[blk(b)], ys.at[rows_of(item, b)], sem.at[1])

    def for_blocks(item, fn):
        n = item_nblk[item]
        for b in range(max_blk):
            @pl.when(b < n)
            def _():
                fn(item, b)

    @pl.when(c == 0)
    def _():
        @pl.when(i == 0)
        def _():
            for_blocks(0, lambda it, b: load(it, b).start())

        for_blocks(i, lambda it, b: load(it, b).wait())

    @pl.when((c == last_c) & (i > 0))
    def _():
        for_blocks(i - 1, lambda it, b: store(it, b).wait())

    def run_blocks(first, last):
        def block(rows, w1, w3, w2):
            if first:
                x = _unpack_halves(_tiles_to_rows(x_in[rows])).astype(BF16)
                x_bf[rows, :] = x
            else:
                x = x_bf[rows, :]
            h1 = _dot(x, w1)
            h3 = _dot(x, w3)
            a = (h1 * jax.nn.sigmoid(h1) * h3).astype(BF16)
            y = _dot(a, w2)
            if not first:
                y = acc[rows, :] + y
            if last:
                y_out[rows] = _rows_to_tiles(_pack_halves(y))
            else:
                acc[rows, :] = y

        w1 = w1_ref[0].astype(BF16)
        w3 = w3_ref[0].astype(BF16)
        w2 = w2_ref[0].astype(BF16)
        w1_bf[...] = w1
        w3_bf[...] = w3
        w2_bf[...] = w2
        block(blk(0), w1, w3, w2)

        def body(b, carry):
            rows = pl.ds(pl.multiple_of(b * MOE_ROW_BLOCK, MOE_ROW_BLOCK), MOE_ROW_BLOCK)
            block(rows, w1_bf[...], w3_bf[...], w2_bf[...])
            return carry

        lax.fori_loop(1, nblk, body, 0)

    for first, last in sorted({(cc == 0, cc == last_c) for cc in range(n_chunks)}):
        chunk_is = (c == 0) if first else ((c == last_c) if last else ((c > 0) & (c < last_c)))

        @pl.when((nblk > 0) & chunk_is)
        def _():
            run_blocks(first, last)

    @pl.when((c == 0) & (i + 1 < n_items))
    def _():
        for_blocks(i + 1, lambda it, b: load(it, b).start())

    @pl.when(c == last_c)
    def _():
        for_blocks(i, lambda it, b: store(it, b).start())

        @pl.when(i == n_items - 1)
        def _():
            for_blocks(i, lambda it, b: store(it, b).wait())


def _moe_call(item_e, item_row0, item_nblk, xs, w1, w3, w2):
    padded_rows, row = xs.shape[0], xs.shape[1:]
    d = w1.shape[1]
    n_items = item_e.shape[0]
    f = w1.shape[2]
    nc = f // MOE_F_CHUNK
    chunk = lambda i, c, ib: jnp.where(ib[i] > 0, c, nc - 1)
    return pl.pallas_call(
        functools.partial(_moe_kernel, n_chunks=nc),
        grid_spec=pltpu.PrefetchScalarGridSpec(
            num_scalar_prefetch=3,
            grid=(n_items, nc),
            in_specs=[pl.BlockSpec(memory_space=pl.ANY),
                      pl.BlockSpec((1, d, MOE_F_CHUNK), lambda i, c, ie, ir, ib: (ie[i], 0, chunk(i, c, ib))),
                      pl.BlockSpec((1, d, MOE_F_CHUNK), lambda i, c, ie, ir, ib: (ie[i], 0, chunk(i, c, ib))),
                      pl.BlockSpec((1, MOE_F_CHUNK, d), lambda i, c, ie, ir, ib: (ie[i], chunk(i, c, ib), 0))],
            out_specs=pl.BlockSpec(memory_space=pl.ANY),
            scratch_shapes=[pltpu.VMEM((MOE_ROW_GROUP,) + row, xs.dtype),
                            pltpu.VMEM((MOE_ROW_GROUP, d), BF16),
                            pltpu.VMEM((MOE_ROW_GROUP, d), F32),
                            pltpu.VMEM((MOE_ROW_GROUP,) + row, xs.dtype),
                            pltpu.VMEM((d, MOE_F_CHUNK), BF16),
                            pltpu.VMEM((d, MOE_F_CHUNK), BF16),
                            pltpu.VMEM((MOE_F_CHUNK, d), BF16),
                            pltpu.SemaphoreType.DMA((2,))]),
        out_shape=jax.ShapeDtypeStruct((padded_rows,) + row, xs.dtype),
        compiler_params=_cparams(("arbitrary", "arbitrary")),
        name="moe",
    )(item_e, item_row0, item_nblk, xs, w1, w3, w2)


COMBINE_TM = 256


def _combine_kernel(dest_ref, x1_ref, gw_ref, g2_ref, ys, o_ref, y0, y1, sem, *, tok_base, n_tok):
    tok0 = tok_base + pl.program_id(0) * COMBINE_TM

    def issue(g, carry):
        base = pl.multiple_of(g * ROW_DMA_GROUP, ROW_DMA_GROUP)
        for j in range(ROW_DMA_GROUP):
            _row_copy(ys, dest_ref[tok0 + base + j], y0, base + j, sem).start(priority=0)
            _row_copy(ys, dest_ref[n_tok + tok0 + base + j], y1, base + j, sem).start(priority=1)
        return carry

    lax.fori_loop(0, COMBINE_TM // ROW_DMA_GROUP, issue, 0)
    for buf in (y0, y1):
        pltpu.make_async_copy(ys.at[pl.ds(0, COMBINE_TM)], buf, sem).wait()
    gw = gw_ref[...]
    moe = (gw[:, 0:1] * _unpack_halves(_tiles_to_rows(y0[...]))
           + gw[:, 1:2] * _unpack_halves(_tiles_to_rows(y1[...])))
    o_ref[...] = x1_ref[...] + g2_ref[0] * moe


def _combine_call(dest_flat, x1, gw, g2, ys, tok_base, n_tok, tiles_per_group):
    m, d = x1.shape
    return pl.pallas_call(
        functools.partial(_combine_kernel, tok_base=tok_base, n_tok=n_tok),
        grid_spec=pltpu.PrefetchScalarGridSpec(
            num_scalar_prefetch=1,
            grid=(m // COMBINE_TM,),
            in_specs=[pl.BlockSpec((COMBINE_TM, d), lambda i, dr: (i, 0)),
                      pl.BlockSpec((COMBINE_TM, 2), lambda i, dr: (i, 0)),
                      pl.BlockSpec((1, 1, d), lambda i, dr: (i // tiles_per_group, 0, 0)),
                      pl.BlockSpec(memory_space=pl.ANY)],
            out_specs=pl.BlockSpec((COMBINE_TM, d), lambda i, dr: (i, 0)),
            scratch_shapes=[pltpu.VMEM((COMBINE_TM,) + ys.shape[1:], ys.dtype),
                            pltpu.VMEM((COMBINE_TM,) + ys.shape[1:], ys.dtype),
                            pltpu.SemaphoreType.DMA]),
        out_shape=jax.ShapeDtypeStruct((m, d), F32),
        compiler_params=_cparams(("arbitrary",)),
        name="combine",
    )(dest_flat, x1, gw, g2, ys)


def kernel(x_prompt, x_sample, cache_a_k, cache_a_v, cache_b_k, cache_b_v, c, c_ctx, norm1_w, norm2_w, w_ada, b_ada, w_in, qn_a, kn_a, qn_b, kn_b, rpb_a, sink_b, w_pa, w_pb, w_out, w_rg, b_rg, w_re, b_re, w1, w3, w2):
    batch, seq, d = x_prompt.shape
    dec_batch, dec_seq, _ = x_sample.shape
    depth = norm1_w.shape[0]
    assert depth == 1, "one trunk layer"
    past = cache_a_k.shape[2]
    n_ctx, n_lat = batch * seq, dec_batch * dec_seq
    n_tok = n_ctx + n_lat

    xc = x_prompt.reshape(n_ctx, d)
    xl = x_sample.reshape(n_lat, d)

    cond = jnp.concatenate([c_ctx[None, :], c], axis=0)
    mod = _ada_call(cond, w_ada[0], b_ada[0][None, :])
    sh1, sc1, g1, sh2, sc2, g2 = [mod[:, i * d:(i + 1) * d][:, None, :] for i in range(6)]
    ctx_rows, lat_rows = slice(0, 1), slice(1, 1 + dec_batch)

    nw1, nw2 = norm1_w[0][None, :], norm2_w[0][None, :]
    qna, kna, qnb, knb = qn_a[0][None, :], kn_a[0][None, :], qn_b[0][None, :], kn_b[0][None, :]
    sink = sink_b[0]
    wout_bf = w_out[0].astype(BF16)
    wr = jnp.zeros((d, LOGIT_PAD), F32).at[:, :N_EXPERTS].set(w_re[0]).at[:, N_EXPERTS:N_EXPERTS + N_GROUPS].set(w_rg[0])
    br = jnp.zeros((LOGIT_PAD, 1), F32).at[:N_EXPERTS, 0].set(b_re[0]).at[N_EXPERTS:N_EXPERTS + N_GROUPS, 0].set(b_rg[0])

    proj_c, gates_c = _inproj_call(xc, nw1, sc1[ctx_rows], sh1[ctx_rows], w_in[0], n_ctx)
    proj_l, gates_l = _inproj_call(xl, nw1, sc1[lat_rows], sh1[lat_rows], w_in[0], dec_seq)

    oa_c, ob_c, new_a_k, new_a_v, new_b_k, new_b_v = _ctx_attn_call(proj_c, seq, qna, kna, qnb, knb, sink)

    bias_blocks, pair0, bias_index = _na_bias_blocks(rpb_a[0], dec_seq)
    cos, sin_a, sin_b = _rope_tables(dec_seq)
    rows_of = lambda cache: cache.reshape(-1, HEAD_DIM)
    oa_l = _lat_attn_a_call(proj_l, dec_seq, past, rows_of(cache_a_k), rows_of(cache_a_v),
                            bias_blocks, pair0, bias_index, qna, kna)
    ob_l = _lat_attn_b_call(proj_l, dec_seq, past, rows_of(cache_b_k), rows_of(cache_b_v),
                            cos, sin_a, sin_b, qnb, knb, sink)

    mix_c = _mix_call(oa_c, ob_c, gates_c, w_pa[0], w_pb[0])
    mix_l = _mix_call(oa_l, ob_l, gates_l, w_pa[0], w_pb[0])
    x1_c, h2_c, lg_c = _outproj_call(mix_c, wout_bf, xc, g1[ctx_rows], sc2[ctx_rows], sh2[ctx_rows],
                                     nw2, wr, n_ctx // MERGE_TM)
    x1_l, h2_l, lg_l = _outproj_call(mix_l, wout_bf, xl, g1[lat_rows], sc2[lat_rows], sh2[lat_rows],
                                     nw2, wr, dec_seq // MERGE_TM)

    eid, gw, rank, cnt = _route_call(jnp.concatenate([lg_c, lg_l], axis=0), br)
    padded_rows, n_items = _moe_layout(2 * n_tok)
    dest, tail, item_e, item_row0, item_nblk = _routing_tables(eid[:2], rank[:2], cnt[:, 0], n_items)
    dest_flat = dest.reshape(-1)
    xs = _dispatch_call(dest_flat, tail, h2_c, h2_l, padded_rows)
    ys = _moe_call(item_e, item_row0, item_nblk, xs, w1[0], w3[0], w2[0])
    gw_t = gw[:2].T
    y_c = _combine_call(dest_flat, x1_c, gw_t[:n_ctx], g2[ctx_rows], ys, 0, n_tok, n_ctx // COMBINE_TM)
    y_l = _combine_call(dest_flat, x1_l, gw_t[n_ctx:], g2[lat_rows], ys, n_ctx, n_tok, dec_seq // COMBINE_TM)

    state = lambda a, heads: a.reshape(batch, 1, seq, heads, HEAD_DIM)
    return (y_c.reshape(batch, seq, d), y_l.reshape(dec_batch, dec_seq, d),
            state(new_a_k, NA_HEADS), state(new_a_v, NA_HEADS),
            state(new_b_k, NB_KV_HEADS), state(new_b_v, NB_KV_HEADS))
```

```python
import functools

import jax
import jax.numpy as jnp
import numpy as np
from jax import lax
from jax.experimental import pallas as pl
from jax.experimental.pallas import tpu as pltpu

D_MODEL = 2048
HEAD_DIM = 128
NA_HEADS = 8
NA_WIDTH = NA_HEADS * HEAD_DIM
NB_Q_HEADS = 8
NB_KV_HEADS = 2
NB_GROUP = NB_Q_HEADS // NB_KV_HEADS
NB_WIDTH = NB_Q_HEADS * HEAD_DIM
NB_KV_WIDTH = NB_KV_HEADS * HEAD_DIM
GRID_W = 64
NA_WIN_ROWS = 8
NA_WIN_COLS = 16
WINDOW = 128
N_GROUPS = 4
EXPERTS_PER_GROUP = 8
N_EXPERTS = N_GROUPS * EXPERTS_PER_GROUP
D_EXPERT = 1024
IN_WIDTH = 3 * NA_WIDTH + NB_WIDTH + 2 * NB_KV_WIDTH + 2 * D_MODEL
ROPE_BASE = 10000.0
NORM_EPS = 1e-6
NEG_INF = -1e30
ATTN_SCALE = HEAD_DIM ** -0.5

QA_HEAD0 = 0
KA_HEAD0 = NA_HEADS
VA_HEAD0 = 2 * NA_HEADS
QB_HEAD0 = 3 * NA_HEADS
KB_HEAD0 = QB_HEAD0 + NB_Q_HEADS
VB_HEAD0 = KB_HEAD0 + NB_KV_HEADS
GATE_COL0 = (VB_HEAD0 + NB_KV_HEADS) * HEAD_DIM

LOGIT_PAD = 128
MOE_ROW_BLOCK = 256
MOE_ROW_GROUP = 1024
MOE_F_CHUNK = 512
VMEM_LIMIT = 56 * 1024 * 1024

F32 = jnp.float32
BF16 = jnp.bfloat16


def _cparams(sem, vmem_limit=VMEM_LIMIT):
    return pltpu.CompilerParams(dimension_semantics=sem, vmem_limit_bytes=vmem_limit)


def _rms(x, w):
    x = x.astype(F32)
    return x * lax.rsqrt(jnp.mean(x * x, axis=-1, keepdims=True) + NORM_EPS) * w


def _dot(a, b):
    return jnp.dot(a, b, preferred_element_type=F32)


def _dot_nt(a, b):
    return lax.dot_general(a, b, (((1,), (1,)), ((), ())), preferred_element_type=F32)


def _pack_halves(x):
    n = x.shape[1] // 2
    lo = lax.bitcast_convert_type(x[:, :n].astype(BF16).astype(F32), jnp.uint32)
    hi = lax.bitcast_convert_type(x[:, n:].astype(BF16).astype(F32), jnp.uint32)
    return hi | (lo >> 16)


def _unpack_halves(w):
    lo = lax.bitcast_convert_type(w << 16, F32)
    hi = lax.bitcast_convert_type(w & jnp.uint32(0xFFFF0000), F32)
    return jnp.concatenate([lo, hi], axis=1)


ROW_TILE = (8, 128)


def _rows_to_tiles(w):
    return w.reshape((w.shape[0],) + ROW_TILE)


def _tiles_to_rows(t):
    return t.reshape(t.shape[0], ROW_TILE[0] * ROW_TILE[1])


def _dot_split(a, b):
    a_hi = a.astype(BF16)
    a_lo = (a - a_hi.astype(F32)).astype(BF16)
    b_hi = b.astype(BF16)
    b_lo = (b - b_hi.astype(F32)).astype(BF16)
    return _dot(a_hi, b_hi) + (_dot(a_lo, b_hi) + _dot(a_hi, b_lo))


ADA_ROWS = 8
ADA_TN = 1024


def _ada_kernel(c_ref, w_ref, b_ref, o_ref):
    n_rows, d, lanes = c_ref.shape
    tn = w_ref.shape[1]

    def body(kb, acc):
        ks = pl.ds(pl.multiple_of(kb * 8, 8), 8)
        w = w_ref[ks, :]
        out = []
        for r in range(n_rows):
            c = c_ref[r, ks, :]
            s = c * jax.nn.sigmoid(c)
            out.append(acc[r] + w * jnp.concatenate([s] * (tn // lanes), axis=1))
        return tuple(out)

    acc = lax.fori_loop(0, d // 8, body, tuple(jnp.zeros((8, tn), F32) for _ in range(n_rows)), unroll=8)
    ri = lax.broadcasted_iota(jnp.int32, (ADA_ROWS, tn), 0)
    res = jnp.zeros((ADA_ROWS, tn), F32)
    for r in range(n_rows):
        row = jnp.sum(acc[r], axis=0, keepdims=True) + b_ref[...]
        res = jnp.where(ri == r, row, res)
    o_ref[...] = res


def _ada_call(cond, w_ada, b_ada):
    n_rows, d = cond.shape
    n = w_ada.shape[1]
    lanes = 128
    cond_lanes = jnp.broadcast_to(cond[:, :, None], (n_rows, d, lanes))
    return pl.pallas_call(
        _ada_kernel,
        grid=(n // ADA_TN,),
        in_specs=[pl.BlockSpec((n_rows, d, lanes), lambda j: (0, 0, 0)),
                  pl.BlockSpec((d, ADA_TN), lambda j: (0, j)),
                  pl.BlockSpec((1, ADA_TN), lambda j: (0, j))],
        out_specs=pl.BlockSpec((ADA_ROWS, ADA_TN), lambda j: (0, j)),
        out_shape=jax.ShapeDtypeStruct((ADA_ROWS, n), F32),
        compiler_params=_cparams(("arbitrary",)),
        name="ada",
    )(cond_lanes, w_ada, b_ada)


INPROJ_TM = 2048
INPROJ_TN = 512
NORM_ROWS = 128


def _inproj_kernel(x_hbm, nw_ref, sc_ref, sh_ref, w_ref, qkv_ref, gate_ref, x_buf, h_scr, sem, *, qkv_tiles):
    i = pl.program_id(0)
    j = pl.program_id(1)
    tm = x_buf.shape[0]
    fetch = lambda tile: pltpu.make_async_copy(
        x_hbm.at[pl.ds(pl.multiple_of(tile * tm, tm), tm), :], x_buf, sem)

    @pl.when(j == 0)
    def _():
        @pl.when(i == 0)
        def _():
            fetch(0).start()

        fetch(i).wait()
        nw = nw_ref[...]
        sc = 1.0 + sc_ref[0]
        sh = sh_ref[0]

        def body(r, carry):
            rows = pl.ds(pl.multiple_of(r * NORM_ROWS, NORM_ROWS), NORM_ROWS)
            h_scr[rows, :] = (_rms(x_buf[rows, :], nw) * sc + sh).astype(BF16)
            return carry

        lax.fori_loop(0, tm // NORM_ROWS, body, 0)

    @pl.when((j == 1) & (i + 1 < pl.num_programs(0)))
    def _():
        fetch(i + 1).start()

    res = _dot(h_scr[...], w_ref[...].astype(BF16))

    @pl.when(j < qkv_tiles)
    def _():
        qkv_ref[...] = res.astype(qkv_ref.dtype)

    @pl.when(j >= qkv_tiles)
    def _():
        gate_ref[...] = res.astype(gate_ref.dtype)


def _inproj_call(x, nw, sc, sh, w_in, rows_per_group):
    m, d = x.shape
    n = w_in.shape[1]
    assert n // INPROJ_TN >= 2, "the next token tile is requested in the second column step"
    tm = min(INPROJ_TM, rows_per_group)
    tiles_per_group = rows_per_group // tm
    qkv_tiles = GATE_COL0 // INPROJ_TN
    grp = lambda i, j: (i // tiles_per_group, 0, 0)
    return pl.pallas_call(
        functools.partial(_inproj_kernel, qkv_tiles=qkv_tiles),
        grid=(m // tm, n // INPROJ_TN),
        in_specs=[pl.BlockSpec(memory_space=pl.ANY),
                  pl.BlockSpec((1, d), lambda i, j: (0, 0)),
                  pl.BlockSpec((1, 1, d), grp),
                  pl.BlockSpec((1, 1, d), grp),
                  pl.BlockSpec((d, INPROJ_TN), lambda i, j: (0, j))],
        out_specs=[pl.BlockSpec((tm, INPROJ_TN), lambda i, j: (i, jnp.minimum(j, qkv_tiles - 1))),
                   pl.BlockSpec((tm, INPROJ_TN), lambda i, j: (i, jnp.maximum(j - qkv_tiles, 0)))],
        out_shape=[jax.ShapeDtypeStruct((m, GATE_COL0), BF16),
                   jax.ShapeDtypeStruct((m, n - GATE_COL0), BF16)],
        scratch_shapes=[pltpu.VMEM((tm, d), F32), pltpu.VMEM((tm, d), BF16), pltpu.SemaphoreType.DMA],
        compiler_params=_cparams(("arbitrary", "arbitrary")),
        name="inproj",
    )(x, nw, sc, sh, w_in)


def _with_ones(v):
    return jnp.concatenate([v.astype(BF16), jnp.ones(v.shape, BF16)], axis=1)


def _softmax_pv(scores, values_with_ones, sink=None):
    m = None
    for s in scores:
        ms = jnp.max(s, axis=-1, keepdims=True)
        m = ms if m is None else jnp.maximum(m, ms)
    if sink is not None:
        m = jnp.maximum(m, sink)
    acc = None
    for s, v1 in zip(scores, values_with_ones):
        pv = _dot(jnp.exp(s - m).astype(BF16), v1)
        acc = pv if acc is None else acc + pv
    d = acc.shape[1] // 2
    den = acc[:, d:]
    if sink is not None:
        den = den + jnp.exp(sink - m)
    return acc[:, :d] / den


def _ctx_attn_kernel(qa_ref, ka_ref, va_ref, qb_ref, kb_ref, vb_ref,
                     qna_ref, kna_ref, qnb_ref, knb_ref, sink_ref,
                     oa_ref, ob_ref, nak_ref, nav_ref, nbk_ref, nbv_ref):
    qna, kna, qnb, knb = qna_ref[...], kna_ref[...], qnb_ref[...], knb_ref[...]
    seq = qa_ref.shape[0]
    for h in range(NA_HEADS):
        cols = slice(h * HEAD_DIM, (h + 1) * HEAD_DIM)
        q = (_rms(qa_ref[:, cols], qna) * ATTN_SCALE).astype(BF16)
        k = _rms(ka_ref[:, cols], kna)
        v = va_ref[:, cols]
        nak_ref[pl.ds(h, seq, stride=NA_HEADS), :] = k
        nav_ref[pl.ds(h, seq, stride=NA_HEADS), :] = v.astype(F32)
        s = _dot_nt(q, k.astype(BF16))
        oa_ref[:, cols] = _softmax_pv([s], [_with_ones(v)]).astype(oa_ref.dtype)
    for kv in range(NB_KV_HEADS):
        kcols = slice(kv * HEAD_DIM, (kv + 1) * HEAD_DIM)
        k = _rms(kb_ref[:, kcols], knb)
        v = vb_ref[:, kcols]
        nbk_ref[pl.ds(kv, seq, stride=NB_KV_HEADS), :] = k
        nbv_ref[pl.ds(kv, seq, stride=NB_KV_HEADS), :] = v.astype(F32)
        kb16 = k.astype(BF16)
        vb1 = _with_ones(v)
        for g in range(NB_GROUP):
            hq = kv * NB_GROUP + g
            cols = slice(hq * HEAD_DIM, (hq + 1) * HEAD_DIM)
            q = (_rms(qb_ref[:, cols], qnb) * ATTN_SCALE).astype(BF16)
            s = _dot_nt(q, kb16)
            ob_ref[:, cols] = _softmax_pv([s], [vb1], sink=sink_ref[hq]).astype(ob_ref.dtype)


def _ctx_attn_call(proj, seq, qna, kna, qnb, knb, sink):
    m = proj.shape[0]
    nb = m // seq
    wide = lambda blk: pl.BlockSpec((seq, NA_WIDTH), lambda b: (b, blk))
    narrow = lambda blk: pl.BlockSpec((seq, NB_KV_WIDTH), lambda b: (b, blk))
    vec = pl.BlockSpec((1, HEAD_DIM), lambda b: (0, 0))
    return pl.pallas_call(
        _ctx_attn_kernel,
        grid=(nb,),
        in_specs=[wide(QA_HEAD0 // NA_HEADS), wide(KA_HEAD0 // NA_HEADS), wide(VA_HEAD0 // NA_HEADS),
                  wide(QB_HEAD0 // NA_HEADS), narrow(KB_HEAD0 // NB_KV_HEADS), narrow(VB_HEAD0 // NB_KV_HEADS),
                  vec, vec, vec, vec,
                  pl.BlockSpec(memory_space=pltpu.SMEM)],
        out_specs=[pl.BlockSpec((seq, NA_WIDTH), lambda b: (b, 0)),
                   pl.BlockSpec((seq, NB_WIDTH), lambda b: (b, 0)),
                   pl.BlockSpec((seq * NA_HEADS, HEAD_DIM), lambda b: (b, 0)),
                   pl.BlockSpec((seq * NA_HEADS, HEAD_DIM), lambda b: (b, 0)),
                   pl.BlockSpec((seq * NB_KV_HEADS, HEAD_DIM), lambda b: (b, 0)),
                   pl.BlockSpec((seq * NB_KV_HEADS, HEAD_DIM), lambda b: (b, 0))],
        out_shape=[jax.ShapeDtypeStruct((m, NA_WIDTH), BF16),
                   jax.ShapeDtypeStruct((m, NB_WIDTH), BF16),
                   jax.ShapeDtypeStruct((m * NA_HEADS, HEAD_DIM), F32),
                   jax.ShapeDtypeStruct((m * NA_HEADS, HEAD_DIM), F32),
                   jax.ShapeDtypeStruct((m * NB_KV_HEADS, HEAD_DIM), F32),
                   jax.ShapeDtypeStruct((m * NB_KV_HEADS, HEAD_DIM), F32)],
        compiler_params=_cparams(("arbitrary",)),
        name="ctx_attn",
    )(proj, proj, proj, proj, proj, proj, qna, kna, qnb, knb, sink)


def _rope(x, cos, sin_a, sin_b):
    quarter = HEAD_DIM // 4
    return (x * cos + pltpu.roll(x, HEAD_DIM - quarter, 1) * sin_a
            + pltpu.roll(x, quarter, 1) * sin_b)


def _head_rows(cache_ref, head, n_heads):
    past = cache_ref.shape[0] // n_heads
    return cache_ref[pl.ds(head, past, stride=n_heads), :]


def _lat_attn_a_kernel(q_ref, k_ref, v_ref, ck_ref, cv_ref, cb_ref, qn_ref, kn_ref, o_ref, *, pair0, index):
    head = pl.program_id(0)
    q = (_rms(q_ref[...], qn_ref[...]) * ATTN_SCALE).astype(BF16)
    k = _rms(k_ref[...], kn_ref[...]).astype(BF16)
    v1 = _with_ones(v_ref[...])
    ck = _head_rows(ck_ref, head, NA_HEADS).astype(BF16)
    cv1 = _with_ones(_head_rows(cv_ref, head, NA_HEADS))
    pair = 2 * GRID_W
    rows_per_block = pair // GRID_W
    n_pairs = len(index[0])
    for blk in range(len(pair0) // rows_per_block):
        grid_rows = range(blk * rows_per_block, (blk + 1) * rows_per_block)
        p0 = pair0[grid_rows[0]]
        assert all(pair0[r] == p0 for r in grid_rows), "query rows of a block share their key pairs"
        rows = slice(blk * pair, (blk + 1) * pair)
        keys = slice(p0 * pair, (p0 + n_pairs) * pair)
        bias = jnp.concatenate(
            [jnp.concatenate([cb_ref[u] for u in index[r]], axis=1) for r in grid_rows], axis=0)
        qs = q[rows]
        s_win = _dot_nt(qs, k[keys]) + bias
        s_ctx = _dot_nt(qs, ck)
        o_ref[rows, :] = _softmax_pv([s_win, s_ctx], [v1[keys], cv1]).astype(o_ref.dtype)


def _lat_attn_b_kernel(q_ref, k_ref, v_ref, ck_ref, cv_ref, cos_ref, sina_ref, sinb_ref,
                       qn_ref, kn_ref, sink_ref, o_ref):
    cos, sin_a, sin_b = cos_ref[...], sina_ref[...], sinb_ref[...]
    q = _rope(_rms(q_ref[...], qn_ref[...]), cos, sin_a, sin_b)
    k = _rope(_rms(k_ref[...], kn_ref[...]), cos, sin_a, sin_b)
    q = (q * ATTN_SCALE).astype(BF16)
    k = k.astype(BF16)
    v1 = _with_ones(v_ref[...])
    length = q.shape[0]
    kv = pl.program_id(0) // NB_GROUP
    ck = _head_rows(ck_ref, kv, NB_KV_HEADS).astype(BF16)
    cv1 = _with_ones(_head_rows(cv_ref, kv, NB_KV_HEADS))
    sink = sink_ref[pl.program_id(0)]
    for qb in range(length // WINDOW):
        rows = slice(qb * WINDOW, (qb + 1) * WINDOW)
        lo, hi = max(0, (qb - 1) * WINDOW), min(length, (qb + 2) * WINDOW)
        qs = q[rows]
        s_win = _dot_nt(qs, k[lo:hi])
        qi = qb * WINDOW + lax.broadcasted_iota(jnp.int32, s_win.shape, 0)
        kj = lo + lax.broadcasted_iota(jnp.int32, s_win.shape, 1)
        s_win = jnp.where(jnp.abs(qi - kj) <= WINDOW, s_win, NEG_INF)
        s_ctx = _dot_nt(qs, ck)
        o_ref[rows, :] = _softmax_pv([s_win, s_ctx], [v1[lo:hi], cv1], sink=sink).astype(o_ref.dtype)


def _lat_attn_a_call(proj, length, past, ck, cv, cb, pair0, index, qn, kn):
    m = proj.shape[0]
    head = lambda h0: pl.BlockSpec((length, HEAD_DIM), lambda h, b: (b, h0 + h))
    cache = pl.BlockSpec((past * NA_HEADS, HEAD_DIM), lambda h, b: (b, 0))
    vec = pl.BlockSpec((1, HEAD_DIM), lambda h, b: (0, 0))
    return pl.pallas_call(
        functools.partial(_lat_attn_a_kernel, pair0=pair0, index=index),
        grid=(NA_HEADS, m // length),
        in_specs=[head(QA_HEAD0), head(KA_HEAD0), head(VA_HEAD0), cache, cache,
                  pl.BlockSpec((None,) + cb.shape[1:], lambda h, b: (h, 0, 0, 0)), vec, vec],
        out_specs=pl.BlockSpec((length, HEAD_DIM), lambda h, b: (b, h)),
        out_shape=jax.ShapeDtypeStruct((m, NA_WIDTH), BF16),
        compiler_params=_cparams(("arbitrary", "arbitrary")),
        name="lat_attn_a",
    )(proj, proj, proj, ck, cv, cb, qn, kn)


def _lat_attn_b_call(proj, length, past, ck, cv, cos, sin_a, sin_b, qn, kn, sink):
    m = proj.shape[0]
    qspec = pl.BlockSpec((length, HEAD_DIM), lambda h, b: (b, QB_HEAD0 + h))
    kvspec = lambda h0: pl.BlockSpec((length, HEAD_DIM), lambda h, b: (b, h0 + h // NB_GROUP))
    cache = pl.BlockSpec((past * NB_KV_HEADS, HEAD_DIM), lambda h, b: (b, 0))
    table = pl.BlockSpec((length, HEAD_DIM), lambda h, b: (0, 0))
    vec = pl.BlockSpec((1, HEAD_DIM), lambda h, b: (0, 0))
    return pl.pallas_call(
        _lat_attn_b_kernel,
        grid=(NB_Q_HEADS, m // length),
        in_specs=[qspec, kvspec(KB_HEAD0), kvspec(VB_HEAD0), cache, cache,
                  table, table, table, vec, vec, pl.BlockSpec(memory_space=pltpu.SMEM)],
        out_specs=pl.BlockSpec((length, HEAD_DIM), lambda h, b: (b, h)),
        out_shape=jax.ShapeDtypeStruct((m, NB_WIDTH), BF16),
        compiler_params=_cparams(("arbitrary", "arbitrary")),
        name="lat_attn_b",
    )(proj, proj, proj, ck, cv, cos, sin_a, sin_b, qn, kn, sink)


def _na_bias_blocks(rpb, length):
    rows = length // GRID_W
    kr_n = min(NA_WIN_ROWS, rows)
    n_pairs = min(kr_n // 2 + 1, rows // 2)
    r = np.arange(rows)
    c = np.arange(GRID_W)
    r0 = np.clip(r - kr_n // 2, 0, rows - kr_n)
    c0 = np.clip(c - NA_WIN_COLS // 2, 0, GRID_W - NA_WIN_COLS)
    pair0 = np.minimum(r0 // 2, rows // 2 - n_pairs)
    kr = 2 * (pair0[:, None, None] + np.arange(n_pairs)[None, :, None]) + np.arange(2)[None, None, :]
    row_ok = (kr >= r0[:, None, None]) & (kr < r0[:, None, None] + kr_n)
    col_ok = (c[None, :] >= c0[:, None]) & (c[None, :] < c0[:, None] + NA_WIN_COLS)
    dr = kr - r[:, None, None] + (NA_WIN_ROWS - 1)
    dc = np.clip(c[None, :] - c[:, None], -(NA_WIN_COLS - 1), NA_WIN_COLS - 1) + (NA_WIN_COLS - 1)
    offs = np.where(row_ok, dr, -1).reshape(-1, 2)
    uniq, inverse = np.unique(offs, axis=0, return_inverse=True)
    index = inverse.reshape(rows, n_pairs)
    row_sel = (uniq[:, :, None] == np.arange(2 * NA_WIN_ROWS - 1)[None, None, :]).astype(np.float32)
    col_hit = (dc[None] == np.arange(2 * NA_WIN_COLS - 1)[:, None, None]) & col_ok[None]
    col_sel = np.zeros((2,) + col_hit.shape[:2] + (2 * GRID_W,), np.float32)
    for half in range(2):
        col_sel[half, :, :, half * GRID_W:(half + 1) * GRID_W] = col_hit
    hi = lax.Precision.HIGHEST
    per_col = jnp.einsum("hde,lecn->hldcn", rpb.astype(F32), col_sel, precision=hi)
    table = jnp.einsum("uld,hldcn->hucn", row_sel, per_col, precision=hi)
    valid = ((uniq >= 0)[:, None, :, None] & col_ok[None, :, None, :]).reshape(len(uniq), GRID_W, 2 * GRID_W)
    return (jnp.where(valid[None], table, NEG_INF), tuple(int(p) for p in pair0),
            tuple(tuple(int(u) for u in row) for row in index))


def _rope_tables(length):
    t = jnp.arange(length)
    row = (t // GRID_W).astype(F32)
    col = (t % GRID_W).astype(F32)
    n_freq = HEAD_DIM // 4
    inv = ROPE_BASE ** (-jnp.arange(n_freq, dtype=F32) / n_freq)
    ar = row[:, None] * inv
    ac = col[:, None] * inv
    ang = jnp.concatenate([ar, ar, ac, ac], axis=-1)
    cos, sin = jnp.cos(ang), jnp.sin(ang)
    lane = jnp.arange(HEAD_DIM)
    takes_left = ((lane // n_freq) % 2 == 0)[None, :]
    return cos, jnp.where(takes_left, -sin, 0.0), jnp.where(takes_left, 0.0, sin)


MERGE_TM = 512


def _resident(shape):
    zeros = (0,) * len(shape)
    return pl.BlockSpec(shape, lambda i: zeros, pipeline_mode=pl.Buffered(1))


def _mix_kernel(oa_ref, ob_ref, ga_ref, gb_ref, wpa_ref, wpb_ref, mix_ref):
    ya = _dot(oa_ref[...], wpa_ref[...].astype(BF16))
    yb = _dot(ob_ref[...], wpb_ref[...].astype(BF16))
    mix = (jax.nn.sigmoid(ga_ref[...].astype(F32)) * ya
           + jax.nn.sigmoid(gb_ref[...].astype(F32)) * yb)
    mix_ref[...] = mix.astype(mix_ref.dtype)


def _mix_call(oa, ob, gates, w_pa, w_pb):
    m = oa.shape[0]
    d = w_pa.shape[1]
    row = lambda i: (i, 0)
    return pl.pallas_call(
        _mix_kernel,
        grid=(m // MERGE_TM,),
        in_specs=[pl.BlockSpec((MERGE_TM, NA_WIDTH), row),
                  pl.BlockSpec((MERGE_TM, NB_WIDTH), row),
                  pl.BlockSpec((MERGE_TM, d), lambda i: (i, 0)),
                  pl.BlockSpec((MERGE_TM, d), lambda i: (i, 1)),
                  _resident((NA_WIDTH, d)), _resident((NB_WIDTH, d))],
        out_specs=pl.BlockSpec((MERGE_TM, d), row),
        out_shape=jax.ShapeDtypeStruct((m, d), BF16),
        compiler_params=_cparams(("arbitrary",)),
        name="mix",
    )(oa, ob, gates, gates, w_pa, w_pb)


def _outproj_kernel(mix_ref, wout_ref, x_ref, g1_ref, sc2_ref, sh2_ref, n2w_ref, wr_ref,
                    x1_ref, h2_ref, lg_ref, acc_ref):
    i = pl.program_id(0)
    last = pl.num_programs(0) - 1

    def finish(prod):
        x1 = x_ref[...] + g1_ref[0] * prod
        x1_ref[...] = x1
        h2 = _rms(x1, n2w_ref[...]) * (1.0 + sc2_ref[0]) + sh2_ref[0]
        h2_ref[...] = _rows_to_tiles(_pack_halves(h2))
        lg_ref[...] = _dot_split(h2, wr_ref[...])

    @pl.when(i == 0)
    def _():
        acc_ref[...] = _dot(mix_ref[...], wout_ref[...].astype(BF16))

    @pl.when((i > 0) & (i < last))
    def _():
        prod = acc_ref[...]
        acc_ref[...] = _dot(mix_ref[...], wout_ref[...].astype(BF16))
        finish(prod)

    @pl.when(i == last)
    def _():
        finish(acc_ref[...])


def _outproj_call(mix, w_out, x, g1, sc2, sh2, n2w, wr, tiles_per_group):
    m, d = x.shape
    n_tiles = m // MERGE_TM
    ahead = lambda i: (jnp.minimum(i, n_tiles - 1), 0)
    done = lambda i: (jnp.maximum(i - 1, 0), 0)
    grp = lambda i: (jnp.maximum(i - 1, 0) // tiles_per_group, 0, 0)
    return pl.pallas_call(
        _outproj_kernel,
        grid=(n_tiles + 1,),
        in_specs=[pl.BlockSpec((MERGE_TM, d), ahead),
                  _resident((d, d)),
                  pl.BlockSpec((MERGE_TM, d), done),
                  pl.BlockSpec((1, 1, d), grp), pl.BlockSpec((1, 1, d), grp), pl.BlockSpec((1, 1, d), grp),
                  pl.BlockSpec((1, d), lambda i: (0, 0)),
                  _resident((d, LOGIT_PAD))],
        out_specs=[pl.BlockSpec((MERGE_TM, d), done),
                   pl.BlockSpec((MERGE_TM,) + ROW_TILE, lambda i: (jnp.maximum(i - 1, 0), 0, 0)),
                   pl.BlockSpec((MERGE_TM, LOGIT_PAD), done)],
        out_shape=[jax.ShapeDtypeStruct((m, d), F32),
                   jax.ShapeDtypeStruct((m,) + ROW_TILE, jnp.uint32),
                   jax.ShapeDtypeStruct((m, LOGIT_PAD), F32)],
        scratch_shapes=[pltpu.VMEM((MERGE_TM, d), F32)],
        compiler_params=_cparams(("arbitrary",)),
        name="outproj",
    )(mix, w_out, x, g1, sc2, sh2, n2w, wr)


ROUTE_TM = 512


def _first_index_of_max(vals, idx, n):
    mx = jnp.max(vals, axis=0, keepdims=True)
    first = jnp.min(jnp.where(vals == mx, idx, n), axis=0, keepdims=True)
    return mx, first


def _route_kernel(lg_ctx_ref, lg_lat_ref, bias_ref, eid_ref, gw_ref, rank_ref, cnt_ref, base_ref, *, ctx_tiles):
    step = pl.program_id(0)

    @pl.when(step == 0)
    def _():
        base_ref[...] = jnp.zeros_like(base_ref)

    lg = jnp.where(step < ctx_tiles, lg_ctx_ref[...], lg_lat_ref[...])
    lt = lg.T + bias_ref[...]
    n_tok = lt.shape[1]
    le = lt[0:N_EXPERTS]
    lgrp = lt[N_EXPERTS:N_EXPERTS + N_GROUPS]
    gi = lax.broadcasted_iota(jnp.int32, (N_GROUPS, n_tok), 0)
    gmax, gsel = _first_index_of_max(lgrp, gi, N_GROUPS)
    pg_sel = 1.0 / jnp.sum(jnp.exp(lgrp - gmax), axis=0, keepdims=True)
    le_sel = jnp.zeros((EXPERTS_PER_GROUP, n_tok), F32)
    for g in range(N_GROUPS):
        le_sel = jnp.where(gsel == g, le[g * EXPERTS_PER_GROUP:(g + 1) * EXPERTS_PER_GROUP], le_sel)
    ei = lax.broadcasted_iota(jnp.int32, (EXPERTS_PER_GROUP, n_tok), 0)
    v0, i0 = _first_index_of_max(le_sel, ei, EXPERTS_PER_GROUP)
    rest = jnp.where(ei == i0, -jnp.inf, le_sel)
    v1, i1 = _first_index_of_max(rest, ei, EXPERTS_PER_GROUP)
    e1 = jnp.exp(v1 - v0)
    w0 = pg_sel / (1.0 + e1)
    w1 = pg_sel * e1 / (1.0 + e1)
    eid0 = gsel * EXPERTS_PER_GROUP + i0
    eid1 = gsel * EXPERTS_PER_GROUP + i1

    xi = lax.broadcasted_iota(jnp.int32, (N_EXPERTS, n_tok), 0)
    si = lax.broadcasted_iota(jnp.int32, (n_tok, n_tok), 0)
    ti = lax.broadcasted_iota(jnp.int32, (n_tok, n_tok), 1)
    before = (si < ti).astype(BF16)
    base = base_ref[...]
    hot0 = (xi == eid0).astype(F32)
    hot1 = (xi == eid1).astype(F32)
    pre0 = _dot(hot0.astype(BF16), before)
    pre1 = _dot(hot1.astype(BF16), before)
    tot0 = jnp.sum(hot0, axis=1, keepdims=True)
    tot1 = jnp.sum(hot1, axis=1, keepdims=True)
    rank0 = jnp.sum(hot0 * (base + pre0), axis=0, keepdims=True)
    rank1 = jnp.sum(hot1 * (base + tot0 + pre1), axis=0, keepdims=True)
    base = base + tot0 + tot1
    base_ref[...] = base

    ri = lax.broadcasted_iota(jnp.int32, (8, n_tok), 0)
    pick = lambda a, b: jnp.where(ri == 0, a, jnp.where(ri == 1, b, jnp.zeros_like(a)))
    eid_ref[...] = pick(eid0, eid1)
    gw_ref[...] = pick(w0, w1)
    rank_ref[...] = pick(rank0, rank1).astype(jnp.int32)
    cnt_ref[...] = jnp.broadcast_to(base, cnt_ref.shape).astype(jnp.int32)


def _route_call(logits_ctx, logits_lat, bias_col):
    ctx_tiles = logits_ctx.shape[0] // ROUTE_TM
    t = logits_ctx.shape[0] + logits_lat.shape[0]
    tok = pl.BlockSpec((8, ROUTE_TM), lambda i: (0, i))
    return pl.pallas_call(
        functools.partial(_route_kernel, ctx_tiles=ctx_tiles),
        grid=(t // ROUTE_TM,),
        in_specs=[pl.BlockSpec((ROUTE_TM, LOGIT_PAD), lambda i: (jnp.minimum(i, ctx_tiles - 1), 0)),
                  pl.BlockSpec((ROUTE_TM, LOGIT_PAD), lambda i: (jnp.maximum(i - ctx_tiles, 0), 0)),
                  pl.BlockSpec((LOGIT_PAD, 1), lambda i: (0, 0))],
        out_specs=[tok, tok, tok, pl.BlockSpec((N_EXPERTS, 128), lambda i: (0, 0))],
        out_shape=[jax.ShapeDtypeStruct((8, t), jnp.int32),
                   jax.ShapeDtypeStruct((8, t), F32),
                   jax.ShapeDtypeStruct((8, t), jnp.int32),
                   jax.ShapeDtypeStruct((N_EXPERTS, 128), jnp.int32)],
        scratch_shapes=[pltpu.VMEM((N_EXPERTS, 1), F32)],
        compiler_params=_cparams(("arbitrary",)),
        name="route",
    )(logits_ctx, logits_lat, bias_col)


def _moe_layout(n_pairs):
    padded_rows = -(-(n_pairs + N_EXPERTS * (MOE_ROW_BLOCK - 1)) // MOE_ROW_BLOCK) * MOE_ROW_BLOCK
    n_items = (padded_rows + N_EXPERTS * (MOE_ROW_GROUP - MOE_ROW_BLOCK)) // MOE_ROW_GROUP
    return padded_rows, n_items


def _routing_tables(eid, rank, counts, n_items):
    padded = (counts + MOE_ROW_BLOCK - 1) // MOE_ROW_BLOCK * MOE_ROW_BLOCK
    pad_end = jnp.cumsum(padded)
    pad_start = pad_end - padded
    hot = eid[..., None] == jnp.arange(N_EXPERTS, dtype=jnp.int32)
    dest = (jnp.sum(jnp.where(hot, pad_start, 0), axis=-1) + rank).astype(jnp.int32)
    tail = jnp.where(padded > counts, pad_end - MOE_ROW_BLOCK, -1).astype(jnp.int32)
    per_expert = (padded + MOE_ROW_GROUP - 1) // MOE_ROW_GROUP
    item_end = jnp.cumsum(per_expert)
    item_start = item_end - per_expert
    total = item_end[-1]
    ii = jnp.arange(n_items, dtype=jnp.int32)
    owner = lambda v: jnp.minimum(jnp.sum(item_end[None, :] <= v[:, None], axis=1), N_EXPERTS - 1).astype(jnp.int32)
    e_of = owner(ii)
    sel = e_of[:, None] == jnp.arange(N_EXPERTS, dtype=jnp.int32)
    pick = lambda v: jnp.sum(jnp.where(sel, v[None, :], 0), axis=1)
    valid = ii < total
    e_last = owner(jnp.maximum(total - 1, 0)[None])[0]
    local = ii - pick(item_start)
    row0 = pick(pad_start) + local * MOE_ROW_GROUP
    nblk = jnp.clip((pick(padded) - local * MOE_ROW_GROUP) // MOE_ROW_BLOCK, 0, MOE_ROW_GROUP // MOE_ROW_BLOCK)
    item_e = jnp.where(valid, e_of, e_last).astype(jnp.int32)
    item_row0 = jnp.where(valid, row0, 0).astype(jnp.int32)
    item_nblk = jnp.where(valid, nblk, 0).astype(jnp.int32)
    return dest, tail, item_e, item_row0, item_nblk


DISPATCH_TOKENS = 256
ROW_DMA_GROUP = 8


def _row_copy(src, s, dst, d, sem):
    return pltpu.make_async_copy(src.at[pl.ds(s, 1)], dst.at[pl.ds(d, 1)], sem)


def _dispatch_kernel(dest_ref, tail_ref, h_ctx, h_lat, xs, zero_buf, sem, *, n_ctx, n_tok):
    step = pl.program_id(0)
    tail_copy = lambda e: pltpu.make_async_copy(
        zero_buf, xs.at[pl.ds(pl.multiple_of(tail_ref[e], MOE_ROW_BLOCK), MOE_ROW_BLOCK)], sem.at[1])

    @pl.when(step == 0)
    def _():
        zero_buf[...] = jnp.zeros_like(zero_buf)
        for e in range(N_EXPERTS):
            @pl.when(tail_ref[e] >= 0)
            def _():
                tail_copy(e).start()
        for e in range(N_EXPERTS):
            @pl.when(tail_ref[e] >= 0)
            def _():
                tail_copy(e).wait()

    tok0 = step * DISPATCH_TOKENS

    def scatter(src):
        def issue(g, carry):
            base = pl.multiple_of(g * ROW_DMA_GROUP, ROW_DMA_GROUP)
            for j in range(ROW_DMA_GROUP):
                _row_copy(src, base + j, xs, dest_ref[tok0 + base + j], sem.at[0]).start(priority=0)
                _row_copy(src, base + j, xs, dest_ref[n_tok + tok0 + base + j], sem.at[0]).start(priority=1)
            return carry

        lax.fori_loop(0, DISPATCH_TOKENS // ROW_DMA_GROUP, issue, 0)
        for _ in range(2):
            pltpu.make_async_copy(src, xs.at[pl.ds(0, DISPATCH_TOKENS)], sem.at[0]).wait()

    @pl.when(tok0 < n_ctx)
    def _():
        scatter(h_ctx)

    @pl.when(tok0 >= n_ctx)
    def _():
        scatter(h_lat)


def _dispatch_call(dest_flat, tail, h_ctx, h_lat, padded_rows):
    n_ctx = h_ctx.shape[0]
    row = h_ctx.shape[1:]
    n_tok = n_ctx + h_lat.shape[0]
    ctx_tiles = n_ctx // DISPATCH_TOKENS
    return pl.pallas_call(
        functools.partial(_dispatch_kernel, n_ctx=n_ctx, n_tok=n_tok),
        grid_spec=pltpu.PrefetchScalarGridSpec(
            num_scalar_prefetch=2,
            grid=(n_tok // DISPATCH_TOKENS,),
            in_specs=[pl.BlockSpec((DISPATCH_TOKENS,) + row, lambda i, dr, tr: (jnp.minimum(i, ctx_tiles - 1), 0, 0)),
                      pl.BlockSpec((DISPATCH_TOKENS,) + row, lambda i, dr, tr: (jnp.maximum(i - ctx_tiles, 0), 0, 0))],
            out_specs=pl.BlockSpec(memory_space=pl.ANY),
            scratch_shapes=[pltpu.VMEM((MOE_ROW_BLOCK,) + row, h_ctx.dtype), pltpu.SemaphoreType.DMA((2,))]),
        out_shape=jax.ShapeDtypeStruct((padded_rows,) + row, h_ctx.dtype),
        compiler_params=_cparams(("arbitrary",)),
        name="dispatch",
    )(dest_flat, tail, h_ctx, h_lat)


def _moe_kernel(item_e, item_row0, item_nblk, xs, w1_ref, w3_ref, w2_ref, ys,
                x_in, x_bf, acc, y_out, w1_bf, w3_bf, w2_bf, sem, *, n_chunks):
    i = pl.program_id(0)
    c = pl.program_id(1)
    n_items = pl.num_programs(0)
    last_c = n_chunks - 1
    nblk = item_nblk[i]
    max_blk = MOE_ROW_GROUP // MOE_ROW_BLOCK
    blk = lambda b: pl.ds(b * MOE_ROW_BLOCK, MOE_ROW_BLOCK)

    def rows_of(item, b):
        return pl.ds(pl.multiple_of(item_row0[item], MOE_ROW_BLOCK) + b * MOE_ROW_BLOCK, MOE_ROW_BLOCK)

    load = lambda item, b: pltpu.make_async_copy(xs.at[rows_of(item, b)], x_in.at[blk(b)], sem.at[0])
    store = lambda item, b: pltpu.make_async_copy(y_out.at[blk(b)], ys.at[rows_of(item, b)], sem.at[1])

    def for_blocks(item, fn):
        n = item_nblk[item]
        for b in range(max_blk):
            @pl.when(b < n)
            def _():
                fn(item, b)

    @pl.when(c == 0)
    def _():
        @pl.when(i == 0)
        def _():
            for_blocks(0, lambda it, b: load(it, b).start())

        for_blocks(i, lambda it, b: load(it, b).wait())

    @pl.when((c == last_c) & (i > 0))
    def _():
        for_blocks(i - 1, lambda it, b: store(it, b).wait())

    def run_blocks(first, last):
        def block(rows, w1, w3, w2):
            if first:
                x = _unpack_halves(_tiles_to_rows(x_in[rows])).astype(BF16)
                x_bf[rows, :] = x
            else:
                x = x_bf[rows, :]
            h1 = _dot(x, w1)
            h3 = _dot(x, w3)
            a = (h1 * jax.nn.sigmoid(h1) * h3).astype(BF16)
            y = _dot(a, w2)
            if not first:
                y = acc[rows, :] + y
            if last:
                y_out[rows] = _rows_to_tiles(_pack_halves(y))
            else:
                acc[rows, :] = y

        w1 = w1_ref[0].astype(BF16)
        w3 = w3_ref[0].astype(BF16)
        w2 = w2_ref[0].astype(BF16)
        w1_bf[...] = w1
        w3_bf[...] = w3
        w2_bf[...] = w2
        block(blk(0), w1, w3, w2)

        def body(b, carry):
            rows = pl.ds(pl.multiple_of(b * MOE_ROW_BLOCK, MOE_ROW_BLOCK), MOE_ROW_BLOCK)
            block(rows, w1_bf[...], w3_bf[...], w2_bf[...])
            return carry

        lax.fori_loop(1, nblk, body, 0)

    for first, last in sorted({(cc == 0, cc == last_c) for cc in range(n_chunks)}):
        chunk_is = (c == 0) if first else ((c == last_c) if last else ((c > 0) & (c < last_c)))

        @pl.when((nblk > 0) & chunk_is)
        def _():
            run_blocks(first, last)

    @pl.when((c == 0) & (i + 1 < n_items))
    def _():
        for_blocks(i + 1, lambda it, b: load(it, b).start())

    @pl.when(c == last_c)
    def _():
        for_blocks(i, lambda it, b: store(it, b).start())

        @pl.when(i == n_items - 1)
        def _():
            for_blocks(i, lambda it, b: store(it, b).wait())


def _moe_call(item_e, item_row0, item_nblk, xs, w1, w3, w2):
    padded_rows, row = xs.shape[0], xs.shape[1:]
    d = w1.shape[1]
    n_items = item_e.shape[0]
    f = w1.shape[2]
    nc = f // MOE_F_CHUNK
    chunk = lambda i, c, ib: jnp.where(ib[i] > 0, c, nc - 1)
    return pl.pallas_call(
        functools.partial(_moe_kernel, n_chunks=nc),
        grid_spec=pltpu.PrefetchScalarGridSpec(
            num_scalar_prefetch=3,
            grid=(n_items, nc),
            in_specs=[pl.BlockSpec(memory_space=pl.ANY),
                      pl.BlockSpec((1, d, MOE_F_CHUNK), lambda i, c, ie, ir, ib: (ie[i], 0, chunk(i, c, ib))),
                      pl.BlockSpec((1, d, MOE_F_CHUNK), lambda i, c, ie, ir, ib: (ie[i], 0, chunk(i, c, ib))),
                      pl.BlockSpec((1, MOE_F_CHUNK, d), lambda i, c, ie, ir, ib: (ie[i], chunk(i, c, ib), 0))],
            out_specs=pl.BlockSpec(memory_space=pl.ANY),
            scratch_shapes=[pltpu.VMEM((MOE_ROW_GROUP,) + row, xs.dtype),
                            pltpu.VMEM((MOE_ROW_GROUP, d), BF16),
                            pltpu.VMEM((MOE_ROW_GROUP, d), F32),
                            pltpu.VMEM((MOE_ROW_GROUP,) + row, xs.dtype),
                            pltpu.VMEM((d, MOE_F_CHUNK), BF16),
                            pltpu.VMEM((d, MOE_F_CHUNK), BF16),
                            pltpu.VMEM((MOE_F_CHUNK, d), BF16),
                            pltpu.SemaphoreType.DMA((2,))]),
        out_shape=jax.ShapeDtypeStruct((padded_rows,) + row, xs.dtype),
        compiler_params=_cparams(("arbitrary", "arbitrary")),
        name="moe",
    )(item_e, item_row0, item_nblk, xs, w1, w3, w2)


COMBINE_TM = 256


def _combine_kernel(dest_ref, x1_ref, gw_ref, g2_ref, ys, o_ref, y0, y1, sem, *, tok_base, n_tok):
    tok0 = tok_base + pl.program_id(0) * COMBINE_TM

    def issue(g, carry):
        base = pl.multiple_of(g * ROW_DMA_GROUP, ROW_DMA_GROUP)
        for j in range(ROW_DMA_GROUP):
            _row_copy(ys, dest_ref[tok0 + base + j], y0, base + j, sem).start(priority=0)
            _row_copy(ys, dest_ref[n_tok + tok0 + base + j], y1, base + j, sem).start(priority=1)
        return carry

    lax.fori_loop(0, COMBINE_TM // ROW_DMA_GROUP, issue, 0)
    for buf in (y0, y1):
        pltpu.make_async_copy(ys.at[pl.ds(0, COMBINE_TM)], buf, sem).wait()
    gw = gw_ref[...]
    moe = (gw[:, 0:1] * _unpack_halves(_tiles_to_rows(y0[...]))
           + gw[:, 1:2] * _unpack_halves(_tiles_to_rows(y1[...])))
    o_ref[...] = x1_ref[...] + g2_ref[0] * moe


def _combine_call(dest_flat, x1, gw, g2, ys, tok_base, n_tok, tiles_per_group):
    m, d = x1.shape
    return pl.pallas_call(
        functools.partial(_combine_kernel, tok_base=tok_base, n_tok=n_tok),
        grid_spec=pltpu.PrefetchScalarGridSpec(
            num_scalar_prefetch=1,
            grid=(m // COMBINE_TM,),
            in_specs=[pl.BlockSpec((COMBINE_TM, d), lambda i, dr: (i, 0)),
                      pl.BlockSpec((COMBINE_TM, 2), lambda i, dr: (i, 0)),
                      pl.BlockSpec((1, 1, d), lambda i, dr: (i // tiles_per_group, 0, 0)),
                      pl.BlockSpec(memory_space=pl.ANY)],
            out_specs=pl.BlockSpec((COMBINE_TM, d), lambda i, dr: (i, 0)),
            scratch_shapes=[pltpu.VMEM((COMBINE_TM,) + ys.shape[1:], ys.dtype),
                            pltpu.VMEM((COMBINE_TM,) + ys.shape[1:], ys.dtype),
                            pltpu.SemaphoreType.DMA]),
        out_shape=jax.ShapeDtypeStruct((m, d), F32),
        compiler_params=_cparams(("arbitrary",)),
        name="combine",
    )(dest_flat, x1, gw, g2, ys)


def kernel(x_prompt, x_sample, cache_a_k, cache_a_v, cache_b_k, cache_b_v, c, c_ctx, norm1_w, norm2_w, w_ada, b_ada, w_in, qn_a, kn_a, qn_b, kn_b, rpb_a, sink_b, w_pa, w_pb, w_out, w_rg, b_rg, w_re, b_re, w1, w3, w2):
    batch, seq, d = x_prompt.shape
    dec_batch, dec_seq, _ = x_sample.shape
    depth = norm1_w.shape[0]
    assert depth == 1, "one trunk layer"
    past = cache_a_k.shape[2]
    n_ctx, n_lat = batch * seq, dec_batch * dec_seq
    n_tok = n_ctx + n_lat

    xc = x_prompt.reshape(n_ctx, d)
    xl = x_sample.reshape(n_lat, d)

    cond = jnp.concatenate([c_ctx[None, :], c], axis=0)
    mod = _ada_call(cond, w_ada[0], b_ada[0][None, :])
    sh1, sc1, g1, sh2, sc2, g2 = [mod[:, i * d:(i + 1) * d][:, None, :] for i in range(6)]
    ctx_rows, lat_rows = slice(0, 1), slice(1, 1 + dec_batch)

    nw1, nw2 = norm1_w[0][None, :], norm2_w[0][None, :]
    qna, kna, qnb, knb = qn_a[0][None, :], kn_a[0][None, :], qn_b[0][None, :], kn_b[0][None, :]
    sink = sink_b[0]
    n_logits = N_EXPERTS + N_GROUPS
    wr = jnp.concatenate([w_re[0], w_rg[0], jnp.zeros((d, LOGIT_PAD - n_logits), F32)], axis=1)
    br = jnp.concatenate([b_re[0], b_rg[0], jnp.zeros((LOGIT_PAD - n_logits,), F32)])[:, None]

    proj_c, gates_c = _inproj_call(xc, nw1, sc1[ctx_rows], sh1[ctx_rows], w_in[0], n_ctx)
    proj_l, gates_l = _inproj_call(xl, nw1, sc1[lat_rows], sh1[lat_rows], w_in[0], dec_seq)

    oa_c, ob_c, new_a_k, new_a_v, new_b_k, new_b_v = _ctx_attn_call(proj_c, seq, qna, kna, qnb, knb, sink)

    bias_blocks, pair0, bias_index = _na_bias_blocks(rpb_a[0], dec_seq)
    cos, sin_a, sin_b = _rope_tables(dec_seq)
    rows_of = lambda cache: cache.reshape(-1, HEAD_DIM)
    oa_l = _lat_attn_a_call(proj_l, dec_seq, past, rows_of(cache_a_k), rows_of(cache_a_v),
                            bias_blocks, pair0, bias_index, qna, kna)
    ob_l = _lat_attn_b_call(proj_l, dec_seq, past, rows_of(cache_b_k), rows_of(cache_b_v),
                            cos, sin_a, sin_b, qnb, knb, sink)

    mix_c = _mix_call(oa_c, ob_c, gates_c, w_pa[0], w_pb[0])
    mix_l = _mix_call(oa_l, ob_l, gates_l, w_pa[0], w_pb[0])
    x1_c, h2_c, lg_c = _outproj_call(mix_c, w_out[0], xc, g1[ctx_rows], sc2[ctx_rows], sh2[ctx_rows],
                                     nw2, wr, n_ctx // MERGE_TM)
    x1_l, h2_l, lg_l = _outproj_call(mix_l, w_out[0], xl, g1[lat_rows], sc2[lat_rows], sh2[lat_rows],
                                     nw2, wr, dec_seq // MERGE_TM)

    eid, gw, rank, cnt = _route_call(lg_c, lg_l, br)
    padded_rows, n_items = _moe_layout(2 * n_tok)
    dest, tail, item_e, item_row0, item_nblk = _routing_tables(eid[:2], rank[:2], cnt[:, 0], n_items)
    dest_flat = dest.reshape(-1)
    xs = _dispatch_call(dest_flat, tail, h2_c, h2_l, padded_rows)
    ys = _moe_call(item_e, item_row0, item_nblk, xs, w1[0], w3[0], w2[0])
    gw_t = gw[:2].T
    y_c = _combine_call(dest_flat, x1_c, gw_t[:n_ctx], g2[ctx_rows], ys, 0, n_tok, n_ctx // COMBINE_TM)
    y_l = _combine_call(dest_flat, x1_l, gw_t[n_ctx:], g2[lat_rows], ys, n_ctx, n_tok, dec_seq // COMBINE_TM)

    state = lambda a, heads: a.reshape(batch, 1, seq, heads, HEAD_DIM)
    return (y_c.reshape(batch, seq, d), y_l.reshape(dec_batch, dec_seq, d),
            state(new_a_k, NA_HEADS), state(new_a_v, NA_HEADS),
            state(new_b_k, NB_KV_HEADS), state(new_b_v, NB_KV_HEADS))
```

```python
import functools

import jax
import jax.numpy as jnp
import numpy as np
from jax import lax
from jax.experimental import pallas as pl
from jax.experimental.pallas import tpu as pltpu

D_MODEL = 2048
HEAD_DIM = 128
NA_HEADS = 8
NA_WIDTH = NA_HEADS * HEAD_DIM
NB_Q_HEADS = 8
NB_KV_HEADS = 2
NB_GROUP = NB_Q_HEADS // NB_KV_HEADS
NB_WIDTH = NB_Q_HEADS * HEAD_DIM
NB_KV_WIDTH = NB_KV_HEADS * HEAD_DIM
GRID_W = 64
NA_WIN_ROWS = 8
NA_WIN_COLS = 16
WINDOW = 128
N_GROUPS = 4
EXPERTS_PER_GROUP = 8
N_EXPERTS = N_GROUPS * EXPERTS_PER_GROUP
D_EXPERT = 1024
IN_WIDTH = 3 * NA_WIDTH + NB_WIDTH + 2 * NB_KV_WIDTH + 2 * D_MODEL
ROPE_BASE = 10000.0
NORM_EPS = 1e-6
NEG_INF = -1e30
ATTN_SCALE = HEAD_DIM ** -0.5

QA_HEAD0 = 0
KA_HEAD0 = NA_HEADS
VA_HEAD0 = 2 * NA_HEADS
QB_HEAD0 = 3 * NA_HEADS
KB_HEAD0 = QB_HEAD0 + NB_Q_HEADS
VB_HEAD0 = KB_HEAD0 + NB_KV_HEADS
GATE_COL0 = (VB_HEAD0 + NB_KV_HEADS) * HEAD_DIM

LOGIT_PAD = 128
MOE_ROW_BLOCK = 256
MOE_ROW_GROUP = 1024
MOE_F_CHUNK = 512
VMEM_LIMIT = 56 * 1024 * 1024

F32 = jnp.float32
BF16 = jnp.bfloat16


def _cparams(sem, vmem_limit=VMEM_LIMIT):
    return pltpu.CompilerParams(dimension_semantics=sem, vmem_limit_bytes=vmem_limit)


def _rms(x, w):
    x = x.astype(F32)
    return x * lax.rsqrt(jnp.mean(x * x, axis=-1, keepdims=True) + NORM_EPS) * w


def _dot(a, b):
    return jnp.dot(a, b, preferred_element_type=F32)


def _dot_nt(a, b):
    return lax.dot_general(a, b, (((1,), (1,)), ((), ())), preferred_element_type=F32)


def _pack_halves(x):
    n = x.shape[1] // 2
    lo = lax.bitcast_convert_type(x[:, :n].astype(BF16).astype(F32), jnp.uint32)
    hi = lax.bitcast_convert_type(x[:, n:].astype(BF16).astype(F32), jnp.uint32)
    return hi | (lo >> 16)


def _unpack_halves(w):
    lo = lax.bitcast_convert_type(w << 16, F32)
    hi = lax.bitcast_convert_type(w & jnp.uint32(0xFFFF0000), F32)
    return jnp.concatenate([lo, hi], axis=1)


ROW_TILE = (8, 128)


def _rows_to_tiles(w):
    return w.reshape((w.shape[0],) + ROW_TILE)


def _tiles_to_rows(t):
    return t.reshape(t.shape[0], ROW_TILE[0] * ROW_TILE[1])


def _dot_split(a, b):
    a_hi = a.astype(BF16)
    a_lo = (a - a_hi.astype(F32)).astype(BF16)
    b_hi = b.astype(BF16)
    b_lo = (b - b_hi.astype(F32)).astype(BF16)
    return _dot(a_hi, b_hi) + (_dot(a_lo, b_hi) + _dot(a_hi, b_lo))


ADA_ROWS = 8
ADA_TN = 1024


def _ada_kernel(c_ref, w_ref, b_ref, o_ref):
    n_rows, d, lanes = c_ref.shape
    tn = w_ref.shape[1]

    def body(kb, acc):
        ks = pl.ds(pl.multiple_of(kb * 8, 8), 8)
        w = w_ref[ks, :]
        out = []
        for r in range(n_rows):
            c = c_ref[r, ks, :]
            s = c * jax.nn.sigmoid(c)
            out.append(acc[r] + w * jnp.concatenate([s] * (tn // lanes), axis=1))
        return tuple(out)

    acc = lax.fori_loop(0, d // 8, body, tuple(jnp.zeros((8, tn), F32) for _ in range(n_rows)), unroll=8)
    ri = lax.broadcasted_iota(jnp.int32, (ADA_ROWS, tn), 0)
    res = jnp.zeros((ADA_ROWS, tn), F32)
    for r in range(n_rows):
        row = jnp.sum(acc[r], axis=0, keepdims=True) + b_ref[...]
        res = jnp.where(ri == r, row, res)
    o_ref[...] = res


def _ada_call(cond, w_ada, b_ada):
    n_rows, d = cond.shape
    n = w_ada.shape[1]
    lanes = 128
    cond_lanes = jnp.broadcast_to(cond[:, :, None], (n_rows, d, lanes))
    return pl.pallas_call(
        _ada_kernel,
        grid=(n // ADA_TN,),
        in_specs=[pl.BlockSpec((n_rows, d, lanes), lambda j: (0, 0, 0)),
                  pl.BlockSpec((d, ADA_TN), lambda j: (0, j)),
                  pl.BlockSpec((1, ADA_TN), lambda j: (0, j))],
        out_specs=pl.BlockSpec((ADA_ROWS, ADA_TN), lambda j: (0, j)),
        out_shape=jax.ShapeDtypeStruct((ADA_ROWS, n), F32),
        compiler_params=_cparams(("arbitrary",)),
        name="ada",
    )(cond_lanes, w_ada, b_ada)


INPROJ_TM = 2048
INPROJ_TN = 512
NORM_ROWS = 128


def _inproj_kernel(x_hbm, nw_ref, sc_ref, sh_ref, w_ref, o_ref, x_buf, h_scr, sem):
    i = pl.program_id(0)
    j = pl.program_id(1)
    tm = x_buf.shape[0]
    fetch = lambda tile: pltpu.make_async_copy(
        x_hbm.at[pl.ds(pl.multiple_of(tile * tm, tm), tm), :], x_buf, sem)

    @pl.when(j == 0)
    def _():
        @pl.when(i == 0)
        def _():
            fetch(0).start()

        fetch(i).wait()
        nw = nw_ref[...]
        sc = 1.0 + sc_ref[0]
        sh = sh_ref[0]

        def body(r, carry):
            rows = pl.ds(pl.multiple_of(r * NORM_ROWS, NORM_ROWS), NORM_ROWS)
            h_scr[rows, :] = (_rms(x_buf[rows, :], nw) * sc + sh).astype(BF16)
            return carry

        lax.fori_loop(0, tm // NORM_ROWS, body, 0)

    @pl.when((j == 1) & (i + 1 < pl.num_programs(0)))
    def _():
        fetch(i + 1).start()

    o_ref[...] = _dot(h_scr[...], w_ref[...].astype(BF16)).astype(o_ref.dtype)


def _inproj_call(x, nw, sc, sh, w_in, rows_per_group):
    m, d = x.shape
    n = w_in.shape[1]
    assert n // INPROJ_TN >= 2, "the next token tile is requested in the second column step"
    tm = min(INPROJ_TM, rows_per_group)
    tiles_per_group = rows_per_group // tm
    grp = lambda i, j: (i // tiles_per_group, 0, 0)
    return pl.pallas_call(
        _inproj_kernel,
        grid=(m // tm, n // INPROJ_TN),
        in_specs=[pl.BlockSpec(memory_space=pl.ANY),
                  pl.BlockSpec((1, d), lambda i, j: (0, 0)),
                  pl.BlockSpec((1, 1, d), grp),
                  pl.BlockSpec((1, 1, d), grp),
                  pl.BlockSpec((d, INPROJ_TN), lambda i, j: (0, j))],
        out_specs=pl.BlockSpec((tm, INPROJ_TN), lambda i, j: (i, j)),
        out_shape=jax.ShapeDtypeStruct((m, n), BF16),
        scratch_shapes=[pltpu.VMEM((tm, d), F32), pltpu.VMEM((tm, d), BF16), pltpu.SemaphoreType.DMA],
        compiler_params=_cparams(("arbitrary", "arbitrary")),
        name="inproj",
    )(x, nw, sc, sh, w_in)


def _with_ones(v):
    return jnp.concatenate([v.astype(BF16), jnp.ones(v.shape, BF16)], axis=1)


def _softmax_pv(scores, values_with_ones, sink=None):
    m = None
    for s in scores:
        ms = jnp.max(s, axis=-1, keepdims=True)
        m = ms if m is None else jnp.maximum(m, ms)
    if sink is not None:
        m = jnp.maximum(m, sink)
    acc = None
    for s, v1 in zip(scores, values_with_ones):
        pv = _dot(jnp.exp(s - m).astype(BF16), v1)
        acc = pv if acc is None else acc + pv
    d = acc.shape[1] // 2
    den = acc[:, d:]
    if sink is not None:
        den = den + jnp.exp(sink - m)
    return acc[:, :d] / den


def _ctx_attn_kernel(qa_ref, ka_ref, va_ref, qb_ref, kb_ref, vb_ref,
                     qna_ref, kna_ref, qnb_ref, knb_ref, sink_ref,
                     oa_ref, ob_ref, nak_ref, nav_ref, nbk_ref, nbv_ref):
    qna, kna, qnb, knb = qna_ref[...], kna_ref[...], qnb_ref[...], knb_ref[...]
    seq = qa_ref.shape[0]
    for h in range(NA_HEADS):
        cols = slice(h * HEAD_DIM, (h + 1) * HEAD_DIM)
        q = (_rms(qa_ref[:, cols], qna) * ATTN_SCALE).astype(BF16)
        k = _rms(ka_ref[:, cols], kna)
        v = va_ref[:, cols]
        nak_ref[pl.ds(h, seq, stride=NA_HEADS), :] = k
        nav_ref[pl.ds(h, seq, stride=NA_HEADS), :] = v.astype(F32)
        s = _dot_nt(q, k.astype(BF16))
        oa_ref[:, cols] = _softmax_pv([s], [_with_ones(v)]).astype(oa_ref.dtype)
    for kv in range(NB_KV_HEADS):
        kcols = slice(kv * HEAD_DIM, (kv + 1) * HEAD_DIM)
        k = _rms(kb_ref[:, kcols], knb)
        v = vb_ref[:, kcols]
        nbk_ref[pl.ds(kv, seq, stride=NB_KV_HEADS), :] = k
        nbv_ref[pl.ds(kv, seq, stride=NB_KV_HEADS), :] = v.astype(F32)
        kb16 = k.astype(BF16)
        vb1 = _with_ones(v)
        for g in range(NB_GROUP):
            hq = kv * NB_GROUP + g
            cols = slice(hq * HEAD_DIM, (hq + 1) * HEAD_DIM)
            q = (_rms(qb_ref[:, cols], qnb) * ATTN_SCALE).astype(BF16)
            s = _dot_nt(q, kb16)
            ob_ref[:, cols] = _softmax_pv([s], [vb1], sink=sink_ref[hq]).astype(ob_ref.dtype)


def _ctx_attn_call(proj, seq, qna, kna, qnb, knb, sink):
    m = proj.shape[0]
    nb = m // seq
    wide = lambda blk: pl.BlockSpec((seq, NA_WIDTH), lambda b: (b, blk))
    narrow = lambda blk: pl.BlockSpec((seq, NB_KV_WIDTH), lambda b: (b, blk))
    vec = pl.BlockSpec((1, HEAD_DIM), lambda b: (0, 0))
    return pl.pallas_call(
        _ctx_attn_kernel,
        grid=(nb,),
        in_specs=[wide(QA_HEAD0 // NA_HEADS), wide(KA_HEAD0 // NA_HEADS), wide(VA_HEAD0 // NA_HEADS),
                  wide(QB_HEAD0 // NA_HEADS), narrow(KB_HEAD0 // NB_KV_HEADS), narrow(VB_HEAD0 // NB_KV_HEADS),
                  vec, vec, vec, vec,
                  pl.BlockSpec(memory_space=pltpu.SMEM)],
        out_specs=[pl.BlockSpec((seq, NA_WIDTH), lambda b: (b, 0)),
                   pl.BlockSpec((seq, NB_WIDTH), lambda b: (b, 0)),
                   pl.BlockSpec((seq * NA_HEADS, HEAD_DIM), lambda b: (b, 0)),
                   pl.BlockSpec((seq * NA_HEADS, HEAD_DIM), lambda b: (b, 0)),
                   pl.BlockSpec((seq * NB_KV_HEADS, HEAD_DIM), lambda b: (b, 0)),
                   pl.BlockSpec((seq * NB_KV_HEADS, HEAD_DIM), lambda b: (b, 0))],
        out_shape=[jax.ShapeDtypeStruct((m, NA_WIDTH), BF16),
                   jax.ShapeDtypeStruct((m, NB_WIDTH), BF16),
                   jax.ShapeDtypeStruct((m * NA_HEADS, HEAD_DIM), F32),
                   jax.ShapeDtypeStruct((m * NA_HEADS, HEAD_DIM), F32),
                   jax.ShapeDtypeStruct((m * NB_KV_HEADS, HEAD_DIM), F32),
                   jax.ShapeDtypeStruct((m * NB_KV_HEADS, HEAD_DIM), F32)],
        compiler_params=_cparams(("arbitrary",)),
        name="ctx_attn",
    )(proj, proj, proj, proj, proj, proj, qna, kna, qnb, knb, sink)


def _rope(x, cos, sin_a, sin_b):
    quarter = HEAD_DIM // 4
    return (x * cos + pltpu.roll(x, HEAD_DIM - quarter, 1) * sin_a
            + pltpu.roll(x, quarter, 1) * sin_b)


def _head_rows(cache_ref, head, n_heads):
    past = cache_ref.shape[0] // n_heads
    return cache_ref[pl.ds(head, past, stride=n_heads), :]


def _lat_attn_a_kernel(q_ref, k_ref, v_ref, ck_ref, cv_ref, cb_ref, qn_ref, kn_ref, o_ref, *, pair0, index):
    head = pl.program_id(0)
    q = (_rms(q_ref[...], qn_ref[...]) * ATTN_SCALE).astype(BF16)
    k = _rms(k_ref[...], kn_ref[...]).astype(BF16)
    v1 = _with_ones(v_ref[...])
    ck = _head_rows(ck_ref, head, NA_HEADS).astype(BF16)
    cv1 = _with_ones(_head_rows(cv_ref, head, NA_HEADS))
    pair = 2 * GRID_W
    rows_per_block = pair // GRID_W
    n_pairs = len(index[0])
    for blk in range(len(pair0) // rows_per_block):
        grid_rows = range(blk * rows_per_block, (blk + 1) * rows_per_block)
        p0 = pair0[grid_rows[0]]
        assert all(pair0[r] == p0 for r in grid_rows), "query rows of a block share their key pairs"
        rows = slice(blk * pair, (blk + 1) * pair)
        keys = slice(p0 * pair, (p0 + n_pairs) * pair)
        bias = jnp.concatenate(
            [jnp.concatenate([cb_ref[u] for u in index[r]], axis=1) for r in grid_rows], axis=0)
        qs = q[rows]
        s_win = _dot_nt(qs, k[keys]) + bias
        s_ctx = _dot_nt(qs, ck)
        o_ref[rows, :] = _softmax_pv([s_win, s_ctx], [v1[keys], cv1]).astype(o_ref.dtype)


def _lat_attn_b_kernel(q_ref, k_ref, v_ref, ck_ref, cv_ref, cos_ref, sina_ref, sinb_ref,
                       qn_ref, kn_ref, sink_ref, o_ref):
    cos, sin_a, sin_b = cos_ref[...], sina_ref[...], sinb_ref[...]
    q = _rope(_rms(q_ref[...], qn_ref[...]), cos, sin_a, sin_b)
    k = _rope(_rms(k_ref[...], kn_ref[...]), cos, sin_a, sin_b)
    q = (q * ATTN_SCALE).astype(BF16)
    k = k.astype(BF16)
    v1 = _with_ones(v_ref[...])
    length = q.shape[0]
    kv = pl.program_id(0) // NB_GROUP
    ck = _head_rows(ck_ref, kv, NB_KV_HEADS).astype(BF16)
    cv1 = _with_ones(_head_rows(cv_ref, kv, NB_KV_HEADS))
    sink = sink_ref[pl.program_id(0)]
    for qb in range(length // WINDOW):
        rows = slice(qb * WINDOW, (qb + 1) * WINDOW)
        lo, hi = max(0, (qb - 1) * WINDOW), min(length, (qb + 2) * WINDOW)
        qs = q[rows]
        s_win = _dot_nt(qs, k[lo:hi])
        qi = qb * WINDOW + lax.broadcasted_iota(jnp.int32, s_win.shape, 0)
        kj = lo + lax.broadcasted_iota(jnp.int32, s_win.shape, 1)
        s_win = jnp.where(jnp.abs(qi - kj) <= WINDOW, s_win, NEG_INF)
        s_ctx = _dot_nt(qs, ck)
        o_ref[rows, :] = _softmax_pv([s_win, s_ctx], [v1[lo:hi], cv1], sink=sink).astype(o_ref.dtype)


def _lat_attn_a_call(proj, length, past, ck, cv, cb, pair0, index, qn, kn):
    m = proj.shape[0]
    head = lambda h0: pl.BlockSpec((length, HEAD_DIM), lambda h, b: (b, h0 + h))
    cache = pl.BlockSpec((past * NA_HEADS, HEAD_DIM), lambda h, b: (b, 0))
    vec = pl.BlockSpec((1, HEAD_DIM), lambda h, b: (0, 0))
    return pl.pallas_call(
        functools.partial(_lat_attn_a_kernel, pair0=pair0, index=index),
        grid=(NA_HEADS, m // length),
        in_specs=[head(QA_HEAD0), head(KA_HEAD0), head(VA_HEAD0), cache, cache,
                  pl.BlockSpec((None,) + cb.shape[1:], lambda h, b: (h, 0, 0, 0)), vec, vec],
        out_specs=pl.BlockSpec((length, HEAD_DIM), lambda h, b: (b, h)),
        out_shape=jax.ShapeDtypeStruct((m, NA_WIDTH), BF16),
        compiler_params=_cparams(("arbitrary", "arbitrary")),
        name="lat_attn_a",
    )(proj, proj, proj, ck, cv, cb, qn, kn)


def _lat_attn_b_call(proj, length, past, ck, cv, cos, sin_a, sin_b, qn, kn, sink):
    m = proj.shape[0]
    qspec = pl.BlockSpec((length, HEAD_DIM), lambda h, b: (b, QB_HEAD0 + h))
    kvspec = lambda h0: pl.BlockSpec((length, HEAD_DIM), lambda h, b: (b, h0 + h // NB_GROUP))
    cache = pl.BlockSpec((past * NB_KV_HEADS, HEAD_DIM), lambda h, b: (b, 0))
    table = pl.BlockSpec((length, HEAD_DIM), lambda h, b: (0, 0))
    vec = pl.BlockSpec((1, HEAD_DIM), lambda h, b: (0, 0))
    return pl.pallas_call(
        _lat_attn_b_kernel,
        grid=(NB_Q_HEADS, m // length),
        in_specs=[qspec, kvspec(KB_HEAD0), kvspec(VB_HEAD0), cache, cache,
                  table, table, table, vec, vec, pl.BlockSpec(memory_space=pltpu.SMEM)],
        out_specs=pl.BlockSpec((length, HEAD_DIM), lambda h, b: (b, h)),
        out_shape=jax.ShapeDtypeStruct((m, NB_WIDTH), BF16),
        compiler_params=_cparams(("arbitrary", "arbitrary")),
        name="lat_attn_b",
    )(proj, proj, proj, ck, cv, cos, sin_a, sin_b, qn, kn, sink)


def _na_bias_blocks(rpb, length):
    rows = length // GRID_W
    kr_n = min(NA_WIN_ROWS, rows)
    n_pairs = min(kr_n // 2 + 1, rows // 2)
    r = np.arange(rows)
    c = np.arange(GRID_W)
    r0 = np.clip(r - kr_n // 2, 0, rows - kr_n)
    c0 = np.clip(c - NA_WIN_COLS // 2, 0, GRID_W - NA_WIN_COLS)
    pair0 = np.minimum(r0 // 2, rows // 2 - n_pairs)
    kr = 2 * (pair0[:, None, None] + np.arange(n_pairs)[None, :, None]) + np.arange(2)[None, None, :]
    row_ok = (kr >= r0[:, None, None]) & (kr < r0[:, None, None] + kr_n)
    col_ok = (c[None, :] >= c0[:, None]) & (c[None, :] < c0[:, None] + NA_WIN_COLS)
    dr = kr - r[:, None, None] + (NA_WIN_ROWS - 1)
    dc = np.clip(c[None, :] - c[:, None], -(NA_WIN_COLS - 1), NA_WIN_COLS - 1) + (NA_WIN_COLS - 1)
    offs = np.where(row_ok, dr, -1).reshape(-1, 2)
    uniq, inverse = np.unique(offs, axis=0, return_inverse=True)
    index = inverse.reshape(rows, n_pairs)
    row_sel = (uniq[:, :, None] == np.arange(2 * NA_WIN_ROWS - 1)[None, None, :]).astype(np.float32)
    col_hit = (dc[None] == np.arange(2 * NA_WIN_COLS - 1)[:, None, None]) & col_ok[None]
    col_sel = np.zeros((2,) + col_hit.shape[:2] + (2 * GRID_W,), np.float32)
    for half in range(2):
        col_sel[half, :, :, half * GRID_W:(half + 1) * GRID_W] = col_hit
    hi = lax.Precision.HIGHEST
    per_col = jnp.einsum("hde,lecn->hldcn", rpb.astype(F32), col_sel, precision=hi)
    table = jnp.einsum("uld,hldcn->hucn", row_sel, per_col, precision=hi)
    valid = ((uniq >= 0)[:, None, :, None] & col_ok[None, :, None, :]).reshape(len(uniq), GRID_W, 2 * GRID_W)
    return (jnp.where(valid[None], table, NEG_INF), tuple(int(p) for p in pair0),
            tuple(tuple(int(u) for u in row) for row in index))


def _rope_tables(length):
    t = jnp.arange(length)
    row = (t // GRID_W).astype(F32)
    col = (t % GRID_W).astype(F32)
    n_freq = HEAD_DIM // 4
    inv = ROPE_BASE ** (-jnp.arange(n_freq, dtype=F32) / n_freq)
    ar = row[:, None] * inv
    ac = col[:, None] * inv
    ang = jnp.concatenate([ar, ar, ac, ac], axis=-1)
    cos, sin = jnp.cos(ang), jnp.sin(ang)
    lane = jnp.arange(HEAD_DIM)
    takes_left = ((lane // n_freq) % 2 == 0)[None, :]
    return cos, jnp.where(takes_left, -sin, 0.0), jnp.where(takes_left, 0.0, sin)


MERGE_TM = 512


def _resident(shape):
    zeros = (0,) * len(shape)
    return pl.BlockSpec(shape, lambda i: zeros, pipeline_mode=pl.Buffered(1))


def _mix_kernel(oa_ref, ob_ref, *refs):
    n_blk = (len(refs) - 3) // 2
    wpa_ref, wpb_ref, mix_ref = refs[-3:]
    gates = lambda blocks: jnp.concatenate([r[...] for r in blocks], axis=1).astype(F32)
    ya = _dot(oa_ref[...], wpa_ref[...].astype(BF16))
    yb = _dot(ob_ref[...], wpb_ref[...].astype(BF16))
    mix = jax.nn.sigmoid(gates(refs[:n_blk])) * ya + jax.nn.sigmoid(gates(refs[n_blk:2 * n_blk])) * yb
    mix_ref[...] = mix.astype(mix_ref.dtype)


def _mix_call(oa, ob, proj, w_pa, w_pb):
    m = oa.shape[0]
    d = w_pa.shape[1]
    row = lambda i: (i, 0)
    gate0, n_blk = GATE_COL0 // INPROJ_TN, d // INPROJ_TN
    gate_specs = [pl.BlockSpec((MERGE_TM, INPROJ_TN), lambda i, q=q: (i, gate0 + q)) for q in range(2 * n_blk)]
    return pl.pallas_call(
        _mix_kernel,
        grid=(m // MERGE_TM,),
        in_specs=[pl.BlockSpec((MERGE_TM, NA_WIDTH), row),
                  pl.BlockSpec((MERGE_TM, NB_WIDTH), row),
                  *gate_specs,
                  _resident((NA_WIDTH, d)), _resident((NB_WIDTH, d))],
        out_specs=pl.BlockSpec((MERGE_TM, d), row),
        out_shape=jax.ShapeDtypeStruct((m, d), BF16),
        compiler_params=_cparams(("arbitrary",)),
        name="mix",
    )(oa, ob, *([proj] * (2 * n_blk)), w_pa, w_pb)


def _outproj_kernel(mix_ref, wout_ref, x_ref, g1_ref, sc2_ref, sh2_ref, n2w_ref, wr_ref,
                    x1_ref, h2_ref, lg_ref, acc_ref):
    i = pl.program_id(0)
    last = pl.num_programs(0) - 1

    def finish(prod):
        x1 = x_ref[...] + g1_ref[0] * prod
        x1_ref[...] = x1
        h2 = _rms(x1, n2w_ref[...]) * (1.0 + sc2_ref[0]) + sh2_ref[0]
        h2_ref[...] = _rows_to_tiles(_pack_halves(h2))
        lg_ref[...] = _dot_split(h2, wr_ref[...])

    @pl.when(i == 0)
    def _():
        acc_ref[...] = _dot(mix_ref[...], wout_ref[...].astype(BF16))

    @pl.when((i > 0) & (i < last))
    def _():
        prod = acc_ref[...]
        acc_ref[...] = _dot(mix_ref[...], wout_ref[...].astype(BF16))
        finish(prod)

    @pl.when(i == last)
    def _():
        finish(acc_ref[...])


def _outproj_call(mix, w_out, x, g1, sc2, sh2, n2w, wr, tiles_per_group):
    m, d = x.shape
    n_tiles = m // MERGE_TM
    ahead = lambda i: (jnp.minimum(i, n_tiles - 1), 0)
    done = lambda i: (jnp.maximum(i - 1, 0), 0)
    grp = lambda i: (jnp.maximum(i - 1, 0) // tiles_per_group, 0, 0)
    return pl.pallas_call(
        _outproj_kernel,
        grid=(n_tiles + 1,),
        in_specs=[pl.BlockSpec((MERGE_TM, d), ahead),
                  _resident((d, d)),
                  pl.BlockSpec((MERGE_TM, d), done),
                  pl.BlockSpec((1, 1, d), grp), pl.BlockSpec((1, 1, d), grp), pl.BlockSpec((1, 1, d), grp),
                  pl.BlockSpec((1, d), lambda i: (0, 0)),
                  _resident((d, LOGIT_PAD))],
        out_specs=[pl.BlockSpec((MERGE_TM, d), done),
                   pl.BlockSpec((MERGE_TM,) + ROW_TILE, lambda i: (jnp.maximum(i - 1, 0), 0, 0)),
                   pl.BlockSpec((MERGE_TM, LOGIT_PAD), done)],
        out_shape=[jax.ShapeDtypeStruct((m, d), F32),
                   jax.ShapeDtypeStruct((m,) + ROW_TILE, jnp.uint32),
                   jax.ShapeDtypeStruct((m, LOGIT_PAD), F32)],
        scratch_shapes=[pltpu.VMEM((MERGE_TM, d), F32)],
        compiler_params=_cparams(("arbitrary",)),
        name="outproj",
    )(mix, w_out, x, g1, sc2, sh2, n2w, wr)


ROUTE_TM = 512


def _first_index_of_max(vals, idx, n):
    mx = jnp.max(vals, axis=0, keepdims=True)
    first = jnp.min(jnp.where(vals == mx, idx, n), axis=0, keepdims=True)
    return mx, first


def _route_kernel(lg_ctx_ref, lg_lat_ref, bias_ref, eid_ref, gw_ref, rank_ref, cnt_ref, base_ref, *, ctx_tiles):
    step = pl.program_id(0)

    @pl.when(step == 0)
    def _():
        base_ref[...] = jnp.zeros_like(base_ref)

    lg = jnp.where(step < ctx_tiles, lg_ctx_ref[...], lg_lat_ref[...])
    lt = lg.T + bias_ref[...]
    n_tok = lt.shape[1]
    le = lt[0:N_EXPERTS]
    lgrp = lt[N_EXPERTS:N_EXPERTS + N_GROUPS]
    gi = lax.broadcasted_iota(jnp.int32, (N_GROUPS, n_tok), 0)
    gmax, gsel = _first_index_of_max(lgrp, gi, N_GROUPS)
    pg_sel = 1.0 / jnp.sum(jnp.exp(lgrp - gmax), axis=0, keepdims=True)
    le_sel = jnp.zeros((EXPERTS_PER_GROUP, n_tok), F32)
    for g in range(N_GROUPS):
        le_sel = jnp.where(gsel == g, le[g * EXPERTS_PER_GROUP:(g + 1) * EXPERTS_PER_GROUP], le_sel)
    ei = lax.broadcasted_iota(jnp.int32, (EXPERTS_PER_GROUP, n_tok), 0)
    v0, i0 = _first_index_of_max(le_sel, ei, EXPERTS_PER_GROUP)
    rest = jnp.where(ei == i0, -jnp.inf, le_sel)
    v1, i1 = _first_index_of_max(rest, ei, EXPERTS_PER_GROUP)
    e1 = jnp.exp(v1 - v0)
    w0 = pg_sel / (1.0 + e1)
    w1 = pg_sel * e1 / (1.0 + e1)
    eid0 = gsel * EXPERTS_PER_GROUP + i0
    eid1 = gsel * EXPERTS_PER_GROUP + i1

    xi = lax.broadcasted_iota(jnp.int32, (N_EXPERTS, n_tok), 0)
    si = lax.broadcasted_iota(jnp.int32, (n_tok, n_tok), 0)
    ti = lax.broadcasted_iota(jnp.int32, (n_tok, n_tok), 1)
    before = (si < ti).astype(BF16)
    base = base_ref[...]
    hot0 = (xi == eid0).astype(F32)
    hot1 = (xi == eid1).astype(F32)
    pre0 = _dot(hot0.astype(BF16), before)
    pre1 = _dot(hot1.astype(BF16), before)
    tot0 = jnp.sum(hot0, axis=1, keepdims=True)
    tot1 = jnp.sum(hot1, axis=1, keepdims=True)
    rank0 = jnp.sum(hot0 * (base + pre0), axis=0, keepdims=True)
    rank1 = jnp.sum(hot1 * (base + tot0 + pre1), axis=0, keepdims=True)
    base = base + tot0 + tot1
    base_ref[...] = base

    ri = lax.broadcasted_iota(jnp.int32, (8, n_tok), 0)
    pick = lambda a, b: jnp.where(ri == 0, a, jnp.where(ri == 1, b, jnp.zeros_like(a)))
    eid_ref[...] = pick(eid0, eid1)
    gw_ref[...] = pick(w0, w1)
    rank_ref[...] = pick(rank0, rank1).astype(jnp.int32)
    cnt_ref[...] = jnp.broadcast_to(base, cnt_ref.shape).astype(jnp.int32)


def _route_call(logits_ctx, logits_lat, bias_col):
    ctx_tiles = logits_ctx.shape[0] // ROUTE_TM
    t = logits_ctx.shape[0] + logits_lat.shape[0]
    tok = pl.BlockSpec((8, ROUTE_TM), lambda i: (0, i))
    return pl.pallas_call(
        functools.partial(_route_kernel, ctx_tiles=ctx_tiles),
        grid=(t // ROUTE_TM,),
        in_specs=[pl.BlockSpec((ROUTE_TM, LOGIT_PAD), lambda i: (jnp.minimum(i, ctx_tiles - 1), 0)),
                  pl.BlockSpec((ROUTE_TM, LOGIT_PAD), lambda i: (jnp.maximum(i - ctx_tiles, 0), 0)),
                  pl.BlockSpec((LOGIT_PAD, 1), lambda i: (0, 0))],
        out_specs=[tok, tok, tok, pl.BlockSpec((N_EXPERTS, 128), lambda i: (0, 0))],
        out_shape=[jax.ShapeDtypeStruct((8, t), jnp.int32),
                   jax.ShapeDtypeStruct((8, t), F32),
                   jax.ShapeDtypeStruct((8, t), jnp.int32),
                   jax.ShapeDtypeStruct((N_EXPERTS, 128), jnp.int32)],
        scratch_shapes=[pltpu.VMEM((N_EXPERTS, 1), F32)],
        compiler_params=_cparams(("arbitrary",)),
        name="route",
    )(logits_ctx, logits_lat, bias_col)


def _moe_layout(n_pairs):
    padded_rows = -(-(n_pairs + N_EXPERTS * (MOE_ROW_BLOCK - 1)) // MOE_ROW_BLOCK) * MOE_ROW_BLOCK
    n_items = (padded_rows + N_EXPERTS * (MOE_ROW_GROUP - MOE_ROW_BLOCK)) // MOE_ROW_GROUP
    return padded_rows, n_items


def _routing_tables(eid, rank, counts, n_items):
    padded = (counts + MOE_ROW_BLOCK - 1) // MOE_ROW_BLOCK * MOE_ROW_BLOCK
    pad_end = jnp.cumsum(padded)
    pad_start = pad_end - padded
    hot = eid[..., None] == jnp.arange(N_EXPERTS, dtype=jnp.int32)
    dest = (jnp.sum(jnp.where(hot, pad_start, 0), axis=-1) + rank).astype(jnp.int32)
    tail = jnp.where(padded > counts, pad_end - MOE_ROW_BLOCK, -1).astype(jnp.int32)
    per_expert = (padded + MOE_ROW_GROUP - 1) // MOE_ROW_GROUP
    item_end = jnp.cumsum(per_expert)
    item_start = item_end - per_expert
    total = item_end[-1]
    ii = jnp.arange(n_items, dtype=jnp.int32)
    owner = lambda v: jnp.minimum(jnp.sum(item_end[None, :] <= v[:, None], axis=1), N_EXPERTS - 1).astype(jnp.int32)
    e_of = owner(ii)
    sel = e_of[:, None] == jnp.arange(N_EXPERTS, dtype=jnp.int32)
    pick = lambda v: jnp.sum(jnp.where(sel, v[None, :], 0), axis=1)
    valid = ii < total
    e_last = owner(jnp.maximum(total - 1, 0)[None])[0]
    local = ii - pick(item_start)
    row0 = pick(pad_start) + local * MOE_ROW_GROUP
    nblk = jnp.clip((pick(padded) - local * MOE_ROW_GROUP) // MOE_ROW_BLOCK, 0, MOE_ROW_GROUP // MOE_ROW_BLOCK)
    item_e = jnp.where(valid, e_of, e_last).astype(jnp.int32)
    item_row0 = jnp.where(valid, row0, 0).astype(jnp.int32)
    item_nblk = jnp.where(valid, nblk, 0).astype(jnp.int32)
    return dest, tail, item_e, item_row0, item_nblk


DISPATCH_TOKENS = 256
ROW_DMA_GROUP = 8


def _row_copy(src, s, dst, d, sem):
    return pltpu.make_async_copy(src.at[pl.ds(s, 1)], dst.at[pl.ds(d, 1)], sem)


def _dispatch_kernel(dest_ref, tail_ref, h_ctx, h_lat, xs, zero_buf, sem, *, n_ctx, n_tok):
    step = pl.program_id(0)
    tail_copy = lambda e: pltpu.make_async_copy(
        zero_buf, xs.at[pl.ds(pl.multiple_of(tail_ref[e], MOE_ROW_BLOCK), MOE_ROW_BLOCK)], sem.at[1])

    @pl.when(step == 0)
    def _():
        zero_buf[...] = jnp.zeros_like(zero_buf)
        for e in range(N_EXPERTS):
            @pl.when(tail_ref[e] >= 0)
            def _():
                tail_copy(e).start()
        for e in range(N_EXPERTS):
            @pl.when(tail_ref[e] >= 0)
            def _():
                tail_copy(e).wait()

    tok0 = step * DISPATCH_TOKENS

    def scatter(src):
        def issue(g, carry):
            base = pl.multiple_of(g * ROW_DMA_GROUP, ROW_DMA_GROUP)
            for j in range(ROW_DMA_GROUP):
                _row_copy(src, base + j, xs, dest_ref[tok0 + base + j], sem.at[0]).start(priority=0)
                _row_copy(src, base + j, xs, dest_ref[n_tok + tok0 + base + j], sem.at[0]).start(priority=1)
            return carry

        lax.fori_loop(0, DISPATCH_TOKENS // ROW_DMA_GROUP, issue, 0)
        for _ in range(2):
            pltpu.make_async_copy(src, xs.at[pl.ds(0, DISPATCH_TOKENS)], sem.at[0]).wait()

    @pl.when(tok0 < n_ctx)
    def _():
        scatter(h_ctx)

    @pl.when(tok0 >= n_ctx)
    def _():
        scatter(h_lat)


def _dispatch_call(dest_flat, tail, h_ctx, h_lat, padded_rows):
    n_ctx = h_ctx.shape[0]
    row = h_ctx.shape[1:]
    n_tok = n_ctx + h_lat.shape[0]
    ctx_tiles = n_ctx // DISPATCH_TOKENS
    return pl.pallas_call(
        functools.partial(_dispatch_kernel, n_ctx=n_ctx, n_tok=n_tok),
        grid_spec=pltpu.PrefetchScalarGridSpec(
            num_scalar_prefetch=2,
            grid=(n_tok // DISPATCH_TOKENS,),
            in_specs=[pl.BlockSpec((DISPATCH_TOKENS,) + row, lambda i, dr, tr: (jnp.minimum(i, ctx_tiles - 1), 0, 0)),
                      pl.BlockSpec((DISPATCH_TOKENS,) + row, lambda i, dr, tr: (jnp.maximum(i - ctx_tiles, 0), 0, 0))],
            out_specs=pl.BlockSpec(memory_space=pl.ANY),
            scratch_shapes=[pltpu.VMEM((MOE_ROW_BLOCK,) + row, h_ctx.dtype), pltpu.SemaphoreType.DMA((2,))]),
        out_shape=jax.ShapeDtypeStruct((padded_rows,) + row, h_ctx.dtype),
        compiler_params=_cparams(("arbitrary",)),
        name="dispatch",
    )(dest_flat, tail, h_ctx, h_lat)


def _moe_kernel(item_e, item_row0, item_nblk, xs, w1_ref, w3_ref, w2_ref, ys,
                x_in, x_bf, acc, y_out, w1_bf, w3_bf, w2_bf, sem, *, n_chunks):
    i = pl.program_id(0)
    c = pl.program_id(1)
    n_items = pl.num_programs(0)
    last_c = n_chunks - 1
    nblk = item_nblk[i]
    max_blk = MOE_ROW_GROUP // MOE_ROW_BLOCK
    blk = lambda b: pl.ds(b * MOE_ROW_BLOCK, MOE_ROW_BLOCK)

    def rows_of(item, b):
        return pl.ds(pl.multiple_of(item_row0[item], MOE_ROW_BLOCK) + b * MOE_ROW_BLOCK, MOE_ROW_BLOCK)

    load = lambda item, b: pltpu.make_async_copy(xs.at[rows_of(item, b)], x_in.at[blk(b)], sem.at[0])
    store = lambda item, b: pltpu.make_async_copy(y_out.at[blk(b)], ys.at[rows_of(item, b)], sem.at[1])

    def for_blocks(item, fn):
        n = item_nblk[item]
        for b in range(max_blk):
            @pl.when(b < n)
            def _():
                fn(item, b)

    @pl.when(c == 0)
    def _():
        @pl.when(i == 0)
        def _():
            for_blocks(0, lambda it, b: load(it, b).start())

        for_blocks(i, lambda it, b: load(it, b).wait())

    @pl.when((c == last_c) & (i > 0))
    def _():
        for_blocks(i - 1, lambda it, b: store(it, b).wait())

    def run_blocks(first, last):
        def block(rows, w1, w3, w2):
            if first:
                x = _unpack_halves(_tiles_to_rows(x_in[rows])).astype(BF16)
                x_bf[rows, :] = x
            else:
                x = x_bf[rows, :]
            h1 = _dot(x, w1)
            h3 = _dot(x, w3)
            a = (h1 * jax.nn.sigmoid(h1) * h3).astype(BF16)
            y = _dot(a, w2)
            if not first:
                y = acc[rows, :] + y
            if last:
                y_out[rows] = _rows_to_tiles(_pack_halves(y))
            else:
                acc[rows, :] = y

        w1 = w1_ref[0].astype(BF16)
        w3 = w3_ref[0].astype(BF16)
        w2 = w2_ref[0].astype(BF16)
        w1_bf[...] = w1
        w3_bf[...] = w3
        w2_bf[...] = w2
        block(blk(0), w1, w3, w2)

        def body(b, carry):
            rows = pl.ds(pl.multiple_of(b * MOE_ROW_BLOCK, MOE_ROW_BLOCK), MOE_ROW_BLOCK)
            block(rows, w1_bf[...], w3_bf[...], w2_bf[...])
            return carry

        lax.fori_loop(1, nblk, body, 0)

    for first, last in sorted({(cc == 0, cc == last_c) for cc in range(n_chunks)}):
        chunk_is = (c == 0) if first else ((c == last_c) if last else ((c > 0) & (c < last_c)))

        @pl.when((nblk > 0) & chunk_is)
        def _():
            run_blocks(first, last)

    @pl.when((c == 0) & (i + 1 < n_items))
    def _():
        for_blocks(i + 1, lambda it, b: load(it, b).start())

    @pl.when(c == last_c)
    def _():
        for_blocks(i, lambda it, b: store(it, b).start())

        @pl.when(i == n_items - 1)
        def _():
            for_blocks(i, lambda it, b: store(it, b).wait())


def _moe_call(item_e, item_row0, item_nblk, xs, w1, w3, w2):
    padded_rows, row = xs.shape[0], xs.shape[1:]
    d = w1.shape[1]
    n_items = item_e.shape[0]
    f = w1.shape[2]
    nc = f // MOE_F_CHUNK
    chunk = lambda i, c, ib: jnp.where(ib[i] > 0, c, nc - 1)
    return pl.pallas_call(
        functools.partial(_moe_kernel, n_chunks=nc),
        grid_spec=pltpu.PrefetchScalarGridSpec(
            num_scalar_prefetch=3,
            grid=(n_items, nc),
            in_specs=[pl.BlockSpec(memory_space=pl.ANY),
                      pl.BlockSpec((1, d, MOE_F_CHUNK), lambda i, c, ie, ir, ib: (ie[i], 0, chunk(i, c, ib))),
                      pl.BlockSpec((1, d, MOE_F_CHUNK), lambda i, c, ie, ir, ib: (ie[i], 0, chunk(i, c, ib))),
                      pl.BlockSpec((1, MOE_F_CHUNK, d), lambda i, c, ie, ir, ib: (ie[i], chunk(i, c, ib), 0))],
            out_specs=pl.BlockSpec(memory_space=pl.ANY),
            scratch_shapes=[pltpu.VMEM((MOE_ROW_GROUP,) + row, xs.dtype),
                            pltpu.VMEM((MOE_ROW_GROUP, d), BF16),
                            pltpu.VMEM((MOE_ROW_GROUP, d), F32),
                            pltpu.VMEM((MOE_ROW_GROUP,) + row, xs.dtype),
                            pltpu.VMEM((d, MOE_F_CHUNK), BF16),
                            pltpu.VMEM((d, MOE_F_CHUNK), BF16),
                            pltpu.VMEM((MOE_F_CHUNK, d), BF16),
                            pltpu.SemaphoreType.DMA((2,))]),
        out_shape=jax.ShapeDtypeStruct((padded_rows,) + row, xs.dtype),
        compiler_params=_cparams(("arbitrary", "arbitrary")),
        name="moe",
    )(item_e, item_row0, item_nblk, xs, w1, w3, w2)


COMBINE_TM = 256


def _combine_kernel(dest_ref, x1_ref, gw_ref, g2_ref, ys, o_ref, y0, y1, sem, *, tok_base, n_tok):
    tok0 = tok_base + pl.program_id(0) * COMBINE_TM

    def issue(g, carry):
        base = pl.multiple_of(g * ROW_DMA_GROUP, ROW_DMA_GROUP)
        for j in range(ROW_DMA_GROUP):
            _row_copy(ys, dest_ref[tok0 + base + j], y0, base + j, sem).start(priority=0)
            _row_copy(ys, dest_ref[n_tok + tok0 + base + j], y1, base + j, sem).start(priority=1)
        return carry

    lax.fori_loop(0, COMBINE_TM // ROW_DMA_GROUP, issue, 0)
    for buf in (y0, y1):
        pltpu.make_async_copy(ys.at[pl.ds(0, COMBINE_TM)], buf, sem).wait()
    gw = gw_ref[...]
    moe = (gw[:, 0:1] * _unpack_halves(_tiles_to_rows(y0[...]))
           + gw[:, 1:2] * _unpack_halves(_tiles_to_rows(y1[...])))
    o_ref[...] = x1_ref[...] + g2_ref[0] * moe


def _combine_call(dest_flat, x1, gw, g2, ys, tok_base, n_tok, tiles_per_group):
    m, d = x1.shape
    return pl.pallas_call(
        functools.partial(_combine_kernel, tok_base=tok_base, n_tok=n_tok),
        grid_spec=pltpu.PrefetchScalarGridSpec(
            num_scalar_prefetch=1,
            grid=(m // COMBINE_TM,),
            in_specs=[pl.BlockSpec((COMBINE_TM, d), lambda i, dr: (i, 0)),
                      pl.BlockSpec((COMBINE_TM, 2), lambda i, dr: (i, 0)),
                      pl.BlockSpec((1, 1, d), lambda i, dr: (i // tiles_per_group, 0, 0)),
                      pl.BlockSpec(memory_space=pl.ANY)],
            out_specs=pl.BlockSpec((COMBINE_TM, d), lambda i, dr: (i, 0)),
            scratch_shapes=[pltpu.VMEM((COMBINE_TM,) + ys.shape[1:], ys.dtype),
                            pltpu.VMEM((COMBINE_TM,) + ys.shape[1:], ys.dtype),
                            pltpu.SemaphoreType.DMA]),
        out_shape=jax.ShapeDtypeStruct((m, d), F32),
        compiler_params=_cparams(("arbitrary",)),
        name="combine",
    )(dest_flat, x1, gw, g2, ys)


def kernel(x_prompt, x_sample, cache_a_k, cache_a_v, cache_b_k, cache_b_v, c, c_ctx, norm1_w, norm2_w, w_ada, b_ada, w_in, qn_a, kn_a, qn_b, kn_b, rpb_a, sink_b, w_pa, w_pb, w_out, w_rg, b_rg, w_re, b_re, w1, w3, w2):
    batch, seq, d = x_prompt.shape
    dec_batch, dec_seq, _ = x_sample.shape
    depth = norm1_w.shape[0]
    assert depth == 1, "one trunk layer"
    past = cache_a_k.shape[2]
    n_ctx, n_lat = batch * seq, dec_batch * dec_seq
    n_tok = n_ctx + n_lat

    xc = x_prompt.reshape(n_ctx, d)
    xl = x_sample.reshape(n_lat, d)

    cond = jnp.concatenate([c_ctx[None, :], c], axis=0)
    mod = _ada_call(cond, w_ada[0], b_ada[0][None, :])
    sh1, sc1, g1, sh2, sc2, g2 = [mod[:, i * d:(i + 1) * d][:, None, :] for i in range(6)]
    ctx_rows, lat_rows = slice(0, 1), slice(1, 1 + dec_batch)

    nw1, nw2 = norm1_w[0][None, :], norm2_w[0][None, :]
    qna, kna, qnb, knb = qn_a[0][None, :], kn_a[0][None, :], qn_b[0][None, :], kn_b[0][None, :]
    sink = sink_b[0]
    n_logits = N_EXPERTS + N_GROUPS
    wr = jnp.concatenate([w_re[0], w_rg[0], jnp.zeros((d, LOGIT_PAD - n_logits), F32)], axis=1)
    br = jnp.concatenate([b_re[0], b_rg[0], jnp.zeros((LOGIT_PAD - n_logits,), F32)])[:, None]

    proj_c = _inproj_call(xc, nw1, sc1[ctx_rows], sh1[ctx_rows], w_in[0], n_ctx)
    proj_l = _inproj_call(xl, nw1, sc1[lat_rows], sh1[lat_rows], w_in[0], dec_seq)

    oa_c, ob_c, new_a_k, new_a_v, new_b_k, new_b_v = _ctx_attn_call(proj_c, seq, qna, kna, qnb, knb, sink)

    bias_blocks, pair0, bias_index = _na_bias_blocks(rpb_a[0], dec_seq)
    cos, sin_a, sin_b = _rope_tables(dec_seq)
    rows_of = lambda cache: cache.reshape(-1, HEAD_DIM)
    oa_l = _lat_attn_a_call(proj_l, dec_seq, past, rows_of(cache_a_k), rows_of(cache_a_v),
                            bias_blocks, pair0, bias_index, qna, kna)
    ob_l = _lat_attn_b_call(proj_l, dec_seq, past, rows_of(cache_b_k), rows_of(cache_b_v),
                            cos, sin_a, sin_b, qnb, knb, sink)

    mix_c = _mix_call(oa_c, ob_c, proj_c, w_pa[0], w_pb[0])
    mix_l = _mix_call(oa_l, ob_l, proj_l, w_pa[0], w_pb[0])
    x1_c, h2_c, lg_c = _outproj_call(mix_c, w_out[0], xc, g1[ctx_rows], sc2[ctx_rows], sh2[ctx_rows],
                                     nw2, wr, n_ctx // MERGE_TM)
    x1_l, h2_l, lg_l = _outproj_call(mix_l, w_out[0], xl, g1[lat_rows], sc2[lat_rows], sh2[lat_rows],
                                     nw2, wr, dec_seq // MERGE_TM)

    eid, gw, rank, cnt = _route_call(lg_c, lg_l, br)
    padded_rows, n_items = _moe_layout(2 * n_tok)
    dest, tail, item_e, item_row0, item_nblk = _routing_tables(eid[:2], rank[:2], cnt[:, 0], n_items)
    dest_flat = dest.reshape(-1)
    xs = _dispatch_call(dest_flat, tail, h2_c, h2_l, padded_rows)
    ys = _moe_call(item_e, item_row0, item_nblk, xs, w1[0], w3[0], w2[0])
    gw_t = gw[:2].T
    y_c = _combine_call(dest_flat, x1_c, gw_t[:n_ctx], g2[ctx_rows], ys, 0, n_tok, n_ctx // COMBINE_TM)
    y_l = _combine_call(dest_flat, x1_l, gw_t[n_ctx:], g2[lat_rows], ys, n_ctx, n_tok, dec_seq // COMBINE_TM)

    state = lambda a, heads: a.reshape(batch, 1, seq, heads, HEAD_DIM)
    return (y_c.reshape(batch, seq, d), y_l.reshape(dec_batch, dec_seq, d),
            state(new_a_k, NA_HEADS), state(new_a_v, NA_HEADS),
            state(new_b_k, NB_KV_HEADS), state(new_b_v, NB_KV_HEADS))
```

```python
import functools

import jax
import jax.numpy as jnp
import numpy as np
from jax import lax
from jax.experimental import pallas as pl
from jax.experimental.pallas import tpu as pltpu

D_MODEL = 2048
HEAD_DIM = 128
NA_HEADS = 8
NA_WIDTH = NA_HEADS * HEAD_DIM
NB_Q_HEADS = 8
NB_KV_HEADS = 2
NB_GROUP = NB_Q_HEADS // NB_KV_HEADS
NB_WIDTH = NB_Q_HEADS * HEAD_DIM
NB_KV_WIDTH = NB_KV_HEADS * HEAD_DIM
GRID_W = 64
NA_WIN_ROWS = 8
NA_WIN_COLS = 16
WINDOW = 128
N_GROUPS = 4
EXPERTS_PER_GROUP = 8
N_EXPERTS = N_GROUPS * EXPERTS_PER_GROUP
D_EXPERT = 1024
IN_WIDTH = 3 * NA_WIDTH + NB_WIDTH + 2 * NB_KV_WIDTH + 2 * D_MODEL
ROPE_BASE = 10000.0
NORM_EPS = 1e-6
NEG_INF = -1e30
ATTN_SCALE = HEAD_DIM ** -0.5

QA_HEAD0 = 0
KA_HEAD0 = NA_HEADS
VA_HEAD0 = 2 * NA_HEADS
QB_HEAD0 = 3 * NA_HEADS
KB_HEAD0 = QB_HEAD0 + NB_Q_HEADS
VB_HEAD0 = KB_HEAD0 + NB_KV_HEADS
GATE_COL0 = (VB_HEAD0 + NB_KV_HEADS) * HEAD_DIM

LOGIT_PAD = 128
MOE_ROW_BLOCK = 256
MOE_ROW_GROUP = 1024
MOE_F_CHUNK = 512
VMEM_LIMIT = 56 * 1024 * 1024

F32 = jnp.float32
BF16 = jnp.bfloat16


def _cparams(sem, vmem_limit=VMEM_LIMIT):
    return pltpu.CompilerParams(dimension_semantics=sem, vmem_limit_bytes=vmem_limit)


def _rms(x, w):
    x = x.astype(F32)
    return x * lax.rsqrt(jnp.mean(x * x, axis=-1, keepdims=True) + NORM_EPS) * w


def _dot(a, b):
    return jnp.dot(a, b, preferred_element_type=F32)


def _dot_nt(a, b):
    return lax.dot_general(a, b, (((1,), (1,)), ((), ())), preferred_element_type=F32)


def _pack_halves(x):
    n = x.shape[1] // 2
    lo = lax.bitcast_convert_type(x[:, :n].astype(BF16).astype(F32), jnp.uint32)
    hi = lax.bitcast_convert_type(x[:, n:].astype(BF16).astype(F32), jnp.uint32)
    return hi | (lo >> 16)


def _unpack_halves(w):
    lo = lax.bitcast_convert_type(w << 16, F32)
    hi = lax.bitcast_convert_type(w & jnp.uint32(0xFFFF0000), F32)
    return jnp.concatenate([lo, hi], axis=1)


ROW_TILE = (8, 128)


def _rows_to_tiles(w):
    return w.reshape((w.shape[0],) + ROW_TILE)


def _tiles_to_rows(t):
    return t.reshape(t.shape[0], ROW_TILE[0] * ROW_TILE[1])


def _dot_split(a, b):
    a_hi = a.astype(BF16)
    a_lo = (a - a_hi.astype(F32)).astype(BF16)
    b_hi = b.astype(BF16)
    b_lo = (b - b_hi.astype(F32)).astype(BF16)
    return _dot(a_hi, b_hi) + (_dot(a_lo, b_hi) + _dot(a_hi, b_lo))


ADA_ROWS = 8
ADA_TN = 1024


def _ada_kernel(c_ref, w_ref, b_ref, o_ref):
    n_rows, d, lanes = c_ref.shape
    tn = w_ref.shape[1]

    def body(kb, acc):
        ks = pl.ds(pl.multiple_of(kb * 8, 8), 8)
        w = w_ref[ks, :]
        out = []
        for r in range(n_rows):
            c = c_ref[r, ks, :]
            s = c * jax.nn.sigmoid(c)
            out.append(acc[r] + w * jnp.concatenate([s] * (tn // lanes), axis=1))
        return tuple(out)

    acc = lax.fori_loop(0, d // 8, body, tuple(jnp.zeros((8, tn), F32) for _ in range(n_rows)), unroll=8)
    ri = lax.broadcasted_iota(jnp.int32, (ADA_ROWS, tn), 0)
    res = jnp.zeros((ADA_ROWS, tn), F32)
    for r in range(n_rows):
        row = jnp.sum(acc[r], axis=0, keepdims=True) + b_ref[...]
        res = jnp.where(ri == r, row, res)
    o_ref[...] = res


def _ada_call(cond, w_ada, b_ada):
    n_rows, d = cond.shape
    n = w_ada.shape[1]
    lanes = 128
    cond_lanes = jnp.broadcast_to(cond[:, :, None], (n_rows, d, lanes))
    return pl.pallas_call(
        _ada_kernel,
        grid=(n // ADA_TN,),
        in_specs=[pl.BlockSpec((n_rows, d, lanes), lambda j: (0, 0, 0)),
                  pl.BlockSpec((d, ADA_TN), lambda j: (0, j)),
                  pl.BlockSpec((1, ADA_TN), lambda j: (0, j))],
        out_specs=pl.BlockSpec((ADA_ROWS, ADA_TN), lambda j: (0, j)),
        out_shape=jax.ShapeDtypeStruct((ADA_ROWS, n), F32),
        compiler_params=_cparams(("arbitrary",)),
        name="ada",
    )(cond_lanes, w_ada, b_ada)


INPROJ_TM = 2048
INPROJ_TN = 512
NORM_ROWS = 128


def _inproj_kernel(x_hbm, nw_ref, sc_ref, sh_ref, w_ref, o_ref, x_buf, h_scr, sem):
    i = pl.program_id(0)
    j = pl.program_id(1)
    tm = x_buf.shape[0]
    fetch = lambda tile: pltpu.make_async_copy(
        x_hbm.at[pl.ds(pl.multiple_of(tile * tm, tm), tm), :], x_buf, sem)

    @pl.when(j == 0)
    def _():
        @pl.when(i == 0)
        def _():
            fetch(0).start()

        fetch(i).wait()
        nw = nw_ref[...]
        sc = 1.0 + sc_ref[0]
        sh = sh_ref[0]

        def body(r, carry):
            rows = pl.ds(pl.multiple_of(r * NORM_ROWS, NORM_ROWS), NORM_ROWS)
            h_scr[rows, :] = (_rms(x_buf[rows, :], nw) * sc + sh).astype(BF16)
            return carry

        lax.fori_loop(0, tm // NORM_ROWS, body, 0)

    @pl.when((j == 1) & (i + 1 < pl.num_programs(0)))
    def _():
        fetch(i + 1).start()

    o_ref[...] = _dot(h_scr[...], w_ref[...].astype(BF16)).astype(o_ref.dtype)


def _inproj_call(x, nw, sc, sh, w_in, rows_per_group):
    m, d = x.shape
    n = w_in.shape[1]
    assert n // INPROJ_TN >= 2, "the next token tile is requested in the second column step"
    tm = min(INPROJ_TM, rows_per_group)
    tiles_per_group = rows_per_group // tm
    grp = lambda i, j: (i // tiles_per_group, 0, 0)
    return pl.pallas_call(
        _inproj_kernel,
        grid=(m // tm, n // INPROJ_TN),
        in_specs=[pl.BlockSpec(memory_space=pl.ANY),
                  pl.BlockSpec((1, d), lambda i, j: (0, 0)),
                  pl.BlockSpec((1, 1, d), grp),
                  pl.BlockSpec((1, 1, d), grp),
                  pl.BlockSpec((d, INPROJ_TN), lambda i, j: (0, j))],
        out_specs=pl.BlockSpec((tm, INPROJ_TN), lambda i, j: (i, j)),
        out_shape=jax.ShapeDtypeStruct((m, n), BF16),
        scratch_shapes=[pltpu.VMEM((tm, d), F32), pltpu.VMEM((tm, d), BF16), pltpu.SemaphoreType.DMA],
        compiler_params=_cparams(("arbitrary", "arbitrary")),
        name="inproj",
    )(x, nw, sc, sh, w_in)


def _with_ones(v):
    return jnp.concatenate([v.astype(BF16), jnp.ones(v.shape, BF16)], axis=1)


def _softmax_pv(scores, values_with_ones, sink=None):
    m = None
    for s in scores:
        ms = jnp.max(s, axis=-1, keepdims=True)
        m = ms if m is None else jnp.maximum(m, ms)
    if sink is not None:
        m = jnp.maximum(m, sink)
    acc = None
    for s, v1 in zip(scores, values_with_ones):
        pv = _dot(jnp.exp(s - m).astype(BF16), v1)
        acc = pv if acc is None else acc + pv
    d = acc.shape[1] // 2
    den = acc[:, d:]
    if sink is not None:
        den = den + jnp.exp(sink - m)
    return acc[:, :d] / den


def _ctx_attn_kernel(qa_ref, ka_ref, va_ref, qb_ref, kb_ref, vb_ref,
                     qna_ref, kna_ref, qnb_ref, knb_ref, sink_ref,
                     oa_ref, ob_ref, nak_ref, nav_ref, nbk_ref, nbv_ref):
    qna, kna, qnb, knb = qna_ref[...], kna_ref[...], qnb_ref[...], knb_ref[...]
    seq = qa_ref.shape[0]
    for h in range(NA_HEADS):
        cols = slice(h * HEAD_DIM, (h + 1) * HEAD_DIM)
        q = (_rms(qa_ref[:, cols], qna) * ATTN_SCALE).astype(BF16)
        k = _rms(ka_ref[:, cols], kna)
        v = va_ref[:, cols]
        nak_ref[pl.ds(h, seq, stride=NA_HEADS), :] = k
        nav_ref[pl.ds(h, seq, stride=NA_HEADS), :] = v.astype(F32)
        s = _dot_nt(q, k.astype(BF16))
        oa_ref[:, cols] = _softmax_pv([s], [_with_ones(v)]).astype(oa_ref.dtype)
    for kv in range(NB_KV_HEADS):
        kcols = slice(kv * HEAD_DIM, (kv + 1) * HEAD_DIM)
        k = _rms(kb_ref[:, kcols], knb)
        v = vb_ref[:, kcols]
        nbk_ref[pl.ds(kv, seq, stride=NB_KV_HEADS), :] = k
        nbv_ref[pl.ds(kv, seq, stride=NB_KV_HEADS), :] = v.astype(F32)
        kb16 = k.astype(BF16)
        vb1 = _with_ones(v)
        for g in range(NB_GROUP):
            hq = kv * NB_GROUP + g
            cols = slice(hq * HEAD_DIM, (hq + 1) * HEAD_DIM)
            q = (_rms(qb_ref[:, cols], qnb) * ATTN_SCALE).astype(BF16)
            s = _dot_nt(q, kb16)
            ob_ref[:, cols] = _softmax_pv([s], [vb1], sink=sink_ref[hq]).astype(ob_ref.dtype)


def _ctx_attn_call(proj, seq, qna, kna, qnb, knb, sink):
    m = proj.shape[0]
    nb = m // seq
    wide = lambda blk: pl.BlockSpec((seq, NA_WIDTH), lambda b: (b, blk))
    narrow = lambda blk: pl.BlockSpec((seq, NB_KV_WIDTH), lambda b: (b, blk))
    vec = pl.BlockSpec((1, HEAD_DIM), lambda b: (0, 0))
    return pl.pallas_call(
        _ctx_attn_kernel,
        grid=(nb,),
        in_specs=[wide(QA_HEAD0 // NA_HEADS), wide(KA_HEAD0 // NA_HEADS), wide(VA_HEAD0 // NA_HEADS),
                  wide(QB_HEAD0 // NA_HEADS), narrow(KB_HEAD0 // NB_KV_HEADS), narrow(VB_HEAD0 // NB_KV_HEADS),
                  vec, vec, vec, vec,
                  pl.BlockSpec(memory_space=pltpu.SMEM)],
        out_specs=[pl.BlockSpec((seq, NA_WIDTH), lambda b: (b, 0)),
                   pl.BlockSpec((seq, NB_WIDTH), lambda b: (b, 0)),
                   pl.BlockSpec((seq * NA_HEADS, HEAD_DIM), lambda b: (b, 0)),
                   pl.BlockSpec((seq * NA_HEADS, HEAD_DIM), lambda b: (b, 0)),
                   pl.BlockSpec((seq * NB_KV_HEADS, HEAD_DIM), lambda b: (b, 0)),
                   pl.BlockSpec((seq * NB_KV_HEADS, HEAD_DIM), lambda b: (b, 0))],
        out_shape=[jax.ShapeDtypeStruct((m, NA_WIDTH), BF16),
                   jax.ShapeDtypeStruct((m, NB_WIDTH), BF16),
                   jax.ShapeDtypeStruct((m * NA_HEADS, HEAD_DIM), F32),
                   jax.ShapeDtypeStruct((m * NA_HEADS, HEAD_DIM), F32),
                   jax.ShapeDtypeStruct((m * NB_KV_HEADS, HEAD_DIM), F32),
                   jax.ShapeDtypeStruct((m * NB_KV_HEADS, HEAD_DIM), F32)],
        compiler_params=_cparams(("arbitrary",)),
        name="ctx_attn",
    )(proj, proj, proj, proj, proj, proj, qna, kna, qnb, knb, sink)


def _rope(x, cos, sin_a, sin_b):
    quarter = HEAD_DIM // 4
    return (x * cos + pltpu.roll(x, HEAD_DIM - quarter, 1) * sin_a
            + pltpu.roll(x, quarter, 1) * sin_b)


def _head_rows(cache_ref, head, n_heads):
    past = cache_ref.shape[0] // n_heads
    return cache_ref[pl.ds(head, past, stride=n_heads), :]


def _lat_attn_a_kernel(q_ref, k_ref, v_ref, ck_ref, cv_ref, cb_ref, qn_ref, kn_ref, o_ref, *, pair0, index):
    head = pl.program_id(0)
    q = (_rms(q_ref[...], qn_ref[...]) * ATTN_SCALE).astype(BF16)
    k = _rms(k_ref[...], kn_ref[...]).astype(BF16)
    v1 = _with_ones(v_ref[...])
    ck = _head_rows(ck_ref, head, NA_HEADS).astype(BF16)
    cv1 = _with_ones(_head_rows(cv_ref, head, NA_HEADS))
    pair = 2 * GRID_W
    rows_per_block = pair // GRID_W
    n_pairs = len(index[0])
    for blk in range(len(pair0) // rows_per_block):
        grid_rows = range(blk * rows_per_block, (blk + 1) * rows_per_block)
        p0 = pair0[grid_rows[0]]
        assert all(pair0[r] == p0 for r in grid_rows), "query rows of a block share their key pairs"
        rows = slice(blk * pair, (blk + 1) * pair)
        keys = slice(p0 * pair, (p0 + n_pairs) * pair)
        bias = jnp.concatenate(
            [jnp.concatenate([cb_ref[u] for u in index[r]], axis=1) for r in grid_rows], axis=0)
        qs = q[rows]
        s_win = _dot_nt(qs, k[keys]) + bias
        s_ctx = _dot_nt(qs, ck)
        o_ref[rows, :] = _softmax_pv([s_win, s_ctx], [v1[keys], cv1]).astype(o_ref.dtype)


def _lat_attn_b_kernel(q_ref, k_ref, v_ref, ck_ref, cv_ref, cos_ref, sina_ref, sinb_ref,
                       qn_ref, kn_ref, sink_ref, o_ref):
    cos, sin_a, sin_b = cos_ref[...], sina_ref[...], sinb_ref[...]
    q = _rope(_rms(q_ref[...], qn_ref[...]), cos, sin_a, sin_b)
    k = _rope(_rms(k_ref[...], kn_ref[...]), cos, sin_a, sin_b)
    q = (q * ATTN_SCALE).astype(BF16)
    k = k.astype(BF16)
    v1 = _with_ones(v_ref[...])
    length = q.shape[0]
    kv = pl.program_id(0) // NB_GROUP
    ck = _head_rows(ck_ref, kv, NB_KV_HEADS).astype(BF16)
    cv1 = _with_ones(_head_rows(cv_ref, kv, NB_KV_HEADS))
    sink = sink_ref[pl.program_id(0)]
    for qb in range(length // WINDOW):
        rows = slice(qb * WINDOW, (qb + 1) * WINDOW)
        lo, hi = max(0, (qb - 1) * WINDOW), min(length, (qb + 2) * WINDOW)
        qs = q[rows]
        s_win = _dot_nt(qs, k[lo:hi])
        qi = qb * WINDOW + lax.broadcasted_iota(jnp.int32, s_win.shape, 0)
        kj = lo + lax.broadcasted_iota(jnp.int32, s_win.shape, 1)
        s_win = jnp.where(jnp.abs(qi - kj) <= WINDOW, s_win, NEG_INF)
        s_ctx = _dot_nt(qs, ck)
        o_ref[rows, :] = _softmax_pv([s_win, s_ctx], [v1[lo:hi], cv1], sink=sink).astype(o_ref.dtype)


def _lat_attn_a_call(proj, length, past, ck, cv, cb, pair0, index, qn, kn):
    m = proj.shape[0]
    head = lambda h0: pl.BlockSpec((length, HEAD_DIM), lambda h, b: (b, h0 + h))
    cache = pl.BlockSpec((past * NA_HEADS, HEAD_DIM), lambda h, b: (b, 0))
    vec = pl.BlockSpec((1, HEAD_DIM), lambda h, b: (0, 0))
    return pl.pallas_call(
        functools.partial(_lat_attn_a_kernel, pair0=pair0, index=index),
        grid=(NA_HEADS, m // length),
        in_specs=[head(QA_HEAD0), head(KA_HEAD0), head(VA_HEAD0), cache, cache,
                  pl.BlockSpec((None,) + cb.shape[1:], lambda h, b: (h, 0, 0, 0)), vec, vec],
        out_specs=pl.BlockSpec((length, HEAD_DIM), lambda h, b: (b, h)),
        out_shape=jax.ShapeDtypeStruct((m, NA_WIDTH), BF16),
        compiler_params=_cparams(("arbitrary", "arbitrary")),
        name="lat_attn_a",
    )(proj, proj, proj, ck, cv, cb, qn, kn)


def _lat_attn_b_call(proj, length, past, ck, cv, cos, sin_a, sin_b, qn, kn, sink):
    m = proj.shape[0]
    qspec = pl.BlockSpec((length, HEAD_DIM), lambda h, b: (b, QB_HEAD0 + h))
    kvspec = lambda h0: pl.BlockSpec((length, HEAD_DIM), lambda h, b: (b, h0 + h // NB_GROUP))
    cache = pl.BlockSpec((past * NB_KV_HEADS, HEAD_DIM), lambda h, b: (b, 0))
    table = pl.BlockSpec((length, HEAD_DIM), lambda h, b: (0, 0))
    vec = pl.BlockSpec((1, HEAD_DIM), lambda h, b: (0, 0))
    return pl.pallas_call(
        _lat_attn_b_kernel,
        grid=(NB_Q_HEADS, m // length),
        in_specs=[qspec, kvspec(KB_HEAD0), kvspec(VB_HEAD0), cache, cache,
                  table, table, table, vec, vec, pl.BlockSpec(memory_space=pltpu.SMEM)],
        out_specs=pl.BlockSpec((length, HEAD_DIM), lambda h, b: (b, h)),
        out_shape=jax.ShapeDtypeStruct((m, NB_WIDTH), BF16),
        compiler_params=_cparams(("arbitrary", "arbitrary")),
        name="lat_attn_b",
    )(proj, proj, proj, ck, cv, cos, sin_a, sin_b, qn, kn, sink)


def _na_bias_blocks(rpb, length):
    rows = length // GRID_W
    kr_n = min(NA_WIN_ROWS, rows)
    n_pairs = min(kr_n // 2 + 1, rows // 2)
    r = np.arange(rows)
    c = np.arange(GRID_W)
    r0 = np.clip(r - kr_n // 2, 0, rows - kr_n)
    c0 = np.clip(c - NA_WIN_COLS // 2, 0, GRID_W - NA_WIN_COLS)
    pair0 = np.minimum(r0 // 2, rows // 2 - n_pairs)
    kr = 2 * (pair0[:, None, None] + np.arange(n_pairs)[None, :, None]) + np.arange(2)[None, None, :]
    row_ok = (kr >= r0[:, None, None]) & (kr < r0[:, None, None] + kr_n)
    col_ok = (c[None, :] >= c0[:, None]) & (c[None, :] < c0[:, None] + NA_WIN_COLS)
    dr = kr - r[:, None, None] + (NA_WIN_ROWS - 1)
    dc = np.clip(c[None, :] - c[:, None], -(NA_WIN_COLS - 1), NA_WIN_COLS - 1) + (NA_WIN_COLS - 1)
    offs = np.where(row_ok, dr, -1).reshape(-1, 2)
    uniq, inverse = np.unique(offs, axis=0, return_inverse=True)
    index = inverse.reshape(rows, n_pairs)
    row_sel = (uniq[:, :, None] == np.arange(2 * NA_WIN_ROWS - 1)[None, None, :]).astype(np.float32)
    col_hit = (dc[None] == np.arange(2 * NA_WIN_COLS - 1)[:, None, None]) & col_ok[None]
    col_sel = np.zeros((2,) + col_hit.shape[:2] + (2 * GRID_W,), np.float32)
    for half in range(2):
        col_sel[half, :, :, half * GRID_W:(half + 1) * GRID_W] = col_hit
    hi = lax.Precision.HIGHEST
    per_col = jnp.einsum("hde,lecn->hldcn", rpb.astype(F32), col_sel, precision=hi)
    table = jnp.einsum("uld,hldcn->hucn", row_sel, per_col, precision=hi)
    valid = ((uniq >= 0)[:, None, :, None] & col_ok[None, :, None, :]).reshape(len(uniq), GRID_W, 2 * GRID_W)
    return (jnp.where(valid[None], table, NEG_INF), tuple(int(p) for p in pair0),
            tuple(tuple(int(u) for u in row) for row in index))


def _rope_tables(length):
    t = jnp.arange(length)
    row = (t // GRID_W).astype(F32)
    col = (t % GRID_W).astype(F32)
    n_freq = HEAD_DIM // 4
    inv = ROPE_BASE ** (-jnp.arange(n_freq, dtype=F32) / n_freq)
    ar = row[:, None] * inv
    ac = col[:, None] * inv
    ang = jnp.concatenate([ar, ar, ac, ac], axis=-1)
    cos, sin = jnp.cos(ang), jnp.sin(ang)
    lane = jnp.arange(HEAD_DIM)
    takes_left = ((lane // n_freq) % 2 == 0)[None, :]
    return cos, jnp.where(takes_left, -sin, 0.0), jnp.where(takes_left, 0.0, sin)


MERGE_TM = 512


def _resident(shape):
    zeros = (0,) * len(shape)
    return pl.BlockSpec(shape, lambda i: zeros, pipeline_mode=pl.Buffered(1))


def _mix_kernel(oa_ref, ob_ref, *refs):
    n_blk = (len(refs) - 3) // 2
    wpa_ref, wpb_ref, mix_ref = refs[-3:]
    gates = lambda blocks: jnp.concatenate([r[...] for r in blocks], axis=1).astype(F32)
    ya = _dot(oa_ref[...], wpa_ref[...].astype(BF16))
    yb = _dot(ob_ref[...], wpb_ref[...].astype(BF16))
    mix = jax.nn.sigmoid(gates(refs[:n_blk])) * ya + jax.nn.sigmoid(gates(refs[n_blk:2 * n_blk])) * yb
    mix_ref[...] = mix.astype(mix_ref.dtype)


def _mix_call(oa, ob, proj, w_pa, w_pb):
    m = oa.shape[0]
    d = w_pa.shape[1]
    row = lambda i: (i, 0)
    gate0, n_blk = GATE_COL0 // INPROJ_TN, d // INPROJ_TN
    gate_specs = [pl.BlockSpec((MERGE_TM, INPROJ_TN), lambda i, q=q: (i, gate0 + q)) for q in range(2 * n_blk)]
    return pl.pallas_call(
        _mix_kernel,
        grid=(m // MERGE_TM,),
        in_specs=[pl.BlockSpec((MERGE_TM, NA_WIDTH), row),
                  pl.BlockSpec((MERGE_TM, NB_WIDTH), row),
                  *gate_specs,
                  _resident((NA_WIDTH, d)), _resident((NB_WIDTH, d))],
        out_specs=pl.BlockSpec((MERGE_TM, d), row),
        out_shape=jax.ShapeDtypeStruct((m, d), BF16),
        compiler_params=_cparams(("arbitrary",)),
        name="mix",
    )(oa, ob, *([proj] * (2 * n_blk)), w_pa, w_pb)


def _outproj_kernel(mix_ref, wout_ref, x_ref, g1_ref, sc2_ref, sh2_ref, n2w_ref, wr_ref,
                    x1_ref, h2_ref, lg_ref, acc_ref):
    i = pl.program_id(0)
    last = pl.num_programs(0) - 1

    def finish(prod):
        x1 = x_ref[...] + g1_ref[0] * prod
        x1_ref[...] = x1
        h2 = _rms(x1, n2w_ref[...]) * (1.0 + sc2_ref[0]) + sh2_ref[0]
        h2_ref[...] = _rows_to_tiles(_pack_halves(h2))
        lg_ref[...] = _dot_split(h2, wr_ref[...])

    @pl.when(i == 0)
    def _():
        acc_ref[...] = _dot(mix_ref[...], wout_ref[...].astype(BF16))

    @pl.when((i > 0) & (i < last))
    def _():
        prod = acc_ref[...]
        acc_ref[...] = _dot(mix_ref[...], wout_ref[...].astype(BF16))
        finish(prod)

    @pl.when(i == last)
    def _():
        finish(acc_ref[...])


def _outproj_call(mix, w_out, x, g1, sc2, sh2, n2w, wr, tiles_per_group):
    m, d = x.shape
    n_tiles = m // MERGE_TM
    ahead = lambda i: (jnp.minimum(i, n_tiles - 1), 0)
    done = lambda i: (jnp.maximum(i - 1, 0), 0)
    grp = lambda i: (jnp.maximum(i - 1, 0) // tiles_per_group, 0, 0)
    return pl.pallas_call(
        _outproj_kernel,
        grid=(n_tiles + 1,),
        in_specs=[pl.BlockSpec((MERGE_TM, d), ahead),
                  _resident((d, d)),
                  pl.BlockSpec((MERGE_TM, d), done),
                  pl.BlockSpec((1, 1, d), grp), pl.BlockSpec((1, 1, d), grp), pl.BlockSpec((1, 1, d), grp),
                  pl.BlockSpec((1, d), lambda i: (0, 0)),
                  _resident((d, LOGIT_PAD))],
        out_specs=[pl.BlockSpec((MERGE_TM, d), done),
                   pl.BlockSpec((MERGE_TM,) + ROW_TILE, lambda i: (jnp.maximum(i - 1, 0), 0, 0)),
                   pl.BlockSpec((MERGE_TM, LOGIT_PAD), done)],
        out_shape=[jax.ShapeDtypeStruct((m, d), F32),
                   jax.ShapeDtypeStruct((m,) + ROW_TILE, jnp.uint32),
                   jax.ShapeDtypeStruct((m, LOGIT_PAD), F32)],
        scratch_shapes=[pltpu.VMEM((MERGE_TM, d), F32)],
        compiler_params=_cparams(("arbitrary",)),
        name="outproj",
    )(mix, w_out, x, g1, sc2, sh2, n2w, wr)


ROUTE_TM = 512


def _first_index_of_max(vals, idx, n):
    mx = jnp.max(vals, axis=0, keepdims=True)
    first = jnp.min(jnp.where(vals == mx, idx, n), axis=0, keepdims=True)
    return mx, first


def _route_kernel(lg_ctx_ref, lg_lat_ref, bias_ref, eid_ref, gw_ref, rank_ref, cnt_ref, base_ref, *, ctx_tiles):
    step = pl.program_id(0)

    @pl.when(step == 0)
    def _():
        base_ref[...] = jnp.zeros_like(base_ref)

    lg = jnp.where(step < ctx_tiles, lg_ctx_ref[...], lg_lat_ref[...])
    lt = lg.T + bias_ref[...]
    n_tok = lt.shape[1]
    le = lt[0:N_EXPERTS]
    lgrp = lt[N_EXPERTS:N_EXPERTS + N_GROUPS]
    gi = lax.broadcasted_iota(jnp.int32, (N_GROUPS, n_tok), 0)
    gmax, gsel = _first_index_of_max(lgrp, gi, N_GROUPS)
    pg_sel = 1.0 / jnp.sum(jnp.exp(lgrp - gmax), axis=0, keepdims=True)
    le_sel = jnp.zeros((EXPERTS_PER_GROUP, n_tok), F32)
    for g in range(N_GROUPS):
        le_sel = jnp.where(gsel == g, le[g * EXPERTS_PER_GROUP:(g + 1) * EXPERTS_PER_GROUP], le_sel)
    ei = lax.broadcasted_iota(jnp.int32, (EXPERTS_PER_GROUP, n_tok), 0)
    v0, i0 = _first_index_of_max(le_sel, ei, EXPERTS_PER_GROUP)
    rest = jnp.where(ei == i0, -jnp.inf, le_sel)
    v1, i1 = _first_index_of_max(rest, ei, EXPERTS_PER_GROUP)
    e1 = jnp.exp(v1 - v0)
    w0 = pg_sel / (1.0 + e1)
    w1 = pg_sel * e1 / (1.0 + e1)
    eid0 = gsel * EXPERTS_PER_GROUP + i0
    eid1 = gsel * EXPERTS_PER_GROUP + i1

    xi = lax.broadcasted_iota(jnp.int32, (N_EXPERTS, n_tok), 0)
    si = lax.broadcasted_iota(jnp.int32, (n_tok, n_tok), 0)
    ti = lax.broadcasted_iota(jnp.int32, (n_tok, n_tok), 1)
    before = (si < ti).astype(BF16)
    base = base_ref[...]
    hot0 = (xi == eid0).astype(F32)
    hot1 = (xi == eid1).astype(F32)
    pre0 = _dot(hot0.astype(BF16), before)
    pre1 = _dot(hot1.astype(BF16), before)
    tot0 = jnp.sum(hot0, axis=1, keepdims=True)
    tot1 = jnp.sum(hot1, axis=1, keepdims=True)
    rank0 = jnp.sum(hot0 * (base + pre0), axis=0, keepdims=True)
    rank1 = jnp.sum(hot1 * (base + tot0 + pre1), axis=0, keepdims=True)
    base = base + tot0 + tot1
    base_ref[...] = base

    ri = lax.broadcasted_iota(jnp.int32, (8, n_tok), 0)
    pick = lambda a, b: jnp.where(ri == 0, a, jnp.where(ri == 1, b, jnp.zeros_like(a)))
    eid_ref[...] = pick(eid0, eid1)
    gw_ref[...] = pick(w0, w1)
    rank_ref[...] = pick(rank0, rank1).astype(jnp.int32)
    cnt_ref[...] = jnp.broadcast_to(base, cnt_ref.shape).astype(jnp.int32)


def _route_call(logits_ctx, logits_lat, bias_col):
    ctx_tiles = logits_ctx.shape[0] // ROUTE_TM
    t = logits_ctx.shape[0] + logits_lat.shape[0]
    tok = pl.BlockSpec((8, ROUTE_TM), lambda i: (0, i))
    return pl.pallas_call(
        functools.partial(_route_kernel, ctx_tiles=ctx_tiles),
        grid=(t // ROUTE_TM,),
        in_specs=[pl.BlockSpec((ROUTE_TM, LOGIT_PAD), lambda i: (jnp.minimum(i, ctx_tiles - 1), 0)),
                  pl.BlockSpec((ROUTE_TM, LOGIT_PAD), lambda i: (jnp.maximum(i - ctx_tiles, 0), 0)),
                  pl.BlockSpec((LOGIT_PAD, 1), lambda i: (0, 0))],
        out_specs=[tok, tok, tok, pl.BlockSpec((N_EXPERTS, 128), lambda i: (0, 0))],
        out_shape=[jax.ShapeDtypeStruct((8, t), jnp.int32),
                   jax.ShapeDtypeStruct((8, t), F32),
                   jax.ShapeDtypeStruct((8, t), jnp.int32),
                   jax.ShapeDtypeStruct((N_EXPERTS, 128), jnp.int32)],
        scratch_shapes=[pltpu.VMEM((N_EXPERTS, 1), F32)],
        compiler_params=_cparams(("arbitrary",)),
        name="route",
    )(logits_ctx, logits_lat, bias_col)


def _moe_layout(n_pairs):
    padded_rows = -(-(n_pairs + N_EXPERTS * (MOE_ROW_BLOCK - 1)) // MOE_ROW_BLOCK) * MOE_ROW_BLOCK
    n_items = (padded_rows + N_EXPERTS * (MOE_ROW_GROUP - MOE_ROW_BLOCK)) // MOE_ROW_GROUP
    return padded_rows, n_items


def _routing_tables(eid, rank, counts, n_items):
    padded = (counts + MOE_ROW_BLOCK - 1) // MOE_ROW_BLOCK * MOE_ROW_BLOCK
    pad_end = jnp.cumsum(padded)
    pad_start = pad_end - padded
    hot = eid[..., None] == jnp.arange(N_EXPERTS, dtype=jnp.int32)
    dest = (jnp.sum(jnp.where(hot, pad_start, 0), axis=-1) + rank).astype(jnp.int32)
    tail = jnp.where(padded > counts, pad_end - MOE_ROW_BLOCK, -1).astype(jnp.int32)
    per_expert = (padded + MOE_ROW_GROUP - 1) // MOE_ROW_GROUP
    item_end = jnp.cumsum(per_expert)
    item_start = item_end - per_expert
    total = item_end[-1]
    ii = jnp.arange(n_items, dtype=jnp.int32)
    owner = lambda v: jnp.minimum(jnp.sum(item_end[None, :] <= v[:, None], axis=1), N_EXPERTS - 1).astype(jnp.int32)
    e_of = owner(ii)
    sel = e_of[:, None] == jnp.arange(N_EXPERTS, dtype=jnp.int32)
    pick = lambda v: jnp.sum(jnp.where(sel, v[None, :], 0), axis=1)
    valid = ii < total
    e_last = owner(jnp.maximum(total - 1, 0)[None])[0]
    local = ii - pick(item_start)
    row0 = pick(pad_start) + local * MOE_ROW_GROUP
    nblk = jnp.clip((pick(padded) - local * MOE_ROW_GROUP) // MOE_ROW_BLOCK, 0, MOE_ROW_GROUP // MOE_ROW_BLOCK)
    item_e = jnp.where(valid, e_of, e_last).astype(jnp.int32)
    item_row0 = jnp.where(valid, row0, 0).astype(jnp.int32)
    item_nblk = jnp.where(valid, nblk, 0).astype(jnp.int32)
    return dest, tail, item_e, item_row0, item_nblk


DISPATCH_TOKENS = 256
ROW_DMA_GROUP = 8


def _row_copy(src, s, dst, d, sem):
    return pltpu.make_async_copy(src.at[pl.ds(s, 1)], dst.at[pl.ds(d, 1)], sem)


def _dispatch_kernel(dest_ref, tail_ref, h_ctx, h_lat, xs, zero_buf, sem, *, n_ctx, n_tok):
    step = pl.program_id(0)
    tail_copy = lambda e: pltpu.make_async_copy(
        zero_buf, xs.at[pl.ds(pl.multiple_of(tail_ref[e], MOE_ROW_BLOCK), MOE_ROW_BLOCK)], sem.at[1])

    @pl.when(step == 0)
    def _():
        zero_buf[...] = jnp.zeros_like(zero_buf)
        for e in range(N_EXPERTS):
            @pl.when(tail_ref[e] >= 0)
            def _():
                tail_copy(e).start()
        for e in range(N_EXPERTS):
            @pl.when(tail_ref[e] >= 0)
            def _():
                tail_copy(e).wait()

    tok0 = step * DISPATCH_TOKENS

    def scatter(src):
        def issue(g, carry):
            base = pl.multiple_of(g * ROW_DMA_GROUP, ROW_DMA_GROUP)
            for j in range(ROW_DMA_GROUP):
                _row_copy(src, base + j, xs, dest_ref[tok0 + base + j], sem.at[0]).start(priority=0)
                _row_copy(src, base + j, xs, dest_ref[n_tok + tok0 + base + j], sem.at[0]).start(priority=1)
            return carry

        lax.fori_loop(0, DISPATCH_TOKENS // ROW_DMA_GROUP, issue, 0)
        for _ in range(2):
            pltpu.make_async_copy(src, xs.at[pl.ds(0, DISPATCH_TOKENS)], sem.at[0]).wait()

    @pl.when(tok0 < n_ctx)
    def _():
        scatter(h_ctx)

    @pl.when(tok0 >= n_ctx)
    def _():
        scatter(h_lat)


def _dispatch_call(dest_flat, tail, h_ctx, h_lat, padded_rows):
    n_ctx = h_ctx.shape[0]
    row = h_ctx.shape[1:]
    n_tok = n_ctx + h_lat.shape[0]
    ctx_tiles = n_ctx // DISPATCH_TOKENS
    return pl.pallas_call(
        functools.partial(_dispatch_kernel, n_ctx=n_ctx, n_tok=n_tok),
        grid_spec=pltpu.PrefetchScalarGridSpec(
            num_scalar_prefetch=2,
            grid=(n_tok // DISPATCH_TOKENS,),
            in_specs=[pl.BlockSpec((DISPATCH_TOKENS,) + row, lambda i, dr, tr: (jnp.minimum(i, ctx_tiles - 1), 0, 0)),
                      pl.BlockSpec((DISPATCH_TOKENS,) + row, lambda i, dr, tr: (jnp.maximum(i - ctx_tiles, 0), 0, 0))],
            out_specs=pl.BlockSpec(memory_space=pl.ANY),
            scratch_shapes=[pltpu.VMEM((MOE_ROW_BLOCK,) + row, h_ctx.dtype), pltpu.SemaphoreType.DMA((2,))]),
        out_shape=jax.ShapeDtypeStruct((padded_rows,) + row, h_ctx.dtype),
        compiler_params=_cparams(("arbitrary",)),
        name="dispatch",
    )(dest_flat, tail, h_ctx, h_lat)


def _moe_kernel(item_e, item_row0, item_nblk, xs, w1_ref, w3_ref, w2_ref, ys,
                x_in, x_bf, acc, y_out, w1_bf, w3_bf, w2_bf, sem, *, n_chunks):
    i = pl.program_id(0)
    c = pl.program_id(1)
    n_items = pl.num_programs(0)
    last_c = n_chunks - 1
    nblk = item_nblk[i]
    max_blk = MOE_ROW_GROUP // MOE_ROW_BLOCK
    blk = lambda b: pl.ds(b * MOE_ROW_BLOCK, MOE_ROW_BLOCK)

    def rows_of(item, b):
        return pl.ds(pl.multiple_of(item_row0[item], MOE_ROW_BLOCK) + b * MOE_ROW_BLOCK, MOE_ROW_BLOCK)

    load = lambda item, b: pltpu.make_async_copy(xs.at[rows_of(item, b)], x_in.at[blk(b)], sem.at[0])
    store = lambda item, b: pltpu.make_async_copy(y_out.at[blk(b)], ys.at[rows_of(item, b)], sem.at[1])

    def for_blocks(item, fn):
        n = item_nblk[item]
        for b in range(max_blk):
            @pl.when(b < n)
            def _():
                fn(item, b)

    @pl.when(c == 0)
    def _():
        @pl.when(i == 0)
        def _():
            for_blocks(0, lambda it, b: load(it, b).start())

        for_blocks(i, lambda it, b: load(it, b).wait())

    @pl.when((c == last_c) & (i > 0))
    def _():
        for_blocks(i - 1, lambda it, b: store(it, b).wait())

    def run_blocks(first, last):
        def block(rows, w1, w3, w2):
            if first:
                x = _unpack_halves(_tiles_to_rows(x_in[rows])).astype(BF16)
                x_bf[rows, :] = x
            else:
                x = x_bf[rows, :]
            h1 = _dot(x, w1)
            h3 = _dot(x, w3)
            a = (h1 * jax.nn.sigmoid(h1) * h3).astype(BF16)
            y = _dot(a, w2)
            if not first:
                y = acc[rows, :] + y
            if last:
                y_out[rows] = _rows_to_tiles(_pack_halves(y))
            else:
                acc[rows, :] = y

        w1 = w1_ref[0].astype(BF16)
        w3 = w3_ref[0].astype(BF16)
        w2 = w2_ref[0].astype(BF16)
        w1_bf[...] = w1
        w3_bf[...] = w3
        w2_bf[...] = w2
        block(blk(0), w1, w3, w2)

        def body(b, carry):
            rows = pl.ds(pl.multiple_of(b * MOE_ROW_BLOCK, MOE_ROW_BLOCK), MOE_ROW_BLOCK)
            block(rows, w1_bf[...], w3_bf[...], w2_bf[...])
            return carry

        lax.fori_loop(1, nblk, body, 0)

    for first, last in sorted({(cc == 0, cc == last_c) for cc in range(n_chunks)}):
        chunk_is = (c == 0) if first else ((c == last_c) if last else ((c > 0) & (c < last_c)))

        @pl.when((nblk > 0) & chunk_is)
        def _():
            run_blocks(first, last)

    @pl.when((c == 0) & (i + 1 < n_items))
    def _():
        for_blocks(i + 1, lambda it, b: load(it, b).start())

    @pl.when(c == last_c)
    def _():
        for_blocks(i, lambda it, b: store(it, b).start())

        @pl.when(i == n_items - 1)
        def _():
            for_blocks(i, lambda it, b: store(it, b).wait())


def _moe_call(item_e, item_row0, item_nblk, xs, w1, w3, w2):
    padded_rows, row = xs.shape[0], xs.shape[1:]
    d = w1.shape[1]
    n_items = item_e.shape[0]
    f = w1.shape[2]
    nc = f // MOE_F_CHUNK
    chunk = lambda i, c, ib: jnp.where(ib[i] > 0, c, nc - 1)
    return pl.pallas_call(
        functools.partial(_moe_kernel, n_chunks=nc),
        grid_spec=pltpu.PrefetchScalarGridSpec(
            num_scalar_prefetch=3,
            grid=(n_items, nc),
            in_specs=[pl.BlockSpec(memory_space=pl.ANY),
                      pl.BlockSpec((1, d, MOE_F_CHUNK), lambda i, c, ie, ir, ib: (ie[i], 0, chunk(i, c, ib))),
                      pl.BlockSpec((1, d, MOE_F_CHUNK), lambda i, c, ie, ir, ib: (ie[i], 0, chunk(i, c, ib))),
                      pl.BlockSpec((1, MOE_F_CHUNK, d), lambda i, c, ie, ir, ib: (ie[i], chunk(i, c, ib), 0))],
            out_specs=pl.BlockSpec(memory_space=pl.ANY),
            scratch_shapes=[pltpu.VMEM((MOE_ROW_GROUP,) + row, xs.dtype),
                            pltpu.VMEM((MOE_ROW_GROUP, d), BF16),
                            pltpu.VMEM((MOE_ROW_GROUP, d), F32),
                            pltpu.VMEM((MOE_ROW_GROUP,) + row, xs.dtype),
                            pltpu.VMEM((d, MOE_F_CHUNK), BF16),
                            pltpu.VMEM((d, MOE_F_CHUNK), BF16),
                            pltpu.VMEM((MOE_F_CHUNK, d), BF16),
                            pltpu.SemaphoreType.DMA((2,))]),
        out_shape=jax.ShapeDtypeStruct((padded_rows,) + row, xs.dtype),
        compiler_params=_cparams(("arbitrary", "arbitrary")),
        name="moe",
    )(item_e, item_row0, item_nblk, xs, w1, w3, w2)


COMBINE_TM = 256


def _combine_kernel(dest_ref, x1_ref, gw_ref, g2_ref, ys, o_ref, y0, y1, sem, *, tok_base, n_tok):
    i = pl.program_id(0)

    def gather(tile, slot):
        tok0 = tok_base + tile * COMBINE_TM

        def issue(g, carry):
            base = pl.multiple_of(g * ROW_DMA_GROUP, ROW_DMA_GROUP)
            for j in range(ROW_DMA_GROUP):
                _row_copy(ys, dest_ref[tok0 + base + j], y0.at[slot], base + j, sem.at[slot]).start(priority=0)
                _row_copy(ys, dest_ref[n_tok + tok0 + base + j], y1.at[slot], base + j, sem.at[slot]).start(priority=1)
            return carry

        lax.fori_loop(0, COMBINE_TM // ROW_DMA_GROUP, issue, 0)

    @pl.when(i == 0)
    def _():
        gather(0, 0)

    @pl.when(i + 1 < pl.num_programs(0))
    def _():
        gather(i + 1, (i + 1) % 2)

    slot = i % 2
    for buf in (y0, y1):
        pltpu.make_async_copy(ys.at[pl.ds(0, COMBINE_TM)], buf.at[slot], sem.at[slot]).wait()
    gw = gw_ref[...]
    moe = (gw[:, 0:1] * _unpack_halves(_tiles_to_rows(y0[slot]))
           + gw[:, 1:2] * _unpack_halves(_tiles_to_rows(y1[slot])))
    o_ref[...] = x1_ref[...] + g2_ref[0] * moe


def _combine_call(dest_flat, x1, gw, g2, ys, tok_base, n_tok, tiles_per_group):
    m, d = x1.shape
    return pl.pallas_call(
        functools.partial(_combine_kernel, tok_base=tok_base, n_tok=n_tok),
        grid_spec=pltpu.PrefetchScalarGridSpec(
            num_scalar_prefetch=1,
            grid=(m // COMBINE_TM,),
            in_specs=[pl.BlockSpec((COMBINE_TM, d), lambda i, dr: (i, 0)),
                      pl.BlockSpec((COMBINE_TM, 2), lambda i, dr: (i, 0)),
                      pl.BlockSpec((1, 1, d), lambda i, dr: (i // tiles_per_group, 0, 0)),
                      pl.BlockSpec(memory_space=pl.ANY)],
            out_specs=pl.BlockSpec((COMBINE_TM, d), lambda i, dr: (i, 0)),
            scratch_shapes=[pltpu.VMEM((2, COMBINE_TM) + ys.shape[1:], ys.dtype),
                            pltpu.VMEM((2, COMBINE_TM) + ys.shape[1:], ys.dtype),
                            pltpu.SemaphoreType.DMA((2,))]),
        out_shape=jax.ShapeDtypeStruct((m, d), F32),
        compiler_params=_cparams(("arbitrary",)),
        name="combine",
    )(dest_flat, x1, gw, g2, ys)


def kernel(x_prompt, x_sample, cache_a_k, cache_a_v, cache_b_k, cache_b_v, c, c_ctx, norm1_w, norm2_w, w_ada, b_ada, w_in, qn_a, kn_a, qn_b, kn_b, rpb_a, sink_b, w_pa, w_pb, w_out, w_rg, b_rg, w_re, b_re, w1, w3, w2):
    batch, seq, d = x_prompt.shape
    dec_batch, dec_seq, _ = x_sample.shape
    depth = norm1_w.shape[0]
    assert depth == 1, "one trunk layer"
    past = cache_a_k.shape[2]
    n_ctx, n_lat = batch * seq, dec_batch * dec_seq
    n_tok = n_ctx + n_lat

    xc = x_prompt.reshape(n_ctx, d)
    xl = x_sample.reshape(n_lat, d)

    cond = jnp.concatenate([c_ctx[None, :], c], axis=0)
    mod = _ada_call(cond, w_ada[0], b_ada[0][None, :])
    sh1, sc1, g1, sh2, sc2, g2 = [mod[:, i * d:(i + 1) * d][:, None, :] for i in range(6)]
    ctx_rows, lat_rows = slice(0, 1), slice(1, 1 + dec_batch)

    nw1, nw2 = norm1_w[0][None, :], norm2_w[0][None, :]
    qna, kna, qnb, knb = qn_a[0][None, :], kn_a[0][None, :], qn_b[0][None, :], kn_b[0][None, :]
    sink = sink_b[0]
    n_logits = N_EXPERTS + N_GROUPS
    wr = jnp.concatenate([w_re[0], w_rg[0], jnp.zeros((d, LOGIT_PAD - n_logits), F32)], axis=1)
    br = jnp.concatenate([b_re[0], b_rg[0], jnp.zeros((LOGIT_PAD - n_logits,), F32)])[:, None]

    proj_c = _inproj_call(xc, nw1, sc1[ctx_rows], sh1[ctx_rows], w_in[0], n_ctx)
    proj_l = _inproj_call(xl, nw1, sc1[lat_rows], sh1[lat_rows], w_in[0], dec_seq)

    oa_c, ob_c, new_a_k, new_a_v, new_b_k, new_b_v = _ctx_attn_call(proj_c, seq, qna, kna, qnb, knb, sink)

    bias_blocks, pair0, bias_index = _na_bias_blocks(rpb_a[0], dec_seq)
    cos, sin_a, sin_b = _rope_tables(dec_seq)
    rows_of = lambda cache: cache.reshape(-1, HEAD_DIM)
    oa_l = _lat_attn_a_call(proj_l, dec_seq, past, rows_of(cache_a_k), rows_of(cache_a_v),
                            bias_blocks, pair0, bias_index, qna, kna)
    ob_l = _lat_attn_b_call(proj_l, dec_seq, past, rows_of(cache_b_k), rows_of(cache_b_v),
                            cos, sin_a, sin_b, qnb, knb, sink)

    mix_c = _mix_call(oa_c, ob_c, proj_c, w_pa[0], w_pb[0])
    mix_l = _mix_call(oa_l, ob_l, proj_l, w_pa[0], w_pb[0])
    x1_c, h2_c, lg_c = _outproj_call(mix_c, w_out[0], xc, g1[ctx_rows], sc2[ctx_rows], sh2[ctx_rows],
                                     nw2, wr, n_ctx // MERGE_TM)
    x1_l, h2_l, lg_l = _outproj_call(mix_l, w_out[0], xl, g1[lat_rows], sc2[lat_rows], sh2[lat_rows],
                                     nw2, wr, dec_seq // MERGE_TM)

    eid, gw, rank, cnt = _route_call(lg_c, lg_l, br)
    padded_rows, n_items = _moe_layout(2 * n_tok)
    dest, tail, item_e, item_row0, item_nblk = _routing_tables(eid[:2], rank[:2], cnt[:, 0], n_items)
    dest_flat = dest.reshape(-1)
    xs = _dispatch_call(dest_flat, tail, h2_c, h2_l, padded_rows)
    ys = _moe_call(item_e, item_row0, item_nblk, xs, w1[0], w3[0], w2[0])
    gw_t = gw[:2].T
    y_c = _combine_call(dest_flat, x1_c, gw_t[:n_ctx], g2[ctx_rows], ys, 0, n_tok, n_ctx // COMBINE_TM)
    y_l = _combine_call(dest_flat, x1_l, gw_t[n_ctx:], g2[lat_rows], ys, n_ctx, n_tok, dec_seq // COMBINE_TM)

    state = lambda a, heads: a.reshape(batch, 1, seq, heads, HEAD_DIM)
    return (y_c.reshape(batch, seq, d), y_l.reshape(dec_batch, dec_seq, d),
            state(new_a_k, NA_HEADS), state(new_a_v, NA_HEADS),
            state(new_b_k, NB_KV_HEADS), state(new_b_v, NB_KV_HEADS))
```

```python
import functools

import jax
import jax.numpy as jnp
import numpy as np
from jax import lax
from jax.experimental import pallas as pl
from jax.experimental.pallas import tpu as pltpu

D_MODEL = 2048
HEAD_DIM = 128
NA_HEADS = 8
NA_WIDTH = NA_HEADS * HEAD_DIM
NB_Q_HEADS = 8
NB_KV_HEADS = 2
NB_GROUP = NB_Q_HEADS // NB_KV_HEADS
NB_WIDTH = NB_Q_HEADS * HEAD_DIM
NB_KV_WIDTH = NB_KV_HEADS * HEAD_DIM
GRID_W = 64
NA_WIN_ROWS = 8
NA_WIN_COLS = 16
WINDOW = 128
N_GROUPS = 4
EXPERTS_PER_GROUP = 8
N_EXPERTS = N_GROUPS * EXPERTS_PER_GROUP
D_EXPERT = 1024
IN_WIDTH = 3 * NA_WIDTH + NB_WIDTH + 2 * NB_KV_WIDTH + 2 * D_MODEL
ROPE_BASE = 10000.0
NORM_EPS = 1e-6
NEG_INF = -1e30
ATTN_SCALE = HEAD_DIM ** -0.5

QA_HEAD0 = 0
KA_HEAD0 = NA_HEADS
VA_HEAD0 = 2 * NA_HEADS
QB_HEAD0 = 3 * NA_HEADS
KB_HEAD0 = QB_HEAD0 + NB_Q_HEADS
VB_HEAD0 = KB_HEAD0 + NB_KV_HEADS
GATE_COL0 = (VB_HEAD0 + NB_KV_HEADS) * HEAD_DIM

LOGIT_PAD = 128
MOE_ROW_BLOCK = 256
MOE_ROW_GROUP = 1024
MOE_F_CHUNK = 512
VMEM_LIMIT = 56 * 1024 * 1024

F32 = jnp.float32
BF16 = jnp.bfloat16


def _cparams(sem, vmem_limit=VMEM_LIMIT):
    return pltpu.CompilerParams(dimension_semantics=sem, vmem_limit_bytes=vmem_limit)


def _rms(x, w):
    x = x.astype(F32)
    return x * lax.rsqrt(jnp.mean(x * x, axis=-1, keepdims=True) + NORM_EPS) * w


def _dot(a, b):
    return jnp.dot(a, b, preferred_element_type=F32)


def _dot_nt(a, b):
    return lax.dot_general(a, b, (((1,), (1,)), ((), ())), preferred_element_type=F32)


def _pack_halves(x):
    n = x.shape[1] // 2
    lo = lax.bitcast_convert_type(x[:, :n].astype(BF16).astype(F32), jnp.uint32)
    hi = lax.bitcast_convert_type(x[:, n:].astype(BF16).astype(F32), jnp.uint32)
    return hi | (lo >> 16)


def _unpack_halves(w):
    lo = lax.bitcast_convert_type(w << 16, F32)
    hi = lax.bitcast_convert_type(w & jnp.uint32(0xFFFF0000), F32)
    return jnp.concatenate([lo, hi], axis=1)


ROW_TILE = (8, 128)


def _rows_to_tiles(w):
    return w.reshape((w.shape[0],) + ROW_TILE)


def _tiles_to_rows(t):
    return t.reshape(t.shape[0], ROW_TILE[0] * ROW_TILE[1])


def _dot_split(a, b):
    a_hi = a.astype(BF16)
    a_lo = (a - a_hi.astype(F32)).astype(BF16)
    b_hi = b.astype(BF16)
    b_lo = (b - b_hi.astype(F32)).astype(BF16)
    return _dot(a_hi, b_hi) + (_dot(a_lo, b_hi) + _dot(a_hi, b_lo))


ADA_ROWS = 8
ADA_TN = 1024


def _ada_kernel(c_ref, w_ref, b_ref, o_ref):
    n_rows, d, lanes = c_ref.shape
    tn = w_ref.shape[1]

    def body(kb, acc):
        ks = pl.ds(pl.multiple_of(kb * 8, 8), 8)
        w = w_ref[ks, :]
        out = []
        for r in range(n_rows):
            c = c_ref[r, ks, :]
            s = c * jax.nn.sigmoid(c)
            out.append(acc[r] + w * jnp.concatenate([s] * (tn // lanes), axis=1))
        return tuple(out)

    acc = lax.fori_loop(0, d // 8, body, tuple(jnp.zeros((8, tn), F32) for _ in range(n_rows)), unroll=8)
    ri = lax.broadcasted_iota(jnp.int32, (ADA_ROWS, tn), 0)
    res = jnp.zeros((ADA_ROWS, tn), F32)
    for r in range(n_rows):
        row = jnp.sum(acc[r], axis=0, keepdims=True) + b_ref[...]
        res = jnp.where(ri == r, row, res)
    o_ref[...] = res


def _ada_call(cond, w_ada, b_ada):
    n_rows, d = cond.shape
    n = w_ada.shape[1]
    lanes = 128
    cond_lanes = jnp.broadcast_to(cond[:, :, None], (n_rows, d, lanes))
    return pl.pallas_call(
        _ada_kernel,
        grid=(n // ADA_TN,),
        in_specs=[pl.BlockSpec((n_rows, d, lanes), lambda j: (0, 0, 0)),
                  pl.BlockSpec((d, ADA_TN), lambda j: (0, j)),
                  pl.BlockSpec((1, ADA_TN), lambda j: (0, j))],
        out_specs=pl.BlockSpec((ADA_ROWS, ADA_TN), lambda j: (0, j)),
        out_shape=jax.ShapeDtypeStruct((ADA_ROWS, n), F32),
        compiler_params=_cparams(("arbitrary",)),
        name="ada",
    )(cond_lanes, w_ada, b_ada)


INPROJ_TM = 2048
INPROJ_TN = 512
NORM_ROWS = 128


def _inproj_kernel(x_hbm, nw_ref, sc_ref, sh_ref, w_ref, o_ref, x_buf, h_scr, sem):
    i = pl.program_id(0)
    j = pl.program_id(1)
    tm = x_buf.shape[0]
    fetch = lambda tile: pltpu.make_async_copy(
        x_hbm.at[pl.ds(pl.multiple_of(tile * tm, tm), tm), :], x_buf, sem)

    @pl.when(j == 0)
    def _():
        @pl.when(i == 0)
        def _():
            fetch(0).start()

        fetch(i).wait()
        nw = nw_ref[...]
        sc = 1.0 + sc_ref[0]
        sh = sh_ref[0]

        def body(r, carry):
            rows = pl.ds(pl.multiple_of(r * NORM_ROWS, NORM_ROWS), NORM_ROWS)
            h_scr[rows, :] = (_rms(x_buf[rows, :], nw) * sc + sh).astype(BF16)
            return carry

        lax.fori_loop(0, tm // NORM_ROWS, body, 0)

    @pl.when((j == 1) & (i + 1 < pl.num_programs(0)))
    def _():
        fetch(i + 1).start()

    o_ref[...] = _dot(h_scr[...], w_ref[...].astype(BF16)).astype(o_ref.dtype)


def _inproj_call(x, nw, sc, sh, w_in, rows_per_group):
    m, d = x.shape
    n = w_in.shape[1]
    assert n // INPROJ_TN >= 2, "the next token tile is requested in the second column step"
    tm = min(INPROJ_TM, rows_per_group)
    tiles_per_group = rows_per_group // tm
    grp = lambda i, j: (i // tiles_per_group, 0, 0)
    return pl.pallas_call(
        _inproj_kernel,
        grid=(m // tm, n // INPROJ_TN),
        in_specs=[pl.BlockSpec(memory_space=pl.ANY),
                  pl.BlockSpec((1, d), lambda i, j: (0, 0)),
                  pl.BlockSpec((1, 1, d), grp),
                  pl.BlockSpec((1, 1, d), grp),
                  pl.BlockSpec((d, INPROJ_TN), lambda i, j: (0, j))],
        out_specs=pl.BlockSpec((tm, INPROJ_TN), lambda i, j: (i, j)),
        out_shape=jax.ShapeDtypeStruct((m, n), BF16),
        scratch_shapes=[pltpu.VMEM((tm, d), F32), pltpu.VMEM((tm, d), BF16), pltpu.SemaphoreType.DMA],
        compiler_params=_cparams(("arbitrary", "arbitrary")),
        name="inproj",
    )(x, nw, sc, sh, w_in)


def _with_ones(v):
    return jnp.concatenate([v.astype(BF16), jnp.ones(v.shape, BF16)], axis=1)


def _softmax_pv(scores, values_with_ones, sink=None):
    m = None
    for s in scores:
        ms = jnp.max(s, axis=-1, keepdims=True)
        m = ms if m is None else jnp.maximum(m, ms)
    if sink is not None:
        m = jnp.maximum(m, sink)
    acc = None
    for s, v1 in zip(scores, values_with_ones):
        pv = _dot(jnp.exp(s - m).astype(BF16), v1)
        acc = pv if acc is None else acc + pv
    d = acc.shape[1] // 2
    den = acc[:, d:]
    if sink is not None:
        den = den + jnp.exp(sink - m)
    return acc[:, :d] / den


def _ctx_attn_kernel(qa_ref, ka_ref, va_ref, qb_ref, kb_ref, vb_ref,
                     qna_ref, kna_ref, qnb_ref, knb_ref, sink_ref,
                     oa_ref, ob_ref, nak_ref, nav_ref, nbk_ref, nbv_ref):
    qna, kna, qnb, knb = qna_ref[...], kna_ref[...], qnb_ref[...], knb_ref[...]
    seq = qa_ref.shape[0]
    for h in range(NA_HEADS):
        cols = slice(h * HEAD_DIM, (h + 1) * HEAD_DIM)
        q = (_rms(qa_ref[:, cols], qna) * ATTN_SCALE).astype(BF16)
        k = _rms(ka_ref[:, cols], kna)
        v = va_ref[:, cols]
        nak_ref[pl.ds(h, seq, stride=NA_HEADS), :] = k
        nav_ref[pl.ds(h, seq, stride=NA_HEADS), :] = v.astype(F32)
        s = _dot_nt(q, k.astype(BF16))
        oa_ref[:, cols] = _softmax_pv([s], [_with_ones(v)]).astype(oa_ref.dtype)
    for kv in range(NB_KV_HEADS):
        kcols = slice(kv * HEAD_DIM, (kv + 1) * HEAD_DIM)
        k = _rms(kb_ref[:, kcols], knb)
        v = vb_ref[:, kcols]
        nbk_ref[pl.ds(kv, seq, stride=NB_KV_HEADS), :] = k
        nbv_ref[pl.ds(kv, seq, stride=NB_KV_HEADS), :] = v.astype(F32)
        kb16 = k.astype(BF16)
        vb1 = _with_ones(v)
        for g in range(NB_GROUP):
            hq = kv * NB_GROUP + g
            cols = slice(hq * HEAD_DIM, (hq + 1) * HEAD_DIM)
            q = (_rms(qb_ref[:, cols], qnb) * ATTN_SCALE).astype(BF16)
            s = _dot_nt(q, kb16)
            ob_ref[:, cols] = _softmax_pv([s], [vb1], sink=sink_ref[hq]).astype(ob_ref.dtype)


def _ctx_attn_call(proj, seq, qna, kna, qnb, knb, sink):
    m = proj.shape[0]
    nb = m // seq
    wide = lambda blk: pl.BlockSpec((seq, NA_WIDTH), lambda b: (b, blk))
    narrow = lambda blk: pl.BlockSpec((seq, NB_KV_WIDTH), lambda b: (b, blk))
    vec = pl.BlockSpec((1, HEAD_DIM), lambda b: (0, 0))
    return pl.pallas_call(
        _ctx_attn_kernel,
        grid=(nb,),
        in_specs=[wide(QA_HEAD0 // NA_HEADS), wide(KA_HEAD0 // NA_HEADS), wide(VA_HEAD0 // NA_HEADS),
                  wide(QB_HEAD0 // NA_HEADS), narrow(KB_HEAD0 // NB_KV_HEADS), narrow(VB_HEAD0 // NB_KV_HEADS),
                  vec, vec, vec, vec,
                  pl.BlockSpec(memory_space=pltpu.SMEM)],
        out_specs=[pl.BlockSpec((seq, NA_WIDTH), lambda b: (b, 0)),
                   pl.BlockSpec((seq, NB_WIDTH), lambda b: (b, 0)),
                   pl.BlockSpec((seq * NA_HEADS, HEAD_DIM), lambda b: (b, 0)),
                   pl.BlockSpec((seq * NA_HEADS, HEAD_DIM), lambda b: (b, 0)),
                   pl.BlockSpec((seq * NB_KV_HEADS, HEAD_DIM), lambda b: (b, 0)),
                   pl.BlockSpec((seq * NB_KV_HEADS, HEAD_DIM), lambda b: (b, 0))],
        out_shape=[jax.ShapeDtypeStruct((m, NA_WIDTH), BF16),
                   jax.ShapeDtypeStruct((m, NB_WIDTH), BF16),
                   jax.ShapeDtypeStruct((m * NA_HEADS, HEAD_DIM), F32),
                   jax.ShapeDtypeStruct((m * NA_HEADS, HEAD_DIM), F32),
                   jax.ShapeDtypeStruct((m * NB_KV_HEADS, HEAD_DIM), F32),
                   jax.ShapeDtypeStruct((m * NB_KV_HEADS, HEAD_DIM), F32)],
        compiler_params=_cparams(("arbitrary",)),
        name="ctx_attn",
    )(proj, proj, proj, proj, proj, proj, qna, kna, qnb, knb, sink)


def _rope(x, cos, sin_a, sin_b):
    quarter = HEAD_DIM // 4
    return (x * cos + pltpu.roll(x, HEAD_DIM - quarter, 1) * sin_a
            + pltpu.roll(x, quarter, 1) * sin_b)


def _head_rows(cache_ref, head, n_heads):
    past = cache_ref.shape[0] // n_heads
    return cache_ref[pl.ds(head, past, stride=n_heads), :]


def _lat_attn_a_kernel(q_ref, k_ref, v_ref, ck_ref, cv_ref, cb_ref, qn_ref, kn_ref, o_ref, *, pair0, index):
    head = pl.program_id(0)
    q = (_rms(q_ref[...], qn_ref[...]) * ATTN_SCALE).astype(BF16)
    k = _rms(k_ref[...], kn_ref[...]).astype(BF16)
    v1 = _with_ones(v_ref[...])
    ck = _head_rows(ck_ref, head, NA_HEADS).astype(BF16)
    cv1 = _with_ones(_head_rows(cv_ref, head, NA_HEADS))
    pair = 2 * GRID_W
    rows_per_block = pair // GRID_W
    n_pairs = len(index[0])
    for blk in range(len(pair0) // rows_per_block):
        grid_rows = range(blk * rows_per_block, (blk + 1) * rows_per_block)
        p0 = pair0[grid_rows[0]]
        assert all(pair0[r] == p0 for r in grid_rows), "query rows of a block share their key pairs"
        rows = slice(blk * pair, (blk + 1) * pair)
        keys = slice(p0 * pair, (p0 + n_pairs) * pair)
        bias = jnp.concatenate(
            [jnp.concatenate([cb_ref[u] for u in index[r]], axis=1) for r in grid_rows], axis=0)
        qs = q[rows]
        s_win = _dot_nt(qs, k[keys]) + bias
        s_ctx = _dot_nt(qs, ck)
        o_ref[rows, :] = _softmax_pv([s_win, s_ctx], [v1[keys], cv1]).astype(o_ref.dtype)


def _lat_attn_b_kernel(q_ref, k_ref, v_ref, ck_ref, cv_ref, cos_ref, sina_ref, sinb_ref,
                       qn_ref, kn_ref, sink_ref, o_ref):
    cos, sin_a, sin_b = cos_ref[...], sina_ref[...], sinb_ref[...]
    q = _rope(_rms(q_ref[...], qn_ref[...]), cos, sin_a, sin_b)
    k = _rope(_rms(k_ref[...], kn_ref[...]), cos, sin_a, sin_b)
    q = (q * ATTN_SCALE).astype(BF16)
    k = k.astype(BF16)
    v1 = _with_ones(v_ref[...])
    length = q.shape[0]
    kv = pl.program_id(0) // NB_GROUP
    ck = _head_rows(ck_ref, kv, NB_KV_HEADS).astype(BF16)
    cv1 = _with_ones(_head_rows(cv_ref, kv, NB_KV_HEADS))
    sink = sink_ref[pl.program_id(0)]
    for qb in range(length // WINDOW):
        rows = slice(qb * WINDOW, (qb + 1) * WINDOW)
        lo, hi = max(0, (qb - 1) * WINDOW), min(length, (qb + 2) * WINDOW)
        qs = q[rows]
        s_win = _dot_nt(qs, k[lo:hi])
        qi = qb * WINDOW + lax.broadcasted_iota(jnp.int32, s_win.shape, 0)
        kj = lo + lax.broadcasted_iota(jnp.int32, s_win.shape, 1)
        s_win = jnp.where(jnp.abs(qi - kj) <= WINDOW, s_win, NEG_INF)
        s_ctx = _dot_nt(qs, ck)
        o_ref[rows, :] = _softmax_pv([s_win, s_ctx], [v1[lo:hi], cv1], sink=sink).astype(o_ref.dtype)


def _lat_attn_a_call(proj, length, past, ck, cv, cb, pair0, index, qn, kn):
    m = proj.shape[0]
    head = lambda h0: pl.BlockSpec((length, HEAD_DIM), lambda h, b: (b, h0 + h))
    cache = pl.BlockSpec((past * NA_HEADS, HEAD_DIM), lambda h, b: (b, 0))
    vec = pl.BlockSpec((1, HEAD_DIM), lambda h, b: (0, 0))
    return pl.pallas_call(
        functools.partial(_lat_attn_a_kernel, pair0=pair0, index=index),
        grid=(NA_HEADS, m // length),
        in_specs=[head(QA_HEAD0), head(KA_HEAD0), head(VA_HEAD0), cache, cache,
                  pl.BlockSpec((None,) + cb.shape[1:], lambda h, b: (h, 0, 0, 0)), vec, vec],
        out_specs=pl.BlockSpec((length, HEAD_DIM), lambda h, b: (b, h)),
        out_shape=jax.ShapeDtypeStruct((m, NA_WIDTH), BF16),
        compiler_params=_cparams(("arbitrary", "arbitrary")),
        name="lat_attn_a",
    )(proj, proj, proj, ck, cv, cb, qn, kn)


def _lat_attn_b_call(proj, length, past, ck, cv, cos, sin_a, sin_b, qn, kn, sink):
    m = proj.shape[0]
    qspec = pl.BlockSpec((length, HEAD_DIM), lambda h, b: (b, QB_HEAD0 + h))
    kvspec = lambda h0: pl.BlockSpec((length, HEAD_DIM), lambda h, b: (b, h0 + h // NB_GROUP))
    cache = pl.BlockSpec((past * NB_KV_HEADS, HEAD_DIM), lambda h, b: (b, 0))
    table = pl.BlockSpec((length, HEAD_DIM), lambda h, b: (0, 0))
    vec = pl.BlockSpec((1, HEAD_DIM), lambda h, b: (0, 0))
    return pl.pallas_call(
        _lat_attn_b_kernel,
        grid=(NB_Q_HEADS, m // length),
        in_specs=[qspec, kvspec(KB_HEAD0), kvspec(VB_HEAD0), cache, cache,
                  table, table, table, vec, vec, pl.BlockSpec(memory_space=pltpu.SMEM)],
        out_specs=pl.BlockSpec((length, HEAD_DIM), lambda h, b: (b, h)),
        out_shape=jax.ShapeDtypeStruct((m, NB_WIDTH), BF16),
        compiler_params=_cparams(("arbitrary", "arbitrary")),
        name="lat_attn_b",
    )(proj, proj, proj, ck, cv, cos, sin_a, sin_b, qn, kn, sink)


def _na_bias_blocks(rpb, length):
    rows = length // GRID_W
    kr_n = min(NA_WIN_ROWS, rows)
    n_pairs = min(kr_n // 2 + 1, rows // 2)
    r = np.arange(rows)
    c = np.arange(GRID_W)
    r0 = np.clip(r - kr_n // 2, 0, rows - kr_n)
    c0 = np.clip(c - NA_WIN_COLS // 2, 0, GRID_W - NA_WIN_COLS)
    pair0 = np.minimum(r0 // 2, rows // 2 - n_pairs)
    kr = 2 * (pair0[:, None, None] + np.arange(n_pairs)[None, :, None]) + np.arange(2)[None, None, :]
    row_ok = (kr >= r0[:, None, None]) & (kr < r0[:, None, None] + kr_n)
    col_ok = (c[None, :] >= c0[:, None]) & (c[None, :] < c0[:, None] + NA_WIN_COLS)
    dr = kr - r[:, None, None] + (NA_WIN_ROWS - 1)
    dc = np.clip(c[None, :] - c[:, None], -(NA_WIN_COLS - 1), NA_WIN_COLS - 1) + (NA_WIN_COLS - 1)
    offs = np.where(row_ok, dr, -1).reshape(-1, 2)
    uniq, inverse = np.unique(offs, axis=0, return_inverse=True)
    index = inverse.reshape(rows, n_pairs)
    row_sel = (uniq[:, :, None] == np.arange(2 * NA_WIN_ROWS - 1)[None, None, :]).astype(np.float32)
    col_hit = (dc[None] == np.arange(2 * NA_WIN_COLS - 1)[:, None, None]) & col_ok[None]
    col_sel = np.zeros((2,) + col_hit.shape[:2] + (2 * GRID_W,), np.float32)
    for half in range(2):
        col_sel[half, :, :, half * GRID_W:(half + 1) * GRID_W] = col_hit
    hi = lax.Precision.HIGHEST
    per_col = jnp.einsum("hde,lecn->hldcn", rpb.astype(F32), col_sel, precision=hi)
    table = jnp.einsum("uld,hldcn->hucn", row_sel, per_col, precision=hi)
    valid = ((uniq >= 0)[:, None, :, None] & col_ok[None, :, None, :]).reshape(len(uniq), GRID_W, 2 * GRID_W)
    return (jnp.where(valid[None], table, NEG_INF), tuple(int(p) for p in pair0),
            tuple(tuple(int(u) for u in row) for row in index))


def _rope_tables(length):
    t = jnp.arange(length)
    row = (t // GRID_W).astype(F32)
    col = (t % GRID_W).astype(F32)
    n_freq = HEAD_DIM // 4
    inv = ROPE_BASE ** (-jnp.arange(n_freq, dtype=F32) / n_freq)
    ar = row[:, None] * inv
    ac = col[:, None] * inv
    ang = jnp.concatenate([ar, ar, ac, ac], axis=-1)
    cos, sin = jnp.cos(ang), jnp.sin(ang)
    lane = jnp.arange(HEAD_DIM)
    takes_left = ((lane // n_freq) % 2 == 0)[None, :]
    return cos, jnp.where(takes_left, -sin, 0.0), jnp.where(takes_left, 0.0, sin)


MERGE_TM = 512


def _resident(shape):
    zeros = (0,) * len(shape)
    return pl.BlockSpec(shape, lambda i: zeros, pipeline_mode=pl.Buffered(1))


def _mix_kernel(oa_ref, ob_ref, *refs):
    n_blk = (len(refs) - 3) // 2
    wpa_ref, wpb_ref, mix_ref = refs[-3:]
    gates = lambda blocks: jnp.concatenate([r[...] for r in blocks], axis=1).astype(F32)
    ya = _dot(oa_ref[...], wpa_ref[...].astype(BF16))
    yb = _dot(ob_ref[...], wpb_ref[...].astype(BF16))
    mix = jax.nn.sigmoid(gates(refs[:n_blk])) * ya + jax.nn.sigmoid(gates(refs[n_blk:2 * n_blk])) * yb
    mix_ref[...] = mix.astype(mix_ref.dtype)


def _mix_call(oa, ob, proj, w_pa, w_pb):
    m = oa.shape[0]
    d = w_pa.shape[1]
    row = lambda i: (i, 0)
    gate0, n_blk = GATE_COL0 // INPROJ_TN, d // INPROJ_TN
    gate_specs = [pl.BlockSpec((MERGE_TM, INPROJ_TN), lambda i, q=q: (i, gate0 + q)) for q in range(2 * n_blk)]
    return pl.pallas_call(
        _mix_kernel,
        grid=(m // MERGE_TM,),
        in_specs=[pl.BlockSpec((MERGE_TM, NA_WIDTH), row),
                  pl.BlockSpec((MERGE_TM, NB_WIDTH), row),
                  *gate_specs,
                  _resident((NA_WIDTH, d)), _resident((NB_WIDTH, d))],
        out_specs=pl.BlockSpec((MERGE_TM, d), row),
        out_shape=jax.ShapeDtypeStruct((m, d), BF16),
        compiler_params=_cparams(("arbitrary",)),
        name="mix",
    )(oa, ob, *([proj] * (2 * n_blk)), w_pa, w_pb)


def _outproj_kernel(mix_ref, wout_ref, x_ref, g1_ref, sc2_ref, sh2_ref, n2w_ref, wr_ref,
                    x1_ref, h2_ref, lg_ref, acc_ref):
    i = pl.program_id(0)
    last = pl.num_programs(0) - 1

    def finish(prod):
        x1 = x_ref[...] + g1_ref[0] * prod
        x1_ref[...] = x1
        h2 = _rms(x1, n2w_ref[...]) * (1.0 + sc2_ref[0]) + sh2_ref[0]
        h2_ref[...] = _rows_to_tiles(_pack_halves(h2))
        lg_ref[...] = _dot_split(h2, wr_ref[...])

    @pl.when(i == 0)
    def _():
        acc_ref[...] = _dot(mix_ref[...], wout_ref[...].astype(BF16))

    @pl.when((i > 0) & (i < last))
    def _():
        prod = acc_ref[...]
        acc_ref[...] = _dot(mix_ref[...], wout_ref[...].astype(BF16))
        finish(prod)

    @pl.when(i == last)
    def _():
        finish(acc_ref[...])


def _outproj_call(mix, w_out, x, g1, sc2, sh2, n2w, wr, tiles_per_group):
    m, d = x.shape
    n_tiles = m // MERGE_TM
    ahead = lambda i: (jnp.minimum(i, n_tiles - 1), 0)
    done = lambda i: (jnp.maximum(i - 1, 0), 0)
    grp = lambda i: (jnp.maximum(i - 1, 0) // tiles_per_group, 0, 0)
    return pl.pallas_call(
        _outproj_kernel,
        grid=(n_tiles + 1,),
        in_specs=[pl.BlockSpec((MERGE_TM, d), ahead),
                  _resident((d, d)),
                  pl.BlockSpec((MERGE_TM, d), done),
                  pl.BlockSpec((1, 1, d), grp), pl.BlockSpec((1, 1, d), grp), pl.BlockSpec((1, 1, d), grp),
                  pl.BlockSpec((1, d), lambda i: (0, 0)),
                  _resident((d, LOGIT_PAD))],
        out_specs=[pl.BlockSpec((MERGE_TM, d), done),
                   pl.BlockSpec((MERGE_TM,) + ROW_TILE, lambda i: (jnp.maximum(i - 1, 0), 0, 0)),
                   pl.BlockSpec((MERGE_TM, LOGIT_PAD), done)],
        out_shape=[jax.ShapeDtypeStruct((m, d), F32),
                   jax.ShapeDtypeStruct((m,) + ROW_TILE, jnp.uint32),
                   jax.ShapeDtypeStruct((m, LOGIT_PAD), F32)],
        scratch_shapes=[pltpu.VMEM((MERGE_TM, d), F32)],
        compiler_params=_cparams(("arbitrary",)),
        name="outproj",
    )(mix, w_out, x, g1, sc2, sh2, n2w, wr)


ROUTE_TM = 512


def _first_index_of_max(vals, idx, n):
    mx = jnp.max(vals, axis=0, keepdims=True)
    first = jnp.min(jnp.where(vals == mx, idx, n), axis=0, keepdims=True)
    return mx, first


def _route_kernel(lg_ctx_ref, lg_lat_ref, bias_ref, eid_ref, gw_ref, rank_ref, cnt_ref, base_ref, *, ctx_tiles):
    step = pl.program_id(0)

    @pl.when(step == 0)
    def _():
        base_ref[...] = jnp.zeros_like(base_ref)

    lg = jnp.where(step < ctx_tiles, lg_ctx_ref[...], lg_lat_ref[...])
    lt = lg.T + bias_ref[...]
    n_tok = lt.shape[1]
    le = lt[0:N_EXPERTS]
    lgrp = lt[N_EXPERTS:N_EXPERTS + N_GROUPS]
    gi = lax.broadcasted_iota(jnp.int32, (N_GROUPS, n_tok), 0)
    gmax, gsel = _first_index_of_max(lgrp, gi, N_GROUPS)
    pg_sel = 1.0 / jnp.sum(jnp.exp(lgrp - gmax), axis=0, keepdims=True)
    le_sel = jnp.zeros((EXPERTS_PER_GROUP, n_tok), F32)
    for g in range(N_GROUPS):
        le_sel = jnp.where(gsel == g, le[g * EXPERTS_PER_GROUP:(g + 1) * EXPERTS_PER_GROUP], le_sel)
    ei = lax.broadcasted_iota(jnp.int32, (EXPERTS_PER_GROUP, n_tok), 0)
    v0, i0 = _first_index_of_max(le_sel, ei, EXPERTS_PER_GROUP)
    rest = jnp.where(ei == i0, -jnp.inf, le_sel)
    v1, i1 = _first_index_of_max(rest, ei, EXPERTS_PER_GROUP)
    e1 = jnp.exp(v1 - v0)
    w0 = pg_sel / (1.0 + e1)
    w1 = pg_sel * e1 / (1.0 + e1)
    eid0 = gsel * EXPERTS_PER_GROUP + i0
    eid1 = gsel * EXPERTS_PER_GROUP + i1

    xi = lax.broadcasted_iota(jnp.int32, (N_EXPERTS, n_tok), 0)
    si = lax.broadcasted_iota(jnp.int32, (n_tok, n_tok), 0)
    ti = lax.broadcasted_iota(jnp.int32, (n_tok, n_tok), 1)
    before = (si < ti).astype(BF16)
    base = base_ref[...]
    hot0 = (xi == eid0).astype(F32)
    hot1 = (xi == eid1).astype(F32)
    pre0 = _dot(hot0.astype(BF16), before)
    pre1 = _dot(hot1.astype(BF16), before)
    tot0 = jnp.sum(hot0, axis=1, keepdims=True)
    tot1 = jnp.sum(hot1, axis=1, keepdims=True)
    rank0 = jnp.sum(hot0 * (base + pre0), axis=0, keepdims=True)
    rank1 = jnp.sum(hot1 * (base + tot0 + pre1), axis=0, keepdims=True)
    base = base + tot0 + tot1
    base_ref[...] = base

    ri = lax.broadcasted_iota(jnp.int32, (8, n_tok), 0)
    pick = lambda a, b: jnp.where(ri == 0, a, jnp.where(ri == 1, b, jnp.zeros_like(a)))
    eid_ref[...] = pick(eid0, eid1)
    gw_ref[...] = pick(w0, w1)
    rank_ref[...] = pick(rank0, rank1).astype(jnp.int32)
    cnt_ref[...] = jnp.broadcast_to(base, cnt_ref.shape).astype(jnp.int32)


def _route_call(logits_ctx, logits_lat, bias_col):
    ctx_tiles = logits_ctx.shape[0] // ROUTE_TM
    t = logits_ctx.shape[0] + logits_lat.shape[0]
    tok = pl.BlockSpec((8, ROUTE_TM), lambda i: (0, i))
    return pl.pallas_call(
        functools.partial(_route_kernel, ctx_tiles=ctx_tiles),
        grid=(t // ROUTE_TM,),
        in_specs=[pl.BlockSpec((ROUTE_TM, LOGIT_PAD), lambda i: (jnp.minimum(i, ctx_tiles - 1), 0)),
                  pl.BlockSpec((ROUTE_TM, LOGIT_PAD), lambda i: (jnp.maximum(i - ctx_tiles, 0), 0)),
                  pl.BlockSpec((LOGIT_PAD, 1), lambda i: (0, 0))],
        out_specs=[tok, tok, tok, pl.BlockSpec((N_EXPERTS, 128), lambda i: (0, 0))],
        out_shape=[jax.ShapeDtypeStruct((8, t), jnp.int32),
                   jax.ShapeDtypeStruct((8, t), F32),
                   jax.ShapeDtypeStruct((8, t), jnp.int32),
                   jax.ShapeDtypeStruct((N_EXPERTS, 128), jnp.int32)],
        scratch_shapes=[pltpu.VMEM((N_EXPERTS, 1), F32)],
        compiler_params=_cparams(("arbitrary",)),
        name="route",
    )(logits_ctx, logits_lat, bias_col)


def _moe_layout(n_pairs):
    padded_rows = -(-(n_pairs + N_EXPERTS * (MOE_ROW_BLOCK - 1)) // MOE_ROW_BLOCK) * MOE_ROW_BLOCK
    n_items = (padded_rows + N_EXPERTS * (MOE_ROW_GROUP - MOE_ROW_BLOCK)) // MOE_ROW_GROUP
    return padded_rows, n_items


def _routing_tables(eid, rank, counts, n_items):
    padded = (counts + MOE_ROW_BLOCK - 1) // MOE_ROW_BLOCK * MOE_ROW_BLOCK
    pad_end = jnp.cumsum(padded)
    pad_start = pad_end - padded
    hot = eid[..., None] == jnp.arange(N_EXPERTS, dtype=jnp.int32)
    dest = (jnp.sum(jnp.where(hot, pad_start, 0), axis=-1) + rank).astype(jnp.int32)
    tail = jnp.where(padded > counts, pad_end - MOE_ROW_BLOCK, -1).astype(jnp.int32)
    per_expert = (padded + MOE_ROW_GROUP - 1) // MOE_ROW_GROUP
    item_end = jnp.cumsum(per_expert)
    item_start = item_end - per_expert
    total = item_end[-1]
    ii = jnp.arange(n_items, dtype=jnp.int32)
    owner = lambda v: jnp.minimum(jnp.sum(item_end[None, :] <= v[:, None], axis=1), N_EXPERTS - 1).astype(jnp.int32)
    e_of = owner(ii)
    sel = e_of[:, None] == jnp.arange(N_EXPERTS, dtype=jnp.int32)
    pick = lambda v: jnp.sum(jnp.where(sel, v[None, :], 0), axis=1)
    valid = ii < total
    e_last = owner(jnp.maximum(total - 1, 0)[None])[0]
    local = ii - pick(item_start)
    row0 = pick(pad_start) + local * MOE_ROW_GROUP
    nblk = jnp.clip((pick(padded) - local * MOE_ROW_GROUP) // MOE_ROW_BLOCK, 0, MOE_ROW_GROUP // MOE_ROW_BLOCK)
    item_e = jnp.where(valid, e_of, e_last).astype(jnp.int32)
    item_row0 = jnp.where(valid, row0, 0).astype(jnp.int32)
    item_nblk = jnp.where(valid, nblk, 0).astype(jnp.int32)
    return dest, tail, item_e, item_row0, item_nblk


DISPATCH_TOKENS = 256
ROW_DMA_GROUP = 8


def _row_copy(src, s, dst, d, sem):
    return pltpu.make_async_copy(src.at[pl.ds(s, 1)], dst.at[pl.ds(d, 1)], sem)


DISPATCH_SLOTS = 3


def _dispatch_kernel(dest_ref, tail_ref, h_ctx, h_lat, xs, zero_buf, src_buf, sem, *, n_ctx, n_tok):
    step = pl.program_id(0)
    last = pl.num_programs(0) - 1
    ctx_tiles = n_ctx // DISPATCH_TOKENS
    tail_copy = lambda e: pltpu.make_async_copy(
        zero_buf, xs.at[pl.ds(pl.multiple_of(tail_ref[e], MOE_ROW_BLOCK), MOE_ROW_BLOCK)], sem.at[0])

    def fetch(tile):
        dst, fsem = src_buf.at[tile % DISPATCH_SLOTS], sem.at[1 + tile % 2]

        @pl.when(tile < ctx_tiles)
        def _():
            pltpu.make_async_copy(h_ctx.at[pl.ds(tile * DISPATCH_TOKENS, DISPATCH_TOKENS)], dst, fsem).start()

        @pl.when(tile >= ctx_tiles)
        def _():
            pltpu.make_async_copy(
                h_lat.at[pl.ds((tile - ctx_tiles) * DISPATCH_TOKENS, DISPATCH_TOKENS)], dst, fsem).start()

    def drain_scatters(tile):
        for _ in range(2):
            pltpu.make_async_copy(src_buf.at[tile % DISPATCH_SLOTS], xs.at[pl.ds(0, DISPATCH_TOKENS)],
                                  sem.at[3 + tile % 2]).wait()

    @pl.when(step == 0)
    def _():
        zero_buf[...] = jnp.zeros_like(zero_buf)
        for e in range(N_EXPERTS):
            @pl.when(tail_ref[e] >= 0)
            def _():
                tail_copy(e).start()
        for e in range(N_EXPERTS):
            @pl.when(tail_ref[e] >= 0)
            def _():
                tail_copy(e).wait()
        fetch(0)

    @pl.when(step < last)
    def _():
        fetch(step + 1)

    src = src_buf.at[step % DISPATCH_SLOTS]
    pltpu.make_async_copy(h_ctx.at[pl.ds(0, DISPATCH_TOKENS)], src, sem.at[1 + step % 2]).wait()
    tok0 = step * DISPATCH_TOKENS
    ssem = sem.at[3 + step % 2]

    def issue(g, carry):
        base = pl.multiple_of(g * ROW_DMA_GROUP, ROW_DMA_GROUP)
        for j in range(ROW_DMA_GROUP):
            _row_copy(src, base + j, xs, dest_ref[tok0 + base + j], ssem).start(priority=0)
            _row_copy(src, base + j, xs, dest_ref[n_tok + tok0 + base + j], ssem).start(priority=1)
        return carry

    lax.fori_loop(0, DISPATCH_TOKENS // ROW_DMA_GROUP, issue, 0)

    @pl.when(step > 0)
    def _():
        drain_scatters(step - 1)

    @pl.when(step == last)
    def _():
        drain_scatters(step)


def _dispatch_call(dest_flat, tail, h_ctx, h_lat, padded_rows):
    n_ctx = h_ctx.shape[0]
    row = h_ctx.shape[1:]
    n_tok = n_ctx + h_lat.shape[0]
    return pl.pallas_call(
        functools.partial(_dispatch_kernel, n_ctx=n_ctx, n_tok=n_tok),
        grid_spec=pltpu.PrefetchScalarGridSpec(
            num_scalar_prefetch=2,
            grid=(n_tok // DISPATCH_TOKENS,),
            in_specs=[pl.BlockSpec(memory_space=pl.ANY), pl.BlockSpec(memory_space=pl.ANY)],
            out_specs=pl.BlockSpec(memory_space=pl.ANY),
            scratch_shapes=[pltpu.VMEM((MOE_ROW_BLOCK,) + row, h_ctx.dtype),
                            pltpu.VMEM((DISPATCH_SLOTS, DISPATCH_TOKENS) + row, h_ctx.dtype),
                            pltpu.SemaphoreType.DMA((5,))]),
        out_shape=jax.ShapeDtypeStruct((padded_rows,) + row, h_ctx.dtype),
        compiler_params=_cparams(("arbitrary",)),
        name="dispatch",
    )(dest_flat, tail, h_ctx, h_lat)


def _moe_kernel(item_e, item_row0, item_nblk, xs, w1_ref, w3_ref, w2_ref, ys,
                x_in, x_bf, acc, y_out, w1_bf, w3_bf, w2_bf, sem, *, n_chunks):
    i = pl.program_id(0)
    c = pl.program_id(1)
    n_items = pl.num_programs(0)
    last_c = n_chunks - 1
    nblk = item_nblk[i]
    max_blk = MOE_ROW_GROUP // MOE_ROW_BLOCK
    blk = lambda b: pl.ds(b * MOE_ROW_BLOCK, MOE_ROW_BLOCK)

    def rows_of(item, b):
        return pl.ds(pl.multiple_of(item_row0[item], MOE_ROW_BLOCK) + b * MOE_ROW_BLOCK, MOE_ROW_BLOCK)

    load = lambda item, b: pltpu.make_async_copy(xs.at[rows_of(item, b)], x_in.at[blk(b)], sem.at[0])
    store = lambda item, b: pltpu.make_async_copy(y_out.at[blk(b)], ys.at[rows_of(item, b)], sem.at[1])

    def for_blocks(item, fn):
        n = item_nblk[item]
        for b in range(max_blk):
            @pl.when(b < n)
            def _():
                fn(item, b)

    @pl.when(c == 0)
    def _():
        @pl.when(i == 0)
        def _():
            for_blocks(0, lambda it, b: load(it, b).start())

        for_blocks(i, lambda it, b: load(it, b).wait())

    @pl.when((c == last_c) & (i > 0))
    def _():
        for_blocks(i - 1, lambda it, b: store(it, b).wait())

    def run_blocks(first, last):
        def block(rows, w1, w3, w2):
            if first:
                x = _unpack_halves(_tiles_to_rows(x_in[rows])).astype(BF16)
                x_bf[rows, :] = x
            else:
                x = x_bf[rows, :]
            h1 = _dot(x, w1)
            h3 = _dot(x, w3)
            a = (h1 * jax.nn.sigmoid(h1) * h3).astype(BF16)
            y = _dot(a, w2)
            if not first:
                y = acc[rows, :] + y
            if last:
                y_out[rows] = _rows_to_tiles(_pack_halves(y))
            else:
                acc[rows, :] = y

        w1 = w1_ref[0].astype(BF16)
        w3 = w3_ref[0].astype(BF16)
        w2 = w2_ref[0].astype(BF16)
        w1_bf[...] = w1
        w3_bf[...] = w3
        w2_bf[...] = w2
        block(blk(0), w1, w3, w2)

        def body(b, carry):
            rows = pl.ds(pl.multiple_of(b * MOE_ROW_BLOCK, MOE_ROW_BLOCK), MOE_ROW_BLOCK)
            block(rows, w1_bf[...], w3_bf[...], w2_bf[...])
            return carry

        lax.fori_loop(1, nblk, body, 0)

    for first, last in sorted({(cc == 0, cc == last_c) for cc in range(n_chunks)}):
        chunk_is = (c == 0) if first else ((c == last_c) if last else ((c > 0) & (c < last_c)))

        @pl.when((nblk > 0) & chunk_is)
        def _():
            run_blocks(first, last)

    @pl.when((c == 0) & (i + 1 < n_items))
    def _():
        for_blocks(i + 1, lambda it, b: load(it, b).start())

    @pl.when(c == last_c)
    def _():
        for_blocks(i, lambda it, b: store(it, b).start())

        @pl.when(i == n_items - 1)
        def _():
            for_blocks(i, lambda it, b: store(it, b).wait())


def _moe_call(item_e, item_row0, item_nblk, xs, w1, w3, w2):
    padded_rows, row = xs.shape[0], xs.shape[1:]
    d = w1.shape[1]
    n_items = item_e.shape[0]
    f = w1.shape[2]
    nc = f // MOE_F_CHUNK
    chunk = lambda i, c, ib: jnp.where(ib[i] > 0, c, nc - 1)
    return pl.pallas_call(
        functools.partial(_moe_kernel, n_chunks=nc),
        grid_spec=pltpu.PrefetchScalarGridSpec(
            num_scalar_prefetch=3,
            grid=(n_items, nc),
            in_specs=[pl.BlockSpec(memory_space=pl.ANY),
                      pl.BlockSpec((1, d, MOE_F_CHUNK), lambda i, c, ie, ir, ib: (ie[i], 0, chunk(i, c, ib))),
                      pl.BlockSpec((1, d, MOE_F_CHUNK), lambda i, c, ie, ir, ib: (ie[i], 0, chunk(i, c, ib))),
                      pl.BlockSpec((1, MOE_F_CHUNK, d), lambda i, c, ie, ir, ib: (ie[i], chunk(i, c, ib), 0))],
            out_specs=pl.BlockSpec(memory_space=pl.ANY),
            scratch_shapes=[pltpu.VMEM((MOE_ROW_GROUP,) + row, xs.dtype),
                            pltpu.VMEM((MOE_ROW_GROUP, d), BF16),
                            pltpu.VMEM((MOE_ROW_GROUP, d), F32),
                            pltpu.VMEM((MOE_ROW_GROUP,) + row, xs.dtype),
                            pltpu.VMEM((d, MOE_F_CHUNK), BF16),
                            pltpu.VMEM((d, MOE_F_CHUNK), BF16),
                            pltpu.VMEM((MOE_F_CHUNK, d), BF16),
                            pltpu.SemaphoreType.DMA((2,))]),
        out_shape=jax.ShapeDtypeStruct((padded_rows,) + row, xs.dtype),
        compiler_params=_cparams(("arbitrary", "arbitrary")),
        name="moe",
    )(item_e, item_row0, item_nblk, xs, w1, w3, w2)


COMBINE_TM = 256


def _combine_kernel(dest_ref, x1_ref, gw_ref, g2_ref, ys, o_ref, y0, y1, sem, *, tok_base, n_tok):
    i = pl.program_id(0)

    def gather(tile, slot):
        tok0 = tok_base + tile * COMBINE_TM

        def issue(g, carry):
            base = pl.multiple_of(g * ROW_DMA_GROUP, ROW_DMA_GROUP)
            for j in range(ROW_DMA_GROUP):
                _row_copy(ys, dest_ref[tok0 + base + j], y0.at[slot], base + j, sem.at[slot]).start(priority=0)
                _row_copy(ys, dest_ref[n_tok + tok0 + base + j], y1.at[slot], base + j, sem.at[slot]).start(priority=1)
            return carry

        lax.fori_loop(0, COMBINE_TM // ROW_DMA_GROUP, issue, 0)

    @pl.when(i == 0)
    def _():
        gather(0, 0)

    @pl.when(i + 1 < pl.num_programs(0))
    def _():
        gather(i + 1, (i + 1) % 2)

    slot = i % 2
    for buf in (y0, y1):
        pltpu.make_async_copy(ys.at[pl.ds(0, COMBINE_TM)], buf.at[slot], sem.at[slot]).wait()
    gw = gw_ref[...]
    moe = (gw[:, 0:1] * _unpack_halves(_tiles_to_rows(y0[slot]))
           + gw[:, 1:2] * _unpack_halves(_tiles_to_rows(y1[slot])))
    o_ref[...] = x1_ref[...] + g2_ref[0] * moe


def _combine_call(dest_flat, x1, gw, g2, ys, tok_base, n_tok, tiles_per_group):
    m, d = x1.shape
    return pl.pallas_call(
        functools.partial(_combine_kernel, tok_base=tok_base, n_tok=n_tok),
        grid_spec=pltpu.PrefetchScalarGridSpec(
            num_scalar_prefetch=1,
            grid=(m // COMBINE_TM,),
            in_specs=[pl.BlockSpec((COMBINE_TM, d), lambda i, dr: (i, 0)),
                      pl.BlockSpec((COMBINE_TM, 2), lambda i, dr: (i, 0)),
                      pl.BlockSpec((1, 1, d), lambda i, dr: (i // tiles_per_group, 0, 0)),
                      pl.BlockSpec(memory_space=pl.ANY)],
            out_specs=pl.BlockSpec((COMBINE_TM, d), lambda i, dr: (i, 0)),
            scratch_shapes=[pltpu.VMEM((2, COMBINE_TM) + ys.shape[1:], ys.dtype),
                            pltpu.VMEM((2, COMBINE_TM) + ys.shape[1:], ys.dtype),
                            pltpu.SemaphoreType.DMA((2,))]),
        out_shape=jax.ShapeDtypeStruct((m, d), F32),
        compiler_params=_cparams(("arbitrary",)),
        name="combine",
    )(dest_flat, x1, gw, g2, ys)


def kernel(x_prompt, x_sample, cache_a_k, cache_a_v, cache_b_k, cache_b_v, c, c_ctx, norm1_w, norm2_w, w_ada, b_ada, w_in, qn_a, kn_a, qn_b, kn_b, rpb_a, sink_b, w_pa, w_pb, w_out, w_rg, b_rg, w_re, b_re, w1, w3, w2):
    batch, seq, d = x_prompt.shape
    dec_batch, dec_seq, _ = x_sample.shape
    depth = norm1_w.shape[0]
    assert depth == 1, "one trunk layer"
    past = cache_a_k.shape[2]
    n_ctx, n_lat = batch * seq, dec_batch * dec_seq
    n_tok = n_ctx + n_lat

    xc = x_prompt.reshape(n_ctx, d)
    xl = x_sample.reshape(n_lat, d)

    cond = jnp.concatenate([c_ctx[None, :], c], axis=0)
    mod = _ada_call(cond, w_ada[0], b_ada[0][None, :])
    sh1, sc1, g1, sh2, sc2, g2 = [mod[:, i * d:(i + 1) * d][:, None, :] for i in range(6)]
    ctx_rows, lat_rows = slice(0, 1), slice(1, 1 + dec_batch)

    nw1, nw2 = norm1_w[0][None, :], norm2_w[0][None, :]
    qna, kna, qnb, knb = qn_a[0][None, :], kn_a[0][None, :], qn_b[0][None, :], kn_b[0][None, :]
    sink = sink_b[0]
    n_logits = N_EXPERTS + N_GROUPS
    wr = jnp.concatenate([w_re[0], w_rg[0], jnp.zeros((d, LOGIT_PAD - n_logits), F32)], axis=1)
    br = jnp.concatenate([b_re[0], b_rg[0], jnp.zeros((LOGIT_PAD - n_logits,), F32)])[:, None]

    proj_c = _inproj_call(xc, nw1, sc1[ctx_rows], sh1[ctx_rows], w_in[0], n_ctx)
    proj_l = _inproj_call(xl, nw1, sc1[lat_rows], sh1[lat_rows], w_in[0], dec_seq)

    oa_c, ob_c, new_a_k, new_a_v, new_b_k, new_b_v = _ctx_attn_call(proj_c, seq, qna, kna, qnb, knb, sink)

    bias_blocks, pair0, bias_index = _na_bias_blocks(rpb_a[0], dec_seq)
    cos, sin_a, sin_b = _rope_tables(dec_seq)
    rows_of = lambda cache: cache.reshape(-1, HEAD_DIM)
    oa_l = _lat_attn_a_call(proj_l, dec_seq, past, rows_of(cache_a_k), rows_of(cache_a_v),
                            bias_blocks, pair0, bias_index, qna, kna)
    ob_l = _lat_attn_b_call(proj_l, dec_seq, past, rows_of(cache_b_k), rows_of(cache_b_v),
                            cos, sin_a, sin_b, qnb, knb, sink)

    mix_c = _mix_call(oa_c, ob_c, proj_c, w_pa[0], w_pb[0])
    mix_l = _mix_call(oa_l, ob_l, proj_l, w_pa[0], w_pb[0])
    x1_c, h2_c, lg_c = _outproj_call(mix_c, w_out[0], xc, g1[ctx_rows], sc2[ctx_rows], sh2[ctx_rows],
                                     nw2, wr, n_ctx // MERGE_TM)
    x1_l, h2_l, lg_l = _outproj_call(mix_l, w_out[0], xl, g1[lat_rows], sc2[lat_rows], sh2[lat_rows],
                                     nw2, wr, dec_seq // MERGE_TM)

    eid, gw, rank, cnt = _route_call(lg_c, lg_l, br)
    padded_rows, n_items = _moe_layout(2 * n_tok)
    dest, tail, item_e, item_row0, item_nblk = _routing_tables(eid[:2], rank[:2], cnt[:, 0], n_items)
    dest_flat = dest.reshape(-1)
    xs = _dispatch_call(dest_flat, tail, h2_c, h2_l, padded_rows)
    ys = _moe_call(item_e, item_row0, item_nblk, xs, w1[0], w3[0], w2[0])
    gw_t = gw[:2].T
    y_c = _combine_call(dest_flat, x1_c, gw_t[:n_ctx], g2[ctx_rows], ys, 0, n_tok, n_ctx // COMBINE_TM)
    y_l = _combine_call(dest_flat, x1_l, gw_t[n_ctx:], g2[lat_rows], ys, n_ctx, n_tok, dec_seq // COMBINE_TM)

    state = lambda a, heads: a.reshape(batch, 1, seq, heads, HEAD_DIM)
    return (y_c.reshape(batch, seq, d), y_l.reshape(dec_batch, dec_seq, d),
            state(new_a_k, NA_HEADS), state(new_a_v, NA_HEADS),
            state(new_b_k, NB_KV_HEADS), state(new_b_v, NB_KV_HEADS))
```

```python
import functools

import jax
import jax.numpy as jnp
import numpy as np
from jax import lax
from jax.experimental import pallas as pl
from jax.experimental.pallas import tpu as pltpu

D_MODEL = 2048
HEAD_DIM = 128
NA_HEADS = 8
NA_WIDTH = NA_HEADS * HEAD_DIM
NB_Q_HEADS = 8
NB_KV_HEADS = 2
NB_GROUP = NB_Q_HEADS // NB_KV_HEADS
NB_WIDTH = NB_Q_HEADS * HEAD_DIM
NB_KV_WIDTH = NB_KV_HEADS * HEAD_DIM
GRID_W = 64
NA_WIN_ROWS = 8
NA_WIN_COLS = 16
WINDOW = 128
N_GROUPS = 4
EXPERTS_PER_GROUP = 8
N_EXPERTS = N_GROUPS * EXPERTS_PER_GROUP
D_EXPERT = 1024
IN_WIDTH = 3 * NA_WIDTH + NB_WIDTH + 2 * NB_KV_WIDTH + 2 * D_MODEL
ROPE_BASE = 10000.0
NORM_EPS = 1e-6
NEG_INF = -1e30
ATTN_SCALE = HEAD_DIM ** -0.5

QA_HEAD0 = 0
KA_HEAD0 = NA_HEADS
VA_HEAD0 = 2 * NA_HEADS
QB_HEAD0 = 3 * NA_HEADS
KB_HEAD0 = QB_HEAD0 + NB_Q_HEADS
VB_HEAD0 = KB_HEAD0 + NB_KV_HEADS
GATE_COL0 = (VB_HEAD0 + NB_KV_HEADS) * HEAD_DIM

LOGIT_PAD = 128
MOE_ROW_BLOCK = 256
MOE_ROW_GROUP = 1024
MOE_F_CHUNK = 512
VMEM_LIMIT = 56 * 1024 * 1024

F32 = jnp.float32
BF16 = jnp.bfloat16


def _cparams(sem, vmem_limit=VMEM_LIMIT):
    return pltpu.CompilerParams(dimension_semantics=sem, vmem_limit_bytes=vmem_limit)


def _rms(x, w):
    x = x.astype(F32)
    return x * lax.rsqrt(jnp.mean(x * x, axis=-1, keepdims=True) + NORM_EPS) * w


def _dot(a, b):
    return jnp.dot(a, b, preferred_element_type=F32)


def _dot_nt(a, b):
    return lax.dot_general(a, b, (((1,), (1,)), ((), ())), preferred_element_type=F32)


def _pack_halves(x):
    n = x.shape[1] // 2
    lo = lax.bitcast_convert_type(x[:, :n].astype(BF16).astype(F32), jnp.uint32)
    hi = lax.bitcast_convert_type(x[:, n:].astype(BF16).astype(F32), jnp.uint32)
    return hi | (lo >> 16)


def _unpack_halves(w):
    lo = lax.bitcast_convert_type(w << 16, F32)
    hi = lax.bitcast_convert_type(w & jnp.uint32(0xFFFF0000), F32)
    return jnp.concatenate([lo, hi], axis=1)


ROW_TILE = (8, 128)


def _rows_to_tiles(w):
    return w.reshape((w.shape[0],) + ROW_TILE)


def _tiles_to_rows(t):
    return t.reshape(t.shape[0], ROW_TILE[0] * ROW_TILE[1])


def _dot_split(a, b):
    a_hi = a.astype(BF16)
    a_lo = (a - a_hi.astype(F32)).astype(BF16)
    b_hi = b.astype(BF16)
    b_lo = (b - b_hi.astype(F32)).astype(BF16)
    return _dot(a_hi, b_hi) + (_dot(a_lo, b_hi) + _dot(a_hi, b_lo))


ADA_ROWS = 8
ADA_TN = 1024


def _ada_kernel(c_ref, w_ref, b_ref, o_ref):
    n_rows, d, lanes = c_ref.shape
    tn = w_ref.shape[1]

    def body(kb, acc):
        ks = pl.ds(pl.multiple_of(kb * 8, 8), 8)
        w = w_ref[ks, :]
        out = []
        for r in range(n_rows):
            c = c_ref[r, ks, :]
            s = c * jax.nn.sigmoid(c)
            out.append(acc[r] + w * jnp.concatenate([s] * (tn // lanes), axis=1))
        return tuple(out)

    acc = lax.fori_loop(0, d // 8, body, tuple(jnp.zeros((8, tn), F32) for _ in range(n_rows)), unroll=8)
    ri = lax.broadcasted_iota(jnp.int32, (ADA_ROWS, tn), 0)
    res = jnp.zeros((ADA_ROWS, tn), F32)
    for r in range(n_rows):
        row = jnp.sum(acc[r], axis=0, keepdims=True) + b_ref[...]
        res = jnp.where(ri == r, row, res)
    o_ref[...] = res


def _ada_call(cond, w_ada, b_ada):
    n_rows, d = cond.shape
    n = w_ada.shape[1]
    lanes = 128
    cond_lanes = jnp.broadcast_to(cond[:, :, None], (n_rows, d, lanes))
    return pl.pallas_call(
        _ada_kernel,
        grid=(n // ADA_TN,),
        in_specs=[pl.BlockSpec((n_rows, d, lanes), lambda j: (0, 0, 0)),
                  pl.BlockSpec((d, ADA_TN), lambda j: (0, j)),
                  pl.BlockSpec((1, ADA_TN), lambda j: (0, j))],
        out_specs=pl.BlockSpec((ADA_ROWS, ADA_TN), lambda j: (0, j)),
        out_shape=jax.ShapeDtypeStruct((ADA_ROWS, n), F32),
        compiler_params=_cparams(("arbitrary",)),
        name="ada",
    )(cond_lanes, w_ada, b_ada)


INPROJ_TM = 2048
INPROJ_TN = 512
NORM_ROWS = 128


def _inproj_kernel(x_hbm, nw_ref, sc_ref, sh_ref, w_ref, o_ref, x_buf, h_scr, sem, *, rows_per_group):
    i = pl.program_id(0)
    j = pl.program_id(1)
    tm = x_buf.shape[0]
    fetch = lambda tile: pltpu.make_async_copy(
        x_hbm.at[pl.ds(pl.multiple_of(tile * tm, tm), tm), :], x_buf, sem)

    @pl.when(j == 0)
    def _():
        @pl.when(i == 0)
        def _():
            fetch(0).start()

        fetch(i).wait()
        nw = nw_ref[...]

        def body(r, carry):
            rows = pl.ds(pl.multiple_of(r * NORM_ROWS, NORM_ROWS), NORM_ROWS)
            group = (i * tm + r * NORM_ROWS) // rows_per_group
            h_scr[rows, :] = (_rms(x_buf[rows, :], nw) * (1.0 + sc_ref[group]) + sh_ref[group]).astype(BF16)
            return carry

        lax.fori_loop(0, tm // NORM_ROWS, body, 0)

    @pl.when((j == 1) & (i + 1 < pl.num_programs(0)))
    def _():
        fetch(i + 1).start()

    o_ref[...] = _dot(h_scr[...], w_ref[...].astype(BF16)).astype(o_ref.dtype)


def _inproj_call(x, nw, sc, sh, w_in, rows_per_group):
    m, d = x.shape
    n = w_in.shape[1]
    assert n // INPROJ_TN >= 2, "the next token tile is requested in the second column step"
    tm = min(INPROJ_TM, m)
    assert rows_per_group % NORM_ROWS == 0 and m % tm == 0
    whole = pl.BlockSpec(sc.shape, lambda i, j: (0, 0, 0))
    return pl.pallas_call(
        functools.partial(_inproj_kernel, rows_per_group=rows_per_group),
        grid=(m // tm, n // INPROJ_TN),
        in_specs=[pl.BlockSpec(memory_space=pl.ANY),
                  pl.BlockSpec((1, d), lambda i, j: (0, 0)),
                  whole,
                  whole,
                  pl.BlockSpec((d, INPROJ_TN), lambda i, j: (0, j))],
        out_specs=pl.BlockSpec((tm, INPROJ_TN), lambda i, j: (i, j)),
        out_shape=jax.ShapeDtypeStruct((m, n), BF16),
        scratch_shapes=[pltpu.VMEM((tm, d), F32), pltpu.VMEM((tm, d), BF16), pltpu.SemaphoreType.DMA],
        compiler_params=_cparams(("arbitrary", "arbitrary")),
        name="inproj",
    )(x, nw, sc, sh, w_in)


def _with_ones(v):
    return jnp.concatenate([v.astype(BF16), jnp.ones(v.shape, BF16)], axis=1)


def _softmax_pv(scores, values_with_ones, sink=None):
    m = None
    for s in scores:
        ms = jnp.max(s, axis=-1, keepdims=True)
        m = ms if m is None else jnp.maximum(m, ms)
    if sink is not None:
        m = jnp.maximum(m, sink)
    acc = None
    for s, v1 in zip(scores, values_with_ones):
        pv = _dot(jnp.exp(s - m).astype(BF16), v1)
        acc = pv if acc is None else acc + pv
    d = acc.shape[1] // 2
    den = acc[:, d:]
    if sink is not None:
        den = den + jnp.exp(sink - m)
    return acc[:, :d] / den


def _ctx_attn_kernel(qa_ref, ka_ref, va_ref, qb_ref, kb_ref, vb_ref,
                     qna_ref, kna_ref, qnb_ref, knb_ref, sink_ref,
                     oa_ref, ob_ref, nak_ref, nav_ref, nbk_ref, nbv_ref):
    qna, kna, qnb, knb = qna_ref[...], kna_ref[...], qnb_ref[...], knb_ref[...]
    seq = qa_ref.shape[0]
    for h in range(NA_HEADS):
        cols = slice(h * HEAD_DIM, (h + 1) * HEAD_DIM)
        q = (_rms(qa_ref[:, cols], qna) * ATTN_SCALE).astype(BF16)
        k = _rms(ka_ref[:, cols], kna)
        v = va_ref[:, cols]
        nak_ref[pl.ds(h, seq, stride=NA_HEADS), :] = k
        nav_ref[pl.ds(h, seq, stride=NA_HEADS), :] = v.astype(F32)
        s = _dot_nt(q, k.astype(BF16))
        oa_ref[:, cols] = _softmax_pv([s], [_with_ones(v)]).astype(oa_ref.dtype)
    for kv in range(NB_KV_HEADS):
        kcols = slice(kv * HEAD_DIM, (kv + 1) * HEAD_DIM)
        k = _rms(kb_ref[:, kcols], knb)
        v = vb_ref[:, kcols]
        nbk_ref[pl.ds(kv, seq, stride=NB_KV_HEADS), :] = k
        nbv_ref[pl.ds(kv, seq, stride=NB_KV_HEADS), :] = v.astype(F32)
        kb16 = k.astype(BF16)
        vb1 = _with_ones(v)
        for g in range(NB_GROUP):
            hq = kv * NB_GROUP + g
            cols = slice(hq * HEAD_DIM, (hq + 1) * HEAD_DIM)
            q = (_rms(qb_ref[:, cols], qnb) * ATTN_SCALE).astype(BF16)
            s = _dot_nt(q, kb16)
            ob_ref[:, cols] = _softmax_pv([s], [vb1], sink=sink_ref[hq]).astype(ob_ref.dtype)


def _ctx_attn_call(proj, seq, qna, kna, qnb, knb, sink):
    m = proj.shape[0]
    nb = m // seq
    wide = lambda blk: pl.BlockSpec((seq, NA_WIDTH), lambda b: (b, blk))
    narrow = lambda blk: pl.BlockSpec((seq, NB_KV_WIDTH), lambda b: (b, blk))
    vec = pl.BlockSpec((1, HEAD_DIM), lambda b: (0, 0))
    return pl.pallas_call(
        _ctx_attn_kernel,
        grid=(nb,),
        in_specs=[wide(QA_HEAD0 // NA_HEADS), wide(KA_HEAD0 // NA_HEADS), wide(VA_HEAD0 // NA_HEADS),
                  wide(QB_HEAD0 // NA_HEADS), narrow(KB_HEAD0 // NB_KV_HEADS), narrow(VB_HEAD0 // NB_KV_HEADS),
                  vec, vec, vec, vec,
                  pl.BlockSpec(memory_space=pltpu.SMEM)],
        out_specs=[pl.BlockSpec((seq, NA_WIDTH), lambda b: (b, 0)),
                   pl.BlockSpec((seq, NB_WIDTH), lambda b: (b, 0)),
                   pl.BlockSpec((seq * NA_HEADS, HEAD_DIM), lambda b: (b, 0)),
                   pl.BlockSpec((seq * NA_HEADS, HEAD_DIM), lambda b: (b, 0)),
                   pl.BlockSpec((seq * NB_KV_HEADS, HEAD_DIM), lambda b: (b, 0)),
                   pl.BlockSpec((seq * NB_KV_HEADS, HEAD_DIM), lambda b: (b, 0))],
        out_shape=[jax.ShapeDtypeStruct((m, NA_WIDTH), BF16),
                   jax.ShapeDtypeStruct((m, NB_WIDTH), BF16),
                   jax.ShapeDtypeStruct((m * NA_HEADS, HEAD_DIM), F32),
                   jax.ShapeDtypeStruct((m * NA_HEADS, HEAD_DIM), F32),
                   jax.ShapeDtypeStruct((m * NB_KV_HEADS, HEAD_DIM), F32),
                   jax.ShapeDtypeStruct((m * NB_KV_HEADS, HEAD_DIM), F32)],
        compiler_params=_cparams(("arbitrary",)),
        name="ctx_attn",
    )(proj, proj, proj, proj, proj, proj, qna, kna, qnb, knb, sink)


def _rope(x, cos, sin_a, sin_b):
    quarter = HEAD_DIM // 4
    return (x * cos + pltpu.roll(x, HEAD_DIM - quarter, 1) * sin_a
            + pltpu.roll(x, quarter, 1) * sin_b)


def _head_rows(cache_ref, head, n_heads):
    past = cache_ref.shape[0] // n_heads
    return cache_ref[pl.ds(head, past, stride=n_heads), :]


def _lat_attn_a_kernel(q_ref, k_ref, v_ref, ck_ref, cv_ref, cb_ref, qn_ref, kn_ref, o_ref, *, pair0, index):
    head = pl.program_id(0)
    q = (_rms(q_ref[...], qn_ref[...]) * ATTN_SCALE).astype(BF16)
    k = _rms(k_ref[...], kn_ref[...]).astype(BF16)
    v1 = _with_ones(v_ref[...])
    ck = _head_rows(ck_ref, head, NA_HEADS).astype(BF16)
    cv1 = _with_ones(_head_rows(cv_ref, head, NA_HEADS))
    pair = 2 * GRID_W
    rows_per_block = pair // GRID_W
    n_pairs = len(index[0])
    for blk in range(len(pair0) // rows_per_block):
        grid_rows = range(blk * rows_per_block, (blk + 1) * rows_per_block)
        p0 = pair0[grid_rows[0]]
        assert all(pair0[r] == p0 for r in grid_rows), "query rows of a block share their key pairs"
        rows = slice(blk * pair, (blk + 1) * pair)
        keys = slice(p0 * pair, (p0 + n_pairs) * pair)
        bias = jnp.concatenate(
            [jnp.concatenate([cb_ref[u] for u in index[r]], axis=1) for r in grid_rows], axis=0)
        qs = q[rows]
        s_win = _dot_nt(qs, k[keys]) + bias
        s_ctx = _dot_nt(qs, ck)
        o_ref[rows, :] = _softmax_pv([s_win, s_ctx], [v1[keys], cv1]).astype(o_ref.dtype)


def _lat_attn_b_kernel(q_ref, k_ref, v_ref, ck_ref, cv_ref, cos_ref, sina_ref, sinb_ref,
                       qn_ref, kn_ref, sink_ref, o_ref):
    cos, sin_a, sin_b = cos_ref[...], sina_ref[...], sinb_ref[...]
    q = _rope(_rms(q_ref[...], qn_ref[...]), cos, sin_a, sin_b)
    k = _rope(_rms(k_ref[...], kn_ref[...]), cos, sin_a, sin_b)
    q = (q * ATTN_SCALE).astype(BF16)
    k = k.astype(BF16)
    v1 = _with_ones(v_ref[...])
    length = q.shape[0]
    kv = pl.program_id(0) // NB_GROUP
    ck = _head_rows(ck_ref, kv, NB_KV_HEADS).astype(BF16)
    cv1 = _with_ones(_head_rows(cv_ref, kv, NB_KV_HEADS))
    sink = sink_ref[pl.program_id(0)]
    for qb in range(length // WINDOW):
        rows = slice(qb * WINDOW, (qb + 1) * WINDOW)
        lo, hi = max(0, (qb - 1) * WINDOW), min(length, (qb + 2) * WINDOW)
        qs = q[rows]
        s_win = _dot_nt(qs, k[lo:hi])
        qi = qb * WINDOW + lax.broadcasted_iota(jnp.int32, s_win.shape, 0)
        kj = lo + lax.broadcasted_iota(jnp.int32, s_win.shape, 1)
        s_win = jnp.where(jnp.abs(qi - kj) <= WINDOW, s_win, NEG_INF)
        s_ctx = _dot_nt(qs, ck)
        o_ref[rows, :] = _softmax_pv([s_win, s_ctx], [v1[lo:hi], cv1], sink=sink).astype(o_ref.dtype)


def _lat_attn_a_call(proj, length, past, ck, cv, cb, pair0, index, qn, kn):
    m = proj.shape[0]
    head = lambda h0: pl.BlockSpec((length, HEAD_DIM), lambda h, b: (b, h0 + h))
    cache = pl.BlockSpec((past * NA_HEADS, HEAD_DIM), lambda h, b: (b, 0))
    vec = pl.BlockSpec((1, HEAD_DIM), lambda h, b: (0, 0))
    return pl.pallas_call(
        functools.partial(_lat_attn_a_kernel, pair0=pair0, index=index),
        grid=(NA_HEADS, m // length),
        in_specs=[head(QA_HEAD0), head(KA_HEAD0), head(VA_HEAD0), cache, cache,
                  pl.BlockSpec((None,) + cb.shape[1:], lambda h, b: (h, 0, 0, 0)), vec, vec],
        out_specs=pl.BlockSpec((length, HEAD_DIM), lambda h, b: (b, h)),
        out_shape=jax.ShapeDtypeStruct((m, NA_WIDTH), BF16),
        compiler_params=_cparams(("arbitrary", "arbitrary")),
        name="lat_attn_a",
    )(proj, proj, proj, ck, cv, cb, qn, kn)


def _lat_attn_b_call(proj, length, past, ck, cv, cos, sin_a, sin_b, qn, kn, sink):
    m = proj.shape[0]
    qspec = pl.BlockSpec((length, HEAD_DIM), lambda h, b: (b, QB_HEAD0 + h))
    kvspec = lambda h0: pl.BlockSpec((length, HEAD_DIM), lambda h, b: (b, h0 + h // NB_GROUP))
    cache = pl.BlockSpec((past * NB_KV_HEADS, HEAD_DIM), lambda h, b: (b, 0))
    table = pl.BlockSpec((length, HEAD_DIM), lambda h, b: (0, 0))
    vec = pl.BlockSpec((1, HEAD_DIM), lambda h, b: (0, 0))
    return pl.pallas_call(
        _lat_attn_b_kernel,
        grid=(NB_Q_HEADS, m // length),
        in_specs=[qspec, kvspec(KB_HEAD0), kvspec(VB_HEAD0), cache, cache,
                  table, table, table, vec, vec, pl.BlockSpec(memory_space=pltpu.SMEM)],
        out_specs=pl.BlockSpec((length, HEAD_DIM), lambda h, b: (b, h)),
        out_shape=jax.ShapeDtypeStruct((m, NB_WIDTH), BF16),
        compiler_params=_cparams(("arbitrary", "arbitrary")),
        name="lat_attn_b",
    )(proj, proj, proj, ck, cv, cos, sin_a, sin_b, qn, kn, sink)


def _na_bias_blocks(rpb, length):
    rows = length // GRID_W
    kr_n = min(NA_WIN_ROWS, rows)
    n_pairs = min(kr_n // 2 + 1, rows // 2)
    r = np.arange(rows)
    c = np.arange(GRID_W)
    r0 = np.clip(r - kr_n // 2, 0, rows - kr_n)
    c0 = np.clip(c - NA_WIN_COLS // 2, 0, GRID_W - NA_WIN_COLS)
    pair0 = np.minimum(r0 // 2, rows // 2 - n_pairs)
    kr = 2 * (pair0[:, None, None] + np.arange(n_pairs)[None, :, None]) + np.arange(2)[None, None, :]
    row_ok = (kr >= r0[:, None, None]) & (kr < r0[:, None, None] + kr_n)
    col_ok = (c[None, :] >= c0[:, None]) & (c[None, :] < c0[:, None] + NA_WIN_COLS)
    dr = kr - r[:, None, None] + (NA_WIN_ROWS - 1)
    dc = np.clip(c[None, :] - c[:, None], -(NA_WIN_COLS - 1), NA_WIN_COLS - 1) + (NA_WIN_COLS - 1)
    offs = np.where(row_ok, dr, -1).reshape(-1, 2)
    uniq, inverse = np.unique(offs, axis=0, return_inverse=True)
    index = inverse.reshape(rows, n_pairs)
    row_sel = (uniq[:, :, None] == np.arange(2 * NA_WIN_ROWS - 1)[None, None, :]).astype(np.float32)
    col_hit = (dc[None] == np.arange(2 * NA_WIN_COLS - 1)[:, None, None]) & col_ok[None]
    col_sel = np.zeros((2,) + col_hit.shape[:2] + (2 * GRID_W,), np.float32)
    for half in range(2):
        col_sel[half, :, :, half * GRID_W:(half + 1) * GRID_W] = col_hit
    hi = lax.Precision.HIGHEST
    per_col = jnp.einsum("hde,lecn->hldcn", rpb.astype(F32), col_sel, precision=hi)
    table = jnp.einsum("uld,hldcn->hucn", row_sel, per_col, precision=hi)
    valid = ((uniq >= 0)[:, None, :, None] & col_ok[None, :, None, :]).reshape(len(uniq), GRID_W, 2 * GRID_W)
    return (jnp.where(valid[None], table, NEG_INF), tuple(int(p) for p in pair0),
            tuple(tuple(int(u) for u in row) for row in index))


def _rope_tables(length):
    t = jnp.arange(length)
    row = (t // GRID_W).astype(F32)
    col = (t % GRID_W).astype(F32)
    n_freq = HEAD_DIM // 4
    inv = ROPE_BASE ** (-jnp.arange(n_freq, dtype=F32) / n_freq)
    ar = row[:, None] * inv
    ac = col[:, None] * inv
    ang = jnp.concatenate([ar, ar, ac, ac], axis=-1)
    cos, sin = jnp.cos(ang), jnp.sin(ang)
    lane = jnp.arange(HEAD_DIM)
    takes_left = ((lane // n_freq) % 2 == 0)[None, :]
    return cos, jnp.where(takes_left, -sin, 0.0), jnp.where(takes_left, 0.0, sin)


MERGE_TM = 512


def _resident(shape):
    zeros = (0,) * len(shape)
    return pl.BlockSpec(shape, lambda i: zeros, pipeline_mode=pl.Buffered(1))


def _mix_kernel(oa_ref, ob_ref, *refs):
    n_blk = (len(refs) - 3) // 2
    wpa_ref, wpb_ref, mix_ref = refs[-3:]
    gates = lambda blocks: jnp.concatenate([r[...] for r in blocks], axis=1).astype(F32)
    ya = _dot(oa_ref[...], wpa_ref[...].astype(BF16))
    yb = _dot(ob_ref[...], wpb_ref[...].astype(BF16))
    mix = jax.nn.sigmoid(gates(refs[:n_blk])) * ya + jax.nn.sigmoid(gates(refs[n_blk:2 * n_blk])) * yb
    mix_ref[...] = mix.astype(mix_ref.dtype)


def _mix_call(oa, ob, proj, w_pa, w_pb):
    m = oa.shape[0]
    d = w_pa.shape[1]
    row = lambda i: (i, 0)
    gate0, n_blk = GATE_COL0 // INPROJ_TN, d // INPROJ_TN
    gate_specs = [pl.BlockSpec((MERGE_TM, INPROJ_TN), lambda i, q=q: (i, gate0 + q)) for q in range(2 * n_blk)]
    return pl.pallas_call(
        _mix_kernel,
        grid=(m // MERGE_TM,),
        in_specs=[pl.BlockSpec((MERGE_TM, NA_WIDTH), row),
                  pl.BlockSpec((MERGE_TM, NB_WIDTH), row),
                  *gate_specs,
                  _resident((NA_WIDTH, d)), _resident((NB_WIDTH, d))],
        out_specs=pl.BlockSpec((MERGE_TM, d), row),
        out_shape=jax.ShapeDtypeStruct((m, d), BF16),
        compiler_params=_cparams(("arbitrary",)),
        name="mix",
    )(oa, ob, *([proj] * (2 * n_blk)), w_pa, w_pb)


def _outproj_kernel(mix_ref, wout_ref, x_ref, g1_ref, sc2_ref, sh2_ref, n2w_ref, wr_ref,
                    x1_ref, h2_ref, lg_ref, acc_ref):
    i = pl.program_id(0)
    last = pl.num_programs(0) - 1

    def finish(prod):
        x1 = x_ref[...] + g1_ref[0] * prod
        x1_ref[...] = x1
        h2 = _rms(x1, n2w_ref[...]) * (1.0 + sc2_ref[0]) + sh2_ref[0]
        h2_ref[...] = _rows_to_tiles(_pack_halves(h2))
        lg_ref[...] = _dot_split(h2, wr_ref[...])

    @pl.when(i == 0)
    def _():
        acc_ref[...] = _dot(mix_ref[...], wout_ref[...].astype(BF16))

    @pl.when((i > 0) & (i < last))
    def _():
        prod = acc_ref[...]
        acc_ref[...] = _dot(mix_ref[...], wout_ref[...].astype(BF16))
        finish(prod)

    @pl.when(i == last)
    def _():
        finish(acc_ref[...])


def _outproj_call(mix, w_out, x, g1, sc2, sh2, n2w, wr, tiles_per_group):
    m, d = x.shape
    n_tiles = m // MERGE_TM
    ahead = lambda i: (jnp.minimum(i, n_tiles - 1), 0)
    done = lambda i: (jnp.maximum(i - 1, 0), 0)
    grp = lambda i: (jnp.maximum(i - 1, 0) // tiles_per_group, 0, 0)
    return pl.pallas_call(
        _outproj_kernel,
        grid=(n_tiles + 1,),
        in_specs=[pl.BlockSpec((MERGE_TM, d), ahead),
                  _resident((d, d)),
                  pl.BlockSpec((MERGE_TM, d), done),
                  pl.BlockSpec((1, 1, d), grp), pl.BlockSpec((1, 1, d), grp), pl.BlockSpec((1, 1, d), grp),
                  pl.BlockSpec((1, d), lambda i: (0, 0)),
                  _resident((d, LOGIT_PAD))],
        out_specs=[pl.BlockSpec((MERGE_TM, d), done),
                   pl.BlockSpec((MERGE_TM,) + ROW_TILE, lambda i: (jnp.maximum(i - 1, 0), 0, 0)),
                   pl.BlockSpec((MERGE_TM, LOGIT_PAD), done)],
        out_shape=[jax.ShapeDtypeStruct((m, d), F32),
                   jax.ShapeDtypeStruct((m,) + ROW_TILE, jnp.uint32),
                   jax.ShapeDtypeStruct((m, LOGIT_PAD), F32)],
        scratch_shapes=[pltpu.VMEM((MERGE_TM, d), F32)],
        compiler_params=_cparams(("arbitrary",)),
        name="outproj",
    )(mix, w_out, x, g1, sc2, sh2, n2w, wr)


ROUTE_TM = 512


def _first_index_of_max(vals, idx, n):
    mx = jnp.max(vals, axis=0, keepdims=True)
    first = jnp.min(jnp.where(vals == mx, idx, n), axis=0, keepdims=True)
    return mx, first


def _route_kernel(lg_ctx_ref, lg_lat_ref, bias_ref, eid_ref, gw_ref, rank_ref, cnt_ref, base_ref, *, ctx_tiles):
    step = pl.program_id(0)

    @pl.when(step == 0)
    def _():
        base_ref[...] = jnp.zeros_like(base_ref)

    lg = jnp.where(step < ctx_tiles, lg_ctx_ref[...], lg_lat_ref[...])
    lt = lg.T + bias_ref[...]
    n_tok = lt.shape[1]
    le = lt[0:N_EXPERTS]
    lgrp = lt[N_EXPERTS:N_EXPERTS + N_GROUPS]
    gi = lax.broadcasted_iota(jnp.int32, (N_GROUPS, n_tok), 0)
    gmax, gsel = _first_index_of_max(lgrp, gi, N_GROUPS)
    pg_sel = 1.0 / jnp.sum(jnp.exp(lgrp - gmax), axis=0, keepdims=True)
    le_sel = jnp.zeros((EXPERTS_PER_GROUP, n_tok), F32)
    for g in range(N_GROUPS):
        le_sel = jnp.where(gsel == g, le[g * EXPERTS_PER_GROUP:(g + 1) * EXPERTS_PER_GROUP], le_sel)
    ei = lax.broadcasted_iota(jnp.int32, (EXPERTS_PER_GROUP, n_tok), 0)
    v0, i0 = _first_index_of_max(le_sel, ei, EXPERTS_PER_GROUP)
    rest = jnp.where(ei == i0, -jnp.inf, le_sel)
    v1, i1 = _first_index_of_max(rest, ei, EXPERTS_PER_GROUP)
    e1 = jnp.exp(v1 - v0)
    w0 = pg_sel / (1.0 + e1)
    w1 = pg_sel * e1 / (1.0 + e1)
    eid0 = gsel * EXPERTS_PER_GROUP + i0
    eid1 = gsel * EXPERTS_PER_GROUP + i1

    xi = lax.broadcasted_iota(jnp.int32, (N_EXPERTS, n_tok), 0)
    si = lax.broadcasted_iota(jnp.int32, (n_tok, n_tok), 0)
    ti = lax.broadcasted_iota(jnp.int32, (n_tok, n_tok), 1)
    before = (si < ti).astype(BF16)
    base = base_ref[...]
    hot0 = (xi == eid0).astype(F32)
    hot1 = (xi == eid1).astype(F32)
    pre0 = _dot(hot0.astype(BF16), before)
    pre1 = _dot(hot1.astype(BF16), before)
    tot0 = jnp.sum(hot0, axis=1, keepdims=True)
    tot1 = jnp.sum(hot1, axis=1, keepdims=True)
    rank0 = jnp.sum(hot0 * (base + pre0), axis=0, keepdims=True)
    rank1 = jnp.sum(hot1 * (base + tot0 + pre1), axis=0, keepdims=True)
    base = base + tot0 + tot1
    base_ref[...] = base

    ri = lax.broadcasted_iota(jnp.int32, (8, n_tok), 0)
    pick = lambda a, b: jnp.where(ri == 0, a, jnp.where(ri == 1, b, jnp.zeros_like(a)))
    eid_ref[...] = pick(eid0, eid1)
    gw_ref[...] = pick(w0, w1)
    rank_ref[...] = pick(rank0, rank1).astype(jnp.int32)
    cnt_ref[...] = jnp.broadcast_to(base, cnt_ref.shape).astype(jnp.int32)


def _route_call(logits_ctx, logits_lat, bias_col):
    ctx_tiles = logits_ctx.shape[0] // ROUTE_TM
    t = logits_ctx.shape[0] + logits_lat.shape[0]
    tok = pl.BlockSpec((8, ROUTE_TM), lambda i: (0, i))
    return pl.pallas_call(
        functools.partial(_route_kernel, ctx_tiles=ctx_tiles),
        grid=(t // ROUTE_TM,),
        in_specs=[pl.BlockSpec((ROUTE_TM, LOGIT_PAD), lambda i: (jnp.minimum(i, ctx_tiles - 1), 0)),
                  pl.BlockSpec((ROUTE_TM, LOGIT_PAD), lambda i: (jnp.maximum(i - ctx_tiles, 0), 0)),
                  pl.BlockSpec((LOGIT_PAD, 1), lambda i: (0, 0))],
        out_specs=[tok, tok, tok, pl.BlockSpec((N_EXPERTS, 128), lambda i: (0, 0))],
        out_shape=[jax.ShapeDtypeStruct((8, t), jnp.int32),
                   jax.ShapeDtypeStruct((8, t), F32),
                   jax.ShapeDtypeStruct((8, t), jnp.int32),
                   jax.ShapeDtypeStruct((N_EXPERTS, 128), jnp.int32)],
        scratch_shapes=[pltpu.VMEM((N_EXPERTS, 1), F32)],
        compiler_params=_cparams(("arbitrary",)),
        name="route",
    )(logits_ctx, logits_lat, bias_col)


def _moe_layout(n_pairs):
    padded_rows = -(-(n_pairs + N_EXPERTS * (MOE_ROW_BLOCK - 1)) // MOE_ROW_BLOCK) * MOE_ROW_BLOCK
    n_items = (padded_rows + N_EXPERTS * (MOE_ROW_GROUP - MOE_ROW_BLOCK)) // MOE_ROW_GROUP
    return padded_rows, n_items


def _routing_tables(eid, rank, counts, n_items):
    padded = (counts + MOE_ROW_BLOCK - 1) // MOE_ROW_BLOCK * MOE_ROW_BLOCK
    pad_end = jnp.cumsum(padded)
    pad_start = pad_end - padded
    hot = eid[..., None] == jnp.arange(N_EXPERTS, dtype=jnp.int32)
    dest = (jnp.sum(jnp.where(hot, pad_start, 0), axis=-1) + rank).astype(jnp.int32)
    tail = jnp.where(padded > counts, pad_end - MOE_ROW_BLOCK, -1).astype(jnp.int32)
    per_expert = (padded + MOE_ROW_GROUP - 1) // MOE_ROW_GROUP
    item_end = jnp.cumsum(per_expert)
    item_start = item_end - per_expert
    total = item_end[-1]
    ii = jnp.arange(n_items, dtype=jnp.int32)
    owner = lambda v: jnp.minimum(jnp.sum(item_end[None, :] <= v[:, None], axis=1), N_EXPERTS - 1).astype(jnp.int32)
    e_of = owner(ii)
    sel = e_of[:, None] == jnp.arange(N_EXPERTS, dtype=jnp.int32)
    pick = lambda v: jnp.sum(jnp.where(sel, v[None, :], 0), axis=1)
    valid = ii < total
    e_last = owner(jnp.maximum(total - 1, 0)[None])[0]
    local = ii - pick(item_start)
    row0 = pick(pad_start) + local * MOE_ROW_GROUP
    nblk = jnp.clip((pick(padded) - local * MOE_ROW_GROUP) // MOE_ROW_BLOCK, 0, MOE_ROW_GROUP // MOE_ROW_BLOCK)
    item_e = jnp.where(valid, e_of, e_last).astype(jnp.int32)
    item_row0 = jnp.where(valid, row0, 0).astype(jnp.int32)
    item_nblk = jnp.where(valid, nblk, 0).astype(jnp.int32)
    return dest, tail, item_e, item_row0, item_nblk


DISPATCH_TOKENS = 256
ROW_DMA_GROUP = 8


def _row_copy(src, s, dst, d, sem):
    return pltpu.make_async_copy(src.at[pl.ds(s, 1)], dst.at[pl.ds(d, 1)], sem)


DISPATCH_SLOTS = 3


def _dispatch_kernel(dest_ref, tail_ref, h_ctx, h_lat, xs, zero_buf, src_buf, sem, *, n_ctx, n_tok):
    step = pl.program_id(0)
    last = pl.num_programs(0) - 1
    ctx_tiles = n_ctx // DISPATCH_TOKENS
    tail_copy = lambda e: pltpu.make_async_copy(
        zero_buf, xs.at[pl.ds(pl.multiple_of(tail_ref[e], MOE_ROW_BLOCK), MOE_ROW_BLOCK)], sem.at[0])

    def fetch(tile):
        dst, fsem = src_buf.at[tile % DISPATCH_SLOTS], sem.at[1 + tile % 2]

        @pl.when(tile < ctx_tiles)
        def _():
            pltpu.make_async_copy(h_ctx.at[pl.ds(tile * DISPATCH_TOKENS, DISPATCH_TOKENS)], dst, fsem).start()

        @pl.when(tile >= ctx_tiles)
        def _():
            pltpu.make_async_copy(
                h_lat.at[pl.ds((tile - ctx_tiles) * DISPATCH_TOKENS, DISPATCH_TOKENS)], dst, fsem).start()

    def drain_scatters(tile):
        for _ in range(2):
            pltpu.make_async_copy(src_buf.at[tile % DISPATCH_SLOTS], xs.at[pl.ds(0, DISPATCH_TOKENS)],
                                  sem.at[3 + tile % 2]).wait()

    @pl.when(step == 0)
    def _():
        zero_buf[...] = jnp.zeros_like(zero_buf)
        for e in range(N_EXPERTS):
            @pl.when(tail_ref[e] >= 0)
            def _():
                tail_copy(e).start()
        for e in range(N_EXPERTS):
            @pl.when(tail_ref[e] >= 0)
            def _():
                tail_copy(e).wait()
        fetch(0)

    @pl.when(step < last)
    def _():
        fetch(step + 1)

    src = src_buf.at[step % DISPATCH_SLOTS]
    pltpu.make_async_copy(h_ctx.at[pl.ds(0, DISPATCH_TOKENS)], src, sem.at[1 + step % 2]).wait()
    tok0 = step * DISPATCH_TOKENS
    ssem = sem.at[3 + step % 2]

    def issue(g, carry):
        base = pl.multiple_of(g * ROW_DMA_GROUP, ROW_DMA_GROUP)
        for j in range(ROW_DMA_GROUP):
            _row_copy(src, base + j, xs, dest_ref[tok0 + base + j], ssem).start(priority=0)
            _row_copy(src, base + j, xs, dest_ref[n_tok + tok0 + base + j], ssem).start(priority=1)
        return carry

    lax.fori_loop(0, DISPATCH_TOKENS // ROW_DMA_GROUP, issue, 0)

    @pl.when(step > 0)
    def _():
        drain_scatters(step - 1)

    @pl.when(step == last)
    def _():
        drain_scatters(step)


def _dispatch_call(dest_flat, tail, h_ctx, h_lat, padded_rows):
    n_ctx = h_ctx.shape[0]
    row = h_ctx.shape[1:]
    n_tok = n_ctx + h_lat.shape[0]
    return pl.pallas_call(
        functools.partial(_dispatch_kernel, n_ctx=n_ctx, n_tok=n_tok),
        grid_spec=pltpu.PrefetchScalarGridSpec(
            num_scalar_prefetch=2,
            grid=(n_tok // DISPATCH_TOKENS,),
            in_specs=[pl.BlockSpec(memory_space=pl.ANY), pl.BlockSpec(memory_space=pl.ANY)],
            out_specs=pl.BlockSpec(memory_space=pl.ANY),
            scratch_shapes=[pltpu.VMEM((MOE_ROW_BLOCK,) + row, h_ctx.dtype),
                            pltpu.VMEM((DISPATCH_SLOTS, DISPATCH_TOKENS) + row, h_ctx.dtype),
                            pltpu.SemaphoreType.DMA((5,))]),
        out_shape=jax.ShapeDtypeStruct((padded_rows,) + row, h_ctx.dtype),
        compiler_params=_cparams(("arbitrary",)),
        name="dispatch",
    )(dest_flat, tail, h_ctx, h_lat)


def _moe_kernel(item_e, item_row0, item_nblk, xs, w1_ref, w3_ref, w2_ref, ys,
                x_in, x_bf, acc, y_out, w1_bf, w3_bf, w2_bf, sem, *, n_chunks):
    i = pl.program_id(0)
    c = pl.program_id(1)
    n_items = pl.num_programs(0)
    last_c = n_chunks - 1
    nblk = item_nblk[i]
    max_blk = MOE_ROW_GROUP // MOE_ROW_BLOCK
    blk = lambda b: pl.ds(b * MOE_ROW_BLOCK, MOE_ROW_BLOCK)

    def rows_of(item, b):
        return pl.ds(pl.multiple_of(item_row0[item], MOE_ROW_BLOCK) + b * MOE_ROW_BLOCK, MOE_ROW_BLOCK)

    load = lambda item, b: pltpu.make_async_copy(xs.at[rows_of(item, b)], x_in.at[blk(b)], sem.at[0])
    store = lambda item, b: pltpu.make_async_copy(y_out.at[blk(b)], ys.at[rows_of(item, b)], sem.at[1])

    def for_blocks(item, fn):
        n = item_nblk[item]
        for b in range(max_blk):
            @pl.when(b < n)
            def _():
                fn(item, b)

    @pl.when(c == 0)
    def _():
        @pl.when(i == 0)
        def _():
            for_blocks(0, lambda it, b: load(it, b).start())

        for_blocks(i, lambda it, b: load(it, b).wait())

    @pl.when((c == last_c) & (i > 0))
    def _():
        for_blocks(i - 1, lambda it, b: store(it, b).wait())

    def run_blocks(first, last):
        def block(rows, w1, w3, w2):
            if first:
                x = _unpack_halves(_tiles_to_rows(x_in[rows])).astype(BF16)
                x_bf[rows, :] = x
            else:
                x = x_bf[rows, :]
            h1 = _dot(x, w1)
            h3 = _dot(x, w3)
            a = (h1 * jax.nn.sigmoid(h1) * h3).astype(BF16)
            y = _dot(a, w2)
            if not first:
                y = acc[rows, :] + y
            if last:
                y_out[rows] = _rows_to_tiles(_pack_halves(y))
            else:
                acc[rows, :] = y

        w1 = w1_ref[0].astype(BF16)
        w3 = w3_ref[0].astype(BF16)
        w2 = w2_ref[0].astype(BF16)
        w1_bf[...] = w1
        w3_bf[...] = w3
        w2_bf[...] = w2
        block(blk(0), w1, w3, w2)

        def body(b, carry):
            rows = pl.ds(pl.multiple_of(b * MOE_ROW_BLOCK, MOE_ROW_BLOCK), MOE_ROW_BLOCK)
            block(rows, w1_bf[...], w3_bf[...], w2_bf[...])
            return carry

        lax.fori_loop(1, nblk, body, 0)

    for first, last in sorted({(cc == 0, cc == last_c) for cc in range(n_chunks)}):
        chunk_is = (c == 0) if first else ((c == last_c) if last else ((c > 0) & (c < last_c)))

        @pl.when((nblk > 0) & chunk_is)
        def _():
            run_blocks(first, last)

    @pl.when((c == 0) & (i + 1 < n_items))
    def _():
        for_blocks(i + 1, lambda it, b: load(it, b).start())

    @pl.when(c == last_c)
    def _():
        for_blocks(i, lambda it, b: store(it, b).start())

        @pl.when(i == n_items - 1)
        def _():
            for_blocks(i, lambda it, b: store(it, b).wait())


def _moe_call(item_e, item_row0, item_nblk, xs, w1, w3, w2):
    padded_rows, row = xs.shape[0], xs.shape[1:]
    d = w1.shape[1]
    n_items = item_e.shape[0]
    f = w1.shape[2]
    nc = f // MOE_F_CHUNK
    chunk = lambda i, c, ib: jnp.where(ib[i] > 0, c, nc - 1)
    return pl.pallas_call(
        functools.partial(_moe_kernel, n_chunks=nc),
        grid_spec=pltpu.PrefetchScalarGridSpec(
            num_scalar_prefetch=3,
            grid=(n_items, nc),
            in_specs=[pl.BlockSpec(memory_space=pl.ANY),
                      pl.BlockSpec((1, d, MOE_F_CHUNK), lambda i, c, ie, ir, ib: (ie[i], 0, chunk(i, c, ib))),
                      pl.BlockSpec((1, d, MOE_F_CHUNK), lambda i, c, ie, ir, ib: (ie[i], 0, chunk(i, c, ib))),
                      pl.BlockSpec((1, MOE_F_CHUNK, d), lambda i, c, ie, ir, ib: (ie[i], chunk(i, c, ib), 0))],
            out_specs=pl.BlockSpec(memory_space=pl.ANY),
            scratch_shapes=[pltpu.VMEM((MOE_ROW_GROUP,) + row, xs.dtype),
                            pltpu.VMEM((MOE_ROW_GROUP, d), BF16),
                            pltpu.VMEM((MOE_ROW_GROUP, d), F32),
                            pltpu.VMEM((MOE_ROW_GROUP,) + row, xs.dtype),
                            pltpu.VMEM((d, MOE_F_CHUNK), BF16),
                            pltpu.VMEM((d, MOE_F_CHUNK), BF16),
                            pltpu.VMEM((MOE_F_CHUNK, d), BF16),
                            pltpu.SemaphoreType.DMA((2,))]),
        out_shape=jax.ShapeDtypeStruct((padded_rows,) + row, xs.dtype),
        compiler_params=_cparams(("arbitrary", "arbitrary")),
        name="moe",
    )(item_e, item_row0, item_nblk, xs, w1, w3, w2)


COMBINE_TM = 256


def _combine_kernel(dest_ref, x1_ref, gw_ref, g2_ref, ys, o_ref, y0, y1, sem, *, tok_base, n_tok):
    i = pl.program_id(0)

    def gather(tile, slot):
        tok0 = tok_base + tile * COMBINE_TM

        def issue(g, carry):
            base = pl.multiple_of(g * ROW_DMA_GROUP, ROW_DMA_GROUP)
            for j in range(ROW_DMA_GROUP):
                _row_copy(ys, dest_ref[tok0 + base + j], y0.at[slot], base + j, sem.at[slot]).start(priority=0)
                _row_copy(ys, dest_ref[n_tok + tok0 + base + j], y1.at[slot], base + j, sem.at[slot]).start(priority=1)
            return carry

        lax.fori_loop(0, COMBINE_TM // ROW_DMA_GROUP, issue, 0)

    @pl.when(i == 0)
    def _():
        gather(0, 0)

    @pl.when(i + 1 < pl.num_programs(0))
    def _():
        gather(i + 1, (i + 1) % 2)

    slot = i % 2
    for buf in (y0, y1):
        pltpu.make_async_copy(ys.at[pl.ds(0, COMBINE_TM)], buf.at[slot], sem.at[slot]).wait()
    gw = gw_ref[...]
    moe = (gw[:, 0:1] * _unpack_halves(_tiles_to_rows(y0[slot]))
           + gw[:, 1:2] * _unpack_halves(_tiles_to_rows(y1[slot])))
    o_ref[...] = x1_ref[...] + g2_ref[0] * moe


def _combine_call(dest_flat, x1, gw, g2, ys, tok_base, n_tok, tiles_per_group):
    m, d = x1.shape
    return pl.pallas_call(
        functools.partial(_combine_kernel, tok_base=tok_base, n_tok=n_tok),
        grid_spec=pltpu.PrefetchScalarGridSpec(
            num_scalar_prefetch=1,
            grid=(m // COMBINE_TM,),
            in_specs=[pl.BlockSpec((COMBINE_TM, d), lambda i, dr: (i, 0)),
                      pl.BlockSpec((COMBINE_TM, 2), lambda i, dr: (i, 0)),
                      pl.BlockSpec((1, 1, d), lambda i, dr: (i // tiles_per_group, 0, 0)),
                      pl.BlockSpec(memory_space=pl.ANY)],
            out_specs=pl.BlockSpec((COMBINE_TM, d), lambda i, dr: (i, 0)),
            scratch_shapes=[pltpu.VMEM((2, COMBINE_TM) + ys.shape[1:], ys.dtype),
                            pltpu.VMEM((2, COMBINE_TM) + ys.shape[1:], ys.dtype),
                            pltpu.SemaphoreType.DMA((2,))]),
        out_shape=jax.ShapeDtypeStruct((m, d), F32),
        compiler_params=_cparams(("arbitrary",)),
        name="combine",
    )(dest_flat, x1, gw, g2, ys)


def kernel(x_prompt, x_sample, cache_a_k, cache_a_v, cache_b_k, cache_b_v, c, c_ctx, norm1_w, norm2_w, w_ada, b_ada, w_in, qn_a, kn_a, qn_b, kn_b, rpb_a, sink_b, w_pa, w_pb, w_out, w_rg, b_rg, w_re, b_re, w1, w3, w2):
    batch, seq, d = x_prompt.shape
    dec_batch, dec_seq, _ = x_sample.shape
    depth = norm1_w.shape[0]
    assert depth == 1, "one trunk layer"
    past = cache_a_k.shape[2]
    n_ctx, n_lat = batch * seq, dec_batch * dec_seq
    n_tok = n_ctx + n_lat

    xc = x_prompt.reshape(n_ctx, d)
    xl = x_sample.reshape(n_lat, d)

    cond = jnp.concatenate([c_ctx[None, :], c], axis=0)
    mod = _ada_call(cond, w_ada[0], b_ada[0][None, :])
    sh1, sc1, g1, sh2, sc2, g2 = [mod[:, i * d:(i + 1) * d][:, None, :] for i in range(6)]
    ctx_rows, lat_rows = slice(0, 1), slice(1, 1 + dec_batch)

    nw1, nw2 = norm1_w[0][None, :], norm2_w[0][None, :]
    qna, kna, qnb, knb = qn_a[0][None, :], kn_a[0][None, :], qn_b[0][None, :], kn_b[0][None, :]
    sink = sink_b[0]
    n_logits = N_EXPERTS + N_GROUPS
    wr = jnp.concatenate([w_re[0], w_rg[0], jnp.zeros((d, LOGIT_PAD - n_logits), F32)], axis=1)
    br = jnp.concatenate([b_re[0], b_rg[0], jnp.zeros((LOGIT_PAD - n_logits,), F32)])[:, None]

    proj_c = _inproj_call(xc, nw1, sc1[ctx_rows], sh1[ctx_rows], w_in[0], n_ctx)
    proj_l = _inproj_call(xl, nw1, sc1[lat_rows], sh1[lat_rows], w_in[0], dec_seq)

    oa_c, ob_c, new_a_k, new_a_v, new_b_k, new_b_v = _ctx_attn_call(proj_c, seq, qna, kna, qnb, knb, sink)

    bias_blocks, pair0, bias_index = _na_bias_blocks(rpb_a[0], dec_seq)
    cos, sin_a, sin_b = _rope_tables(dec_seq)
    rows_of = lambda cache: cache.reshape(-1, HEAD_DIM)
    oa_l = _lat_attn_a_call(proj_l, dec_seq, past, rows_of(cache_a_k), rows_of(cache_a_v),
                            bias_blocks, pair0, bias_index, qna, kna)
    ob_l = _lat_attn_b_call(proj_l, dec_seq, past, rows_of(cache_b_k), rows_of(cache_b_v),
                            cos, sin_a, sin_b, qnb, knb, sink)

    mix_c = _mix_call(oa_c, ob_c, proj_c, w_pa[0], w_pb[0])
    mix_l = _mix_call(oa_l, ob_l, proj_l, w_pa[0], w_pb[0])
    x1_c, h2_c, lg_c = _outproj_call(mix_c, w_out[0], xc, g1[ctx_rows], sc2[ctx_rows], sh2[ctx_rows],
                                     nw2, wr, n_ctx // MERGE_TM)
    x1_l, h2_l, lg_l = _outproj_call(mix_l, w_out[0], xl, g1[lat_rows], sc2[lat_rows], sh2[lat_rows],
                                     nw2, wr, dec_seq // MERGE_TM)

    eid, gw, rank, cnt = _route_call(lg_c, lg_l, br)
    padded_rows, n_items = _moe_layout(2 * n_tok)
    dest, tail, item_e, item_row0, item_nblk = _routing_tables(eid[:2], rank[:2], cnt[:, 0], n_items)
    dest_flat = dest.reshape(-1)
    xs = _dispatch_call(dest_flat, tail, h2_c, h2_l, padded_rows)
    ys = _moe_call(item_e, item_row0, item_nblk, xs, w1[0], w3[0], w2[0])
    gw_t = gw[:2].T
    y_c = _combine_call(dest_flat, x1_c, gw_t[:n_ctx], g2[ctx_rows], ys, 0, n_tok, n_ctx // COMBINE_TM)
    y_l = _combine_call(dest_flat, x1_l, gw_t[n_ctx:], g2[lat_rows], ys, n_ctx, n_tok, dec_seq // COMBINE_TM)

    state = lambda a, heads: a.reshape(batch, 1, seq, heads, HEAD_DIM)
    return (y_c.reshape(batch, seq, d), y_l.reshape(dec_batch, dec_seq, d),
            state(new_a_k, NA_HEADS), state(new_a_v, NA_HEADS),
            state(new_b_k, NB_KV_HEADS), state(new_b_v, NB_KV_HEADS))
```
